```python
import jax, jax.numpy as jnp
from jax import lax
import numpy as np

D_MODEL = 2048
BATCH = 8
SEQ = 4096
DEPTH = 1

D_MIX = D_MODEL
D_GMLP = D_MIX // 2
D_ATTN = D_MIX - D_GMLP
CHUNK = 128
GMLP_GROUPS = 8
GMLP_GROUP_DIM = D_GMLP // GMLP_GROUPS
HEAD_DIM = 64
N_Q_HEADS = D_ATTN // HEAD_DIM
N_KV_HEADS = 2
GQA_GROUP = N_Q_HEADS // N_KV_HEADS
D_KV = N_KV_HEADS * HEAD_DIM
WINDOW = 128
BLOCK = 128
ROPE_THETA = 10000.0
EPS = 1e-6

_SIZES = [D_GMLP, D_GMLP, D_GMLP, D_ATTN, D_KV, D_KV, D_ATTN]
_OFFS = list(np.cumsum(_SIZES)[:-1].tolist())
D_IN_PROJ = int(sum(_SIZES))
D_QKV = D_ATTN + 2 * D_KV
QKV_START = 3 * D_GMLP

kernel_name = "hybrid_gmlp_swa_sink_layer"


def rms_norm(x, g):
    xf = x.astype(jnp.float32)
    y = xf * lax.rsqrt(jnp.mean(xf * xf, axis=-1, keepdims=True) + EPS)
    return (y * g.astype(jnp.float32)).astype(x.dtype)


def layer_norm(x, g, b):
    xf = x.astype(jnp.float32)
    mu = jnp.mean(xf, axis=-1, keepdims=True)
    var = jnp.mean(jnp.square(xf - mu), axis=-1, keepdims=True)
    y = (xf - mu) * lax.rsqrt(var + EPS) * g.astype(jnp.float32) + b.astype(jnp.float32)
    return y.astype(x.dtype)


def rope(x, positions):
    half = HEAD_DIM // 2
    inv_freq = ROPE_THETA ** (-jnp.arange(half, dtype=jnp.float32) * (2.0 / HEAD_DIM))
    ang = positions.astype(jnp.float32)[..., None] * inv_freq
    cos = jnp.cos(ang)[:, :, None, :]
    sin = jnp.sin(ang)[:, :, None, :]
    xf = x.astype(jnp.float32)
    x1, x2 = xf[..., :half], xf[..., half:]
    return jnp.concatenate([x1 * cos - x2 * sin, x2 * cos + x1 * sin], axis=-1).astype(x.dtype)


def chunked_spatial_gating(u, v, ln_g, ln_b, w_s, b_s):
    B, S, _ = u.shape
    nc = S // CHUNK
    vn = layer_norm(v, ln_g, ln_b).reshape(B, nc, CHUNK, GMLP_GROUPS, GMLP_GROUP_DIM)
    causal = jnp.tril(jnp.ones((CHUNK, CHUNK), dtype=bool))
    w = w_s * causal.astype(w_s.dtype)
    mixed = jnp.einsum('gts,bnsgc->bntgc', w, vn)
    mixed = mixed + jnp.transpose(b_s)[None, None, :, :, None]
    return u * mixed.reshape(B, S, D_GMLP)


def sliding_window_attention(q, k, v, sinks):
    B, S, _, _ = q.shape
    nb = S // BLOCK
    qb = q.reshape(B, nb, BLOCK, N_KV_HEADS, GQA_GROUP, HEAD_DIM)
    kb = k.reshape(B, nb, BLOCK, N_KV_HEADS, HEAD_DIM)
    vb = v.reshape(B, nb, BLOCK, N_KV_HEADS, HEAD_DIM)
    pad = ((0, 0), (1, 0), (0, 0), (0, 0), (0, 0))
    kk = jnp.concatenate([jnp.pad(kb, pad)[:, :-1], kb], axis=2)
    vv = jnp.concatenate([jnp.pad(vb, pad)[:, :-1], vb], axis=2)
    scores = jnp.einsum('bnqhgd,bnkhd->bnhgqk', qb, kk,
                        preferred_element_type=jnp.float32) * (HEAD_DIM ** -0.5)
    qi = jnp.arange(BLOCK)[:, None] + BLOCK
    kj = jnp.arange(2 * BLOCK)[None, :]
    dist = qi - kj
    band = (dist >= 0) & (dist < WINDOW)
    blk = jnp.arange(nb)[:, None, None]
    valid = band[None] & ((blk * BLOCK + kj[None] - BLOCK) >= 0)
    scores = jnp.where(valid[None, :, None, None], scores, -jnp.inf)
    sink = sinks.astype(jnp.float32).reshape(N_KV_HEADS, GQA_GROUP)[None, None, :, :, None, None]
    m = jnp.maximum(jnp.max(scores, axis=-1, keepdims=True), sink)
    p = jnp.exp(scores - m)
    denom = jnp.sum(p, axis=-1, keepdims=True) + jnp.exp(sink - m)
    p = (p / denom).astype(v.dtype)
    out = jnp.einsum('bnhgqk,bnkhd->bnqhgd', p, vv)
    return out.reshape(B, S, N_Q_HEADS * HEAD_DIM)


def _fwd_setup_inputs(seed: int = 0) -> dict:
    key = jax.random.key(seed)
    ks = jax.random.split(key, 12)
    f32 = jnp.float32
    x = jax.random.normal(ks[0], (BATCH, SEQ, D_MODEL), f32)
    offsets = jax.random.randint(ks[1], (BATCH, 1), 0, 1024, dtype=jnp.int32)
    positions = (jnp.arange(SEQ, dtype=jnp.int32)[None, :] + offsets).astype(jnp.int32)
    g_pre = 1.0 + 0.02 * jax.random.normal(ks[2], (DEPTH, D_MODEL), f32)
    w_in = jax.random.normal(ks[3], (DEPTH, D_MODEL, D_IN_PROJ), f32) * (D_MODEL ** -0.5)
    b_qkv = 0.01 * jax.random.normal(ks[4], (DEPTH, D_QKV), f32)
    ln_v_g = 1.0 + 0.02 * jax.random.normal(ks[5], (DEPTH, D_GMLP), f32)
    ln_v_b = 0.01 * jax.random.normal(ks[6], (DEPTH, D_GMLP), f32)
    w_spatial = jax.random.normal(ks[7], (DEPTH, GMLP_GROUPS, CHUNK, CHUNK), f32) * (CHUNK ** -0.5)
    b_spatial = 1.0 + 0.02 * jax.random.normal(ks[8], (DEPTH, GMLP_GROUPS, CHUNK), f32)
    attn_sinks = jax.random.normal(ks[9], (DEPTH, N_Q_HEADS), f32)
    w_out = jax.random.normal(ks[10], (DEPTH, D_MIX, D_MODEL), f32) * (D_MIX ** -0.5)
    g_post = 1.0 + 0.02 * jax.random.normal(ks[11], (DEPTH, D_MODEL), f32)
    return {"x": x, "positions": positions, "g_pre": g_pre, "w_in": w_in, "b_qkv": b_qkv,
            "ln_v_g": ln_v_g, "ln_v_b": ln_v_b, "w_spatial": w_spatial, "b_spatial": b_spatial,
            "attn_sinks": attn_sinks, "w_out": w_out, "g_post": g_post}


def _fwd_reference(x, positions, g_pre, w_in, b_qkv, ln_v_g, ln_v_b, w_spatial, b_spatial,
              attn_sinks, w_out, g_post):
    B, S, _ = x.shape
    for l in range(DEPTH):
        h = rms_norm(x, g_pre[l])
        proj = jnp.einsum('bsd,de->bse', h, w_in[l])
        bias = jnp.concatenate([jnp.zeros((QKV_START,), proj.dtype), b_qkv[l].astype(proj.dtype),
                                jnp.zeros((D_ATTN,), proj.dtype)])
        proj = proj + bias
        u, v_g, z_a, q, k, v_a, z_b = jnp.split(proj, _OFFS, axis=-1)
        y_a = chunked_spatial_gating(u, v_g, ln_v_g[l], ln_v_b[l], w_spatial[l], b_spatial[l])
        y_a = y_a * jax.nn.silu(z_a)
        q = rope(q.reshape(B, S, N_Q_HEADS, HEAD_DIM), positions)
        k = rope(k.reshape(B, S, N_KV_HEADS, HEAD_DIM), positions)
        v_a = v_a.reshape(B, S, N_KV_HEADS, HEAD_DIM)
        y_b = sliding_window_attention(q, k, v_a, attn_sinks[l]) * jax.nn.silu(z_b)
        y = jnp.einsum('bse,ed->bsd', jnp.concatenate([y_a, y_b], axis=-1), w_out[l])
        x = x + rms_norm(y, g_post[l])
    return x


import jax as _jax
import jax.numpy as _jnp

TWIN_FORMAT = 'train_step'
FWD_PARAMS = ['x', 'positions', 'g_pre', 'w_in', 'b_qkv', 'ln_v_g', 'ln_v_b', 'w_spatial', 'b_spatial', 'attn_sinks', 'w_out', 'g_post']
TWIN_WEIGHTS = ['g_pre', 'w_in', 'b_qkv', 'ln_v_g', 'ln_v_b', 'w_spatial', 'b_spatial', 'attn_sinks', 'w_out', 'g_post']
TWIN_DIFF_INPUT = 'x'
TWIN_INPUTS = ['x', 'positions', 'g_pre', 'w_in', 'b_qkv', 'ln_v_g', 'ln_v_b', 'w_spatial', 'b_spatial', 'attn_sinks', 'w_out', 'g_post', 'loss_target', 'm_g_pre', 'm_w_in', 'm_b_qkv', 'm_ln_v_g', 'm_ln_v_b', 'm_w_spatial', 'm_b_spatial', 'm_attn_sinks', 'm_w_out', 'm_g_post', 'v_g_pre', 'v_w_in', 'v_b_qkv', 'v_ln_v_g', 'v_ln_v_b', 'v_w_spatial', 'v_b_spatial', 'v_attn_sinks', 'v_w_out', 'v_g_post']
TWIN_OUTPUTS = ['loss', 'grad_x', 'grad_g_pre', 'grad_w_in', 'grad_b_qkv', 'grad_ln_v_g', 'grad_ln_v_b', 'grad_w_spatial', 'grad_b_spatial', 'grad_attn_sinks', 'grad_w_out', 'grad_g_post', 'delta_g_pre', 'delta_w_in', 'delta_b_qkv', 'delta_ln_v_g', 'delta_ln_v_b', 'delta_w_spatial', 'delta_b_spatial', 'delta_attn_sinks', 'delta_w_out', 'delta_g_post', 'new_m_g_pre', 'new_m_w_in', 'new_m_b_qkv', 'new_m_ln_v_g', 'new_m_ln_v_b', 'new_m_w_spatial', 'new_m_b_spatial', 'new_m_attn_sinks', 'new_m_w_out', 'new_m_g_post', 'new_v_g_pre', 'new_v_w_in', 'new_v_b_qkv', 'new_v_ln_v_g', 'new_v_ln_v_b', 'new_v_w_spatial', 'new_v_b_spatial', 'new_v_attn_sinks', 'new_v_w_out', 'new_v_g_post']
TWIN_LEAF_KINDS = {'loss': 'loss', 'grad_x': 'grad_x', 'grad_g_pre': 'grad_w', 'grad_w_in': 'grad_w', 'grad_b_qkv': 'grad_w', 'grad_ln_v_g': 'grad_w', 'grad_ln_v_b': 'grad_w', 'grad_w_spatial': 'grad_w', 'grad_b_spatial': 'grad_w', 'grad_attn_sinks': 'grad_w', 'grad_w_out': 'grad_w', 'grad_g_post': 'grad_w', 'delta_g_pre': 'delta_w', 'delta_w_in': 'delta_w', 'delta_b_qkv': 'delta_w', 'delta_ln_v_g': 'delta_w', 'delta_ln_v_b': 'delta_w', 'delta_w_spatial': 'delta_w', 'delta_b_spatial': 'delta_w', 'delta_attn_sinks': 'delta_w', 'delta_w_out': 'delta_w', 'delta_g_post': 'delta_w', 'new_m_g_pre': 'new_m', 'new_m_w_in': 'new_m', 'new_m_b_qkv': 'new_m', 'new_m_ln_v_g': 'new_m', 'new_m_ln_v_b': 'new_m', 'new_m_w_spatial': 'new_m', 'new_m_b_spatial': 'new_m', 'new_m_attn_sinks': 'new_m', 'new_m_w_out': 'new_m', 'new_m_g_post': 'new_m', 'new_v_g_pre': 'new_v', 'new_v_w_in': 'new_v', 'new_v_b_qkv': 'new_v', 'new_v_ln_v_g': 'new_v', 'new_v_ln_v_b': 'new_v', 'new_v_w_spatial': 'new_v', 'new_v_b_spatial': 'new_v', 'new_v_attn_sinks': 'new_v', 'new_v_w_out': 'new_v', 'new_v_g_post': 'new_v'}


def _forward(args):
    return _fwd_reference(*[args[k] for k in FWD_PARAMS])


def _output_shape():
    def fwd():
        inp = _fwd_setup_inputs(0)
        return _fwd_reference(*[inp[k] for k in FWD_PARAMS])
    out = _jax.eval_shape(fwd)
    return out.shape, out.dtype

N_MICROBATCH = 1
ADAM_LR = 0.001
ADAM_B1 = 0.9
ADAM_B2 = 0.999
ADAM_EPS = 1e-08
ADAM_WD = 0.01
ADAM_STEP = 10
PER_EXAMPLE_BATCH_AXIS = {'x': 0, 'positions': 0, 'loss_target': 0}
SHARED_INPUTS = []
_WEIGHT_DTYPES = {'g_pre': _jnp.float32, 'w_in': _jnp.float32, 'b_qkv': _jnp.float32, 'ln_v_g': _jnp.float32, 'ln_v_b': _jnp.float32, 'w_spatial': _jnp.float32, 'b_spatial': _jnp.float32, 'attn_sinks': _jnp.float32, 'w_out': _jnp.float32, 'g_post': _jnp.float32}
MOMENT_SCALE = {'g_pre': 2.151546e-01, 'w_in': 1.269475e-01, 'b_qkv': 3.418980e-01, 'ln_v_g': 9.961172e-02, 'ln_v_b': 1.091078e-01, 'w_spatial': 9.767827e-02, 'b_spatial': 1.409812e-01, 'attn_sinks': 3.943601e-02, 'w_out': 1.430373e-01, 'g_post': 1.599497e+01}


def _to_microbatches(a, axis):
    t = _jnp.moveaxis(a, axis, 0)
    t = t.reshape((N_MICROBATCH, t.shape[0] // N_MICROBATCH) + t.shape[1:])
    return _jnp.moveaxis(t, 1, axis + 1)


def setup_inputs(seed: int = 0) -> dict:
    inp = _fwd_setup_inputs(seed)
    key = _jax.random.fold_in(_jax.random.key(seed), 7919)
    shape, _ = _output_shape()
    out = dict(inp)
    out["loss_target"] = _jax.random.normal(_jax.random.fold_in(key, 0), shape, _jnp.float32)
    for i, name in enumerate(TWIN_WEIGHTS):
        w = inp[name].astype(_jnp.float32)
        if MOMENT_SCALE is None:
            s = _jnp.sqrt(_jnp.mean(_jnp.square(w)) + 1e-30)
        else:
            s = MOMENT_SCALE[name]
        km, kv = _jax.random.split(_jax.random.fold_in(key, i + 1))
        out[name] = w
        out["m_" + name] = s * _jax.random.normal(km, w.shape, _jnp.float32)
        out["v_" + name] = (s * s) * _jax.random.uniform(kv, w.shape, _jnp.float32, 0.5, 1.5)
    if N_MICROBATCH > 1:
        for name, axis in PER_EXAMPLE_BATCH_AXIS.items():
            out[name] = _to_microbatches(out[name], axis)
    return {'x': out['x'], 'positions': out['positions'], 'g_pre': out['g_pre'], 'w_in': out['w_in'], 'b_qkv': out['b_qkv'], 'ln_v_g': out['ln_v_g'], 'ln_v_b': out['ln_v_b'], 'w_spatial': out['w_spatial'], 'b_spatial': out['b_spatial'], 'attn_sinks': out['attn_sinks'], 'w_out': out['w_out'], 'g_post': out['g_post'], 'loss_target': out['loss_target'], 'm_g_pre': out['m_g_pre'], 'm_w_in': out['m_w_in'], 'm_b_qkv': out['m_b_qkv'], 'm_ln_v_g': out['m_ln_v_g'], 'm_ln_v_b': out['m_ln_v_b'], 'm_w_spatial': out['m_w_spatial'], 'm_b_spatial': out['m_b_spatial'], 'm_attn_sinks': out['m_attn_sinks'], 'm_w_out': out['m_w_out'], 'm_g_post': out['m_g_post'], 'v_g_pre': out['v_g_pre'], 'v_w_in': out['v_w_in'], 'v_b_qkv': out['v_b_qkv'], 'v_ln_v_g': out['v_ln_v_g'], 'v_ln_v_b': out['v_ln_v_b'], 'v_w_spatial': out['v_w_spatial'], 'v_b_spatial': out['v_b_spatial'], 'v_attn_sinks': out['v_attn_sinks'], 'v_w_out': out['v_w_out'], 'v_g_post': out['v_g_post']}


def _loss(weights, diff, rest, loss_target):
    with _jax.named_scope("forward"):
        args = {**rest, TWIN_DIFF_INPUT: diff, **{k: w.astype(_WEIGHT_DTYPES[k]) for k, w in weights.items()}}
        y = _forward(args)
    with _jax.named_scope("loss_head"):
        err = _jnp.square(y.astype(_jnp.float32) - loss_target)
        return 0.5 * _jnp.sum(_jnp.mean(err, axis=-1)) if err.ndim else 0.5 * err


def _adamw(w, g, m, v):
    m = ADAM_B1 * m + (1.0 - ADAM_B1) * g
    v = ADAM_B2 * v + (1.0 - ADAM_B2) * _jnp.square(g)
    m_hat = m / (1.0 - ADAM_B1 ** ADAM_STEP)
    v_hat = v / (1.0 - ADAM_B2 ** ADAM_STEP)
    delta = -ADAM_LR * (m_hat / (_jnp.sqrt(v_hat) + ADAM_EPS) + ADAM_WD * w)
    return delta, m, v


def reference(x, positions, g_pre, w_in, b_qkv, ln_v_g, ln_v_b, w_spatial, b_spatial, attn_sinks, w_out, g_post, loss_target, m_g_pre, m_w_in, m_b_qkv, m_ln_v_g, m_ln_v_b, m_w_spatial, m_b_spatial, m_attn_sinks, m_w_out, m_g_post, v_g_pre, v_w_in, v_b_qkv, v_ln_v_g, v_ln_v_b, v_w_spatial, v_b_spatial, v_attn_sinks, v_w_out, v_g_post):
    given = dict(x=x, positions=positions, g_pre=g_pre, w_in=w_in, b_qkv=b_qkv, ln_v_g=ln_v_g, ln_v_b=ln_v_b, w_spatial=w_spatial, b_spatial=b_spatial, attn_sinks=attn_sinks, w_out=w_out, g_post=g_post, loss_target=loss_target, m_g_pre=m_g_pre, m_w_in=m_w_in, m_b_qkv=m_b_qkv, m_ln_v_g=m_ln_v_g, m_ln_v_b=m_ln_v_b, m_w_spatial=m_w_spatial, m_b_spatial=m_b_spatial, m_attn_sinks=m_attn_sinks, m_w_out=m_w_out, m_g_post=m_g_post, v_g_pre=v_g_pre, v_w_in=v_w_in, v_b_qkv=v_b_qkv, v_ln_v_g=v_ln_v_g, v_ln_v_b=v_ln_v_b, v_w_spatial=v_w_spatial, v_b_spatial=v_b_spatial, v_attn_sinks=v_attn_sinks, v_w_out=v_w_out, v_g_post=v_g_post)
    weights = {n: given[n] for n in TWIN_WEIGHTS}
    shared = {n: given[n] for n in SHARED_INPUTS}
    per_example = {n: given[n] for n in ['x', 'positions']}
    grad_fn = _jax.value_and_grad(_loss, argnums=(0, 1))

    def one_microbatch(ex, loss_target):
        ex = dict(ex)
        diff = ex.pop(TWIN_DIFF_INPUT)
        return grad_fn(weights, diff, {**shared, **ex}, loss_target)

    if N_MICROBATCH == 1:
        loss, (grad_w, grad_x) = one_microbatch(per_example, given["loss_target"])
    else:
        def body(carry, xs):
            loss_sum, grad_sum = carry
            l_k, (gw_k, gx_k) = one_microbatch(xs[0], xs[1])
            with _jax.named_scope("update"):
                return (loss_sum + l_k, _jax.tree.map(_jnp.add, grad_sum, gw_k)), gx_k

        init = (_jnp.zeros((), _jnp.float32), _jax.tree.map(_jnp.zeros_like, weights))
        (loss, grad_w), grad_x = _jax.lax.scan(body, init, (per_example, given["loss_target"]))
    with _jax.named_scope("update"):
        delta_w, new_m, new_v = {}, {}, {}
        for n in TWIN_WEIGHTS:
            delta_w[n], new_m[n], new_v[n] = _adamw(weights[n], grad_w[n], given["m_" + n], given["v_" + n])
    return (loss, grad_x, *[grad_w[n] for n in TWIN_WEIGHTS], *[delta_w[n] for n in TWIN_WEIGHTS],
            *[new_m[n] for n in TWIN_WEIGHTS], *[new_v[n] for n in TWIN_WEIGHTS])
```

```python
import functools

import jax
import jax.numpy as jnp
from jax import lax
from jax.experimental import pallas as pl
from jax.experimental.pallas import tpu as pltpu

F32 = jnp.float32
BF16 = jnp.bfloat16
MESH = pl.DeviceIdType.MESH

D = 2048
DG = 1024
T = 128
NG = 8
HD = 64
NQ = 16
D_IN = 5376
OFF_U, OFF_V, OFF_ZA, OFF_Q, OFF_K, OFF_VA, OFF_ZB = 0, 1024, 2048, 3072, 4096, 4224, 4352
D_QKV = 1280
EPS = 1e-6
ROPE_THETA = 10000.0
N_CHIPS = 4
N_DEV = 8
W_IN_ROWS = D_IN // N_DEV
W_OUT_ROWS = D // N_DEV

ADAM_LR, ADAM_B1, ADAM_B2, ADAM_EPS, ADAM_WD, ADAM_STEP = 0.001, 0.9, 0.999, 1e-08, 0.01, 10

VMEM_LIMIT = 56 * 1024 * 1024

SMALL = (("g_pre", 2048, 16), ("b_qkv", 1280, 10), ("ln_v_g", 1024, 8), ("ln_v_b", 1024, 8),
         ("w_spatial", 131072, 1024), ("b_spatial", 1024, 8), ("attn_sinks", 16, 1), ("g_post", 2048, 16),
         ("loss", 1, 1))
SMALL_ROWS = 1096


def _tile(n, pref):
    return pref if n % pref == 0 else n


def _params(sem=None, vmem=VMEM_LIMIT):
    return pltpu.CompilerParams(dimension_semantics=sem, vmem_limit_bytes=vmem)


def _sigmoid(z):
    return 1.0 / (1.0 + jnp.exp(-z))


def _dot(a, b, dims):
    return lax.dot_general(a, b, (dims, ((), ())), preferred_element_type=F32)


NN = ((1,), (0,))
NT = ((1,), (1,))
TN = ((0,), (0,))


def _prenorm(x, g):
    S = x.shape[0]
    tm = _tile(S, 512)

    def body(x_ref, g_ref, h_ref):
        xv = x_ref[...]
        r = lax.rsqrt(jnp.mean(xv * xv, axis=-1, keepdims=True) + EPS)
        h_ref[...] = (xv * r * g_ref[...]).astype(BF16)

    return pl.pallas_call(
        body, name="prenorm", grid=(S // tm,),
        in_specs=[pl.BlockSpec((tm, D), lambda i: (i, 0)), pl.BlockSpec((1, D), lambda i: (0, 0))],
        out_specs=pl.BlockSpec((tm, D), lambda i: (i, 0)),
        out_shape=jax.ShapeDtypeStruct((S, D), BF16),
        compiler_params=_params(("parallel",)),
    )(x, g)


def _inproj(h, wt, bias):
    S = h.shape[0]
    tm, tn = _tile(S, 1024), 768

    def body(h_ref, w_ref, b_ref, o_ref):
        o_ref[...] = _dot(h_ref[...], w_ref[...], NT) + b_ref[...]

    return pl.pallas_call(
        body, name="inproj", grid=(S // tm, D_IN // tn),
        in_specs=[pl.BlockSpec((tm, D), lambda i, j: (i, 0)), pl.BlockSpec((tn, D), lambda i, j: (j, 0)),
                  pl.BlockSpec((1, tn), lambda i, j: (0, j))],
        out_specs=pl.BlockSpec((tm, tn), lambda i, j: (i, j)),
        out_shape=jax.ShapeDtypeStruct((S, D_IN), F32),
        compiler_params=_params(("parallel", "arbitrary")),
    )(h, wt, bias)


def _rope_tables(pos_ref, invf_ref):
    ang = pos_ref[...].astype(F32) * invf_ref[...]
    return jnp.cos(ang), jnp.sin(ang)


def _rot_half(xs, first_half):
    return jnp.where(first_half, -pltpu.roll(xs, 96, 1), pltpu.roll(xs, 32, 1))


def _attn_consts():
    lane = lax.broadcasted_iota(jnp.int32, (T, 128), 1)
    first_half = (lane & (HD - 1)) < (HD // 2)
    lo64 = lane < HD
    qi = lax.broadcasted_iota(jnp.int32, (T, 2 * T), 0) + T
    kj = lax.broadcasted_iota(jnp.int32, (T, 2 * T), 1)
    dist = qi - kj
    band = (dist >= 0) & (dist < T)
    return first_half, lo64, band, kj


def _layer_norm_stats(v):
    mu = jnp.mean(v, axis=-1, keepdims=True)
    xc = v - mu
    var = jnp.mean(xc * xc, axis=-1, keepdims=True)
    rs = lax.rsqrt(var + EPS)
    return xc * rs, rs


def _softmax_sink(qh, kh, valid, sink):
    s = _dot(qh, kh, NT) * (HD ** -0.5)
    s = jnp.where(valid, s, -1e30)
    m = jnp.maximum(jnp.max(s, axis=-1, keepdims=True), sink)
    e = jnp.exp(s - m)
    es = jnp.exp(sink - m)
    den = jnp.sum(e, axis=-1, keepdims=True) + es
    return e / den, es / den


def _mid_specs(nb, rev):
    blk = (lambda i: nb - 1 - i) if rev else (lambda i: i)
    prev = lambda i: jnp.maximum(blk(i) - 1, 0)
    return blk, prev, [
        pl.BlockSpec((T, D_IN), lambda i: (blk(i), 0)),
        pl.BlockSpec((T, 2 * T), lambda i: (prev(i), OFF_K // (2 * T))),
    ], [
        pl.BlockSpec((T, 1), lambda i: (blk(i), 0)),
        pl.BlockSpec((T, 1), lambda i: (prev(i), 0)),
        pl.BlockSpec((1, 128), lambda i: (0, 0)),
        pl.BlockSpec((1, DG), lambda i: (0, 0)),
        pl.BlockSpec((1, DG), lambda i: (0, 0)),
        pl.BlockSpec((NG, T, T), lambda i: (0, 0, 0)),
        pl.BlockSpec((NG, T, 1), lambda i: (0, 0, 0)),
        pl.BlockSpec(memory_space=pltpu.SMEM),
    ]


def _mid_fwd(proj, pos, invf, ln_g, ln_b, w_s, b_s, sinks):
    S = proj.shape[0]
    nb = S // T
    blk, _, proj_specs, par_specs = _mid_specs(nb, False)

    def body(p_ref, kvp_ref, pos_ref, posp_ref, invf_ref, lng_ref, lnb_ref, ws_ref, bs_ref, sink_ref, y_ref):
        n = pl.program_id(0)
        first_half, lo64, band, kj = _attn_consts()
        tril = lax.broadcasted_iota(jnp.int32, (T, T), 0) >= lax.broadcasted_iota(jnp.int32, (T, T), 1)

        xhat, _ = _layer_norm_stats(p_ref[:, OFF_V:OFF_V + DG])
        vn = xhat * lng_ref[...] + lnb_ref[...]
        for g in range(NG):
            sl = slice(128 * g, 128 * g + 128)
            wm = jnp.where(tril, ws_ref[g], 0.0).astype(BF16)
            mixed = _dot(wm, vn[:, sl].astype(BF16), NN) + bs_ref[g]
            z = p_ref[:, OFF_ZA + 128 * g:OFF_ZA + 128 * g + 128]
            u = p_ref[:, OFF_U + 128 * g:OFF_U + 128 * g + 128]
            y_ref[:, sl] = (u * mixed * (z * _sigmoid(z))).astype(BF16)

        cosv, sinv = _rope_tables(pos_ref, invf_ref)
        cosp, sinp = _rope_tables(posp_ref, invf_ref)
        k_cur = p_ref[:, OFF_K:OFF_K + 128]
        k_prev = kvp_ref[:, 0:128]
        kc = jnp.concatenate([k_prev * cosp + _rot_half(k_prev, first_half) * sinp,
                              k_cur * cosv + _rot_half(k_cur, first_half) * sinv], axis=0)
        vc = jnp.concatenate([kvp_ref[:, 128:256], p_ref[:, OFF_VA:OFF_VA + 128]], axis=0)
        kc_d, kc_s = kc.astype(BF16), pltpu.roll(kc, HD, 1).astype(BF16)
        vc_d, vc_s = vc.astype(BF16), pltpu.roll(vc, HD, 1).astype(BF16)
        valid = band & ((kj >= T) | (n > 0))
        for j in range(NQ // 2):
            c0 = OFF_Q + 128 * j
            qraw = p_ref[:, c0:c0 + 128]
            qs = qraw * cosv + _rot_half(qraw, first_half) * sinv
            direct_first = j < NQ // 4
            outs = []
            for half in range(2):
                in_head = lo64 if half == 0 else jnp.logical_not(lo64)
                qh = jnp.where(in_head, qs, 0.0).astype(BF16)
                use_direct = direct_first == (half == 0)
                kh, vh = (kc_d, vc_d) if use_direct else (kc_s, vc_s)
                p, _ = _softmax_sink(qh, kh, valid, sink_ref[2 * j + half])
                outs.append(_dot(p.astype(BF16), vh, NN))
            o = jnp.where(lo64, outs[0], outs[1])
            zb = p_ref[:, OFF_ZB + 128 * j:OFF_ZB + 128 * j + 128]
            y_ref[:, DG + 128 * j:DG + 128 * j + 128] = (o * (zb * _sigmoid(zb))).astype(BF16)

    return pl.pallas_call(
        body, name="mid_fwd", grid=(nb,),
        in_specs=proj_specs + par_specs,
        out_specs=pl.BlockSpec((T, D), lambda i: (blk(i), 0)),
        out_shape=jax.ShapeDtypeStruct((S, D), BF16),
        compiler_params=_params(("arbitrary",)),
    )(proj, proj, pos, pos, invf, ln_g, ln_b, w_s, b_s, sinks)


def _mid_bwd(proj, dycat, pos, invf, ln_g, ln_b, w_s, b_s, sinks):
    S = proj.shape[0]
    nb = S // T
    blk, _, proj_specs, par_specs = _mid_specs(nb, True)
    const2 = lambda i: (0, 0)

    def body(p_ref, kvp_ref, dyc_ref, pos_ref, posp_ref, invf_ref, lng_ref, lnb_ref, ws_ref, bs_ref, sink_ref,
             dp_ref, dlng_ref, dlnb_ref, dws_ref, dbs_ref, dsink_ref, dbqkv_ref, carry_ref, dvn_ref):
        i = pl.program_id(0)
        n = nb - 1 - i
        first_half, lo64, band, kj = _attn_consts()
        tril = lax.broadcasted_iota(jnp.int32, (T, T), 0) >= lax.broadcasted_iota(jnp.int32, (T, T), 1)

        @pl.when(i == 0)
        def _():
            dlng_ref[...] = jnp.zeros_like(dlng_ref)
            dlnb_ref[...] = jnp.zeros_like(dlnb_ref)
            dws_ref[...] = jnp.zeros_like(dws_ref)
            dbs_ref[...] = jnp.zeros_like(dbs_ref)
            dsink_ref[...] = jnp.zeros_like(dsink_ref)
            dbqkv_ref[...] = jnp.zeros_like(dbqkv_ref)
            carry_ref[...] = jnp.zeros_like(carry_ref)

        xhat, rs = _layer_norm_stats(p_ref[:, OFF_V:OFF_V + DG])
        lng = lng_ref[...]
        vn = xhat * lng + lnb_ref[...]
        for g in range(NG):
            sl = slice(128 * g, 128 * g + 128)
            wm = jnp.where(tril, ws_ref[g], 0.0).astype(BF16)
            vng = vn[:, sl].astype(BF16)
            mixed = _dot(wm, vng, NN) + bs_ref[g]
            z = p_ref[:, OFF_ZA + 128 * g:OFF_ZA + 128 * g + 128]
            u = p_ref[:, OFF_U + 128 * g:OFF_U + 128 * g + 128]
            dy = dyc_ref[:, sl]
            sg = _sigmoid(z)
            sa = z * sg
            dp_ref[:, OFF_U + 128 * g:OFF_U + 128 * g + 128] = (dy * mixed * sa).astype(BF16)
            dp_ref[:, OFF_ZA + 128 * g:OFF_ZA + 128 * g + 128] = (
                dy * u * mixed * (sg * (1.0 + z * (1.0 - sg)))).astype(BF16)
            dm = dy * u * sa
            dmb = dm.astype(BF16)
            dvn_ref[:, sl] = _dot(wm, dmb, TN)
            dws_ref[g] += jnp.where(tril, _dot(dmb, vng, NT), 0.0)
            dbs_ref[g] += jnp.sum(dm, axis=1, keepdims=True)
        dvn = dvn_ref[...]
        dlng_ref[...] += jnp.sum(dvn * xhat, axis=0, keepdims=True)
        dlnb_ref[...] += jnp.sum(dvn, axis=0, keepdims=True)
        dxh = dvn * lng
        dv_g = rs * (dxh - jnp.mean(dxh, axis=-1, keepdims=True)
                     - xhat * jnp.mean(dxh * xhat, axis=-1, keepdims=True))
        dp_ref[:, OFF_V:OFF_V + DG] = dv_g.astype(BF16)

        cosv, sinv = _rope_tables(pos_ref, invf_ref)
        cosp, sinp = _rope_tables(posp_ref, invf_ref)
        k_cur = p_ref[:, OFF_K:OFF_K + 128]
        k_prev = kvp_ref[:, 0:128]
        kc = jnp.concatenate([k_prev * cosp + _rot_half(k_prev, first_half) * sinp,
                              k_cur * cosv + _rot_half(k_cur, first_half) * sinv], axis=0)
        vc = jnp.concatenate([kvp_ref[:, 128:256], p_ref[:, OFF_VA:OFF_VA + 128]], axis=0)
        kc_d, kc_s = kc.astype(BF16), pltpu.roll(kc, HD, 1).astype(BF16)
        vc_d, vc_s = vc.astype(BF16), pltpu.roll(vc, HD, 1).astype(BF16)
        valid = band & ((kj >= T) | (n > 0))
        acc_k = [jnp.zeros((2 * T, 128), F32), jnp.zeros((2 * T, 128), F32)]
        acc_v = [jnp.zeros((2 * T, 128), F32), jnp.zeros((2 * T, 128), F32)]
        for j in range(NQ // 2):
            c0 = OFF_Q + 128 * j
            qraw = p_ref[:, c0:c0 + 128]
            qs = qraw * cosv + _rot_half(qraw, first_half) * sinv
            grp = 0 if j < NQ // 4 else 1
            zb = p_ref[:, OFF_ZB + 128 * j:OFF_ZB + 128 * j + 128]
            dyb = dyc_ref[:, DG + 128 * j:DG + 128 * j + 128]
            sg = _sigmoid(zb)
            d_o = dyb * (zb * sg)
            outs, dqs = [], []
            for half in range(2):
                in_head = lo64 if half == 0 else jnp.logical_not(lo64)
                qh = jnp.where(in_head, qs, 0.0).astype(BF16)
                doh = jnp.where(in_head, d_o, 0.0)
                dohb = doh.astype(BF16)
                use_direct = (grp == 0) == (half == 0)
                kh, vh = (kc_d, vc_d) if use_direct else (kc_s, vc_s)
                h = 2 * j + half
                p, ps = _softmax_sink(qh, kh, valid, sink_ref[h])
                pb = p.astype(BF16)
                o = _dot(pb, vh, NN)
                delta = jnp.sum(doh * o, axis=-1, keepdims=True)
                dp = _dot(dohb, vh, NT)
                dsb = (p * (dp - delta) * (HD ** -0.5)).astype(BF16)
                dqs.append(_dot(dsb, kh, NN))
                outs.append(o)
                acc_k[grp] = acc_k[grp] + _dot(dsb, qh, TN)
                acc_v[grp] = acc_v[grp] + _dot(pb, dohb, TN)
                dsink_ref[h:h + 1, :] += jnp.broadcast_to(-jnp.sum(ps * delta, axis=0, keepdims=True), (1, 128))
            o = jnp.where(lo64, outs[0], outs[1])
            dq = jnp.where(lo64, dqs[0], dqs[1])
            dp_ref[:, OFF_ZB + 128 * j:OFF_ZB + 128 * j + 128] = (
                dyb * o * (sg * (1.0 + zb * (1.0 - sg)))).astype(BF16)
            dq_pre = dq * cosv - _rot_half(dq, first_half) * sinv
            dp_ref[:, c0:c0 + 128] = dq_pre.astype(BF16)
            dbqkv_ref[:, 128 * j:128 * j + 128] += jnp.sum(dq_pre, axis=0, keepdims=True)
        lo64_kv = lax.broadcasted_iota(jnp.int32, (2 * T, 128), 1) < HD
        dkc = jnp.where(lo64_kv, acc_k[0] + pltpu.roll(acc_k[0], HD, 1), acc_k[1] + pltpu.roll(acc_k[1], HD, 1))
        dvc = jnp.where(lo64_kv, acc_v[0] + pltpu.roll(acc_v[0], HD, 1), acc_v[1] + pltpu.roll(acc_v[1], HD, 1))
        dk_cur = dkc[T:2 * T] + carry_ref[:, 0:128]
        dv_cur = dvc[T:2 * T] + carry_ref[:, 128:256]
        carry_ref[:, 0:128] = dkc[0:T]
        carry_ref[:, 128:256] = dvc[0:T]
        dk_pre = dk_cur * cosv - _rot_half(dk_cur, first_half) * sinv
        dp_ref[:, OFF_K:OFF_K + 128] = dk_pre.astype(BF16)
        dp_ref[:, OFF_VA:OFF_VA + 128] = dv_cur.astype(BF16)
        dbqkv_ref[:, 1024:1152] += jnp.sum(dk_pre, axis=0, keepdims=True)
        dbqkv_ref[:, 1152:1280] += jnp.sum(dv_cur, axis=0, keepdims=True)

    return pl.pallas_call(
        body, name="mid_bwd", grid=(nb,),
        in_specs=proj_specs + [pl.BlockSpec((T, D), lambda i: (blk(i), 0))] + par_specs,
        out_specs=[pl.BlockSpec((T, D_IN), lambda i: (blk(i), 0)),
                   pl.BlockSpec((1, DG), const2), pl.BlockSpec((1, DG), const2),
                   pl.BlockSpec((NG, T, T), lambda i: (0, 0, 0)), pl.BlockSpec((NG, T, 1), lambda i: (0, 0, 0)),
                   pl.BlockSpec((NQ, 128), const2), pl.BlockSpec((1, D_QKV), const2)],
        out_shape=[jax.ShapeDtypeStruct((S, D_IN), BF16),
                   jax.ShapeDtypeStruct((1, DG), F32), jax.ShapeDtypeStruct((1, DG), F32),
                   jax.ShapeDtypeStruct((NG, T, T), F32), jax.ShapeDtypeStruct((NG, T, 1), F32),
                   jax.ShapeDtypeStruct((NQ, 128), F32), jax.ShapeDtypeStruct((1, D_QKV), F32)],
        scratch_shapes=[pltpu.VMEM((T, 2 * T), F32), pltpu.VMEM((T, DG), F32)],
        compiler_params=_params(("arbitrary",)),
    )(proj, proj, dycat, pos, pos, invf, ln_g, ln_b, w_s, b_s, sinks)


def _outproj_loss(ycat, wo, x, target, g_post):
    S = ycat.shape[0]
    tm = _tile(S, 256)
    nt = S // tm
    const2 = lambda i: (0, 0)

    def body(yc_ref, w_ref, x_ref, t_ref, g_ref, dy_ref, dout_ref, loss_ref, dg_ref, lacc_ref):
        i = pl.program_id(0)

        @pl.when(i == 0)
        def _():
            dg_ref[...] = jnp.zeros_like(dg_ref)
            lacc_ref[...] = jnp.zeros_like(lacc_ref)

        y = _dot(yc_ref[...], w_ref[...], NN)
        r = lax.rsqrt(jnp.mean(y * y, axis=-1, keepdims=True) + EPS)
        yh = y * r
        g = g_ref[...]
        diff = x_ref[...] + yh * g - t_ref[...]
        lacc_ref[...] += jnp.sum(diff * diff, axis=0, keepdims=True)
        dout = diff * (1.0 / D)
        dout_ref[...] = dout
        dg_ref[...] += jnp.sum(dout * yh, axis=0, keepdims=True)
        dyh = dout * g
        dy_ref[...] = (r * (dyh - yh * jnp.mean(dyh * yh, axis=-1, keepdims=True))).astype(BF16)

        @pl.when(i == nt - 1)
        def _():
            loss_ref[...] = jnp.broadcast_to(jnp.sum(lacc_ref[...], axis=1, keepdims=True) * (0.5 / D), (1, 128))

    row = lambda i: (i, 0)
    return pl.pallas_call(
        body, name="outproj_loss", grid=(nt,),
        in_specs=[pl.BlockSpec((tm, D), row), pl.BlockSpec((D, D), const2), pl.BlockSpec((tm, D), row),
                  pl.BlockSpec((tm, D), row), pl.BlockSpec((1, D), const2)],
        out_specs=[pl.BlockSpec((tm, D), row), pl.BlockSpec((tm, D), row), pl.BlockSpec((1, 128), const2),
                   pl.BlockSpec((1, D), const2)],
        out_shape=[jax.ShapeDtypeStruct((S, D), BF16), jax.ShapeDtypeStruct((S, D), F32),
                   jax.ShapeDtypeStruct((1, 128), F32), jax.ShapeDtypeStruct((1, D), F32)],
        scratch_shapes=[pltpu.VMEM((1, D), F32)],
        compiler_params=_params(("arbitrary",)),
    )(ycat, wo, x, target, g_post)


def _dycat(dy, wo):
    S = dy.shape[0]
    tm = _tile(S, 512)

    def body(dy_ref, w_ref, o_ref):
        o_ref[...] = _dot(dy_ref[...], w_ref[...], NT)

    return pl.pallas_call(
        body, name="dycat", grid=(S // tm,),
        in_specs=[pl.BlockSpec((tm, D), lambda i: (i, 0)), pl.BlockSpec((D, D), lambda i: (0, 0))],
        out_specs=pl.BlockSpec((tm, D), lambda i: (i, 0)),
        out_shape=jax.ShapeDtypeStruct((S, D), F32),
        compiler_params=_params(("parallel",)),
    )(dy, wo)


def _matmul_tn(a, b, tm, name):
    K, M = a.shape
    N = b.shape[1]
    tk = _tile(K, 1024)
    nk = K // tk

    def body(a_ref, b_ref, o_ref, acc_ref):
        k = pl.program_id(1)

        @pl.when(k == 0)
        def _():
            acc_ref[...] = jnp.zeros_like(acc_ref)

        acc_ref[...] += _dot(a_ref[...], b_ref[...], TN)

        @pl.when(k == nk - 1)
        def _():
            o_ref[...] = acc_ref[...].astype(BF16)

    return pl.pallas_call(
        body, name=name, grid=(M // tm, nk),
        in_specs=[pl.BlockSpec((tk, tm), lambda i, k: (k, i)), pl.BlockSpec((tk, N), lambda i, k: (k, 0))],
        out_specs=pl.BlockSpec((tm, N), lambda i, k: (i, 0)),
        out_shape=jax.ShapeDtypeStruct((M, N), BF16),
        scratch_shapes=[pltpu.VMEM((tm, N), F32)],
        compiler_params=_params(("parallel", "arbitrary")),
    )(a, b)


def _dh_prenorm_bwd(dproj, wt, x, dout, g_pre):
    S = x.shape[0]
    tm, tk = _tile(S, 512), 768
    nk = D_IN // tk

    def body(dp_ref, w_ref, x_ref, dout_ref, g_ref, gx_ref, dg_ref, acc_ref):
        i, k = pl.program_id(0), pl.program_id(1)

        @pl.when((i == 0) & (k == 0))
        def _():
            dg_ref[...] = jnp.zeros_like(dg_ref)

        @pl.when(k == 0)
        def _():
            acc_ref[...] = jnp.zeros_like(acc_ref)

        acc_ref[...] += _dot(dp_ref[...], w_ref[...], NN)

        @pl.when(k == nk - 1)
        def _():
            dh = acc_ref[...]
            xv = x_ref[...]
            r = lax.rsqrt(jnp.mean(xv * xv, axis=-1, keepdims=True) + EPS)
            xh = xv * r
            dg_ref[...] += jnp.sum(dh * xh, axis=0, keepdims=True)
            dxh = dh * g_ref[...]
            gx_ref[...] = dout_ref[...] + r * (dxh - xh * jnp.mean(dxh * xh, axis=-1, keepdims=True))

    row = lambda i, k: (i, 0)
    return pl.pallas_call(
        body, name="dh_prenorm_bwd", grid=(S // tm, nk),
        in_specs=[pl.BlockSpec((tm, tk), lambda i, k: (i, k)), pl.BlockSpec((tk, D), lambda i, k: (k, 0)),
                  pl.BlockSpec((tm, D), row), pl.BlockSpec((tm, D), row), pl.BlockSpec((1, D), lambda i, k: (0, 0))],
        out_specs=[pl.BlockSpec((tm, D), row), pl.BlockSpec((1, D), lambda i, k: (0, 0))],
        out_shape=[jax.ShapeDtypeStruct((S, D), F32), jax.ShapeDtypeStruct((1, D), F32)],
        scratch_shapes=[pltpu.VMEM((tm, D), F32)],
        compiler_params=_params(("arbitrary", "arbitrary")),
    )(dproj, wt, x, dout, g_pre)


def _presum(c_arr, own, recv, half_rows):
    n_cols = own.shape[-1]
    own4 = own.reshape(N_CHIPS, 2, half_rows, n_cols)

    def body(c_ref, own_ref, recv_ref, o_ref):
        o_ref[...] = (own_ref[...].astype(F32) + recv_ref[...].astype(F32)).astype(BF16)

    return pl.pallas_call(
        body, name="presum_%d" % half_rows,
        grid_spec=pltpu.PrefetchScalarGridSpec(
            num_scalar_prefetch=1, grid=(N_CHIPS,),
            in_specs=[pl.BlockSpec((None, None, half_rows, n_cols), lambda j, c: (j, c[0], 0, 0)),
                      pl.BlockSpec((None, half_rows, n_cols), lambda j, c: (j, 0, 0))],
            out_specs=pl.BlockSpec((None, half_rows, n_cols), lambda j, c: (j, 0, 0))),
        out_shape=jax.ShapeDtypeStruct((N_CHIPS, half_rows, n_cols), BF16),
        compiler_params=_params(("parallel",)),
    )(c_arr, own4, recv)


def _sum_chips(parts, name):
    _, rows, n_cols = parts.shape
    tr = rows // 2 if rows % 16 == 0 else rows

    def body(p_ref, o_ref):
        o_ref[...] = ((p_ref[0].astype(F32) + p_ref[1].astype(F32)) + p_ref[2].astype(F32)) + p_ref[3].astype(F32)

    return pl.pallas_call(
        body, name=name, grid=(rows // tr,),
        in_specs=[pl.BlockSpec((N_CHIPS, tr, n_cols), lambda i: (0, i, 0))],
        out_specs=pl.BlockSpec((tr, n_cols), lambda i: (i, 0)),
        out_shape=jax.ShapeDtypeStruct((rows, n_cols), F32),
        compiler_params=_params(("parallel",)),
    )(parts)


def _adamw(w, g, m, v, name):
    R, C = w.shape
    tr = _tile(R, 256)
    c1 = 1.0 - ADAM_B1 ** ADAM_STEP
    c2 = 1.0 - ADAM_B2 ** ADAM_STEP

    def body(w_ref, g_ref, m_ref, v_ref, d_ref, mo_ref, vo_ref):
        gv = g_ref[...]
        mn = ADAM_B1 * m_ref[...] + (1.0 - ADAM_B1) * gv
        vn = ADAM_B2 * v_ref[...] + (1.0 - ADAM_B2) * (gv * gv)
        mo_ref[...] = mn
        vo_ref[...] = vn
        d_ref[...] = -ADAM_LR * ((mn / c1) / (jnp.sqrt(vn / c2) + ADAM_EPS) + ADAM_WD * w_ref[...])

    spec = pl.BlockSpec((tr, C), lambda i: (i, 0))
    shp = jax.ShapeDtypeStruct((R, C), F32)
    return pl.pallas_call(
        body, name=name, grid=(R // tr,), in_specs=[spec] * 4, out_specs=[spec] * 3, out_shape=[shp] * 3,
        compiler_params=_params(("parallel",)),
    )(w, g, m, v)


HBM_SPEC = pl.BlockSpec(memory_space=pltpu.HBM)


def _mesh_pos():
    return lax.axis_index("x"), lax.axis_index("y"), lax.axis_index("c")


def _gather_weights(wt_part, wo_part):
    arrays = ((wt_part, W_IN_ROWS, D_IN), (wo_part, W_OUT_ROWS, D))
    n_arr = len(arrays)

    def body(wt_ref, wo_ref, wt_full, wo_full, send_sems, recv_sems, local_sems):
        x, y, c = _mesh_pos()
        me, sibling = (x, y, c), (x, y, 1 - c)
        chips = [(1 - x, y), (x, 1 - y), (1 - x, 1 - y)]
        started = []
        for a, (src_ref, full_ref, rows) in enumerate(((wt_ref, wt_full, W_IN_ROWS), (wo_ref, wo_full, W_OUT_ROWS))):
            def block(pos, full_ref=full_ref, rows=rows):
                px, py, pc = pos
                return full_ref.at[pl.ds(pl.multiple_of((4 * px + 2 * py + pc) * rows, 16), rows), :]

            def copy(k, blk, to, src=None, a=a, block=block):
                return pltpu.make_async_remote_copy(
                    src_ref=block(blk) if src is None else src, dst_ref=block(blk),
                    send_sem=send_sems.at[7 * a + k], recv_sem=recv_sems.at[7 * a + k],
                    device_id=to, device_id_type=MESH)

            mine = pltpu.make_async_copy(src_ref, block(me), local_sems.at[a])
            mine.start()
            first = [copy(0, me, sibling, src=src_ref)]
            first += [copy(1 + j, me, (*chip, c), src=src_ref) for j, chip in enumerate(chips)]
            for cp in first:
                cp.start()
            started.append((mine, first, copy))
        passed_all = []
        for a in range(n_arr):
            mine, first, copy = started[a]
            passed = [copy(4 + j, (*chip, c), sibling) for j, chip in enumerate(chips)]
            for j, chip in enumerate(chips):
                copy(1 + j, (*chip, c), me).wait_recv()
                passed[j].start()
            passed_all.append(passed)
        for a in range(n_arr):
            mine, first, copy = started[a]
            copy(0, sibling, me).wait_recv()
            for j, chip in enumerate(chips):
                copy(4 + j, (*chip, 1 - c), me).wait_recv()
            for cp in first + passed_all[a]:
                cp.wait_send()
            mine.wait()

    return pl.pallas_call(
        body, name="gather_weights",
        in_specs=[HBM_SPEC, HBM_SPEC], out_specs=[HBM_SPEC, HBM_SPEC],
        out_shape=[jax.ShapeDtypeStruct((D_IN, D), BF16), jax.ShapeDtypeStruct((D, D), BF16)],
        scratch_shapes=[pltpu.SemaphoreType.DMA((7 * n_arr,)), pltpu.SemaphoreType.DMA((7 * n_arr,)),
                        pltpu.SemaphoreType.DMA((n_arr,))],
    )(wt_part, wo_part)


def _to_sibling(dwt, dwo):
    def body(dwt_ref, dwo_ref, rt_ref, ro_ref, send_sems, recv_sems):
        x, y, c = _mesh_pos()
        sibling = (x, y, 1 - c)
        copies = []
        for a, (src, dst, rows) in enumerate(((dwt_ref, rt_ref, W_IN_ROWS), (dwo_ref, ro_ref, W_OUT_ROWS))):
            for j in range(N_CHIPS):
                start = pl.multiple_of((2 * j + (1 - c)) * rows, 16)
                copies.append(pltpu.make_async_remote_copy(
                    src_ref=src.at[pl.ds(start, rows), :], dst_ref=dst.at[j],
                    send_sem=send_sems.at[N_CHIPS * a + j], recv_sem=recv_sems.at[N_CHIPS * a + j],
                    device_id=sibling, device_id_type=MESH))
        for cp in copies:
            cp.start()
        for cp in copies:
            cp.wait_recv()
        for cp in copies:
            cp.wait_send()

    return pl.pallas_call(
        body, name="to_sibling",
        in_specs=[HBM_SPEC, HBM_SPEC], out_specs=[HBM_SPEC, HBM_SPEC],
        out_shape=[jax.ShapeDtypeStruct((N_CHIPS, W_IN_ROWS, D), BF16),
                   jax.ShapeDtypeStruct((N_CHIPS, W_OUT_ROWS, D), BF16)],
        scratch_shapes=[pltpu.SemaphoreType.DMA((2 * N_CHIPS,)), pltpu.SemaphoreType.DMA((2 * N_CHIPS,))],
    )(dwt, dwo)


def _to_owner_chips(pt, po):
    def body(pt_ref, po_ref, bt_ref, bo_ref, send_sems, recv_sems, local_sems):
        x, y, c = _mesh_pos()
        my_chip = 2 * x + y
        peers = [(1 - x, y), (x, 1 - y), (1 - x, 1 - y)]
        local, remote = [], []
        for a, (src, dst) in enumerate(((pt_ref, bt_ref), (po_ref, bo_ref))):
            cp = pltpu.make_async_copy(src.at[my_chip], dst.at[my_chip], local_sems.at[a])
            cp.start()
            local.append(cp)
            for k, (px, py) in enumerate(peers):
                remote.append(pltpu.make_async_remote_copy(
                    src_ref=src.at[2 * px + py], dst_ref=dst.at[my_chip],
                    send_sem=send_sems.at[3 * a + k], recv_sem=recv_sems.at[3 * a + k],
                    device_id=(px, py, c), device_id_type=MESH))
        for cp in remote:
            cp.start()
        for cp in remote:
            cp.wait_recv()
        for cp in remote:
            cp.wait_send()
        for cp in local:
            cp.wait()

    return pl.pallas_call(
        body, name="to_owner_chips",
        in_specs=[HBM_SPEC, HBM_SPEC], out_specs=[HBM_SPEC, HBM_SPEC],
        out_shape=[jax.ShapeDtypeStruct((N_CHIPS, W_IN_ROWS, D), BF16),
                   jax.ShapeDtypeStruct((N_CHIPS, W_OUT_ROWS, D), BF16)],
        scratch_shapes=[pltpu.SemaphoreType.DMA((6,)), pltpu.SemaphoreType.DMA((6,)), pltpu.SemaphoreType.DMA((2,))],
    )(pt, po)


def _pair_halves(rt, ro):
    def body(rt_ref, ro_ref, gt_ref, go_ref, send_sems, recv_sems, local_sems):
        x, y, c = _mesh_pos()
        sibling = (x, y, 1 - c)
        local, remote = [], []
        for a, (src, dst, rows) in enumerate(((rt_ref, gt_ref, W_IN_ROWS), (ro_ref, go_ref, W_OUT_ROWS))):
            mine = dst.at[pl.ds(pl.multiple_of(c * rows, 8), rows), :]
            cp = pltpu.make_async_copy(src, mine, local_sems.at[a])
            cp.start()
            local.append(cp)
            remote.append(pltpu.make_async_remote_copy(
                src_ref=src, dst_ref=mine, send_sem=send_sems.at[a], recv_sem=recv_sems.at[a],
                device_id=sibling, device_id_type=MESH))
        for cp in remote:
            cp.start()
        for cp in remote:
            cp.wait_recv()
        for cp in remote:
            cp.wait_send()
        for cp in local:
            cp.wait()

    return pl.pallas_call(
        body, name="pair_halves",
        in_specs=[HBM_SPEC, HBM_SPEC], out_specs=[HBM_SPEC, HBM_SPEC],
        out_shape=[jax.ShapeDtypeStruct((2 * W_IN_ROWS, D), F32), jax.ShapeDtypeStruct((2 * W_OUT_ROWS, D), F32)],
        scratch_shapes=[pltpu.SemaphoreType.DMA((2,)), pltpu.SemaphoreType.DMA((2,)), pltpu.SemaphoreType.DMA((2,))],
    )(rt, ro)


def _allreduce_small(packed):
    rows = packed.shape[0]

    def body(x_ref, out_ref, buf, send_sems, recv_sems):
        x, y, c = _mesh_pos()
        me, sibling = (x, y, c), (x, y, 1 - c)
        chips = [(1 - x, y), (x, 1 - y), (1 - x, 1 - y)]

        def slot(pos):
            px, py, pc = pos
            return buf.at[4 * px + 2 * py + pc]

        def copy(k, blk, to):
            return pltpu.make_async_remote_copy(
                src_ref=slot(blk), dst_ref=slot(blk),
                send_sem=send_sems.at[k], recv_sem=recv_sems.at[k], device_id=to, device_id_type=MESH)

        buf[4 * x + 2 * y + c] = x_ref[...]
        first = [copy(0, me, sibling)] + [copy(1 + j, me, (*chip, c)) for j, chip in enumerate(chips)]
        for cp in first:
            cp.start()
        passed = [copy(4 + j, (*chip, c), sibling) for j, chip in enumerate(chips)]
        for j, chip in enumerate(chips):
            copy(1 + j, (*chip, c), me).wait_recv()
            passed[j].start()
        copy(0, sibling, me).wait_recv()
        for j, chip in enumerate(chips):
            copy(4 + j, (*chip, 1 - c), me).wait_recv()
        for cp in first + passed:
            cp.wait_send()
        total = buf[0]
        for d in range(1, N_DEV):
            total = total + buf[d]
        out_ref[...] = total

    return pl.pallas_call(
        body, name="allreduce_small",
        in_specs=[pl.BlockSpec(memory_space=pltpu.VMEM)], out_specs=pl.BlockSpec(memory_space=pltpu.VMEM),
        out_shape=jax.ShapeDtypeStruct((rows, 128), F32),
        scratch_shapes=[pltpu.VMEM((N_DEV, rows, 128), F32), pltpu.SemaphoreType.DMA((7,)),
                        pltpu.SemaphoreType.DMA((7,))],
        compiler_params=pltpu.CompilerParams(vmem_limit_bytes=VMEM_LIMIT),
    )(packed)


def _pack_small(parts):
    rows = []
    for name, size, nrows in SMALL:
        flat = parts[name].reshape(-1).astype(F32)
        rows.append(jnp.pad(flat, (0, nrows * 128 - size)).reshape(nrows, 128))
    used = sum(r for _, _, r in SMALL)
    rows.append(jnp.zeros((SMALL_ROWS - used, 128), F32))
    return jnp.concatenate(rows, axis=0)


def _unpack_small(packed, shapes):
    out, r0 = {}, 0
    for name, size, nrows in SMALL:
        out[name] = packed[r0:r0 + nrows].reshape(-1)[:size].reshape(shapes[name])
        r0 += nrows
    return out


def kernel(x, positions, g_pre, w_in, b_qkv, ln_v_g, ln_v_b, w_spatial, b_spatial, attn_sinks, w_out, g_post, loss_target, m_g_pre, m_w_in, m_b_qkv, m_ln_v_g, m_ln_v_b, m_w_spatial, m_b_spatial, m_attn_sinks, m_w_out, m_g_post, v_g_pre, v_w_in, v_b_qkv, v_ln_v_g, v_ln_v_b, v_w_spatial, v_b_spatial, v_attn_sinks, v_w_out, v_g_post):
    S = x.shape[1]
    c = lax.axis_index("c")
    c_arr = jnp.reshape(c, (1,)).astype(jnp.int32)
    x2 = x[0]
    target = loss_target[0]
    pos = positions.reshape(S, 1)
    half = HD // 2
    inv_freq = ROPE_THETA ** (-jnp.arange(half, dtype=F32) * (2.0 / HD))
    invf = jnp.tile(inv_freq, 128 // half).reshape(1, 128)
    bias = jnp.concatenate([jnp.zeros((OFF_Q,), F32), b_qkv[0], jnp.zeros((D_IN - OFF_ZB,), F32)]).reshape(1, D_IN)
    b_s_col = b_spatial[0].reshape(NG, T, 1)
    sinks = attn_sinks[0]

    wt_part = lax.dynamic_slice_in_dim(w_in[0].T.astype(BF16), c * W_IN_ROWS, W_IN_ROWS, axis=0)
    wo_part = lax.dynamic_slice_in_dim(w_out[0].astype(BF16), c * W_OUT_ROWS, W_OUT_ROWS, axis=0)
    wt, wo = _gather_weights(wt_part, wo_part)

    h = _prenorm(x2, g_pre)
    proj = _inproj(h, wt, bias)
    ycat = _mid_fwd(proj, pos, invf, ln_v_g, ln_v_b, w_spatial[0], b_s_col, sinks)
    dy, dout, loss_part, dg_post = _outproj_loss(ycat, wo, x2, target, g_post)

    dycat = _dycat(dy, wo)
    dwo = _matmul_tn(ycat, dy, 512, "dw_out")
    dproj, dln_g, dln_b, dws, dbs, dsink, dbqkv = _mid_bwd(
        proj, dycat, pos, invf, ln_v_g, ln_v_b, w_spatial[0], b_s_col, sinks)
    grad_x, dg_pre = _dh_prenorm_bwd(dproj, wt, x2, dout, g_pre)
    dwt = _matmul_tn(dproj, h, 768, "dw_in_t")

    rt, ro = _to_sibling(dwt, dwo)
    pt = _presum(c_arr, dwt, rt, W_IN_ROWS)
    po = _presum(c_arr, dwo, ro, W_OUT_ROWS)
    bt, bo = _to_owner_chips(pt, po)
    gt_half = _sum_chips(bt, "sum_chips_in")
    go_half = _sum_chips(bo, "sum_chips_out")
    gt, g_w_out = _pair_halves(gt_half, go_half)
    g_w_in = gt.T

    d_w_in, nm_w_in, nv_w_in = _adamw(w_in[0], g_w_in, m_w_in[0], v_w_in[0], "adamw_w_in")
    d_w_out, nm_w_out, nv_w_out = _adamw(w_out[0], g_w_out, m_w_out[0], v_w_out[0], "adamw_w_out")

    small_g = _allreduce_small(_pack_small({
        "g_pre": dg_pre, "b_qkv": dbqkv, "ln_v_g": dln_g, "ln_v_b": dln_b, "w_spatial": dws, "b_spatial": dbs,
        "attn_sinks": dsink[:, 0], "g_post": dg_post, "loss": loss_part[0, :1]}))
    weights = {"g_pre": g_pre, "b_qkv": b_qkv, "ln_v_g": ln_v_g, "ln_v_b": ln_v_b, "w_spatial": w_spatial,
               "b_spatial": b_spatial, "attn_sinks": attn_sinks, "g_post": g_post}
    moms = {"g_pre": m_g_pre, "b_qkv": m_b_qkv, "ln_v_g": m_ln_v_g, "ln_v_b": m_ln_v_b, "w_spatial": m_w_spatial,
            "b_spatial": m_b_spatial, "attn_sinks": m_attn_sinks, "g_post": m_g_post}
    vels = {"g_pre": v_g_pre, "b_qkv": v_b_qkv, "ln_v_g": v_ln_v_g, "ln_v_b": v_ln_v_b, "w_spatial": v_w_spatial,
            "b_spatial": v_b_spatial, "attn_sinks": v_attn_sinks, "g_post": v_g_post}
    zero = jnp.zeros((1,), F32)
    pw = _pack_small({**weights, "loss": zero})
    pm = _pack_small({**moms, "loss": zero})
    pv = _pack_small({**vels, "loss": zero})
    pd, pnm, pnv = _adamw(pw, small_g, pm, pv, "adamw_small")
    shapes = {k: a.shape for k, a in weights.items()}
    shapes["loss"] = ()
    g_s = _unpack_small(small_g, shapes)
    d_s = _unpack_small(pd, shapes)
    m_s = _unpack_small(pnm, shapes)
    v_s = _unpack_small(pnv, shapes)

    def ordered(small, w_in_leaf, w_out_leaf):
        return [small["g_pre"], w_in_leaf[None], small["b_qkv"], small["ln_v_g"], small["ln_v_b"], small["w_spatial"],
                small["b_spatial"], small["attn_sinks"], w_out_leaf[None], small["g_post"]]

    return (g_s["loss"], grad_x[None],
            *ordered(g_s, g_w_in, g_w_out), *ordered(d_s, d_w_in, d_w_out),
            *ordered(m_s, nm_w_in, nm_w_out), *ordered(v_s, nv_w_in, nv_w_out))
```

```python
import functools

import jax
import jax.numpy as jnp
from jax import lax
from jax.experimental import pallas as pl
from jax.experimental.pallas import tpu as pltpu

F32 = jnp.float32
BF16 = jnp.bfloat16
MESH = pl.DeviceIdType.MESH

D = 2048
DG = 1024
T = 128
NG = 8
HD = 64
NQ = 16
D_IN = 5376
OFF_U, OFF_V, OFF_ZA, OFF_Q, OFF_K, OFF_VA, OFF_ZB = 0, 1024, 2048, 3072, 4096, 4224, 4352
D_QKV = 1280
EPS = 1e-6
ROPE_THETA = 10000.0
N_CHIPS = 4
N_DEV = 8
W_IN_ROWS = D_IN // N_DEV
W_OUT_ROWS = D // N_DEV

ADAM_LR, ADAM_B1, ADAM_B2, ADAM_EPS, ADAM_WD, ADAM_STEP = 0.001, 0.9, 0.999, 1e-08, 0.01, 10

VMEM_LIMIT = 56 * 1024 * 1024

SMALL = (("g_pre", 2048, 16), ("b_qkv", 1280, 10), ("ln_v_g", 1024, 8), ("ln_v_b", 1024, 8),
         ("w_spatial", 131072, 1024), ("b_spatial", 1024, 8), ("attn_sinks", 16, 1), ("g_post", 2048, 16),
         ("loss", 1, 1))
SMALL_ROWS = 1096


def _tile(n, pref):
    return pref if n % pref == 0 else n


def _params(sem=None, vmem=VMEM_LIMIT):
    return pltpu.CompilerParams(dimension_semantics=sem, vmem_limit_bytes=vmem)


def _sigmoid(z):
    return 1.0 / (1.0 + jnp.exp(-z))


def _dot(a, b, dims):
    return lax.dot_general(a, b, (dims, ((), ())), preferred_element_type=F32)


NN = ((1,), (0,))
NT = ((1,), (1,))
TN = ((0,), (0,))


def _prenorm_gather(x, g, wt_part):
    S = x.shape[0]
    tm = _tile(S, 512)
    nt = S // tm

    def body(x_ref, g_ref, wpart_ref, h_ref, wt_ref, send_sems, recv_sems, local_sem):
        i = pl.program_id(0)
        gather = _RowGather(wpart_ref, wt_ref, W_IN_ROWS, send_sems, recv_sems, local_sem)
        pl.when(i == 0)(gather.start)
        xv = x_ref[...]
        r = lax.rsqrt(jnp.mean(xv * xv, axis=-1, keepdims=True) + EPS)
        h_ref[...] = (xv * r * g_ref[...]).astype(BF16)
        pl.when(i == nt - 1)(gather.finish)

    return pl.pallas_call(
        body, name="prenorm_gather", grid=(nt,),
        in_specs=[pl.BlockSpec((tm, D), lambda i: (i, 0)), pl.BlockSpec((1, D), lambda i: (0, 0)), HBM_SPEC],
        out_specs=[pl.BlockSpec((tm, D), lambda i: (i, 0)), HBM_SPEC],
        out_shape=[jax.ShapeDtypeStruct((S, D), BF16), jax.ShapeDtypeStruct((D_IN, D), BF16)],
        scratch_shapes=GATHER_SEMS,
        compiler_params=_params(("arbitrary",)),
    )(x, g, wt_part)


def _inproj_gather(h, wt, bias, wo_part):
    S = h.shape[0]
    tm, tn = _tile(S, 1024), 768
    ni, nj = S // tm, D_IN // tn

    def body(h_ref, w_ref, b_ref, wpart_ref, o_ref, wo_ref, send_sems, recv_sems, local_sem):
        i, j = pl.program_id(0), pl.program_id(1)
        gather = _RowGather(wpart_ref, wo_ref, W_OUT_ROWS, send_sems, recv_sems, local_sem)
        pl.when((i == 0) & (j == 0))(gather.start)
        o_ref[...] = _dot(h_ref[...], w_ref[...], NT) + b_ref[...]
        pl.when((i == ni - 1) & (j == nj - 1))(gather.finish)

    return pl.pallas_call(
        body, name="inproj_gather", grid=(ni, nj),
        in_specs=[pl.BlockSpec((tm, D), lambda i, j: (i, 0)), pl.BlockSpec((tn, D), lambda i, j: (j, 0)),
                  pl.BlockSpec((1, tn), lambda i, j: (0, j)), HBM_SPEC],
        out_specs=[pl.BlockSpec((tm, tn), lambda i, j: (i, j)), HBM_SPEC],
        out_shape=[jax.ShapeDtypeStruct((S, D_IN), F32), jax.ShapeDtypeStruct((D, D), BF16)],
        scratch_shapes=GATHER_SEMS,
        compiler_params=_params(("arbitrary", "arbitrary")),
    )(h, wt, bias, wo_part)


def _rope_tables(pos_ref, invf_ref):
    ang = pos_ref[...].astype(F32) * invf_ref[...]
    return jnp.cos(ang), jnp.sin(ang)


def _rot_half(xs, first_half):
    return jnp.where(first_half, -pltpu.roll(xs, 96, 1), pltpu.roll(xs, 32, 1))


def _attn_consts():
    lane = lax.broadcasted_iota(jnp.int32, (T, 128), 1)
    first_half = (lane & (HD - 1)) < (HD // 2)
    lo64 = lane < HD
    qi = lax.broadcasted_iota(jnp.int32, (T, 2 * T), 0) + T
    kj = lax.broadcasted_iota(jnp.int32, (T, 2 * T), 1)
    dist = qi - kj
    band = (dist >= 0) & (dist < T)
    return first_half, lo64, band, kj


def _layer_norm_stats(v):
    mu = jnp.mean(v, axis=-1, keepdims=True)
    xc = v - mu
    var = jnp.mean(xc * xc, axis=-1, keepdims=True)
    rs = lax.rsqrt(var + EPS)
    return xc * rs, rs


def _softmax_sink(qh, kh, valid, sink):
    s = _dot(qh, kh, NT) * (HD ** -0.5)
    s = jnp.where(valid, s, -1e30)
    m = jnp.maximum(jnp.max(s, axis=-1, keepdims=True), sink)
    e = jnp.exp(s - m)
    es = jnp.exp(sink - m)
    den = jnp.sum(e, axis=-1, keepdims=True) + es
    return e / den, es / den


def _mid_specs(nb, rev):
    blk = (lambda i: nb - 1 - i) if rev else (lambda i: i)
    prev = lambda i: jnp.maximum(blk(i) - 1, 0)
    return blk, prev, [
        pl.BlockSpec((T, D_IN), lambda i: (blk(i), 0)),
        pl.BlockSpec((T, 2 * T), lambda i: (prev(i), OFF_K // (2 * T))),
    ], [
        pl.BlockSpec((T, 1), lambda i: (blk(i), 0)),
        pl.BlockSpec((T, 1), lambda i: (prev(i), 0)),
        pl.BlockSpec((1, 128), lambda i: (0, 0)),
        pl.BlockSpec((1, DG), lambda i: (0, 0)),
        pl.BlockSpec((1, DG), lambda i: (0, 0)),
        pl.BlockSpec((NG, T, T), lambda i: (0, 0, 0)),
        pl.BlockSpec((NG, T, 1), lambda i: (0, 0, 0)),
        pl.BlockSpec(memory_space=pltpu.SMEM),
    ]


def _mid_fwd(proj, pos, invf, ln_g, ln_b, w_s, b_s, sinks):
    S = proj.shape[0]
    nb = S // T
    blk, _, proj_specs, par_specs = _mid_specs(nb, False)

    def body(p_ref, kvp_ref, pos_ref, posp_ref, invf_ref, lng_ref, lnb_ref, ws_ref, bs_ref, sink_ref, y_ref):
        n = pl.program_id(0)
        first_half, lo64, band, kj = _attn_consts()
        tril = lax.broadcasted_iota(jnp.int32, (T, T), 0) >= lax.broadcasted_iota(jnp.int32, (T, T), 1)

        xhat, _ = _layer_norm_stats(p_ref[:, OFF_V:OFF_V + DG])
        vn = xhat * lng_ref[...] + lnb_ref[...]
        for g in range(NG):
            sl = slice(128 * g, 128 * g + 128)
            wm = jnp.where(tril, ws_ref[g], 0.0).astype(BF16)
            mixed = _dot(wm, vn[:, sl].astype(BF16), NN) + bs_ref[g]
            z = p_ref[:, OFF_ZA + 128 * g:OFF_ZA + 128 * g + 128]
            u = p_ref[:, OFF_U + 128 * g:OFF_U + 128 * g + 128]
            y_ref[:, sl] = (u * mixed * (z * _sigmoid(z))).astype(BF16)

        cosv, sinv = _rope_tables(pos_ref, invf_ref)
        cosp, sinp = _rope_tables(posp_ref, invf_ref)
        k_cur = p_ref[:, OFF_K:OFF_K + 128]
        k_prev = kvp_ref[:, 0:128]
        kc = jnp.concatenate([k_prev * cosp + _rot_half(k_prev, first_half) * sinp,
                              k_cur * cosv + _rot_half(k_cur, first_half) * sinv], axis=0)
        vc = jnp.concatenate([kvp_ref[:, 128:256], p_ref[:, OFF_VA:OFF_VA + 128]], axis=0)
        kc_d, kc_s = kc.astype(BF16), pltpu.roll(kc, HD, 1).astype(BF16)
        vc_d, vc_s = vc.astype(BF16), pltpu.roll(vc, HD, 1).astype(BF16)
        valid = band & ((kj >= T) | (n > 0))
        for j in range(NQ // 2):
            c0 = OFF_Q + 128 * j
            qraw = p_ref[:, c0:c0 + 128]
            qs = qraw * cosv + _rot_half(qraw, first_half) * sinv
            direct_first = j < NQ // 4
            outs = []
            for half in range(2):
                in_head = lo64 if half == 0 else jnp.logical_not(lo64)
                qh = jnp.where(in_head, qs, 0.0).astype(BF16)
                use_direct = direct_first == (half == 0)
                kh, vh = (kc_d, vc_d) if use_direct else (kc_s, vc_s)
                p, _ = _softmax_sink(qh, kh, valid, sink_ref[2 * j + half])
                outs.append(_dot(p.astype(BF16), vh, NN))
            o = jnp.where(lo64, outs[0], outs[1])
            zb = p_ref[:, OFF_ZB + 128 * j:OFF_ZB + 128 * j + 128]
            y_ref[:, DG + 128 * j:DG + 128 * j + 128] = (o * (zb * _sigmoid(zb))).astype(BF16)

    return pl.pallas_call(
        body, name="mid_fwd", grid=(nb,),
        in_specs=proj_specs + par_specs,
        out_specs=pl.BlockSpec((T, D), lambda i: (blk(i), 0)),
        out_shape=jax.ShapeDtypeStruct((S, D), BF16),
        compiler_params=_params(("arbitrary",)),
    )(proj, proj, pos, pos, invf, ln_g, ln_b, w_s, b_s, sinks)


def _mid_bwd(proj, dycat, pos, invf, ln_g, ln_b, w_s, b_s, sinks, po):
    S = proj.shape[0]
    nb = S // T
    blk, _, proj_specs, par_specs = _mid_specs(nb, True)
    const2 = lambda i: (0, 0)

    def body(p_ref, kvp_ref, dyc_ref, pos_ref, posp_ref, invf_ref, lng_ref, lnb_ref, ws_ref, bs_ref, sink_ref, po_ref,
             dp_ref, dlng_ref, dlnb_ref, dws_ref, dbs_ref, dsink_ref, dbqkv_ref, bo_ref,
             carry_ref, dvn_ref, send_sems, recv_sems, local_sem):
        i = pl.program_id(0)
        n = nb - 1 - i
        first_half, lo64, band, kj = _attn_consts()
        tril = lax.broadcasted_iota(jnp.int32, (T, T), 0) >= lax.broadcasted_iota(jnp.int32, (T, T), 1)
        exchange = _OwnerExchange(po_ref, bo_ref, send_sems, recv_sems, local_sem)
        pl.when(i == 0)(exchange.start)

        @pl.when(i == 0)
        def _():
            dlng_ref[...] = jnp.zeros_like(dlng_ref)
            dlnb_ref[...] = jnp.zeros_like(dlnb_ref)
            dws_ref[...] = jnp.zeros_like(dws_ref)
            dbs_ref[...] = jnp.zeros_like(dbs_ref)
            dsink_ref[...] = jnp.zeros_like(dsink_ref)
            dbqkv_ref[...] = jnp.zeros_like(dbqkv_ref)
            carry_ref[...] = jnp.zeros_like(carry_ref)

        xhat, rs = _layer_norm_stats(p_ref[:, OFF_V:OFF_V + DG])
        lng = lng_ref[...]
        vn = xhat * lng + lnb_ref[...]
        for g in range(NG):
            sl = slice(128 * g, 128 * g + 128)
            wm = jnp.where(tril, ws_ref[g], 0.0).astype(BF16)
            vng = vn[:, sl].astype(BF16)
            mixed = _dot(wm, vng, NN) + bs_ref[g]
            z = p_ref[:, OFF_ZA + 128 * g:OFF_ZA + 128 * g + 128]
            u = p_ref[:, OFF_U + 128 * g:OFF_U + 128 * g + 128]
            dy = dyc_ref[:, sl]
            sg = _sigmoid(z)
            sa = z * sg
            dp_ref[:, OFF_U + 128 * g:OFF_U + 128 * g + 128] = (dy * mixed * sa).astype(BF16)
            dp_ref[:, OFF_ZA + 128 * g:OFF_ZA + 128 * g + 128] = (
                dy * u * mixed * (sg * (1.0 + z * (1.0 - sg)))).astype(BF16)
            dm = dy * u * sa
            dmb = dm.astype(BF16)
            dvn_ref[:, sl] = _dot(wm, dmb, TN)
            dws_ref[g] += jnp.where(tril, _dot(dmb, vng, NT), 0.0)
            dbs_ref[g] += jnp.sum(dm, axis=1, keepdims=True)
        dvn = dvn_ref[...]
        dlng_ref[...] += jnp.sum(dvn * xhat, axis=0, keepdims=True)
        dlnb_ref[...] += jnp.sum(dvn, axis=0, keepdims=True)
        dxh = dvn * lng
        dv_g = rs * (dxh - jnp.mean(dxh, axis=-1, keepdims=True)
                     - xhat * jnp.mean(dxh * xhat, axis=-1, keepdims=True))
        dp_ref[:, OFF_V:OFF_V + DG] = dv_g.astype(BF16)

        cosv, sinv = _rope_tables(pos_ref, invf_ref)
        cosp, sinp = _rope_tables(posp_ref, invf_ref)
        k_cur = p_ref[:, OFF_K:OFF_K + 128]
        k_prev = kvp_ref[:, 0:128]
        kc = jnp.concatenate([k_prev * cosp + _rot_half(k_prev, first_half) * sinp,
                              k_cur * cosv + _rot_half(k_cur, first_half) * sinv], axis=0)
        vc = jnp.concatenate([kvp_ref[:, 128:256], p_ref[:, OFF_VA:OFF_VA + 128]], axis=0)
        kc_d, kc_s = kc.astype(BF16), pltpu.roll(kc, HD, 1).astype(BF16)
        vc_d, vc_s = vc.astype(BF16), pltpu.roll(vc, HD, 1).astype(BF16)
        valid = band & ((kj >= T) | (n > 0))
        acc_k = [jnp.zeros((2 * T, 128), F32), jnp.zeros((2 * T, 128), F32)]
        acc_v = [jnp.zeros((2 * T, 128), F32), jnp.zeros((2 * T, 128), F32)]
        for j in range(NQ // 2):
            c0 = OFF_Q + 128 * j
            qraw = p_ref[:, c0:c0 + 128]
            qs = qraw * cosv + _rot_half(qraw, first_half) * sinv
            grp = 0 if j < NQ // 4 else 1
            zb = p_ref[:, OFF_ZB + 128 * j:OFF_ZB + 128 * j + 128]
            dyb = dyc_ref[:, DG + 128 * j:DG + 128 * j + 128]
            sg = _sigmoid(zb)
            d_o = dyb * (zb * sg)
            outs, dqs = [], []
            for half in range(2):
                in_head = lo64 if half == 0 else jnp.logical_not(lo64)
                qh = jnp.where(in_head, qs, 0.0).astype(BF16)
                doh = jnp.where(in_head, d_o, 0.0)
                dohb = doh.astype(BF16)
                use_direct = (grp == 0) == (half == 0)
                kh, vh = (kc_d, vc_d) if use_direct else (kc_s, vc_s)
                h = 2 * j + half
                p, ps = _softmax_sink(qh, kh, valid, sink_ref[h])
                pb = p.astype(BF16)
                o = _dot(pb, vh, NN)
                delta = jnp.sum(doh * o, axis=-1, keepdims=True)
                dp = _dot(dohb, vh, NT)
                dsb = (p * (dp - delta) * (HD ** -0.5)).astype(BF16)
                dqs.append(_dot(dsb, kh, NN))
                outs.append(o)
                acc_k[grp] = acc_k[grp] + _dot(dsb, qh, TN)
                acc_v[grp] = acc_v[grp] + _dot(pb, dohb, TN)
                dsink_ref[h:h + 1, :] += jnp.broadcast_to(-jnp.sum(ps * delta, axis=0, keepdims=True), (1, 128))
            o = jnp.where(lo64, outs[0], outs[1])
            dq = jnp.where(lo64, dqs[0], dqs[1])
            dp_ref[:, OFF_ZB + 128 * j:OFF_ZB + 128 * j + 128] = (
                dyb * o * (sg * (1.0 + zb * (1.0 - sg)))).astype(BF16)
            dq_pre = dq * cosv - _rot_half(dq, first_half) * sinv
            dp_ref[:, c0:c0 + 128] = dq_pre.astype(BF16)
            dbqkv_ref[:, 128 * j:128 * j + 128] += jnp.sum(dq_pre, axis=0, keepdims=True)
        lo64_kv = lax.broadcasted_iota(jnp.int32, (2 * T, 128), 1) < HD
        dkc = jnp.where(lo64_kv, acc_k[0] + pltpu.roll(acc_k[0], HD, 1), acc_k[1] + pltpu.roll(acc_k[1], HD, 1))
        dvc = jnp.where(lo64_kv, acc_v[0] + pltpu.roll(acc_v[0], HD, 1), acc_v[1] + pltpu.roll(acc_v[1], HD, 1))
        dk_cur = dkc[T:2 * T] + carry_ref[:, 0:128]
        dv_cur = dvc[T:2 * T] + carry_ref[:, 128:256]
        carry_ref[:, 0:128] = dkc[0:T]
        carry_ref[:, 128:256] = dvc[0:T]
        dk_pre = dk_cur * cosv - _rot_half(dk_cur, first_half) * sinv
        dp_ref[:, OFF_K:OFF_K + 128] = dk_pre.astype(BF16)
        dp_ref[:, OFF_VA:OFF_VA + 128] = dv_cur.astype(BF16)
        dbqkv_ref[:, 1024:1152] += jnp.sum(dk_pre, axis=0, keepdims=True)
        dbqkv_ref[:, 1152:1280] += jnp.sum(dv_cur, axis=0, keepdims=True)
        pl.when(i == nb - 1)(exchange.finish)

    return pl.pallas_call(
        body, name="mid_bwd", grid=(nb,),
        in_specs=proj_specs + [pl.BlockSpec((T, D), lambda i: (blk(i), 0))] + par_specs + [HBM_SPEC],
        out_specs=[pl.BlockSpec((T, D_IN), lambda i: (blk(i), 0)),
                   pl.BlockSpec((1, DG), const2), pl.BlockSpec((1, DG), const2),
                   pl.BlockSpec((NG, T, T), lambda i: (0, 0, 0)), pl.BlockSpec((NG, T, 1), lambda i: (0, 0, 0)),
                   pl.BlockSpec((NQ, 128), const2), pl.BlockSpec((1, D_QKV), const2), HBM_SPEC],
        out_shape=[jax.ShapeDtypeStruct((S, D_IN), BF16),
                   jax.ShapeDtypeStruct((1, DG), F32), jax.ShapeDtypeStruct((1, DG), F32),
                   jax.ShapeDtypeStruct((NG, T, T), F32), jax.ShapeDtypeStruct((NG, T, 1), F32),
                   jax.ShapeDtypeStruct((NQ, 128), F32), jax.ShapeDtypeStruct((1, D_QKV), F32),
                   jax.ShapeDtypeStruct(po.shape, BF16)],
        scratch_shapes=[pltpu.VMEM((T, 2 * T), F32), pltpu.VMEM((T, DG), F32)] + OWNER_SEMS,
        compiler_params=_params(("arbitrary",)),
    )(proj, proj, dycat, pos, pos, invf, ln_g, ln_b, w_s, b_s, sinks, po)


def _outproj_loss(ycat, wo, x, target, g_post):
    S = ycat.shape[0]
    tm = _tile(S, 256)
    nt = S // tm
    const2 = lambda i: (0, 0)

    def body(yc_ref, w_ref, x_ref, t_ref, g_ref, dy_ref, dout_ref, loss_ref, dg_ref, lacc_ref):
        i = pl.program_id(0)

        @pl.when(i == 0)
        def _():
            dg_ref[...] = jnp.zeros_like(dg_ref)
            lacc_ref[...] = jnp.zeros_like(lacc_ref)

        y = _dot(yc_ref[...], w_ref[...], NN)
        r = lax.rsqrt(jnp.mean(y * y, axis=-1, keepdims=True) + EPS)
        yh = y * r
        g = g_ref[...]
        diff = x_ref[...] + yh * g - t_ref[...]
        lacc_ref[...] += jnp.sum(diff * diff, axis=0, keepdims=True)
        dout = diff * (1.0 / D)
        dout_ref[...] = dout
        dg_ref[...] += jnp.sum(dout * yh, axis=0, keepdims=True)
        dyh = dout * g
        dy_ref[...] = (r * (dyh - yh * jnp.mean(dyh * yh, axis=-1, keepdims=True))).astype(BF16)

        @pl.when(i == nt - 1)
        def _():
            loss_ref[...] = jnp.broadcast_to(jnp.sum(lacc_ref[...], axis=1, keepdims=True) * (0.5 / D), (1, 128))

    row = lambda i: (i, 0)
    return pl.pallas_call(
        body, name="outproj_loss", grid=(nt,),
        in_specs=[pl.BlockSpec((tm, D), row), pl.BlockSpec((D, D), const2), pl.BlockSpec((tm, D), row),
                  pl.BlockSpec((tm, D), row), pl.BlockSpec((1, D), const2)],
        out_specs=[pl.BlockSpec((tm, D), row), pl.BlockSpec((tm, D), row), pl.BlockSpec((1, 128), const2),
                   pl.BlockSpec((1, D), const2)],
        out_shape=[jax.ShapeDtypeStruct((S, D), BF16), jax.ShapeDtypeStruct((S, D), F32),
                   jax.ShapeDtypeStruct((1, 128), F32), jax.ShapeDtypeStruct((1, D), F32)],
        scratch_shapes=[pltpu.VMEM((1, D), F32)],
        compiler_params=_params(("arbitrary",)),
    )(ycat, wo, x, target, g_post)


def _dycat(dy, wo):
    S = dy.shape[0]
    tm = _tile(S, 512)

    def body(dy_ref, w_ref, o_ref):
        o_ref[...] = _dot(dy_ref[...], w_ref[...], NT)

    return pl.pallas_call(
        body, name="dycat", grid=(S // tm,),
        in_specs=[pl.BlockSpec((tm, D), lambda i: (i, 0)), pl.BlockSpec((D, D), lambda i: (0, 0))],
        out_specs=pl.BlockSpec((tm, D), lambda i: (i, 0)),
        out_shape=jax.ShapeDtypeStruct((S, D), F32),
        compiler_params=_params(("parallel",)),
    )(dy, wo)


def _matmul_tn(a, b, tm, name):
    K, M = a.shape
    N = b.shape[1]
    tk = _tile(K, 1024)
    nk = K // tk

    def body(a_ref, b_ref, o_ref, acc_ref):
        k = pl.program_id(1)

        @pl.when(k == 0)
        def _():
            acc_ref[...] = jnp.zeros_like(acc_ref)

        acc_ref[...] += _dot(a_ref[...], b_ref[...], TN)

        @pl.when(k == nk - 1)
        def _():
            o_ref[...] = acc_ref[...].astype(BF16)

    return pl.pallas_call(
        body, name=name, grid=(M // tm, nk),
        in_specs=[pl.BlockSpec((tk, tm), lambda i, k: (k, i)), pl.BlockSpec((tk, N), lambda i, k: (k, 0))],
        out_specs=pl.BlockSpec((tm, N), lambda i, k: (i, 0)),
        out_shape=jax.ShapeDtypeStruct((M, N), BF16),
        scratch_shapes=[pltpu.VMEM((tm, N), F32)],
        compiler_params=_params(("parallel", "arbitrary")),
    )(a, b)


def _dh_prenorm_bwd(dproj, wt, x, dout, g_pre, pt):
    S = x.shape[0]
    tm, tk = _tile(S, 512), 768
    ni, nk = S // tm, D_IN // tk

    def body(dp_ref, w_ref, x_ref, dout_ref, g_ref, pt_ref, gx_ref, dg_ref, bt_ref, acc_ref,
             send_sems, recv_sems, local_sem):
        i, k = pl.program_id(0), pl.program_id(1)
        exchange = _OwnerExchange(pt_ref, bt_ref, send_sems, recv_sems, local_sem)
        pl.when((i == 0) & (k == 0))(exchange.start)

        @pl.when((i == 0) & (k == 0))
        def _():
            dg_ref[...] = jnp.zeros_like(dg_ref)

        @pl.when(k == 0)
        def _():
            acc_ref[...] = jnp.zeros_like(acc_ref)

        acc_ref[...] += _dot(dp_ref[...], w_ref[...], NN)

        @pl.when(k == nk - 1)
        def _():
            dh = acc_ref[...]
            xv = x_ref[...]
            r = lax.rsqrt(jnp.mean(xv * xv, axis=-1, keepdims=True) + EPS)
            xh = xv * r
            dg_ref[...] += jnp.sum(dh * xh, axis=0, keepdims=True)
            dxh = dh * g_ref[...]
            gx_ref[...] = dout_ref[...] + r * (dxh - xh * jnp.mean(dxh * xh, axis=-1, keepdims=True))

        pl.when((i == ni - 1) & (k == nk - 1))(exchange.finish)

    row = lambda i, k: (i, 0)
    return pl.pallas_call(
        body, name="dh_prenorm_bwd", grid=(ni, nk),
        in_specs=[pl.BlockSpec((tm, tk), lambda i, k: (i, k)), pl.BlockSpec((tk, D), lambda i, k: (k, 0)),
                  pl.BlockSpec((tm, D), row), pl.BlockSpec((tm, D), row), pl.BlockSpec((1, D), lambda i, k: (0, 0)),
                  HBM_SPEC],
        out_specs=[pl.BlockSpec((tm, D), row), pl.BlockSpec((1, D), lambda i, k: (0, 0)), HBM_SPEC],
        out_shape=[jax.ShapeDtypeStruct((S, D), F32), jax.ShapeDtypeStruct((1, D), F32),
                   jax.ShapeDtypeStruct(pt.shape, BF16)],
        scratch_shapes=[pltpu.VMEM((tm, D), F32)] + OWNER_SEMS,
        compiler_params=_params(("arbitrary", "arbitrary")),
    )(dproj, wt, x, dout, g_pre, pt)


def _presum(c_arr, own, recv, half_rows):
    n_cols = own.shape[-1]
    own4 = own.reshape(N_CHIPS, 2, half_rows, n_cols)

    def body(c_ref, own_ref, recv_ref, o_ref):
        o_ref[...] = (own_ref[...].astype(F32) + recv_ref[...].astype(F32)).astype(BF16)

    return pl.pallas_call(
        body, name="presum_%d" % half_rows,
        grid_spec=pltpu.PrefetchScalarGridSpec(
            num_scalar_prefetch=1, grid=(N_CHIPS,),
            in_specs=[pl.BlockSpec((None, None, half_rows, n_cols), lambda j, c: (j, c[0], 0, 0)),
                      pl.BlockSpec((None, half_rows, n_cols), lambda j, c: (j, 0, 0))],
            out_specs=pl.BlockSpec((None, half_rows, n_cols), lambda j, c: (j, 0, 0))),
        out_shape=jax.ShapeDtypeStruct((N_CHIPS, half_rows, n_cols), BF16),
        compiler_params=_params(("parallel",)),
    )(c_arr, own4, recv)


def _sum_chips(c_arr, parts, name):
    _, rows, n_cols = parts.shape
    nt = 2
    tr = rows // nt

    def body(c_ref, p_ref, o_ref):
        o_ref[...] = ((p_ref[0].astype(F32) + p_ref[1].astype(F32)) + p_ref[2].astype(F32)) + p_ref[3].astype(F32)

    return pl.pallas_call(
        body, name=name,
        grid_spec=pltpu.PrefetchScalarGridSpec(
            num_scalar_prefetch=1, grid=(nt,),
            in_specs=[pl.BlockSpec((N_CHIPS, tr, n_cols), lambda i, c: (0, i, 0))],
            out_specs=pl.BlockSpec((tr, n_cols), lambda i, c: (c[0] * nt + i, 0))),
        out_shape=jax.ShapeDtypeStruct((2 * rows, n_cols), F32),
        compiler_params=_params(("parallel",)),
    )(c_arr, parts)


def _adamw(w, g, m, v, name):
    R, C = w.shape
    tr = _tile(R, 256)
    c1 = 1.0 - ADAM_B1 ** ADAM_STEP
    c2 = 1.0 - ADAM_B2 ** ADAM_STEP

    def body(w_ref, g_ref, m_ref, v_ref, d_ref, mo_ref, vo_ref):
        gv = g_ref[...]
        mn = ADAM_B1 * m_ref[...] + (1.0 - ADAM_B1) * gv
        vn = ADAM_B2 * v_ref[...] + (1.0 - ADAM_B2) * (gv * gv)
        mo_ref[...] = mn
        vo_ref[...] = vn
        d_ref[...] = -ADAM_LR * ((mn / c1) / (jnp.sqrt(vn / c2) + ADAM_EPS) + ADAM_WD * w_ref[...])

    spec = pl.BlockSpec((tr, C), lambda i: (i, 0))
    shp = jax.ShapeDtypeStruct((R, C), F32)
    return pl.pallas_call(
        body, name=name, grid=(R // tr,), in_specs=[spec] * 4, out_specs=[spec] * 3, out_shape=[shp] * 3,
        compiler_params=_params(("parallel",)),
    )(w, g, m, v)


HBM_SPEC = pl.BlockSpec(memory_space=pltpu.HBM)
GATHER_SEMS = [pltpu.SemaphoreType.DMA((7,)), pltpu.SemaphoreType.DMA((7,)), pltpu.SemaphoreType.DMA(())]
OWNER_SEMS = [pltpu.SemaphoreType.DMA((3,)), pltpu.SemaphoreType.DMA((3,)), pltpu.SemaphoreType.DMA(())]


def _mesh_pos():
    return lax.axis_index("x"), lax.axis_index("y"), lax.axis_index("c")


class _RowGather:
    def __init__(self, src_ref, full_ref, rows, send_sems, recv_sems, local_sem):
        self.src, self.full, self.rows = src_ref, full_ref, rows
        self.send, self.recv, self.local = send_sems, recv_sems, local_sem
        x, y, c = _mesh_pos()
        self.c, self.me, self.sibling = c, (x, y, c), (x, y, 1 - c)
        self.chips = [(1 - x, y), (x, 1 - y), (1 - x, 1 - y)]

    def _block(self, pos):
        px, py, pc = pos
        return self.full.at[pl.ds(pl.multiple_of((4 * px + 2 * py + pc) * self.rows, 16), self.rows), :]

    def _copy(self, k, blk, to, own=False):
        return pltpu.make_async_remote_copy(
            src_ref=self.src if own else self._block(blk), dst_ref=self._block(blk),
            send_sem=self.send.at[k], recv_sem=self.recv.at[k], device_id=to, device_id_type=MESH)

    def _mine(self):
        return pltpu.make_async_copy(self.src, self._block(self.me), self.local)

    def _first(self):
        return [self._copy(0, self.me, self.sibling, own=True)] + [
            self._copy(1 + j, self.me, (*chip, self.c), own=True) for j, chip in enumerate(self.chips)]

    def start(self):
        self._mine().start()
        for cp in self._first():
            cp.start()

    def finish(self):
        passed = [self._copy(4 + j, (*chip, self.c), self.sibling) for j, chip in enumerate(self.chips)]
        for j, chip in enumerate(self.chips):
            self._copy(1 + j, (*chip, self.c), self.me).wait_recv()
            passed[j].start()
        self._copy(0, self.sibling, self.me).wait_recv()
        for j, chip in enumerate(self.chips):
            self._copy(4 + j, (*chip, 1 - self.c), self.me).wait_recv()
        for cp in self._first() + passed:
            cp.wait_send()
        self._mine().wait()


class _OwnerExchange:
    def __init__(self, src_ref, dst_ref, send_sems, recv_sems, local_sem):
        self.src, self.dst, self.send, self.recv, self.local = src_ref, dst_ref, send_sems, recv_sems, local_sem
        x, y, c = _mesh_pos()
        self.c, self.my_chip = c, 2 * x + y
        self.peers = [(1 - x, y), (x, 1 - y), (1 - x, 1 - y)]

    def _copies(self):
        local = pltpu.make_async_copy(self.src.at[self.my_chip], self.dst.at[self.my_chip], self.local)
        remote = [pltpu.make_async_remote_copy(
            src_ref=self.src.at[2 * px + py], dst_ref=self.dst.at[self.my_chip],
            send_sem=self.send.at[k], recv_sem=self.recv.at[k], device_id=(px, py, self.c), device_id_type=MESH)
            for k, (px, py) in enumerate(self.peers)]
        return local, remote

    def start(self):
        local, remote = self._copies()
        local.start()
        for cp in remote:
            cp.start()

    def finish(self):
        local, remote = self._copies()
        for cp in remote:
            cp.wait_recv()
        for cp in remote:
            cp.wait_send()
        local.wait()


def _to_sibling(dw, rows, name):
    def body(dw_ref, r_ref, send_sems, recv_sems):
        x, y, c = _mesh_pos()
        copies = [pltpu.make_async_remote_copy(
            src_ref=dw_ref.at[pl.ds(pl.multiple_of((2 * j + (1 - c)) * rows, 16), rows), :], dst_ref=r_ref.at[j],
            send_sem=send_sems.at[j], recv_sem=recv_sems.at[j], device_id=(x, y, 1 - c), device_id_type=MESH)
            for j in range(N_CHIPS)]
        for cp in copies:
            cp.start()
        for cp in copies:
            cp.wait_recv()
        for cp in copies:
            cp.wait_send()

    return pl.pallas_call(
        body, name=name, in_specs=[HBM_SPEC], out_specs=HBM_SPEC,
        out_shape=jax.ShapeDtypeStruct((N_CHIPS, rows, dw.shape[1]), BF16),
        scratch_shapes=[pltpu.SemaphoreType.DMA((N_CHIPS,)), pltpu.SemaphoreType.DMA((N_CHIPS,))],
    )(dw)


PAIR_CHUNKS = 4


def _pair_halves(gt, go):
    def body(gt_in, go_in, gt_ref, go_ref, send_sems, recv_sems):
        del gt_in, go_in
        x, y, c = _mesh_pos()
        copies = []
        for a, (ref, rows) in enumerate(((gt_ref, W_IN_ROWS), (go_ref, W_OUT_ROWS))):
            ch = rows // PAIR_CHUNKS
            for q in range(PAIR_CHUNKS):
                part = ref.at[pl.ds(pl.multiple_of(c * rows + q * ch, 8), ch), :]
                copies.append(pltpu.make_async_remote_copy(
                    src_ref=part, dst_ref=part, send_sem=send_sems.at[PAIR_CHUNKS * a + q],
                    recv_sem=recv_sems.at[PAIR_CHUNKS * a + q], device_id=(x, y, 1 - c), device_id_type=MESH))
        for cp in copies:
            cp.start()
        for cp in copies:
            cp.wait_recv()
        for cp in copies:
            cp.wait_send()

    return pl.pallas_call(
        body, name="pair_halves",
        in_specs=[HBM_SPEC, HBM_SPEC], out_specs=[HBM_SPEC, HBM_SPEC],
        out_shape=[jax.ShapeDtypeStruct(gt.shape, F32), jax.ShapeDtypeStruct(go.shape, F32)],
        input_output_aliases={0: 0, 1: 1},
        scratch_shapes=[pltpu.SemaphoreType.DMA((2 * PAIR_CHUNKS,)), pltpu.SemaphoreType.DMA((2 * PAIR_CHUNKS,))],
    )(gt, go)


def _allreduce_small(packed):
    rows = packed.shape[0]

    def body(x_ref, out_ref, buf, send_sems, recv_sems):
        x, y, c = _mesh_pos()
        me, sibling = (x, y, c), (x, y, 1 - c)
        chips = [(1 - x, y), (x, 1 - y), (1 - x, 1 - y)]

        def slot(pos):
            px, py, pc = pos
            return buf.at[4 * px + 2 * py + pc]

        def copy(k, blk, to):
            return pltpu.make_async_remote_copy(
                src_ref=slot(blk), dst_ref=slot(blk),
                send_sem=send_sems.at[k], recv_sem=recv_sems.at[k], device_id=to, device_id_type=MESH)

        buf[4 * x + 2 * y + c] = x_ref[...]
        first = [copy(0, me, sibling)] + [copy(1 + j, me, (*chip, c)) for j, chip in enumerate(chips)]
        for cp in first:
            cp.start()
        passed = [copy(4 + j, (*chip, c), sibling) for j, chip in enumerate(chips)]
        for j, chip in enumerate(chips):
            copy(1 + j, (*chip, c), me).wait_recv()
            passed[j].start()
        copy(0, sibling, me).wait_recv()
        for j, chip in enumerate(chips):
            copy(4 + j, (*chip, 1 - c), me).wait_recv()
        for cp in first + passed:
            cp.wait_send()
        total = buf[0]
        for d in range(1, N_DEV):
            total = total + buf[d]
        out_ref[...] = total

    return pl.pallas_call(
        body, name="allreduce_small",
        in_specs=[pl.BlockSpec(memory_space=pltpu.VMEM)], out_specs=pl.BlockSpec(memory_space=pltpu.VMEM),
        out_shape=jax.ShapeDtypeStruct((rows, 128), F32),
        scratch_shapes=[pltpu.VMEM((N_DEV, rows, 128), F32), pltpu.SemaphoreType.DMA((7,)),
                        pltpu.SemaphoreType.DMA((7,))],
        compiler_params=pltpu.CompilerParams(vmem_limit_bytes=VMEM_LIMIT),
    )(packed)


def _pack_small(parts):
    rows = []
    for name, size, nrows in SMALL:
        flat = parts[name].reshape(-1).astype(F32)
        rows.append(jnp.pad(flat, (0, nrows * 128 - size)).reshape(nrows, 128))
    used = sum(r for _, _, r in SMALL)
    rows.append(jnp.zeros((SMALL_ROWS - used, 128), F32))
    return jnp.concatenate(rows, axis=0)


def _unpack_small(packed, shapes):
    out, r0 = {}, 0
    for name, size, nrows in SMALL:
        out[name] = packed[r0:r0 + nrows].reshape(-1)[:size].reshape(shapes[name])
        r0 += nrows
    return out


def kernel(x, positions, g_pre, w_in, b_qkv, ln_v_g, ln_v_b, w_spatial, b_spatial, attn_sinks, w_out, g_post, loss_target, m_g_pre, m_w_in, m_b_qkv, m_ln_v_g, m_ln_v_b, m_w_spatial, m_b_spatial, m_attn_sinks, m_w_out, m_g_post, v_g_pre, v_w_in, v_b_qkv, v_ln_v_g, v_ln_v_b, v_w_spatial, v_b_spatial, v_attn_sinks, v_w_out, v_g_post):
    S = x.shape[1]
    c = lax.axis_index("c")
    c_arr = jnp.reshape(c, (1,)).astype(jnp.int32)
    x2 = x[0]
    target = loss_target[0]
    pos = positions.reshape(S, 1)
    half = HD // 2
    inv_freq = ROPE_THETA ** (-jnp.arange(half, dtype=F32) * (2.0 / HD))
    invf = jnp.tile(inv_freq, 128 // half).reshape(1, 128)
    bias = jnp.concatenate([jnp.zeros((OFF_Q,), F32), b_qkv[0], jnp.zeros((D_IN - OFF_ZB,), F32)]).reshape(1, D_IN)
    b_s_col = b_spatial[0].reshape(NG, T, 1)
    sinks = attn_sinks[0]

    wt_part = lax.dynamic_slice_in_dim(w_in[0].T.astype(BF16), c * W_IN_ROWS, W_IN_ROWS, axis=0)
    wo_part = lax.dynamic_slice_in_dim(w_out[0].astype(BF16), c * W_OUT_ROWS, W_OUT_ROWS, axis=0)

    h, wt = _prenorm_gather(x2, g_pre, wt_part)
    proj, wo = _inproj_gather(h, wt, bias, wo_part)
    ycat = _mid_fwd(proj, pos, invf, ln_v_g, ln_v_b, w_spatial[0], b_s_col, sinks)
    dy, dout, loss_part, dg_post = _outproj_loss(ycat, wo, x2, target, g_post)

    dycat = _dycat(dy, wo)
    dwo = _matmul_tn(ycat, dy, 512, "dw_out")
    po = _presum(c_arr, dwo, _to_sibling(dwo, W_OUT_ROWS, "to_sibling_out"), W_OUT_ROWS)
    dproj, dln_g, dln_b, dws, dbs, dsink, dbqkv, bo = _mid_bwd(
        proj, dycat, pos, invf, ln_v_g, ln_v_b, w_spatial[0], b_s_col, sinks, po)
    dwt = _matmul_tn(dproj, h, 768, "dw_in_t")
    pt = _presum(c_arr, dwt, _to_sibling(dwt, W_IN_ROWS, "to_sibling_in"), W_IN_ROWS)
    grad_x, dg_pre, bt = _dh_prenorm_bwd(dproj, wt, x2, dout, g_pre, pt)
    gt, g_w_out = _pair_halves(_sum_chips(c_arr, bt, "sum_chips_in"), _sum_chips(c_arr, bo, "sum_chips_out"))
    g_w_in = gt.T

    d_w_in, nm_w_in, nv_w_in = _adamw(w_in[0], g_w_in, m_w_in[0], v_w_in[0], "adamw_w_in")
    d_w_out, nm_w_out, nv_w_out = _adamw(w_out[0], g_w_out, m_w_out[0], v_w_out[0], "adamw_w_out")

    small_g = _allreduce_small(_pack_small({
        "g_pre": dg_pre, "b_qkv": dbqkv, "ln_v_g": dln_g, "ln_v_b": dln_b, "w_spatial": dws, "b_spatial": dbs,
        "attn_sinks": dsink[:, 0], "g_post": dg_post, "loss": loss_part[0, :1]}))
    weights = {"g_pre": g_pre, "b_qkv": b_qkv, "ln_v_g": ln_v_g, "ln_v_b": ln_v_b, "w_spatial": w_spatial,
               "b_spatial": b_spatial, "attn_sinks": attn_sinks, "g_post": g_post}
    moms = {"g_pre": m_g_pre, "b_qkv": m_b_qkv, "ln_v_g": m_ln_v_g, "ln_v_b": m_ln_v_b, "w_spatial": m_w_spatial,
            "b_spatial": m_b_spatial, "attn_sinks": m_attn_sinks, "g_post": m_g_post}
    vels = {"g_pre": v_g_pre, "b_qkv": v_b_qkv, "ln_v_g": v_ln_v_g, "ln_v_b": v_ln_v_b, "w_spatial": v_w_spatial,
            "b_spatial": v_b_spatial, "attn_sinks": v_attn_sinks, "g_post": v_g_post}
    zero = jnp.zeros((1,), F32)
    pw = _pack_small({**weights, "loss": zero})
    pm = _pack_small({**moms, "loss": zero})
    pv = _pack_small({**vels, "loss": zero})
    pd, pnm, pnv = _adamw(pw, small_g, pm, pv, "adamw_small")
    shapes = {k: a.shape for k, a in weights.items()}
    shapes["loss"] = ()
    g_s = _unpack_small(small_g, shapes)
    d_s = _unpack_small(pd, shapes)
    m_s = _unpack_small(pnm, shapes)
    v_s = _unpack_small(pnv, shapes)

    def ordered(small, w_in_leaf, w_out_leaf):
        return [small["g_pre"], w_in_leaf[None], small["b_qkv"], small["ln_v_g"], small["ln_v_b"], small["w_spatial"],
                small["b_spatial"], small["attn_sinks"], w_out_leaf[None], small["g_post"]]

    return (g_s["loss"], grad_x[None],
            *ordered(g_s, g_w_in, g_w_out), *ordered(d_s, d_w_in, d_w_out),
            *ordered(m_s, nm_w_in, nm_w_out), *ordered(v_s, nv_w_in, nv_w_out))
```

```python
import jax
import jax.numpy as jnp
import numpy as np
from jax import lax
from jax.experimental import pallas as pl
from jax.experimental.pallas import tpu as pltpu

F32 = jnp.float32
BF16 = jnp.bfloat16
MESH = pl.DeviceIdType.MESH

D = 2048
DG = 1024
T = 128
NG = 8
HD = 64
NQ = 16
D_IN = 5376
OFF_U, OFF_V, OFF_ZA, OFF_Q, OFF_K, OFF_VA, OFF_ZB = 0, 1024, 2048, 3072, 4096, 4224, 4352
D_QKV = 1280
EPS = 1e-6
ROPE_THETA = 10000.0
N_CHIPS = 4
N_DEV = 8
W_IN_ROWS = D_IN // N_DEV
W_OUT_ROWS = D // N_DEV

ADAM_LR, ADAM_B1, ADAM_B2, ADAM_EPS, ADAM_WD, ADAM_STEP = 0.001, 0.9, 0.999, 1e-08, 0.01, 10

VMEM_LIMIT = 56 * 1024 * 1024


def _tile(n, pref):
    return pref if n % pref == 0 else n


def _params(sem=None, vmem=VMEM_LIMIT):
    return pltpu.CompilerParams(dimension_semantics=sem, vmem_limit_bytes=vmem)


def _sigmoid(z):
    return 1.0 / (1.0 + jnp.exp(-z))


def _dot(a, b, dims):
    return lax.dot_general(a, b, (dims, ((), ())), preferred_element_type=F32)


NN = ((1,), (0,))
NT = ((1,), (1,))
TN = ((0,), (0,))


def _prenorm_gather(x, g, wt_part):
    S = x.shape[0]
    tm = _tile(S, 512)
    nt = S // tm

    def body(x_ref, g_ref, wpart_ref, h_ref, wt_ref, send_sems, recv_sems, local_sem):
        i = pl.program_id(0)
        gather = _RowGather(wpart_ref, wt_ref, W_IN_ROWS, send_sems, recv_sems, local_sem)
        pl.when(i == 0)(gather.start)
        xv = x_ref[...]
        r = lax.rsqrt(jnp.mean(xv * xv, axis=-1, keepdims=True) + EPS)
        h_ref[...] = (xv * r * g_ref[...]).astype(BF16)
        pl.when(i == nt - 1)(gather.finish)

    return pl.pallas_call(
        body, name="prenorm_gather", grid=(nt,),
        in_specs=[pl.BlockSpec((tm, D), lambda i: (i, 0)), pl.BlockSpec((1, D), lambda i: (0, 0)), HBM_SPEC],
        out_specs=[pl.BlockSpec((tm, D), lambda i: (i, 0)), HBM_SPEC],
        out_shape=[jax.ShapeDtypeStruct((S, D), BF16), jax.ShapeDtypeStruct((D_IN, D), BF16)],
        scratch_shapes=GATHER_SEMS,
        compiler_params=_params(("arbitrary",)),
    )(x, g, wt_part)


def _inproj_gather(h, wt, bias, wo_part):
    S = h.shape[0]
    tm, tn = _tile(S, 1024), 768
    ni, nj = S // tm, D_IN // tn

    def body(h_ref, w_ref, b_ref, wpart_ref, o_ref, wo_ref, send_sems, recv_sems, local_sem):
        i, j = pl.program_id(0), pl.program_id(1)
        gather = _RowGather(wpart_ref, wo_ref, W_OUT_ROWS, send_sems, recv_sems, local_sem)
        pl.when((i == 0) & (j == 0))(gather.start)
        o_ref[...] = _dot(h_ref[...], w_ref[...], NT) + b_ref[...]
        pl.when((i == ni - 1) & (j == nj - 1))(gather.finish)

    return pl.pallas_call(
        body, name="inproj_gather", grid=(ni, nj),
        in_specs=[pl.BlockSpec((tm, D), lambda i, j: (i, 0)), pl.BlockSpec((tn, D), lambda i, j: (j, 0)),
                  pl.BlockSpec((1, tn), lambda i, j: (0, j)), HBM_SPEC],
        out_specs=[pl.BlockSpec((tm, tn), lambda i, j: (i, j)), HBM_SPEC],
        out_shape=[jax.ShapeDtypeStruct((S, D_IN), F32), jax.ShapeDtypeStruct((D, D), BF16)],
        scratch_shapes=GATHER_SEMS,
        compiler_params=_params(("arbitrary", "arbitrary")),
    )(h, wt, bias, wo_part)


def _rope_tables(pos_ref, invf_ref):
    ang = pos_ref[...].astype(F32) * invf_ref[...]
    return jnp.cos(ang), jnp.sin(ang)


def _rot_half(xs, first_half):
    return jnp.where(first_half, -pltpu.roll(xs, 96, 1), pltpu.roll(xs, 32, 1))


GH = NQ // 2


MASKED = -1e30


def _score_bias():
    qi = (np.arange(GH * T)[:, None] % T) + T
    kj = np.arange(2 * T)[None, :]
    band = (qi - kj >= 0) & (qi - kj < T)
    return jnp.asarray(np.where(np.stack([band & (kj >= T), band]), 0.0, MASKED), dtype=F32)


def _attn_consts():
    lane = lax.broadcasted_iota(jnp.int32, (T, 128), 1)
    return (lane & (HD - 1)) < (HD // 2), lane < HD


def _stack_heads(slab_fn, grp, lo64):
    blocks = []
    for jj in range(GH // 2):
        s = slab_fn(GH // 2 * grp + jj)
        r = pltpu.roll(s, HD, 1)
        if grp == 0:
            blocks += [jnp.where(lo64, s, 0.0), jnp.where(lo64, r, 0.0)]
        else:
            blocks += [jnp.where(lo64, 0.0, r), jnp.where(lo64, 0.0, s)]
    return jnp.concatenate(blocks, axis=0)


def _unstack_heads(stacked, jj, grp, lo64):
    a = stacked[(2 * jj) * T:(2 * jj + 1) * T]
    b = stacked[(2 * jj + 1) * T:(2 * jj + 2) * T]
    if grp == 0:
        return jnp.where(lo64, a, pltpu.roll(b, HD, 1))
    return jnp.where(lo64, pltpu.roll(a, HD, 1), b)


def _layer_norm_stats(v):
    mu = jnp.mean(v, axis=-1, keepdims=True)
    xc = v - mu
    var = jnp.mean(xc * xc, axis=-1, keepdims=True)
    rs = lax.rsqrt(var + EPS)
    return xc * rs, rs


def _softmax_sink(q_scaled, k, bias, sink):
    s = _dot(q_scaled, k, NT) + bias
    m = jnp.maximum(jnp.max(s, axis=-1, keepdims=True), sink)
    e = jnp.exp(s - m)
    es = jnp.exp(sink - m)
    inv = 1.0 / (jnp.sum(e, axis=-1, keepdims=True) + es)
    return e * inv, es * inv


def _mid_specs(nb, rev):
    blk = (lambda i: nb - 1 - i) if rev else (lambda i: i)
    prev = lambda i: jnp.maximum(blk(i) - 1, 0)
    return blk, prev, [
        pl.BlockSpec((T, D_IN), lambda i: (blk(i), 0)),
        pl.BlockSpec((T, 2 * T), lambda i: (prev(i), OFF_K // (2 * T))),
    ], [
        pl.BlockSpec((T, 1), lambda i: (blk(i), 0)),
        pl.BlockSpec((T, 1), lambda i: (prev(i), 0)),
        pl.BlockSpec((1, 128), lambda i: (0, 0)),
        pl.BlockSpec((1, DG), lambda i: (0, 0)),
        pl.BlockSpec((1, DG), lambda i: (0, 0)),
        pl.BlockSpec((NG, T, T), lambda i: (0, 0, 0)),
        pl.BlockSpec((NG, T, 1), lambda i: (0, 0, 0)),
        pl.BlockSpec((NQ * T, 1), lambda i: (0, 0)),
        pl.BlockSpec((None, GH * T, 2 * T), lambda i: (jnp.minimum(blk(i), 1), 0, 0)),
    ]


def _mid_fwd(proj, pos, invf, ln_g, ln_b, w_s, b_s, sinks):
    S = proj.shape[0]
    nb = S // T
    blk, _, proj_specs, par_specs = _mid_specs(nb, False)

    def body(p_ref, kvp_ref, pos_ref, posp_ref, invf_ref, lng_ref, lnb_ref, ws_ref, bs_ref, sink_ref, bias_ref,
             y_ref):
        n = pl.program_id(0)
        first_half, lo64 = _attn_consts()
        tril = lax.broadcasted_iota(jnp.int32, (T, T), 0) >= lax.broadcasted_iota(jnp.int32, (T, T), 1)

        xhat, _ = _layer_norm_stats(p_ref[:, OFF_V:OFF_V + DG])
        vn = xhat * lng_ref[...] + lnb_ref[...]
        for g in range(NG):
            sl = slice(128 * g, 128 * g + 128)
            wm = jnp.where(tril, ws_ref[g], 0.0).astype(BF16)
            mixed = _dot(wm, vn[:, sl].astype(BF16), NN) + bs_ref[g]
            z = p_ref[:, OFF_ZA + 128 * g:OFF_ZA + 128 * g + 128]
            u = p_ref[:, OFF_U + 128 * g:OFF_U + 128 * g + 128]
            y_ref[:, sl] = (u * mixed * (z * _sigmoid(z))).astype(BF16)

        cosv, sinv = _rope_tables(pos_ref, invf_ref)
        cosp, sinp = _rope_tables(posp_ref, invf_ref)
        k_cur = p_ref[:, OFF_K:OFF_K + 128]
        k_prev = kvp_ref[:, 0:128]
        kc = jnp.concatenate([k_prev * cosp + _rot_half(k_prev, first_half) * sinp,
                              k_cur * cosv + _rot_half(k_cur, first_half) * sinv], axis=0)
        vc = jnp.concatenate([kvp_ref[:, 128:256], p_ref[:, OFF_VA:OFF_VA + 128]], axis=0)
        kcb, vcb = kc.astype(BF16), vc.astype(BF16)

        def q_slab(j):
            qraw = p_ref[:, OFF_Q + 128 * j:OFF_Q + 128 * j + 128]
            return (qraw * cosv + _rot_half(qraw, first_half) * sinv) * (HD ** -0.5)

        for grp in range(2):
            qs = _stack_heads(q_slab, grp, lo64).astype(BF16)
            p, _ = _softmax_sink(qs, kcb, bias_ref[...], sink_ref[GH * T * grp:GH * T * (grp + 1), :])
            o_st = _dot(p.astype(BF16), vcb, NN)
            for jj in range(GH // 2):
                c0 = 128 * (GH // 2 * grp + jj)
                zb = p_ref[:, OFF_ZB + c0:OFF_ZB + c0 + 128]
                o = _unstack_heads(o_st, jj, grp, lo64)
                y_ref[:, DG + c0:DG + c0 + 128] = (o * (zb * _sigmoid(zb))).astype(BF16)

    return pl.pallas_call(
        body, name="mid_fwd", grid=(nb,),
        in_specs=proj_specs + par_specs,
        out_specs=pl.BlockSpec((T, D), lambda i: (blk(i), 0)),
        out_shape=jax.ShapeDtypeStruct((S, D), BF16),
        compiler_params=_params(("arbitrary",)),
    )(proj, proj, pos, pos, invf, ln_g, ln_b, w_s, b_s, sinks, _score_bias())


def _mid_bwd(proj, dycat, pos, invf, ln_g, ln_b, w_s, b_s, sinks, po):
    S = proj.shape[0]
    nb = S // T
    blk, _, proj_specs, par_specs = _mid_specs(nb, True)
    const2 = lambda i: (0, 0)

    def body(p_ref, kvp_ref, dyc_ref, pos_ref, posp_ref, invf_ref, lng_ref, lnb_ref, ws_ref, bs_ref, sink_ref, bias_ref, po_ref,
             dp_ref, dlng_ref, dlnb_ref, dws_ref, dbs_ref, dsink_ref, dbqkv_ref, bo_ref,
             carry_ref, dvn_ref, send_sems, recv_sems, local_sem):
        i = pl.program_id(0)
        n = nb - 1 - i
        first_half, lo64 = _attn_consts()
        tril = lax.broadcasted_iota(jnp.int32, (T, T), 0) >= lax.broadcasted_iota(jnp.int32, (T, T), 1)
        exchange =_OwnerExchange(po_ref, bo_ref, send_sems, recv_sems, local_sem)
        pl.when(i == 0)(exchange.start)

        @pl.when(i == 0)
        def _():
            dlng_ref[...] = jnp.zeros_like(dlng_ref)
            dlnb_ref[...] = jnp.zeros_like(dlnb_ref)
            dws_ref[...] = jnp.zeros_like(dws_ref)
            dbs_ref[...] = jnp.zeros_like(dbs_ref)
            dsink_ref[...] = jnp.zeros_like(dsink_ref)
            dbqkv_ref[...] = jnp.zeros_like(dbqkv_ref)
            carry_ref[...] = jnp.zeros_like(carry_ref)

        xhat, rs = _layer_norm_stats(p_ref[:, OFF_V:OFF_V + DG])
        lng = lng_ref[...]
        vn = xhat * lng + lnb_ref[...]
        for g in range(NG):
            sl = slice(128 * g, 128 * g + 128)
            wm = jnp.where(tril, ws_ref[g], 0.0).astype(BF16)
            vng = vn[:, sl].astype(BF16)
            mixed = _dot(wm, vng, NN) + bs_ref[g]
            z = p_ref[:, OFF_ZA + 128 * g:OFF_ZA + 128 * g + 128]
            u = p_ref[:, OFF_U + 128 * g:OFF_U + 128 * g + 128]
            dy = dyc_ref[:, sl]
            sg = _sigmoid(z)
            sa = z * sg
            dp_ref[:, OFF_U + 128 * g:OFF_U + 128 * g + 128] = (dy * mixed * sa).astype(BF16)
            dp_ref[:, OFF_ZA + 128 * g:OFF_ZA + 128 * g + 128] = (
                dy * u * mixed * (sg * (1.0 + z * (1.0 - sg)))).astype(BF16)
            dm = dy * u * sa
            dmb = dm.astype(BF16)
            dvn_ref[:, sl] = _dot(wm, dmb, TN)
            dws_ref[g] += jnp.where(tril, _dot(dmb, vng, NT), 0.0)
            dbs_ref[g] += jnp.sum(dm, axis=1, keepdims=True)
        dvn = dvn_ref[...]
        dlng_ref[...] += jnp.sum(dvn * xhat, axis=0, keepdims=True)
        dlnb_ref[...] += jnp.sum(dvn, axis=0, keepdims=True)
        dxh = dvn * lng
        dv_g = rs * (dxh - jnp.mean(dxh, axis=-1, keepdims=True)
                     - xhat * jnp.mean(dxh * xhat, axis=-1, keepdims=True))
        dp_ref[:, OFF_V:OFF_V + DG] = dv_g.astype(BF16)

        cosv, sinv = _rope_tables(pos_ref, invf_ref)
        cosp, sinp = _rope_tables(posp_ref, invf_ref)
        k_cur = p_ref[:, OFF_K:OFF_K + 128]
        k_prev = kvp_ref[:, 0:128]
        kc = jnp.concatenate([k_prev * cosp + _rot_half(k_prev, first_half) * sinp,
                              k_cur * cosv + _rot_half(k_cur, first_half) * sinv], axis=0)
        vc = jnp.concatenate([kvp_ref[:, 128:256], p_ref[:, OFF_VA:OFF_VA + 128]], axis=0)
        kcb, vcb = kc.astype(BF16), vc.astype(BF16)

        def q_slab(j):
            qraw = p_ref[:, OFF_Q + 128 * j:OFF_Q + 128 * j + 128]
            return (qraw * cosv + _rot_half(qraw, first_half) * sinv) * (HD ** -0.5)

        def do_slab(j):
            zb = p_ref[:, OFF_ZB + 128 * j:OFF_ZB + 128 * j + 128]
            return dyc_ref[:, DG + 128 * j:DG + 128 * j + 128] * (zb * _sigmoid(zb))

        dkc = jnp.zeros((2 * T, 128), F32)
        dvc = jnp.zeros((2 * T, 128), F32)
        for grp in range(2):
            qs = _stack_heads(q_slab, grp, lo64).astype(BF16)
            d_o = _stack_heads(do_slab, grp, lo64)
            dob = d_o.astype(BF16)
            p, ps = _softmax_sink(qs, kcb, bias_ref[...], sink_ref[GH * T * grp:GH * T * (grp + 1), :])
            pb = p.astype(BF16)
            o_st = _dot(pb, vcb, NN)
            delta = jnp.sum(d_o * o_st, axis=-1, keepdims=True)
            dsb = (p * (_dot(dob, vcb, NT) - delta)).astype(BF16)
            dq_st = _dot(dsb, kcb, NN) * (HD ** -0.5)
            dkc = dkc + _dot(dsb, qs, TN)
            dvc = dvc + _dot(pb, dob, TN)
            dsink_rows = ps * delta
            for hh in range(GH):
                h = GH * grp + hh
                dsink_ref[h:h + 1, :] += jnp.broadcast_to(
                    -jnp.sum(dsink_rows[hh * T:(hh + 1) * T], axis=0, keepdims=True), (1, 128))
            for jj in range(GH // 2):
                c0 = 128 * (GH // 2 * grp + jj)
                zb = p_ref[:, OFF_ZB + c0:OFF_ZB + c0 + 128]
                sg = _sigmoid(zb)
                o = _unstack_heads(o_st, jj, grp, lo64)
                dp_ref[:, OFF_ZB + c0:OFF_ZB + c0 + 128] = (
                    dyc_ref[:, DG + c0:DG + c0 + 128] * o * (sg * (1.0 + zb * (1.0 - sg)))).astype(BF16)
                dq = _unstack_heads(dq_st, jj, grp, lo64)
                dq_pre = dq * cosv - _rot_half(dq, first_half) * sinv
                dp_ref[:, OFF_Q + c0:OFF_Q + c0 + 128] = dq_pre.astype(BF16)
                dbqkv_ref[:, c0:c0 + 128] += jnp.sum(dq_pre, axis=0, keepdims=True)
        dk_cur = dkc[T:2 * T] + carry_ref[:, 0:128]
        dv_cur = dvc[T:2 * T] + carry_ref[:, 128:256]
        carry_ref[:, 0:128] = dkc[0:T]
        carry_ref[:, 128:256] = dvc[0:T]
        dk_pre = dk_cur * cosv - _rot_half(dk_cur, first_half) * sinv
        dp_ref[:, OFF_K:OFF_K + 128] = dk_pre.astype(BF16)
        dp_ref[:, OFF_VA:OFF_VA + 128] = dv_cur.astype(BF16)
        dbqkv_ref[:, 1024:1152] += jnp.sum(dk_pre, axis=0, keepdims=True)
        dbqkv_ref[:, 1152:1280] += jnp.sum(dv_cur, axis=0, keepdims=True)
        pl.when(i == nb - 1)(exchange.finish)

    return pl.pallas_call(
        body, name="mid_bwd", grid=(nb,),
        in_specs=proj_specs + [pl.BlockSpec((T, D), lambda i: (blk(i), 0))] + par_specs + [HBM_SPEC],
        out_specs=[pl.BlockSpec((T, D_IN), lambda i: (blk(i), 0)),
                   pl.BlockSpec((1, DG), const2), pl.BlockSpec((1, DG), const2),
                   pl.BlockSpec((NG, T, T), lambda i: (0, 0, 0)), pl.BlockSpec((NG, T, 1), lambda i: (0, 0, 0)),
                   pl.BlockSpec((NQ, 128), const2), pl.BlockSpec((1, D_QKV), const2), HBM_SPEC],
        out_shape=[jax.ShapeDtypeStruct((S, D_IN), BF16),
                   jax.ShapeDtypeStruct((1, DG), F32), jax.ShapeDtypeStruct((1, DG), F32),
                   jax.ShapeDtypeStruct((NG, T, T), F32), jax.ShapeDtypeStruct((NG, T, 1), F32),
                   jax.ShapeDtypeStruct((NQ, 128), F32), jax.ShapeDtypeStruct((1, D_QKV), F32),
                   jax.ShapeDtypeStruct(po.shape, BF16)],
        scratch_shapes=[pltpu.VMEM((T, 2 * T), F32), pltpu.VMEM((T, DG), F32)] + OWNER_SEMS,
        compiler_params=_params(("arbitrary",)),
    )(proj, proj, dycat, pos, pos, invf, ln_g, ln_b, w_s, b_s, sinks, _score_bias(), po)


def _outproj_loss(ycat, wo, x, target, g_post):
    S = ycat.shape[0]
    tm = _tile(S, 256)
    nt = S // tm
    const2 = lambda i: (0, 0)

    def body(yc_ref, w_ref, x_ref, t_ref, g_ref, dy_ref, dout_ref, loss_ref, dg_ref, lacc_ref):
        i = pl.program_id(0)

        @pl.when(i == 0)
        def _():
            dg_ref[...] = jnp.zeros_like(dg_ref)
            lacc_ref[...] = jnp.zeros_like(lacc_ref)

        y = _dot(yc_ref[...], w_ref[...], NN)
        r = lax.rsqrt(jnp.mean(y * y, axis=-1, keepdims=True) + EPS)
        yh = y * r
        g = g_ref[...]
        diff = x_ref[...] + yh * g - t_ref[...]
        lacc_ref[...] += jnp.sum(diff * diff, axis=0, keepdims=True)
        dout = diff * (1.0 / D)
        dout_ref[...] = dout
        dg_ref[...] += jnp.sum(dout * yh, axis=0, keepdims=True)
        dyh = dout * g
        dy_ref[...] = (r * (dyh - yh * jnp.mean(dyh * yh, axis=-1, keepdims=True))).astype(BF16)

        @pl.when(i == nt - 1)
        def _():
            loss_ref[...] = jnp.broadcast_to(jnp.sum(lacc_ref[...], axis=1, keepdims=True) * (0.5 / D), (1, 128))

    row = lambda i: (i, 0)
    return pl.pallas_call(
        body, name="outproj_loss", grid=(nt,),
        in_specs=[pl.BlockSpec((tm, D), row), pl.BlockSpec((D, D), const2), pl.BlockSpec((tm, D), row),
                  pl.BlockSpec((tm, D), row), pl.BlockSpec((1, D), const2)],
        out_specs=[pl.BlockSpec((tm, D), row), pl.BlockSpec((tm, D), row), pl.BlockSpec((1, 128), const2),
                   pl.BlockSpec((1, D), const2)],
        out_shape=[jax.ShapeDtypeStruct((S, D), BF16), jax.ShapeDtypeStruct((S, D), F32),
                   jax.ShapeDtypeStruct((1, 128), F32), jax.ShapeDtypeStruct((1, D), F32)],
        scratch_shapes=[pltpu.VMEM((1, D), F32)],
        compiler_params=_params(("arbitrary",)),
    )(ycat, wo, x, target, g_post)


def _dycat(dy, wo):
    S = dy.shape[0]
    tm = _tile(S, 512)

    def body(dy_ref, w_ref, o_ref):
        o_ref[...] = _dot(dy_ref[...], w_ref[...], NT)

    return pl.pallas_call(
        body, name="dycat", grid=(S // tm,),
        in_specs=[pl.BlockSpec((tm, D), lambda i: (i, 0)), pl.BlockSpec((D, D), lambda i: (0, 0))],
        out_specs=pl.BlockSpec((tm, D), lambda i: (i, 0)),
        out_shape=jax.ShapeDtypeStruct((S, D), F32),
        compiler_params=_params(("parallel",)),
    )(dy, wo)


def _matmul_tn(a, b, tm, name):
    K, M = a.shape
    N = b.shape[1]
    tk = _tile(K, 1024)
    nk = K // tk

    def body(a_ref, b_ref, o_ref, acc_ref):
        k = pl.program_id(1)

        @pl.when(k == 0)
        def _():
            acc_ref[...] = jnp.zeros_like(acc_ref)

        acc_ref[...] += _dot(a_ref[...], b_ref[...], TN)

        @pl.when(k == nk - 1)
        def _():
            o_ref[...] = acc_ref[...].astype(BF16)

    return pl.pallas_call(
        body, name=name, grid=(M // tm, nk),
        in_specs=[pl.BlockSpec((tk, tm), lambda i, k: (k, i)), pl.BlockSpec((tk, N), lambda i, k: (k, 0))],
        out_specs=pl.BlockSpec((tm, N), lambda i, k: (i, 0)),
        out_shape=jax.ShapeDtypeStruct((M, N), BF16),
        scratch_shapes=[pltpu.VMEM((tm, N), F32)],
        compiler_params=_params(("parallel", "arbitrary")),
    )(a, b)


def _dh_prenorm_bwd(dproj, wt, x, dout, g_pre, pt):
    S = x.shape[0]
    tm, tk = _tile(S, 512), 768
    ni, nk = S // tm, D_IN // tk

    def body(dp_ref, w_ref, x_ref, dout_ref, g_ref, pt_ref, gx_ref, dg_ref, bt_ref, acc_ref,
             send_sems, recv_sems, local_sem):
        i, k = pl.program_id(0), pl.program_id(1)
        exchange = _OwnerExchange(pt_ref, bt_ref, send_sems, recv_sems, local_sem)
        pl.when((i == 0) & (k == 0))(exchange.start)

        @pl.when((i == 0) & (k == 0))
        def _():
            dg_ref[...] = jnp.zeros_like(dg_ref)

        @pl.when(k == 0)
        def _():
            acc_ref[...] = jnp.zeros_like(acc_ref)

        acc_ref[...] += _dot(dp_ref[...], w_ref[...], NN)

        @pl.when(k == nk - 1)
        def _():
            dh = acc_ref[...]
            xv = x_ref[...]
            r = lax.rsqrt(jnp.mean(xv * xv, axis=-1, keepdims=True) + EPS)
            xh = xv * r
            dg_ref[...] += jnp.sum(dh * xh, axis=0, keepdims=True)
            dxh = dh * g_ref[...]
            gx_ref[...] = dout_ref[...] + r * (dxh - xh * jnp.mean(dxh * xh, axis=-1, keepdims=True))

        pl.when((i == ni - 1) & (k == nk - 1))(exchange.finish)

    row = lambda i, k: (i, 0)
    return pl.pallas_call(
        body, name="dh_prenorm_bwd", grid=(ni, nk),
        in_specs=[pl.BlockSpec((tm, tk), lambda i, k: (i, k)), pl.BlockSpec((tk, D), lambda i, k: (k, 0)),
                  pl.BlockSpec((tm, D), row), pl.BlockSpec((tm, D), row), pl.BlockSpec((1, D), lambda i, k: (0, 0)),
                  HBM_SPEC],
        out_specs=[pl.BlockSpec((tm, D), row), pl.BlockSpec((1, D), lambda i, k: (0, 0)), HBM_SPEC],
        out_shape=[jax.ShapeDtypeStruct((S, D), F32), jax.ShapeDtypeStruct((1, D), F32),
                   jax.ShapeDtypeStruct(pt.shape, BF16)],
        scratch_shapes=[pltpu.VMEM((tm, D), F32)] + OWNER_SEMS,
        compiler_params=_params(("arbitrary", "arbitrary")),
    )(dproj, wt, x, dout, g_pre, pt)


def _presum(c_arr, own, recv, half_rows):
    n_cols = own.shape[-1]
    own4 = own.reshape(N_CHIPS, 2, half_rows, n_cols)

    def body(c_ref, own_ref, recv_ref, o_ref):
        o_ref[...] = (own_ref[...].astype(F32) + recv_ref[...].astype(F32)).astype(BF16)

    return pl.pallas_call(
        body, name="presum_%d" % half_rows,
        grid_spec=pltpu.PrefetchScalarGridSpec(
            num_scalar_prefetch=1, grid=(N_CHIPS,),
            in_specs=[pl.BlockSpec((None, None, half_rows, n_cols), lambda j, c: (j, c[0], 0, 0)),
                      pl.BlockSpec((None, half_rows, n_cols), lambda j, c: (j, 0, 0))],
            out_specs=pl.BlockSpec((None, half_rows, n_cols), lambda j, c: (j, 0, 0))),
        out_shape=jax.ShapeDtypeStruct((N_CHIPS, half_rows, n_cols), BF16),
        compiler_params=_params(("parallel",)),
    )(c_arr, own4, recv)


def _sum_chips(c_arr, parts, name):
    _, rows, n_cols = parts.shape
    nt = 2
    tr = rows // nt

    def body(c_ref, p_ref, o_ref):
        o_ref[...] = ((p_ref[0].astype(F32) + p_ref[1].astype(F32)) + p_ref[2].astype(F32)) + p_ref[3].astype(F32)

    return pl.pallas_call(
        body, name=name,
        grid_spec=pltpu.PrefetchScalarGridSpec(
            num_scalar_prefetch=1, grid=(nt,),
            in_specs=[pl.BlockSpec((N_CHIPS, tr, n_cols), lambda i, c: (0, i, 0))],
            out_specs=pl.BlockSpec((tr, n_cols), lambda i, c: (c[0] * nt + i, 0))),
        out_shape=jax.ShapeDtypeStruct((2 * rows, n_cols), F32),
        compiler_params=_params(("parallel",)),
    )(c_arr, parts)


def _adamw_math(w, g, m, v):
    mn = ADAM_B1 * m + (1.0 - ADAM_B1) * g
    vn = ADAM_B2 * v + (1.0 - ADAM_B2) * (g * g)
    m_hat = mn / (1.0 - ADAM_B1 ** ADAM_STEP)
    v_hat = vn / (1.0 - ADAM_B2 ** ADAM_STEP)
    return -ADAM_LR * (m_hat / (jnp.sqrt(v_hat) + ADAM_EPS) + ADAM_WD * w), mn, vn


def _adamw(w, g, m, v, name):
    R, C = w.shape
    tr = next((t for t in (256, 192, 128) if R % t == 0), R)

    def body(w_ref, g_ref, m_ref, v_ref, d_ref, mo_ref, vo_ref):
        d_ref[...], mo_ref[...], vo_ref[...] = _adamw_math(w_ref[...], g_ref[...], m_ref[...], v_ref[...])

    spec = pl.BlockSpec((tr, C), lambda i: (i, 0))
    shp = jax.ShapeDtypeStruct((R, C), F32)
    return pl.pallas_call(
        body, name=name, grid=(R // tr,), in_specs=[spec] * 4, out_specs=[spec] * 3, out_shape=[shp] * 3,
        compiler_params=_params(("parallel",)),
    )(w, g, m, v)


HBM_SPEC = pl.BlockSpec(memory_space=pltpu.HBM)
GATHER_SEMS = [pltpu.SemaphoreType.DMA((7,)), pltpu.SemaphoreType.DMA((7,)), pltpu.SemaphoreType.DMA(())]
OWNER_SEMS = [pltpu.SemaphoreType.DMA((3,)), pltpu.SemaphoreType.DMA((3,)), pltpu.SemaphoreType.DMA(())]


def _mesh_pos():
    return lax.axis_index("x"), lax.axis_index("y"), lax.axis_index("c")


class _RowGather:
    def __init__(self, src_ref, full_ref, rows, send_sems, recv_sems, local_sem):
        self.src, self.full, self.rows = src_ref, full_ref, rows
        self.send, self.recv, self.local = send_sems, recv_sems, local_sem
        x, y, c = _mesh_pos()
        self.c, self.me, self.sibling = c, (x, y, c), (x, y, 1 - c)
        self.chips = [(1 - x, y), (x, 1 - y), (1 - x, 1 - y)]

    def _block(self, pos):
        px, py, pc = pos
        return self.full.at[pl.ds(pl.multiple_of((4 * px + 2 * py + pc) * self.rows, 16), self.rows), :]

    def _copy(self, k, blk, to, own=False):
        return pltpu.make_async_remote_copy(
            src_ref=self.src if own else self._block(blk), dst_ref=self._block(blk),
            send_sem=self.send.at[k], recv_sem=self.recv.at[k], device_id=to, device_id_type=MESH)

    def _mine(self):
        return pltpu.make_async_copy(self.src, self._block(self.me), self.local)

    def _first(self):
        return [self._copy(0, self.me, self.sibling, own=True)] + [
            self._copy(1 + j, self.me, (*chip, self.c), own=True) for j, chip in enumerate(self.chips)]

    def start(self):
        self._mine().start()
        for cp in self._first():
            cp.start()

    def finish(self):
        passed = [self._copy(4 + j, (*chip, self.c), self.sibling) for j, chip in enumerate(self.chips)]
        for j, chip in enumerate(self.chips):
            self._copy(1 + j, (*chip, self.c), self.me).wait_recv()
            passed[j].start()
        self._copy(0, self.sibling, self.me).wait_recv()
        for j, chip in enumerate(self.chips):
            self._copy(4 + j, (*chip, 1 - self.c), self.me).wait_recv()
        for cp in self._first() + passed:
            cp.wait_send()
        self._mine().wait()


class _OwnerExchange:
    def __init__(self, src_ref, dst_ref, send_sems, recv_sems, local_sem):
        self.src, self.dst, self.send, self.recv, self.local = src_ref, dst_ref, send_sems, recv_sems, local_sem
        x, y, c = _mesh_pos()
        self.c, self.my_chip = c, 2 * x + y
        self.peers = [(1 - x, y), (x, 1 - y), (1 - x, 1 - y)]

    def _copies(self):
        local = pltpu.make_async_copy(self.src.at[self.my_chip], self.dst.at[self.my_chip], self.local)
        remote = [pltpu.make_async_remote_copy(
            src_ref=self.src.at[2 * px + py], dst_ref=self.dst.at[self.my_chip],
            send_sem=self.send.at[k], recv_sem=self.recv.at[k], device_id=(px, py, self.c), device_id_type=MESH)
            for k, (px, py) in enumerate(self.peers)]
        return local, remote

    def start(self):
        local, remote = self._copies()
        local.start()
        for cp in remote:
            cp.start()

    def finish(self):
        local, remote = self._copies()
        for cp in remote:
            cp.wait_recv()
        for cp in remote:
            cp.wait_send()
        local.wait()


def _to_sibling(dw, rows, name):
    def body(dw_ref, r_ref, send_sems, recv_sems):
        x, y, c = _mesh_pos()
        copies = [pltpu.make_async_remote_copy(
            src_ref=dw_ref.at[pl.ds(pl.multiple_of((2 * j + (1 - c)) * rows, 16), rows), :], dst_ref=r_ref.at[j],
            send_sem=send_sems.at[j], recv_sem=recv_sems.at[j], device_id=(x, y, 1 - c), device_id_type=MESH)
            for j in range(N_CHIPS)]
        for cp in copies:
            cp.start()
        for cp in copies:
            cp.wait_recv()
        for cp in copies:
            cp.wait_send()

    return pl.pallas_call(
        body, name=name, in_specs=[HBM_SPEC], out_specs=HBM_SPEC,
        out_shape=jax.ShapeDtypeStruct((N_CHIPS, rows, dw.shape[1]), BF16),
        scratch_shapes=[pltpu.SemaphoreType.DMA((N_CHIPS,)), pltpu.SemaphoreType.DMA((N_CHIPS,))],
    )(dw)


PAIR_CHUNKS = 4


def _pair_halves(gt, go):
    def body(gt_in, go_in, gt_ref, go_ref, send_sems, recv_sems):
        del gt_in, go_in
        x, y, c = _mesh_pos()
        copies = []
        for a, (ref, rows) in enumerate(((gt_ref, W_IN_ROWS), (go_ref, W_OUT_ROWS))):
            ch = rows // PAIR_CHUNKS
            for q in range(PAIR_CHUNKS):
                part = ref.at[pl.ds(pl.multiple_of(c * rows + q * ch, 8), ch), :]
                copies.append(pltpu.make_async_remote_copy(
                    src_ref=part, dst_ref=part, send_sem=send_sems.at[PAIR_CHUNKS * a + q],
                    recv_sem=recv_sems.at[PAIR_CHUNKS * a + q], device_id=(x, y, 1 - c), device_id_type=MESH))
        for cp in copies:
            cp.start()
        for cp in copies:
            cp.wait_recv()
        for cp in copies:
            cp.wait_send()

    return pl.pallas_call(
        body, name="pair_halves",
        in_specs=[HBM_SPEC, HBM_SPEC], out_specs=[HBM_SPEC, HBM_SPEC],
        out_shape=[jax.ShapeDtypeStruct(gt.shape, F32), jax.ShapeDtypeStruct(go.shape, F32)],
        input_output_aliases={0: 0, 1: 1},
        scratch_shapes=[pltpu.SemaphoreType.DMA((2 * PAIR_CHUNKS,)), pltpu.SemaphoreType.DMA((2 * PAIR_CHUNKS,))],
    )(gt, go)


VEC = (("g_pre", 2048), ("g_post", 2048), ("b_qkv", 1280), ("ln_v_g", 1024), ("ln_v_b", 1024), ("attn_sinks", 16))
VEC_ROWS = 8
LOSS_ROW = len(VEC)
MAT = (("w_spatial", NG * T), ("b_spatial", NG))
MAT_ROWS = sum(r for _, r in MAT)


def _small_update(vec_grads, loss_part, mat_grads, vec_state, mat_state):
    n_vec, n_mat = len(VEC), len(MAT)
    n_par = n_vec + n_mat
    n_in = n_par + 1 + 3 * n_par

    def body(*refs):
        g_in, loss_in = refs[:n_par], refs[n_par]
        st_in = refs[n_par + 1:n_in]
        outs, loss_out = refs[n_in:n_in + 4 * n_par], refs[n_in + 4 * n_par]
        vbuf, mbuf, tot_v, tot_m, send_sems, recv_sems = refs[n_in + 4 * n_par + 1:]
        x, y, c = _mesh_pos()
        me, sibling = (x, y, c), (x, y, 1 - c)
        chips = [(1 - x, y), (x, 1 - y), (1 - x, 1 - y)]
        my_id = 4 * x + 2 * y + c

        vbuf[my_id] = jnp.zeros((VEC_ROWS, D), F32)
        for r, (_, n) in enumerate(VEC):
            vbuf[my_id, r:r + 1, 0:n] = g_in[r][...]
        vbuf[my_id, LOSS_ROW:LOSS_ROW + 1, 0:128] = loss_in[...]
        r0 = 0
        for q, (_, rows) in enumerate(MAT):
            mbuf[my_id, r0:r0 + rows, :] = g_in[n_vec + q][...]
            r0 += rows

        def copy(a, k, blk, to):
            buf = (vbuf, mbuf)[a]
            px, py, pc = blk
            slot = buf.at[4 * px + 2 * py + pc]
            return pltpu.make_async_remote_copy(
                src_ref=slot, dst_ref=slot, send_sem=send_sems.at[7 * a + k], recv_sem=recv_sems.at[7 * a + k],
                device_id=to, device_id_type=MESH)

        first = [copy(a, 0, me, sibling) for a in range(2)]
        first += [copy(a, 1 + j, me, (*chip, c)) for a in range(2) for j, chip in enumerate(chips)]
        for cp in first:
            cp.start()
        passed = []
        for a in range(2):
            for j, chip in enumerate(chips):
                copy(a, 1 + j, (*chip, c), me).wait_recv()
                passed.append(copy(a, 4 + j, (*chip, c), sibling))
                passed[-1].start()
        for a in range(2):
            copy(a, 0, sibling, me).wait_recv()
            for j, chip in enumerate(chips):
                copy(a, 4 + j, (*chip, 1 - c), me).wait_recv()
        for cp in first + passed:
            cp.wait_send()

        tv, tm = vbuf[0], mbuf[0]
        for d in range(1, N_DEV):
            tv, tm = tv + vbuf[d], tm + mbuf[d]
        tot_v[...] = tv
        tot_m[...] = tm
        loss_out[...] = tot_v[LOSS_ROW:LOSS_ROW + 1, 0:128]
        r0 = 0
        for q in range(n_par):
            if q < n_vec:
                g = tot_v[q:q + 1, 0:VEC[q][1]]
            else:
                rows = MAT[q - n_vec][1]
                g = tot_m[r0:r0 + rows, :]
                r0 += rows
            w, m, v = (st_in[3 * q + t][...] for t in range(3))
            outs[4 * q][...] = g
            outs[4 * q + 1][...], outs[4 * q + 2][...], outs[4 * q + 3][...] = _adamw_math(w, g, m, v)

    grads = list(vec_grads) + list(mat_grads)
    state = [a for wmv in list(vec_state) + list(mat_state) for a in wmv]
    vmem = pl.BlockSpec(memory_space=pltpu.VMEM)
    out_shape = [jax.ShapeDtypeStruct(g.shape, F32) for g in grads for _ in range(4)]
    out_shape.append(jax.ShapeDtypeStruct((1, 128), F32))
    res = pl.pallas_call(
        body, name="small_update",
        in_specs=[vmem] * n_in, out_specs=[vmem] * len(out_shape), out_shape=out_shape,
        scratch_shapes=[pltpu.VMEM((N_DEV, VEC_ROWS, D), F32), pltpu.VMEM((N_DEV, MAT_ROWS, 128), F32),
                        pltpu.VMEM((VEC_ROWS, D), F32), pltpu.VMEM((MAT_ROWS, 128), F32),
                        pltpu.SemaphoreType.DMA((14,)), pltpu.SemaphoreType.DMA((14,))],
        compiler_params=pltpu.CompilerParams(vmem_limit_bytes=VMEM_LIMIT),
    )(*grads, loss_part, *state)
    return [res[4 * q:4 * q + 4] for q in range(n_par)], res[-1]


def kernel(x, positions, g_pre, w_in, b_qkv, ln_v_g, ln_v_b, w_spatial, b_spatial, attn_sinks, w_out, g_post, loss_target, m_g_pre, m_w_in, m_b_qkv, m_ln_v_g, m_ln_v_b, m_w_spatial, m_b_spatial, m_attn_sinks, m_w_out, m_g_post, v_g_pre, v_w_in, v_b_qkv, v_ln_v_g, v_ln_v_b, v_w_spatial, v_b_spatial, v_attn_sinks, v_w_out, v_g_post):
    S = x.shape[1]
    c = lax.axis_index("c")
    c_arr = jnp.reshape(c, (1,)).astype(jnp.int32)
    x2 = x[0]
    target = loss_target[0]
    pos = positions.reshape(S, 1)
    half = HD // 2
    inv_freq = ROPE_THETA ** (-jnp.arange(half, dtype=F32) * (2.0 / HD))
    invf = jnp.tile(inv_freq, 128 // half).reshape(1, 128)
    bias = jnp.concatenate([jnp.zeros((OFF_Q,), F32), b_qkv[0], jnp.zeros((D_IN - OFF_ZB,), F32)]).reshape(1, D_IN)
    b_s_col = b_spatial[0].reshape(NG, T, 1)
    sinks = jnp.repeat(attn_sinks[0], T).reshape(NQ * T, 1)

    wt_part = lax.dynamic_slice_in_dim(w_in[0].T.astype(BF16), c * W_IN_ROWS, W_IN_ROWS, axis=0)
    wo_part = lax.dynamic_slice_in_dim(w_out[0].astype(BF16), c * W_OUT_ROWS, W_OUT_ROWS, axis=0)

    h, wt = _prenorm_gather(x2, g_pre, wt_part)
    proj, wo = _inproj_gather(h, wt, bias, wo_part)
    ycat = _mid_fwd(proj, pos, invf, ln_v_g, ln_v_b, w_spatial[0], b_s_col, sinks)
    dy, dout, loss_part, dg_post = _outproj_loss(ycat, wo, x2, target, g_post)

    dycat = _dycat(dy, wo)
    dwo = _matmul_tn(ycat, dy, 512, "dw_out")
    po = _presum(c_arr, dwo, _to_sibling(dwo, W_OUT_ROWS, "to_sibling_out"), W_OUT_ROWS)
    dproj, dln_g, dln_b, dws, dbs, dsink, dbqkv, bo = _mid_bwd(
        proj, dycat, pos, invf, ln_v_g, ln_v_b, w_spatial[0], b_s_col, sinks, po)
    dwt = _matmul_tn(dproj, h, 768, "dw_in_t")
    pt = _presum(c_arr, dwt, _to_sibling(dwt, W_IN_ROWS, "to_sibling_in"), W_IN_ROWS)
    grad_x, dg_pre, bt = _dh_prenorm_bwd(dproj, wt, x2, dout, g_pre, pt)
    gt, g_w_out = _pair_halves(_sum_chips(c_arr, bt, "sum_chips_in"), _sum_chips(c_arr, bo, "sum_chips_out"))

    d_t, nm_t, nv_t = _adamw(w_in[0].T, gt, m_w_in[0].T, v_w_in[0].T, "adamw_w_in")
    g_w_in, d_w_in, nm_w_in, nv_w_in = gt.T, d_t.T, nm_t.T, nv_t.T
    d_w_out, nm_w_out, nv_w_out = _adamw(w_out[0], g_w_out, m_w_out[0], v_w_out[0], "adamw_w_out")

    state = {"g_pre": (g_pre, m_g_pre, v_g_pre), "g_post": (g_post, m_g_post, v_g_post),
             "b_qkv": (b_qkv, m_b_qkv, v_b_qkv), "ln_v_g": (ln_v_g, m_ln_v_g, v_ln_v_g),
             "ln_v_b": (ln_v_b, m_ln_v_b, v_ln_v_b), "attn_sinks": (attn_sinks, m_attn_sinks, v_attn_sinks),
             "w_spatial": tuple(a.reshape(NG * T, T) for a in (w_spatial, m_w_spatial, v_w_spatial)),
             "b_spatial": tuple(a.reshape(NG, T) for a in (b_spatial, m_b_spatial, v_b_spatial))}
    grads = {"g_pre": dg_pre, "g_post": dg_post, "b_qkv": dbqkv, "ln_v_g": dln_g, "ln_v_b": dln_b,
             "attn_sinks": dsink[:, 0].reshape(1, NQ), "w_spatial": dws.reshape(NG * T, T), "b_spatial": dbs.reshape(NG, T)}
    results, loss = _small_update([grads[n] for n, _ in VEC], loss_part, [grads[n] for n, _ in MAT],
                                  [state[n] for n, _ in VEC], [state[n] for n, _ in MAT])
    small = {n: [a.reshape(w.shape) for a in res]
             for (n, _), res, w in zip(VEC + MAT, results, [state[n][0] for n, _ in VEC + MAT])}
    small["w_spatial"] = [a.reshape(w_spatial.shape) for a in small["w_spatial"]]
    small["b_spatial"] = [a.reshape(b_spatial.shape) for a in small["b_spatial"]]
    big = {"w_in": [a[None] for a in (g_w_in, d_w_in, nm_w_in, nv_w_in)],
           "w_out": [a[None] for a in (g_w_out, d_w_out, nm_w_out, nv_w_out)]}
    order = ("g_pre", "w_in", "b_qkv", "ln_v_g", "ln_v_b", "w_spatial", "b_spatial", "attn_sinks", "w_out", "g_post")
    leaves = {**small, **big}
    return (loss[0, 0], grad_x[None], *[leaves[n][t] for t in range(4) for n in order])
```

```python
import jax
import jax.numpy as jnp
from jax import lax
from jax.experimental import pallas as pl
from jax.experimental.pallas import tpu as pltpu

F32 = jnp.float32
BF16 = jnp.bfloat16
MESH = pl.DeviceIdType.MESH

D = 2048
DG = 1024
T = 128
NG = 8
HD = 64
NQ = 16
D_IN = 5376
OFF_U, OFF_V, OFF_ZA, OFF_Q, OFF_K, OFF_VA, OFF_ZB = 0, 1024, 2048, 3072, 4096, 4224, 4352
D_QKV = 1280
EPS = 1e-6
ROPE_THETA = 10000.0
N_CHIPS = 4
N_DEV = 8
W_IN_ROWS = D_IN // N_DEV
W_OUT_ROWS = D // N_DEV

ADAM_LR, ADAM_B1, ADAM_B2, ADAM_EPS, ADAM_WD, ADAM_STEP = 0.001, 0.9, 0.999, 1e-08, 0.01, 10

VMEM_LIMIT = 56 * 1024 * 1024


def _tile(n, pref):
    return pref if n % pref == 0 else n


def _params(sem=None, vmem=VMEM_LIMIT):
    return pltpu.CompilerParams(dimension_semantics=sem, vmem_limit_bytes=vmem)


def _sigmoid(z):
    return 1.0 / (1.0 + jnp.exp(-z))


def _dot(a, b, dims):
    return lax.dot_general(a, b, (dims, ((), ())), preferred_element_type=F32)


NN = ((1,), (0,))
NT = ((1,), (1,))
TN = ((0,), (0,))


PROJ_TN = 768
N_PROJ_TILES = D_IN // PROJ_TN
PROJ_SCHEDULE = ((0, 2, 4, 1, 3, 6, 5, 5), (2, 0, 6, 1, 4, 3, 5, 4), (4, 6, 0, 5, 2, 1, 3, 4), (6, 4, 2, 5, 3, 0, 1, 5))


def _prenorm_inproj(sched, x, g, bias, wt_buf):
    S = x.shape[0]
    tp, tm = _tile(S, 512), _tile(S, 1024)
    n_pre, ns = S // tp, S // tm
    n_steps = n_pre + N_PROJ_TILES * ns
    pos_of = lambda i: jnp.maximum(i - n_pre, 0) // ns
    row_of = lambda i: jnp.maximum(i - n_pre, 0) % ns

    def body(sched_ref, x_ref, g_ref, b_ref, wt_in, h_ref, proj_ref, wt_ref, h_all, w_tile, send_sems, recv_sems,
             w_sems):
        del wt_in
        i = pl.program_id(0)
        x_, y_, c = _mesh_pos()
        me, sibling = (x_, y_, c), (x_, y_, 1 - c)
        chips = [(1 - x_, y_), (x_, 1 - y_), (1 - x_, 1 - y_)]

        def block(pos):
            px, py, pc = pos
            return wt_ref.at[pl.ds(pl.multiple_of((4 * px + 2 * py + pc) * W_IN_ROWS, 16), W_IN_ROWS), :]

        def copy(k, blk, to):
            return pltpu.make_async_remote_copy(
                src_ref=block(blk), dst_ref=block(blk), send_sem=send_sems.at[k], recv_sem=recv_sems.at[k],
                device_id=to, device_id_type=MESH)

        def own_sends():
            return [copy(0, me, sibling), copy(2, me, (*chips[1], c)), copy(1, me, (*chips[0], c)),
                    copy(3, me, (*chips[2], c))]

        def arrive(k):
            copy(k, (*chips[k - 1], c), me).wait_recv()
            copy(3 + k, (*chips[k - 1], c), sibling).start()
            copy(3 + k, (*chips[k - 1], 1 - c), me).wait_recv()

        def tile_load(p):
            slot = p % 2
            rows = wt_ref.at[pl.ds(pl.multiple_of(sched_ref[p] * PROJ_TN, 16), PROJ_TN), :]
            return pltpu.make_async_copy(rows, w_tile.at[slot], w_sems.at[slot])

        def prepare(p):
            p = jnp.asarray(p, jnp.int32)
            pl.when(p == 0)(lambda: copy(0, sibling, me).wait_recv())
            pl.when(p == 1)(lambda: arrive(2))
            pl.when(p == 2)(lambda: arrive(1))
            pl.when(p == sched_ref[N_PROJ_TILES])(lambda: arrive(3))
            tile_load(p).start()

        @pl.when(i == 0)
        def _():
            for cp in own_sends():
                cp.start()

        @pl.when(i < n_pre)
        def _():
            xv = x_ref[...]
            r = lax.rsqrt(jnp.mean(xv * xv, axis=-1, keepdims=True) + EPS)
            hv = (xv * r * g_ref[...]).astype(BF16)
            h_ref[...] = hv
            h_all[pl.ds(pl.multiple_of(i * tp, tp), tp), :] = hv

        pl.when(i == n_pre - 1)(lambda: prepare(0))

        @pl.when(i >= n_pre)
        def _():
            p, s = pos_of(i), row_of(i)
            pl.when(s == 0)(lambda: tile_load(p).wait())
            pl.when((s == ns - 1) & (p < N_PROJ_TILES - 1))(lambda: prepare(p + 1))
            hv = h_all[pl.ds(pl.multiple_of(s * tm, tm), tm), :]
            proj_ref[...] = _dot(hv, w_tile[p % 2], NT) + b_ref[...]

        @pl.when(i == n_steps - 1)
        def _():
            for cp in own_sends() + [copy(3 + k, (*chips[k - 1], c), sibling) for k in (1, 2, 3)]:
                cp.wait_send()

    return pl.pallas_call(
        body, name="prenorm_inproj",
        grid_spec=pltpu.PrefetchScalarGridSpec(
            num_scalar_prefetch=1, grid=(n_steps,),
            in_specs=[pl.BlockSpec((tp, D), lambda i, sc: (jnp.minimum(i, n_pre - 1), 0)),
                      pl.BlockSpec((1, D), lambda i, sc: (0, 0)),
                      pl.BlockSpec((1, PROJ_TN), lambda i, sc: (0, sc[pos_of(i)])),
                      HBM_SPEC],
            out_specs=[pl.BlockSpec((tp, D), lambda i, sc: (jnp.minimum(i, n_pre - 1), 0)),
                       pl.BlockSpec((tm, PROJ_TN), lambda i, sc: (row_of(i), sc[pos_of(i)])),
                       HBM_SPEC],
            scratch_shapes=[pltpu.VMEM((S, D), BF16), pltpu.VMEM((2, PROJ_TN, D), BF16),
                            pltpu.SemaphoreType.DMA((7,)), pltpu.SemaphoreType.DMA((7,)),
                            pltpu.SemaphoreType.DMA((2,))]),
        out_shape=[jax.ShapeDtypeStruct((S, D), BF16), jax.ShapeDtypeStruct((S, D_IN), F32),
                   jax.ShapeDtypeStruct((D_IN, D), BF16)],
        input_output_aliases={4: 2},
        compiler_params=_params(("arbitrary",)),
    )(sched, x, g, bias, wt_buf)


def _rope_tables(pos_ref, invf_ref):
    ang = pos_ref[...].astype(F32) * invf_ref[...]
    return jnp.cos(ang), jnp.sin(ang)


def _rot_half(xs, first_half):
    return jnp.where(first_half, -pltpu.roll(xs, 96, 1), pltpu.roll(xs, 32, 1))


GH = NQ // 2


MASKED = -1e30


def _attn_consts():
    lane = lax.broadcasted_iota(jnp.int32, (T, 128), 1)
    row = lax.broadcasted_iota(jnp.int32, (GH * T, T), 0) & (T - 1)
    on_diag_or_below = row >= lax.broadcasted_iota(jnp.int32, (GH * T, T), 1)
    return (lane & (HD - 1)) < (HD // 2), lane < HD, on_diag_or_below


def _stack_heads(slab_fn, grp, lo64):
    blocks = []
    for jj in range(GH // 2):
        s = slab_fn(GH // 2 * grp + jj)
        r = pltpu.roll(s, HD, 1)
        if grp == 0:
            blocks += [jnp.where(lo64, s, 0.0), jnp.where(lo64, r, 0.0)]
        else:
            blocks += [jnp.where(lo64, 0.0, r), jnp.where(lo64, 0.0, s)]
    return jnp.concatenate(blocks, axis=0)


def _unstack_heads(stacked, jj, grp, lo64):
    a = stacked[(2 * jj) * T:(2 * jj + 1) * T]
    b = stacked[(2 * jj + 1) * T:(2 * jj + 2) * T]
    if grp == 0:
        return jnp.where(lo64, a, pltpu.roll(b, HD, 1))
    return jnp.where(lo64, pltpu.roll(a, HD, 1), b)


def _layer_norm_stats(v):
    mu = jnp.mean(v, axis=-1, keepdims=True)
    xc = v - mu
    var = jnp.mean(xc * xc, axis=-1, keepdims=True)
    rs = lax.rsqrt(var + EPS)
    return xc * rs, rs


def _band_softmax(q_scaled, k_prev, k_cur, own, has_prev, sink):
    s_prev = _dot(q_scaled, k_prev, NT) + jnp.where(has_prev, 0.0, MASKED)
    s = jnp.where(own, _dot(q_scaled, k_cur, NT), s_prev)
    m = jnp.maximum(jnp.max(s, axis=-1, keepdims=True), sink)
    e = jnp.exp(s - m)
    es = jnp.exp(sink - m)
    inv = 1.0 / (jnp.sum(e, axis=-1, keepdims=True) + es)
    return e * inv, es * inv


def _unfold(p, own):
    return jnp.where(own, p, 0.0).astype(BF16), jnp.where(own, 0.0, p).astype(BF16)


def _mid_specs(nb, rev):
    blk = (lambda i: nb - 1 - i) if rev else (lambda i: i)
    prev = lambda i: jnp.maximum(blk(i) - 1, 0)
    return blk, prev, [
        pl.BlockSpec((T, D_IN), lambda i: (blk(i), 0)),
        pl.BlockSpec((T, 2 * T), lambda i: (prev(i), OFF_K // (2 * T))),
    ], [
        pl.BlockSpec((T, 1), lambda i: (blk(i), 0)),
        pl.BlockSpec((T, 1), lambda i: (prev(i), 0)),
        pl.BlockSpec((1, 128), lambda i: (0, 0)),
        pl.BlockSpec((1, DG), lambda i: (0, 0)),
        pl.BlockSpec((1, DG), lambda i: (0, 0)),
        pl.BlockSpec((NG, T, T), lambda i: (0, 0, 0)),
        pl.BlockSpec((NG, T, 1), lambda i: (0, 0, 0)),
        pl.BlockSpec((NQ * T, 1), lambda i: (0, 0)),
    ]


def _mid_fwd(proj, pos, invf, ln_g, ln_b, w_s, b_s, sinks, wo_buf):
    S = proj.shape[0]
    nb = S // T
    blk, _, proj_specs, par_specs = _mid_specs(nb, False)

    def body(p_ref, kvp_ref, pos_ref, posp_ref, invf_ref, lng_ref, lnb_ref, ws_ref, bs_ref, sink_ref, wo_in,
             y_ref, wo_ref, send_sems, recv_sems):
        del wo_in
        n = pl.program_id(0)
        first_half, lo64, own = _attn_consts()
        gather = _RowGather(wo_ref, W_OUT_ROWS, send_sems, recv_sems)
        pl.when(n == 0)(gather.start)
        tril = lax.broadcasted_iota(jnp.int32, (T, T), 0) >= lax.broadcasted_iota(jnp.int32, (T, T), 1)

        xhat, _ = _layer_norm_stats(p_ref[:, OFF_V:OFF_V + DG])
        vn = xhat * lng_ref[...] + lnb_ref[...]
        for g in range(NG):
            sl = slice(128 * g, 128 * g + 128)
            wm = jnp.where(tril, ws_ref[g], 0.0).astype(BF16)
            mixed = _dot(wm, vn[:, sl].astype(BF16), NN) + bs_ref[g]
            z = p_ref[:, OFF_ZA + 128 * g:OFF_ZA + 128 * g + 128]
            u = p_ref[:, OFF_U + 128 * g:OFF_U + 128 * g + 128]
            y_ref[:, sl] = (u * mixed * (z * _sigmoid(z))).astype(BF16)

        cosv, sinv = _rope_tables(pos_ref, invf_ref)
        cosp, sinp = _rope_tables(posp_ref, invf_ref)
        k_cur = p_ref[:, OFF_K:OFF_K + 128]
        k_prev = kvp_ref[:, 0:128]
        k_cur = (k_cur * cosv + _rot_half(k_cur, first_half) * sinv).astype(BF16)
        k_prev = (k_prev * cosp + _rot_half(k_prev, first_half) * sinp).astype(BF16)
        v_cur, v_prev = p_ref[:, OFF_VA:OFF_VA + 128].astype(BF16), kvp_ref[:, 128:256].astype(BF16)

        def q_slab(j):
            qraw = p_ref[:, OFF_Q + 128 * j:OFF_Q + 128 * j + 128]
            return (qraw * cosv + _rot_half(qraw, first_half) * sinv) * (HD ** -0.5)

        for grp in range(2):
            qs = _stack_heads(q_slab, grp, lo64).astype(BF16)
            p, _ = _band_softmax(qs, k_prev, k_cur, own, n > 0, sink_ref[GH * T * grp:GH * T * (grp + 1), :])
            p_cur, p_prev = _unfold(p, own)
            o_st = _dot(p_cur, v_cur, NN) + _dot(p_prev, v_prev, NN)
            for jj in range(GH // 2):
                c0 = 128 * (GH // 2 * grp + jj)
                zb = p_ref[:, OFF_ZB + c0:OFF_ZB + c0 + 128]
                o = _unstack_heads(o_st, jj, grp, lo64)
                y_ref[:, DG + c0:DG + c0 + 128] = (o * (zb * _sigmoid(zb))).astype(BF16)
        pl.when(n == nb - 1)(gather.finish)

    n_in = len(proj_specs) + len(par_specs)
    return pl.pallas_call(
        body, name="mid_fwd", grid=(nb,),
        in_specs=proj_specs + par_specs + [HBM_SPEC],
        out_specs=[pl.BlockSpec((T, D), lambda i: (blk(i), 0)), HBM_SPEC],
        out_shape=[jax.ShapeDtypeStruct((S, D), BF16), jax.ShapeDtypeStruct((D, D), BF16)],
        input_output_aliases={n_in: 1},
        scratch_shapes=GATHER_SEMS,
        compiler_params=_params(("arbitrary",)),
    )(proj, proj, pos, pos, invf, ln_g, ln_b, w_s, b_s, sinks, wo_buf)


def _mid_bwd(proj, dycat, pos, invf, ln_g, ln_b, w_s, b_s, sinks, po):
    S = proj.shape[0]
    nb = S // T
    blk, _, proj_specs, par_specs = _mid_specs(nb, True)
    const2 = lambda i: (0, 0)

    def body(p_ref, kvp_ref, dyc_ref, pos_ref, posp_ref, invf_ref, lng_ref, lnb_ref, ws_ref, bs_ref, sink_ref, po_ref,
             dp_ref, dlng_ref, dlnb_ref, dws_ref, dbs_ref, dsink_ref, dbqkv_ref, bo_ref,
             carry_ref, dvn_ref, send_sems, recv_sems, local_sem):
        i = pl.program_id(0)
        n = nb - 1 - i
        first_half, lo64, own = _attn_consts()
        tril = lax.broadcasted_iota(jnp.int32, (T, T), 0) >= lax.broadcasted_iota(jnp.int32, (T, T), 1)
        exchange =_OwnerExchange(po_ref, bo_ref, send_sems, recv_sems, local_sem)
        pl.when(i == 0)(exchange.start)

        @pl.when(i == 0)
        def _():
            dlng_ref[...] = jnp.zeros_like(dlng_ref)
            dlnb_ref[...] = jnp.zeros_like(dlnb_ref)
            dws_ref[...] = jnp.zeros_like(dws_ref)
            dbs_ref[...] = jnp.zeros_like(dbs_ref)
            dsink_ref[...] = jnp.zeros_like(dsink_ref)
            dbqkv_ref[...] = jnp.zeros_like(dbqkv_ref)
            carry_ref[...] = jnp.zeros_like(carry_ref)

        xhat, rs = _layer_norm_stats(p_ref[:, OFF_V:OFF_V + DG])
        lng = lng_ref[...]
        vn = xhat * lng + lnb_ref[...]
        for g in range(NG):
            sl = slice(128 * g, 128 * g + 128)
            wm = jnp.where(tril, ws_ref[g], 0.0).astype(BF16)
            vng = vn[:, sl].astype(BF16)
            mixed = _dot(wm, vng, NN) + bs_ref[g]
            z = p_ref[:, OFF_ZA + 128 * g:OFF_ZA + 128 * g + 128]
            u = p_ref[:, OFF_U + 128 * g:OFF_U + 128 * g + 128]
            dy = dyc_ref[:, sl]
            sg = _sigmoid(z)
            sa = z * sg
            dp_ref[:, OFF_U + 128 * g:OFF_U + 128 * g + 128] = (dy * mixed * sa).astype(BF16)
            dp_ref[:, OFF_ZA + 128 * g:OFF_ZA + 128 * g + 128] = (
                dy * u * mixed * (sg * (1.0 + z * (1.0 - sg)))).astype(BF16)
            dm = dy * u * sa
            dmb = dm.astype(BF16)
            dvn_ref[:, sl] = _dot(wm, dmb, TN)
            dws_ref[g] += jnp.where(tril, _dot(dmb, vng, NT), 0.0)
            dbs_ref[g] += jnp.sum(dm, axis=1, keepdims=True)
        dvn = dvn_ref[...]
        dlng_ref[...] += jnp.sum(dvn * xhat, axis=0, keepdims=True)
        dlnb_ref[...] += jnp.sum(dvn, axis=0, keepdims=True)
        dxh = dvn * lng
        dv_g = rs * (dxh - jnp.mean(dxh, axis=-1, keepdims=True)
                     - xhat * jnp.mean(dxh * xhat, axis=-1, keepdims=True))
        dp_ref[:, OFF_V:OFF_V + DG] = dv_g.astype(BF16)

        cosv, sinv = _rope_tables(pos_ref, invf_ref)
        cosp, sinp = _rope_tables(posp_ref, invf_ref)
        k_cur = p_ref[:, OFF_K:OFF_K + 128]
        k_prev = kvp_ref[:, 0:128]
        k_cur = (k_cur * cosv + _rot_half(k_cur, first_half) * sinv).astype(BF16)
        k_prev = (k_prev * cosp + _rot_half(k_prev, first_half) * sinp).astype(BF16)
        v_cur, v_prev = p_ref[:, OFF_VA:OFF_VA + 128].astype(BF16), kvp_ref[:, 128:256].astype(BF16)

        def q_slab(j):
            qraw = p_ref[:, OFF_Q + 128 * j:OFF_Q + 128 * j + 128]
            return (qraw * cosv + _rot_half(qraw, first_half) * sinv) * (HD ** -0.5)

        def do_slab(j):
            zb = p_ref[:, OFF_ZB + 128 * j:OFF_ZB + 128 * j + 128]
            return dyc_ref[:, DG + 128 * j:DG + 128 * j + 128] * (zb * _sigmoid(zb))

        dk_cur, dv_cur = carry_ref[:, 0:128], carry_ref[:, 128:256]
        dk_prev, dv_prev = jnp.zeros((T, 128), F32), jnp.zeros((T, 128), F32)
        for grp in range(2):
            qs = _stack_heads(q_slab, grp, lo64).astype(BF16)
            d_o = _stack_heads(do_slab, grp, lo64)
            dob = d_o.astype(BF16)
            p, ps = _band_softmax(qs, k_prev, k_cur, own, n > 0, sink_ref[GH * T * grp:GH * T * (grp + 1), :])
            p_cur, p_prev = _unfold(p, own)
            o_st = _dot(p_cur, v_cur, NN) + _dot(p_prev, v_prev, NN)
            delta = jnp.sum(d_o * o_st, axis=-1, keepdims=True)
            dp_fold = jnp.where(own, _dot(dob, v_cur, NT), _dot(dob, v_prev, NT))
            ds_cur, ds_prev = _unfold(p * (dp_fold - delta), own)
            dq_st = (_dot(ds_cur, k_cur, NN) + _dot(ds_prev, k_prev, NN)) * (HD ** -0.5)
            dk_cur, dk_prev = dk_cur + _dot(ds_cur, qs, TN), dk_prev + _dot(ds_prev, qs, TN)
            dv_cur, dv_prev = dv_cur + _dot(p_cur, dob, TN), dv_prev + _dot(p_prev, dob, TN)
            dsink_rows = ps * delta
            for hh in range(GH):
                h = GH * grp + hh
                dsink_ref[h:h + 1, :] += jnp.broadcast_to(
                    -jnp.sum(dsink_rows[hh * T:(hh + 1) * T], axis=0, keepdims=True), (1, 128))
            for jj in range(GH // 2):
                c0 = 128 * (GH // 2 * grp + jj)
                zb = p_ref[:, OFF_ZB + c0:OFF_ZB + c0 + 128]
                sg = _sigmoid(zb)
                o = _unstack_heads(o_st, jj, grp, lo64)
                dp_ref[:, OFF_ZB + c0:OFF_ZB + c0 + 128] = (
                    dyc_ref[:, DG + c0:DG + c0 + 128] * o * (sg * (1.0 + zb * (1.0 - sg)))).astype(BF16)
                dq = _unstack_heads(dq_st, jj, grp, lo64)
                dq_pre = dq * cosv - _rot_half(dq, first_half) * sinv
                dp_ref[:, OFF_Q + c0:OFF_Q + c0 + 128] = dq_pre.astype(BF16)
                dbqkv_ref[:, c0:c0 + 128] += jnp.sum(dq_pre, axis=0, keepdims=True)
        carry_ref[:, 0:128] = dk_prev
        carry_ref[:, 128:256] = dv_prev
        dk_pre =dk_cur * cosv - _rot_half(dk_cur, first_half) * sinv
        dp_ref[:, OFF_K:OFF_K + 128] = dk_pre.astype(BF16)
        dp_ref[:, OFF_VA:OFF_VA + 128] = dv_cur.astype(BF16)
        dbqkv_ref[:, 1024:1152] += jnp.sum(dk_pre, axis=0, keepdims=True)
        dbqkv_ref[:, 1152:1280] += jnp.sum(dv_cur, axis=0, keepdims=True)
        pl.when(i == nb - 1)(exchange.finish)

    return pl.pallas_call(
        body, name="mid_bwd", grid=(nb,),
        in_specs=proj_specs + [pl.BlockSpec((T, D), lambda i: (blk(i), 0))] + par_specs + [HBM_SPEC],
        out_specs=[pl.BlockSpec((T, D_IN), lambda i: (blk(i), 0)),
                   pl.BlockSpec((1, DG), const2), pl.BlockSpec((1, DG), const2),
                   pl.BlockSpec((NG, T, T), lambda i: (0, 0, 0)), pl.BlockSpec((NG, T, 1), lambda i: (0, 0, 0)),
                   pl.BlockSpec((NQ, 128), const2), pl.BlockSpec((1, D_QKV), const2), HBM_SPEC],
        out_shape=[jax.ShapeDtypeStruct((S, D_IN), BF16),
                   jax.ShapeDtypeStruct((1, DG), F32), jax.ShapeDtypeStruct((1, DG), F32),
                   jax.ShapeDtypeStruct((NG, T, T), F32), jax.ShapeDtypeStruct((NG, T, 1), F32),
                   jax.ShapeDtypeStruct((NQ, 128), F32), jax.ShapeDtypeStruct((1, D_QKV), F32),
                   jax.ShapeDtypeStruct(po.shape, BF16)],
        scratch_shapes=[pltpu.VMEM((T, 2 * T), F32), pltpu.VMEM((T, DG), F32)] + OWNER_SEMS,
        compiler_params=_params(("arbitrary",)),
    )(proj, proj, dycat, pos, pos, invf, ln_g, ln_b, w_s, b_s, sinks, po)


def _outproj_loss(ycat, wo, x, target, g_post):
    S = ycat.shape[0]
    tm = _tile(S, 256)
    nt = S // tm
    const2 = lambda i: (0, 0)

    def body(yc_ref, w_ref, x_ref, t_ref, g_ref, dy_ref, dout_ref, loss_ref, dg_ref, lacc_ref):
        i = pl.program_id(0)

        @pl.when(i == 0)
        def _():
            dg_ref[...] = jnp.zeros_like(dg_ref)
            lacc_ref[...] = jnp.zeros_like(lacc_ref)

        y = _dot(yc_ref[...], w_ref[...], NN)
        r = lax.rsqrt(jnp.mean(y * y, axis=-1, keepdims=True) + EPS)
        yh = y * r
        g = g_ref[...]
        diff = x_ref[...] + yh * g - t_ref[...]
        lacc_ref[...] += jnp.sum(diff * diff, axis=0, keepdims=True)
        dout = diff * (1.0 / D)
        dout_ref[...] = dout
        dg_ref[...] += jnp.sum(dout * yh, axis=0, keepdims=True)
        dyh = dout * g
        dy_ref[...] = (r * (dyh - yh * jnp.mean(dyh * yh, axis=-1, keepdims=True))).astype(BF16)

        @pl.when(i == nt - 1)
        def _():
            loss_ref[...] = jnp.broadcast_to(jnp.sum(lacc_ref[...], axis=1, keepdims=True) * (0.5 / D), (1, 128))

    row = lambda i: (i, 0)
    return pl.pallas_call(
        body, name="outproj_loss", grid=(nt,),
        in_specs=[pl.BlockSpec((tm, D), row), pl.BlockSpec((D, D), const2), pl.BlockSpec((tm, D), row),
                  pl.BlockSpec((tm, D), row), pl.BlockSpec((1, D), const2)],
        out_specs=[pl.BlockSpec((tm, D), row), pl.BlockSpec((tm, D), row), pl.BlockSpec((1, 128), const2),
                   pl.BlockSpec((1, D), const2)],
        out_shape=[jax.ShapeDtypeStruct((S, D), BF16), jax.ShapeDtypeStruct((S, D), F32),
                   jax.ShapeDtypeStruct((1, 128), F32), jax.ShapeDtypeStruct((1, D), F32)],
        scratch_shapes=[pltpu.VMEM((1, D), F32)],
        compiler_params=_params(("arbitrary",)),
    )(ycat, wo, x, target, g_post)


def _dycat(dy, wo):
    S = dy.shape[0]
    tm = _tile(S, 512)

    def body(dy_ref, w_ref, o_ref):
        o_ref[...] = _dot(dy_ref[...], w_ref[...], NT)

    return pl.pallas_call(
        body, name="dycat", grid=(S // tm,),
        in_specs=[pl.BlockSpec((tm, D), lambda i: (i, 0)), pl.BlockSpec((D, D), lambda i: (0, 0))],
        out_specs=pl.BlockSpec((tm, D), lambda i: (i, 0)),
        out_shape=jax.ShapeDtypeStruct((S, D), F32),
        compiler_params=_params(("parallel",)),
    )(dy, wo)


def _matmul_tn(a, b, tm, name):
    K, M = a.shape
    N = b.shape[1]
    tk = _tile(K, 1024)
    nk = K // tk

    def body(a_ref, b_ref, o_ref, acc_ref):
        k = pl.program_id(1)

        @pl.when(k == 0)
        def _():
            acc_ref[...] = jnp.zeros_like(acc_ref)

        acc_ref[...] += _dot(a_ref[...], b_ref[...], TN)

        @pl.when(k == nk - 1)
        def _():
            o_ref[...] = acc_ref[...].astype(BF16)

    return pl.pallas_call(
        body, name=name, grid=(M // tm, nk),
        in_specs=[pl.BlockSpec((tk, tm), lambda i, k: (k, i)), pl.BlockSpec((tk, N), lambda i, k: (k, 0))],
        out_specs=pl.BlockSpec((tm, N), lambda i, k: (i, 0)),
        out_shape=jax.ShapeDtypeStruct((M, N), BF16),
        scratch_shapes=[pltpu.VMEM((tm, N), F32)],
        compiler_params=_params(("parallel", "arbitrary")),
    )(a, b)


def _dh_prenorm_bwd(dproj, wt, x, dout, g_pre, pt):
    S = x.shape[0]
    tm, tk = _tile(S, 512), 768
    ni, nk = S // tm, D_IN // tk

    def body(dp_ref, w_ref, x_ref, dout_ref, g_ref, pt_ref, gx_ref, dg_ref, bt_ref, acc_ref,
             send_sems, recv_sems, local_sem):
        i, k = pl.program_id(0), pl.program_id(1)
        exchange = _OwnerExchange(pt_ref, bt_ref, send_sems, recv_sems, local_sem)
        pl.when((i == 0) & (k == 0))(exchange.start)

        @pl.when((i == 0) & (k == 0))
        def _():
            dg_ref[...] = jnp.zeros_like(dg_ref)

        @pl.when(k == 0)
        def _():
            acc_ref[...] = jnp.zeros_like(acc_ref)

        acc_ref[...] += _dot(dp_ref[...], w_ref[...], NN)

        @pl.when(k == nk - 1)
        def _():
            dh = acc_ref[...]
            xv = x_ref[...]
            r = lax.rsqrt(jnp.mean(xv * xv, axis=-1, keepdims=True) + EPS)
            xh = xv * r
            dg_ref[...] += jnp.sum(dh * xh, axis=0, keepdims=True)
            dxh = dh * g_ref[...]
            gx_ref[...] = dout_ref[...] + r * (dxh - xh * jnp.mean(dxh * xh, axis=-1, keepdims=True))

        pl.when((i == ni - 1) & (k == nk - 1))(exchange.finish)

    row = lambda i, k: (i, 0)
    return pl.pallas_call(
        body, name="dh_prenorm_bwd", grid=(ni, nk),
        in_specs=[pl.BlockSpec((tm, tk), lambda i, k: (i, k)), pl.BlockSpec((tk, D), lambda i, k: (k, 0)),
                  pl.BlockSpec((tm, D), row), pl.BlockSpec((tm, D), row), pl.BlockSpec((1, D), lambda i, k: (0, 0)),
                  HBM_SPEC],
        out_specs=[pl.BlockSpec((tm, D), row), pl.BlockSpec((1, D), lambda i, k: (0, 0)), HBM_SPEC],
        out_shape=[jax.ShapeDtypeStruct((S, D), F32), jax.ShapeDtypeStruct((1, D), F32),
                   jax.ShapeDtypeStruct(pt.shape, BF16)],
        scratch_shapes=[pltpu.VMEM((tm, D), F32)] + OWNER_SEMS,
        compiler_params=_params(("arbitrary", "arbitrary")),
    )(dproj, wt, x, dout, g_pre, pt)


def _presum(c_arr, own, recv, half_rows):
    n_cols = own.shape[-1]
    own4 = own.reshape(N_CHIPS, 2, half_rows, n_cols)

    def body(c_ref, own_ref, recv_ref, o_ref):
        o_ref[...] = (own_ref[...].astype(F32) + recv_ref[...].astype(F32)).astype(BF16)

    return pl.pallas_call(
        body, name="presum_%d" % half_rows,
        grid_spec=pltpu.PrefetchScalarGridSpec(
            num_scalar_prefetch=1, grid=(N_CHIPS,),
            in_specs=[pl.BlockSpec((None, None, half_rows, n_cols), lambda j, c: (j, c[0], 0, 0)),
                      pl.BlockSpec((None, half_rows, n_cols), lambda j, c: (j, 0, 0))],
            out_specs=pl.BlockSpec((None, half_rows, n_cols), lambda j, c: (j, 0, 0))),
        out_shape=jax.ShapeDtypeStruct((N_CHIPS, half_rows, n_cols), BF16),
        compiler_params=_params(("parallel",)),
    )(c_arr, own4, recv)


def _sum_chips(c_arr, parts, name):
    _, rows, n_cols = parts.shape
    nt = 2
    tr = rows // nt

    def body(c_ref, p_ref, o_ref):
        o_ref[...] = ((p_ref[0].astype(F32) + p_ref[1].astype(F32)) + p_ref[2].astype(F32)) + p_ref[3].astype(F32)

    return pl.pallas_call(
        body, name=name,
        grid_spec=pltpu.PrefetchScalarGridSpec(
            num_scalar_prefetch=1, grid=(nt,),
            in_specs=[pl.BlockSpec((N_CHIPS, tr, n_cols), lambda i, c: (0, i, 0))],
            out_specs=pl.BlockSpec((tr, n_cols), lambda i, c: (c[0] * nt + i, 0))),
        out_shape=jax.ShapeDtypeStruct((2 * rows, n_cols), F32),
        compiler_params=_params(("parallel",)),
    )(c_arr, parts)


def _adamw_math(w, g, m, v):
    mn = ADAM_B1 * m + (1.0 - ADAM_B1) * g
    vn = ADAM_B2 * v + (1.0 - ADAM_B2) * (g * g)
    m_hat = mn / (1.0 - ADAM_B1 ** ADAM_STEP)
    v_hat = vn / (1.0 - ADAM_B2 ** ADAM_STEP)
    return -ADAM_LR * (m_hat / (jnp.sqrt(v_hat) + ADAM_EPS) + ADAM_WD * w), mn, vn


def _adamw(w, g, m, v, name):
    R, C = w.shape
    tr = next((t for t in (256, 192, 128) if R % t == 0), R)

    def body(w_ref, g_ref, m_ref, v_ref, d_ref, mo_ref, vo_ref):
        d_ref[...], mo_ref[...], vo_ref[...] = _adamw_math(w_ref[...], g_ref[...], m_ref[...], v_ref[...])

    spec = pl.BlockSpec((tr, C), lambda i: (i, 0))
    shp = jax.ShapeDtypeStruct((R, C), F32)
    return pl.pallas_call(
        body, name=name, grid=(R // tr,), in_specs=[spec] * 4, out_specs=[spec] * 3, out_shape=[shp] * 3,
        compiler_params=_params(("parallel",)),
    )(w, g, m, v)


HBM_SPEC = pl.BlockSpec(memory_space=pltpu.HBM)
GATHER_SEMS = [pltpu.SemaphoreType.DMA((7,)), pltpu.SemaphoreType.DMA((7,))]
OWNER_SEMS = [pltpu.SemaphoreType.DMA((3,)), pltpu.SemaphoreType.DMA((3,)), pltpu.SemaphoreType.DMA(())]


def _mesh_pos():
    return lax.axis_index("x"), lax.axis_index("y"), lax.axis_index("c")


class _RowGather:
    def __init__(self, full_ref, rows, send_sems, recv_sems):
        self.full, self.rows, self.send, self.recv = full_ref, rows, send_sems, recv_sems
        x, y, c = _mesh_pos()
        self.c, self.me, self.sibling = c, (x, y, c), (x, y, 1 - c)
        self.chips = [(1 - x, y), (x, 1 - y), (1 - x, 1 - y)]

    def _block(self, pos):
        px, py, pc = pos
        return self.full.at[pl.ds(pl.multiple_of((4 * px + 2 * py + pc) * self.rows, 16), self.rows), :]

    def _copy(self, k, blk, to):
        return pltpu.make_async_remote_copy(
            src_ref=self._block(blk), dst_ref=self._block(blk),
            send_sem=self.send.at[k], recv_sem=self.recv.at[k], device_id=to, device_id_type=MESH)

    def _first(self):
        return [self._copy(0, self.me, self.sibling)] + [
            self._copy(1 + j, self.me, (*chip, self.c)) for j, chip in enumerate(self.chips)]

    def start(self):
        for cp in self._first():
            cp.start()

    def finish(self):
        passed = [self._copy(4 + j, (*chip, self.c), self.sibling) for j, chip in enumerate(self.chips)]
        for j, chip in enumerate(self.chips):
            self._copy(1 + j, (*chip, self.c), self.me).wait_recv()
            passed[j].start()
        self._copy(0, self.sibling, self.me).wait_recv()
        for j, chip in enumerate(self.chips):
            self._copy(4 + j, (*chip, 1 - self.c), self.me).wait_recv()
        for cp in self._first() + passed:
            cp.wait_send()


class _OwnerExchange:
    def __init__(self, src_ref, dst_ref, send_sems, recv_sems, local_sem):
        self.src, self.dst, self.send, self.recv, self.local = src_ref, dst_ref, send_sems, recv_sems, local_sem
        x, y, c = _mesh_pos()
        self.c, self.my_chip = c, 2 * x + y
        self.peers = [(1 - x, y), (x, 1 - y), (1 - x, 1 - y)]

    def _copies(self):
        local = pltpu.make_async_copy(self.src.at[self.my_chip], self.dst.at[self.my_chip], self.local)
        remote = [pltpu.make_async_remote_copy(
            src_ref=self.src.at[2 * px + py], dst_ref=self.dst.at[self.my_chip],
            send_sem=self.send.at[k], recv_sem=self.recv.at[k], device_id=(px, py, self.c), device_id_type=MESH)
            for k, (px, py) in enumerate(self.peers)]
        return local, remote

    def start(self):
        local, remote = self._copies()
        local.start()
        for cp in remote:
            cp.start()

    def finish(self):
        local, remote = self._copies()
        for cp in remote:
            cp.wait_recv()
        for cp in remote:
            cp.wait_send()
        local.wait()


def _to_sibling(dw, rows, name):
    def body(dw_ref, r_ref, send_sems, recv_sems):
        x, y, c = _mesh_pos()
        copies = [pltpu.make_async_remote_copy(
            src_ref=dw_ref.at[pl.ds(pl.multiple_of((2 * j + (1 - c)) * rows, 16), rows), :], dst_ref=r_ref.at[j],
            send_sem=send_sems.at[j], recv_sem=recv_sems.at[j], device_id=(x, y, 1 - c), device_id_type=MESH)
            for j in range(N_CHIPS)]
        for cp in copies:
            cp.start()
        for cp in copies:
            cp.wait_recv()
        for cp in copies:
            cp.wait_send()

    return pl.pallas_call(
        body, name=name, in_specs=[HBM_SPEC], out_specs=HBM_SPEC,
        out_shape=jax.ShapeDtypeStruct((N_CHIPS, rows, dw.shape[1]), BF16),
        scratch_shapes=[pltpu.SemaphoreType.DMA((N_CHIPS,)), pltpu.SemaphoreType.DMA((N_CHIPS,))],
    )(dw)


PAIR_CHUNKS = 4


def _pair_halves(gt, go):
    def body(gt_in, go_in, gt_ref, go_ref, send_sems, recv_sems):
        del gt_in, go_in
        x, y, c = _mesh_pos()
        copies = []
        for a, (ref, rows) in enumerate(((gt_ref, W_IN_ROWS), (go_ref, W_OUT_ROWS))):
            ch = rows // PAIR_CHUNKS
            for q in range(PAIR_CHUNKS):
                part = ref.at[pl.ds(pl.multiple_of(c * rows + q * ch, 8), ch), :]
                copies.append(pltpu.make_async_remote_copy(
                    src_ref=part, dst_ref=part, send_sem=send_sems.at[PAIR_CHUNKS * a + q],
                    recv_sem=recv_sems.at[PAIR_CHUNKS * a + q], device_id=(x, y, 1 - c), device_id_type=MESH))
        for cp in copies:
            cp.start()
        for cp in copies:
            cp.wait_recv()
        for cp in copies:
            cp.wait_send()

    return pl.pallas_call(
        body, name="pair_halves",
        in_specs=[HBM_SPEC, HBM_SPEC], out_specs=[HBM_SPEC, HBM_SPEC],
        out_shape=[jax.ShapeDtypeStruct(gt.shape, F32), jax.ShapeDtypeStruct(go.shape, F32)],
        input_output_aliases={0: 0, 1: 1},
        scratch_shapes=[pltpu.SemaphoreType.DMA((2 * PAIR_CHUNKS,)), pltpu.SemaphoreType.DMA((2 * PAIR_CHUNKS,))],
    )(gt, go)


VEC = (("g_pre", 2048), ("g_post", 2048), ("b_qkv", 1280), ("ln_v_g", 1024), ("ln_v_b", 1024), ("attn_sinks", 16))
VEC_ROWS = 8
LOSS_ROW = len(VEC)
MAT = (("w_spatial", NG * T), ("b_spatial", NG))
MAT_ROWS = sum(r for _, r in MAT)


def _small_update(vec_grads, loss_part, mat_grads, vec_state, mat_state):
    n_vec, n_mat = len(VEC), len(MAT)
    n_par = n_vec + n_mat
    n_in = n_par + 1 + 3 * n_par

    def body(*refs):
        g_in, loss_in = refs[:n_par], refs[n_par]
        st_in = refs[n_par + 1:n_in]
        outs, loss_out = refs[n_in:n_in + 4 * n_par], refs[n_in + 4 * n_par]
        vbuf, mbuf, tot_v, tot_m, send_sems, recv_sems = refs[n_in + 4 * n_par + 1:]
        x, y, c = _mesh_pos()
        me, sibling = (x, y, c), (x, y, 1 - c)
        chips = [(1 - x, y), (x, 1 - y), (1 - x, 1 - y)]
        my_id = 4 * x + 2 * y + c

        vbuf[my_id] = jnp.zeros((VEC_ROWS, D), F32)
        for r, (_, n) in enumerate(VEC):
            vbuf[my_id, r:r + 1, 0:n] = g_in[r][...]
        vbuf[my_id, LOSS_ROW:LOSS_ROW + 1, 0:128] = loss_in[...]
        r0 = 0
        for q, (_, rows) in enumerate(MAT):
            mbuf[my_id, r0:r0 + rows, :] = g_in[n_vec + q][...]
            r0 += rows

        def copy(a, k, blk, to):
            buf = (vbuf, mbuf)[a]
            px, py, pc = blk
            slot = buf.at[4 * px + 2 * py + pc]
            return pltpu.make_async_remote_copy(
                src_ref=slot, dst_ref=slot, send_sem=send_sems.at[7 * a + k], recv_sem=recv_sems.at[7 * a + k],
                device_id=to, device_id_type=MESH)

        first = [copy(a, 0, me, sibling) for a in range(2)]
        first += [copy(a, 1 + j, me, (*chip, c)) for a in range(2) for j, chip in enumerate(chips)]
        for cp in first:
            cp.start()
        passed = []
        for a in range(2):
            for j, chip in enumerate(chips):
                copy(a, 1 + j, (*chip, c), me).wait_recv()
                passed.append(copy(a, 4 + j, (*chip, c), sibling))
                passed[-1].start()
        for a in range(2):
            copy(a, 0, sibling, me).wait_recv()
            for j, chip in enumerate(chips):
                copy(a, 4 + j, (*chip, 1 - c), me).wait_recv()
        for cp in first + passed:
            cp.wait_send()

        tv, tm = vbuf[0], mbuf[0]
        for d in range(1, N_DEV):
            tv, tm = tv + vbuf[d], tm + mbuf[d]
        tot_v[...] = tv
        tot_m[...] = tm
        loss_out[...] = tot_v[LOSS_ROW:LOSS_ROW + 1, 0:128]
        r0 = 0
        for q in range(n_par):
            if q < n_vec:
                g = tot_v[q:q + 1, 0:VEC[q][1]]
            else:
                rows = MAT[q - n_vec][1]
                g = tot_m[r0:r0 + rows, :]
                r0 += rows
            w, m, v = (st_in[3 * q + t][...] for t in range(3))
            outs[4 * q][...] = g
            outs[4 * q + 1][...], outs[4 * q + 2][...], outs[4 * q + 3][...] = _adamw_math(w, g, m, v)

    grads = list(vec_grads) + list(mat_grads)
    state = [a for wmv in list(vec_state) + list(mat_state) for a in wmv]
    vmem = pl.BlockSpec(memory_space=pltpu.VMEM)
    out_shape = [jax.ShapeDtypeStruct(g.shape, F32) for g in grads for _ in range(4)]
    out_shape.append(jax.ShapeDtypeStruct((1, 128), F32))
    res = pl.pallas_call(
        body, name="small_update",
        in_specs=[vmem] * n_in, out_specs=[vmem] * len(out_shape), out_shape=out_shape,
        scratch_shapes=[pltpu.VMEM((N_DEV, VEC_ROWS, D), F32), pltpu.VMEM((N_DEV, MAT_ROWS, 128), F32),
                        pltpu.VMEM((VEC_ROWS, D), F32), pltpu.VMEM((MAT_ROWS, 128), F32),
                        pltpu.SemaphoreType.DMA((14,)), pltpu.SemaphoreType.DMA((14,))],
        compiler_params=pltpu.CompilerParams(vmem_limit_bytes=VMEM_LIMIT),
    )(*grads, loss_part, *state)
    return [res[4 * q:4 * q + 4] for q in range(n_par)], res[-1]


def kernel(x, positions, g_pre, w_in, b_qkv, ln_v_g, ln_v_b, w_spatial, b_spatial, attn_sinks, w_out, g_post, loss_target, m_g_pre, m_w_in, m_b_qkv, m_ln_v_g, m_ln_v_b, m_w_spatial, m_b_spatial, m_attn_sinks, m_w_out, m_g_post, v_g_pre, v_w_in, v_b_qkv, v_ln_v_g, v_ln_v_b, v_w_spatial, v_b_spatial, v_attn_sinks, v_w_out, v_g_post):
    S = x.shape[1]
    c = lax.axis_index("c")
    c_arr = jnp.reshape(c, (1,)).astype(jnp.int32)
    x2 = x[0]
    target = loss_target[0]
    pos = positions.reshape(S, 1)
    half = HD // 2
    inv_freq = ROPE_THETA ** (-jnp.arange(half, dtype=F32) * (2.0 / HD))
    invf = jnp.tile(inv_freq, 128 // half).reshape(1, 128)
    bias = jnp.concatenate([jnp.zeros((OFF_Q,), F32), b_qkv[0], jnp.zeros((D_IN - OFF_ZB,), F32)]).reshape(1, D_IN)
    b_s_col = b_spatial[0].reshape(NG, T, 1)
    sinks = jnp.repeat(attn_sinks[0], T).reshape(NQ * T, 1)

    chip = 2 * lax.axis_index("x") + lax.axis_index("y")
    dev = 2 * chip + c
    wt_part = lax.dynamic_slice_in_dim(w_in[0].T.astype(BF16), c * W_IN_ROWS, W_IN_ROWS, axis=0)
    wo_part = lax.dynamic_slice_in_dim(w_out[0].astype(BF16), c * W_OUT_ROWS, W_OUT_ROWS, axis=0)
    wt_buf = lax.dynamic_update_slice_in_dim(lax.empty((D_IN, D), BF16), wt_part, dev * W_IN_ROWS, axis=0)
    wo_buf = lax.dynamic_update_slice_in_dim(lax.empty((D, D), BF16), wo_part, dev * W_OUT_ROWS, axis=0)
    sched = jnp.asarray(PROJ_SCHEDULE, jnp.int32)[chip]

    h, proj, wt = _prenorm_inproj(sched, x2, g_pre, bias, wt_buf)
    ycat, wo = _mid_fwd(proj, pos, invf, ln_v_g, ln_v_b, w_spatial[0], b_s_col, sinks, wo_buf)
    dy, dout, loss_part, dg_post = _outproj_loss(ycat, wo, x2, target, g_post)

    dycat = _dycat(dy, wo)
    dwo = _matmul_tn(ycat, dy, 512, "dw_out")
    po = _presum(c_arr, dwo, _to_sibling(dwo, W_OUT_ROWS, "to_sibling_out"), W_OUT_ROWS)
    dproj, dln_g, dln_b, dws, dbs, dsink, dbqkv, bo = _mid_bwd(
        proj, dycat, pos, invf, ln_v_g, ln_v_b, w_spatial[0], b_s_col, sinks, po)
    dwt = _matmul_tn(dproj, h, 768, "dw_in_t")
    pt = _presum(c_arr, dwt, _to_sibling(dwt, W_IN_ROWS, "to_sibling_in"), W_IN_ROWS)
    grad_x, dg_pre, bt = _dh_prenorm_bwd(dproj, wt, x2, dout, g_pre, pt)
    gt, g_w_out = _pair_halves(_sum_chips(c_arr, bt, "sum_chips_in"), _sum_chips(c_arr, bo, "sum_chips_out"))

    d_t, nm_t, nv_t = _adamw(w_in[0].T, gt, m_w_in[0].T, v_w_in[0].T, "adamw_w_in")
    g_w_in, d_w_in, nm_w_in, nv_w_in = gt.T, d_t.T, nm_t.T, nv_t.T
    d_w_out, nm_w_out, nv_w_out = _adamw(w_out[0], g_w_out, m_w_out[0], v_w_out[0], "adamw_w_out")

    state = {"g_pre": (g_pre, m_g_pre, v_g_pre), "g_post": (g_post, m_g_post, v_g_post),
             "b_qkv": (b_qkv, m_b_qkv, v_b_qkv), "ln_v_g": (ln_v_g, m_ln_v_g, v_ln_v_g),
             "ln_v_b": (ln_v_b, m_ln_v_b, v_ln_v_b), "attn_sinks": (attn_sinks, m_attn_sinks, v_attn_sinks),
             "w_spatial": tuple(a.reshape(NG * T, T) for a in (w_spatial, m_w_spatial, v_w_spatial)),
             "b_spatial": tuple(a.reshape(NG, T) for a in (b_spatial, m_b_spatial, v_b_spatial))}
    grads = {"g_pre": dg_pre, "g_post": dg_post, "b_qkv": dbqkv, "ln_v_g": dln_g, "ln_v_b": dln_b,
             "attn_sinks": dsink[:, 0].reshape(1, NQ), "w_spatial": dws.reshape(NG * T, T), "b_spatial": dbs.reshape(NG, T)}
    results, loss = _small_update([grads[n] for n, _ in VEC], loss_part, [grads[n] for n, _ in MAT],
                                  [state[n] for n, _ in VEC], [state[n] for n, _ in MAT])
    small = {n: [a.reshape(w.shape) for a in res]
             for (n, _), res, w in zip(VEC + MAT, results, [state[n][0] for n, _ in VEC + MAT])}
    small["w_spatial"] = [a.reshape(w_spatial.shape) for a in small["w_spatial"]]
    small["b_spatial"] = [a.reshape(b_spatial.shape) for a in small["b_spatial"]]
    big = {"w_in": [a[None] for a in (g_w_in, d_w_in, nm_w_in, nv_w_in)],
           "w_out": [a[None] for a in (g_w_out, d_w_out, nm_w_out, nv_w_out)]}
    order = ("g_pre", "w_in", "b_qkv", "ln_v_g", "ln_v_b", "w_spatial", "b_spatial", "attn_sinks", "w_out", "g_post")
    leaves = {**small, **big}
    return (loss[0, 0], grad_x[None], *[leaves[n][t] for t in range(4) for n in order])
```

```python
import jax
import jax.numpy as jnp
from jax import lax
from jax.experimental import pallas as pl
from jax.experimental.pallas import tpu as pltpu

F32 = jnp.float32
BF16 = jnp.bfloat16
MESH = pl.DeviceIdType.MESH

D = 2048
DG = 1024
T = 128
NG = 8
HD = 64
NQ = 16
D_IN = 5376
OFF_U, OFF_V, OFF_ZA, OFF_Q, OFF_K, OFF_VA, OFF_ZB = 0, 1024, 2048, 3072, 4096, 4224, 4352
D_QKV = 1280
EPS = 1e-6
ROPE_THETA = 10000.0
N_CHIPS = 4
N_DEV = 8
W_IN_ROWS = D_IN // N_DEV
W_OUT_ROWS = D // N_DEV

ADAM_LR, ADAM_B1, ADAM_B2, ADAM_EPS, ADAM_WD, ADAM_STEP = 0.001, 0.9, 0.999, 1e-08, 0.01, 10

VMEM_LIMIT = 56 * 1024 * 1024


def _tile(n, pref):
    return pref if n % pref == 0 else n


def _params(sem=None, vmem=VMEM_LIMIT):
    return pltpu.CompilerParams(dimension_semantics=sem, vmem_limit_bytes=vmem)


def _sigmoid(z):
    return 1.0 / (1.0 + jnp.exp(-z))


def _dot(a, b, dims):
    return lax.dot_general(a, b, (dims, ((), ())), preferred_element_type=F32)


NN = ((1,), (0,))
NT = ((1,), (1,))
TN = ((0,), (0,))


PROJ_TN = 768
N_PROJ_TILES = D_IN // PROJ_TN
PROJ_SCHEDULE = ((0, 2, 4, 1, 3, 6, 5, 5), (2, 0, 6, 1, 4, 3, 5, 4), (4, 6, 0, 5, 2, 1, 3, 4), (6, 4, 2, 5, 3, 0, 1, 5))


def _prenorm_inproj(sched, x, g, bias, wt_buf):
    S = x.shape[0]
    tp, tm = _tile(S, 512), _tile(S, 1024)
    n_pre, ns = S // tp, S // tm
    n_steps = n_pre + N_PROJ_TILES * ns
    pos_of = lambda i: jnp.maximum(i - n_pre, 0) // ns
    row_of = lambda i: jnp.maximum(i - n_pre, 0) % ns

    def body(sched_ref, x_ref, g_ref, b_ref, wt_in, h_ref, proj_ref, wt_ref, h_all, w_tile, send_sems, recv_sems,
             w_sems):
        del wt_in
        i = pl.program_id(0)
        x_, y_, c = _mesh_pos()
        me, sibling = (x_, y_, c), (x_, y_, 1 - c)
        chips = [(1 - x_, y_), (x_, 1 - y_), (1 - x_, 1 - y_)]

        def block(pos):
            px, py, pc = pos
            return wt_ref.at[pl.ds(pl.multiple_of((4 * px + 2 * py + pc) * W_IN_ROWS, 16), W_IN_ROWS), :]

        def copy(k, blk, to):
            return pltpu.make_async_remote_copy(
                src_ref=block(blk), dst_ref=block(blk), send_sem=send_sems.at[k], recv_sem=recv_sems.at[k],
                device_id=to, device_id_type=MESH)

        relay = (jnp.where(c == 0, x_, 1 - x_), jnp.where(c == 0, 1 - y_, y_))
        relayed = (jnp.where(c == 0, 1 - x_, x_), jnp.where(c == 0, y_, 1 - y_))

        def own_sends():
            return [copy(0, me, sibling), copy(2, me, (*chips[1], c)), copy(1, me, (*chips[0], c))]

        def passed_on():
            return [copy(4, (*chips[0], c), sibling), copy(5, (*chips[1], c), sibling), copy(3, (*relayed, c), (*relay, c))]

        def neighbours_arrive():
            copy(1, (*chips[0], c), me).wait_recv()
            copy(2, (*chips[1], c), me).wait_recv()
            for cp in passed_on():
                cp.start()
            copy(5, (*chips[1], 1 - c), me).wait_recv()

        def diagonal_arrives():
            copy(3, (*chips[2], c), me).wait_recv()
            copy(6, (*chips[2], c), sibling).start()
            copy(6, (*chips[2], 1 - c), me).wait_recv()

        def tile_load(p):
            slot = p % 2
            rows = wt_ref.at[pl.ds(pl.multiple_of(sched_ref[p] * PROJ_TN, 16), PROJ_TN), :]
            return pltpu.make_async_copy(rows, w_tile.at[slot], w_sems.at[slot])

        def prepare(p):
            p = jnp.asarray(p, jnp.int32)
            pl.when(p == 0)(lambda: copy(0, sibling, me).wait_recv())
            pl.when(p == 1)(neighbours_arrive)
            pl.when(p == 2)(lambda: copy(4, (*chips[0], 1 - c), me).wait_recv())
            pl.when(p == sched_ref[N_PROJ_TILES])(diagonal_arrives)
            tile_load(p).start()

        @pl.when(i == 0)
        def _():
            for cp in own_sends():
                cp.start()

        @pl.when(i < n_pre)
        def _():
            xv = x_ref[...]
            r = lax.rsqrt(jnp.mean(xv * xv, axis=-1, keepdims=True) + EPS)
            hv = (xv * r * g_ref[...]).astype(BF16)
            h_ref[...] = hv
            h_all[pl.ds(pl.multiple_of(i * tp, tp), tp), :] = hv

        pl.when(i == n_pre - 1)(lambda: prepare(0))

        @pl.when(i >= n_pre)
        def _():
            p, s = pos_of(i), row_of(i)
            pl.when(s == 0)(lambda: tile_load(p).wait())
            pl.when((s == ns - 1) & (p < N_PROJ_TILES - 1))(lambda: prepare(p + 1))
            hv = h_all[pl.ds(pl.multiple_of(s * tm, tm), tm), :]
            proj_ref[...] = _dot(hv, w_tile[p % 2], NT) + b_ref[...]

        @pl.when(i == n_steps - 1)
        def _():
            for cp in own_sends() + passed_on() + [copy(6, (*chips[2], c), sibling)]:
                cp.wait_send()

    return pl.pallas_call(
        body, name="prenorm_inproj",
        grid_spec=pltpu.PrefetchScalarGridSpec(
            num_scalar_prefetch=1, grid=(n_steps,),
            in_specs=[pl.BlockSpec((tp, D), lambda i, sc: (jnp.minimum(i, n_pre - 1), 0)),
                      pl.BlockSpec((1, D), lambda i, sc: (0, 0)),
                      pl.BlockSpec((1, PROJ_TN), lambda i, sc: (0, sc[pos_of(i)])),
                      HBM_SPEC],
            out_specs=[pl.BlockSpec((tp, D), lambda i, sc: (jnp.minimum(i, n_pre - 1), 0)),
                       pl.BlockSpec((tm, PROJ_TN), lambda i, sc: (row_of(i), sc[pos_of(i)])),
                       HBM_SPEC],
            scratch_shapes=[pltpu.VMEM((S, D), BF16), pltpu.VMEM((2, PROJ_TN, D), BF16),
                            pltpu.SemaphoreType.DMA((7,)), pltpu.SemaphoreType.DMA((7,)),
                            pltpu.SemaphoreType.DMA((2,))]),
        out_shape=[jax.ShapeDtypeStruct((S, D), BF16), jax.ShapeDtypeStruct((S, D_IN), F32),
                   jax.ShapeDtypeStruct((D_IN, D), BF16)],
        input_output_aliases={4: 2},
        compiler_params=_params(("arbitrary",)),
    )(sched, x, g, bias, wt_buf)


def _rope_tables(pos_ref, invf_ref):
    ang = pos_ref[...].astype(F32) * invf_ref[...]
    return jnp.cos(ang), jnp.sin(ang)


def _rot_half(xs, first_half):
    return jnp.where(first_half, -pltpu.roll(xs, 96, 1), pltpu.roll(xs, 32, 1))


GH = NQ // 2


MASKED = -1e30


def _attn_consts():
    lane = lax.broadcasted_iota(jnp.int32, (T, 128), 1)
    row = lax.broadcasted_iota(jnp.int32, (GH * T, T), 0) & (T - 1)
    on_diag_or_below = row >= lax.broadcasted_iota(jnp.int32, (GH * T, T), 1)
    return (lane & (HD - 1)) < (HD // 2), lane < HD, on_diag_or_below


def _stack_heads(slab_fn, grp, lo64):
    blocks = []
    for jj in range(GH // 2):
        s = slab_fn(GH // 2 * grp + jj)
        r = pltpu.roll(s, HD, 1)
        if grp == 0:
            blocks += [jnp.where(lo64, s, 0.0), jnp.where(lo64, r, 0.0)]
        else:
            blocks += [jnp.where(lo64, 0.0, r), jnp.where(lo64, 0.0, s)]
    return jnp.concatenate(blocks, axis=0)


def _unstack_heads(stacked, jj, grp, lo64):
    a = stacked[(2 * jj) * T:(2 * jj + 1) * T]
    b = stacked[(2 * jj + 1) * T:(2 * jj + 2) * T]
    if grp == 0:
        return jnp.where(lo64, a, pltpu.roll(b, HD, 1))
    return jnp.where(lo64, pltpu.roll(a, HD, 1), b)


def _layer_norm_stats(v):
    mu = jnp.mean(v, axis=-1, keepdims=True)
    xc = v - mu
    var = jnp.mean(xc * xc, axis=-1, keepdims=True)
    rs = lax.rsqrt(var + EPS)
    return xc * rs, rs


def _band_softmax(q_scaled, k_prev, k_cur, own, has_prev, sink):
    s_prev = _dot(q_scaled, k_prev, NT) + jnp.where(has_prev, 0.0, MASKED)
    s = jnp.where(own, _dot(q_scaled, k_cur, NT), s_prev)
    m = jnp.maximum(jnp.max(s, axis=-1, keepdims=True), sink)
    e = jnp.exp(s - m)
    es = jnp.exp(sink - m)
    inv = 1.0 / (jnp.sum(e, axis=-1, keepdims=True) + es)
    return e * inv, es * inv


def _unfold(p, own):
    return jnp.where(own, p, 0.0).astype(BF16), jnp.where(own, 0.0, p).astype(BF16)


def _mid_specs(nb, rev):
    blk = (lambda i: nb - 1 - i) if rev else (lambda i: i)
    prev = lambda i: jnp.maximum(blk(i) - 1, 0)
    return blk, prev, [
        pl.BlockSpec((T, D_IN), lambda i: (blk(i), 0)),
        pl.BlockSpec((T, 2 * T), lambda i: (prev(i), OFF_K // (2 * T))),
    ], [
        pl.BlockSpec((T, 1), lambda i: (blk(i), 0)),
        pl.BlockSpec((T, 1), lambda i: (prev(i), 0)),
        pl.BlockSpec((1, 128), lambda i: (0, 0)),
        pl.BlockSpec((1, DG), lambda i: (0, 0)),
        pl.BlockSpec((1, DG), lambda i: (0, 0)),
        pl.BlockSpec((NG, T, T), lambda i: (0, 0, 0)),
        pl.BlockSpec((NG, T, 1), lambda i: (0, 0, 0)),
        pl.BlockSpec((NQ * T, 1), lambda i: (0, 0)),
    ]


def _mid_fwd(proj, pos, invf, ln_g, ln_b, w_s, b_s, sinks, wo_buf):
    S = proj.shape[0]
    nb = S // T
    blk, _, proj_specs, par_specs = _mid_specs(nb, False)

    def body(p_ref, kvp_ref, pos_ref, posp_ref, invf_ref, lng_ref, lnb_ref, ws_ref, bs_ref, sink_ref, wo_in,
             y_ref, wo_ref, send_sems, recv_sems):
        del wo_in
        n = pl.program_id(0)
        first_half, lo64, own = _attn_consts()
        gather = _RowGather(wo_ref, W_OUT_ROWS, send_sems, recv_sems)
        pl.when(n == 0)(gather.start)
        tril = lax.broadcasted_iota(jnp.int32, (T, T), 0) >= lax.broadcasted_iota(jnp.int32, (T, T), 1)

        xhat, _ = _layer_norm_stats(p_ref[:, OFF_V:OFF_V + DG])
        vn = xhat * lng_ref[...] + lnb_ref[...]
        for g in range(NG):
            sl = slice(128 * g, 128 * g + 128)
            wm = jnp.where(tril, ws_ref[g], 0.0).astype(BF16)
            mixed = _dot(wm, vn[:, sl].astype(BF16), NN) + bs_ref[g]
            z = p_ref[:, OFF_ZA + 128 * g:OFF_ZA + 128 * g + 128]
            u = p_ref[:, OFF_U + 128 * g:OFF_U + 128 * g + 128]
            y_ref[:, sl] = (u * mixed * (z * _sigmoid(z))).astype(BF16)

        cosv, sinv = _rope_tables(pos_ref, invf_ref)
        cosp, sinp = _rope_tables(posp_ref, invf_ref)
        k_cur = p_ref[:, OFF_K:OFF_K + 128]
        k_prev = kvp_ref[:, 0:128]
        k_cur = (k_cur * cosv + _rot_half(k_cur, first_half) * sinv).astype(BF16)
        k_prev = (k_prev * cosp + _rot_half(k_prev, first_half) * sinp).astype(BF16)
        v_cur, v_prev = p_ref[:, OFF_VA:OFF_VA + 128].astype(BF16), kvp_ref[:, 128:256].astype(BF16)

        def q_slab(j):
            qraw = p_ref[:, OFF_Q + 128 * j:OFF_Q + 128 * j + 128]
            return (qraw * cosv + _rot_half(qraw, first_half) * sinv) * (HD ** -0.5)

        for grp in range(2):
            qs = _stack_heads(q_slab, grp, lo64).astype(BF16)
            p, _ = _band_softmax(qs, k_prev, k_cur, own, n > 0, sink_ref[GH * T * grp:GH * T * (grp + 1), :])
            p_cur, p_prev = _unfold(p, own)
            o_st = _dot(p_cur, v_cur, NN) + _dot(p_prev, v_prev, NN)
            for jj in range(GH // 2):
                c0 = 128 * (GH // 2 * grp + jj)
                zb = p_ref[:, OFF_ZB + c0:OFF_ZB + c0 + 128]
                o = _unstack_heads(o_st, jj, grp, lo64)
                y_ref[:, DG + c0:DG + c0 + 128] = (o * (zb * _sigmoid(zb))).astype(BF16)
        pl.when(n == nb - 1)(gather.finish)

    n_in = len(proj_specs) + len(par_specs)
    return pl.pallas_call(
        body, name="mid_fwd", grid=(nb,),
        in_specs=proj_specs + par_specs + [HBM_SPEC],
        out_specs=[pl.BlockSpec((T, D), lambda i: (blk(i), 0)), HBM_SPEC],
        out_shape=[jax.ShapeDtypeStruct((S, D), BF16), jax.ShapeDtypeStruct((D, D), BF16)],
        input_output_aliases={n_in: 1},
        scratch_shapes=GATHER_SEMS,
        compiler_params=_params(("arbitrary",)),
    )(proj, proj, pos, pos, invf, ln_g, ln_b, w_s, b_s, sinks, wo_buf)


def _mid_bwd(proj, dycat, pos, invf, ln_g, ln_b, w_s, b_s, sinks, po):
    S = proj.shape[0]
    nb = S // T
    blk, _, proj_specs, par_specs = _mid_specs(nb, True)
    const2 = lambda i: (0, 0)

    def body(p_ref, kvp_ref, dyc_ref, pos_ref, posp_ref, invf_ref, lng_ref, lnb_ref, ws_ref, bs_ref, sink_ref, po_ref,
             dp_ref, dlng_ref, dlnb_ref, dws_ref, dbs_ref, dsink_ref, dbqkv_ref, bo_ref,
             carry_ref, dvn_ref, send_sems, recv_sems, local_sem):
        i = pl.program_id(0)
        n = nb - 1 - i
        first_half, lo64, own = _attn_consts()
        tril = lax.broadcasted_iota(jnp.int32, (T, T), 0) >= lax.broadcasted_iota(jnp.int32, (T, T), 1)
        exchange =_OwnerExchange(po_ref, bo_ref, send_sems, recv_sems, local_sem)
        pl.when(i == 0)(exchange.start)

        @pl.when(i == 0)
        def _():
            dlng_ref[...] = jnp.zeros_like(dlng_ref)
            dlnb_ref[...] = jnp.zeros_like(dlnb_ref)
            dws_ref[...] = jnp.zeros_like(dws_ref)
            dbs_ref[...] = jnp.zeros_like(dbs_ref)
            dsink_ref[...] = jnp.zeros_like(dsink_ref)
            dbqkv_ref[...] = jnp.zeros_like(dbqkv_ref)
            carry_ref[...] = jnp.zeros_like(carry_ref)

        xhat, rs = _layer_norm_stats(p_ref[:, OFF_V:OFF_V + DG])
        lng = lng_ref[...]
        vn = xhat * lng + lnb_ref[...]
        for g in range(NG):
            sl = slice(128 * g, 128 * g + 128)
            wm = jnp.where(tril, ws_ref[g], 0.0).astype(BF16)
            vng = vn[:, sl].astype(BF16)
            mixed = _dot(wm, vng, NN) + bs_ref[g]
            z = p_ref[:, OFF_ZA + 128 * g:OFF_ZA + 128 * g + 128]
            u = p_ref[:, OFF_U + 128 * g:OFF_U + 128 * g + 128]
            dy = dyc_ref[:, sl]
            sg = _sigmoid(z)
            sa = z * sg
            dp_ref[:, OFF_U + 128 * g:OFF_U + 128 * g + 128] = (dy * mixed * sa).astype(BF16)
            dp_ref[:, OFF_ZA + 128 * g:OFF_ZA + 128 * g + 128] = (
                dy * u * mixed * (sg * (1.0 + z * (1.0 - sg)))).astype(BF16)
            dm = dy * u * sa
            dmb = dm.astype(BF16)
            dvn_ref[:, sl] = _dot(wm, dmb, TN)
            dws_ref[g] += jnp.where(tril, _dot(dmb, vng, NT), 0.0)
            dbs_ref[g] += jnp.sum(dm, axis=1, keepdims=True)
        dvn = dvn_ref[...]
        dlng_ref[...] += jnp.sum(dvn * xhat, axis=0, keepdims=True)
        dlnb_ref[...] += jnp.sum(dvn, axis=0, keepdims=True)
        dxh = dvn * lng
        dv_g = rs * (dxh - jnp.mean(dxh, axis=-1, keepdims=True)
                     - xhat * jnp.mean(dxh * xhat, axis=-1, keepdims=True))
        dp_ref[:, OFF_V:OFF_V + DG] = dv_g.astype(BF16)

        cosv, sinv = _rope_tables(pos_ref, invf_ref)
        cosp, sinp = _rope_tables(posp_ref, invf_ref)
        k_cur = p_ref[:, OFF_K:OFF_K + 128]
        k_prev = kvp_ref[:, 0:128]
        k_cur = (k_cur * cosv + _rot_half(k_cur, first_half) * sinv).astype(BF16)
        k_prev = (k_prev * cosp + _rot_half(k_prev, first_half) * sinp).astype(BF16)
        v_cur, v_prev = p_ref[:, OFF_VA:OFF_VA + 128].astype(BF16), kvp_ref[:, 128:256].astype(BF16)

        def q_slab(j):
            qraw = p_ref[:, OFF_Q + 128 * j:OFF_Q + 128 * j + 128]
            return (qraw * cosv + _rot_half(qraw, first_half) * sinv) * (HD ** -0.5)

        def do_slab(j):
            zb = p_ref[:, OFF_ZB + 128 * j:OFF_ZB + 128 * j + 128]
            return dyc_ref[:, DG + 128 * j:DG + 128 * j + 128] * (zb * _sigmoid(zb))

        dk_cur, dv_cur = carry_ref[:, 0:128], carry_ref[:, 128:256]
        dk_prev, dv_prev = jnp.zeros((T, 128), F32), jnp.zeros((T, 128), F32)
        for grp in range(2):
            qs = _stack_heads(q_slab, grp, lo64).astype(BF16)
            d_o = _stack_heads(do_slab, grp, lo64)
            dob = d_o.astype(BF16)
            p, ps = _band_softmax(qs, k_prev, k_cur, own, n > 0, sink_ref[GH * T * grp:GH * T * (grp + 1), :])
            p_cur, p_prev = _unfold(p, own)
            o_st = _dot(p_cur, v_cur, NN) + _dot(p_prev, v_prev, NN)
            delta = jnp.sum(d_o * o_st, axis=-1, keepdims=True)
            dp_fold = jnp.where(own, _dot(dob, v_cur, NT), _dot(dob, v_prev, NT))
            ds_cur, ds_prev = _unfold(p * (dp_fold - delta), own)
            dq_st = (_dot(ds_cur, k_cur, NN) + _dot(ds_prev, k_prev, NN)) * (HD ** -0.5)
            dk_cur, dk_prev = dk_cur + _dot(ds_cur, qs, TN), dk_prev + _dot(ds_prev, qs, TN)
            dv_cur, dv_prev = dv_cur + _dot(p_cur, dob, TN), dv_prev + _dot(p_prev, dob, TN)
            dsink_rows = ps * delta
            for hh in range(GH):
                h = GH * grp + hh
                dsink_ref[h:h + 1, :] += jnp.broadcast_to(
                    -jnp.sum(dsink_rows[hh * T:(hh + 1) * T], axis=0, keepdims=True), (1, 128))
            for jj in range(GH // 2):
                c0 = 128 * (GH // 2 * grp + jj)
                zb = p_ref[:, OFF_ZB + c0:OFF_ZB + c0 + 128]
                sg = _sigmoid(zb)
                o = _unstack_heads(o_st, jj, grp, lo64)
                dp_ref[:, OFF_ZB + c0:OFF_ZB + c0 + 128] = (
                    dyc_ref[:, DG + c0:DG + c0 + 128] * o * (sg * (1.0 + zb * (1.0 - sg)))).astype(BF16)
                dq = _unstack_heads(dq_st, jj, grp, lo64)
                dq_pre = dq * cosv - _rot_half(dq, first_half) * sinv
                dp_ref[:, OFF_Q + c0:OFF_Q + c0 + 128] = dq_pre.astype(BF16)
                dbqkv_ref[:, c0:c0 + 128] += jnp.sum(dq_pre, axis=0, keepdims=True)
        carry_ref[:, 0:128] = dk_prev
        carry_ref[:, 128:256] = dv_prev
        dk_pre =dk_cur * cosv - _rot_half(dk_cur, first_half) * sinv
        dp_ref[:, OFF_K:OFF_K + 128] = dk_pre.astype(BF16)
        dp_ref[:, OFF_VA:OFF_VA + 128] = dv_cur.astype(BF16)
        dbqkv_ref[:, 1024:1152] += jnp.sum(dk_pre, axis=0, keepdims=True)
        dbqkv_ref[:, 1152:1280] += jnp.sum(dv_cur, axis=0, keepdims=True)
        pl.when(i == nb - 1)(exchange.finish)

    return pl.pallas_call(
        body, name="mid_bwd", grid=(nb,),
        in_specs=proj_specs + [pl.BlockSpec((T, D), lambda i: (blk(i), 0))] + par_specs + [HBM_SPEC],
        out_specs=[pl.BlockSpec((T, D_IN), lambda i: (blk(i), 0)),
                   pl.BlockSpec((1, DG), const2), pl.BlockSpec((1, DG), const2),
                   pl.BlockSpec((NG, T, T), lambda i: (0, 0, 0)), pl.BlockSpec((NG, T, 1), lambda i: (0, 0, 0)),
                   pl.BlockSpec((NQ, 128), const2), pl.BlockSpec((1, D_QKV), const2), HBM_SPEC],
        out_shape=[jax.ShapeDtypeStruct((S, D_IN), BF16),
                   jax.ShapeDtypeStruct((1, DG), F32), jax.ShapeDtypeStruct((1, DG), F32),
                   jax.ShapeDtypeStruct((NG, T, T), F32), jax.ShapeDtypeStruct((NG, T, 1), F32),
                   jax.ShapeDtypeStruct((NQ, 128), F32), jax.ShapeDtypeStruct((1, D_QKV), F32),
                   jax.ShapeDtypeStruct(po.shape, BF16)],
        scratch_shapes=[pltpu.VMEM((T, 2 * T), F32), pltpu.VMEM((T, DG), F32)] + OWNER_SEMS,
        compiler_params=_params(("arbitrary",)),
    )(proj, proj, dycat, pos, pos, invf, ln_g, ln_b, w_s, b_s, sinks, po)


def _outproj_loss(ycat, wo, x, target, g_post):
    S = ycat.shape[0]
    tm = _tile(S, 256)
    nt = S // tm
    const2 = lambda i: (0, 0)

    def body(yc_ref, w_ref, x_ref, t_ref, g_ref, dy_ref, dout_ref, loss_ref, dg_ref, lacc_ref):
        i = pl.program_id(0)

        @pl.when(i == 0)
        def _():
            dg_ref[...] = jnp.zeros_like(dg_ref)
            lacc_ref[...] = jnp.zeros_like(lacc_ref)

        y = _dot(yc_ref[...], w_ref[...], NN)
        r = lax.rsqrt(jnp.mean(y * y, axis=-1, keepdims=True) + EPS)
        yh = y * r
        g = g_ref[...]
        diff = x_ref[...] + yh * g - t_ref[...]
        lacc_ref[...] += jnp.sum(diff * diff, axis=0, keepdims=True)
        dout = diff * (1.0 / D)
        dout_ref[...] = dout
        dg_ref[...] += jnp.sum(dout * yh, axis=0, keepdims=True)
        dyh = dout * g
        dy_ref[...] = (r * (dyh - yh * jnp.mean(dyh * yh, axis=-1, keepdims=True))).astype(BF16)

        @pl.when(i == nt - 1)
        def _():
            loss_ref[...] = jnp.broadcast_to(jnp.sum(lacc_ref[...], axis=1, keepdims=True) * (0.5 / D), (1, 128))

    row = lambda i: (i, 0)
    return pl.pallas_call(
        body, name="outproj_loss", grid=(nt,),
        in_specs=[pl.BlockSpec((tm, D), row), pl.BlockSpec((D, D), const2), pl.BlockSpec((tm, D), row),
                  pl.BlockSpec((tm, D), row), pl.BlockSpec((1, D), const2)],
        out_specs=[pl.BlockSpec((tm, D), row), pl.BlockSpec((tm, D), row), pl.BlockSpec((1, 128), const2),
                   pl.BlockSpec((1, D), const2)],
        out_shape=[jax.ShapeDtypeStruct((S, D), BF16), jax.ShapeDtypeStruct((S, D), F32),
                   jax.ShapeDtypeStruct((1, 128), F32), jax.ShapeDtypeStruct((1, D), F32)],
        scratch_shapes=[pltpu.VMEM((1, D), F32)],
        compiler_params=_params(("arbitrary",)),
    )(ycat, wo, x, target, g_post)


def _dycat(dy, wo):
    S = dy.shape[0]
    tm = _tile(S, 512)

    def body(dy_ref, w_ref, o_ref):
        o_ref[...] = _dot(dy_ref[...], w_ref[...], NT)

    return pl.pallas_call(
        body, name="dycat", grid=(S // tm,),
        in_specs=[pl.BlockSpec((tm, D), lambda i: (i, 0)), pl.BlockSpec((D, D), lambda i: (0, 0))],
        out_specs=pl.BlockSpec((tm, D), lambda i: (i, 0)),
        out_shape=jax.ShapeDtypeStruct((S, D), F32),
        compiler_params=_params(("parallel",)),
    )(dy, wo)


def _matmul_tn(a, b, tm, name):
    K, M = a.shape
    N = b.shape[1]
    tk = _tile(K, 1024)
    nk = K // tk

    def body(a_ref, b_ref, o_ref, acc_ref):
        k = pl.program_id(1)

        @pl.when(k == 0)
        def _():
            acc_ref[...] = jnp.zeros_like(acc_ref)

        acc_ref[...] += _dot(a_ref[...], b_ref[...], TN)

        @pl.when(k == nk - 1)
        def _():
            o_ref[...] = acc_ref[...].astype(BF16)

    return pl.pallas_call(
        body, name=name, grid=(M // tm, nk),
        in_specs=[pl.BlockSpec((tk, tm), lambda i, k: (k, i)), pl.BlockSpec((tk, N), lambda i, k: (k, 0))],
        out_specs=pl.BlockSpec((tm, N), lambda i, k: (i, 0)),
        out_shape=jax.ShapeDtypeStruct((M, N), BF16),
        scratch_shapes=[pltpu.VMEM((tm, N), F32)],
        compiler_params=_params(("parallel", "arbitrary")),
    )(a, b)


def _dh_prenorm_bwd(dproj, wt, x, dout, g_pre, pt):
    S = x.shape[0]
    tm, tk = _tile(S, 512), 768
    ni, nk = S // tm, D_IN // tk

    def body(dp_ref, w_ref, x_ref, dout_ref, g_ref, pt_ref, gx_ref, dg_ref, bt_ref, acc_ref,
             send_sems, recv_sems, local_sem):
        i, k = pl.program_id(0), pl.program_id(1)
        exchange = _OwnerExchange(pt_ref, bt_ref, send_sems, recv_sems, local_sem)
        pl.when((i == 0) & (k == 0))(exchange.start)

        @pl.when((i == 0) & (k == 0))
        def _():
            dg_ref[...] = jnp.zeros_like(dg_ref)

        @pl.when(k == 0)
        def _():
            acc_ref[...] = jnp.zeros_like(acc_ref)

        acc_ref[...] += _dot(dp_ref[...], w_ref[...], NN)

        @pl.when(k == nk - 1)
        def _():
            dh = acc_ref[...]
            xv = x_ref[...]
            r = lax.rsqrt(jnp.mean(xv * xv, axis=-1, keepdims=True) + EPS)
            xh = xv * r
            dg_ref[...] += jnp.sum(dh * xh, axis=0, keepdims=True)
            dxh = dh * g_ref[...]
            gx_ref[...] = dout_ref[...] + r * (dxh - xh * jnp.mean(dxh * xh, axis=-1, keepdims=True))

        pl.when((i == ni - 1) & (k == nk - 1))(exchange.finish)

    row = lambda i, k: (i, 0)
    return pl.pallas_call(
        body, name="dh_prenorm_bwd", grid=(ni, nk),
        in_specs=[pl.BlockSpec((tm, tk), lambda i, k: (i, k)), pl.BlockSpec((tk, D), lambda i, k: (k, 0)),
                  pl.BlockSpec((tm, D), row), pl.BlockSpec((tm, D), row), pl.BlockSpec((1, D), lambda i, k: (0, 0)),
                  HBM_SPEC],
        out_specs=[pl.BlockSpec((tm, D), row), pl.BlockSpec((1, D), lambda i, k: (0, 0)), HBM_SPEC],
        out_shape=[jax.ShapeDtypeStruct((S, D), F32), jax.ShapeDtypeStruct((1, D), F32),
                   jax.ShapeDtypeStruct(pt.shape, BF16)],
        scratch_shapes=[pltpu.VMEM((tm, D), F32)] + OWNER_SEMS,
        compiler_params=_params(("arbitrary", "arbitrary")),
    )(dproj, wt, x, dout, g_pre, pt)


def _presum(c_arr, own, recv, half_rows):
    n_cols = own.shape[-1]
    own4 = own.reshape(N_CHIPS, 2, half_rows, n_cols)

    def body(c_ref, own_ref, recv_ref, o_ref):
        o_ref[...] = (own_ref[...].astype(F32) + recv_ref[...].astype(F32)).astype(BF16)

    return pl.pallas_call(
        body, name="presum_%d" % half_rows,
        grid_spec=pltpu.PrefetchScalarGridSpec(
            num_scalar_prefetch=1, grid=(N_CHIPS,),
            in_specs=[pl.BlockSpec((None, None, half_rows, n_cols), lambda j, c: (j, c[0], 0, 0)),
                      pl.BlockSpec((None, half_rows, n_cols), lambda j, c: (j, 0, 0))],
            out_specs=pl.BlockSpec((None, half_rows, n_cols), lambda j, c: (j, 0, 0))),
        out_shape=jax.ShapeDtypeStruct((N_CHIPS, half_rows, n_cols), BF16),
        compiler_params=_params(("parallel",)),
    )(c_arr, own4, recv)


def _sum_chips(c_arr, parts, name):
    _, rows, n_cols = parts.shape
    nt = 2
    tr = rows // nt

    def body(c_ref, p_ref, o_ref):
        o_ref[...] = ((p_ref[0].astype(F32) + p_ref[1].astype(F32)) + p_ref[2].astype(F32)) + p_ref[3].astype(F32)

    return pl.pallas_call(
        body, name=name,
        grid_spec=pltpu.PrefetchScalarGridSpec(
            num_scalar_prefetch=1, grid=(nt,),
            in_specs=[pl.BlockSpec((N_CHIPS, tr, n_cols), lambda i, c: (0, i, 0))],
            out_specs=pl.BlockSpec((tr, n_cols), lambda i, c: (c[0] * nt + i, 0))),
        out_shape=jax.ShapeDtypeStruct((2 * rows, n_cols), F32),
        compiler_params=_params(("parallel",)),
    )(c_arr, parts)


def _adamw_math(w, g, m, v):
    mn = ADAM_B1 * m + (1.0 - ADAM_B1) * g
    vn = ADAM_B2 * v + (1.0 - ADAM_B2) * (g * g)
    m_hat = mn / (1.0 - ADAM_B1 ** ADAM_STEP)
    v_hat = vn / (1.0 - ADAM_B2 ** ADAM_STEP)
    return -ADAM_LR * (m_hat / (jnp.sqrt(v_hat) + ADAM_EPS) + ADAM_WD * w), mn, vn


def _adamw(w, g, m, v, name):
    R, C = w.shape
    tr = next((t for t in (256, 192, 128) if R % t == 0), R)

    def body(w_ref, g_ref, m_ref, v_ref, d_ref, mo_ref, vo_ref):
        d_ref[...], mo_ref[...], vo_ref[...] = _adamw_math(w_ref[...], g_ref[...], m_ref[...], v_ref[...])

    spec = pl.BlockSpec((tr, C), lambda i: (i, 0))
    shp = jax.ShapeDtypeStruct((R, C), F32)
    return pl.pallas_call(
        body, name=name, grid=(R // tr,), in_specs=[spec] * 4, out_specs=[spec] * 3, out_shape=[shp] * 3,
        compiler_params=_params(("parallel",)),
    )(w, g, m, v)


HBM_SPEC = pl.BlockSpec(memory_space=pltpu.HBM)
GATHER_SEMS = [pltpu.SemaphoreType.DMA((7,)), pltpu.SemaphoreType.DMA((7,))]
OWNER_SEMS = [pltpu.SemaphoreType.DMA((3,)), pltpu.SemaphoreType.DMA((3,)), pltpu.SemaphoreType.DMA(())]


def _mesh_pos():
    return lax.axis_index("x"), lax.axis_index("y"), lax.axis_index("c")


class _RowGather:
    def __init__(self, full_ref, rows, send_sems, recv_sems):
        self.full, self.rows, self.send, self.recv = full_ref, rows, send_sems, recv_sems
        x, y, c = _mesh_pos()
        self.c, self.me, self.sibling = c, (x, y, c), (x, y, 1 - c)
        self.chips = [(1 - x, y), (x, 1 - y), (1 - x, 1 - y)]

    def _block(self, pos):
        px, py, pc = pos
        return self.full.at[pl.ds(pl.multiple_of((4 * px + 2 * py + pc) * self.rows, 16), self.rows), :]

    def _copy(self, k, blk, to):
        return pltpu.make_async_remote_copy(
            src_ref=self._block(blk), dst_ref=self._block(blk),
            send_sem=self.send.at[k], recv_sem=self.recv.at[k], device_id=to, device_id_type=MESH)

    def _first(self):
        return [self._copy(0, self.me, self.sibling)] + [
            self._copy(1 + j, self.me, (*chip, self.c)) for j, chip in enumerate(self.chips)]

    def start(self):
        for cp in self._first():
            cp.start()

    def finish(self):
        passed = [self._copy(4 + j, (*chip, self.c), self.sibling) for j, chip in enumerate(self.chips)]
        for j, chip in enumerate(self.chips):
            self._copy(1 + j, (*chip, self.c), self.me).wait_recv()
            passed[j].start()
        self._copy(0, self.sibling, self.me).wait_recv()
        for j, chip in enumerate(self.chips):
            self._copy(4 + j, (*chip, 1 - self.c), self.me).wait_recv()
        for cp in self._first() + passed:
            cp.wait_send()


class _OwnerExchange:
    def __init__(self, src_ref, dst_ref, send_sems, recv_sems, local_sem):
        self.src, self.dst, self.send, self.recv, self.local = src_ref, dst_ref, send_sems, recv_sems, local_sem
        x, y, c = _mesh_pos()
        self.c, self.my_chip = c, 2 * x + y
        self.peers = [(1 - x, y), (x, 1 - y), (1 - x, 1 - y)]

    def _copies(self):
        local = pltpu.make_async_copy(self.src.at[self.my_chip], self.dst.at[self.my_chip], self.local)
        remote = [pltpu.make_async_remote_copy(
            src_ref=self.src.at[2 * px + py], dst_ref=self.dst.at[self.my_chip],
            send_sem=self.send.at[k], recv_sem=self.recv.at[k], device_id=(px, py, self.c), device_id_type=MESH)
            for k, (px, py) in enumerate(self.peers)]
        return local, remote

    def start(self):
        local, remote = self._copies()
        local.start()
        for cp in remote:
            cp.start()

    def finish(self):
        local, remote = self._copies()
        for cp in remote:
            cp.wait_recv()
        for cp in remote:
            cp.wait_send()
        local.wait()


def _to_sibling(dw, rows, name):
    def body(dw_ref, r_ref, send_sems, recv_sems):
        x, y, c = _mesh_pos()
        copies = [pltpu.make_async_remote_copy(
            src_ref=dw_ref.at[pl.ds(pl.multiple_of((2 * j + (1 - c)) * rows, 16), rows), :], dst_ref=r_ref.at[j],
            send_sem=send_sems.at[j], recv_sem=recv_sems.at[j], device_id=(x, y, 1 - c), device_id_type=MESH)
            for j in range(N_CHIPS)]
        for cp in copies:
            cp.start()
        for cp in copies:
            cp.wait_recv()
        for cp in copies:
            cp.wait_send()

    return pl.pallas_call(
        body, name=name, in_specs=[HBM_SPEC], out_specs=HBM_SPEC,
        out_shape=jax.ShapeDtypeStruct((N_CHIPS, rows, dw.shape[1]), BF16),
        scratch_shapes=[pltpu.SemaphoreType.DMA((N_CHIPS,)), pltpu.SemaphoreType.DMA((N_CHIPS,))],
    )(dw)


PAIR_CHUNKS = 4


def _pair_halves(gt, go):
    def body(gt_in, go_in, gt_ref, go_ref, send_sems, recv_sems):
        del gt_in, go_in
        x, y, c = _mesh_pos()
        copies = []
        for a, (ref, rows) in enumerate(((gt_ref, W_IN_ROWS), (go_ref, W_OUT_ROWS))):
            ch = rows // PAIR_CHUNKS
            for q in range(PAIR_CHUNKS):
                part = ref.at[pl.ds(pl.multiple_of(c * rows + q * ch, 8), ch), :]
                copies.append(pltpu.make_async_remote_copy(
                    src_ref=part, dst_ref=part, send_sem=send_sems.at[PAIR_CHUNKS * a + q],
                    recv_sem=recv_sems.at[PAIR_CHUNKS * a + q], device_id=(x, y, 1 - c), device_id_type=MESH))
        for cp in copies:
            cp.start()
        for cp in copies:
            cp.wait_recv()
        for cp in copies:
            cp.wait_send()

    return pl.pallas_call(
        body, name="pair_halves",
        in_specs=[HBM_SPEC, HBM_SPEC], out_specs=[HBM_SPEC, HBM_SPEC],
        out_shape=[jax.ShapeDtypeStruct(gt.shape, F32), jax.ShapeDtypeStruct(go.shape, F32)],
        input_output_aliases={0: 0, 1: 1},
        scratch_shapes=[pltpu.SemaphoreType.DMA((2 * PAIR_CHUNKS,)), pltpu.SemaphoreType.DMA((2 * PAIR_CHUNKS,))],
    )(gt, go)


VEC = (("g_pre", 2048), ("g_post", 2048), ("b_qkv", 1280), ("ln_v_g", 1024), ("ln_v_b", 1024), ("attn_sinks", 16))
VEC_ROWS = 8
LOSS_ROW = len(VEC)
MAT = (("w_spatial", NG * T), ("b_spatial", NG))
MAT_ROWS = sum(r for _, r in MAT)


def _small_update(vec_grads, loss_part, mat_grads, vec_state, mat_state):
    n_vec, n_mat = len(VEC), len(MAT)
    n_par = n_vec + n_mat
    n_in = n_par + 1 + 3 * n_par

    def body(*refs):
        g_in, loss_in = refs[:n_par], refs[n_par]
        st_in = refs[n_par + 1:n_in]
        outs, loss_out = refs[n_in:n_in + 4 * n_par], refs[n_in + 4 * n_par]
        vbuf, mbuf, tot_v, tot_m, send_sems, recv_sems = refs[n_in + 4 * n_par + 1:]
        x, y, c = _mesh_pos()
        me, sibling = (x, y, c), (x, y, 1 - c)
        chips = [(1 - x, y), (x, 1 - y), (1 - x, 1 - y)]
        my_id = 4 * x + 2 * y + c

        vbuf[my_id] = jnp.zeros((VEC_ROWS, D), F32)
        for r, (_, n) in enumerate(VEC):
            vbuf[my_id, r:r + 1, 0:n] = g_in[r][...]
        vbuf[my_id, LOSS_ROW:LOSS_ROW + 1, 0:128] = loss_in[...]
        r0 = 0
        for q, (_, rows) in enumerate(MAT):
            mbuf[my_id, r0:r0 + rows, :] = g_in[n_vec + q][...]
            r0 += rows

        def copy(a, k, blk, to):
            buf = (vbuf, mbuf)[a]
            px, py, pc = blk
            slot = buf.at[4 * px + 2 * py + pc]
            return pltpu.make_async_remote_copy(
                src_ref=slot, dst_ref=slot, send_sem=send_sems.at[7 * a + k], recv_sem=recv_sems.at[7 * a + k],
                device_id=to, device_id_type=MESH)

        first = [copy(a, 0, me, sibling) for a in range(2)]
        first += [copy(a, 1 + j, me, (*chip, c)) for a in range(2) for j, chip in enumerate(chips)]
        for cp in first:
            cp.start()
        passed = []
        for a in range(2):
            for j, chip in enumerate(chips):
                copy(a, 1 + j, (*chip, c), me).wait_recv()
                passed.append(copy(a, 4 + j, (*chip, c), sibling))
                passed[-1].start()
        for a in range(2):
            copy(a, 0, sibling, me).wait_recv()
            for j, chip in enumerate(chips):
                copy(a, 4 + j, (*chip, 1 - c), me).wait_recv()
        for cp in first + passed:
            cp.wait_send()

        tv, tm = vbuf[0], mbuf[0]
        for d in range(1, N_DEV):
            tv, tm = tv + vbuf[d], tm + mbuf[d]
        tot_v[...] = tv
        tot_m[...] = tm
        loss_out[...] = tot_v[LOSS_ROW:LOSS_ROW + 1, 0:128]
        r0 = 0
        for q in range(n_par):
            if q < n_vec:
                g = tot_v[q:q + 1, 0:VEC[q][1]]
            else:
                rows = MAT[q - n_vec][1]
                g = tot_m[r0:r0 + rows, :]
                r0 += rows
            w, m, v = (st_in[3 * q + t][...] for t in range(3))
            outs[4 * q][...] = g
            outs[4 * q + 1][...], outs[4 * q + 2][...], outs[4 * q + 3][...] = _adamw_math(w, g, m, v)

    grads = list(vec_grads) + list(mat_grads)
    state = [a for wmv in list(vec_state) + list(mat_state) for a in wmv]
    vmem = pl.BlockSpec(memory_space=pltpu.VMEM)
    out_shape = [jax.ShapeDtypeStruct(g.shape, F32) for g in grads for _ in range(4)]
    out_shape.append(jax.ShapeDtypeStruct((1, 128), F32))
    res = pl.pallas_call(
        body, name="small_update",
        in_specs=[vmem] * n_in, out_specs=[vmem] * len(out_shape), out_shape=out_shape,
        scratch_shapes=[pltpu.VMEM((N_DEV, VEC_ROWS, D), F32), pltpu.VMEM((N_DEV, MAT_ROWS, 128), F32),
                        pltpu.VMEM((VEC_ROWS, D), F32), pltpu.VMEM((MAT_ROWS, 128), F32),
                        pltpu.SemaphoreType.DMA((14,)), pltpu.SemaphoreType.DMA((14,))],
        compiler_params=pltpu.CompilerParams(vmem_limit_bytes=VMEM_LIMIT),
    )(*grads, loss_part, *state)
    return [res[4 * q:4 * q + 4] for q in range(n_par)], res[-1]


def kernel(x, positions, g_pre, w_in, b_qkv, ln_v_g, ln_v_b, w_spatial, b_spatial, attn_sinks, w_out, g_post, loss_target, m_g_pre, m_w_in, m_b_qkv, m_ln_v_g, m_ln_v_b, m_w_spatial, m_b_spatial, m_attn_sinks, m_w_out, m_g_post, v_g_pre, v_w_in, v_b_qkv, v_ln_v_g, v_ln_v_b, v_w_spatial, v_b_spatial, v_attn_sinks, v_w_out, v_g_post):
    S = x.shape[1]
    c = lax.axis_index("c")
    c_arr = jnp.reshape(c, (1,)).astype(jnp.int32)
    x2 = x[0]
    target = loss_target[0]
    pos = positions.reshape(S, 1)
    half = HD // 2
    inv_freq = ROPE_THETA ** (-jnp.arange(half, dtype=F32) * (2.0 / HD))
    invf = jnp.tile(inv_freq, 128 // half).reshape(1, 128)
    bias = jnp.concatenate([jnp.zeros((OFF_Q,), F32), b_qkv[0], jnp.zeros((D_IN - OFF_ZB,), F32)]).reshape(1, D_IN)
    b_s_col = b_spatial[0].reshape(NG, T, 1)
    sinks = jnp.repeat(attn_sinks[0], T).reshape(NQ * T, 1)

    chip = 2 * lax.axis_index("x") + lax.axis_index("y")
    dev = 2 * chip + c
    wt_part = lax.dynamic_slice_in_dim(w_in[0].T.astype(BF16), c * W_IN_ROWS, W_IN_ROWS, axis=0)
    wo_part = lax.dynamic_slice_in_dim(w_out[0].astype(BF16), c * W_OUT_ROWS, W_OUT_ROWS, axis=0)
    wt_buf = lax.dynamic_update_slice_in_dim(lax.empty((D_IN, D), BF16), wt_part, dev * W_IN_ROWS, axis=0)
    wo_buf = lax.dynamic_update_slice_in_dim(lax.empty((D, D), BF16), wo_part, dev * W_OUT_ROWS, axis=0)
    sched = jnp.asarray(PROJ_SCHEDULE, jnp.int32)[chip]

    h, proj, wt = _prenorm_inproj(sched, x2, g_pre, bias, wt_buf)
    ycat, wo = _mid_fwd(proj, pos, invf, ln_v_g, ln_v_b, w_spatial[0], b_s_col, sinks, wo_buf)
    dy, dout, loss_part, dg_post = _outproj_loss(ycat, wo, x2, target, g_post)

    dycat = _dycat(dy, wo)
    dwo = _matmul_tn(ycat, dy, 512, "dw_out")
    po = _presum(c_arr, dwo, _to_sibling(dwo, W_OUT_ROWS, "to_sibling_out"), W_OUT_ROWS)
    dproj, dln_g, dln_b, dws, dbs, dsink, dbqkv, bo = _mid_bwd(
        proj, dycat, pos, invf, ln_v_g, ln_v_b, w_spatial[0], b_s_col, sinks, po)
    dwt = _matmul_tn(dproj, h, 768, "dw_in_t")
    pt = _presum(c_arr, dwt, _to_sibling(dwt, W_IN_ROWS, "to_sibling_in"), W_IN_ROWS)
    grad_x, dg_pre, bt = _dh_prenorm_bwd(dproj, wt, x2, dout, g_pre, pt)
    gt, g_w_out = _pair_halves(_sum_chips(c_arr, bt, "sum_chips_in"), _sum_chips(c_arr, bo, "sum_chips_out"))

    d_t, nm_t, nv_t = _adamw(w_in[0].T, gt, m_w_in[0].T, v_w_in[0].T, "adamw_w_in")
    g_w_in, d_w_in, nm_w_in, nv_w_in = gt.T, d_t.T, nm_t.T, nv_t.T
    d_w_out, nm_w_out, nv_w_out = _adamw(w_out[0], g_w_out, m_w_out[0], v_w_out[0], "adamw_w_out")

    state = {"g_pre": (g_pre, m_g_pre, v_g_pre), "g_post": (g_post, m_g_post, v_g_post),
             "b_qkv": (b_qkv, m_b_qkv, v_b_qkv), "ln_v_g": (ln_v_g, m_ln_v_g, v_ln_v_g),
             "ln_v_b": (ln_v_b, m_ln_v_b, v_ln_v_b), "attn_sinks": (attn_sinks, m_attn_sinks, v_attn_sinks),
             "w_spatial": tuple(a.reshape(NG * T, T) for a in (w_spatial, m_w_spatial, v_w_spatial)),
             "b_spatial": tuple(a.reshape(NG, T) for a in (b_spatial, m_b_spatial, v_b_spatial))}
    grads = {"g_pre": dg_pre, "g_post": dg_post, "b_qkv": dbqkv, "ln_v_g": dln_g, "ln_v_b": dln_b,
             "attn_sinks": dsink[:, 0].reshape(1, NQ), "w_spatial": dws.reshape(NG * T, T), "b_spatial": dbs.reshape(NG, T)}
    results, loss = _small_update([grads[n] for n, _ in VEC], loss_part, [grads[n] for n, _ in MAT],
                                  [state[n] for n, _ in VEC], [state[n] for n, _ in MAT])
    small = {n: [a.reshape(w.shape) for a in res]
             for (n, _), res, w in zip(VEC + MAT, results, [state[n][0] for n, _ in VEC + MAT])}
    small["w_spatial"] = [a.reshape(w_spatial.shape) for a in small["w_spatial"]]
    small["b_spatial"] = [a.reshape(b_spatial.shape) for a in small["b_spatial"]]
    big = {"w_in": [a[None] for a in (g_w_in, d_w_in, nm_w_in, nv_w_in)],
           "w_out": [a[None] for a in (g_w_out, d_w_out, nm_w_out, nv_w_out)]}
    order = ("g_pre", "w_in", "b_qkv", "ln_v_g", "ln_v_b", "w_spatial", "b_spatial", "attn_sinks", "w_out", "g_post")
    leaves = {**small, **big}
    return (loss[0, 0], grad_x[None], *[leaves[n][t] for t in range(4) for n in order])
```

```python
import jax
import jax.numpy as jnp
from jax import lax
from jax.experimental import pallas as pl
from jax.experimental.pallas import tpu as pltpu

F32 = jnp.float32
BF16 = jnp.bfloat16
MESH = pl.DeviceIdType.MESH

D = 2048
DG = 1024
T = 128
NG = 8
HD = 64
NQ = 16
D_IN = 5376
OFF_U, OFF_V, OFF_ZA, OFF_Q, OFF_K, OFF_VA, OFF_ZB = 0, 1024, 2048, 3072, 4096, 4224, 4352
D_QKV = 1280
EPS = 1e-6
ROPE_THETA = 10000.0
N_CHIPS = 4
N_DEV = 8
W_IN_ROWS = D_IN // N_DEV
W_OUT_ROWS = D // N_DEV

ADAM_LR, ADAM_B1, ADAM_B2, ADAM_EPS, ADAM_WD, ADAM_STEP = 0.001, 0.9, 0.999, 1e-08, 0.01, 10

VMEM_LIMIT = 56 * 1024 * 1024


def _tile(n, pref):
    return pref if n % pref == 0 else n


def _params(sem=None, vmem=VMEM_LIMIT):
    return pltpu.CompilerParams(dimension_semantics=sem, vmem_limit_bytes=vmem)


def _sigmoid(z):
    return 1.0 / (1.0 + jnp.exp(-z))


def _dot(a, b, dims):
    return lax.dot_general(a, b, (dims, ((), ())), preferred_element_type=F32)


NN = ((1,), (0,))
NT = ((1,), (1,))
TN = ((0,), (0,))


PROJ_TN = 768
N_PROJ_TILES = D_IN // PROJ_TN
PROJ_SCHEDULE = ((0, 2, 4, 1, 3, 6, 5, 5), (2, 0, 6, 1, 4, 3, 5, 4), (4, 6, 0, 5, 2, 1, 3, 4), (6, 4, 2, 5, 3, 0, 1, 5))


LOCAL_CHUNKS = 6


def _place_locally(src_ref, dst_rows_ref, sems, n_chunks=LOCAL_CHUNKS):
    ch = src_ref.shape[0] // n_chunks
    return [pltpu.make_async_copy(src_ref.at[pl.ds(q * ch, ch), :], dst_rows_ref.at[pl.ds(q * ch, ch), :], sems.at[q])
            for q in range(n_chunks)]


def _prenorm_inproj(sched, x, g, bias, wt_part):
    S = x.shape[0]
    tp, tm = _tile(S, 512), _tile(S, 1024)
    n_pre, ns = S // tp, S // tm
    n_steps = n_pre + N_PROJ_TILES * ns
    pos_of = lambda i: jnp.maximum(i - n_pre, 0) // ns
    row_of = lambda i: jnp.maximum(i - n_pre, 0) % ns

    def body(sched_ref, x_ref, g_ref, b_ref, wpart_ref, h_ref, proj_ref, wt_ref, h_all, w_tile, send_sems, recv_sems,
             w_sems, local_sems):
        i = pl.program_id(0)
        x_, y_, c = _mesh_pos()
        me, sibling = (x_, y_, c), (x_, y_, 1 - c)
        chips = [(1 - x_, y_), (x_, 1 - y_), (1 - x_, 1 - y_)]

        def block(pos):
            px, py, pc = pos
            return wt_ref.at[pl.ds(pl.multiple_of((4 * px + 2 * py + pc) * W_IN_ROWS, 16), W_IN_ROWS), :]

        def copy(k, blk, to):
            return pltpu.make_async_remote_copy(
                src_ref=wpart_ref if blk is me else block(blk), dst_ref=block(blk),
                send_sem=send_sems.at[k], recv_sem=recv_sems.at[k], device_id=to, device_id_type=MESH)

        def mine():
            return _place_locally(wpart_ref, block(me), local_sems)

        relay = (jnp.where(c == 0, x_, 1 - x_), jnp.where(c == 0, 1 - y_, y_))
        relayed = (jnp.where(c == 0, 1 - x_, x_), jnp.where(c == 0, y_, 1 - y_))

        def own_sends():
            return [copy(0, me, sibling), copy(2, me, (*chips[1], c)), copy(1, me, (*chips[0], c))]

        def passed_on():
            return [copy(4, (*chips[0], c), sibling), copy(5, (*chips[1], c), sibling), copy(3, (*relayed, c), (*relay, c))]

        def neighbours_arrive():
            copy(1, (*chips[0], c), me).wait_recv()
            copy(2, (*chips[1], c), me).wait_recv()
            for cp in passed_on():
                cp.start()
            copy(5, (*chips[1], 1 - c), me).wait_recv()

        def diagonal_arrives():
            copy(3, (*chips[2], c), me).wait_recv()
            copy(6, (*chips[2], c), sibling).start()
            copy(6, (*chips[2], 1 - c), me).wait_recv()

        def tile_load(p):
            slot = p % 2
            rows = wt_ref.at[pl.ds(pl.multiple_of(sched_ref[p] * PROJ_TN, 16), PROJ_TN), :]
            return pltpu.make_async_copy(rows, w_tile.at[slot], w_sems.at[slot])

        def prepare(p):
            p = jnp.asarray(p, jnp.int32)

            @pl.when(p == 0)
            def _():
                copy(0, sibling, me).wait_recv()
                for cp in mine():
                    cp.wait()

            pl.when(p == 1)(neighbours_arrive)
            pl.when(p == 2)(lambda: copy(4, (*chips[0], 1 - c), me).wait_recv())
            pl.when(p == sched_ref[N_PROJ_TILES])(diagonal_arrives)
            tile_load(p).start()

        @pl.when(i == 0)
        def _():
            for cp in own_sends() + mine():
                cp.start()

        @pl.when(i < n_pre)
        def _():
            xv = x_ref[...]
            r = lax.rsqrt(jnp.mean(xv * xv, axis=-1, keepdims=True) + EPS)
            hv = (xv * r * g_ref[...]).astype(BF16)
            h_ref[...] = hv
            h_all[pl.ds(pl.multiple_of(i * tp, tp), tp), :] = hv

        pl.when(i == n_pre - 1)(lambda: prepare(0))

        @pl.when(i >= n_pre)
        def _():
            p, s = pos_of(i), row_of(i)
            pl.when(s == 0)(lambda: tile_load(p).wait())
            pl.when((s == ns - 1) & (p < N_PROJ_TILES - 1))(lambda: prepare(p + 1))
            hv = h_all[pl.ds(pl.multiple_of(s * tm, tm), tm), :]
            proj_ref[...] = _dot(hv, w_tile[p % 2], NT) + b_ref[...]

        @pl.when(i == n_steps - 1)
        def _():
            for cp in own_sends() + passed_on() + [copy(6, (*chips[2], c), sibling)]:
                cp.wait_send()

    return pl.pallas_call(
        body, name="prenorm_inproj",
        grid_spec=pltpu.PrefetchScalarGridSpec(
            num_scalar_prefetch=1, grid=(n_steps,),
            in_specs=[pl.BlockSpec((tp, D), lambda i, sc: (jnp.minimum(i, n_pre - 1), 0)),
                      pl.BlockSpec((1, D), lambda i, sc: (0, 0)),
                      pl.BlockSpec((1, PROJ_TN), lambda i, sc: (0, sc[pos_of(i)])),
                      HBM_SPEC],
            out_specs=[pl.BlockSpec((tp, D), lambda i, sc: (jnp.minimum(i, n_pre - 1), 0)),
                       pl.BlockSpec((tm, PROJ_TN), lambda i, sc: (row_of(i), sc[pos_of(i)])),
                       HBM_SPEC],
            scratch_shapes=[pltpu.VMEM((S, D), BF16), pltpu.VMEM((2, PROJ_TN, D), BF16),
                            pltpu.SemaphoreType.DMA((7,)), pltpu.SemaphoreType.DMA((7,)),
                            pltpu.SemaphoreType.DMA((2,)), pltpu.SemaphoreType.DMA((LOCAL_CHUNKS,))]),
        out_shape=[jax.ShapeDtypeStruct((S, D), BF16), jax.ShapeDtypeStruct((S, D_IN), F32),
                   jax.ShapeDtypeStruct((D_IN, D), BF16)],
        compiler_params=_params(("arbitrary",)),
    )(sched, x, g, bias, wt_part)


def _rope_table(pos, invf):
    S = pos.shape[0]
    ts = _tile(S, 1024)

    def body(pos_ref, invf_ref, cos_ref, sin_ref):
        ang = pos_ref[...].astype(F32) * invf_ref[...]
        cos_ref[...] = jnp.cos(ang)
        sin_ref[...] = jnp.sin(ang)

    return pl.pallas_call(
        body, name="rope_table", grid=(S // ts,),
        in_specs=[pl.BlockSpec((ts, 1), lambda i: (i, 0)), pl.BlockSpec((1, 128), lambda i: (0, 0))],
        out_specs=[pl.BlockSpec((ts, 128), lambda i: (i, 0))] * 2,
        out_shape=[jax.ShapeDtypeStruct((S, 128), F32)] * 2,
        compiler_params=_params(("parallel",)),
    )(pos, invf)


def _rot_half(xs, first_half):
    return jnp.where(first_half, -pltpu.roll(xs, 96, 1), pltpu.roll(xs, 32, 1))


GH = NQ // 2


MASKED = -1e30


def _attn_consts():
    lane = lax.broadcasted_iota(jnp.int32, (T, 128), 1)
    row = lax.broadcasted_iota(jnp.int32, (GH * T, T), 0) & (T - 1)
    on_diag_or_below = row >= lax.broadcasted_iota(jnp.int32, (GH * T, T), 1)
    return (lane & (HD - 1)) < (HD // 2), lane < HD, on_diag_or_below


def _stack_heads(slab_fn, grp, lo64):
    blocks = []
    for jj in range(GH // 2):
        s = slab_fn(GH // 2 * grp + jj)
        r = pltpu.roll(s, HD, 1)
        if grp == 0:
            blocks += [jnp.where(lo64, s, 0.0), jnp.where(lo64, r, 0.0)]
        else:
            blocks += [jnp.where(lo64, 0.0, r), jnp.where(lo64, 0.0, s)]
    return jnp.concatenate(blocks, axis=0)


def _unstack_heads(stacked, jj, grp, lo64):
    a = stacked[(2 * jj) * T:(2 * jj + 1) * T]
    b = stacked[(2 * jj + 1) * T:(2 * jj + 2) * T]
    if grp == 0:
        return jnp.where(lo64, a, pltpu.roll(b, HD, 1))
    return jnp.where(lo64, pltpu.roll(a, HD, 1), b)


def _layer_norm_stats(v):
    mu = jnp.mean(v, axis=-1, keepdims=True)
    xc = v - mu
    var = jnp.mean(xc * xc, axis=-1, keepdims=True)
    rs = lax.rsqrt(var + EPS)
    return xc * rs, rs


def _band_softmax(q_scaled, k_prev, k_cur, own, has_prev, sink):
    s_prev = _dot(q_scaled, k_prev, NT)
    if has_prev is not None:
        s_prev = s_prev + jnp.where(has_prev, 0.0, MASKED)
    s = jnp.where(own, _dot(q_scaled, k_cur, NT), s_prev)
    m = jnp.maximum(jnp.max(s, axis=-1, keepdims=True), sink)
    e = jnp.exp(s - m)
    es = jnp.exp(sink - m)
    inv = 1.0 / (jnp.sum(e, axis=-1, keepdims=True) + es)
    return e * inv, es * inv


def _unfold(p, own):
    return jnp.where(own, p, 0.0).astype(BF16), jnp.where(own, 0.0, p).astype(BF16)


BPS_FWD = 4
BPS_BWD = 2


def _mid_specs(nt, rev, bps):
    tile = (lambda i: nt - 1 - i) if rev else (lambda i: i)
    prev = lambda i: jnp.maximum(bps * tile(i) - 1, 0)
    rows = bps * T
    return tile, [
        pl.BlockSpec((rows, D_IN), lambda i: (tile(i), 0)),
        pl.BlockSpec((T, 2 * T), lambda i: (prev(i), OFF_K // (2 * T))),
    ], [
        pl.BlockSpec((rows, 128), lambda i: (tile(i), 0)),
        pl.BlockSpec((rows, 128), lambda i: (tile(i), 0)),
        pl.BlockSpec((T, 128), lambda i: (prev(i), 0)),
        pl.BlockSpec((T, 128), lambda i: (prev(i), 0)),
        pl.BlockSpec((1, DG), lambda i: (0, 0)),
        pl.BlockSpec((1, DG), lambda i: (0, 0)),
        pl.BlockSpec((NG, T, T), lambda i: (0, 0, 0)),
        pl.BlockSpec((NG, T, 1), lambda i: (0, 0, 0)),
        pl.BlockSpec((NQ * T, 1), lambda i: (0, 0)),
    ]


def _rope(xs, cosv, sinv, first_half):
    return xs * cosv + _rot_half(xs, first_half) * sinv


def _mid_fwd(proj, cos, sin, ln_g, ln_b, w_s, b_s, sinks, wo_part):
    S = proj.shape[0]
    BPS = BPS_FWD if S % (BPS_FWD * T) == 0 else 1
    nt = S // (BPS * T)
    tile, proj_specs, par_specs = _mid_specs(nt, False, BPS)

    def body(p_ref, kvp_ref, cos_ref, sin_ref, cosp_ref, sinp_ref, lng_ref, lnb_ref, ws_ref, bs_ref, sink_ref, wpart_ref,
             y_ref, wo_ref, send_sems, recv_sems, local_sems):
        i = pl.program_id(0)
        first_half, lo64, own = _attn_consts()
        gather = _RowGather(wpart_ref, wo_ref, W_OUT_ROWS, send_sems, recv_sems, local_sems)
        pl.when(i == 0)(gather.start)
        tril = lax.broadcasted_iota(jnp.int32, (T, T), 0) >= lax.broadcasted_iota(jnp.int32, (T, T), 1)

        def block(b, k_prev, v_prev, has_prev):
            rows = slice(b * T, (b + 1) * T)
            xhat, _ = _layer_norm_stats(p_ref[rows, OFF_V:OFF_V + DG])
            vn = xhat * lng_ref[...] + lnb_ref[...]
            for g in range(NG):
                sl = slice(128 * g, 128 * g + 128)
                wm = jnp.where(tril, ws_ref[g], 0.0).astype(BF16)
                mixed = _dot(wm, vn[:, sl].astype(BF16), NN) + bs_ref[g]
                z = p_ref[rows, OFF_ZA + 128 * g:OFF_ZA + 128 * g + 128]
                u = p_ref[rows, OFF_U + 128 * g:OFF_U + 128 * g + 128]
                y_ref[rows, sl] = (u * mixed * (z * _sigmoid(z))).astype(BF16)

            cosv, sinv = cos_ref[rows, :], sin_ref[rows, :]
            k_cur = _rope(p_ref[rows, OFF_K:OFF_K + 128], cosv, sinv, first_half).astype(BF16)
            v_cur = p_ref[rows, OFF_VA:OFF_VA + 128].astype(BF16)

            def q_slab(j):
                return _rope(p_ref[rows, OFF_Q + 128 * j:OFF_Q + 128 * j + 128], cosv, sinv, first_half) * (HD ** -0.5)

            for grp in range(2):
                qs = _stack_heads(q_slab, grp, lo64).astype(BF16)
                p, _ = _band_softmax(qs, k_prev, k_cur, own, has_prev, sink_ref[GH * T * grp:GH * T * (grp + 1), :])
                p_cur, p_prev = _unfold(p, own)
                o_st = _dot(p_cur, v_cur, NN) + _dot(p_prev, v_prev, NN)
                for jj in range(GH // 2):
                    c0 = 128 * (GH // 2 * grp + jj)
                    zb = p_ref[rows, OFF_ZB + c0:OFF_ZB + c0 + 128]
                    o = _unstack_heads(o_st, jj, grp, lo64)
                    y_ref[rows, DG + c0:DG + c0 + 128] = (o * (zb * _sigmoid(zb))).astype(BF16)
            return k_cur, v_cur

        k_prev = _rope(kvp_ref[:, 0:128], cosp_ref[...], sinp_ref[...], first_half).astype(BF16)
        kv = block(0, k_prev, kvp_ref[:, 128:256].astype(BF16), i > 0)
        for b in range(1, BPS):
            kv = block(b, *kv, None)
        pl.when(i == nt - 1)(gather.finish)

    return pl.pallas_call(
        body, name="mid_fwd", grid=(nt,),
        in_specs=proj_specs + par_specs + [HBM_SPEC],
        out_specs=[pl.BlockSpec((BPS * T, D), lambda i: (tile(i), 0)), HBM_SPEC],
        out_shape=[jax.ShapeDtypeStruct((S, D), BF16), jax.ShapeDtypeStruct((D, D), BF16)],
        scratch_shapes=GATHER_SEMS,
        compiler_params=_params(("arbitrary",)),
    )(proj, proj, cos, sin, cos, sin, ln_g, ln_b, w_s, b_s, sinks, wo_part)


def _mid_bwd(proj, dycat, cos, sin, ln_g, ln_b, w_s, b_s, sinks, po):
    S = proj.shape[0]
    BPS = BPS_BWD if S % (BPS_BWD * T) == 0 else 1
    nt = S // (BPS * T)
    tile, proj_specs, par_specs = _mid_specs(nt, True, BPS)
    const2 = lambda i: (0, 0)

    def body(p_ref, kvp_ref, dyc_ref, cos_ref, sin_ref, cosp_ref, sinp_ref, lng_ref, lnb_ref, ws_ref, bs_ref, sink_ref,
             po_ref, dp_ref, dlng_ref, dlnb_ref, dws_ref, dbs_ref, dsink_ref, dbqkv_ref, bo_ref,
             carry_ref, dvn_ref, send_sems, recv_sems, local_sem):
        i = pl.program_id(0)
        first_half, lo64, own = _attn_consts()
        tril = lax.broadcasted_iota(jnp.int32, (T, T), 0) >= lax.broadcasted_iota(jnp.int32, (T, T), 1)
        exchange = _OwnerExchange(po_ref, bo_ref, send_sems, recv_sems, local_sem)
        pl.when(i == 0)(exchange.start)

        @pl.when(i == 0)
        def _():
            dlng_ref[...] = jnp.zeros_like(dlng_ref)
            dlnb_ref[...] = jnp.zeros_like(dlnb_ref)
            dws_ref[...] = jnp.zeros_like(dws_ref)
            dbs_ref[...] = jnp.zeros_like(dbs_ref)
            dsink_ref[...] = jnp.zeros_like(dsink_ref)
            dbqkv_ref[...] = jnp.zeros_like(dbqkv_ref)
            carry_ref[...] = jnp.zeros_like(carry_ref)

        def roped_k(b):
            rows = slice(b * T, (b + 1) * T)
            return _rope(p_ref[rows, OFF_K:OFF_K + 128], cos_ref[rows, :], sin_ref[rows, :], first_half).astype(BF16)

        def block(b, k_prev, v_prev, has_prev, dk_next, dv_next):
            rows = slice(b * T, (b + 1) * T)
            xhat, rs = _layer_norm_stats(p_ref[rows, OFF_V:OFF_V + DG])
            lng = lng_ref[...]
            vn = xhat * lng + lnb_ref[...]
            for g in range(NG):
                sl = slice(128 * g, 128 * g + 128)
                wm = jnp.where(tril, ws_ref[g], 0.0).astype(BF16)
                vng = vn[:, sl].astype(BF16)
                mixed = _dot(wm, vng, NN) + bs_ref[g]
                z = p_ref[rows, OFF_ZA + 128 * g:OFF_ZA + 128 * g + 128]
                u = p_ref[rows, OFF_U + 128 * g:OFF_U + 128 * g + 128]
                dy = dyc_ref[rows, sl]
                sg = _sigmoid(z)
                sa = z * sg
                dp_ref[rows, OFF_U + 128 * g:OFF_U + 128 * g + 128] = (dy * mixed * sa).astype(BF16)
                dp_ref[rows, OFF_ZA + 128 * g:OFF_ZA + 128 * g + 128] = (
                    dy * u * mixed * (sg * (1.0 + z * (1.0 - sg)))).astype(BF16)
                dm = dy * u * sa
                dmb = dm.astype(BF16)
                dvn_ref[rows, sl] = _dot(wm, dmb, TN)
                dws_ref[g] += jnp.where(tril, _dot(dmb, vng, NT), 0.0)
                dbs_ref[g] += jnp.sum(dm, axis=1, keepdims=True)
            dvn = dvn_ref[rows, :]
            dlng_ref[...] += jnp.sum(dvn * xhat, axis=0, keepdims=True)
            dlnb_ref[...] += jnp.sum(dvn, axis=0, keepdims=True)
            dxh = dvn * lng
            dv_g = rs * (dxh - jnp.mean(dxh, axis=-1, keepdims=True)
                         - xhat * jnp.mean(dxh * xhat, axis=-1, keepdims=True))
            dp_ref[rows, OFF_V:OFF_V + DG] = dv_g.astype(BF16)

            cosv, sinv = cos_ref[rows, :], sin_ref[rows, :]
            k_cur = roped_k(b)
            v_cur = p_ref[rows, OFF_VA:OFF_VA + 128].astype(BF16)

            def q_slab(j):
                return _rope(p_ref[rows, OFF_Q + 128 * j:OFF_Q + 128 * j + 128], cosv, sinv, first_half) * (HD ** -0.5)

            def do_slab(j):
                zb = p_ref[rows, OFF_ZB + 128 * j:OFF_ZB + 128 * j + 128]
                return dyc_ref[rows, DG + 128 * j:DG + 128 * j + 128] * (zb * _sigmoid(zb))

            dk_cur, dv_cur = jnp.zeros((T, 128), F32), jnp.zeros((T, 128), F32)
            dk_prev, dv_prev = jnp.zeros((T, 128), F32), jnp.zeros((T, 128), F32)
            for grp in range(2):
                qs = _stack_heads(q_slab, grp, lo64).astype(BF16)
                d_o = _stack_heads(do_slab, grp, lo64)
                dob = d_o.astype(BF16)
                p, ps = _band_softmax(qs, k_prev, k_cur, own, has_prev, sink_ref[GH * T * grp:GH * T * (grp + 1), :])
                p_cur, p_prev = _unfold(p, own)
                o_st = _dot(p_cur, v_cur, NN) + _dot(p_prev, v_prev, NN)
                delta = jnp.sum(d_o * o_st, axis=-1, keepdims=True)
                dp_fold = jnp.where(own, _dot(dob, v_cur, NT), _dot(dob, v_prev, NT))
                ds_cur, ds_prev = _unfold(p * (dp_fold - delta), own)
                dq_st = (_dot(ds_cur, k_cur, NN) + _dot(ds_prev, k_prev, NN)) * (HD ** -0.5)
                dk_cur, dk_prev = dk_cur + _dot(ds_cur, qs, TN), dk_prev + _dot(ds_prev, qs, TN)
                dv_cur, dv_prev = dv_cur + _dot(p_cur, dob, TN), dv_prev + _dot(p_prev, dob, TN)
                dsink_rows = ps * delta
                for hh in range(GH):
                    h = GH * grp + hh
                    dsink_ref[h:h + 1, :] += jnp.broadcast_to(
                        -jnp.sum(dsink_rows[hh * T:(hh + 1) * T], axis=0, keepdims=True), (1, 128))
                for jj in range(GH // 2):
                    c0 = 128 * (GH // 2 * grp + jj)
                    zb = p_ref[rows, OFF_ZB + c0:OFF_ZB + c0 + 128]
                    sg = _sigmoid(zb)
                    o = _unstack_heads(o_st, jj, grp, lo64)
                    dp_ref[rows, OFF_ZB + c0:OFF_ZB + c0 + 128] = (
                        dyc_ref[rows, DG + c0:DG + c0 + 128] * o * (sg * (1.0 + zb * (1.0 - sg)))).astype(BF16)
                    dq = _unstack_heads(dq_st, jj, grp, lo64)
                    dq_pre = dq * cosv - _rot_half(dq, first_half) * sinv
                    dp_ref[rows, OFF_Q + c0:OFF_Q + c0 + 128] = dq_pre.astype(BF16)
                    dbqkv_ref[:, c0:c0 + 128] += jnp.sum(dq_pre, axis=0, keepdims=True)
            dk_cur, dv_cur = dk_cur + dk_next, dv_cur + dv_next
            dk_pre = dk_cur * cosv - _rot_half(dk_cur, first_half) * sinv
            dp_ref[rows, OFF_K:OFF_K + 128] = dk_pre.astype(BF16)
            dp_ref[rows, OFF_VA:OFF_VA + 128] = dv_cur.astype(BF16)
            dbqkv_ref[:, 1024:1152] += jnp.sum(dk_pre, axis=0, keepdims=True)
            dbqkv_ref[:, 1152:1280] += jnp.sum(dv_cur, axis=0, keepdims=True)
            return dk_prev, dv_prev

        grads = carry_ref[:, 0:128], carry_ref[:, 128:256]
        for b in range(BPS - 1, 0, -1):
            prows = slice((b - 1) * T, b * T)
            grads = block(b, roped_k(b - 1), p_ref[prows, OFF_VA:OFF_VA + 128].astype(BF16), None, *grads)
        k_prev = _rope(kvp_ref[:, 0:128], cosp_ref[...], sinp_ref[...], first_half).astype(BF16)
        grads = block(0, k_prev, kvp_ref[:, 128:256].astype(BF16), i < nt - 1, *grads)
        carry_ref[:, 0:128], carry_ref[:, 128:256] = grads
        pl.when(i == nt - 1)(exchange.finish)

    return pl.pallas_call(
        body, name="mid_bwd", grid=(nt,),
        in_specs=proj_specs + [pl.BlockSpec((BPS * T, D), lambda i: (tile(i), 0))] + par_specs + [HBM_SPEC],
        out_specs=[pl.BlockSpec((BPS * T, D_IN), lambda i: (tile(i), 0)),
                   pl.BlockSpec((1, DG), const2), pl.BlockSpec((1, DG), const2),
                   pl.BlockSpec((NG, T, T), lambda i: (0, 0, 0)), pl.BlockSpec((NG, T, 1), lambda i: (0, 0, 0)),
                   pl.BlockSpec((NQ, 128), const2), pl.BlockSpec((1, D_QKV), const2), HBM_SPEC],
        out_shape=[jax.ShapeDtypeStruct((S, D_IN), BF16),
                   jax.ShapeDtypeStruct((1, DG), F32), jax.ShapeDtypeStruct((1, DG), F32),
                   jax.ShapeDtypeStruct((NG, T, T), F32), jax.ShapeDtypeStruct((NG, T, 1), F32),
                   jax.ShapeDtypeStruct((NQ, 128), F32), jax.ShapeDtypeStruct((1, D_QKV), F32),
                   jax.ShapeDtypeStruct(po.shape, BF16)],
        scratch_shapes=[pltpu.VMEM((T, 2 * T), F32), pltpu.VMEM((BPS * T, DG), F32)] + OWNER_SEMS,
        compiler_params=_params(("arbitrary",)),
    )(proj, proj, dycat, cos, sin, cos, sin, ln_g, ln_b, w_s, b_s, sinks, po)


def _outproj_loss(ycat, wo, x, target, g_post):
    S = ycat.shape[0]
    tm = _tile(S, 256)
    nt = S // tm
    const2 = lambda i: (0, 0)

    def body(yc_ref, w_ref, x_ref, t_ref, g_ref, dy_ref, dout_ref, loss_ref, dg_ref, lacc_ref):
        i = pl.program_id(0)

        @pl.when(i == 0)
        def _():
            dg_ref[...] = jnp.zeros_like(dg_ref)
            lacc_ref[...] = jnp.zeros_like(lacc_ref)

        y = _dot(yc_ref[...], w_ref[...], NN)
        r = lax.rsqrt(jnp.mean(y * y, axis=-1, keepdims=True) + EPS)
        yh = y * r
        g = g_ref[...]
        diff = x_ref[...] + yh * g - t_ref[...]
        lacc_ref[...] += jnp.sum(diff * diff, axis=0, keepdims=True)
        dout = diff * (1.0 / D)
        dout_ref[...] = dout
        dg_ref[...] += jnp.sum(dout * yh, axis=0, keepdims=True)
        dyh = dout * g
        dy_ref[...] = (r * (dyh - yh * jnp.mean(dyh * yh, axis=-1, keepdims=True))).astype(BF16)

        @pl.when(i == nt - 1)
        def _():
            loss_ref[...] = jnp.broadcast_to(jnp.sum(lacc_ref[...], axis=1, keepdims=True) * (0.5 / D), (1, 128))

    row = lambda i: (i, 0)
    return pl.pallas_call(
        body, name="outproj_loss", grid=(nt,),
        in_specs=[pl.BlockSpec((tm, D), row), pl.BlockSpec((D, D), const2), pl.BlockSpec((tm, D), row),
                  pl.BlockSpec((tm, D), row), pl.BlockSpec((1, D), const2)],
        out_specs=[pl.BlockSpec((tm, D), row), pl.BlockSpec((tm, D), row), pl.BlockSpec((1, 128), const2),
                   pl.BlockSpec((1, D), const2)],
        out_shape=[jax.ShapeDtypeStruct((S, D), BF16), jax.ShapeDtypeStruct((S, D), F32),
                   jax.ShapeDtypeStruct((1, 128), F32), jax.ShapeDtypeStruct((1, D), F32)],
        scratch_shapes=[pltpu.VMEM((1, D), F32)],
        compiler_params=_params(("arbitrary",)),
    )(ycat, wo, x, target, g_post)


def _dycat(dy, wo):
    S = dy.shape[0]
    tm = _tile(S, 512)

    def body(dy_ref, w_ref, o_ref):
        o_ref[...] = _dot(dy_ref[...], w_ref[...], NT)

    return pl.pallas_call(
        body, name="dycat", grid=(S // tm,),
        in_specs=[pl.BlockSpec((tm, D), lambda i: (i, 0)), pl.BlockSpec((D, D), lambda i: (0, 0))],
        out_specs=pl.BlockSpec((tm, D), lambda i: (i, 0)),
        out_shape=jax.ShapeDtypeStruct((S, D), F32),
        compiler_params=_params(("parallel",)),
    )(dy, wo)


def _matmul_tn(a, b, tm, name):
    K, M = a.shape
    N = b.shape[1]
    tk = _tile(K, 1024)
    nk = K // tk

    def body(a_ref, b_ref, o_ref, acc_ref):
        k = pl.program_id(1)

        @pl.when(k == 0)
        def _():
            acc_ref[...] = jnp.zeros_like(acc_ref)

        acc_ref[...] += _dot(a_ref[...], b_ref[...], TN)

        @pl.when(k == nk - 1)
        def _():
            o_ref[...] = acc_ref[...].astype(BF16)

    return pl.pallas_call(
        body, name=name, grid=(M // tm, nk),
        in_specs=[pl.BlockSpec((tk, tm), lambda i, k: (k, i)), pl.BlockSpec((tk, N), lambda i, k: (k, 0))],
        out_specs=pl.BlockSpec((tm, N), lambda i, k: (i, 0)),
        out_shape=jax.ShapeDtypeStruct((M, N), BF16),
        scratch_shapes=[pltpu.VMEM((tm, N), F32)],
        compiler_params=_params(("parallel", "arbitrary")),
    )(a, b)


def _dh_prenorm_bwd(dproj, wt, x, dout, g_pre, pt):
    S = x.shape[0]
    tm, tk = _tile(S, 512), 768
    ni, nk = S // tm, D_IN // tk

    def body(dp_ref, w_ref, x_ref, dout_ref, g_ref, pt_ref, gx_ref, dg_ref, bt_ref, acc_ref,
             send_sems, recv_sems, local_sem):
        i, k = pl.program_id(0), pl.program_id(1)
        exchange = _OwnerExchange(pt_ref, bt_ref, send_sems, recv_sems, local_sem)
        pl.when((i == 0) & (k == 0))(exchange.start)

        @pl.when((i == 0) & (k == 0))
        def _():
            dg_ref[...] = jnp.zeros_like(dg_ref)

        @pl.when(k == 0)
        def _():
            acc_ref[...] = jnp.zeros_like(acc_ref)

        acc_ref[...] += _dot(dp_ref[...], w_ref[...], NN)

        @pl.when(k == nk - 1)
        def _():
            dh = acc_ref[...]
            xv = x_ref[...]
            r = lax.rsqrt(jnp.mean(xv * xv, axis=-1, keepdims=True) + EPS)
            xh = xv * r
            dg_ref[...] += jnp.sum(dh * xh, axis=0, keepdims=True)
            dxh = dh * g_ref[...]
            gx_ref[...] = dout_ref[...] + r * (dxh - xh * jnp.mean(dxh * xh, axis=-1, keepdims=True))

        pl.when((i == ni - 1) & (k == nk - 1))(exchange.finish)

    row = lambda i, k: (i, 0)
    return pl.pallas_call(
        body, name="dh_prenorm_bwd", grid=(ni, nk),
        in_specs=[pl.BlockSpec((tm, tk), lambda i, k: (i, k)), pl.BlockSpec((tk, D), lambda i, k: (k, 0)),
                  pl.BlockSpec((tm, D), row), pl.BlockSpec((tm, D), row), pl.BlockSpec((1, D), lambda i, k: (0, 0)),
                  HBM_SPEC],
        out_specs=[pl.BlockSpec((tm, D), row), pl.BlockSpec((1, D), lambda i, k: (0, 0)), HBM_SPEC],
        out_shape=[jax.ShapeDtypeStruct((S, D), F32), jax.ShapeDtypeStruct((1, D), F32),
                   jax.ShapeDtypeStruct(pt.shape, BF16)],
        scratch_shapes=[pltpu.VMEM((tm, D), F32)] + OWNER_SEMS,
        compiler_params=_params(("arbitrary", "arbitrary")),
    )(dproj, wt, x, dout, g_pre, pt)


def _presum(c_arr, own, recv, half_rows):
    n_cols = own.shape[-1]
    own4 = own.reshape(N_CHIPS, 2, half_rows, n_cols)

    def body(c_ref, own_ref, recv_ref, o_ref):
        o_ref[...] = (own_ref[...].astype(F32) + recv_ref[...].astype(F32)).astype(BF16)

    return pl.pallas_call(
        body, name="presum_%d" % half_rows,
        grid_spec=pltpu.PrefetchScalarGridSpec(
            num_scalar_prefetch=1, grid=(N_CHIPS,),
            in_specs=[pl.BlockSpec((None, None, half_rows, n_cols), lambda j, c: (j, c[0], 0, 0)),
                      pl.BlockSpec((None, half_rows, n_cols), lambda j, c: (j, 0, 0))],
            out_specs=pl.BlockSpec((None, half_rows, n_cols), lambda j, c: (j, 0, 0))),
        out_shape=jax.ShapeDtypeStruct((N_CHIPS, half_rows, n_cols), BF16),
        compiler_params=_params(("parallel",)),
    )(c_arr, own4, recv)


def _sum_chips(c_arr, parts, name):
    _, rows, n_cols = parts.shape
    nt = 2
    tr = rows // nt

    def body(c_ref, p_ref, o_ref):
        o_ref[...] = ((p_ref[0].astype(F32) + p_ref[1].astype(F32)) + p_ref[2].astype(F32)) + p_ref[3].astype(F32)

    return pl.pallas_call(
        body, name=name,
        grid_spec=pltpu.PrefetchScalarGridSpec(
            num_scalar_prefetch=1, grid=(nt,),
            in_specs=[pl.BlockSpec((N_CHIPS, tr, n_cols), lambda i, c: (0, i, 0))],
            out_specs=pl.BlockSpec((tr, n_cols), lambda i, c: (c[0] * nt + i, 0))),
        out_shape=jax.ShapeDtypeStruct((2 * rows, n_cols), F32),
        compiler_params=_params(("parallel",)),
    )(c_arr, parts)


def _adamw_math(w, g, m, v):
    mn = ADAM_B1 * m + (1.0 - ADAM_B1) * g
    vn = ADAM_B2 * v + (1.0 - ADAM_B2) * (g * g)
    m_hat = mn / (1.0 - ADAM_B1 ** ADAM_STEP)
    v_hat = vn / (1.0 - ADAM_B2 ** ADAM_STEP)
    return -ADAM_LR * (m_hat / (jnp.sqrt(v_hat) + ADAM_EPS) + ADAM_WD * w), mn, vn


def _adamw(w, g, m, v, name):
    R, C = w.shape
    tr = next((t for t in (256, 192, 128) if R % t == 0), R)

    def body(w_ref, g_ref, m_ref, v_ref, go_ref, d_ref, mo_ref, vo_ref):
        gv = g_ref[...]
        go_ref[...] = gv
        d_ref[...], mo_ref[...], vo_ref[...] = _adamw_math(w_ref[...], gv, m_ref[...], v_ref[...])

    spec = pl.BlockSpec((tr, C), lambda i: (i, 0))
    shp = jax.ShapeDtypeStruct((R, C), F32)
    return pl.pallas_call(
        body, name=name, grid=(R // tr,), in_specs=[spec] * 4, out_specs=[spec] * 4, out_shape=[shp] * 4,
        compiler_params=_params(("parallel",)),
    )(w, g, m, v)


HBM_SPEC = pl.BlockSpec(memory_space=pltpu.HBM)
GATHER_LOCAL_CHUNKS = 4
GATHER_SEMS = [pltpu.SemaphoreType.DMA((7,)), pltpu.SemaphoreType.DMA((7,)),
               pltpu.SemaphoreType.DMA((GATHER_LOCAL_CHUNKS,))]
OWNER_SEMS = [pltpu.SemaphoreType.DMA((3,)), pltpu.SemaphoreType.DMA((3,)), pltpu.SemaphoreType.DMA(())]


def _mesh_pos():
    return lax.axis_index("x"), lax.axis_index("y"), lax.axis_index("c")


class _RowGather:
    def __init__(self, src_ref, full_ref, rows, send_sems, recv_sems, local_sems):
        self.src, self.full, self.rows = src_ref, full_ref, rows
        self.send, self.recv, self.local = send_sems, recv_sems, local_sems
        x, y, c = _mesh_pos()
        self.c, self.me, self.sibling = c, (x, y, c), (x, y, 1 - c)
        self.chips = [(1 - x, y), (x, 1 - y), (1 - x, 1 - y)]

    def _block(self, pos):
        px, py, pc = pos
        return self.full.at[pl.ds(pl.multiple_of((4 * px + 2 * py + pc) * self.rows, 16), self.rows), :]

    def _copy(self, k, blk, to):
        return pltpu.make_async_remote_copy(
            src_ref=self.src if blk is self.me else self._block(blk), dst_ref=self._block(blk),
            send_sem=self.send.at[k], recv_sem=self.recv.at[k], device_id=to, device_id_type=MESH)

    def _mine(self):
        return _place_locally(self.src, self._block(self.me), self.local, GATHER_LOCAL_CHUNKS)

    def _first(self):
        return [self._copy(0, self.me, self.sibling)] + [
            self._copy(1 + j, self.me, (*chip, self.c)) for j, chip in enumerate(self.chips)]

    def start(self):
        for cp in self._first() + self._mine():
            cp.start()

    def finish(self):
        passed = [self._copy(4 + j, (*chip, self.c), self.sibling) for j, chip in enumerate(self.chips)]
        for j, chip in enumerate(self.chips):
            self._copy(1 + j, (*chip, self.c), self.me).wait_recv()
            passed[j].start()
        self._copy(0, self.sibling, self.me).wait_recv()
        for j, chip in enumerate(self.chips):
            self._copy(4 + j, (*chip, 1 - self.c), self.me).wait_recv()
        for cp in self._first() + passed:
            cp.wait_send()
        for cp in self._mine():
            cp.wait()


class _OwnerExchange:
    def __init__(self, src_ref, dst_ref, send_sems, recv_sems, local_sem):
        self.src, self.dst, self.send, self.recv, self.local = src_ref, dst_ref, send_sems, recv_sems, local_sem
        x, y, c = _mesh_pos()
        self.c, self.my_chip = c, 2 * x + y
        self.peers = [(1 - x, y), (x, 1 - y), (1 - x, 1 - y)]

    def _copies(self):
        local = pltpu.make_async_copy(self.src.at[self.my_chip], self.dst.at[self.my_chip], self.local)
        remote = [pltpu.make_async_remote_copy(
            src_ref=self.src.at[2 * px + py], dst_ref=self.dst.at[self.my_chip],
            send_sem=self.send.at[k], recv_sem=self.recv.at[k], device_id=(px, py, self.c), device_id_type=MESH)
            for k, (px, py) in enumerate(self.peers)]
        return local, remote

    def start(self):
        local, remote = self._copies()
        local.start()
        for cp in remote:
            cp.start()

    def finish(self):
        local, remote = self._copies()
        for cp in remote:
            cp.wait_recv()
        for cp in remote:
            cp.wait_send()
        local.wait()


def _to_sibling(dw, rows, name):
    def body(dw_ref, r_ref, send_sems, recv_sems):
        x, y, c = _mesh_pos()
        copies = [pltpu.make_async_remote_copy(
            src_ref=dw_ref.at[pl.ds(pl.multiple_of((2 * j + (1 - c)) * rows, 16), rows), :], dst_ref=r_ref.at[j],
            send_sem=send_sems.at[j], recv_sem=recv_sems.at[j], device_id=(x, y, 1 - c), device_id_type=MESH)
            for j in range(N_CHIPS)]
        for cp in copies:
            cp.start()
        for cp in copies:
            cp.wait_recv()
        for cp in copies:
            cp.wait_send()

    return pl.pallas_call(
        body, name=name, in_specs=[HBM_SPEC], out_specs=HBM_SPEC,
        out_shape=jax.ShapeDtypeStruct((N_CHIPS, rows, dw.shape[1]), BF16),
        scratch_shapes=[pltpu.SemaphoreType.DMA((N_CHIPS,)), pltpu.SemaphoreType.DMA((N_CHIPS,))],
    )(dw)


PAIR_CHUNKS = 4


def _pair_halves(gt, go):
    def body(gt_in, go_in, gt_ref, go_ref, send_sems, recv_sems):
        del gt_in, go_in
        x, y, c = _mesh_pos()
        copies = []
        for a, (ref, rows) in enumerate(((gt_ref, W_IN_ROWS), (go_ref, W_OUT_ROWS))):
            ch = rows // PAIR_CHUNKS
            for q in range(PAIR_CHUNKS):
                part = ref.at[pl.ds(pl.multiple_of(c * rows + q * ch, 8), ch), :]
                copies.append(pltpu.make_async_remote_copy(
                    src_ref=part, dst_ref=part, send_sem=send_sems.at[PAIR_CHUNKS * a + q],
                    recv_sem=recv_sems.at[PAIR_CHUNKS * a + q], device_id=(x, y, 1 - c), device_id_type=MESH))
        for cp in copies:
            cp.start()
        for cp in copies:
            cp.wait_recv()
        for cp in copies:
            cp.wait_send()

    return pl.pallas_call(
        body, name="pair_halves",
        in_specs=[HBM_SPEC, HBM_SPEC], out_specs=[HBM_SPEC, HBM_SPEC],
        out_shape=[jax.ShapeDtypeStruct(gt.shape, F32), jax.ShapeDtypeStruct(go.shape, F32)],
        input_output_aliases={0: 0, 1: 1},
        scratch_shapes=[pltpu.SemaphoreType.DMA((2 * PAIR_CHUNKS,)), pltpu.SemaphoreType.DMA((2 * PAIR_CHUNKS,))],
    )(gt, go)


VEC = (("g_pre", 2048), ("g_post", 2048), ("b_qkv", 1280), ("ln_v_g", 1024), ("ln_v_b", 1024), ("attn_sinks", 16))
VEC_ROWS = 8
LOSS_ROW = len(VEC)
MAT = (("w_spatial", NG * T), ("b_spatial", NG))
MAT_ROWS = sum(r for _, r in MAT)


def _small_update(vec_grads, loss_part, mat_grads, vec_state, mat_state):
    n_vec, n_mat = len(VEC), len(MAT)
    n_par = n_vec + n_mat
    n_in = n_par + 1 + 3 * n_par

    def body(*refs):
        g_in, loss_in = refs[:n_par], refs[n_par]
        st_in = refs[n_par + 1:n_in]
        outs, loss_out = refs[n_in:n_in + 4 * n_par], refs[n_in + 4 * n_par]
        vbuf, mbuf, tot_v, tot_m, send_sems, recv_sems = refs[n_in + 4 * n_par + 1:]
        x, y, c = _mesh_pos()
        me, sibling = (x, y, c), (x, y, 1 - c)
        chips = [(1 - x, y), (x, 1 - y), (1 - x, 1 - y)]
        my_id = 4 * x + 2 * y + c

        vbuf[my_id] = jnp.zeros((VEC_ROWS, D), F32)
        for r, (_, n) in enumerate(VEC):
            vbuf[my_id, r:r + 1, 0:n] = g_in[r][...]
        vbuf[my_id, LOSS_ROW:LOSS_ROW + 1, 0:128] = loss_in[...]
        r0 = 0
        for q, (_, rows) in enumerate(MAT):
            mbuf[my_id, r0:r0 + rows, :] = g_in[n_vec + q][...]
            r0 += rows

        def copy(a, k, blk, to):
            buf = (vbuf, mbuf)[a]
            px, py, pc = blk
            slot = buf.at[4 * px + 2 * py + pc]
            return pltpu.make_async_remote_copy(
                src_ref=slot, dst_ref=slot, send_sem=send_sems.at[7 * a + k], recv_sem=recv_sems.at[7 * a + k],
                device_id=to, device_id_type=MESH)

        first = [copy(a, 0, me, sibling) for a in range(2)]
        first += [copy(a, 1 + j, me, (*chip, c)) for a in range(2) for j, chip in enumerate(chips)]
        for cp in first:
            cp.start()
        passed = []
        for a in range(2):
            for j, chip in enumerate(chips):
                copy(a, 1 + j, (*chip, c), me).wait_recv()
                passed.append(copy(a, 4 + j, (*chip, c), sibling))
                passed[-1].start()
        for a in range(2):
            copy(a, 0, sibling, me).wait_recv()
            for j, chip in enumerate(chips):
                copy(a, 4 + j, (*chip, 1 - c), me).wait_recv()
        for cp in first + passed:
            cp.wait_send()

        tv, tm = vbuf[0], mbuf[0]
        for d in range(1, N_DEV):
            tv, tm = tv + vbuf[d], tm + mbuf[d]
        tot_v[...] = tv
        tot_m[...] = tm
        loss_out[...] = tot_v[LOSS_ROW:LOSS_ROW + 1, 0:128]
        r0 = 0
        for q in range(n_par):
            if q < n_vec:
                g = tot_v[q:q + 1, 0:VEC[q][1]]
            else:
                rows = MAT[q - n_vec][1]
                g = tot_m[r0:r0 + rows, :]
                r0 += rows
            w, m, v = (st_in[3 * q + t][...] for t in range(3))
            outs[4 * q][...] = g
            outs[4 * q + 1][...], outs[4 * q + 2][...], outs[4 * q + 3][...] = _adamw_math(w, g, m, v)

    grads = list(vec_grads) + list(mat_grads)
    state = [a for wmv in list(vec_state) + list(mat_state) for a in wmv]
    vmem = pl.BlockSpec(memory_space=pltpu.VMEM)
    out_shape = [jax.ShapeDtypeStruct(g.shape, F32) for g in grads for _ in range(4)]
    out_shape.append(jax.ShapeDtypeStruct((1, 128), F32))
    res = pl.pallas_call(
        body, name="small_update",
        in_specs=[vmem] * n_in, out_specs=[vmem] * len(out_shape), out_shape=out_shape,
        scratch_shapes=[pltpu.VMEM((N_DEV, VEC_ROWS, D), F32), pltpu.VMEM((N_DEV, MAT_ROWS, 128), F32),
                        pltpu.VMEM((VEC_ROWS, D), F32), pltpu.VMEM((MAT_ROWS, 128), F32),
                        pltpu.SemaphoreType.DMA((14,)), pltpu.SemaphoreType.DMA((14,))],
        compiler_params=pltpu.CompilerParams(vmem_limit_bytes=VMEM_LIMIT),
    )(*grads, loss_part, *state)
    return [res[4 * q:4 * q + 4] for q in range(n_par)], res[-1]


def kernel(x, positions, g_pre, w_in, b_qkv, ln_v_g, ln_v_b, w_spatial, b_spatial, attn_sinks, w_out, g_post, loss_target, m_g_pre, m_w_in, m_b_qkv, m_ln_v_g, m_ln_v_b, m_w_spatial, m_b_spatial, m_attn_sinks, m_w_out, m_g_post, v_g_pre, v_w_in, v_b_qkv, v_ln_v_g, v_ln_v_b, v_w_spatial, v_b_spatial, v_attn_sinks, v_w_out, v_g_post):
    S = x.shape[1]
    c = lax.axis_index("c")
    c_arr = jnp.reshape(c, (1,)).astype(jnp.int32)
    x2 = x[0]
    target = loss_target[0]
    pos = positions.reshape(S, 1)
    half = HD // 2
    inv_freq = ROPE_THETA ** (-jnp.arange(half, dtype=F32) * (2.0 / HD))
    invf = jnp.tile(inv_freq, 128 // half).reshape(1, 128)
    bias = jnp.concatenate([jnp.zeros((OFF_Q,), F32), b_qkv[0], jnp.zeros((D_IN - OFF_ZB,), F32)]).reshape(1, D_IN)
    b_s_col = b_spatial[0].reshape(NG, T, 1)
    sinks = jnp.repeat(attn_sinks[0], T).reshape(NQ * T, 1)

    chip = 2 * lax.axis_index("x") + lax.axis_index("y")
    wt_part = lax.dynamic_slice_in_dim(w_in[0].T.astype(BF16), c * W_IN_ROWS, W_IN_ROWS, axis=0)
    wo_part = lax.dynamic_slice_in_dim(w_out[0].astype(BF16), c * W_OUT_ROWS, W_OUT_ROWS, axis=0)
    sched = jnp.asarray(PROJ_SCHEDULE, jnp.int32)[chip]
    cos, sin = _rope_table(pos, invf)

    h, proj, wt = _prenorm_inproj(sched, x2, g_pre, bias, wt_part)
    ycat, wo = _mid_fwd(proj, cos, sin, ln_v_g, ln_v_b, w_spatial[0], b_s_col, sinks, wo_part)
    dy, dout, loss_part, dg_post = _outproj_loss(ycat, wo, x2, target, g_post)

    dycat = _dycat(dy, wo)
    dwo = _matmul_tn(ycat, dy, 512, "dw_out")
    po = _presum(c_arr, dwo, _to_sibling(dwo, W_OUT_ROWS, "to_sibling_out"), W_OUT_ROWS)
    dproj, dln_g, dln_b, dws, dbs, dsink, dbqkv, bo = _mid_bwd(
        proj, dycat, cos, sin, ln_v_g, ln_v_b, w_spatial[0], b_s_col, sinks, po)
    dwt = _matmul_tn(dproj, h, 768, "dw_in_t")
    pt = _presum(c_arr, dwt, _to_sibling(dwt, W_IN_ROWS, "to_sibling_in"), W_IN_ROWS)
    grad_x, dg_pre, bt = _dh_prenorm_bwd(dproj, wt, x2, dout, g_pre, pt)
    gt, go = _pair_halves(_sum_chips(c_arr, bt, "sum_chips_in"), _sum_chips(c_arr, bo, "sum_chips_out"))

    g_w_in, d_w_in, nm_w_in, nv_w_in = (a.T for a in _adamw(w_in[0].T, gt, m_w_in[0].T, v_w_in[0].T, "adamw_w_in"))
    g_w_out, d_w_out, nm_w_out, nv_w_out = _adamw(w_out[0], go, m_w_out[0], v_w_out[0], "adamw_w_out")

    state = {"g_pre": (g_pre, m_g_pre, v_g_pre), "g_post": (g_post, m_g_post, v_g_post),
             "b_qkv": (b_qkv, m_b_qkv, v_b_qkv), "ln_v_g": (ln_v_g, m_ln_v_g, v_ln_v_g),
             "ln_v_b": (ln_v_b, m_ln_v_b, v_ln_v_b), "attn_sinks": (attn_sinks, m_attn_sinks, v_attn_sinks),
             "w_spatial": tuple(a.reshape(NG * T, T) for a in (w_spatial, m_w_spatial, v_w_spatial)),
             "b_spatial": tuple(a.reshape(NG, T) for a in (b_spatial, m_b_spatial, v_b_spatial))}
    grads = {"g_pre": dg_pre, "g_post": dg_post, "b_qkv": dbqkv, "ln_v_g": dln_g, "ln_v_b": dln_b,
             "attn_sinks": dsink[:, 0].reshape(1, NQ), "w_spatial": dws.reshape(NG * T, T), "b_spatial": dbs.reshape(NG, T)}
    results, loss = _small_update([grads[n] for n, _ in VEC], loss_part, [grads[n] for n, _ in MAT],
                                  [state[n] for n, _ in VEC], [state[n] for n, _ in MAT])
    small = {n: [a.reshape(w.shape) for a in res]
             for (n, _), res, w in zip(VEC + MAT, results, [state[n][0] for n, _ in VEC + MAT])}
    small["w_spatial"] = [a.reshape(w_spatial.shape) for a in small["w_spatial"]]
    small["b_spatial"] = [a.reshape(b_spatial.shape) for a in small["b_spatial"]]
    big = {"w_in": [a[None] for a in (g_w_in, d_w_in, nm_w_in, nv_w_in)],
           "w_out": [a[None] for a in (g_w_out, d_w_out, nm_w_out, nv_w_out)]}
    order = ("g_pre", "w_in", "b_qkv", "ln_v_g", "ln_v_b", "w_spatial", "b_spatial", "attn_sinks", "w_out", "g_post")
    leaves = {**small, **big}
    return (loss[0, 0], grad_x[None], *[leaves[n][t] for t in range(4) for n in order])
```

```python
import jax
import jax.numpy as jnp
from jax import lax
from jax.experimental import pallas as pl
from jax.experimental.pallas import tpu as pltpu

F32 = jnp.float32
BF16 = jnp.bfloat16
MESH = pl.DeviceIdType.MESH

D = 2048
DG = 1024
T = 128
NG = 8
HD = 64
NQ = 16
D_IN = 5376
OFF_U, OFF_V, OFF_ZA, OFF_Q, OFF_K, OFF_VA, OFF_ZB = 0, 1024, 2048, 3072, 4096, 4224, 4352
D_QKV = 1280
EPS = 1e-6
ROPE_THETA = 10000.0
N_CHIPS = 4
N_DEV = 8
W_IN_ROWS = D_IN // N_DEV
W_OUT_ROWS = D // N_DEV

ADAM_LR, ADAM_B1, ADAM_B2, ADAM_EPS, ADAM_WD, ADAM_STEP = 0.001, 0.9, 0.999, 1e-08, 0.01, 10

VMEM_LIMIT = 56 * 1024 * 1024


def _tile(n, pref):
    return pref if n % pref == 0 else n


def _params(sem=None, vmem=VMEM_LIMIT):
    return pltpu.CompilerParams(dimension_semantics=sem, vmem_limit_bytes=vmem)


def _sigmoid(z):
    return 1.0 / (1.0 + jnp.exp(-z))


def _dot(a, b, dims):
    return lax.dot_general(a, b, (dims, ((), ())), preferred_element_type=F32)


NN = ((1,), (0,))
NT = ((1,), (1,))
TN = ((0,), (0,))


PROJ_TN = 768
N_PROJ_TILES = D_IN // PROJ_TN
PROJ_SCHEDULE = ((0, 2, 4, 1, 3, 6, 5, 5), (2, 0, 6, 1, 4, 3, 5, 4), (4, 6, 0, 5, 2, 1, 3, 4), (6, 4, 2, 5, 3, 0, 1, 5))


LOCAL_CHUNKS = 6


def _place_locally(src_ref, dst_rows_ref, sems, n_chunks=LOCAL_CHUNKS):
    ch = src_ref.shape[0] // n_chunks
    return [pltpu.make_async_copy(src_ref.at[pl.ds(q * ch, ch), :], dst_rows_ref.at[pl.ds(q * ch, ch), :], sems.at[q])
            for q in range(n_chunks)]


def _prenorm_inproj(sched, x, g, bias, wt_part):
    S = x.shape[0]
    tp, tm = _tile(S, 512), _tile(S, 1024)
    n_pre, ns = S // tp, S // tm
    n_steps = n_pre + N_PROJ_TILES * ns
    pos_of = lambda i: jnp.maximum(i - n_pre, 0) // ns
    row_of = lambda i: jnp.maximum(i - n_pre, 0) % ns

    def body(sched_ref, x_ref, g_ref, b_ref, wpart_ref, h_ref, proj_ref, wt_ref, h_all, w_tile, stage, send_sems,
             recv_sems, w_sems, local_sems):
        i = pl.program_id(0)
        x_, y_, c = _mesh_pos()
        me, sibling = (x_, y_, c), (x_, y_, 1 - c)
        chips = [(1 - x_, y_), (x_, 1 - y_), (1 - x_, 1 - y_)]

        def block(pos):
            px, py, pc = pos
            return wt_ref.at[pl.ds(pl.multiple_of((4 * px + 2 * py + pc) * W_IN_ROWS, 16), W_IN_ROWS), :]

        def copy(k, blk, to):
            return pltpu.make_async_remote_copy(
                src_ref=wpart_ref if blk is me else block(blk), dst_ref=block(blk),
                send_sem=send_sems.at[k], recv_sem=recv_sems.at[k], device_id=to, device_id_type=MESH)

        stage_in = pltpu.make_async_copy(wpart_ref, stage, local_sems.at[0])
        stage_out = pltpu.make_async_copy(stage, block(me), local_sems.at[1])

        relay = (jnp.where(c == 0, x_, 1 - x_), jnp.where(c == 0, 1 - y_, y_))
        relayed = (jnp.where(c == 0, 1 - x_, x_), jnp.where(c == 0, y_, 1 - y_))

        def own_sends():
            return [copy(0, me, sibling), copy(2, me, (*chips[1], c)), copy(1, me, (*chips[0], c))]

        def passed_on():
            return [copy(4, (*chips[0], c), sibling), copy(5, (*chips[1], c), sibling), copy(3, (*relayed, c), (*relay, c))]

        def neighbours_arrive():
            copy(1, (*chips[0], c), me).wait_recv()
            copy(2, (*chips[1], c), me).wait_recv()
            for cp in passed_on():
                cp.start()
            copy(5, (*chips[1], 1 - c), me).wait_recv()

        def diagonal_arrives():
            copy(3, (*chips[2], c), me).wait_recv()
            copy(6, (*chips[2], c), sibling).start()
            copy(6, (*chips[2], 1 - c), me).wait_recv()

        def tile_load(p):
            slot = p % 2
            rows = wt_ref.at[pl.ds(pl.multiple_of(sched_ref[p] * PROJ_TN, 16), PROJ_TN), :]
            return pltpu.make_async_copy(rows, w_tile.at[slot], w_sems.at[slot])

        def prepare(p):
            p = jnp.asarray(p, jnp.int32)

            @pl.when(p == 0)
            def _():
                copy(0, sibling, me).wait_recv()
                stage_out.wait()

            pl.when(p == 1)(neighbours_arrive)
            pl.when(p == 2)(lambda: copy(4, (*chips[0], 1 - c), me).wait_recv())
            pl.when(p == sched_ref[N_PROJ_TILES])(diagonal_arrives)
            tile_load(p).start()

        @pl.when(i == 0)
        def _():
            stage_in.start()
            for cp in own_sends():
                cp.start()
            stage_in.wait()
            stage_out.start()

        @pl.when(i < n_pre)
        def _():
            xv = x_ref[...]
            r = lax.rsqrt(jnp.mean(xv * xv, axis=-1, keepdims=True) + EPS)
            hv = (xv * r * g_ref[...]).astype(BF16)
            h_ref[...] = hv
            h_all[pl.ds(pl.multiple_of(i * tp, tp), tp), :] = hv

        pl.when(i == n_pre - 1)(lambda: prepare(0))

        @pl.when(i >= n_pre)
        def _():
            p, s = pos_of(i), row_of(i)
            pl.when(s == 0)(lambda: tile_load(p).wait())
            pl.when((s == ns - 1) & (p < N_PROJ_TILES - 1))(lambda: prepare(p + 1))
            hv = h_all[pl.ds(pl.multiple_of(s * tm, tm), tm), :]
            proj_ref[...] = _dot(hv, w_tile[p % 2], NT) + b_ref[...]

        @pl.when(i == n_steps - 1)
        def _():
            for cp in own_sends() + passed_on() + [copy(6, (*chips[2], c), sibling)]:
                cp.wait_send()

    return pl.pallas_call(
        body, name="prenorm_inproj",
        grid_spec=pltpu.PrefetchScalarGridSpec(
            num_scalar_prefetch=1, grid=(n_steps,),
            in_specs=[pl.BlockSpec((tp, D), lambda i, sc: (jnp.minimum(i, n_pre - 1), 0)),
                      pl.BlockSpec((1, D), lambda i, sc: (0, 0)),
                      pl.BlockSpec((1, PROJ_TN), lambda i, sc: (0, sc[pos_of(i)])),
                      HBM_SPEC],
            out_specs=[pl.BlockSpec((tp, D), lambda i, sc: (jnp.minimum(i, n_pre - 1), 0)),
                       pl.BlockSpec((tm, PROJ_TN), lambda i, sc: (row_of(i), sc[pos_of(i)])),
                       HBM_SPEC],
            scratch_shapes=[pltpu.VMEM((S, D), BF16), pltpu.VMEM((2, PROJ_TN, D), BF16),
                            pltpu.VMEM((W_IN_ROWS, D), BF16),
                            pltpu.SemaphoreType.DMA((7,)), pltpu.SemaphoreType.DMA((7,)),
                            pltpu.SemaphoreType.DMA((2,)), pltpu.SemaphoreType.DMA((2,))]),
        out_shape=[jax.ShapeDtypeStruct((S, D), BF16), jax.ShapeDtypeStruct((S, D_IN), F32),
                   jax.ShapeDtypeStruct((D_IN, D), BF16)],
        compiler_params=_params(("arbitrary",)),
    )(sched, x, g, bias, wt_part)


def _rope_table(pos, invf):
    S = pos.shape[0]
    ts = _tile(S, 1024)

    def body(pos_ref, invf_ref, cos_ref, sin_ref):
        ang = pos_ref[...].astype(F32) * invf_ref[...]
        cos_ref[...] = jnp.cos(ang)
        sin_ref[...] = jnp.sin(ang)

    return pl.pallas_call(
        body, name="rope_table", grid=(S // ts,),
        in_specs=[pl.BlockSpec((ts, 1), lambda i: (i, 0)), pl.BlockSpec((1, 128), lambda i: (0, 0))],
        out_specs=[pl.BlockSpec((ts, 128), lambda i: (i, 0))] * 2,
        out_shape=[jax.ShapeDtypeStruct((S, 128), F32)] * 2,
        compiler_params=_params(("parallel",)),
    )(pos, invf)


def _rot_half(xs, first_half):
    return jnp.where(first_half, -pltpu.roll(xs, 96, 1), pltpu.roll(xs, 32, 1))


GH = NQ // 2


MASKED = -1e30


def _attn_consts():
    lane = lax.broadcasted_iota(jnp.int32, (T, 128), 1)
    row = lax.broadcasted_iota(jnp.int32, (GH * T, T), 0) & (T - 1)
    on_diag_or_below = row >= lax.broadcasted_iota(jnp.int32, (GH * T, T), 1)
    return (lane & (HD - 1)) < (HD // 2), lane < HD, on_diag_or_below


def _stack_heads(slab_fn, grp, lo64):
    blocks = []
    for jj in range(GH // 2):
        s = slab_fn(GH // 2 * grp + jj)
        r = pltpu.roll(s, HD, 1)
        if grp == 0:
            blocks += [jnp.where(lo64, s, 0.0), jnp.where(lo64, r, 0.0)]
        else:
            blocks += [jnp.where(lo64, 0.0, r), jnp.where(lo64, 0.0, s)]
    return jnp.concatenate(blocks, axis=0)


def _unstack_heads(stacked, jj, grp, lo64):
    a = stacked[(2 * jj) * T:(2 * jj + 1) * T]
    b = stacked[(2 * jj + 1) * T:(2 * jj + 2) * T]
    if grp == 0:
        return jnp.where(lo64, a, pltpu.roll(b, HD, 1))
    return jnp.where(lo64, pltpu.roll(a, HD, 1), b)


def _layer_norm_stats(v):
    mu = jnp.mean(v, axis=-1, keepdims=True)
    xc = v - mu
    var = jnp.mean(xc * xc, axis=-1, keepdims=True)
    rs = lax.rsqrt(var + EPS)
    return xc * rs, rs


def _band_softmax(q_scaled, k_prev, k_cur, own, has_prev, sink):
    s_prev = _dot(q_scaled, k_prev, NT)
    if has_prev is not None:
        s_prev = s_prev + jnp.where(has_prev, 0.0, MASKED)
    s = jnp.where(own, _dot(q_scaled, k_cur, NT), s_prev)
    m = jnp.maximum(jnp.max(s, axis=-1, keepdims=True), sink)
    e = jnp.exp(s - m)
    es = jnp.exp(sink - m)
    inv = 1.0 / (jnp.sum(e, axis=-1, keepdims=True) + es)
    return e * inv, es * inv


def _unfold(p, own):
    return jnp.where(own, p, 0.0).astype(BF16), jnp.where(own, 0.0, p).astype(BF16)


BPS_FWD = 4
BPS_BWD = 2


def _mid_specs(nt, rev, bps):
    tile = (lambda i: nt - 1 - i) if rev else (lambda i: i)
    prev = lambda i: jnp.maximum(bps * tile(i) - 1, 0)
    rows = bps * T
    return tile, [
        pl.BlockSpec((rows, D_IN), lambda i: (tile(i), 0)),
        pl.BlockSpec((T, 2 * T), lambda i: (prev(i), OFF_K // (2 * T))),
    ], [
        pl.BlockSpec((rows, 128), lambda i: (tile(i), 0)),
        pl.BlockSpec((rows, 128), lambda i: (tile(i), 0)),
        pl.BlockSpec((T, 128), lambda i: (prev(i), 0)),
        pl.BlockSpec((T, 128), lambda i: (prev(i), 0)),
        pl.BlockSpec((1, DG), lambda i: (0, 0)),
        pl.BlockSpec((1, DG), lambda i: (0, 0)),
        pl.BlockSpec((NG, T, T), lambda i: (0, 0, 0)),
        pl.BlockSpec((NG, T, 1), lambda i: (0, 0, 0)),
        pl.BlockSpec((NQ * T, 1), lambda i: (0, 0)),
    ]


def _rope(xs, cosv, sinv, first_half):
    return xs * cosv + _rot_half(xs, first_half) * sinv


def _mid_fwd(proj, cos, sin, ln_g, ln_b, w_s, b_s, sinks, wo_part):
    S = proj.shape[0]
    BPS = BPS_FWD if S % (BPS_FWD * T) == 0 else 1
    nt = S // (BPS * T)
    tile, proj_specs, par_specs = _mid_specs(nt, False, BPS)

    def body(p_ref, kvp_ref, cos_ref, sin_ref, cosp_ref, sinp_ref, lng_ref, lnb_ref, ws_ref, bs_ref, sink_ref, wpart_ref,
             y_ref, wo_ref, send_sems, recv_sems, local_sems):
        i = pl.program_id(0)
        first_half, lo64, own = _attn_consts()
        gather = _RowGather(wpart_ref, wo_ref, W_OUT_ROWS, send_sems, recv_sems, local_sems)
        pl.when(i == 0)(gather.start)
        tril = lax.broadcasted_iota(jnp.int32, (T, T), 0) >= lax.broadcasted_iota(jnp.int32, (T, T), 1)

        def block(b, k_prev, v_prev, has_prev):
            rows = slice(b * T, (b + 1) * T)
            xhat, _ = _layer_norm_stats(p_ref[rows, OFF_V:OFF_V + DG])
            vn = xhat * lng_ref[...] + lnb_ref[...]
            for g in range(NG):
                sl = slice(128 * g, 128 * g + 128)
                wm = jnp.where(tril, ws_ref[g], 0.0).astype(BF16)
                mixed = _dot(wm, vn[:, sl].astype(BF16), NN) + bs_ref[g]
                z = p_ref[rows, OFF_ZA + 128 * g:OFF_ZA + 128 * g + 128]
                u = p_ref[rows, OFF_U + 128 * g:OFF_U + 128 * g + 128]
                y_ref[rows, sl] = (u * mixed * (z * _sigmoid(z))).astype(BF16)

            cosv, sinv = cos_ref[rows, :], sin_ref[rows, :]
            k_cur = _rope(p_ref[rows, OFF_K:OFF_K + 128], cosv, sinv, first_half).astype(BF16)
            v_cur = p_ref[rows, OFF_VA:OFF_VA + 128].astype(BF16)

            def q_slab(j):
                return _rope(p_ref[rows, OFF_Q + 128 * j:OFF_Q + 128 * j + 128], cosv, sinv, first_half) * (HD ** -0.5)

            for grp in range(2):
                qs = _stack_heads(q_slab, grp, lo64).astype(BF16)
                p, _ = _band_softmax(qs, k_prev, k_cur, own, has_prev, sink_ref[GH * T * grp:GH * T * (grp + 1), :])
                p_cur, p_prev = _unfold(p, own)
                o_st = _dot(p_cur, v_cur, NN) + _dot(p_prev, v_prev, NN)
                for jj in range(GH // 2):
                    c0 = 128 * (GH // 2 * grp + jj)
                    zb = p_ref[rows, OFF_ZB + c0:OFF_ZB + c0 + 128]
                    o = _unstack_heads(o_st, jj, grp, lo64)
                    y_ref[rows, DG + c0:DG + c0 + 128] = (o * (zb * _sigmoid(zb))).astype(BF16)
            return k_cur, v_cur

        k_prev = _rope(kvp_ref[:, 0:128], cosp_ref[...], sinp_ref[...], first_half).astype(BF16)
        kv = block(0, k_prev, kvp_ref[:, 128:256].astype(BF16), i > 0)
        for b in range(1, BPS):
            kv = block(b, *kv, None)
        pl.when(i == nt - 1)(gather.finish)

    return pl.pallas_call(
        body, name="mid_fwd", grid=(nt,),
        in_specs=proj_specs + par_specs + [HBM_SPEC],
        out_specs=[pl.BlockSpec((BPS * T, D), lambda i: (tile(i), 0)), HBM_SPEC],
        out_shape=[jax.ShapeDtypeStruct((S, D), BF16), jax.ShapeDtypeStruct((D, D), BF16)],
        scratch_shapes=GATHER_SEMS,
        compiler_params=_params(("arbitrary",)),
    )(proj, proj, cos, sin, cos, sin, ln_g, ln_b, w_s, b_s, sinks, wo_part)


def _mid_bwd(proj, dycat, cos, sin, ln_g, ln_b, w_s, b_s, sinks, po):
    S = proj.shape[0]
    BPS = BPS_BWD if S % (BPS_BWD * T) == 0 else 1
    nt = S // (BPS * T)
    tile, proj_specs, par_specs = _mid_specs(nt, True, BPS)
    const2 = lambda i: (0, 0)

    def body(p_ref, kvp_ref, dyc_ref, cos_ref, sin_ref, cosp_ref, sinp_ref, lng_ref, lnb_ref, ws_ref, bs_ref, sink_ref,
             po_ref, dp_ref, dlng_ref, dlnb_ref, dws_ref, dbs_ref, dsink_ref, dbqkv_ref, bo_ref,
             carry_ref, dvn_ref, send_sems, recv_sems, local_sem):
        i = pl.program_id(0)
        first_half, lo64, own = _attn_consts()
        tril = lax.broadcasted_iota(jnp.int32, (T, T), 0) >= lax.broadcasted_iota(jnp.int32, (T, T), 1)
        exchange = _OwnerExchange(po_ref, bo_ref, send_sems, recv_sems, local_sem)
        pl.when(i == 0)(exchange.start)

        @pl.when(i == 0)
        def _():
            dlng_ref[...] = jnp.zeros_like(dlng_ref)
            dlnb_ref[...] = jnp.zeros_like(dlnb_ref)
            dws_ref[...] = jnp.zeros_like(dws_ref)
            dbs_ref[...] = jnp.zeros_like(dbs_ref)
            dsink_ref[...] = jnp.zeros_like(dsink_ref)
            dbqkv_ref[...] = jnp.zeros_like(dbqkv_ref)
            carry_ref[...] = jnp.zeros_like(carry_ref)

        def roped_k(b):
            rows = slice(b * T, (b + 1) * T)
            return _rope(p_ref[rows, OFF_K:OFF_K + 128], cos_ref[rows, :], sin_ref[rows, :], first_half).astype(BF16)

        def block(b, k_prev, v_prev, has_prev, dk_next, dv_next):
            rows = slice(b * T, (b + 1) * T)
            xhat, rs = _layer_norm_stats(p_ref[rows, OFF_V:OFF_V + DG])
            lng = lng_ref[...]
            vn = xhat * lng + lnb_ref[...]
            for g in range(NG):
                sl = slice(128 * g, 128 * g + 128)
                wm = jnp.where(tril, ws_ref[g], 0.0).astype(BF16)
                vng = vn[:, sl].astype(BF16)
                mixed = _dot(wm, vng, NN) + bs_ref[g]
                z = p_ref[rows, OFF_ZA + 128 * g:OFF_ZA + 128 * g + 128]
                u = p_ref[rows, OFF_U + 128 * g:OFF_U + 128 * g + 128]
                dy = dyc_ref[rows, sl]
                sg = _sigmoid(z)
                sa = z * sg
                dp_ref[rows, OFF_U + 128 * g:OFF_U + 128 * g + 128] = (dy * mixed * sa).astype(BF16)
                dp_ref[rows, OFF_ZA + 128 * g:OFF_ZA + 128 * g + 128] = (
                    dy * u * mixed * (sg * (1.0 + z * (1.0 - sg)))).astype(BF16)
                dm = dy * u * sa
                dmb = dm.astype(BF16)
                dvn_ref[rows, sl] = _dot(wm, dmb, TN)
                dws_ref[g] += jnp.where(tril, _dot(dmb, vng, NT), 0.0)
                dbs_ref[g] += jnp.sum(dm, axis=1, keepdims=True)
            dvn = dvn_ref[rows, :]
            dlng_ref[...] += jnp.sum(dvn * xhat, axis=0, keepdims=True)
            dlnb_ref[...] += jnp.sum(dvn, axis=0, keepdims=True)
            dxh = dvn * lng
            dv_g = rs * (dxh - jnp.mean(dxh, axis=-1, keepdims=True)
                         - xhat * jnp.mean(dxh * xhat, axis=-1, keepdims=True))
            dp_ref[rows, OFF_V:OFF_V + DG] = dv_g.astype(BF16)

            cosv, sinv = cos_ref[rows, :], sin_ref[rows, :]
            k_cur = roped_k(b)
            v_cur = p_ref[rows, OFF_VA:OFF_VA + 128].astype(BF16)

            def q_slab(j):
                return _rope(p_ref[rows, OFF_Q + 128 * j:OFF_Q + 128 * j + 128], cosv, sinv, first_half) * (HD ** -0.5)

            def do_slab(j):
                zb = p_ref[rows, OFF_ZB + 128 * j:OFF_ZB + 128 * j + 128]
                return dyc_ref[rows, DG + 128 * j:DG + 128 * j + 128] * (zb * _sigmoid(zb))

            dk_cur, dv_cur = jnp.zeros((T, 128), F32), jnp.zeros((T, 128), F32)
            dk_prev, dv_prev = jnp.zeros((T, 128), F32), jnp.zeros((T, 128), F32)
            for grp in range(2):
                qs = _stack_heads(q_slab, grp, lo64).astype(BF16)
                d_o = _stack_heads(do_slab, grp, lo64)
                dob = d_o.astype(BF16)
                p, ps = _band_softmax(qs, k_prev, k_cur, own, has_prev, sink_ref[GH * T * grp:GH * T * (grp + 1), :])
                p_cur, p_prev = _unfold(p, own)
                o_st = _dot(p_cur, v_cur, NN) + _dot(p_prev, v_prev, NN)
                delta = jnp.sum(d_o * o_st, axis=-1, keepdims=True)
                dp_fold = jnp.where(own, _dot(dob, v_cur, NT), _dot(dob, v_prev, NT))
                ds_cur, ds_prev = _unfold(p * (dp_fold - delta), own)
                dq_st = (_dot(ds_cur, k_cur, NN) + _dot(ds_prev, k_prev, NN)) * (HD ** -0.5)
                dk_cur, dk_prev = dk_cur + _dot(ds_cur, qs, TN), dk_prev + _dot(ds_prev, qs, TN)
                dv_cur, dv_prev = dv_cur + _dot(p_cur, dob, TN), dv_prev + _dot(p_prev, dob, TN)
                dsink_rows = ps * delta
                for hh in range(GH):
                    h = GH * grp + hh
                    dsink_ref[h:h + 1, :] += jnp.broadcast_to(
                        -jnp.sum(dsink_rows[hh * T:(hh + 1) * T], axis=0, keepdims=True), (1, 128))
                for jj in range(GH // 2):
                    c0 = 128 * (GH // 2 * grp + jj)
                    zb = p_ref[rows, OFF_ZB + c0:OFF_ZB + c0 + 128]
                    sg = _sigmoid(zb)
                    o = _unstack_heads(o_st, jj, grp, lo64)
                    dp_ref[rows, OFF_ZB + c0:OFF_ZB + c0 + 128] = (
                        dyc_ref[rows, DG + c0:DG + c0 + 128] * o * (sg * (1.0 + zb * (1.0 - sg)))).astype(BF16)
                    dq = _unstack_heads(dq_st, jj, grp, lo64)
                    dq_pre = dq * cosv - _rot_half(dq, first_half) * sinv
                    dp_ref[rows, OFF_Q + c0:OFF_Q + c0 + 128] = dq_pre.astype(BF16)
                    dbqkv_ref[:, c0:c0 + 128] += jnp.sum(dq_pre, axis=0, keepdims=True)
            dk_cur, dv_cur = dk_cur + dk_next, dv_cur + dv_next
            dk_pre = dk_cur * cosv - _rot_half(dk_cur, first_half) * sinv
            dp_ref[rows, OFF_K:OFF_K + 128] = dk_pre.astype(BF16)
            dp_ref[rows, OFF_VA:OFF_VA + 128] = dv_cur.astype(BF16)
            dbqkv_ref[:, 1024:1152] += jnp.sum(dk_pre, axis=0, keepdims=True)
            dbqkv_ref[:, 1152:1280] += jnp.sum(dv_cur, axis=0, keepdims=True)
            return dk_prev, dv_prev

        grads = carry_ref[:, 0:128], carry_ref[:, 128:256]
        for b in range(BPS - 1, 0, -1):
            prows = slice((b - 1) * T, b * T)
            grads = block(b, roped_k(b - 1), p_ref[prows, OFF_VA:OFF_VA + 128].astype(BF16), None, *grads)
        k_prev = _rope(kvp_ref[:, 0:128], cosp_ref[...], sinp_ref[...], first_half).astype(BF16)
        grads = block(0, k_prev, kvp_ref[:, 128:256].astype(BF16), i < nt - 1, *grads)
        carry_ref[:, 0:128], carry_ref[:, 128:256] = grads
        pl.when(i == nt - 1)(exchange.finish)

    return pl.pallas_call(
        body, name="mid_bwd", grid=(nt,),
        in_specs=proj_specs + [pl.BlockSpec((BPS * T, D), lambda i: (tile(i), 0))] + par_specs + [HBM_SPEC],
        out_specs=[pl.BlockSpec((BPS * T, D_IN), lambda i: (tile(i), 0)),
                   pl.BlockSpec((1, DG), const2), pl.BlockSpec((1, DG), const2),
                   pl.BlockSpec((NG, T, T), lambda i: (0, 0, 0)), pl.BlockSpec((NG, T, 1), lambda i: (0, 0, 0)),
                   pl.BlockSpec((NQ, 128), const2), pl.BlockSpec((1, D_QKV), const2), HBM_SPEC],
        out_shape=[jax.ShapeDtypeStruct((S, D_IN), BF16),
                   jax.ShapeDtypeStruct((1, DG), F32), jax.ShapeDtypeStruct((1, DG), F32),
                   jax.ShapeDtypeStruct((NG, T, T), F32), jax.ShapeDtypeStruct((NG, T, 1), F32),
                   jax.ShapeDtypeStruct((NQ, 128), F32), jax.ShapeDtypeStruct((1, D_QKV), F32),
                   jax.ShapeDtypeStruct(po.shape, BF16)],
        scratch_shapes=[pltpu.VMEM((T, 2 * T), F32), pltpu.VMEM((BPS * T, DG), F32)] + OWNER_SEMS,
        compiler_params=_params(("arbitrary",)),
    )(proj, proj, dycat, cos, sin, cos, sin, ln_g, ln_b, w_s, b_s, sinks, po)


def _outproj_loss(ycat, wo, x, target, g_post):
    S = ycat.shape[0]
    tm = _tile(S, 512)
    nt = S // tm
    n_part = 2 if tm % 32 == 0 else 1
    tp = tm // n_part
    const2 = lambda i: (0, 0)

    def body(yc_ref, w_ref, x_ref, t_ref, g_ref, dy_ref, dout_ref, loss_ref, dg_ref, lacc_ref):
        i = pl.program_id(0)

        @pl.when(i == 0)
        def _():
            dg_ref[...] = jnp.zeros_like(dg_ref)
            lacc_ref[...] = jnp.zeros_like(lacc_ref)

        g = g_ref[...]
        ys = [_dot(yc_ref[q * tp:(q + 1) * tp, :], w_ref[...], NN) for q in range(n_part)]
        for q, y in enumerate(ys):
            rows = slice(q * tp, (q + 1) * tp)
            r = lax.rsqrt(jnp.mean(y * y, axis=-1, keepdims=True) + EPS)
            yh = y * r
            diff = x_ref[rows, :] + yh * g - t_ref[rows, :]
            lacc_ref[...] += jnp.sum(diff * diff, axis=0, keepdims=True)
            dout = diff * (1.0 / D)
            dout_ref[rows, :] = dout
            dg_ref[...] += jnp.sum(dout * yh, axis=0, keepdims=True)
            dyh = dout * g
            dy_ref[rows, :] = (r * (dyh - yh * jnp.mean(dyh * yh, axis=-1, keepdims=True))).astype(BF16)

        @pl.when(i == nt - 1)
        def _():
            loss_ref[...] = jnp.broadcast_to(jnp.sum(lacc_ref[...], axis=1, keepdims=True) * (0.5 / D), (1, 128))

    row = lambda i: (i, 0)
    return pl.pallas_call(
        body, name="outproj_loss", grid=(nt,),
        in_specs=[pl.BlockSpec((tm, D), row), pl.BlockSpec((D, D), const2, pipeline_mode=pl.Buffered(1)),
                  pl.BlockSpec((tm, D), row), pl.BlockSpec((tm, D), row), pl.BlockSpec((1, D), const2)],
        out_specs=[pl.BlockSpec((tm, D), row), pl.BlockSpec((tm, D), row), pl.BlockSpec((1, 128), const2),
                   pl.BlockSpec((1, D), const2)],
        out_shape=[jax.ShapeDtypeStruct((S, D), BF16), jax.ShapeDtypeStruct((S, D), F32),
                   jax.ShapeDtypeStruct((1, 128), F32), jax.ShapeDtypeStruct((1, D), F32)],
        scratch_shapes=[pltpu.VMEM((1, D), F32)],
        compiler_params=_params(("arbitrary",)),
    )(ycat, wo, x, target, g_post)


def _dycat(dy, wo):
    S = dy.shape[0]
    tm = _tile(S, 512)

    def body(dy_ref, w_ref, o_ref):
        o_ref[...] = _dot(dy_ref[...], w_ref[...], NT)

    return pl.pallas_call(
        body, name="dycat", grid=(S // tm,),
        in_specs=[pl.BlockSpec((tm, D), lambda i: (i, 0)), pl.BlockSpec((D, D), lambda i: (0, 0))],
        out_specs=pl.BlockSpec((tm, D), lambda i: (i, 0)),
        out_shape=jax.ShapeDtypeStruct((S, D), F32),
        compiler_params=_params(("parallel",)),
    )(dy, wo)


def _matmul_tn(a, b, tm, name):
    K, M = a.shape
    N = b.shape[1]
    tk = _tile(K, 1024)
    nk = K // tk

    def body(a_ref, b_ref, o_ref, acc_ref):
        k = pl.program_id(1)

        @pl.when(k == 0)
        def _():
            acc_ref[...] = jnp.zeros_like(acc_ref)

        acc_ref[...] += _dot(a_ref[...], b_ref[...], TN)

        @pl.when(k == nk - 1)
        def _():
            o_ref[...] = acc_ref[...].astype(BF16)

    return pl.pallas_call(
        body, name=name, grid=(M // tm, nk),
        in_specs=[pl.BlockSpec((tk, tm), lambda i, k: (k, i)), pl.BlockSpec((tk, N), lambda i, k: (k, 0))],
        out_specs=pl.BlockSpec((tm, N), lambda i, k: (i, 0)),
        out_shape=jax.ShapeDtypeStruct((M, N), BF16),
        scratch_shapes=[pltpu.VMEM((tm, N), F32)],
        compiler_params=_params(("parallel", "arbitrary")),
    )(a, b)


def _dh_prenorm_bwd(dproj, wt, x, dout, g_pre, pt):
    S = x.shape[0]
    tm, tk = _tile(S, 1024), 768
    te = tm // 2
    ni, nk, ne = S // tm, D_IN // tk, tm // te
    per = nk + ne
    tile_of, phase_of = (lambda t: t // per), (lambda t: t % per)
    k_of = lambda t: jnp.minimum(phase_of(t), nk - 1)
    half_of = lambda t: tile_of(t) * ne + jnp.maximum(phase_of(t) - nk, 0)

    def body(dp_ref, w_ref, x_ref, dout_ref, g_ref, pt_ref, gx_ref, dg_ref, bt_ref, acc_ref,
             send_sems, recv_sems, local_sem):
        t = pl.program_id(0)
        ph = phase_of(t)
        exchange = _OwnerExchange(pt_ref, bt_ref, send_sems, recv_sems, local_sem)
        pl.when(t == 0)(exchange.start)

        @pl.when(t == 0)
        def _():
            dg_ref[...] = jnp.zeros_like(dg_ref)

        @pl.when(ph == 0)
        def _():
            acc_ref[...] = _dot(dp_ref[...], w_ref[...], NN)

        @pl.when((ph > 0) & (ph < nk))
        def _():
            acc_ref[...] += _dot(dp_ref[...], w_ref[...], NN)

        @pl.when(ph >= nk)
        def _():
            dh = acc_ref[pl.ds(pl.multiple_of((ph - nk) * te, te), te), :]
            xv = x_ref[...]
            r = lax.rsqrt(jnp.mean(xv * xv, axis=-1, keepdims=True) + EPS)
            xh = xv * r
            dg_ref[...] += jnp.sum(dh * xh, axis=0, keepdims=True)
            dxh = dh * g_ref[...]
            gx_ref[...] = dout_ref[...] + r * (dxh - xh * jnp.mean(dxh * xh, axis=-1, keepdims=True))

        pl.when(t == ni * per - 1)(exchange.finish)

    half = lambda t: (half_of(t), 0)
    return pl.pallas_call(
        body, name="dh_prenorm_bwd", grid=(ni * per,),
        in_specs=[pl.BlockSpec((tm, tk), lambda t: (tile_of(t), k_of(t))), pl.BlockSpec((tk, D), lambda t: (k_of(t), 0)),
                  pl.BlockSpec((te, D), half), pl.BlockSpec((te, D), half), pl.BlockSpec((1, D), lambda t: (0, 0)),
                  HBM_SPEC],
        out_specs=[pl.BlockSpec((te, D), half), pl.BlockSpec((1, D), lambda t: (0, 0)), HBM_SPEC],
        out_shape=[jax.ShapeDtypeStruct((S, D), F32), jax.ShapeDtypeStruct((1, D), F32),
                   jax.ShapeDtypeStruct(pt.shape, BF16)],
        scratch_shapes=[pltpu.VMEM((tm, D), F32)] + OWNER_SEMS,
        compiler_params=_params(("arbitrary",)),
    )(dproj, wt, x, dout, g_pre, pt)


def _presum(c_arr, own, recv, half_rows):
    n_cols = own.shape[-1]
    own4 = own.reshape(N_CHIPS, 2, half_rows, n_cols)

    def body(c_ref, own_ref, recv_ref, o_ref):
        o_ref[...] = (own_ref[...].astype(F32) + recv_ref[...].astype(F32)).astype(BF16)

    return pl.pallas_call(
        body, name="presum_%d" % half_rows,
        grid_spec=pltpu.PrefetchScalarGridSpec(
            num_scalar_prefetch=1, grid=(N_CHIPS,),
            in_specs=[pl.BlockSpec((None, None, half_rows, n_cols), lambda j, c: (j, c[0], 0, 0)),
                      pl.BlockSpec((None, half_rows, n_cols), lambda j, c: (j, 0, 0))],
            out_specs=pl.BlockSpec((None, half_rows, n_cols), lambda j, c: (j, 0, 0))),
        out_shape=jax.ShapeDtypeStruct((N_CHIPS, half_rows, n_cols), BF16),
        compiler_params=_params(("parallel",)),
    )(c_arr, own4, recv)


def _sum_chips(c_arr, parts, name):
    _, rows, n_cols = parts.shape
    nt = 2
    tr = rows // nt

    def body(c_ref, p_ref, o_ref):
        o_ref[...] = ((p_ref[0].astype(F32) + p_ref[1].astype(F32)) + p_ref[2].astype(F32)) + p_ref[3].astype(F32)

    return pl.pallas_call(
        body, name=name,
        grid_spec=pltpu.PrefetchScalarGridSpec(
            num_scalar_prefetch=1, grid=(nt,),
            in_specs=[pl.BlockSpec((N_CHIPS, tr, n_cols), lambda i, c: (0, i, 0))],
            out_specs=pl.BlockSpec((tr, n_cols), lambda i, c: (c[0] * nt + i, 0))),
        out_shape=jax.ShapeDtypeStruct((2 * rows, n_cols), F32),
        compiler_params=_params(("parallel",)),
    )(c_arr, parts)


def _adamw_math(w, g, m, v):
    mn = ADAM_B1 * m + (1.0 - ADAM_B1) * g
    vn = ADAM_B2 * v + (1.0 - ADAM_B2) * (g * g)
    m_hat = mn / (1.0 - ADAM_B1 ** ADAM_STEP)
    v_hat = vn / (1.0 - ADAM_B2 ** ADAM_STEP)
    return -ADAM_LR * (m_hat / (jnp.sqrt(v_hat) + ADAM_EPS) + ADAM_WD * w), mn, vn


def _adamw(w, g, m, v, name):
    R, C = w.shape
    tr = next((t for t in (256, 192, 128) if R % t == 0), R)

    def body(w_ref, g_ref, m_ref, v_ref, go_ref, d_ref, mo_ref, vo_ref):
        gv = g_ref[...]
        go_ref[...] = gv
        d_ref[...], mo_ref[...], vo_ref[...] = _adamw_math(w_ref[...], gv, m_ref[...], v_ref[...])

    spec = pl.BlockSpec((tr, C), lambda i: (i, 0))
    shp = jax.ShapeDtypeStruct((R, C), F32)
    return pl.pallas_call(
        body, name=name, grid=(R // tr,), in_specs=[spec] * 4, out_specs=[spec] * 4, out_shape=[shp] * 4,
        compiler_params=_params(("parallel",)),
    )(w, g, m, v)


HBM_SPEC = pl.BlockSpec(memory_space=pltpu.HBM)
GATHER_LOCAL_CHUNKS = 4
GATHER_SEMS = [pltpu.SemaphoreType.DMA((7,)), pltpu.SemaphoreType.DMA((7,)),
               pltpu.SemaphoreType.DMA((GATHER_LOCAL_CHUNKS,))]
OWNER_SEMS = [pltpu.SemaphoreType.DMA((3,)), pltpu.SemaphoreType.DMA((3,)), pltpu.SemaphoreType.DMA(())]


def _mesh_pos():
    return lax.axis_index("x"), lax.axis_index("y"), lax.axis_index("c")


class _RowGather:
    def __init__(self, src_ref, full_ref, rows, send_sems, recv_sems, local_sems):
        self.src, self.full, self.rows = src_ref, full_ref, rows
        self.send, self.recv, self.local = send_sems, recv_sems, local_sems
        x, y, c = _mesh_pos()
        self.c, self.me, self.sibling = c, (x, y, c), (x, y, 1 - c)
        self.chips = [(1 - x, y), (x, 1 - y), (1 - x, 1 - y)]

    def _block(self, pos):
        px, py, pc = pos
        return self.full.at[pl.ds(pl.multiple_of((4 * px + 2 * py + pc) * self.rows, 16), self.rows), :]

    def _copy(self, k, blk, to):
        return pltpu.make_async_remote_copy(
            src_ref=self.src if blk is self.me else self._block(blk), dst_ref=self._block(blk),
            send_sem=self.send.at[k], recv_sem=self.recv.at[k], device_id=to, device_id_type=MESH)

    def _mine(self):
        return _place_locally(self.src, self._block(self.me), self.local, GATHER_LOCAL_CHUNKS)

    def _first(self):
        return [self._copy(0, self.me, self.sibling)] + [
            self._copy(1 + j, self.me, (*chip, self.c)) for j, chip in enumerate(self.chips)]

    def start(self):
        for cp in self._first() + self._mine():
            cp.start()

    def finish(self):
        passed = [self._copy(4 + j, (*chip, self.c), self.sibling) for j, chip in enumerate(self.chips)]
        for j, chip in enumerate(self.chips):
            self._copy(1 + j, (*chip, self.c), self.me).wait_recv()
            passed[j].start()
        self._copy(0, self.sibling, self.me).wait_recv()
        for j, chip in enumerate(self.chips):
            self._copy(4 + j, (*chip, 1 - self.c), self.me).wait_recv()
        for cp in self._first() + passed:
            cp.wait_send()
        for cp in self._mine():
            cp.wait()


class _OwnerExchange:
    def __init__(self, src_ref, dst_ref, send_sems, recv_sems, local_sem):
        self.src, self.dst, self.send, self.recv, self.local = src_ref, dst_ref, send_sems, recv_sems, local_sem
        x, y, c = _mesh_pos()
        self.c, self.my_chip = c, 2 * x + y
        self.peers = [(1 - x, y), (x, 1 - y), (1 - x, 1 - y)]

    def _copies(self):
        local = pltpu.make_async_copy(self.src.at[self.my_chip], self.dst.at[self.my_chip], self.local)
        remote = [pltpu.make_async_remote_copy(
            src_ref=self.src.at[2 * px + py], dst_ref=self.dst.at[self.my_chip],
            send_sem=self.send.at[k], recv_sem=self.recv.at[k], device_id=(px, py, self.c), device_id_type=MESH)
            for k, (px, py) in enumerate(self.peers)]
        return local, remote

    def start(self):
        local, remote = self._copies()
        local.start()
        for cp in remote:
            cp.start()

    def finish(self):
        local, remote = self._copies()
        for cp in remote:
            cp.wait_recv()
        for cp in remote:
            cp.wait_send()
        local.wait()


def _to_sibling(dw, rows, name):
    def body(dw_ref, r_ref, send_sems, recv_sems):
        x, y, c = _mesh_pos()
        copies = [pltpu.make_async_remote_copy(
            src_ref=dw_ref.at[pl.ds(pl.multiple_of((2 * j + (1 - c)) * rows, 16), rows), :], dst_ref=r_ref.at[j],
            send_sem=send_sems.at[j], recv_sem=recv_sems.at[j], device_id=(x, y, 1 - c), device_id_type=MESH)
            for j in range(N_CHIPS)]
        for cp in copies:
            cp.start()
        for cp in copies:
            cp.wait_recv()
        for cp in copies:
            cp.wait_send()

    return pl.pallas_call(
        body, name=name, in_specs=[HBM_SPEC], out_specs=HBM_SPEC,
        out_shape=jax.ShapeDtypeStruct((N_CHIPS, rows, dw.shape[1]), BF16),
        scratch_shapes=[pltpu.SemaphoreType.DMA((N_CHIPS,)), pltpu.SemaphoreType.DMA((N_CHIPS,))],
    )(dw)


PAIR_CHUNKS = 4


def _pair_halves(gt, go):
    def body(gt_in, go_in, gt_ref, go_ref, send_sems, recv_sems):
        del gt_in, go_in
        x, y, c = _mesh_pos()
        copies = []
        for a, (ref, rows) in enumerate(((gt_ref, W_IN_ROWS), (go_ref, W_OUT_ROWS))):
            ch = rows // PAIR_CHUNKS
            for q in range(PAIR_CHUNKS):
                part = ref.at[pl.ds(pl.multiple_of(c * rows + q * ch, 8), ch), :]
                copies.append(pltpu.make_async_remote_copy(
                    src_ref=part, dst_ref=part, send_sem=send_sems.at[PAIR_CHUNKS * a + q],
                    recv_sem=recv_sems.at[PAIR_CHUNKS * a + q], device_id=(x, y, 1 - c), device_id_type=MESH))
        for cp in copies:
            cp.start()
        for cp in copies:
            cp.wait_recv()
        for cp in copies:
            cp.wait_send()

    return pl.pallas_call(
        body, name="pair_halves",
        in_specs=[HBM_SPEC, HBM_SPEC], out_specs=[HBM_SPEC, HBM_SPEC],
        out_shape=[jax.ShapeDtypeStruct(gt.shape, F32), jax.ShapeDtypeStruct(go.shape, F32)],
        input_output_aliases={0: 0, 1: 1},
        scratch_shapes=[pltpu.SemaphoreType.DMA((2 * PAIR_CHUNKS,)), pltpu.SemaphoreType.DMA((2 * PAIR_CHUNKS,))],
    )(gt, go)


VEC = (("g_pre", 2048), ("g_post", 2048), ("b_qkv", 1280), ("ln_v_g", 1024), ("ln_v_b", 1024), ("attn_sinks", 16))
VEC_ROWS = 8
LOSS_ROW = len(VEC)
MAT = (("w_spatial", NG * T), ("b_spatial", NG))
MAT_ROWS = sum(r for _, r in MAT)


def _small_update(vec_grads, loss_part, mat_grads, vec_state, mat_state):
    n_vec, n_mat = len(VEC), len(MAT)
    n_par = n_vec + n_mat
    n_in = n_par + 1 + 3 * n_par

    def body(*refs):
        g_in, loss_in = refs[:n_par], refs[n_par]
        st_in = refs[n_par + 1:n_in]
        outs, loss_out = refs[n_in:n_in + 4 * n_par], refs[n_in + 4 * n_par]
        vbuf, mbuf, tot_v, tot_m, send_sems, recv_sems = refs[n_in + 4 * n_par + 1:]
        x, y, c = _mesh_pos()
        me, sibling = (x, y, c), (x, y, 1 - c)
        chips = [(1 - x, y), (x, 1 - y), (1 - x, 1 - y)]
        my_id = 4 * x + 2 * y + c

        vbuf[my_id] = jnp.zeros((VEC_ROWS, D), F32)
        for r, (_, n) in enumerate(VEC):
            vbuf[my_id, r:r + 1, 0:n] = g_in[r][...]
        vbuf[my_id, LOSS_ROW:LOSS_ROW + 1, 0:128] = loss_in[...]
        r0 = 0
        for q, (_, rows) in enumerate(MAT):
            mbuf[my_id, r0:r0 + rows, :] = g_in[n_vec + q][...]
            r0 += rows

        def copy(a, k, blk, to):
            buf = (vbuf, mbuf)[a]
            px, py, pc = blk
            slot = buf.at[4 * px + 2 * py + pc]
            return pltpu.make_async_remote_copy(
                src_ref=slot, dst_ref=slot, send_sem=send_sems.at[7 * a + k], recv_sem=recv_sems.at[7 * a + k],
                device_id=to, device_id_type=MESH)

        first = [copy(a, 0, me, sibling) for a in range(2)]
        first += [copy(a, 1 + j, me, (*chip, c)) for a in range(2) for j, chip in enumerate(chips)]
        for cp in first:
            cp.start()
        passed = []
        for a in range(2):
            for j, chip in enumerate(chips):
                copy(a, 1 + j, (*chip, c), me).wait_recv()
                passed.append(copy(a, 4 + j, (*chip, c), sibling))
                passed[-1].start()
        for a in range(2):
            copy(a, 0, sibling, me).wait_recv()
            for j, chip in enumerate(chips):
                copy(a, 4 + j, (*chip, 1 - c), me).wait_recv()
        for cp in first + passed:
            cp.wait_send()

        tv, tm = vbuf[0], mbuf[0]
        for d in range(1, N_DEV):
            tv, tm = tv + vbuf[d], tm + mbuf[d]
        tot_v[...] = tv
        tot_m[...] = tm
        loss_out[...] = tot_v[LOSS_ROW:LOSS_ROW + 1, 0:128]
        r0 = 0
        for q in range(n_par):
            if q < n_vec:
                g = tot_v[q:q + 1, 0:VEC[q][1]]
            else:
                rows = MAT[q - n_vec][1]
                g = tot_m[r0:r0 + rows, :]
                r0 += rows
            w, m, v = (st_in[3 * q + t][...] for t in range(3))
            outs[4 * q][...] = g
            outs[4 * q + 1][...], outs[4 * q + 2][...], outs[4 * q + 3][...] = _adamw_math(w, g, m, v)

    grads = list(vec_grads) + list(mat_grads)
    state = [a for wmv in list(vec_state) + list(mat_state) for a in wmv]
    vmem = pl.BlockSpec(memory_space=pltpu.VMEM)
    out_shape = [jax.ShapeDtypeStruct(g.shape, F32) for g in grads for _ in range(4)]
    out_shape.append(jax.ShapeDtypeStruct((1, 128), F32))
    res = pl.pallas_call(
        body, name="small_update",
        in_specs=[vmem] * n_in, out_specs=[vmem] * len(out_shape), out_shape=out_shape,
        scratch_shapes=[pltpu.VMEM((N_DEV, VEC_ROWS, D), F32), pltpu.VMEM((N_DEV, MAT_ROWS, 128), F32),
                        pltpu.VMEM((VEC_ROWS, D), F32), pltpu.VMEM((MAT_ROWS, 128), F32),
                        pltpu.SemaphoreType.DMA((14,)), pltpu.SemaphoreType.DMA((14,))],
        compiler_params=pltpu.CompilerParams(vmem_limit_bytes=VMEM_LIMIT),
    )(*grads, loss_part, *state)
    return [res[4 * q:4 * q + 4] for q in range(n_par)], res[-1]


def kernel(x, positions, g_pre, w_in, b_qkv, ln_v_g, ln_v_b, w_spatial, b_spatial, attn_sinks, w_out, g_post, loss_target, m_g_pre, m_w_in, m_b_qkv, m_ln_v_g, m_ln_v_b, m_w_spatial, m_b_spatial, m_attn_sinks, m_w_out, m_g_post, v_g_pre, v_w_in, v_b_qkv, v_ln_v_g, v_ln_v_b, v_w_spatial, v_b_spatial, v_attn_sinks, v_w_out, v_g_post):
    S = x.shape[1]
    c = lax.axis_index("c")
    c_arr = jnp.reshape(c, (1,)).astype(jnp.int32)
    x2 = x[0]
    target = loss_target[0]
    pos = positions.reshape(S, 1)
    half = HD // 2
    inv_freq = ROPE_THETA ** (-jnp.arange(half, dtype=F32) * (2.0 / HD))
    invf = jnp.tile(inv_freq, 128 // half).reshape(1, 128)
    bias = jnp.concatenate([jnp.zeros((OFF_Q,), F32), b_qkv[0], jnp.zeros((D_IN - OFF_ZB,), F32)]).reshape(1, D_IN)
    b_s_col = b_spatial[0].reshape(NG, T, 1)
    sinks = jnp.repeat(attn_sinks[0], T).reshape(NQ * T, 1)

    chip = 2 * lax.axis_index("x") + lax.axis_index("y")
    wt_part = lax.dynamic_slice_in_dim(w_in[0].T.astype(BF16), c * W_IN_ROWS, W_IN_ROWS, axis=0)
    wo_part = lax.dynamic_slice_in_dim(w_out[0].astype(BF16), c * W_OUT_ROWS, W_OUT_ROWS, axis=0)
    sched = jnp.asarray(PROJ_SCHEDULE, jnp.int32)[chip]
    cos, sin = _rope_table(pos, invf)

    h, proj, wt = _prenorm_inproj(sched, x2, g_pre, bias, wt_part)
    ycat, wo = _mid_fwd(proj, cos, sin, ln_v_g, ln_v_b, w_spatial[0], b_s_col, sinks, wo_part)
    dy, dout, loss_part, dg_post = _outproj_loss(ycat, wo, x2, target, g_post)

    dycat = _dycat(dy, wo)
    dwo = _matmul_tn(ycat, dy, 512, "dw_out")
    po = _presum(c_arr, dwo, _to_sibling(dwo, W_OUT_ROWS, "to_sibling_out"), W_OUT_ROWS)
    dproj, dln_g, dln_b, dws, dbs, dsink, dbqkv, bo = _mid_bwd(
        proj, dycat, cos, sin, ln_v_g, ln_v_b, w_spatial[0], b_s_col, sinks, po)
    dwt = _matmul_tn(dproj, h, 768, "dw_in_t")
    pt = _presum(c_arr, dwt, _to_sibling(dwt, W_IN_ROWS, "to_sibling_in"), W_IN_ROWS)
    grad_x, dg_pre, bt = _dh_prenorm_bwd(dproj, wt, x2, dout, g_pre, pt)
    gt, go = _pair_halves(_sum_chips(c_arr, bt, "sum_chips_in"), _sum_chips(c_arr, bo, "sum_chips_out"))

    g_w_in, d_w_in, nm_w_in, nv_w_in = (a.T for a in _adamw(w_in[0].T, gt, m_w_in[0].T, v_w_in[0].T, "adamw_w_in"))
    g_w_out, d_w_out, nm_w_out, nv_w_out = _adamw(w_out[0], go, m_w_out[0], v_w_out[0], "adamw_w_out")

    state = {"g_pre": (g_pre, m_g_pre, v_g_pre), "g_post": (g_post, m_g_post, v_g_post),
             "b_qkv": (b_qkv, m_b_qkv, v_b_qkv), "ln_v_g": (ln_v_g, m_ln_v_g, v_ln_v_g),
             "ln_v_b": (ln_v_b, m_ln_v_b, v_ln_v_b), "attn_sinks": (attn_sinks, m_attn_sinks, v_attn_sinks),
             "w_spatial": tuple(a.reshape(NG * T, T) for a in (w_spatial, m_w_spatial, v_w_spatial)),
             "b_spatial": tuple(a.reshape(NG, T) for a in (b_spatial, m_b_spatial, v_b_spatial))}
    grads = {"g_pre": dg_pre, "g_post": dg_post, "b_qkv": dbqkv, "ln_v_g": dln_g, "ln_v_b": dln_b,
             "attn_sinks": dsink[:, 0].reshape(1, NQ), "w_spatial": dws.reshape(NG * T, T), "b_spatial": dbs.reshape(NG, T)}
    results, loss = _small_update([grads[n] for n, _ in VEC], loss_part, [grads[n] for n, _ in MAT],
                                  [state[n] for n, _ in VEC], [state[n] for n, _ in MAT])
    small = {n: [a.reshape(w.shape) for a in res]
             for (n, _), res, w in zip(VEC + MAT, results, [state[n][0] for n, _ in VEC + MAT])}
    small["w_spatial"] = [a.reshape(w_spatial.shape) for a in small["w_spatial"]]
    small["b_spatial"] = [a.reshape(b_spatial.shape) for a in small["b_spatial"]]
    big = {"w_in": [a[None] for a in (g_w_in, d_w_in, nm_w_in, nv_w_in)],
           "w_out": [a[None] for a in (g_w_out, d_w_out, nm_w_out, nv_w_out)]}
    order = ("g_pre", "w_in", "b_qkv", "ln_v_g", "ln_v_b", "w_spatial", "b_spatial", "attn_sinks", "w_out", "g_post")
    leaves = {**small, **big}
    return (loss[0, 0], grad_x[None], *[leaves[n][t] for t in range(4) for n in order])
```

```python
import jax
import jax.numpy as jnp
from jax import lax
from jax.experimental import pallas as pl
from jax.experimental.pallas import tpu as pltpu

F32 = jnp.float32
BF16 = jnp.bfloat16
MESH = pl.DeviceIdType.MESH

D = 2048
DG = 1024
T = 128
NG = 8
HD = 64
NQ = 16
D_IN = 5376
OFF_U, OFF_V, OFF_ZA, OFF_Q, OFF_K, OFF_VA, OFF_ZB = 0, 1024, 2048, 3072, 4096, 4224, 4352
D_QKV = 1280
EPS = 1e-6
ROPE_THETA = 10000.0
N_CHIPS = 4
N_DEV = 8
W_IN_ROWS = D_IN // N_DEV
W_OUT_ROWS = D // N_DEV

ADAM_LR, ADAM_B1, ADAM_B2, ADAM_EPS, ADAM_WD, ADAM_STEP = 0.001, 0.9, 0.999, 1e-08, 0.01, 10

VMEM_LIMIT = 56 * 1024 * 1024


def _tile(n, pref):
    return pref if n % pref == 0 else n


def _params(sem=None, vmem=VMEM_LIMIT):
    return pltpu.CompilerParams(dimension_semantics=sem, vmem_limit_bytes=vmem)


def _sigmoid(z):
    return 1.0 / (1.0 + jnp.exp(-z))


def _dot(a, b, dims):
    return lax.dot_general(a, b, (dims, ((), ())), preferred_element_type=F32)


NN = ((1,), (0,))
NT = ((1,), (1,))
TN = ((0,), (0,))


PROJ_TN = 768
N_PROJ_TILES = D_IN // PROJ_TN
PROJ_SCHEDULE = ((0, 2, 4, 1, 3, 6, 5, 5), (2, 0, 6, 1, 4, 3, 5, 4), (4, 6, 0, 5, 2, 1, 3, 4), (6, 4, 2, 5, 3, 0, 1, 5))


LOCAL_CHUNKS = 6


def _place_locally(src_ref, dst_rows_ref, sems, n_chunks=LOCAL_CHUNKS):
    ch = src_ref.shape[0] // n_chunks
    return [pltpu.make_async_copy(src_ref.at[pl.ds(q * ch, ch), :], dst_rows_ref.at[pl.ds(q * ch, ch), :], sems.at[q])
            for q in range(n_chunks)]


def _prenorm_inproj(sched, x, g, bias, wt_part, pos, invf):
    S = x.shape[0]
    tp, tm = _tile(S, 512), _tile(S, 1024)
    n_pre, ns = S // tp, S // tm
    n_steps = n_pre + N_PROJ_TILES * ns
    pos_of = lambda i: jnp.maximum(i - n_pre, 0) // ns
    row_of = lambda i: jnp.maximum(i - n_pre, 0) % ns

    def body(sched_ref, x_ref, g_ref, b_ref, wpart_ref, pos_ref, invf_ref, h_ref, proj_ref, wt_ref, cos_ref, sin_ref,
             h_all, w_tile, stage, send_sems, recv_sems, w_sems, local_sems):
        i = pl.program_id(0)
        x_, y_, c = _mesh_pos()
        me, sibling = (x_, y_, c), (x_, y_, 1 - c)
        chips = [(1 - x_, y_), (x_, 1 - y_), (1 - x_, 1 - y_)]

        def block(pos):
            px, py, pc = pos
            return wt_ref.at[pl.ds(pl.multiple_of((4 * px + 2 * py + pc) * W_IN_ROWS, 16), W_IN_ROWS), :]

        def copy(k, blk, to):
            return pltpu.make_async_remote_copy(
                src_ref=wpart_ref if blk is me else block(blk), dst_ref=block(blk),
                send_sem=send_sems.at[k], recv_sem=recv_sems.at[k], device_id=to, device_id_type=MESH)

        stage_in = pltpu.make_async_copy(wpart_ref, stage, local_sems.at[0])
        stage_out = pltpu.make_async_copy(stage, block(me), local_sems.at[1])

        relay = (jnp.where(c == 0, x_, 1 - x_), jnp.where(c == 0, 1 - y_, y_))
        relayed = (jnp.where(c == 0, 1 - x_, x_), jnp.where(c == 0, y_, 1 - y_))

        def own_sends():
            return [copy(0, me, sibling), copy(2, me, (*chips[1], c)), copy(1, me, (*chips[0], c))]

        def passed_on():
            return [copy(4, (*chips[0], c), sibling), copy(5, (*chips[1], c), sibling), copy(3, (*relayed, c), (*relay, c))]

        def neighbours_arrive():
            copy(1, (*chips[0], c), me).wait_recv()
            copy(2, (*chips[1], c), me).wait_recv()
            for cp in passed_on():
                cp.start()
            copy(5, (*chips[1], 1 - c), me).wait_recv()

        def diagonal_arrives():
            copy(3, (*chips[2], c), me).wait_recv()
            copy(6, (*chips[2], c), sibling).start()
            copy(6, (*chips[2], 1 - c), me).wait_recv()

        def tile_load(p):
            slot = p % 2
            rows = wt_ref.at[pl.ds(pl.multiple_of(sched_ref[p] * PROJ_TN, 16), PROJ_TN), :]
            return pltpu.make_async_copy(rows, w_tile.at[slot], w_sems.at[slot])

        def prepare(p):
            p = jnp.asarray(p, jnp.int32)

            @pl.when(p == 0)
            def _():
                copy(0, sibling, me).wait_recv()
                stage_out.wait()

            pl.when(p == 1)(neighbours_arrive)
            pl.when(p == 2)(lambda: copy(4, (*chips[0], 1 - c), me).wait_recv())
            pl.when(p == sched_ref[N_PROJ_TILES])(diagonal_arrives)
            tile_load(p).start()

        @pl.when(i == 0)
        def _():
            stage_in.start()
            for cp in own_sends():
                cp.start()
            stage_in.wait()
            stage_out.start()

        @pl.when(i < n_pre)
        def _():
            xv = x_ref[...]
            r = lax.rsqrt(jnp.mean(xv * xv, axis=-1, keepdims=True) + EPS)
            hv = (xv * r * g_ref[...]).astype(BF16)
            h_ref[...] = hv
            h_all[pl.ds(pl.multiple_of(i * tp, tp), tp), :] = hv
            ang = pos_ref[...].astype(F32) * invf_ref[...]
            cos_ref[...] = jnp.cos(ang)
            sin_ref[...] = jnp.sin(ang)

        pl.when(i == n_pre - 1)(lambda: prepare(0))

        @pl.when(i >= n_pre)
        def _():
            p, s = pos_of(i), row_of(i)
            pl.when(s == 0)(lambda: tile_load(p).wait())
            pl.when((s == ns - 1) & (p < N_PROJ_TILES - 1))(lambda: prepare(p + 1))
            hv = h_all[pl.ds(pl.multiple_of(s * tm, tm), tm), :]
            proj_ref[...] = _dot(hv, w_tile[p % 2], NT) + b_ref[...]

        @pl.when(i == n_steps - 1)
        def _():
            for cp in own_sends() + passed_on() + [copy(6, (*chips[2], c), sibling)]:
                cp.wait_send()

    return pl.pallas_call(
        body, name="prenorm_inproj",
        grid_spec=pltpu.PrefetchScalarGridSpec(
            num_scalar_prefetch=1, grid=(n_steps,),
            in_specs=[pl.BlockSpec((tp, D), lambda i, sc: (jnp.minimum(i, n_pre - 1), 0)),
                      pl.BlockSpec((1, D), lambda i, sc: (0, 0)),
                      pl.BlockSpec((1, PROJ_TN), lambda i, sc: (0, sc[pos_of(i)])),
                      HBM_SPEC,
                      pl.BlockSpec((tp, 1), lambda i, sc: (jnp.minimum(i, n_pre - 1), 0)),
                      pl.BlockSpec((1, 128), lambda i, sc: (0, 0))],
            out_specs=[pl.BlockSpec((tp, D), lambda i, sc: (jnp.minimum(i, n_pre - 1), 0)),
                       pl.BlockSpec((tm, PROJ_TN), lambda i, sc: (row_of(i), sc[pos_of(i)])),
                       HBM_SPEC,
                       pl.BlockSpec((tp, 128), lambda i, sc: (jnp.minimum(i, n_pre - 1), 0)),
                       pl.BlockSpec((tp, 128), lambda i, sc: (jnp.minimum(i, n_pre - 1), 0))],
            scratch_shapes=[pltpu.VMEM((S, D), BF16), pltpu.VMEM((2, PROJ_TN, D), BF16),
                            pltpu.VMEM((W_IN_ROWS, D), BF16),
                            pltpu.SemaphoreType.DMA((7,)), pltpu.SemaphoreType.DMA((7,)),
                            pltpu.SemaphoreType.DMA((2,)), pltpu.SemaphoreType.DMA((2,))]),
        out_shape=[jax.ShapeDtypeStruct((S, D), BF16), jax.ShapeDtypeStruct((S, D_IN), F32),
                   jax.ShapeDtypeStruct((D_IN, D), BF16),
                   jax.ShapeDtypeStruct((S, 128), F32), jax.ShapeDtypeStruct((S, 128), F32)],
        compiler_params=_params(("arbitrary",)),
    )(sched, x, g, bias, wt_part, pos, invf)


def _rot_half(xs, first_half):
    return jnp.where(first_half, -pltpu.roll(xs, 96, 1), pltpu.roll(xs, 32, 1))


GH = NQ // 2


MASKED = -1e30


def _attn_consts():
    lane = lax.broadcasted_iota(jnp.int32, (T, 128), 1)
    row = lax.broadcasted_iota(jnp.int32, (GH * T, T), 0) & (T - 1)
    on_diag_or_below = row >= lax.broadcasted_iota(jnp.int32, (GH * T, T), 1)
    return (lane & (HD - 1)) < (HD // 2), lane < HD, on_diag_or_below


def _stack_heads(slab_fn, grp, lo64):
    blocks = []
    for jj in range(GH // 2):
        s = slab_fn(GH // 2 * grp + jj)
        r = pltpu.roll(s, HD, 1)
        if grp == 0:
            blocks += [jnp.where(lo64, s, 0.0), jnp.where(lo64, r, 0.0)]
        else:
            blocks += [jnp.where(lo64, 0.0, r), jnp.where(lo64, 0.0, s)]
    return jnp.concatenate(blocks, axis=0)


def _unstack_heads(stacked, jj, grp, lo64):
    a = stacked[(2 * jj) * T:(2 * jj + 1) * T]
    b = stacked[(2 * jj + 1) * T:(2 * jj + 2) * T]
    if grp == 0:
        return jnp.where(lo64, a, pltpu.roll(b, HD, 1))
    return jnp.where(lo64, pltpu.roll(a, HD, 1), b)


def _layer_norm_stats(v):
    mu = jnp.mean(v, axis=-1, keepdims=True)
    xc = v - mu
    var = jnp.mean(xc * xc, axis=-1, keepdims=True)
    rs = lax.rsqrt(var + EPS)
    return xc * rs, rs


def _band_softmax(q_scaled, k_prev, k_cur, own, has_prev, sink):
    s_prev = _dot(q_scaled, k_prev, NT)
    if has_prev is not None:
        s_prev = s_prev + jnp.where(has_prev, 0.0, MASKED)
    s = jnp.where(own, _dot(q_scaled, k_cur, NT), s_prev)
    m = jnp.maximum(jnp.max(s, axis=-1, keepdims=True), sink)
    e = jnp.exp(s - m)
    es = jnp.exp(sink - m)
    inv = 1.0 / (jnp.sum(e, axis=-1, keepdims=True) + es)
    return e * inv, es * inv


def _unfold(p, own):
    return jnp.where(own, p, 0.0).astype(BF16), jnp.where(own, 0.0, p).astype(BF16)


BPS_FWD = 4
BPS_BWD = 2


def _mid_specs(nt, rev, bps):
    tile = (lambda i: nt - 1 - i) if rev else (lambda i: i)
    prev = lambda i: jnp.maximum(bps * tile(i) - 1, 0)
    rows = bps * T
    return tile, [
        pl.BlockSpec((rows, D_IN), lambda i: (tile(i), 0)),
        pl.BlockSpec((T, 2 * T), lambda i: (prev(i), OFF_K // (2 * T))),
    ], [
        pl.BlockSpec((rows, 128), lambda i: (tile(i), 0)),
        pl.BlockSpec((rows, 128), lambda i: (tile(i), 0)),
        pl.BlockSpec((T, 128), lambda i: (prev(i), 0)),
        pl.BlockSpec((T, 128), lambda i: (prev(i), 0)),
        pl.BlockSpec((1, DG), lambda i: (0, 0)),
        pl.BlockSpec((1, DG), lambda i: (0, 0)),
        pl.BlockSpec((NG, T, T), lambda i: (0, 0, 0)),
        pl.BlockSpec((NG, T, 1), lambda i: (0, 0, 0)),
        pl.BlockSpec((NQ * T, 1), lambda i: (0, 0)),
    ]


def _rope(xs, cosv, sinv, first_half):
    return xs * cosv + _rot_half(xs, first_half) * sinv


def _mid_fwd(proj, cos, sin, ln_g, ln_b, w_s, b_s, sinks, wo_part):
    S = proj.shape[0]
    BPS = BPS_FWD if S % (BPS_FWD * T) == 0 else 1
    nt = S // (BPS * T)
    tile, proj_specs, par_specs = _mid_specs(nt, False, BPS)

    def body(p_ref, kvp_ref, cos_ref, sin_ref, cosp_ref, sinp_ref, lng_ref, lnb_ref, ws_ref, bs_ref, sink_ref, wpart_ref,
             y_ref, wo_ref, send_sems, recv_sems, local_sems):
        i = pl.program_id(0)
        first_half, lo64, own = _attn_consts()
        gather = _RowGather(wpart_ref, wo_ref, W_OUT_ROWS, send_sems, recv_sems, local_sems)
        pl.when(i == 0)(gather.start)
        tril = lax.broadcasted_iota(jnp.int32, (T, T), 0) >= lax.broadcasted_iota(jnp.int32, (T, T), 1)

        def block(b, k_prev, v_prev, has_prev):
            rows = slice(b * T, (b + 1) * T)
            xhat, _ = _layer_norm_stats(p_ref[rows, OFF_V:OFF_V + DG])
            vn = xhat * lng_ref[...] + lnb_ref[...]
            for g in range(NG):
                sl = slice(128 * g, 128 * g + 128)
                wm = jnp.where(tril, ws_ref[g], 0.0).astype(BF16)
                mixed = _dot(wm, vn[:, sl].astype(BF16), NN) + bs_ref[g]
                z = p_ref[rows, OFF_ZA + 128 * g:OFF_ZA + 128 * g + 128]
                u = p_ref[rows, OFF_U + 128 * g:OFF_U + 128 * g + 128]
                y_ref[rows, sl] = (u * mixed * (z * _sigmoid(z))).astype(BF16)

            cosv, sinv = cos_ref[rows, :], sin_ref[rows, :]
            k_cur = _rope(p_ref[rows, OFF_K:OFF_K + 128], cosv, sinv, first_half).astype(BF16)
            v_cur = p_ref[rows, OFF_VA:OFF_VA + 128].astype(BF16)

            def q_slab(j):
                return _rope(p_ref[rows, OFF_Q + 128 * j:OFF_Q + 128 * j + 128], cosv, sinv, first_half) * (HD ** -0.5)

            for grp in range(2):
                qs = _stack_heads(q_slab, grp, lo64).astype(BF16)
                p, _ = _band_softmax(qs, k_prev, k_cur, own, has_prev, sink_ref[GH * T * grp:GH * T * (grp + 1), :])
                p_cur, p_prev = _unfold(p, own)
                o_st = _dot(p_cur, v_cur, NN) + _dot(p_prev, v_prev, NN)
                for jj in range(GH // 2):
                    c0 = 128 * (GH // 2 * grp + jj)
                    zb = p_ref[rows, OFF_ZB + c0:OFF_ZB + c0 + 128]
                    o = _unstack_heads(o_st, jj, grp, lo64)
                    y_ref[rows, DG + c0:DG + c0 + 128] = (o * (zb * _sigmoid(zb))).astype(BF16)
            return k_cur, v_cur

        k_prev = _rope(kvp_ref[:, 0:128], cosp_ref[...], sinp_ref[...], first_half).astype(BF16)
        kv = block(0, k_prev, kvp_ref[:, 128:256].astype(BF16), i > 0)
        for b in range(1, BPS):
            kv = block(b, *kv, None)
        pl.when(i == nt - 1)(gather.finish)

    return pl.pallas_call(
        body, name="mid_fwd", grid=(nt,),
        in_specs=proj_specs + par_specs + [HBM_SPEC],
        out_specs=[pl.BlockSpec((BPS * T, D), lambda i: (tile(i), 0)), HBM_SPEC],
        out_shape=[jax.ShapeDtypeStruct((S, D), BF16), jax.ShapeDtypeStruct((D, D), BF16)],
        scratch_shapes=GATHER_SEMS,
        compiler_params=_params(("arbitrary",)),
    )(proj, proj, cos, sin, cos, sin, ln_g, ln_b, w_s, b_s, sinks, wo_part)


def _mid_bwd(proj, dycat, cos, sin, ln_g, ln_b, w_s, b_s, sinks, po):
    S = proj.shape[0]
    BPS = BPS_BWD if S % (BPS_BWD * T) == 0 else 1
    nt = S // (BPS * T)
    tile, proj_specs, par_specs = _mid_specs(nt, True, BPS)
    const2 = lambda i: (0, 0)

    def body(p_ref, kvp_ref, dyc_ref, cos_ref, sin_ref, cosp_ref, sinp_ref, lng_ref, lnb_ref, ws_ref, bs_ref, sink_ref,
             po_ref, dp_ref, dlng_ref, dlnb_ref, dws_ref, dbs_ref, dsink_ref, dbqkv_ref, bo_ref,
             carry_ref, dvn_ref, send_sems, recv_sems, local_sem):
        i = pl.program_id(0)
        first_half, lo64, own = _attn_consts()
        tril = lax.broadcasted_iota(jnp.int32, (T, T), 0) >= lax.broadcasted_iota(jnp.int32, (T, T), 1)
        exchange = _OwnerExchange(po_ref, bo_ref, send_sems, recv_sems, local_sem)
        pl.when(i == 0)(exchange.start)

        @pl.when(i == 0)
        def _():
            dlng_ref[...] = jnp.zeros_like(dlng_ref)
            dlnb_ref[...] = jnp.zeros_like(dlnb_ref)
            dws_ref[...] = jnp.zeros_like(dws_ref)
            dbs_ref[...] = jnp.zeros_like(dbs_ref)
            dsink_ref[...] = jnp.zeros_like(dsink_ref)
            dbqkv_ref[...] = jnp.zeros_like(dbqkv_ref)
            carry_ref[...] = jnp.zeros_like(carry_ref)

        def roped_k(b):
            rows = slice(b * T, (b + 1) * T)
            return _rope(p_ref[rows, OFF_K:OFF_K + 128], cos_ref[rows, :], sin_ref[rows, :], first_half).astype(BF16)

        def block(b, k_prev, v_prev, has_prev, dk_next, dv_next):
            rows = slice(b * T, (b + 1) * T)
            xhat, rs = _layer_norm_stats(p_ref[rows, OFF_V:OFF_V + DG])
            lng = lng_ref[...]
            vn = xhat * lng + lnb_ref[...]
            for g in range(NG):
                sl = slice(128 * g, 128 * g + 128)
                wm = jnp.where(tril, ws_ref[g], 0.0).astype(BF16)
                vng = vn[:, sl].astype(BF16)
                mixed = _dot(wm, vng, NN) + bs_ref[g]
                z = p_ref[rows, OFF_ZA + 128 * g:OFF_ZA + 128 * g + 128]
                u = p_ref[rows, OFF_U + 128 * g:OFF_U + 128 * g + 128]
                dy = dyc_ref[rows, sl]
                sg = _sigmoid(z)
                sa = z * sg
                dp_ref[rows, OFF_U + 128 * g:OFF_U + 128 * g + 128] = (dy * mixed * sa).astype(BF16)
                dp_ref[rows, OFF_ZA + 128 * g:OFF_ZA + 128 * g + 128] = (
                    dy * u * mixed * (sg * (1.0 + z * (1.0 - sg)))).astype(BF16)
                dm = dy * u * sa
                dmb = dm.astype(BF16)
                dvn_ref[rows, sl] = _dot(wm, dmb, TN)
                dws_ref[g] += jnp.where(tril, _dot(dmb, vng, NT), 0.0)
                dbs_ref[g] += jnp.sum(dm, axis=1, keepdims=True)
            dvn = dvn_ref[rows, :]
            dlng_ref[...] += jnp.sum(dvn * xhat, axis=0, keepdims=True)
            dlnb_ref[...] += jnp.sum(dvn, axis=0, keepdims=True)
            dxh = dvn * lng
            dv_g = rs * (dxh - jnp.mean(dxh, axis=-1, keepdims=True)
                         - xhat * jnp.mean(dxh * xhat, axis=-1, keepdims=True))
            dp_ref[rows, OFF_V:OFF_V + DG] = dv_g.astype(BF16)

            cosv, sinv = cos_ref[rows, :], sin_ref[rows, :]
            k_cur = roped_k(b)
            v_cur = p_ref[rows, OFF_VA:OFF_VA + 128].astype(BF16)

            def q_slab(j):
                return _rope(p_ref[rows, OFF_Q + 128 * j:OFF_Q + 128 * j + 128], cosv, sinv, first_half) * (HD ** -0.5)

            def do_slab(j):
                zb = p_ref[rows, OFF_ZB + 128 * j:OFF_ZB + 128 * j + 128]
                return dyc_ref[rows, DG + 128 * j:DG + 128 * j + 128] * (zb * _sigmoid(zb))

            dk_cur, dv_cur = jnp.zeros((T, 128), F32), jnp.zeros((T, 128), F32)
            dk_prev, dv_prev = jnp.zeros((T, 128), F32), jnp.zeros((T, 128), F32)
            for grp in range(2):
                qs = _stack_heads(q_slab, grp, lo64).astype(BF16)
                d_o = _stack_heads(do_slab, grp, lo64)
                dob = d_o.astype(BF16)
                p, ps = _band_softmax(qs, k_prev, k_cur, own, has_prev, sink_ref[GH * T * grp:GH * T * (grp + 1), :])
                p_cur, p_prev = _unfold(p, own)
                o_st = _dot(p_cur, v_cur, NN) + _dot(p_prev, v_prev, NN)
                delta = jnp.sum(d_o * o_st, axis=-1, keepdims=True)
                dp_fold = jnp.where(own, _dot(dob, v_cur, NT), _dot(dob, v_prev, NT))
                ds_cur, ds_prev = _unfold(p * (dp_fold - delta), own)
                dq_st = (_dot(ds_cur, k_cur, NN) + _dot(ds_prev, k_prev, NN)) * (HD ** -0.5)
                dk_cur, dk_prev = dk_cur + _dot(ds_cur, qs, TN), dk_prev + _dot(ds_prev, qs, TN)
                dv_cur, dv_prev = dv_cur + _dot(p_cur, dob, TN), dv_prev + _dot(p_prev, dob, TN)
                dsink_rows = ps * delta
                for hh in range(GH):
                    h = GH * grp + hh
                    dsink_ref[h:h + 1, :] += jnp.broadcast_to(
                        -jnp.sum(dsink_rows[hh * T:(hh + 1) * T], axis=0, keepdims=True), (1, 128))
                for jj in range(GH // 2):
                    c0 = 128 * (GH // 2 * grp + jj)
                    zb = p_ref[rows, OFF_ZB + c0:OFF_ZB + c0 + 128]
                    sg = _sigmoid(zb)
                    o = _unstack_heads(o_st, jj, grp, lo64)
                    dp_ref[rows, OFF_ZB + c0:OFF_ZB + c0 + 128] = (
                        dyc_ref[rows, DG + c0:DG + c0 + 128] * o * (sg * (1.0 + zb * (1.0 - sg)))).astype(BF16)
                    dq = _unstack_heads(dq_st, jj, grp, lo64)
                    dq_pre = dq * cosv - _rot_half(dq, first_half) * sinv
                    dp_ref[rows, OFF_Q + c0:OFF_Q + c0 + 128] = dq_pre.astype(BF16)
                    dbqkv_ref[:, c0:c0 + 128] += jnp.sum(dq_pre, axis=0, keepdims=True)
            dk_cur, dv_cur = dk_cur + dk_next, dv_cur + dv_next
            dk_pre = dk_cur * cosv - _rot_half(dk_cur, first_half) * sinv
            dp_ref[rows, OFF_K:OFF_K + 128] = dk_pre.astype(BF16)
            dp_ref[rows, OFF_VA:OFF_VA + 128] = dv_cur.astype(BF16)
            dbqkv_ref[:, 1024:1152] += jnp.sum(dk_pre, axis=0, keepdims=True)
            dbqkv_ref[:, 1152:1280] += jnp.sum(dv_cur, axis=0, keepdims=True)
            return dk_prev, dv_prev

        grads = carry_ref[:, 0:128], carry_ref[:, 128:256]
        for b in range(BPS - 1, 0, -1):
            prows = slice((b - 1) * T, b * T)
            grads = block(b, roped_k(b - 1), p_ref[prows, OFF_VA:OFF_VA + 128].astype(BF16), None, *grads)
        k_prev = _rope(kvp_ref[:, 0:128], cosp_ref[...], sinp_ref[...], first_half).astype(BF16)
        grads = block(0, k_prev, kvp_ref[:, 128:256].astype(BF16), i < nt - 1, *grads)
        carry_ref[:, 0:128], carry_ref[:, 128:256] = grads
        pl.when(i == nt - 1)(exchange.finish)

    return pl.pallas_call(
        body, name="mid_bwd", grid=(nt,),
        in_specs=proj_specs + [pl.BlockSpec((BPS * T, D), lambda i: (tile(i), 0))] + par_specs + [HBM_SPEC],
        out_specs=[pl.BlockSpec((BPS * T, D_IN), lambda i: (tile(i), 0)),
                   pl.BlockSpec((1, DG), const2), pl.BlockSpec((1, DG), const2),
                   pl.BlockSpec((NG, T, T), lambda i: (0, 0, 0)), pl.BlockSpec((NG, T, 1), lambda i: (0, 0, 0)),
                   pl.BlockSpec((NQ, 128), const2), pl.BlockSpec((1, D_QKV), const2), HBM_SPEC],
        out_shape=[jax.ShapeDtypeStruct((S, D_IN), BF16),
                   jax.ShapeDtypeStruct((1, DG), F32), jax.ShapeDtypeStruct((1, DG), F32),
                   jax.ShapeDtypeStruct((NG, T, T), F32), jax.ShapeDtypeStruct((NG, T, 1), F32),
                   jax.ShapeDtypeStruct((NQ, 128), F32), jax.ShapeDtypeStruct((1, D_QKV), F32),
                   jax.ShapeDtypeStruct(po.shape, BF16)],
        scratch_shapes=[pltpu.VMEM((T, 2 * T), F32), pltpu.VMEM((BPS * T, DG), F32)] + OWNER_SEMS,
        compiler_params=_params(("arbitrary",)),
    )(proj, proj, dycat, cos, sin, cos, sin, ln_g, ln_b, w_s, b_s, sinks, po)


def _outproj_loss(ycat, wo, x, target, g_post):
    S = ycat.shape[0]
    tm = _tile(S, 512)
    nt = S // tm
    n_part = 2 if tm % 32 == 0 else 1
    tp = tm // n_part
    const2 = lambda i: (0, 0)

    def body(yc_ref, w_ref, x_ref, t_ref, g_ref, dy_ref, dout_ref, loss_ref, dg_ref, lacc_ref):
        i = pl.program_id(0)

        @pl.when(i == 0)
        def _():
            dg_ref[...] = jnp.zeros_like(dg_ref)
            lacc_ref[...] = jnp.zeros_like(lacc_ref)

        g = g_ref[...]
        ys = [_dot(yc_ref[q * tp:(q + 1) * tp, :], w_ref[...], NN) for q in range(n_part)]
        for q, y in enumerate(ys):
            rows = slice(q * tp, (q + 1) * tp)
            r = lax.rsqrt(jnp.mean(y * y, axis=-1, keepdims=True) + EPS)
            yh = y * r
            diff = x_ref[rows, :] + yh * g - t_ref[rows, :]
            lacc_ref[...] += jnp.sum(diff * diff, axis=0, keepdims=True)
            dout = diff * (1.0 / D)
            dout_ref[rows, :] = dout
            dg_ref[...] += jnp.sum(dout * yh, axis=0, keepdims=True)
            dyh = dout * g
            dy_ref[rows, :] = (r * (dyh - yh * jnp.mean(dyh * yh, axis=-1, keepdims=True))).astype(BF16)

        @pl.when(i == nt - 1)
        def _():
            loss_ref[...] = jnp.broadcast_to(jnp.sum(lacc_ref[...], axis=1, keepdims=True) * (0.5 / D), (1, 128))

    row = lambda i: (i, 0)
    return pl.pallas_call(
        body, name="outproj_loss", grid=(nt,),
        in_specs=[pl.BlockSpec((tm, D), row), pl.BlockSpec((D, D), const2, pipeline_mode=pl.Buffered(1)),
                  pl.BlockSpec((tm, D), row), pl.BlockSpec((tm, D), row), pl.BlockSpec((1, D), const2)],
        out_specs=[pl.BlockSpec((tm, D), row), pl.BlockSpec((tm, D), row), pl.BlockSpec((1, 128), const2),
                   pl.BlockSpec((1, D), const2)],
        out_shape=[jax.ShapeDtypeStruct((S, D), BF16), jax.ShapeDtypeStruct((S, D), F32),
                   jax.ShapeDtypeStruct((1, 128), F32), jax.ShapeDtypeStruct((1, D), F32)],
        scratch_shapes=[pltpu.VMEM((1, D), F32)],
        compiler_params=_params(("arbitrary",)),
    )(ycat, wo, x, target, g_post)


def _dycat(dy, wo, dwo):
    S = dy.shape[0]
    tm = _tile(S, 512)
    nt = S // tm

    def body(dy_ref, w_ref, dwo_ref, o_ref, r_ref, send_sems, recv_sems):
        i = pl.program_id(0)
        copies = _sibling_copies(dwo_ref, r_ref, W_OUT_ROWS, send_sems, recv_sems)

        @pl.when(i == 0)
        def _():
            for cp in copies:
                cp.start()

        o_ref[...] = _dot(dy_ref[...], w_ref[...], NT)

        @pl.when(i == nt - 1)
        def _():
            for cp in copies:
                cp.wait_recv()
            for cp in copies:
                cp.wait_send()

    return pl.pallas_call(
        body, name="dycat", grid=(nt,),
        in_specs=[pl.BlockSpec((tm, D), lambda i: (i, 0)),
                  pl.BlockSpec((D, D), lambda i: (0, 0), pipeline_mode=pl.Buffered(1)), HBM_SPEC],
        out_specs=[pl.BlockSpec((tm, D), lambda i: (i, 0)), HBM_SPEC],
        out_shape=[jax.ShapeDtypeStruct((S, D), F32), jax.ShapeDtypeStruct((N_CHIPS, W_OUT_ROWS, D), BF16)],
        scratch_shapes=[pltpu.SemaphoreType.DMA((N_CHIPS,)), pltpu.SemaphoreType.DMA((N_CHIPS,))],
        compiler_params=_params(("arbitrary",)),
    )(dy, wo, dwo)


def _matmul_tn(a, b, tm, name, mat_grads=None):
    K, M = a.shape
    N = b.shape[1]
    tk = _tile(K, 1024)
    ni, nk = M // tm, K // tk
    hosting = mat_grads is not None

    def body(a_ref, b_ref, *rest):
        if hosting:
            g_in, (o_ref, tot_ref, acc_ref, mbuf, send_sems, recv_sems) = rest[:len(MAT)], rest[len(MAT):]
        else:
            o_ref, acc_ref = rest
        i, k = pl.program_id(0), pl.program_id(1)

        if hosting:
            gather = _SlotGather(mbuf, send_sems, recv_sems)

            @pl.when((i == 0) & (k == 0))
            def _():
                r0 = 0
                for q, (_, rows) in enumerate(MAT):
                    mbuf[gather.my_id, r0:r0 + rows, :] = g_in[q][...]
                    r0 += rows
                gather.start()

        @pl.when(k == 0)
        def _():
            acc_ref[...] = jnp.zeros_like(acc_ref)

        acc_ref[...] += _dot(a_ref[...], b_ref[...], TN)

        @pl.when(k == nk - 1)
        def _():
            o_ref[...] = acc_ref[...].astype(BF16)

        if hosting:
            @pl.when((i == ni - 1) & (k == nk - 1))
            def _():
                gather.finish()
                tot_ref[...] = gather.total()

    whole = pl.BlockSpec(memory_space=pltpu.VMEM)
    in_specs = [pl.BlockSpec((tk, tm), lambda i, k: (k, i)), pl.BlockSpec((tk, N), lambda i, k: (k, 0))]
    out_specs = [pl.BlockSpec((tm, N), lambda i, k: (i, 0))]
    out_shape = [jax.ShapeDtypeStruct((M, N), BF16)]
    scratch = [pltpu.VMEM((tm, N), F32)]
    if hosting:
        in_specs += [whole] * len(MAT)
        out_specs.append(whole)
        out_shape.append(jax.ShapeDtypeStruct((MAT_ROWS, 128), F32))
        scratch += [pltpu.VMEM((N_DEV, MAT_ROWS, 128), F32), pltpu.SemaphoreType.DMA((7,)),
                    pltpu.SemaphoreType.DMA((7,))]
    res = pl.pallas_call(
        body, name=name, grid=(ni, nk), in_specs=in_specs, out_specs=out_specs, out_shape=out_shape,
        scratch_shapes=scratch,
        compiler_params=_params(("arbitrary", "arbitrary") if hosting else ("parallel", "arbitrary")),
    )(a, b, *(mat_grads or ()))
    return res if hosting else res[0]


def _dh_prenorm_bwd(dproj, wt, x, dout, g_pre, pt):
    S = x.shape[0]
    tm, tk = _tile(S, 1024), 768
    te = tm // 2
    ni, nk, ne = S // tm, D_IN // tk, tm // te
    per = nk + ne
    tile_of, phase_of = (lambda t: t // per), (lambda t: t % per)
    k_of = lambda t: jnp.minimum(phase_of(t), nk - 1)
    half_of = lambda t: tile_of(t) * ne + jnp.maximum(phase_of(t) - nk, 0)

    def body(dp_ref, w_ref, x_ref, dout_ref, g_ref, pt_ref, gx_ref, dg_ref, bt_ref, acc_ref,
             send_sems, recv_sems, local_sem):
        t = pl.program_id(0)
        ph = phase_of(t)
        exchange = _OwnerExchange(pt_ref, bt_ref, send_sems, recv_sems, local_sem)
        pl.when(t == 0)(exchange.start)

        @pl.when(t == 0)
        def _():
            dg_ref[...] = jnp.zeros_like(dg_ref)

        @pl.when(ph == 0)
        def _():
            acc_ref[...] = _dot(dp_ref[...], w_ref[...], NN)

        @pl.when((ph > 0) & (ph < nk))
        def _():
            acc_ref[...] += _dot(dp_ref[...], w_ref[...], NN)

        @pl.when(ph >= nk)
        def _():
            dh = acc_ref[pl.ds(pl.multiple_of((ph - nk) * te, te), te), :]
            xv = x_ref[...]
            r = lax.rsqrt(jnp.mean(xv * xv, axis=-1, keepdims=True) + EPS)
            xh = xv * r
            dg_ref[...] += jnp.sum(dh * xh, axis=0, keepdims=True)
            dxh = dh * g_ref[...]
            gx_ref[...] = dout_ref[...] + r * (dxh - xh * jnp.mean(dxh * xh, axis=-1, keepdims=True))

        pl.when(t == ni * per - 1)(exchange.finish)

    half = lambda t: (half_of(t), 0)
    return pl.pallas_call(
        body, name="dh_prenorm_bwd", grid=(ni * per,),
        in_specs=[pl.BlockSpec((tm, tk), lambda t: (tile_of(t), k_of(t))), pl.BlockSpec((tk, D), lambda t: (k_of(t), 0)),
                  pl.BlockSpec((te, D), half), pl.BlockSpec((te, D), half), pl.BlockSpec((1, D), lambda t: (0, 0)),
                  HBM_SPEC],
        out_specs=[pl.BlockSpec((te, D), half), pl.BlockSpec((1, D), lambda t: (0, 0)), HBM_SPEC],
        out_shape=[jax.ShapeDtypeStruct((S, D), F32), jax.ShapeDtypeStruct((1, D), F32),
                   jax.ShapeDtypeStruct(pt.shape, BF16)],
        scratch_shapes=[pltpu.VMEM((tm, D), F32)] + OWNER_SEMS,
        compiler_params=_params(("arbitrary",)),
    )(dproj, wt, x, dout, g_pre, pt)


def _presum(c_arr, own, recv, half_rows):
    n_cols = own.shape[-1]
    own4 = own.reshape(N_CHIPS, 2, half_rows, n_cols)

    def body(c_ref, own_ref, recv_ref, o_ref):
        o_ref[...] = (own_ref[...].astype(F32) + recv_ref[...].astype(F32)).astype(BF16)

    return pl.pallas_call(
        body, name="presum_%d" % half_rows,
        grid_spec=pltpu.PrefetchScalarGridSpec(
            num_scalar_prefetch=1, grid=(N_CHIPS,),
            in_specs=[pl.BlockSpec((None, None, half_rows, n_cols), lambda j, c: (j, c[0], 0, 0)),
                      pl.BlockSpec((None, half_rows, n_cols), lambda j, c: (j, 0, 0))],
            out_specs=pl.BlockSpec((None, half_rows, n_cols), lambda j, c: (j, 0, 0))),
        out_shape=jax.ShapeDtypeStruct((N_CHIPS, half_rows, n_cols), BF16),
        compiler_params=_params(("parallel",)),
    )(c_arr, own4, recv)


def _sum_chips(c_arr, parts, name):
    _, rows, n_cols = parts.shape
    nt = 2
    tr = rows // nt

    def body(c_ref, p_ref, o_ref):
        o_ref[...] = ((p_ref[0].astype(F32) + p_ref[1].astype(F32)) + p_ref[2].astype(F32)) + p_ref[3].astype(F32)

    return pl.pallas_call(
        body, name=name,
        grid_spec=pltpu.PrefetchScalarGridSpec(
            num_scalar_prefetch=1, grid=(nt,),
            in_specs=[pl.BlockSpec((N_CHIPS, tr, n_cols), lambda i, c: (0, i, 0))],
            out_specs=pl.BlockSpec((tr, n_cols), lambda i, c: (c[0] * nt + i, 0))),
        out_shape=jax.ShapeDtypeStruct((2 * rows, n_cols), F32),
        compiler_params=_params(("parallel",)),
    )(c_arr, parts)


def _adamw_math(w, g, m, v):
    mn = ADAM_B1 * m + (1.0 - ADAM_B1) * g
    vn = ADAM_B2 * v + (1.0 - ADAM_B2) * (g * g)
    m_hat = mn / (1.0 - ADAM_B1 ** ADAM_STEP)
    v_hat = vn / (1.0 - ADAM_B2 ** ADAM_STEP)
    return -ADAM_LR * (m_hat / (jnp.sqrt(v_hat) + ADAM_EPS) + ADAM_WD * w), mn, vn


def _adamw(w, g, m, v, name):
    R, C = w.shape
    tr = next((t for t in (256, 192, 128) if R % t == 0), R)

    def body(w_ref, g_ref, m_ref, v_ref, go_ref, d_ref, mo_ref, vo_ref):
        gv = g_ref[...]
        go_ref[...] = gv
        d_ref[...], mo_ref[...], vo_ref[...] = _adamw_math(w_ref[...], gv, m_ref[...], v_ref[...])

    spec = pl.BlockSpec((tr, C), lambda i: (i, 0))
    shp = jax.ShapeDtypeStruct((R, C), F32)
    return pl.pallas_call(
        body, name=name, grid=(R // tr,), in_specs=[spec] * 4, out_specs=[spec] * 4, out_shape=[shp] * 4,
        compiler_params=_params(("parallel",)),
    )(w, g, m, v)


HBM_SPEC = pl.BlockSpec(memory_space=pltpu.HBM)
GATHER_LOCAL_CHUNKS = 4
GATHER_SEMS = [pltpu.SemaphoreType.DMA((7,)), pltpu.SemaphoreType.DMA((7,)),
               pltpu.SemaphoreType.DMA((GATHER_LOCAL_CHUNKS,))]
OWNER_SEMS = [pltpu.SemaphoreType.DMA((3,)), pltpu.SemaphoreType.DMA((3,)), pltpu.SemaphoreType.DMA(())]


def _mesh_pos():
    return lax.axis_index("x"), lax.axis_index("y"), lax.axis_index("c")


class _RowGather:
    def __init__(self, src_ref, full_ref, rows, send_sems, recv_sems, local_sems):
        self.src, self.full, self.rows = src_ref, full_ref, rows
        self.send, self.recv, self.local = send_sems, recv_sems, local_sems
        x, y, c = _mesh_pos()
        self.c, self.me, self.sibling = c, (x, y, c), (x, y, 1 - c)
        self.chips = [(1 - x, y), (x, 1 - y), (1 - x, 1 - y)]

    def _block(self, pos):
        px, py, pc = pos
        return self.full.at[pl.ds(pl.multiple_of((4 * px + 2 * py + pc) * self.rows, 16), self.rows), :]

    def _copy(self, k, blk, to):
        return pltpu.make_async_remote_copy(
            src_ref=self.src if blk is self.me else self._block(blk), dst_ref=self._block(blk),
            send_sem=self.send.at[k], recv_sem=self.recv.at[k], device_id=to, device_id_type=MESH)

    def _mine(self):
        return _place_locally(self.src, self._block(self.me), self.local, GATHER_LOCAL_CHUNKS)

    def _first(self):
        return [self._copy(0, self.me, self.sibling)] + [
            self._copy(1 + j, self.me, (*chip, self.c)) for j, chip in enumerate(self.chips)]

    def start(self):
        for cp in self._first() + self._mine():
            cp.start()

    def finish(self):
        passed = [self._copy(4 + j, (*chip, self.c), self.sibling) for j, chip in enumerate(self.chips)]
        for j, chip in enumerate(self.chips):
            self._copy(1 + j, (*chip, self.c), self.me).wait_recv()
            passed[j].start()
        self._copy(0, self.sibling, self.me).wait_recv()
        for j, chip in enumerate(self.chips):
            self._copy(4 + j, (*chip, 1 - self.c), self.me).wait_recv()
        for cp in self._first() + passed:
            cp.wait_send()
        for cp in self._mine():
            cp.wait()


class _OwnerExchange:
    def __init__(self, src_ref, dst_ref, send_sems, recv_sems, local_sem):
        self.src, self.dst, self.send, self.recv, self.local = src_ref, dst_ref, send_sems, recv_sems, local_sem
        x, y, c = _mesh_pos()
        self.c, self.my_chip = c, 2 * x + y
        self.peers = [(1 - x, y), (x, 1 - y), (1 - x, 1 - y)]

    def _copies(self):
        local = pltpu.make_async_copy(self.src.at[self.my_chip], self.dst.at[self.my_chip], self.local)
        remote = [pltpu.make_async_remote_copy(
            src_ref=self.src.at[2 * px + py], dst_ref=self.dst.at[self.my_chip],
            send_sem=self.send.at[k], recv_sem=self.recv.at[k], device_id=(px, py, self.c), device_id_type=MESH)
            for k, (px, py) in enumerate(self.peers)]
        return local, remote

    def start(self):
        local, remote = self._copies()
        local.start()
        for cp in remote:
            cp.start()

    def finish(self):
        local, remote = self._copies()
        for cp in remote:
            cp.wait_recv()
        for cp in remote:
            cp.wait_send()
        local.wait()


def _sibling_copies(dw_ref, r_ref, rows, send_sems, recv_sems):
    x, y, c = _mesh_pos()
    return [pltpu.make_async_remote_copy(
        src_ref=dw_ref.at[pl.ds(pl.multiple_of((2 * j + (1 - c)) * rows, 16), rows), :], dst_ref=r_ref.at[j],
        send_sem=send_sems.at[j], recv_sem=recv_sems.at[j], device_id=(x, y, 1 - c), device_id_type=MESH)
        for j in range(N_CHIPS)]


def _to_sibling(dw, rows, name):
    def body(dw_ref, r_ref, send_sems, recv_sems):
        copies = _sibling_copies(dw_ref, r_ref, rows, send_sems, recv_sems)
        for cp in copies:
            cp.start()
        for cp in copies:
            cp.wait_recv()
        for cp in copies:
            cp.wait_send()

    return pl.pallas_call(
        body, name=name, in_specs=[HBM_SPEC], out_specs=HBM_SPEC,
        out_shape=jax.ShapeDtypeStruct((N_CHIPS, rows, dw.shape[1]), BF16),
        scratch_shapes=[pltpu.SemaphoreType.DMA((N_CHIPS,)), pltpu.SemaphoreType.DMA((N_CHIPS,))],
    )(dw)


PAIR_CHUNKS = 4


def _pair_halves(gt, go):
    def body(gt_in, go_in, gt_ref, go_ref, send_sems, recv_sems):
        del gt_in, go_in
        x, y, c = _mesh_pos()
        copies = []
        for a, (ref, rows) in enumerate(((gt_ref, W_IN_ROWS), (go_ref, W_OUT_ROWS))):
            ch = rows // PAIR_CHUNKS
            for q in range(PAIR_CHUNKS):
                part = ref.at[pl.ds(pl.multiple_of(c * rows + q * ch, 8), ch), :]
                copies.append(pltpu.make_async_remote_copy(
                    src_ref=part, dst_ref=part, send_sem=send_sems.at[PAIR_CHUNKS * a + q],
                    recv_sem=recv_sems.at[PAIR_CHUNKS * a + q], device_id=(x, y, 1 - c), device_id_type=MESH))
        for cp in copies:
            cp.start()
        for cp in copies:
            cp.wait_recv()
        for cp in copies:
            cp.wait_send()

    return pl.pallas_call(
        body, name="pair_halves",
        in_specs=[HBM_SPEC, HBM_SPEC], out_specs=[HBM_SPEC, HBM_SPEC],
        out_shape=[jax.ShapeDtypeStruct(gt.shape, F32), jax.ShapeDtypeStruct(go.shape, F32)],
        input_output_aliases={0: 0, 1: 1},
        scratch_shapes=[pltpu.SemaphoreType.DMA((2 * PAIR_CHUNKS,)), pltpu.SemaphoreType.DMA((2 * PAIR_CHUNKS,))],
    )(gt, go)


VEC = (("g_pre", 2048), ("g_post", 2048), ("b_qkv", 1280), ("ln_v_g", 1024), ("ln_v_b", 1024), ("attn_sinks", 16))
VEC_ROWS = 8
LOSS_ROW = len(VEC)
MAT = (("w_spatial", NG * T), ("b_spatial", NG))
MAT_ROWS = sum(r for _, r in MAT)


class _SlotGather:
    def __init__(self, buf, send_sems, recv_sems):
        self.buf, self.send, self.recv = buf, send_sems, recv_sems
        x, y, c = _mesh_pos()
        self.c, self.me, self.sibling, self.my_id = c, (x, y, c), (x, y, 1 - c), 4 * x + 2 * y + c
        self.chips = [(1 - x, y), (x, 1 - y), (1 - x, 1 - y)]

    def _copy(self, k, blk, to):
        px, py, pc = blk
        slot = self.buf.at[4 * px + 2 * py + pc]
        return pltpu.make_async_remote_copy(
            src_ref=slot, dst_ref=slot, send_sem=self.send.at[k], recv_sem=self.recv.at[k],
            device_id=to, device_id_type=MESH)

    def _first(self):
        return [self._copy(0, self.me, self.sibling)] + [
            self._copy(1 + j, self.me, (*chip, self.c)) for j, chip in enumerate(self.chips)]

    def start(self):
        for cp in self._first():
            cp.start()

    def finish(self):
        passed = [self._copy(4 + j, (*chip, self.c), self.sibling) for j, chip in enumerate(self.chips)]
        for j, chip in enumerate(self.chips):
            self._copy(1 + j, (*chip, self.c), self.me).wait_recv()
            passed[j].start()
        self._copy(0, self.sibling, self.me).wait_recv()
        for j, chip in enumerate(self.chips):
            self._copy(4 + j, (*chip, 1 - self.c), self.me).wait_recv()
        for cp in self._first() + passed:
            cp.wait_send()

    def total(self):
        t = self.buf[0]
        for d in range(1, N_DEV):
            t = t + self.buf[d]
        return t


def _small_update(vec_grads, loss_part, mat_total, vec_state, mat_state):
    n_vec, n_mat = len(VEC), len(MAT)
    n_par = n_vec + n_mat
    n_in = n_vec + 2 + 3 * n_par

    def body(*refs):
        g_in, loss_in, tot_m = refs[:n_vec], refs[n_vec], refs[n_vec + 1]
        st_in = refs[n_vec + 2:n_in]
        outs, loss_out = refs[n_in:n_in + 4 * n_par], refs[n_in + 4 * n_par]
        vbuf, tot_v, send_sems, recv_sems = refs[n_in + 4 * n_par + 1:]
        gather = _SlotGather(vbuf, send_sems, recv_sems)
        vbuf[gather.my_id] = jnp.zeros((VEC_ROWS, D), F32)
        for r, (_, n) in enumerate(VEC):
            vbuf[gather.my_id, r:r + 1, 0:n] = g_in[r][...]
        vbuf[gather.my_id, LOSS_ROW:LOSS_ROW + 1, 0:128] = loss_in[...]
        gather.start()
        gather.finish()
        tot_v[...] = gather.total()
        loss_out[...] = tot_v[LOSS_ROW:LOSS_ROW + 1, 0:128]
        r0 = 0
        for q in range(n_par):
            if q < n_vec:
                g = tot_v[q:q + 1, 0:VEC[q][1]]
            else:
                rows = MAT[q - n_vec][1]
                g = tot_m[r0:r0 + rows, :]
                r0 += rows
            w, m, v = (st_in[3 * q + t][...] for t in range(3))
            outs[4 * q][...] = g
            outs[4 * q + 1][...], outs[4 * q + 2][...], outs[4 * q + 3][...] = _adamw_math(w, g, m, v)

    state = [a for wmv in list(vec_state) + list(mat_state) for a in wmv]
    vmem = pl.BlockSpec(memory_space=pltpu.VMEM)
    shapes = [g.shape for g in vec_grads] + [wmv[0].shape for wmv in mat_state]
    out_shape = [jax.ShapeDtypeStruct(s, F32) for s in shapes for _ in range(4)]
    out_shape.append(jax.ShapeDtypeStruct((1, 128), F32))
    res = pl.pallas_call(
        body, name="small_update",
        in_specs=[vmem] * n_in, out_specs=[vmem] * len(out_shape), out_shape=out_shape,
        scratch_shapes=[pltpu.VMEM((N_DEV, VEC_ROWS, D), F32), pltpu.VMEM((VEC_ROWS, D), F32),
                        pltpu.SemaphoreType.DMA((7,)), pltpu.SemaphoreType.DMA((7,))],
        compiler_params=pltpu.CompilerParams(vmem_limit_bytes=VMEM_LIMIT),
    )(*vec_grads, loss_part, mat_total, *state)
    return [res[4 * q:4 * q + 4] for q in range(n_par)], res[-1]


def kernel(x, positions, g_pre, w_in, b_qkv, ln_v_g, ln_v_b, w_spatial, b_spatial, attn_sinks, w_out, g_post, loss_target, m_g_pre, m_w_in, m_b_qkv, m_ln_v_g, m_ln_v_b, m_w_spatial, m_b_spatial, m_attn_sinks, m_w_out, m_g_post, v_g_pre, v_w_in, v_b_qkv, v_ln_v_g, v_ln_v_b, v_w_spatial, v_b_spatial, v_attn_sinks, v_w_out, v_g_post):
    S = x.shape[1]
    c = lax.axis_index("c")
    c_arr = jnp.reshape(c, (1,)).astype(jnp.int32)
    x2 = x[0]
    target = loss_target[0]
    pos = positions.reshape(S, 1)
    half = HD // 2
    inv_freq = ROPE_THETA ** (-jnp.arange(half, dtype=F32) * (2.0 / HD))
    invf = jnp.tile(inv_freq, 128 // half).reshape(1, 128)
    bias = jnp.concatenate([jnp.zeros((OFF_Q,), F32), b_qkv[0], jnp.zeros((D_IN - OFF_ZB,), F32)]).reshape(1, D_IN)
    b_s_col = b_spatial[0].reshape(NG, T, 1)
    sinks = jnp.repeat(attn_sinks[0], T).reshape(NQ * T, 1)

    chip = 2 * lax.axis_index("x") + lax.axis_index("y")
    wt_part = lax.dynamic_slice_in_dim(w_in[0].T.astype(BF16), c * W_IN_ROWS, W_IN_ROWS, axis=0)
    wo_part = lax.dynamic_slice_in_dim(w_out[0].astype(BF16), c * W_OUT_ROWS, W_OUT_ROWS, axis=0)
    sched = jnp.asarray(PROJ_SCHEDULE, jnp.int32)[chip]

    h, proj, wt, cos, sin = _prenorm_inproj(sched, x2, g_pre, bias, wt_part, pos, invf)
    ycat, wo = _mid_fwd(proj, cos, sin, ln_v_g, ln_v_b, w_spatial[0], b_s_col, sinks, wo_part)
    dy, dout, loss_part, dg_post = _outproj_loss(ycat, wo, x2, target, g_post)

    dwo = _matmul_tn(ycat, dy, 512, "dw_out")
    dycat, ro = _dycat(dy, wo, dwo)
    po = _presum(c_arr, dwo, ro, W_OUT_ROWS)
    dproj, dln_g, dln_b, dws, dbs, dsink, dbqkv, bo = _mid_bwd(
        proj, dycat, cos, sin, ln_v_g, ln_v_b, w_spatial[0], b_s_col, sinks, po)
    dwt, mat_total = _matmul_tn(dproj, h, 768, "dw_in_t",
                                mat_grads=[dws.reshape(NG * T, T), dbs.reshape(NG, T)])
    pt = _presum(c_arr, dwt, _to_sibling(dwt, W_IN_ROWS, "to_sibling_in"), W_IN_ROWS)
    grad_x, dg_pre, bt = _dh_prenorm_bwd(dproj, wt, x2, dout, g_pre, pt)
    gt, go = _pair_halves(_sum_chips(c_arr, bt, "sum_chips_in"), _sum_chips(c_arr, bo, "sum_chips_out"))

    g_w_in, d_w_in, nm_w_in, nv_w_in = (a.T for a in _adamw(w_in[0].T, gt, m_w_in[0].T, v_w_in[0].T, "adamw_w_in"))
    g_w_out, d_w_out, nm_w_out, nv_w_out = _adamw(w_out[0], go, m_w_out[0], v_w_out[0], "adamw_w_out")

    state = {"g_pre": (g_pre, m_g_pre, v_g_pre), "g_post": (g_post, m_g_post, v_g_post),
             "b_qkv": (b_qkv, m_b_qkv, v_b_qkv), "ln_v_g": (ln_v_g, m_ln_v_g, v_ln_v_g),
             "ln_v_b": (ln_v_b, m_ln_v_b, v_ln_v_b), "attn_sinks": (attn_sinks, m_attn_sinks, v_attn_sinks),
             "w_spatial": tuple(a.reshape(NG * T, T) for a in (w_spatial, m_w_spatial, v_w_spatial)),
             "b_spatial": tuple(a.reshape(NG, T) for a in (b_spatial, m_b_spatial, v_b_spatial))}
    grads = {"g_pre": dg_pre, "g_post": dg_post, "b_qkv": dbqkv, "ln_v_g": dln_g, "ln_v_b": dln_b,
             "attn_sinks": dsink[:, 0].reshape(1, NQ)}
    results, loss = _small_update([grads[n] for n, _ in VEC], loss_part, mat_total,
                                  [state[n] for n, _ in VEC], [state[n] for n, _ in MAT])
    small = {n: [a.reshape(w.shape) for a in res]
             for (n, _), res, w in zip(VEC + MAT, results, [state[n][0] for n, _ in VEC + MAT])}
    small["w_spatial"] = [a.reshape(w_spatial.shape) for a in small["w_spatial"]]
    small["b_spatial"] = [a.reshape(b_spatial.shape) for a in small["b_spatial"]]
    big = {"w_in": [a[None] for a in (g_w_in, d_w_in, nm_w_in, nv_w_in)],
           "w_out": [a[None] for a in (g_w_out, d_w_out, nm_w_out, nv_w_out)]}
    order = ("g_pre", "w_in", "b_qkv", "ln_v_g", "ln_v_b", "w_spatial", "b_spatial", "attn_sinks", "w_out", "g_post")
    leaves = {**small, **big}
    return (loss[0, 0], grad_x[None], *[leaves[n][t] for t in range(4) for n in order])
```

```python
import jax
import jax.numpy as jnp
from jax import lax
from jax.experimental import pallas as pl
from jax.experimental.pallas import tpu as pltpu

F32 = jnp.float32
BF16 = jnp.bfloat16
MESH = pl.DeviceIdType.MESH

D = 2048
DG = 1024
T = 128
NG = 8
HD = 64
NQ = 16
D_IN = 5376
OFF_U, OFF_V, OFF_ZA, OFF_Q, OFF_K, OFF_VA, OFF_ZB = 0, 1024, 2048, 3072, 4096, 4224, 4352
D_QKV = 1280
EPS = 1e-6
ROPE_THETA = 10000.0
N_CHIPS = 4
N_DEV = 8
W_IN_ROWS = D_IN // N_DEV
W_OUT_ROWS = D // N_DEV

ADAM_LR, ADAM_B1, ADAM_B2, ADAM_EPS, ADAM_WD, ADAM_STEP = 0.001, 0.9, 0.999, 1e-08, 0.01, 10

VMEM_LIMIT = 56 * 1024 * 1024


def _tile(n, pref):
    return pref if n % pref == 0 else n


def _params(sem=None, vmem=VMEM_LIMIT):
    return pltpu.CompilerParams(dimension_semantics=sem, vmem_limit_bytes=vmem)


def _sigmoid(z):
    return 1.0 / (1.0 + jnp.exp(-z))


def _dot(a, b, dims):
    return lax.dot_general(a, b, (dims, ((), ())), preferred_element_type=F32)


NN = ((1,), (0,))
NT = ((1,), (1,))
TN = ((0,), (0,))


PROJ_TN = 768
N_PROJ_TILES = D_IN // PROJ_TN
PROJ_SCHEDULE = ((0, 2, 4, 1, 3, 6, 5, 5), (2, 0, 6, 1, 4, 3, 5, 4), (4, 6, 0, 5, 2, 1, 3, 4), (6, 4, 2, 5, 3, 0, 1, 5))


LOCAL_CHUNKS = 6


def _place_locally(src_ref, dst_rows_ref, sems, n_chunks=LOCAL_CHUNKS):
    ch = src_ref.shape[0] // n_chunks
    return [pltpu.make_async_copy(src_ref.at[pl.ds(q * ch, ch), :], dst_rows_ref.at[pl.ds(q * ch, ch), :], sems.at[q])
            for q in range(n_chunks)]


def _prenorm_inproj(sched, x, g, bias, wt_part, pos, invf):
    S = x.shape[0]
    tp, tm = _tile(S, 512), _tile(S, 1024)
    n_pre, ns = S // tp, S // tm
    n_steps = n_pre + N_PROJ_TILES * ns
    pos_of = lambda i: jnp.maximum(i - n_pre, 0) // ns
    row_of = lambda i: jnp.maximum(i - n_pre, 0) % ns

    def body(sched_ref, x_ref, g_ref, b_ref, wpart_ref, pos_ref, invf_ref, h_ref, proj_ref, wt_ref, cos_ref, sin_ref,
             h_all, w_tile, stage, send_sems, recv_sems, w_sems, local_sems):
        i = pl.program_id(0)
        x_, y_, c = _mesh_pos()
        me, sibling = (x_, y_, c), (x_, y_, 1 - c)
        chips = [(1 - x_, y_), (x_, 1 - y_), (1 - x_, 1 - y_)]

        def block(pos):
            px, py, pc = pos
            return wt_ref.at[pl.ds(pl.multiple_of((4 * px + 2 * py + pc) * W_IN_ROWS, 16), W_IN_ROWS), :]

        def copy(k, blk, to):
            return pltpu.make_async_remote_copy(
                src_ref=wpart_ref if blk is me else block(blk), dst_ref=block(blk),
                send_sem=send_sems.at[k], recv_sem=recv_sems.at[k], device_id=to, device_id_type=MESH)

        stage_in = pltpu.make_async_copy(wpart_ref, stage, local_sems.at[0])
        stage_out = pltpu.make_async_copy(stage, block(me), local_sems.at[1])

        relay = (jnp.where(c == 0, x_, 1 - x_), jnp.where(c == 0, 1 - y_, y_))
        relayed = (jnp.where(c == 0, 1 - x_, x_), jnp.where(c == 0, y_, 1 - y_))

        def own_sends():
            return [copy(0, me, sibling), copy(2, me, (*chips[1], c)), copy(1, me, (*chips[0], c))]

        def passed_on():
            return [copy(4, (*chips[0], c), sibling), copy(5, (*chips[1], c), sibling), copy(3, (*relayed, c), (*relay, c))]

        def neighbours_arrive():
            copy(1, (*chips[0], c), me).wait_recv()
            copy(2, (*chips[1], c), me).wait_recv()
            for cp in passed_on():
                cp.start()
            copy(5, (*chips[1], 1 - c), me).wait_recv()

        def diagonal_arrives():
            copy(3, (*chips[2], c), me).wait_recv()
            copy(6, (*chips[2], c), sibling).start()
            copy(6, (*chips[2], 1 - c), me).wait_recv()

        def tile_load(p):
            slot = p % 2
            rows = wt_ref.at[pl.ds(pl.multiple_of(sched_ref[p] * PROJ_TN, 16), PROJ_TN), :]
            return pltpu.make_async_copy(rows, w_tile.at[slot], w_sems.at[slot])

        def prepare(p):
            p = jnp.asarray(p, jnp.int32)

            @pl.when(p == 0)
            def _():
                copy(0, sibling, me).wait_recv()
                stage_out.wait()

            pl.when(p == 1)(neighbours_arrive)
            pl.when(p == 2)(lambda: copy(4, (*chips[0], 1 - c), me).wait_recv())
            pl.when(p == sched_ref[N_PROJ_TILES])(diagonal_arrives)
            tile_load(p).start()

        @pl.when(i == 0)
        def _():
            stage_in.start()
            for cp in own_sends():
                cp.start()
            stage_in.wait()
            stage_out.start()

        @pl.when(i < n_pre)
        def _():
            xv = x_ref[...]
            r = lax.rsqrt(jnp.mean(xv * xv, axis=-1, keepdims=True) + EPS)
            hv = (xv * r * g_ref[...]).astype(BF16)
            h_ref[...] = hv
            h_all[pl.ds(pl.multiple_of(i * tp, tp), tp), :] = hv
            ang = pos_ref[...].astype(F32) * invf_ref[...]
            cos_ref[...] = jnp.cos(ang)
            sin_ref[...] = jnp.sin(ang)

        pl.when(i == n_pre - 1)(lambda: prepare(0))

        @pl.when(i >= n_pre)
        def _():
            p, s = pos_of(i), row_of(i)
            pl.when(s == 0)(lambda: tile_load(p).wait())
            pl.when((s == ns - 1) & (p < N_PROJ_TILES - 1))(lambda: prepare(p + 1))
            hv = h_all[pl.ds(pl.multiple_of(s * tm, tm), tm), :]
            proj_ref[...] = _dot(hv, w_tile[p % 2], NT) + b_ref[...]

        @pl.when(i == n_steps - 1)
        def _():
            for cp in own_sends() + passed_on() + [copy(6, (*chips[2], c), sibling)]:
                cp.wait_send()

    return pl.pallas_call(
        body, name="prenorm_inproj",
        grid_spec=pltpu.PrefetchScalarGridSpec(
            num_scalar_prefetch=1, grid=(n_steps,),
            in_specs=[pl.BlockSpec((tp, D), lambda i, sc: (jnp.minimum(i, n_pre - 1), 0)),
                      pl.BlockSpec((1, D), lambda i, sc: (0, 0)),
                      pl.BlockSpec((1, PROJ_TN), lambda i, sc: (0, sc[pos_of(i)])),
                      HBM_SPEC,
                      pl.BlockSpec((tp, 1), lambda i, sc: (jnp.minimum(i, n_pre - 1), 0)),
                      pl.BlockSpec((1, 128), lambda i, sc: (0, 0))],
            out_specs=[pl.BlockSpec((tp, D), lambda i, sc: (jnp.minimum(i, n_pre - 1), 0)),
                       pl.BlockSpec((tm, PROJ_TN), lambda i, sc: (row_of(i), sc[pos_of(i)])),
                       HBM_SPEC,
                       pl.BlockSpec((tp, 128), lambda i, sc: (jnp.minimum(i, n_pre - 1), 0)),
                       pl.BlockSpec((tp, 128), lambda i, sc: (jnp.minimum(i, n_pre - 1), 0))],
            scratch_shapes=[pltpu.VMEM((S, D), BF16), pltpu.VMEM((2, PROJ_TN, D), BF16),
                            pltpu.VMEM((W_IN_ROWS, D), BF16),
                            pltpu.SemaphoreType.DMA((7,)), pltpu.SemaphoreType.DMA((7,)),
                            pltpu.SemaphoreType.DMA((2,)), pltpu.SemaphoreType.DMA((2,))]),
        out_shape=[jax.ShapeDtypeStruct((S, D), BF16), jax.ShapeDtypeStruct((S, D_IN), F32),
                   jax.ShapeDtypeStruct((D_IN, D), BF16),
                   jax.ShapeDtypeStruct((S, 128), F32), jax.ShapeDtypeStruct((S, 128), F32)],
        compiler_params=_params(("arbitrary",)),
    )(sched, x, g, bias, wt_part, pos, invf)


def _rot_half(xs, first_half):
    return jnp.where(first_half, -pltpu.roll(xs, 96, 1), pltpu.roll(xs, 32, 1))


GH = NQ // 2


MASKED = -1e30


def _attn_consts():
    lane = lax.broadcasted_iota(jnp.int32, (T, 128), 1)
    row = lax.broadcasted_iota(jnp.int32, (GH * T, T), 0) & (T - 1)
    on_diag_or_below = row >= lax.broadcasted_iota(jnp.int32, (GH * T, T), 1)
    return (lane & (HD - 1)) < (HD // 2), lane < HD, on_diag_or_below


def _stack_heads(slab_fn, grp, lo64):
    blocks = []
    for jj in range(GH // 2):
        s = slab_fn(GH // 2 * grp + jj)
        r = pltpu.roll(s, HD, 1)
        if grp == 0:
            blocks += [jnp.where(lo64, s, 0.0), jnp.where(lo64, r, 0.0)]
        else:
            blocks += [jnp.where(lo64, 0.0, r), jnp.where(lo64, 0.0, s)]
    return jnp.concatenate(blocks, axis=0)


def _unstack_heads(stacked, jj, grp, lo64):
    a = stacked[(2 * jj) * T:(2 * jj + 1) * T]
    b = stacked[(2 * jj + 1) * T:(2 * jj + 2) * T]
    if grp == 0:
        return jnp.where(lo64, a, pltpu.roll(b, HD, 1))
    return jnp.where(lo64, pltpu.roll(a, HD, 1), b)


def _layer_norm_stats(v):
    mu = jnp.mean(v, axis=-1, keepdims=True)
    xc = v - mu
    var = jnp.mean(xc * xc, axis=-1, keepdims=True)
    rs = lax.rsqrt(var + EPS)
    return xc * rs, rs


def _fold(both, own):
    return jnp.where(own, both[:, T:2 * T], both[:, 0:T])


def _unfold(p, own):
    return jnp.concatenate([jnp.where(own, 0.0, p), jnp.where(own, p, 0.0)], axis=1).astype(BF16)


def _band_softmax(q_scaled, k_both, own, has_prev, sink):
    s_both = _dot(q_scaled, k_both, NT)
    s_prev = s_both[:, 0:T]
    if has_prev is not None:
        s_prev = s_prev + jnp.where(has_prev, 0.0, MASKED)
    s = jnp.where(own, s_both[:, T:2 * T], s_prev)
    m = jnp.maximum(jnp.max(s, axis=-1, keepdims=True), sink)
    e = jnp.exp(s - m)
    es = jnp.exp(sink - m)
    inv = 1.0 / (jnp.sum(e, axis=-1, keepdims=True) + es)
    return e * inv, es * inv


BPS_FWD = 4
BPS_BWD = 2


def _mid_specs(nt, rev, bps):
    tile = (lambda i: nt - 1 - i) if rev else (lambda i: i)
    prev = lambda i: jnp.maximum(bps * tile(i) - 1, 0)
    rows = bps * T
    return tile, [
        pl.BlockSpec((rows, D_IN), lambda i: (tile(i), 0)),
        pl.BlockSpec((T, 2 * T), lambda i: (prev(i), OFF_K // (2 * T))),
    ], [
        pl.BlockSpec((rows, 128), lambda i: (tile(i), 0)),
        pl.BlockSpec((rows, 128), lambda i: (tile(i), 0)),
        pl.BlockSpec((T, 128), lambda i: (prev(i), 0)),
        pl.BlockSpec((T, 128), lambda i: (prev(i), 0)),
        pl.BlockSpec((1, DG), lambda i: (0, 0)),
        pl.BlockSpec((1, DG), lambda i: (0, 0)),
        pl.BlockSpec((NG, T, T), lambda i: (0, 0, 0)),
        pl.BlockSpec((NG, T, 1), lambda i: (0, 0, 0)),
        pl.BlockSpec((NQ * T, 1), lambda i: (0, 0)),
    ]


def _rope(xs, cosv, sinv, first_half):
    return xs * cosv + _rot_half(xs, first_half) * sinv


def _mid_fwd(proj, cos, sin, ln_g, ln_b, w_s, b_s, sinks, wo_part):
    S = proj.shape[0]
    BPS = BPS_FWD if S % (BPS_FWD * T) == 0 else 1
    nt = S // (BPS * T)
    tile, proj_specs, par_specs = _mid_specs(nt, False, BPS)

    def body(p_ref, kvp_ref, cos_ref, sin_ref, cosp_ref, sinp_ref, lng_ref, lnb_ref, ws_ref, bs_ref, sink_ref, wpart_ref,
             y_ref, wo_ref, wm_ref, send_sems, recv_sems, local_sems):
        i = pl.program_id(0)
        first_half, lo64, own = _attn_consts()
        gather = _RowGather(wpart_ref, wo_ref, W_OUT_ROWS, send_sems, recv_sems, local_sems)
        pl.when(i == 0)(gather.start)

        @pl.when(i == 0)
        def _():
            tril = lax.broadcasted_iota(jnp.int32, (T, T), 0) >= lax.broadcasted_iota(jnp.int32, (T, T), 1)
            for g in range(NG):
                wm_ref[g] = jnp.where(tril, ws_ref[g], 0.0).astype(BF16)

        def block(b, k_prev, v_prev, has_prev):
            rows = slice(b * T, (b + 1) * T)
            xhat, _ = _layer_norm_stats(p_ref[rows, OFF_V:OFF_V + DG])
            vn = xhat * lng_ref[...] + lnb_ref[...]
            for g in range(NG):
                sl = slice(128 * g, 128 * g + 128)
                mixed = _dot(wm_ref[g], vn[:, sl].astype(BF16), NN) + bs_ref[g]
                z = p_ref[rows, OFF_ZA + 128 * g:OFF_ZA + 128 * g + 128]
                u = p_ref[rows, OFF_U + 128 * g:OFF_U + 128 * g + 128]
                y_ref[rows, sl] = (u * mixed * (z * _sigmoid(z))).astype(BF16)

            cosv, sinv = cos_ref[rows, :], sin_ref[rows, :]
            k_cur = _rope(p_ref[rows, OFF_K:OFF_K + 128], cosv, sinv, first_half).astype(BF16)
            v_cur = p_ref[rows, OFF_VA:OFF_VA + 128].astype(BF16)

            def q_slab(j):
                return _rope(p_ref[rows, OFF_Q + 128 * j:OFF_Q + 128 * j + 128], cosv, sinv, first_half) * (HD ** -0.5)

            k_both = jnp.concatenate([k_prev, k_cur], axis=0)
            v_both = jnp.concatenate([v_prev, v_cur], axis=0)
            for grp in range(2):
                qs = _stack_heads(q_slab, grp, lo64).astype(BF16)
                p, _ = _band_softmax(qs, k_both, own, has_prev, sink_ref[GH * T * grp:GH * T * (grp + 1), :])
                o_st = _dot(_unfold(p, own), v_both, NN)
                for jj in range(GH // 2):
                    c0 = 128 * (GH // 2 * grp + jj)
                    zb = p_ref[rows, OFF_ZB + c0:OFF_ZB + c0 + 128]
                    o = _unstack_heads(o_st, jj, grp, lo64)
                    y_ref[rows, DG + c0:DG + c0 + 128] = (o * (zb * _sigmoid(zb))).astype(BF16)
            return k_cur, v_cur

        k_prev = _rope(kvp_ref[:, 0:128], cosp_ref[...], sinp_ref[...], first_half).astype(BF16)
        kv = block(0, k_prev, kvp_ref[:, 128:256].astype(BF16), i > 0)
        for b in range(1, BPS):
            kv = block(b, *kv, None)
        pl.when(i == nt - 1)(gather.finish)

    return pl.pallas_call(
        body, name="mid_fwd", grid=(nt,),
        in_specs=proj_specs + par_specs + [HBM_SPEC],
        out_specs=[pl.BlockSpec((BPS * T, D), lambda i: (tile(i), 0)), HBM_SPEC],
        out_shape=[jax.ShapeDtypeStruct((S, D), BF16), jax.ShapeDtypeStruct((D, D), BF16)],
        scratch_shapes=[pltpu.VMEM((NG, T, T), BF16)] + GATHER_SEMS,
        compiler_params=_params(("arbitrary",)),
    )(proj, proj, cos, sin, cos, sin, ln_g, ln_b, w_s, b_s, sinks, wo_part)


def _mid_bwd(proj, dycat, cos, sin, ln_g, ln_b, w_s, b_s, sinks, po):
    S = proj.shape[0]
    BPS = BPS_BWD if S % (BPS_BWD * T) == 0 else 1
    nt = S // (BPS * T)
    tile, proj_specs, par_specs = _mid_specs(nt, True, BPS)
    const2 = lambda i: (0, 0)

    def body(p_ref, kvp_ref, dyc_ref, cos_ref, sin_ref, cosp_ref, sinp_ref, lng_ref, lnb_ref, ws_ref, bs_ref, sink_ref,
             po_ref, dp_ref, dlng_ref, dlnb_ref, dws_ref, dbs_ref, dsink_ref, dbqkv_ref, bo_ref,
             carry_ref, dvn_ref, wm_ref, wmt_ref, send_sems, recv_sems, local_sem):
        i = pl.program_id(0)
        first_half, lo64, own = _attn_consts()
        tril = lax.broadcasted_iota(jnp.int32, (T, T), 0) >= lax.broadcasted_iota(jnp.int32, (T, T), 1)
        exchange = _OwnerExchange(po_ref, bo_ref, send_sems, recv_sems, local_sem)
        pl.when(i == 0)(exchange.start)

        @pl.when(i == 0)
        def _():
            for g in range(NG):
                wm = jnp.where(tril, ws_ref[g], 0.0)
                wm_ref[g] = wm.astype(BF16)
                wmt_ref[g] = wm.T.astype(BF16)
            dlng_ref[...] = jnp.zeros_like(dlng_ref)
            dlnb_ref[...] = jnp.zeros_like(dlnb_ref)
            dws_ref[...] = jnp.zeros_like(dws_ref)
            dbs_ref[...] = jnp.zeros_like(dbs_ref)
            dsink_ref[...] = jnp.zeros_like(dsink_ref)
            dbqkv_ref[...] = jnp.zeros_like(dbqkv_ref)
            carry_ref[...] = jnp.zeros_like(carry_ref)

        def roped_k(b):
            rows = slice(b * T, (b + 1) * T)
            return _rope(p_ref[rows, OFF_K:OFF_K + 128], cos_ref[rows, :], sin_ref[rows, :], first_half).astype(BF16)

        def block(b, k_prev, v_prev, has_prev, dk_next, dv_next):
            rows = slice(b * T, (b + 1) * T)
            xhat, rs = _layer_norm_stats(p_ref[rows, OFF_V:OFF_V + DG])
            lng = lng_ref[...]
            vn = xhat * lng + lnb_ref[...]
            for g in range(NG):
                sl = slice(128 * g, 128 * g + 128)
                vng = vn[:, sl].astype(BF16)
                mixed = _dot(wm_ref[g], vng, NN) + bs_ref[g]
                z = p_ref[rows, OFF_ZA + 128 * g:OFF_ZA + 128 * g + 128]
                u = p_ref[rows, OFF_U + 128 * g:OFF_U + 128 * g + 128]
                dy = dyc_ref[rows, sl]
                sg = _sigmoid(z)
                sa = z * sg
                dp_ref[rows, OFF_U + 128 * g:OFF_U + 128 * g + 128] = (dy * mixed * sa).astype(BF16)
                dp_ref[rows, OFF_ZA + 128 * g:OFF_ZA + 128 * g + 128] = (
                    dy * u * mixed * (sg * (1.0 + z * (1.0 - sg)))).astype(BF16)
                dm = dy * u * sa
                dmb = dm.astype(BF16)
                dvn_ref[rows, sl] = _dot(wmt_ref[g], dmb, NN)
                dws_ref[g] += jnp.where(tril, _dot(dmb, vng, NT), 0.0)
                dbs_ref[g] += jnp.sum(dm, axis=1, keepdims=True)
            dvn = dvn_ref[rows, :]
            dlng_ref[...] += jnp.sum(dvn * xhat, axis=0, keepdims=True)
            dlnb_ref[...] += jnp.sum(dvn, axis=0, keepdims=True)
            dxh = dvn * lng
            dv_g = rs * (dxh - jnp.mean(dxh, axis=-1, keepdims=True)
                         - xhat * jnp.mean(dxh * xhat, axis=-1, keepdims=True))
            dp_ref[rows, OFF_V:OFF_V + DG] = dv_g.astype(BF16)

            cosv, sinv = cos_ref[rows, :], sin_ref[rows, :]
            k_cur = roped_k(b)
            v_cur = p_ref[rows, OFF_VA:OFF_VA + 128].astype(BF16)

            def q_slab(j):
                return _rope(p_ref[rows, OFF_Q + 128 * j:OFF_Q + 128 * j + 128], cosv, sinv, first_half) * (HD ** -0.5)

            def do_slab(j):
                zb = p_ref[rows, OFF_ZB + 128 * j:OFF_ZB + 128 * j + 128]
                return dyc_ref[rows, DG + 128 * j:DG + 128 * j + 128] * (zb * _sigmoid(zb))

            k_both = jnp.concatenate([k_prev, k_cur], axis=0)
            v_both = jnp.concatenate([v_prev, v_cur], axis=0)
            dk_both, dv_both = jnp.zeros((2 * T, 128), F32), jnp.zeros((2 * T, 128), F32)
            for grp in range(2):
                qs = _stack_heads(q_slab, grp, lo64).astype(BF16)
                d_o = _stack_heads(do_slab, grp, lo64)
                dob = d_o.astype(BF16)
                p, ps = _band_softmax(qs, k_both, own, has_prev, sink_ref[GH * T * grp:GH * T * (grp + 1), :])
                p_both = _unfold(p, own)
                o_st = _dot(p_both, v_both, NN)
                delta = jnp.sum(d_o * o_st, axis=-1, keepdims=True)
                ds_both = _unfold(p * (_fold(_dot(dob, v_both, NT), own) - delta), own)
                dq_st = _dot(ds_both, k_both, NN) * (HD ** -0.5)
                dk_both = dk_both + _dot(ds_both, qs, TN)
                dv_both = dv_both + _dot(p_both, dob, TN)
                dsink_rows = ps * delta
                for hh in range(GH):
                    h = GH * grp + hh
                    dsink_ref[h:h + 1, :] += jnp.broadcast_to(
                        -jnp.sum(dsink_rows[hh * T:(hh + 1) * T], axis=0, keepdims=True), (1, 128))
                for jj in range(GH // 2):
                    c0 = 128 * (GH // 2 * grp + jj)
                    zb = p_ref[rows, OFF_ZB + c0:OFF_ZB + c0 + 128]
                    sg = _sigmoid(zb)
                    o = _unstack_heads(o_st, jj, grp, lo64)
                    dp_ref[rows, OFF_ZB + c0:OFF_ZB + c0 + 128] = (
                        dyc_ref[rows, DG + c0:DG + c0 + 128] * o * (sg * (1.0 + zb * (1.0 - sg)))).astype(BF16)
                    dq = _unstack_heads(dq_st, jj, grp, lo64)
                    dq_pre = dq * cosv - _rot_half(dq, first_half) * sinv
                    dp_ref[rows, OFF_Q + c0:OFF_Q + c0 + 128] = dq_pre.astype(BF16)
                    dbqkv_ref[:, c0:c0 + 128] += jnp.sum(dq_pre, axis=0, keepdims=True)
            dk_prev, dv_prev = dk_both[0:T], dv_both[0:T]
            dk_cur, dv_cur = dk_both[T:2 * T] + dk_next, dv_both[T:2 * T] + dv_next
            dk_pre = dk_cur * cosv - _rot_half(dk_cur, first_half) * sinv
            dp_ref[rows, OFF_K:OFF_K + 128] = dk_pre.astype(BF16)
            dp_ref[rows, OFF_VA:OFF_VA + 128] = dv_cur.astype(BF16)
            dbqkv_ref[:, 1024:1152] += jnp.sum(dk_pre, axis=0, keepdims=True)
            dbqkv_ref[:, 1152:1280] += jnp.sum(dv_cur, axis=0, keepdims=True)
            return dk_prev, dv_prev

        grads = carry_ref[:, 0:128], carry_ref[:, 128:256]
        for b in range(BPS - 1, 0, -1):
            prows = slice((b - 1) * T, b * T)
            grads = block(b, roped_k(b - 1), p_ref[prows, OFF_VA:OFF_VA + 128].astype(BF16), None, *grads)
        k_prev = _rope(kvp_ref[:, 0:128], cosp_ref[...], sinp_ref[...], first_half).astype(BF16)
        grads = block(0, k_prev, kvp_ref[:, 128:256].astype(BF16), i < nt - 1, *grads)
        carry_ref[:, 0:128], carry_ref[:, 128:256] = grads
        pl.when(i == nt - 1)(exchange.finish)

    return pl.pallas_call(
        body, name="mid_bwd", grid=(nt,),
        in_specs=proj_specs + [pl.BlockSpec((BPS * T, D), lambda i: (tile(i), 0))] + par_specs + [HBM_SPEC],
        out_specs=[pl.BlockSpec((BPS * T, D_IN), lambda i: (tile(i), 0)),
                   pl.BlockSpec((1, DG), const2), pl.BlockSpec((1, DG), const2),
                   pl.BlockSpec((NG, T, T), lambda i: (0, 0, 0)), pl.BlockSpec((NG, T, 1), lambda i: (0, 0, 0)),
                   pl.BlockSpec((NQ, 128), const2), pl.BlockSpec((1, D_QKV), const2), HBM_SPEC],
        out_shape=[jax.ShapeDtypeStruct((S, D_IN), BF16),
                   jax.ShapeDtypeStruct((1, DG), F32), jax.ShapeDtypeStruct((1, DG), F32),
                   jax.ShapeDtypeStruct((NG, T, T), F32), jax.ShapeDtypeStruct((NG, T, 1), F32),
                   jax.ShapeDtypeStruct((NQ, 128), F32), jax.ShapeDtypeStruct((1, D_QKV), F32),
                   jax.ShapeDtypeStruct(po.shape, BF16)],
        scratch_shapes=[pltpu.VMEM((T, 2 * T), F32), pltpu.VMEM((BPS * T, DG), F32),
                        pltpu.VMEM((NG, T, T), BF16), pltpu.VMEM((NG, T, T), BF16)] + OWNER_SEMS,
        compiler_params=_params(("arbitrary",)),
    )(proj, proj, dycat, cos, sin, cos, sin, ln_g, ln_b, w_s, b_s, sinks, po)


def _outproj_loss(ycat, wo, x, target, g_post):
    S = ycat.shape[0]
    tm = _tile(S, 512)
    nt = S // tm
    n_part = 2 if tm % 32 == 0 else 1
    tp = tm // n_part
    const2 = lambda i: (0, 0)

    def body(yc_ref, w_ref, x_ref, t_ref, g_ref, dy_ref, dout_ref, loss_ref, dg_ref, lacc_ref):
        i = pl.program_id(0)

        @pl.when(i == 0)
        def _():
            dg_ref[...] = jnp.zeros_like(dg_ref)
            lacc_ref[...] = jnp.zeros_like(lacc_ref)

        g = g_ref[...]
        ys = [_dot(yc_ref[q * tp:(q + 1) * tp, :], w_ref[...], NN) for q in range(n_part)]
        for q, y in enumerate(ys):
            rows = slice(q * tp, (q + 1) * tp)
            r = lax.rsqrt(jnp.mean(y * y, axis=-1, keepdims=True) + EPS)
            yh = y * r
            diff = x_ref[rows, :] + yh * g - t_ref[rows, :]
            lacc_ref[...] += jnp.sum(diff * diff, axis=0, keepdims=True)
            dout = diff * (1.0 / D)
            dout_ref[rows, :] = dout
            dg_ref[...] += jnp.sum(dout * yh, axis=0, keepdims=True)
            dyh = dout * g
            dy_ref[rows, :] = (r * (dyh - yh * jnp.mean(dyh * yh, axis=-1, keepdims=True))).astype(BF16)

        @pl.when(i == nt - 1)
        def _():
            loss_ref[...] = jnp.broadcast_to(jnp.sum(lacc_ref[...], axis=1, keepdims=True) * (0.5 / D), (1, 128))

    row = lambda i: (i, 0)
    return pl.pallas_call(
        body, name="outproj_loss", grid=(nt,),
        in_specs=[pl.BlockSpec((tm, D), row), pl.BlockSpec((D, D), const2, pipeline_mode=pl.Buffered(1)),
                  pl.BlockSpec((tm, D), row), pl.BlockSpec((tm, D), row), pl.BlockSpec((1, D), const2)],
        out_specs=[pl.BlockSpec((tm, D), row), pl.BlockSpec((tm, D), row), pl.BlockSpec((1, 128), const2),
                   pl.BlockSpec((1, D), const2)],
        out_shape=[jax.ShapeDtypeStruct((S, D), BF16), jax.ShapeDtypeStruct((S, D), F32),
                   jax.ShapeDtypeStruct((1, 128), F32), jax.ShapeDtypeStruct((1, D), F32)],
        scratch_shapes=[pltpu.VMEM((1, D), F32)],
        compiler_params=_params(("arbitrary",)),
    )(ycat, wo, x, target, g_post)


def _dycat(dy, wo, dwo):
    S = dy.shape[0]
    tm = _tile(S, 512)
    nt = S // tm

    def body(dy_ref, w_ref, dwo_ref, o_ref, r_ref, send_sems, recv_sems):
        i = pl.program_id(0)
        copies = _sibling_copies(dwo_ref, r_ref, W_OUT_ROWS, send_sems, recv_sems)

        @pl.when(i == 0)
        def _():
            for cp in copies:
                cp.start()

        o_ref[...] = _dot(dy_ref[...], w_ref[...], NT)

        @pl.when(i == nt - 1)
        def _():
            for cp in copies:
                cp.wait_recv()
            for cp in copies:
                cp.wait_send()

    return pl.pallas_call(
        body, name="dycat", grid=(nt,),
        in_specs=[pl.BlockSpec((tm, D), lambda i: (i, 0)),
                  pl.BlockSpec((D, D), lambda i: (0, 0), pipeline_mode=pl.Buffered(1)), HBM_SPEC],
        out_specs=[pl.BlockSpec((tm, D), lambda i: (i, 0)), HBM_SPEC],
        out_shape=[jax.ShapeDtypeStruct((S, D), F32), jax.ShapeDtypeStruct((N_CHIPS, W_OUT_ROWS, D), BF16)],
        scratch_shapes=[pltpu.SemaphoreType.DMA((N_CHIPS,)), pltpu.SemaphoreType.DMA((N_CHIPS,))],
        compiler_params=_params(("arbitrary",)),
    )(dy, wo, dwo)


def _matmul_tn(a, b, tm, name, mat_grads=None):
    K, M = a.shape
    N = b.shape[1]
    tk = _tile(K, 1024)
    ni, nk = M // tm, K // tk
    hosting = mat_grads is not None

    def body(a_ref, b_ref, *rest):
        if hosting:
            g_in, (o_ref, tot_ref, acc_ref, mbuf, send_sems, recv_sems) = rest[:len(MAT)], rest[len(MAT):]
        else:
            o_ref, acc_ref = rest
        i, k = pl.program_id(0), pl.program_id(1)

        if hosting:
            gather = _SlotGather(mbuf, send_sems, recv_sems)

            @pl.when((i == 0) & (k == 0))
            def _():
                r0 = 0
                for q, (_, rows) in enumerate(MAT):
                    mbuf[gather.my_id, r0:r0 + rows, :] = g_in[q][...]
                    r0 += rows
                gather.start()

        @pl.when(k == 0)
        def _():
            acc_ref[...] = jnp.zeros_like(acc_ref)

        acc_ref[...] += _dot(a_ref[...], b_ref[...], TN)

        @pl.when(k == nk - 1)
        def _():
            o_ref[...] = acc_ref[...].astype(BF16)

        if hosting:
            @pl.when((i == ni - 1) & (k == nk - 1))
            def _():
                gather.finish()
                tot_ref[...] = gather.total()

    whole = pl.BlockSpec(memory_space=pltpu.VMEM)
    in_specs = [pl.BlockSpec((tk, tm), lambda i, k: (k, i)), pl.BlockSpec((tk, N), lambda i, k: (k, 0))]
    out_specs = [pl.BlockSpec((tm, N), lambda i, k: (i, 0))]
    out_shape = [jax.ShapeDtypeStruct((M, N), BF16)]
    scratch = [pltpu.VMEM((tm, N), F32)]
    if hosting:
        in_specs += [whole] * len(MAT)
        out_specs.append(whole)
        out_shape.append(jax.ShapeDtypeStruct((MAT_ROWS, 128), F32))
        scratch += [pltpu.VMEM((N_DEV, MAT_ROWS, 128), F32), pltpu.SemaphoreType.DMA((7,)),
                    pltpu.SemaphoreType.DMA((7,))]
    res = pl.pallas_call(
        body, name=name, grid=(ni, nk), in_specs=in_specs, out_specs=out_specs, out_shape=out_shape,
        scratch_shapes=scratch,
        compiler_params=_params(("arbitrary", "arbitrary") if hosting else ("parallel", "arbitrary")),
    )(a, b, *(mat_grads or ()))
    return res if hosting else res[0]


def _dh_prenorm_bwd(dproj, wt, x, dout, g_pre, pt):
    S = x.shape[0]
    tm, tk = _tile(S, 1024), 768
    te = tm // 2
    ni, nk, ne = S // tm, D_IN // tk, tm // te
    per = nk + ne
    tile_of, phase_of = (lambda t: t // per), (lambda t: t % per)
    k_of = lambda t: jnp.minimum(phase_of(t), nk - 1)
    half_of = lambda t: tile_of(t) * ne + jnp.maximum(phase_of(t) - nk, 0)

    def body(dp_ref, w_ref, x_ref, dout_ref, g_ref, pt_ref, gx_ref, dg_ref, bt_ref, acc_ref,
             send_sems, recv_sems, local_sem):
        t = pl.program_id(0)
        ph = phase_of(t)
        exchange = _OwnerExchange(pt_ref, bt_ref, send_sems, recv_sems, local_sem)
        pl.when(t == 0)(exchange.start)

        @pl.when(t == 0)
        def _():
            dg_ref[...] = jnp.zeros_like(dg_ref)

        @pl.when(ph == 0)
        def _():
            acc_ref[...] = _dot(dp_ref[...], w_ref[...], NN)

        @pl.when((ph > 0) & (ph < nk))
        def _():
            acc_ref[...] += _dot(dp_ref[...], w_ref[...], NN)

        @pl.when(ph >= nk)
        def _():
            dh = acc_ref[pl.ds(pl.multiple_of((ph - nk) * te, te), te), :]
            xv = x_ref[...]
            r = lax.rsqrt(jnp.mean(xv * xv, axis=-1, keepdims=True) + EPS)
            xh = xv * r
            dg_ref[...] += jnp.sum(dh * xh, axis=0, keepdims=True)
            dxh = dh * g_ref[...]
            gx_ref[...] = dout_ref[...] + r * (dxh - xh * jnp.mean(dxh * xh, axis=-1, keepdims=True))

        pl.when(t == ni * per - 1)(exchange.finish)

    half = lambda t: (half_of(t), 0)
    return pl.pallas_call(
        body, name="dh_prenorm_bwd", grid=(ni * per,),
        in_specs=[pl.BlockSpec((tm, tk), lambda t: (tile_of(t), k_of(t))), pl.BlockSpec((tk, D), lambda t: (k_of(t), 0)),
                  pl.BlockSpec((te, D), half), pl.BlockSpec((te, D), half), pl.BlockSpec((1, D), lambda t: (0, 0)),
                  HBM_SPEC],
        out_specs=[pl.BlockSpec((te, D), half), pl.BlockSpec((1, D), lambda t: (0, 0)), HBM_SPEC],
        out_shape=[jax.ShapeDtypeStruct((S, D), F32), jax.ShapeDtypeStruct((1, D), F32),
                   jax.ShapeDtypeStruct(pt.shape, BF16)],
        scratch_shapes=[pltpu.VMEM((tm, D), F32)] + OWNER_SEMS,
        compiler_params=_params(("arbitrary",)),
    )(dproj, wt, x, dout, g_pre, pt)


def _presum(c_arr, own, recv, half_rows):
    n_cols = own.shape[-1]
    own4 = own.reshape(N_CHIPS, 2, half_rows, n_cols)

    def body(c_ref, own_ref, recv_ref, o_ref):
        o_ref[...] = (own_ref[...].astype(F32) + recv_ref[...].astype(F32)).astype(BF16)

    return pl.pallas_call(
        body, name="presum_%d" % half_rows,
        grid_spec=pltpu.PrefetchScalarGridSpec(
            num_scalar_prefetch=1, grid=(N_CHIPS,),
            in_specs=[pl.BlockSpec((None, None, half_rows, n_cols), lambda j, c: (j, c[0], 0, 0)),
                      pl.BlockSpec((None, half_rows, n_cols), lambda j, c: (j, 0, 0))],
            out_specs=pl.BlockSpec((None, half_rows, n_cols), lambda j, c: (j, 0, 0))),
        out_shape=jax.ShapeDtypeStruct((N_CHIPS, half_rows, n_cols), BF16),
        compiler_params=_params(("parallel",)),
    )(c_arr, own4, recv)


def _sum_chips(c_arr, parts, name):
    _, rows, n_cols = parts.shape
    nt = 2
    tr = rows // nt

    def body(c_ref, p_ref, o_ref):
        o_ref[...] = ((p_ref[0].astype(F32) + p_ref[1].astype(F32)) + p_ref[2].astype(F32)) + p_ref[3].astype(F32)

    return pl.pallas_call(
        body, name=name,
        grid_spec=pltpu.PrefetchScalarGridSpec(
            num_scalar_prefetch=1, grid=(nt,),
            in_specs=[pl.BlockSpec((N_CHIPS, tr, n_cols), lambda i, c: (0, i, 0))],
            out_specs=pl.BlockSpec((tr, n_cols), lambda i, c: (c[0] * nt + i, 0))),
        out_shape=jax.ShapeDtypeStruct((2 * rows, n_cols), F32),
        compiler_params=_params(("parallel",)),
    )(c_arr, parts)


def _adamw_math(w, g, m, v):
    mn = ADAM_B1 * m + (1.0 - ADAM_B1) * g
    vn = ADAM_B2 * v + (1.0 - ADAM_B2) * (g * g)
    m_hat = mn / (1.0 - ADAM_B1 ** ADAM_STEP)
    v_hat = vn / (1.0 - ADAM_B2 ** ADAM_STEP)
    return -ADAM_LR * (m_hat / (jnp.sqrt(v_hat) + ADAM_EPS) + ADAM_WD * w), mn, vn


def _adamw(w, g, m, v, name):
    R, C = w.shape
    tr = next((t for t in (256, 192, 128) if R % t == 0), R)

    def body(w_ref, g_ref, m_ref, v_ref, go_ref, d_ref, mo_ref, vo_ref):
        gv = g_ref[...]
        go_ref[...] = gv
        d_ref[...], mo_ref[...], vo_ref[...] = _adamw_math(w_ref[...], gv, m_ref[...], v_ref[...])

    spec = pl.BlockSpec((tr, C), lambda i: (i, 0))
    shp = jax.ShapeDtypeStruct((R, C), F32)
    return pl.pallas_call(
        body, name=name, grid=(R // tr,), in_specs=[spec] * 4, out_specs=[spec] * 4, out_shape=[shp] * 4,
        compiler_params=_params(("parallel",)),
    )(w, g, m, v)


HBM_SPEC = pl.BlockSpec(memory_space=pltpu.HBM)
GATHER_LOCAL_CHUNKS = 4
GATHER_SEMS = [pltpu.SemaphoreType.DMA((7,)), pltpu.SemaphoreType.DMA((7,)),
               pltpu.SemaphoreType.DMA((GATHER_LOCAL_CHUNKS,))]
OWNER_SEMS = [pltpu.SemaphoreType.DMA((3,)), pltpu.SemaphoreType.DMA((3,)), pltpu.SemaphoreType.DMA(())]


def _mesh_pos():
    return lax.axis_index("x"), lax.axis_index("y"), lax.axis_index("c")


class _RowGather:
    def __init__(self, src_ref, full_ref, rows, send_sems, recv_sems, local_sems):
        self.src, self.full, self.rows = src_ref, full_ref, rows
        self.send, self.recv, self.local = send_sems, recv_sems, local_sems
        x, y, c = _mesh_pos()
        self.c, self.me, self.sibling = c, (x, y, c), (x, y, 1 - c)
        self.chips = [(1 - x, y), (x, 1 - y), (1 - x, 1 - y)]

    def _block(self, pos):
        px, py, pc = pos
        return self.full.at[pl.ds(pl.multiple_of((4 * px + 2 * py + pc) * self.rows, 16), self.rows), :]

    def _copy(self, k, blk, to):
        return pltpu.make_async_remote_copy(
            src_ref=self.src if blk is self.me else self._block(blk), dst_ref=self._block(blk),
            send_sem=self.send.at[k], recv_sem=self.recv.at[k], device_id=to, device_id_type=MESH)

    def _mine(self):
        return _place_locally(self.src, self._block(self.me), self.local, GATHER_LOCAL_CHUNKS)

    def _first(self):
        return [self._copy(0, self.me, self.sibling)] + [
            self._copy(1 + j, self.me, (*chip, self.c)) for j, chip in enumerate(self.chips)]

    def start(self):
        for cp in self._first() + self._mine():
            cp.start()

    def finish(self):
        passed = [self._copy(4 + j, (*chip, self.c), self.sibling) for j, chip in enumerate(self.chips)]
        for j, chip in enumerate(self.chips):
            self._copy(1 + j, (*chip, self.c), self.me).wait_recv()
            passed[j].start()
        self._copy(0, self.sibling, self.me).wait_recv()
        for j, chip in enumerate(self.chips):
            self._copy(4 + j, (*chip, 1 - self.c), self.me).wait_recv()
        for cp in self._first() + passed:
            cp.wait_send()
        for cp in self._mine():
            cp.wait()


class _OwnerExchange:
    def __init__(self, src_ref, dst_ref, send_sems, recv_sems, local_sem):
        self.src, self.dst, self.send, self.recv, self.local = src_ref, dst_ref, send_sems, recv_sems, local_sem
        x, y, c = _mesh_pos()
        self.c, self.my_chip = c, 2 * x + y
        self.peers = [(1 - x, y), (x, 1 - y), (1 - x, 1 - y)]

    def _copies(self):
        local = pltpu.make_async_copy(self.src.at[self.my_chip], self.dst.at[self.my_chip], self.local)
        remote = [pltpu.make_async_remote_copy(
            src_ref=self.src.at[2 * px + py], dst_ref=self.dst.at[self.my_chip],
            send_sem=self.send.at[k], recv_sem=self.recv.at[k], device_id=(px, py, self.c), device_id_type=MESH)
            for k, (px, py) in enumerate(self.peers)]
        return local, remote

    def start(self):
        local, remote = self._copies()
        local.start()
        for cp in remote:
            cp.start()

    def finish(self):
        local, remote = self._copies()
        for cp in remote:
            cp.wait_recv()
        for cp in remote:
            cp.wait_send()
        local.wait()


def _sibling_copies(dw_ref, r_ref, rows, send_sems, recv_sems):
    x, y, c = _mesh_pos()
    return [pltpu.make_async_remote_copy(
        src_ref=dw_ref.at[pl.ds(pl.multiple_of((2 * j + (1 - c)) * rows, 16), rows), :], dst_ref=r_ref.at[j],
        send_sem=send_sems.at[j], recv_sem=recv_sems.at[j], device_id=(x, y, 1 - c), device_id_type=MESH)
        for j in range(N_CHIPS)]


def _to_sibling(dw, rows, name):
    def body(dw_ref, r_ref, send_sems, recv_sems):
        copies = _sibling_copies(dw_ref, r_ref, rows, send_sems, recv_sems)
        for cp in copies:
            cp.start()
        for cp in copies:
            cp.wait_recv()
        for cp in copies:
            cp.wait_send()

    return pl.pallas_call(
        body, name=name, in_specs=[HBM_SPEC], out_specs=HBM_SPEC,
        out_shape=jax.ShapeDtypeStruct((N_CHIPS, rows, dw.shape[1]), BF16),
        scratch_shapes=[pltpu.SemaphoreType.DMA((N_CHIPS,)), pltpu.SemaphoreType.DMA((N_CHIPS,))],
    )(dw)


PAIR_CHUNKS = 4


def _pair_halves(gt, go):
    def body(gt_in, go_in, gt_ref, go_ref, send_sems, recv_sems):
        del gt_in, go_in
        x, y, c = _mesh_pos()
        copies = []
        for a, (ref, rows) in enumerate(((gt_ref, W_IN_ROWS), (go_ref, W_OUT_ROWS))):
            ch = rows // PAIR_CHUNKS
            for q in range(PAIR_CHUNKS):
                part = ref.at[pl.ds(pl.multiple_of(c * rows + q * ch, 8), ch), :]
                copies.append(pltpu.make_async_remote_copy(
                    src_ref=part, dst_ref=part, send_sem=send_sems.at[PAIR_CHUNKS * a + q],
                    recv_sem=recv_sems.at[PAIR_CHUNKS * a + q], device_id=(x, y, 1 - c), device_id_type=MESH))
        for cp in copies:
            cp.start()
        for cp in copies:
            cp.wait_recv()
        for cp in copies:
            cp.wait_send()

    return pl.pallas_call(
        body, name="pair_halves",
        in_specs=[HBM_SPEC, HBM_SPEC], out_specs=[HBM_SPEC, HBM_SPEC],
        out_shape=[jax.ShapeDtypeStruct(gt.shape, F32), jax.ShapeDtypeStruct(go.shape, F32)],
        input_output_aliases={0: 0, 1: 1},
        scratch_shapes=[pltpu.SemaphoreType.DMA((2 * PAIR_CHUNKS,)), pltpu.SemaphoreType.DMA((2 * PAIR_CHUNKS,))],
    )(gt, go)


VEC = (("g_pre", 2048), ("g_post", 2048), ("b_qkv", 1280), ("ln_v_g", 1024), ("ln_v_b", 1024), ("attn_sinks", 16))
VEC_ROWS = 8
LOSS_ROW = len(VEC)
MAT = (("w_spatial", NG * T), ("b_spatial", NG))
MAT_ROWS = sum(r for _, r in MAT)


class _SlotGather:
    def __init__(self, buf, send_sems, recv_sems):
        self.buf, self.send, self.recv = buf, send_sems, recv_sems
        x, y, c = _mesh_pos()
        self.c, self.me, self.sibling, self.my_id = c, (x, y, c), (x, y, 1 - c), 4 * x + 2 * y + c
        self.chips = [(1 - x, y), (x, 1 - y), (1 - x, 1 - y)]

    def _copy(self, k, blk, to):
        px, py, pc = blk
        slot = self.buf.at[4 * px + 2 * py + pc]
        return pltpu.make_async_remote_copy(
            src_ref=slot, dst_ref=slot, send_sem=self.send.at[k], recv_sem=self.recv.at[k],
            device_id=to, device_id_type=MESH)

    def _first(self):
        return [self._copy(0, self.me, self.sibling)] + [
            self._copy(1 + j, self.me, (*chip, self.c)) for j, chip in enumerate(self.chips)]

    def start(self):
        for cp in self._first():
            cp.start()

    def finish(self):
        passed = [self._copy(4 + j, (*chip, self.c), self.sibling) for j, chip in enumerate(self.chips)]
        for j, chip in enumerate(self.chips):
            self._copy(1 + j, (*chip, self.c), self.me).wait_recv()
            passed[j].start()
        self._copy(0, self.sibling, self.me).wait_recv()
        for j, chip in enumerate(self.chips):
            self._copy(4 + j, (*chip, 1 - self.c), self.me).wait_recv()
        for cp in self._first() + passed:
            cp.wait_send()

    def total(self):
        t = self.buf[0]
        for d in range(1, N_DEV):
            t = t + self.buf[d]
        return t


def _small_update(vec_grads, loss_part, mat_total, vec_state, mat_state):
    n_vec, n_mat = len(VEC), len(MAT)
    n_par = n_vec + n_mat
    n_in = n_vec + 2 + 3 * n_par

    def body(*refs):
        g_in, loss_in, tot_m = refs[:n_vec], refs[n_vec], refs[n_vec + 1]
        st_in = refs[n_vec + 2:n_in]
        outs, loss_out = refs[n_in:n_in + 4 * n_par], refs[n_in + 4 * n_par]
        vbuf, tot_v, send_sems, recv_sems = refs[n_in + 4 * n_par + 1:]
        gather = _SlotGather(vbuf, send_sems, recv_sems)
        vbuf[gather.my_id] = jnp.zeros((VEC_ROWS, D), F32)
        for r, (_, n) in enumerate(VEC):
            vbuf[gather.my_id, r:r + 1, 0:n] = g_in[r][...]
        vbuf[gather.my_id, LOSS_ROW:LOSS_ROW + 1, 0:128] = loss_in[...]
        gather.start()
        gather.finish()
        tot_v[...] = gather.total()
        loss_out[...] = tot_v[LOSS_ROW:LOSS_ROW + 1, 0:128]
        r0 = 0
        for q in range(n_par):
            if q < n_vec:
                g = tot_v[q:q + 1, 0:VEC[q][1]]
            else:
                rows = MAT[q - n_vec][1]
                g = tot_m[r0:r0 + rows, :]
                r0 += rows
            w, m, v = (st_in[3 * q + t][...] for t in range(3))
            outs[4 * q][...] = g
            outs[4 * q + 1][...], outs[4 * q + 2][...], outs[4 * q + 3][...] = _adamw_math(w, g, m, v)

    state = [a for wmv in list(vec_state) + list(mat_state) for a in wmv]
    vmem = pl.BlockSpec(memory_space=pltpu.VMEM)
    shapes = [g.shape for g in vec_grads] + [wmv[0].shape for wmv in mat_state]
    out_shape = [jax.ShapeDtypeStruct(s, F32) for s in shapes for _ in range(4)]
    out_shape.append(jax.ShapeDtypeStruct((1, 128), F32))
    res = pl.pallas_call(
        body, name="small_update",
        in_specs=[vmem] * n_in, out_specs=[vmem] * len(out_shape), out_shape=out_shape,
        scratch_shapes=[pltpu.VMEM((N_DEV, VEC_ROWS, D), F32), pltpu.VMEM((VEC_ROWS, D), F32),
                        pltpu.SemaphoreType.DMA((7,)), pltpu.SemaphoreType.DMA((7,))],
        compiler_params=pltpu.CompilerParams(vmem_limit_bytes=VMEM_LIMIT),
    )(*vec_grads, loss_part, mat_total, *state)
    return [res[4 * q:4 * q + 4] for q in range(n_par)], res[-1]


def kernel(x, positions, g_pre, w_in, b_qkv, ln_v_g, ln_v_b, w_spatial, b_spatial, attn_sinks, w_out, g_post, loss_target, m_g_pre, m_w_in, m_b_qkv, m_ln_v_g, m_ln_v_b, m_w_spatial, m_b_spatial, m_attn_sinks, m_w_out, m_g_post, v_g_pre, v_w_in, v_b_qkv, v_ln_v_g, v_ln_v_b, v_w_spatial, v_b_spatial, v_attn_sinks, v_w_out, v_g_post):
    S = x.shape[1]
    c = lax.axis_index("c")
    c_arr = jnp.reshape(c, (1,)).astype(jnp.int32)
    x2 = x[0]
    target = loss_target[0]
    pos = positions.reshape(S, 1)
    half = HD // 2
    inv_freq = ROPE_THETA ** (-jnp.arange(half, dtype=F32) * (2.0 / HD))
    invf = jnp.tile(inv_freq, 128 // half).reshape(1, 128)
    bias = jnp.concatenate([jnp.zeros((OFF_Q,), F32), b_qkv[0], jnp.zeros((D_IN - OFF_ZB,), F32)]).reshape(1, D_IN)
    b_s_col = b_spatial[0].reshape(NG, T, 1)
    sinks = jnp.repeat(attn_sinks[0], T).reshape(NQ * T, 1)

    chip = 2 * lax.axis_index("x") + lax.axis_index("y")
    wt_part = lax.dynamic_slice_in_dim(w_in[0].T.astype(BF16), c * W_IN_ROWS, W_IN_ROWS, axis=0)
    wo_part = lax.dynamic_slice_in_dim(w_out[0].astype(BF16), c * W_OUT_ROWS, W_OUT_ROWS, axis=0)
    sched = jnp.asarray(PROJ_SCHEDULE, jnp.int32)[chip]

    h, proj, wt, cos, sin = _prenorm_inproj(sched, x2, g_pre, bias, wt_part, pos, invf)
    ycat, wo = _mid_fwd(proj, cos, sin, ln_v_g, ln_v_b, w_spatial[0], b_s_col, sinks, wo_part)
    dy, dout, loss_part, dg_post = _outproj_loss(ycat, wo, x2, target, g_post)

    dwo = _matmul_tn(ycat, dy, 512, "dw_out")
    dycat, ro = _dycat(dy, wo, dwo)
    po = _presum(c_arr, dwo, ro, W_OUT_ROWS)
    dproj, dln_g, dln_b, dws, dbs, dsink, dbqkv, bo = _mid_bwd(
        proj, dycat, cos, sin, ln_v_g, ln_v_b, w_spatial[0], b_s_col, sinks, po)
    dwt, mat_total = _matmul_tn(dproj, h, 768, "dw_in_t",
                                mat_grads=[dws.reshape(NG * T, T), dbs.reshape(NG, T)])
    pt = _presum(c_arr, dwt, _to_sibling(dwt, W_IN_ROWS, "to_sibling_in"), W_IN_ROWS)
    grad_x, dg_pre, bt = _dh_prenorm_bwd(dproj, wt, x2, dout, g_pre, pt)
    gt, go = _pair_halves(_sum_chips(c_arr, bt, "sum_chips_in"), _sum_chips(c_arr, bo, "sum_chips_out"))

    g_w_in, d_w_in, nm_w_in, nv_w_in = (a.T for a in _adamw(w_in[0].T, gt, m_w_in[0].T, v_w_in[0].T, "adamw_w_in"))
    g_w_out, d_w_out, nm_w_out, nv_w_out = _adamw(w_out[0], go, m_w_out[0], v_w_out[0], "adamw_w_out")

    state = {"g_pre": (g_pre, m_g_pre, v_g_pre), "g_post": (g_post, m_g_post, v_g_post),
             "b_qkv": (b_qkv, m_b_qkv, v_b_qkv), "ln_v_g": (ln_v_g, m_ln_v_g, v_ln_v_g),
             "ln_v_b": (ln_v_b, m_ln_v_b, v_ln_v_b), "attn_sinks": (attn_sinks, m_attn_sinks, v_attn_sinks),
             "w_spatial": tuple(a.reshape(NG * T, T) for a in (w_spatial, m_w_spatial, v_w_spatial)),
             "b_spatial": tuple(a.reshape(NG, T) for a in (b_spatial, m_b_spatial, v_b_spatial))}
    grads = {"g_pre": dg_pre, "g_post": dg_post, "b_qkv": dbqkv, "ln_v_g": dln_g, "ln_v_b": dln_b,
             "attn_sinks": dsink[:, 0].reshape(1, NQ)}
    results, loss = _small_update([grads[n] for n, _ in VEC], loss_part, mat_total,
                                  [state[n] for n, _ in VEC], [state[n] for n, _ in MAT])
    small = {n: [a.reshape(w.shape) for a in res]
             for (n, _), res, w in zip(VEC + MAT, results, [state[n][0] for n, _ in VEC + MAT])}
    small["w_spatial"] = [a.reshape(w_spatial.shape) for a in small["w_spatial"]]
    small["b_spatial"] = [a.reshape(b_spatial.shape) for a in small["b_spatial"]]
    big = {"w_in": [a[None] for a in (g_w_in, d_w_in, nm_w_in, nv_w_in)],
           "w_out": [a[None] for a in (g_w_out, d_w_out, nm_w_out, nv_w_out)]}
    order = ("g_pre", "w_in", "b_qkv", "ln_v_g", "ln_v_b", "w_spatial", "b_spatial", "attn_sinks", "w_out", "g_post")
    leaves = {**small, **big}
    return (loss[0, 0], grad_x[None], *[leaves[n][t] for t in range(4) for n in order])
```

```python
import jax
import jax.numpy as jnp
from jax import lax
from jax.experimental import pallas as pl
from jax.experimental.pallas import tpu as pltpu

F32 = jnp.float32
BF16 = jnp.bfloat16
MESH = pl.DeviceIdType.MESH

D = 2048
DG = 1024
T = 128
NG = 8
HD = 64
NQ = 16
D_IN = 5376
OFF_U, OFF_V, OFF_ZA, OFF_Q, OFF_K, OFF_VA, OFF_ZB = 0, 1024, 2048, 3072, 4096, 4224, 4352
D_QKV = 1280
EPS = 1e-6
ROPE_THETA = 10000.0
N_CHIPS = 4
N_DEV = 8
W_IN_ROWS = D_IN // N_DEV
W_OUT_ROWS = D // N_DEV

ADAM_LR, ADAM_B1, ADAM_B2, ADAM_EPS, ADAM_WD, ADAM_STEP = 0.001, 0.9, 0.999, 1e-08, 0.01, 10

VMEM_LIMIT = 56 * 1024 * 1024


def _tile(n, pref):
    return pref if n % pref == 0 else n


def _params(sem=None, vmem=VMEM_LIMIT):
    return pltpu.CompilerParams(dimension_semantics=sem, vmem_limit_bytes=vmem)


def _sigmoid(z):
    return 1.0 / (1.0 + jnp.exp(-z))


def _row_mean(v):
    return jnp.mean(v, axis=-1, keepdims=True)


def _dot(a, b, dims):
    return lax.dot_general(a, b, (dims, ((), ())), preferred_element_type=F32)


NN = ((1,), (0,))
NT = ((1,), (1,))
TN = ((0,), (0,))


PROJ_TN = 768
N_PROJ_TILES = D_IN // PROJ_TN
PROJ_SCHEDULE = ((0, 2, 4, 1, 3, 6, 5, 5), (2, 0, 6, 1, 4, 3, 5, 4), (4, 6, 0, 5, 2, 1, 3, 4), (6, 4, 2, 5, 3, 0, 1, 5))


LOCAL_CHUNKS = 6


def _place_locally(src_ref, dst_rows_ref, sems, n_chunks=LOCAL_CHUNKS):
    ch = src_ref.shape[0] // n_chunks
    return [pltpu.make_async_copy(src_ref.at[pl.ds(q * ch, ch), :], dst_rows_ref.at[pl.ds(q * ch, ch), :], sems.at[q])
            for q in range(n_chunks)]


def _prenorm_inproj(sched, x, g, bias, wt_part, pos, invf):
    S = x.shape[0]
    tp, tm = _tile(S, 512), _tile(S, 1024)
    n_pre, ns = S // tp, S // tm
    n_steps = n_pre + N_PROJ_TILES * ns
    pos_of = lambda i: jnp.maximum(i - n_pre, 0) // ns
    row_of = lambda i: jnp.maximum(i - n_pre, 0) % ns

    def body(sched_ref, x_ref, g_ref, b_ref, wpart_ref, pos_ref, invf_ref, h_ref, proj_ref, wt_ref, cos_ref, sin_ref,
             h_all, w_tile, stage, send_sems, recv_sems, w_sems, local_sems):
        i = pl.program_id(0)
        x_, y_, c = _mesh_pos()
        me, sibling = (x_, y_, c), (x_, y_, 1 - c)
        chips = [(1 - x_, y_), (x_, 1 - y_), (1 - x_, 1 - y_)]

        def block(pos):
            px, py, pc = pos
            return wt_ref.at[pl.ds(pl.multiple_of((4 * px + 2 * py + pc) * W_IN_ROWS, 16), W_IN_ROWS), :]

        def copy(k, blk, to):
            return pltpu.make_async_remote_copy(
                src_ref=wpart_ref if blk is me else block(blk), dst_ref=block(blk),
                send_sem=send_sems.at[k], recv_sem=recv_sems.at[k], device_id=to, device_id_type=MESH)

        stage_in = pltpu.make_async_copy(wpart_ref, stage, local_sems.at[0])
        stage_out = pltpu.make_async_copy(stage, block(me), local_sems.at[1])

        relay = (jnp.where(c == 0, x_, 1 - x_), jnp.where(c == 0, 1 - y_, y_))
        relayed = (jnp.where(c == 0, 1 - x_, x_), jnp.where(c == 0, y_, 1 - y_))

        def own_sends():
            return [copy(0, me, sibling), copy(2, me, (*chips[1], c)), copy(1, me, (*chips[0], c))]

        def passed_on():
            return [copy(4, (*chips[0], c), sibling), copy(5, (*chips[1], c), sibling), copy(3, (*relayed, c), (*relay, c))]

        def neighbours_arrive():
            copy(1, (*chips[0], c), me).wait_recv()
            copy(2, (*chips[1], c), me).wait_recv()
            for cp in passed_on():
                cp.start()
            copy(5, (*chips[1], 1 - c), me).wait_recv()

        def diagonal_arrives():
            copy(3, (*chips[2], c), me).wait_recv()
            copy(6, (*chips[2], c), sibling).start()
            copy(6, (*chips[2], 1 - c), me).wait_recv()

        def tile_load(p):
            slot = p % 2
            rows = wt_ref.at[pl.ds(pl.multiple_of(sched_ref[p] * PROJ_TN, 16), PROJ_TN), :]
            return pltpu.make_async_copy(rows, w_tile.at[slot], w_sems.at[slot])

        def prepare(p):
            p = jnp.asarray(p, jnp.int32)

            @pl.when(p == 0)
            def _():
                copy(0, sibling, me).wait_recv()
                stage_out.wait()

            pl.when(p == 1)(neighbours_arrive)
            pl.when(p == 2)(lambda: copy(4, (*chips[0], 1 - c), me).wait_recv())
            pl.when(p == sched_ref[N_PROJ_TILES])(diagonal_arrives)
            tile_load(p).start()

        @pl.when(i == 0)
        def _():
            stage_in.start()
            for cp in own_sends():
                cp.start()
            stage_in.wait()
            stage_out.start()

        @pl.when(i < n_pre)
        def _():
            xv = x_ref[...]
            r = lax.rsqrt(_row_mean(xv * xv) + EPS)
            hv = (xv * r * g_ref[...]).astype(BF16)
            h_ref[...] = hv
            h_all[pl.ds(pl.multiple_of(i * tp, tp), tp), :] = hv
            ang = pos_ref[...].astype(F32) * invf_ref[...]
            cos_ref[...] = jnp.cos(ang)
            sin_ref[...] = jnp.sin(ang)

        pl.when(i == n_pre - 1)(lambda: prepare(0))

        @pl.when(i >= n_pre)
        def _():
            p, s = pos_of(i), row_of(i)
            pl.when(s == 0)(lambda: tile_load(p).wait())
            pl.when((s == ns - 1) & (p < N_PROJ_TILES - 1))(lambda: prepare(p + 1))
            hv = h_all[pl.ds(pl.multiple_of(s * tm, tm), tm), :]
            proj_ref[...] = _dot(hv, w_tile[p % 2], NT) + b_ref[...]

        @pl.when(i == n_steps - 1)
        def _():
            for cp in own_sends() + passed_on() + [copy(6, (*chips[2], c), sibling)]:
                cp.wait_send()

    return pl.pallas_call(
        body, name="prenorm_inproj",
        grid_spec=pltpu.PrefetchScalarGridSpec(
            num_scalar_prefetch=1, grid=(n_steps,),
            in_specs=[pl.BlockSpec((tp, D), lambda i, sc: (jnp.minimum(i, n_pre - 1), 0)),
                      pl.BlockSpec((1, D), lambda i, sc: (0, 0)),
                      pl.BlockSpec((1, PROJ_TN), lambda i, sc: (0, sc[pos_of(i)])),
                      HBM_SPEC,
                      pl.BlockSpec((tp, 1), lambda i, sc: (jnp.minimum(i, n_pre - 1), 0)),
                      pl.BlockSpec((1, 128), lambda i, sc: (0, 0))],
            out_specs=[pl.BlockSpec((tp, D), lambda i, sc: (jnp.minimum(i, n_pre - 1), 0)),
                       pl.BlockSpec((tm, PROJ_TN), lambda i, sc: (row_of(i), sc[pos_of(i)])),
                       HBM_SPEC,
                       pl.BlockSpec((tp, 128), lambda i, sc: (jnp.minimum(i, n_pre - 1), 0)),
                       pl.BlockSpec((tp, 128), lambda i, sc: (jnp.minimum(i, n_pre - 1), 0))],
            scratch_shapes=[pltpu.VMEM((S, D), BF16), pltpu.VMEM((2, PROJ_TN, D), BF16),
                            pltpu.VMEM((W_IN_ROWS, D), BF16),
                            pltpu.SemaphoreType.DMA((7,)), pltpu.SemaphoreType.DMA((7,)),
                            pltpu.SemaphoreType.DMA((2,)), pltpu.SemaphoreType.DMA((2,))]),
        out_shape=[jax.ShapeDtypeStruct((S, D), BF16), jax.ShapeDtypeStruct((S, D_IN), F32),
                   jax.ShapeDtypeStruct((D_IN, D), BF16),
                   jax.ShapeDtypeStruct((S, 128), F32), jax.ShapeDtypeStruct((S, 128), F32)],
        compiler_params=_params(("arbitrary",)),
    )(sched, x, g, bias, wt_part, pos, invf)


def _rot_half(xs, first_half):
    return jnp.where(first_half, -pltpu.roll(xs, 96, 1), pltpu.roll(xs, 32, 1))


GH = NQ // 2


MASKED = -1e30


def _attn_consts():
    lane = lax.broadcasted_iota(jnp.int32, (T, 128), 1)
    row = lax.broadcasted_iota(jnp.int32, (GH * T, T), 0) & (T - 1)
    on_diag_or_below = row >= lax.broadcasted_iota(jnp.int32, (GH * T, T), 1)
    return (lane & (HD - 1)) < (HD // 2), lane < HD, on_diag_or_below


def _stack_heads(slab_fn, grp, lo64):
    blocks = []
    for jj in range(GH // 2):
        s = slab_fn(GH // 2 * grp + jj)
        blocks += [jnp.where(lo64, s, 0.0), jnp.where(lo64, 0.0, s)]
    return jnp.concatenate(blocks, axis=0)


def _unstack_heads(stacked, jj, lo64):
    return jnp.where(lo64, stacked[(2 * jj) * T:(2 * jj + 1) * T], stacked[(2 * jj + 1) * T:(2 * jj + 2) * T])


def _both_halves(kv, grp):
    lo = lax.broadcasted_iota(jnp.int32, kv.shape, 1) < HD
    swapped = pltpu.roll(kv, HD, 1)
    return jnp.where(lo, kv, swapped) if grp == 0 else jnp.where(lo, swapped, kv)


def _one_half(acc, grp):
    lo = lax.broadcasted_iota(jnp.int32, acc.shape, 1) < HD
    return jnp.where(lo if grp == 0 else jnp.logical_not(lo), acc + pltpu.roll(acc, HD, 1), 0.0)


def _layer_norm_stats(v):
    mu = _row_mean(v)
    xc = v - mu
    var = _row_mean(xc * xc)
    rs = lax.rsqrt(var + EPS)
    return xc * rs, rs


def _fold(both, own):
    return jnp.where(own, both[:, T:2 * T], both[:, 0:T])


def _unfold(p, own):
    return jnp.concatenate([jnp.where(own, 0.0, p), jnp.where(own, p, 0.0)], axis=1).astype(BF16)


def _band_softmax(q_scaled, k_both, own, has_prev, sink):
    s_both = _dot(q_scaled, k_both, NT)
    s_prev = s_both[:, 0:T]
    if has_prev is not None:
        s_prev = s_prev + jnp.where(has_prev, 0.0, MASKED)
    s = jnp.where(own, s_both[:, T:2 * T], s_prev)
    m = jnp.maximum(jnp.max(s, axis=-1, keepdims=True), sink)
    e = jnp.exp(s - m)
    es = jnp.exp(sink - m)
    inv = 1.0 / (jnp.sum(e, axis=-1, keepdims=True) + es)
    return e * inv, es * inv


BPS_FWD = 4
BPS_BWD = 2


def _mid_specs(nt, rev, bps):
    tile = (lambda i: nt - 1 - i) if rev else (lambda i: i)
    prev = lambda i: jnp.maximum(bps * tile(i) - 1, 0)
    rows = bps * T
    return tile, [
        pl.BlockSpec((rows, D_IN), lambda i: (tile(i), 0)),
        pl.BlockSpec((T, 2 * T), lambda i: (prev(i), OFF_K // (2 * T))),
    ], [
        pl.BlockSpec((rows, 128), lambda i: (tile(i), 0)),
        pl.BlockSpec((rows, 128), lambda i: (tile(i), 0)),
        pl.BlockSpec((T, 128), lambda i: (prev(i), 0)),
        pl.BlockSpec((T, 128), lambda i: (prev(i), 0)),
        pl.BlockSpec((1, DG), lambda i: (0, 0)),
        pl.BlockSpec((1, DG), lambda i: (0, 0)),
        pl.BlockSpec((NG, T, T), lambda i: (0, 0, 0)),
        pl.BlockSpec((NG, T, 1), lambda i: (0, 0, 0)),
        pl.BlockSpec((NQ * T, 1), lambda i: (0, 0)),
    ]


def _rope(xs, cosv, sinv, first_half):
    return xs * cosv + _rot_half(xs, first_half) * sinv


def _mid_fwd(proj, cos, sin, ln_g, ln_b, w_s, b_s, sinks, wo_part):
    S = proj.shape[0]
    BPS = BPS_FWD if S % (BPS_FWD * T) == 0 else 1
    nt = S // (BPS * T)
    tile, proj_specs, par_specs = _mid_specs(nt, False, BPS)

    def body(p_ref, kvp_ref, cos_ref, sin_ref, cosp_ref, sinp_ref, lng_ref, lnb_ref, ws_ref, bs_ref, sink_ref, wpart_ref,
             y_ref, wo_ref, wm_ref, send_sems, recv_sems, local_sems):
        i = pl.program_id(0)
        first_half, lo64, own = _attn_consts()
        gather = _RowGather(wpart_ref, wo_ref, W_OUT_ROWS, send_sems, recv_sems, local_sems)
        pl.when(i == 0)(gather.start)

        @pl.when(i == 0)
        def _():
            tril = lax.broadcasted_iota(jnp.int32, (T, T), 0) >= lax.broadcasted_iota(jnp.int32, (T, T), 1)
            for g in range(NG):
                wm_ref[g] = jnp.where(tril, ws_ref[g], 0.0).astype(BF16)

        def block(b, k_prev, v_prev, has_prev):
            rows = slice(b * T, (b + 1) * T)
            xhat, _ = _layer_norm_stats(p_ref[rows, OFF_V:OFF_V + DG])
            vn = xhat * lng_ref[...] + lnb_ref[...]
            for g in range(NG):
                sl = slice(128 * g, 128 * g + 128)
                mixed = _dot(wm_ref[g], vn[:, sl].astype(BF16), NN) + bs_ref[g]
                z = p_ref[rows, OFF_ZA + 128 * g:OFF_ZA + 128 * g + 128]
                u = p_ref[rows, OFF_U + 128 * g:OFF_U + 128 * g + 128]
                y_ref[rows, sl] = (u * mixed * (z * _sigmoid(z))).astype(BF16)

            cosv, sinv = cos_ref[rows, :], sin_ref[rows, :]
            k_cur = _rope(p_ref[rows, OFF_K:OFF_K + 128], cosv, sinv, first_half)
            v_cur = p_ref[rows, OFF_VA:OFF_VA + 128]

            def q_slab(j):
                return _rope(p_ref[rows, OFF_Q + 128 * j:OFF_Q + 128 * j + 128], cosv, sinv, first_half) * (HD ** -0.5)

            k_both = jnp.concatenate([k_prev, k_cur], axis=0)
            v_both = jnp.concatenate([v_prev, v_cur], axis=0)
            for grp in range(2):
                qs = _stack_heads(q_slab, grp, lo64).astype(BF16)
                kg, vg = _both_halves(k_both, grp).astype(BF16), _both_halves(v_both, grp).astype(BF16)
                p, _ = _band_softmax(qs, kg, own, has_prev, sink_ref[GH * T * grp:GH * T * (grp + 1), :])
                o_st = _dot(_unfold(p, own), vg, NN)
                for jj in range(GH // 2):
                    c0 = 128 * (GH // 2 * grp + jj)
                    zb = p_ref[rows, OFF_ZB + c0:OFF_ZB + c0 + 128]
                    o = _unstack_heads(o_st, jj, lo64)
                    y_ref[rows, DG + c0:DG + c0 + 128] = (o * (zb * _sigmoid(zb))).astype(BF16)
            return k_cur, v_cur

        k_prev = _rope(kvp_ref[:, 0:128], cosp_ref[...], sinp_ref[...], first_half)
        kv = block(0, k_prev, kvp_ref[:, 128:256], i > 0)
        for b in range(1, BPS):
            kv = block(b, *kv, None)
        pl.when(i == nt - 1)(gather.finish)

    return pl.pallas_call(
        body, name="mid_fwd", grid=(nt,),
        in_specs=proj_specs + par_specs + [HBM_SPEC],
        out_specs=[pl.BlockSpec((BPS * T, D), lambda i: (tile(i), 0)), HBM_SPEC],
        out_shape=[jax.ShapeDtypeStruct((S, D), BF16), jax.ShapeDtypeStruct((D, D), BF16)],
        scratch_shapes=[pltpu.VMEM((NG, T, T), BF16)] + GATHER_SEMS,
        compiler_params=_params(("arbitrary",)),
    )(proj, proj, cos, sin, cos, sin, ln_g, ln_b, w_s, b_s, sinks, wo_part)


def _mid_bwd(proj, dycat, cos, sin, ln_g, ln_b, w_s, b_s, sinks, po):
    S = proj.shape[0]
    BPS = BPS_BWD if S % (BPS_BWD * T) == 0 else 1
    nt = S // (BPS * T)
    tile, proj_specs, par_specs = _mid_specs(nt, True, BPS)
    const2 = lambda i: (0, 0)

    def body(p_ref, kvp_ref, dyc_ref, cos_ref, sin_ref, cosp_ref, sinp_ref, lng_ref, lnb_ref, ws_ref, bs_ref, sink_ref,
             po_ref, dp_ref, dlng_ref, dlnb_ref, dws_ref, dbs_ref, dsink_ref, dbqkv_ref, bo_ref,
             carry_ref, dvn_ref, wm_ref, wmt_ref, send_sems, recv_sems, local_sem):
        i = pl.program_id(0)
        first_half, lo64, own = _attn_consts()
        tril = lax.broadcasted_iota(jnp.int32, (T, T), 0) >= lax.broadcasted_iota(jnp.int32, (T, T), 1)
        exchange = _OwnerExchange(po_ref, bo_ref, send_sems, recv_sems, local_sem)
        pl.when(i == 0)(exchange.start)

        @pl.when(i == 0)
        def _():
            for g in range(NG):
                wm = jnp.where(tril, ws_ref[g], 0.0)
                wm_ref[g] = wm.astype(BF16)
                wmt_ref[g] = wm.T.astype(BF16)
            dlng_ref[...] = jnp.zeros_like(dlng_ref)
            dlnb_ref[...] = jnp.zeros_like(dlnb_ref)
            dws_ref[...] = jnp.zeros_like(dws_ref)
            dbs_ref[...] = jnp.zeros_like(dbs_ref)
            dsink_ref[...] = jnp.zeros_like(dsink_ref)
            dbqkv_ref[...] = jnp.zeros_like(dbqkv_ref)
            carry_ref[...] = jnp.zeros_like(carry_ref)

        def roped_k(b):
            rows = slice(b * T, (b + 1) * T)
            return _rope(p_ref[rows, OFF_K:OFF_K + 128], cos_ref[rows, :], sin_ref[rows, :], first_half)

        def block(b, k_prev, v_prev, has_prev, dk_next, dv_next):
            rows = slice(b * T, (b + 1) * T)
            xhat, rs = _layer_norm_stats(p_ref[rows, OFF_V:OFF_V + DG])
            lng = lng_ref[...]
            vn = xhat * lng + lnb_ref[...]
            for g in range(NG):
                sl = slice(128 * g, 128 * g + 128)
                vng = vn[:, sl].astype(BF16)
                mixed = _dot(wm_ref[g], vng, NN) + bs_ref[g]
                z = p_ref[rows, OFF_ZA + 128 * g:OFF_ZA + 128 * g + 128]
                u = p_ref[rows, OFF_U + 128 * g:OFF_U + 128 * g + 128]
                dy = dyc_ref[rows, sl]
                sg = _sigmoid(z)
                sa = z * sg
                dp_ref[rows, OFF_U + 128 * g:OFF_U + 128 * g + 128] = (dy * mixed * sa).astype(BF16)
                dp_ref[rows, OFF_ZA + 128 * g:OFF_ZA + 128 * g + 128] = (
                    dy * u * mixed * (sg * (1.0 + z * (1.0 - sg)))).astype(BF16)
                dm = dy * u * sa
                dmb = dm.astype(BF16)
                dvn_ref[rows, sl] = _dot(wmt_ref[g], dmb, NN)
                dws_ref[g] += jnp.where(tril, _dot(dmb, vng, NT), 0.0)
                dbs_ref[g] += jnp.sum(dm, axis=1, keepdims=True)
            dvn = dvn_ref[rows, :]
            dlng_ref[...] += jnp.sum(dvn * xhat, axis=0, keepdims=True)
            dlnb_ref[...] += jnp.sum(dvn, axis=0, keepdims=True)
            dxh = dvn * lng
            dv_g = rs * (dxh - _row_mean(dxh)
                         - xhat * _row_mean(dxh * xhat))
            dp_ref[rows, OFF_V:OFF_V + DG] = dv_g.astype(BF16)

            cosv, sinv = cos_ref[rows, :], sin_ref[rows, :]
            k_cur = roped_k(b)
            v_cur = p_ref[rows, OFF_VA:OFF_VA + 128]

            def q_slab(j):
                return _rope(p_ref[rows, OFF_Q + 128 * j:OFF_Q + 128 * j + 128], cosv, sinv, first_half) * (HD ** -0.5)

            def do_slab(j):
                zb = p_ref[rows, OFF_ZB + 128 * j:OFF_ZB + 128 * j + 128]
                return dyc_ref[rows, DG + 128 * j:DG + 128 * j + 128] * (zb * _sigmoid(zb))

            k_both = jnp.concatenate([k_prev, k_cur], axis=0)
            v_both = jnp.concatenate([v_prev, v_cur], axis=0)
            dk_both, dv_both = jnp.zeros((2 * T, 128), F32), jnp.zeros((2 * T, 128), F32)
            for grp in range(2):
                qs = _stack_heads(q_slab, grp, lo64).astype(BF16)
                d_o = _stack_heads(do_slab, grp, lo64)
                dob = d_o.astype(BF16)
                kg, vg = _both_halves(k_both, grp).astype(BF16), _both_halves(v_both, grp).astype(BF16)
                p, ps = _band_softmax(qs, kg, own, has_prev, sink_ref[GH * T * grp:GH * T * (grp + 1), :])
                p_both = _unfold(p, own)
                o_st = _dot(p_both, vg, NN)
                delta = jnp.sum(d_o * o_st, axis=-1, keepdims=True)
                ds_both = _unfold(p * (_fold(_dot(dob, vg, NT), own) - delta), own)
                dq_st = _dot(ds_both, kg, NN) * (HD ** -0.5)
                dk_both = dk_both + _one_half(_dot(ds_both, qs, TN), grp)
                dv_both = dv_both + _one_half(_dot(p_both, dob, TN), grp)
                dsink_rows = ps * delta
                for hh in range(GH):
                    h = GH * grp + hh
                    dsink_ref[h:h + 1, :] += jnp.broadcast_to(
                        -jnp.sum(dsink_rows[hh * T:(hh + 1) * T], axis=0, keepdims=True), (1, 128))
                for jj in range(GH // 2):
                    c0 = 128 * (GH // 2 * grp + jj)
                    zb = p_ref[rows, OFF_ZB + c0:OFF_ZB + c0 + 128]
                    sg = _sigmoid(zb)
                    o = _unstack_heads(o_st, jj, lo64)
                    dp_ref[rows, OFF_ZB + c0:OFF_ZB + c0 + 128] = (
                        dyc_ref[rows, DG + c0:DG + c0 + 128] * o * (sg * (1.0 + zb * (1.0 - sg)))).astype(BF16)
                    dq = _unstack_heads(dq_st, jj, lo64)
                    dq_pre = dq * cosv - _rot_half(dq, first_half) * sinv
                    dp_ref[rows, OFF_Q + c0:OFF_Q + c0 + 128] = dq_pre.astype(BF16)
                    dbqkv_ref[:, c0:c0 + 128] += jnp.sum(dq_pre, axis=0, keepdims=True)
            dk_prev, dv_prev = dk_both[0:T], dv_both[0:T]
            dk_cur, dv_cur = dk_both[T:2 * T] + dk_next, dv_both[T:2 * T] + dv_next
            dk_pre = dk_cur * cosv - _rot_half(dk_cur, first_half) * sinv
            dp_ref[rows, OFF_K:OFF_K + 128] = dk_pre.astype(BF16)
            dp_ref[rows, OFF_VA:OFF_VA + 128] = dv_cur.astype(BF16)
            dbqkv_ref[:, 1024:1152] += jnp.sum(dk_pre, axis=0, keepdims=True)
            dbqkv_ref[:, 1152:1280] += jnp.sum(dv_cur, axis=0, keepdims=True)
            return dk_prev, dv_prev

        grads = carry_ref[:, 0:128], carry_ref[:, 128:256]
        for b in range(BPS - 1, 0, -1):
            prows = slice((b - 1) * T, b * T)
            grads = block(b, roped_k(b - 1), p_ref[prows, OFF_VA:OFF_VA + 128], None, *grads)
        k_prev = _rope(kvp_ref[:, 0:128], cosp_ref[...], sinp_ref[...], first_half)
        grads = block(0, k_prev, kvp_ref[:, 128:256], i < nt - 1, *grads)
        carry_ref[:, 0:128], carry_ref[:, 128:256] = grads
        pl.when(i == nt - 1)(exchange.finish)

    return pl.pallas_call(
        body, name="mid_bwd", grid=(nt,),
        in_specs=proj_specs + [pl.BlockSpec((BPS * T, D), lambda i: (tile(i), 0))] + par_specs + [HBM_SPEC],
        out_specs=[pl.BlockSpec((BPS * T, D_IN), lambda i: (tile(i), 0)),
                   pl.BlockSpec((1, DG), const2), pl.BlockSpec((1, DG), const2),
                   pl.BlockSpec((NG, T, T), lambda i: (0, 0, 0)), pl.BlockSpec((NG, T, 1), lambda i: (0, 0, 0)),
                   pl.BlockSpec((NQ, 128), const2), pl.BlockSpec((1, D_QKV), const2), HBM_SPEC],
        out_shape=[jax.ShapeDtypeStruct((S, D_IN), BF16),
                   jax.ShapeDtypeStruct((1, DG), F32), jax.ShapeDtypeStruct((1, DG), F32),
                   jax.ShapeDtypeStruct((NG, T, T), F32), jax.ShapeDtypeStruct((NG, T, 1), F32),
                   jax.ShapeDtypeStruct((NQ, 128), F32), jax.ShapeDtypeStruct((1, D_QKV), F32),
                   jax.ShapeDtypeStruct(po.shape, BF16)],
        scratch_shapes=[pltpu.VMEM((T, 2 * T), F32), pltpu.VMEM((BPS * T, DG), F32),
                        pltpu.VMEM((NG, T, T), BF16), pltpu.VMEM((NG, T, T), BF16)] + OWNER_SEMS,
        compiler_params=_params(("arbitrary",)),
    )(proj, proj, dycat, cos, sin, cos, sin, ln_g, ln_b, w_s, b_s, sinks, po)


def _outproj_loss(ycat, wo, x, target, g_post):
    S = ycat.shape[0]
    tm = _tile(S, 512)
    nt = S // tm
    n_part = 2 if tm % 32 == 0 else 1
    tp = tm // n_part
    const2 = lambda i: (0, 0)

    def body(yc_ref, w_ref, x_ref, t_ref, g_ref, dy_ref, dout_ref, loss_ref, dg_ref, lacc_ref):
        i = pl.program_id(0)

        @pl.when(i == 0)
        def _():
            dg_ref[...] = jnp.zeros_like(dg_ref)
            lacc_ref[...] = jnp.zeros_like(lacc_ref)

        g = g_ref[...]
        ys = [_dot(yc_ref[q * tp:(q + 1) * tp, :], w_ref[...], NN) for q in range(n_part)]
        for q, y in enumerate(ys):
            rows = slice(q * tp, (q + 1) * tp)
            r = lax.rsqrt(_row_mean(y * y) + EPS)
            yh = y * r
            diff = x_ref[rows, :] + yh * g - t_ref[rows, :]
            lacc_ref[...] += jnp.sum(diff * diff, axis=0, keepdims=True)
            dout = diff * (1.0 / D)
            dout_ref[rows, :] = dout
            dg_ref[...] += jnp.sum(dout * yh, axis=0, keepdims=True)
            dyh = dout * g
            dy_ref[rows, :] = (r * (dyh - yh * _row_mean(dyh * yh))).astype(BF16)

        @pl.when(i == nt - 1)
        def _():
            loss_ref[...] = jnp.broadcast_to(jnp.sum(lacc_ref[...], axis=1, keepdims=True) * (0.5 / D), (1, 128))

    row = lambda i: (i, 0)
    return pl.pallas_call(
        body, name="outproj_loss", grid=(nt,),
        in_specs=[pl.BlockSpec((tm, D), row), pl.BlockSpec((D, D), const2, pipeline_mode=pl.Buffered(1)),
                  pl.BlockSpec((tm, D), row), pl.BlockSpec((tm, D), row), pl.BlockSpec((1, D), const2)],
        out_specs=[pl.BlockSpec((tm, D), row), pl.BlockSpec((tm, D), row), pl.BlockSpec((1, 128), const2),
                   pl.BlockSpec((1, D), const2)],
        out_shape=[jax.ShapeDtypeStruct((S, D), BF16), jax.ShapeDtypeStruct((S, D), F32),
                   jax.ShapeDtypeStruct((1, 128), F32), jax.ShapeDtypeStruct((1, D), F32)],
        scratch_shapes=[pltpu.VMEM((1, D), F32)],
        compiler_params=_params(("arbitrary",)),
    )(ycat, wo, x, target, g_post)


def _dycat(dy, wo, dwo):
    S = dy.shape[0]
    tm = _tile(S, 512)
    nt = S // tm

    def body(dy_ref, w_ref, dwo_ref, o_ref, r_ref, send_sems, recv_sems):
        i = pl.program_id(0)
        copies = _sibling_copies(dwo_ref, r_ref, W_OUT_ROWS, send_sems, recv_sems)

        @pl.when(i == 0)
        def _():
            for cp in copies:
                cp.start()

        o_ref[...] = _dot(dy_ref[...], w_ref[...], NT)

        @pl.when(i == nt - 1)
        def _():
            for cp in copies:
                cp.wait_recv()
            for cp in copies:
                cp.wait_send()

    return pl.pallas_call(
        body, name="dycat", grid=(nt,),
        in_specs=[pl.BlockSpec((tm, D), lambda i: (i, 0)),
                  pl.BlockSpec((D, D), lambda i: (0, 0), pipeline_mode=pl.Buffered(1)), HBM_SPEC],
        out_specs=[pl.BlockSpec((tm, D), lambda i: (i, 0)), HBM_SPEC],
        out_shape=[jax.ShapeDtypeStruct((S, D), F32), jax.ShapeDtypeStruct((N_CHIPS, W_OUT_ROWS, D), BF16)],
        scratch_shapes=[pltpu.SemaphoreType.DMA((N_CHIPS,)), pltpu.SemaphoreType.DMA((N_CHIPS,))],
        compiler_params=_params(("arbitrary",)),
    )(dy, wo, dwo)


def _matmul_tn(a, b, tm, name, mat_grads=None):
    K, M = a.shape
    N = b.shape[1]
    tk = _tile(K, 1024)
    ni, nk = M // tm, K // tk
    hosting = mat_grads is not None

    def body(a_ref, b_ref, *rest):
        if hosting:
            g_in, (o_ref, tot_ref, acc_ref, mbuf, send_sems, recv_sems) = rest[:len(MAT)], rest[len(MAT):]
        else:
            o_ref, acc_ref = rest
        i, k = pl.program_id(0), pl.program_id(1)

        if hosting:
            gather = _SlotGather(mbuf, send_sems, recv_sems)

            @pl.when((i == 0) & (k == 0))
            def _():
                r0 = 0
                for q, (_, rows) in enumerate(MAT):
                    mbuf[gather.my_id, r0:r0 + rows, :] = g_in[q][...]
                    r0 += rows
                gather.start()

        @pl.when(k == 0)
        def _():
            acc_ref[...] = jnp.zeros_like(acc_ref)

        acc_ref[...] += _dot(a_ref[...], b_ref[...], TN)

        @pl.when(k == nk - 1)
        def _():
            o_ref[...] = acc_ref[...].astype(BF16)

        if hosting:
            @pl.when((i == ni - 1) & (k == nk - 1))
            def _():
                gather.finish()
                tot_ref[...] = gather.total()

    whole = pl.BlockSpec(memory_space=pltpu.VMEM)
    in_specs = [pl.BlockSpec((tk, tm), lambda i, k: (k, i)), pl.BlockSpec((tk, N), lambda i, k: (k, 0))]
    out_specs = [pl.BlockSpec((tm, N), lambda i, k: (i, 0))]
    out_shape = [jax.ShapeDtypeStruct((M, N), BF16)]
    scratch = [pltpu.VMEM((tm, N), F32)]
    if hosting:
        in_specs += [whole] * len(MAT)
        out_specs.append(whole)
        out_shape.append(jax.ShapeDtypeStruct((MAT_ROWS, 128), F32))
        scratch += [pltpu.VMEM((N_DEV, MAT_ROWS, 128), F32), pltpu.SemaphoreType.DMA((7,)),
                    pltpu.SemaphoreType.DMA((7,))]
    res = pl.pallas_call(
        body, name=name, grid=(ni, nk), in_specs=in_specs, out_specs=out_specs, out_shape=out_shape,
        scratch_shapes=scratch,
        compiler_params=_params(("arbitrary", "arbitrary") if hosting else ("parallel", "arbitrary")),
    )(a, b, *(mat_grads or ()))
    return res if hosting else res[0]


def _dh_prenorm_bwd(dproj, wt, x, dout, g_pre, pt):
    S = x.shape[0]
    tm, tk = _tile(S, 1024), 768
    te = tm // 2
    ni, nk, ne = S // tm, D_IN // tk, tm // te
    per = nk + ne
    tile_of, phase_of = (lambda t: t // per), (lambda t: t % per)
    k_of = lambda t: jnp.minimum(phase_of(t), nk - 1)
    half_of = lambda t: tile_of(t) * ne + jnp.maximum(phase_of(t) - nk, 0)

    def body(dp_ref, w_ref, x_ref, dout_ref, g_ref, pt_ref, gx_ref, dg_ref, bt_ref, acc_ref,
             send_sems, recv_sems, local_sem):
        t = pl.program_id(0)
        ph = phase_of(t)
        exchange = _OwnerExchange(pt_ref, bt_ref, send_sems, recv_sems, local_sem)
        pl.when(t == 0)(exchange.start)

        @pl.when(t == 0)
        def _():
            dg_ref[...] = jnp.zeros_like(dg_ref)

        @pl.when(ph == 0)
        def _():
            acc_ref[...] = _dot(dp_ref[...], w_ref[...], NN)

        @pl.when((ph > 0) & (ph < nk))
        def _():
            acc_ref[...] += _dot(dp_ref[...], w_ref[...], NN)

        @pl.when(ph >= nk)
        def _():
            dh = acc_ref[pl.ds(pl.multiple_of((ph - nk) * te, te), te), :]
            xv = x_ref[...]
            r = lax.rsqrt(_row_mean(xv * xv) + EPS)
            xh = xv * r
            dg_ref[...] += jnp.sum(dh * xh, axis=0, keepdims=True)
            dxh = dh * g_ref[...]
            gx_ref[...] = dout_ref[...] + r * (dxh - xh * _row_mean(dxh * xh))

        pl.when(t == ni * per - 1)(exchange.finish)

    half = lambda t: (half_of(t), 0)
    return pl.pallas_call(
        body, name="dh_prenorm_bwd", grid=(ni * per,),
        in_specs=[pl.BlockSpec((tm, tk), lambda t: (tile_of(t), k_of(t))), pl.BlockSpec((tk, D), lambda t: (k_of(t), 0)),
                  pl.BlockSpec((te, D), half), pl.BlockSpec((te, D), half), pl.BlockSpec((1, D), lambda t: (0, 0)),
                  HBM_SPEC],
        out_specs=[pl.BlockSpec((te, D), half), pl.BlockSpec((1, D), lambda t: (0, 0)), HBM_SPEC],
        out_shape=[jax.ShapeDtypeStruct((S, D), F32), jax.ShapeDtypeStruct((1, D), F32),
                   jax.ShapeDtypeStruct(pt.shape, BF16)],
        scratch_shapes=[pltpu.VMEM((tm, D), F32)] + OWNER_SEMS,
        compiler_params=_params(("arbitrary",)),
    )(dproj, wt, x, dout, g_pre, pt)


def _presum(c_arr, own, recv, half_rows):
    n_cols = own.shape[-1]
    own4 = own.reshape(N_CHIPS, 2, half_rows, n_cols)

    def body(c_ref, own_ref, recv_ref, o_ref):
        o_ref[...] = (own_ref[...].astype(F32) + recv_ref[...].astype(F32)).astype(BF16)

    return pl.pallas_call(
        body, name="presum_%d" % half_rows,
        grid_spec=pltpu.PrefetchScalarGridSpec(
            num_scalar_prefetch=1, grid=(N_CHIPS,),
            in_specs=[pl.BlockSpec((None, None, half_rows, n_cols), lambda j, c: (j, c[0], 0, 0)),
                      pl.BlockSpec((None, half_rows, n_cols), lambda j, c: (j, 0, 0))],
            out_specs=pl.BlockSpec((None, half_rows, n_cols), lambda j, c: (j, 0, 0))),
        out_shape=jax.ShapeDtypeStruct((N_CHIPS, half_rows, n_cols), BF16),
        compiler_params=_params(("parallel",)),
    )(c_arr, own4, recv)


def _sum_chips(c_arr, parts, name):
    _, rows, n_cols = parts.shape
    nt = 2
    tr = rows // nt

    def body(c_ref, p_ref, o_ref):
        o_ref[...] = ((p_ref[0].astype(F32) + p_ref[1].astype(F32)) + p_ref[2].astype(F32)) + p_ref[3].astype(F32)

    return pl.pallas_call(
        body, name=name,
        grid_spec=pltpu.PrefetchScalarGridSpec(
            num_scalar_prefetch=1, grid=(nt,),
            in_specs=[pl.BlockSpec((N_CHIPS, tr, n_cols), lambda i, c: (0, i, 0))],
            out_specs=pl.BlockSpec((tr, n_cols), lambda i, c: (c[0] * nt + i, 0))),
        out_shape=jax.ShapeDtypeStruct((2 * rows, n_cols), F32),
        compiler_params=_params(("parallel",)),
    )(c_arr, parts)


def _adamw_math(w, g, m, v):
    mn = ADAM_B1 * m + (1.0 - ADAM_B1) * g
    vn = ADAM_B2 * v + (1.0 - ADAM_B2) * (g * g)
    m_hat = mn / (1.0 - ADAM_B1 ** ADAM_STEP)
    v_hat = vn / (1.0 - ADAM_B2 ** ADAM_STEP)
    return -ADAM_LR * (m_hat / (jnp.sqrt(v_hat) + ADAM_EPS) + ADAM_WD * w), mn, vn


def _adamw(w, g, m, v, name):
    R, C = w.shape
    tr = next((t for t in (256, 192, 128) if R % t == 0), R)

    def body(w_ref, g_ref, m_ref, v_ref, go_ref, d_ref, mo_ref, vo_ref):
        gv = g_ref[...]
        go_ref[...] = gv
        d_ref[...], mo_ref[...], vo_ref[...] = _adamw_math(w_ref[...], gv, m_ref[...], v_ref[...])

    spec = pl.BlockSpec((tr, C), lambda i: (i, 0))
    shp = jax.ShapeDtypeStruct((R, C), F32)
    return pl.pallas_call(
        body, name=name, grid=(R // tr,), in_specs=[spec] * 4, out_specs=[spec] * 4, out_shape=[shp] * 4,
        compiler_params=_params(("parallel",)),
    )(w, g, m, v)


HBM_SPEC = pl.BlockSpec(memory_space=pltpu.HBM)
GATHER_LOCAL_CHUNKS = 4
GATHER_SEMS = [pltpu.SemaphoreType.DMA((7,)), pltpu.SemaphoreType.DMA((7,)),
               pltpu.SemaphoreType.DMA((GATHER_LOCAL_CHUNKS,))]
OWNER_SEMS = [pltpu.SemaphoreType.DMA((3,)), pltpu.SemaphoreType.DMA((3,)), pltpu.SemaphoreType.DMA(())]


def _mesh_pos():
    return lax.axis_index("x"), lax.axis_index("y"), lax.axis_index("c")


class _RowGather:
    def __init__(self, src_ref, full_ref, rows, send_sems, recv_sems, local_sems):
        self.src, self.full, self.rows = src_ref, full_ref, rows
        self.send, self.recv, self.local = send_sems, recv_sems, local_sems
        x, y, c = _mesh_pos()
        self.c, self.me, self.sibling = c, (x, y, c), (x, y, 1 - c)
        self.chips = [(1 - x, y), (x, 1 - y), (1 - x, 1 - y)]

    def _block(self, pos):
        px, py, pc = pos
        return self.full.at[pl.ds(pl.multiple_of((4 * px + 2 * py + pc) * self.rows, 16), self.rows), :]

    def _copy(self, k, blk, to):
        return pltpu.make_async_remote_copy(
            src_ref=self.src if blk is self.me else self._block(blk), dst_ref=self._block(blk),
            send_sem=self.send.at[k], recv_sem=self.recv.at[k], device_id=to, device_id_type=MESH)

    def _mine(self):
        return _place_locally(self.src, self._block(self.me), self.local, GATHER_LOCAL_CHUNKS)

    def _first(self):
        return [self._copy(0, self.me, self.sibling)] + [
            self._copy(1 + j, self.me, (*chip, self.c)) for j, chip in enumerate(self.chips)]

    def start(self):
        for cp in self._first() + self._mine():
            cp.start()

    def finish(self):
        passed = [self._copy(4 + j, (*chip, self.c), self.sibling) for j, chip in enumerate(self.chips)]
        for j, chip in enumerate(self.chips):
            self._copy(1 + j, (*chip, self.c), self.me).wait_recv()
            passed[j].start()
        self._copy(0, self.sibling, self.me).wait_recv()
        for j, chip in enumerate(self.chips):
            self._copy(4 + j, (*chip, 1 - self.c), self.me).wait_recv()
        for cp in self._first() + passed:
            cp.wait_send()
        for cp in self._mine():
            cp.wait()


class _OwnerExchange:
    def __init__(self, src_ref, dst_ref, send_sems, recv_sems, local_sem):
        self.src, self.dst, self.send, self.recv, self.local = src_ref, dst_ref, send_sems, recv_sems, local_sem
        x, y, c = _mesh_pos()
        self.c, self.my_chip = c, 2 * x + y
        self.peers = [(1 - x, y), (x, 1 - y), (1 - x, 1 - y)]

    def _copies(self):
        local = pltpu.make_async_copy(self.src.at[self.my_chip], self.dst.at[self.my_chip], self.local)
        remote = [pltpu.make_async_remote_copy(
            src_ref=self.src.at[2 * px + py], dst_ref=self.dst.at[self.my_chip],
            send_sem=self.send.at[k], recv_sem=self.recv.at[k], device_id=(px, py, self.c), device_id_type=MESH)
            for k, (px, py) in enumerate(self.peers)]
        return local, remote

    def start(self):
        local, remote = self._copies()
        local.start()
        for cp in remote:
            cp.start()

    def finish(self):
        local, remote = self._copies()
        for cp in remote:
            cp.wait_recv()
        for cp in remote:
            cp.wait_send()
        local.wait()


def _sibling_copies(dw_ref, r_ref, rows, send_sems, recv_sems):
    x, y, c = _mesh_pos()
    return [pltpu.make_async_remote_copy(
        src_ref=dw_ref.at[pl.ds(pl.multiple_of((2 * j + (1 - c)) * rows, 16), rows), :], dst_ref=r_ref.at[j],
        send_sem=send_sems.at[j], recv_sem=recv_sems.at[j], device_id=(x, y, 1 - c), device_id_type=MESH)
        for j in range(N_CHIPS)]


def _to_sibling(dw, rows, name):
    def body(dw_ref, r_ref, send_sems, recv_sems):
        copies = _sibling_copies(dw_ref, r_ref, rows, send_sems, recv_sems)
        for cp in copies:
            cp.start()
        for cp in copies:
            cp.wait_recv()
        for cp in copies:
            cp.wait_send()

    return pl.pallas_call(
        body, name=name, in_specs=[HBM_SPEC], out_specs=HBM_SPEC,
        out_shape=jax.ShapeDtypeStruct((N_CHIPS, rows, dw.shape[1]), BF16),
        scratch_shapes=[pltpu.SemaphoreType.DMA((N_CHIPS,)), pltpu.SemaphoreType.DMA((N_CHIPS,))],
    )(dw)


PAIR_CHUNKS = 4


def _pair_halves(gt, go):
    def body(gt_in, go_in, gt_ref, go_ref, send_sems, recv_sems):
        del gt_in, go_in
        x, y, c = _mesh_pos()
        copies = []
        for a, (ref, rows) in enumerate(((gt_ref, W_IN_ROWS), (go_ref, W_OUT_ROWS))):
            ch = rows // PAIR_CHUNKS
            for q in range(PAIR_CHUNKS):
                part = ref.at[pl.ds(pl.multiple_of(c * rows + q * ch, 8), ch), :]
                copies.append(pltpu.make_async_remote_copy(
                    src_ref=part, dst_ref=part, send_sem=send_sems.at[PAIR_CHUNKS * a + q],
                    recv_sem=recv_sems.at[PAIR_CHUNKS * a + q], device_id=(x, y, 1 - c), device_id_type=MESH))
        for cp in copies:
            cp.start()
        for cp in copies:
            cp.wait_recv()
        for cp in copies:
            cp.wait_send()

    return pl.pallas_call(
        body, name="pair_halves",
        in_specs=[HBM_SPEC, HBM_SPEC], out_specs=[HBM_SPEC, HBM_SPEC],
        out_shape=[jax.ShapeDtypeStruct(gt.shape, F32), jax.ShapeDtypeStruct(go.shape, F32)],
        input_output_aliases={0: 0, 1: 1},
        scratch_shapes=[pltpu.SemaphoreType.DMA((2 * PAIR_CHUNKS,)), pltpu.SemaphoreType.DMA((2 * PAIR_CHUNKS,))],
    )(gt, go)


VEC = (("g_pre", 2048), ("g_post", 2048), ("b_qkv", 1280), ("ln_v_g", 1024), ("ln_v_b", 1024), ("attn_sinks", 16))
VEC_ROWS = 8
LOSS_ROW = len(VEC)
MAT = (("w_spatial", NG * T), ("b_spatial", NG))
MAT_ROWS = sum(r for _, r in MAT)


class _SlotGather:
    def __init__(self, buf, send_sems, recv_sems):
        self.buf, self.send, self.recv = buf, send_sems, recv_sems
        x, y, c = _mesh_pos()
        self.c, self.me, self.sibling, self.my_id = c, (x, y, c), (x, y, 1 - c), 4 * x + 2 * y + c
        self.chips = [(1 - x, y), (x, 1 - y), (1 - x, 1 - y)]

    def _copy(self, k, blk, to):
        px, py, pc = blk
        slot = self.buf.at[4 * px + 2 * py + pc]
        return pltpu.make_async_remote_copy(
            src_ref=slot, dst_ref=slot, send_sem=self.send.at[k], recv_sem=self.recv.at[k],
            device_id=to, device_id_type=MESH)

    def _first(self):
        return [self._copy(0, self.me, self.sibling)] + [
            self._copy(1 + j, self.me, (*chip, self.c)) for j, chip in enumerate(self.chips)]

    def start(self):
        for cp in self._first():
            cp.start()

    def finish(self):
        passed = [self._copy(4 + j, (*chip, self.c), self.sibling) for j, chip in enumerate(self.chips)]
        for j, chip in enumerate(self.chips):
            self._copy(1 + j, (*chip, self.c), self.me).wait_recv()
            passed[j].start()
        self._copy(0, self.sibling, self.me).wait_recv()
        for j, chip in enumerate(self.chips):
            self._copy(4 + j, (*chip, 1 - self.c), self.me).wait_recv()
        for cp in self._first() + passed:
            cp.wait_send()

    def total(self):
        t = self.buf[0]
        for d in range(1, N_DEV):
            t = t + self.buf[d]
        return t


def _small_update(vec_grads, loss_part, mat_total, vec_state, mat_state):
    n_vec, n_mat = len(VEC), len(MAT)
    n_par = n_vec + n_mat
    n_in = n_vec + 2 + 3 * n_par

    def body(*refs):
        g_in, loss_in, tot_m = refs[:n_vec], refs[n_vec], refs[n_vec + 1]
        st_in = refs[n_vec + 2:n_in]
        outs, loss_out = refs[n_in:n_in + 4 * n_par], refs[n_in + 4 * n_par]
        vbuf, tot_v, send_sems, recv_sems = refs[n_in + 4 * n_par + 1:]
        gather = _SlotGather(vbuf, send_sems, recv_sems)
        vbuf[gather.my_id] = jnp.zeros((VEC_ROWS, D), F32)
        for r, (_, n) in enumerate(VEC):
            vbuf[gather.my_id, r:r + 1, 0:n] = g_in[r][...]
        vbuf[gather.my_id, LOSS_ROW:LOSS_ROW + 1, 0:128] = loss_in[...]
        gather.start()
        gather.finish()
        tot_v[...] = gather.total()
        loss_out[...] = tot_v[LOSS_ROW:LOSS_ROW + 1, 0:128]
        r0 = 0
        for q in range(n_par):
            if q < n_vec:
                g = tot_v[q:q + 1, 0:VEC[q][1]]
            else:
                rows = MAT[q - n_vec][1]
                g = tot_m[r0:r0 + rows, :]
                r0 += rows
            w, m, v = (st_in[3 * q + t][...] for t in range(3))
            outs[4 * q][...] = g
            outs[4 * q + 1][...], outs[4 * q + 2][...], outs[4 * q + 3][...] = _adamw_math(w, g, m, v)

    state = [a for wmv in list(vec_state) + list(mat_state) for a in wmv]
    vmem = pl.BlockSpec(memory_space=pltpu.VMEM)
    shapes = [g.shape for g in vec_grads] + [wmv[0].shape for wmv in mat_state]
    out_shape = [jax.ShapeDtypeStruct(s, F32) for s in shapes for _ in range(4)]
    out_shape.append(jax.ShapeDtypeStruct((1, 128), F32))
    res = pl.pallas_call(
        body, name="small_update",
        in_specs=[vmem] * n_in, out_specs=[vmem] * len(out_shape), out_shape=out_shape,
        scratch_shapes=[pltpu.VMEM((N_DEV, VEC_ROWS, D), F32), pltpu.VMEM((VEC_ROWS, D), F32),
                        pltpu.SemaphoreType.DMA((7,)), pltpu.SemaphoreType.DMA((7,))],
        compiler_params=pltpu.CompilerParams(vmem_limit_bytes=VMEM_LIMIT),
    )(*vec_grads, loss_part, mat_total, *state)
    return [res[4 * q:4 * q + 4] for q in range(n_par)], res[-1]


def kernel(x, positions, g_pre, w_in, b_qkv, ln_v_g, ln_v_b, w_spatial, b_spatial, attn_sinks, w_out, g_post, loss_target, m_g_pre, m_w_in, m_b_qkv, m_ln_v_g, m_ln_v_b, m_w_spatial, m_b_spatial, m_attn_sinks, m_w_out, m_g_post, v_g_pre, v_w_in, v_b_qkv, v_ln_v_g, v_ln_v_b, v_w_spatial, v_b_spatial, v_attn_sinks, v_w_out, v_g_post):
    S = x.shape[1]
    c = lax.axis_index("c")
    c_arr = jnp.reshape(c, (1,)).astype(jnp.int32)
    x2 = x[0]
    target = loss_target[0]
    pos = positions.reshape(S, 1)
    half = HD // 2
    inv_freq = ROPE_THETA ** (-jnp.arange(half, dtype=F32) * (2.0 / HD))
    invf = jnp.tile(inv_freq, 128 // half).reshape(1, 128)
    bias = jnp.concatenate([jnp.zeros((OFF_Q,), F32), b_qkv[0], jnp.zeros((D_IN - OFF_ZB,), F32)]).reshape(1, D_IN)
    b_s_col = b_spatial[0].reshape(NG, T, 1)
    sinks = jnp.repeat(attn_sinks[0], T).reshape(NQ * T, 1)

    chip = 2 * lax.axis_index("x") + lax.axis_index("y")
    wt_part = lax.dynamic_slice_in_dim(w_in[0].T.astype(BF16), c * W_IN_ROWS, W_IN_ROWS, axis=0)
    wo_part = lax.dynamic_slice_in_dim(w_out[0].astype(BF16), c * W_OUT_ROWS, W_OUT_ROWS, axis=0)
    sched = jnp.asarray(PROJ_SCHEDULE, jnp.int32)[chip]

    h, proj, wt, cos, sin = _prenorm_inproj(sched, x2, g_pre, bias, wt_part, pos, invf)
    ycat, wo = _mid_fwd(proj, cos, sin, ln_v_g, ln_v_b, w_spatial[0], b_s_col, sinks, wo_part)
    dy, dout, loss_part, dg_post = _outproj_loss(ycat, wo, x2, target, g_post)

    dwo = _matmul_tn(ycat, dy, 512, "dw_out")
    dycat, ro = _dycat(dy, wo, dwo)
    po = _presum(c_arr, dwo, ro, W_OUT_ROWS)
    dproj, dln_g, dln_b, dws, dbs, dsink, dbqkv, bo = _mid_bwd(
        proj, dycat, cos, sin, ln_v_g, ln_v_b, w_spatial[0], b_s_col, sinks, po)
    dwt, mat_total = _matmul_tn(dproj, h, 768, "dw_in_t",
                                mat_grads=[dws.reshape(NG * T, T), dbs.reshape(NG, T)])
    pt = _presum(c_arr, dwt, _to_sibling(dwt, W_IN_ROWS, "to_sibling_in"), W_IN_ROWS)
    grad_x, dg_pre, bt = _dh_prenorm_bwd(dproj, wt, x2, dout, g_pre, pt)
    gt, go = _pair_halves(_sum_chips(c_arr, bt, "sum_chips_in"), _sum_chips(c_arr, bo, "sum_chips_out"))

    g_w_in, d_w_in, nm_w_in, nv_w_in = (a.T for a in _adamw(w_in[0].T, gt, m_w_in[0].T, v_w_in[0].T, "adamw_w_in"))
    g_w_out, d_w_out, nm_w_out, nv_w_out = _adamw(w_out[0], go, m_w_out[0], v_w_out[0], "adamw_w_out")

    state = {"g_pre": (g_pre, m_g_pre, v_g_pre), "g_post": (g_post, m_g_post, v_g_post),
             "b_qkv": (b_qkv, m_b_qkv, v_b_qkv), "ln_v_g": (ln_v_g, m_ln_v_g, v_ln_v_g),
             "ln_v_b": (ln_v_b, m_ln_v_b, v_ln_v_b), "attn_sinks": (attn_sinks, m_attn_sinks, v_attn_sinks),
             "w_spatial": tuple(a.reshape(NG * T, T) for a in (w_spatial, m_w_spatial, v_w_spatial)),
             "b_spatial": tuple(a.reshape(NG, T) for a in (b_spatial, m_b_spatial, v_b_spatial))}
    grads = {"g_pre": dg_pre, "g_post": dg_post, "b_qkv": dbqkv, "ln_v_g": dln_g, "ln_v_b": dln_b,
             "attn_sinks": dsink[:, 0].reshape(1, NQ)}
    results, loss = _small_update([grads[n] for n, _ in VEC], loss_part, mat_total,
                                  [state[n] for n, _ in VEC], [state[n] for n, _ in MAT])
    small = {n: [a.reshape(w.shape) for a in res]
             for (n, _), res, w in zip(VEC + MAT, results, [state[n][0] for n, _ in VEC + MAT])}
    small["w_spatial"] = [a.reshape(w_spatial.shape) for a in small["w_spatial"]]
    small["b_spatial"] = [a.reshape(b_spatial.shape) for a in small["b_spatial"]]
    big = {"w_in": [a[None] for a in (g_w_in, d_w_in, nm_w_in, nv_w_in)],
           "w_out": [a[None] for a in (g_w_out, d_w_out, nm_w_out, nv_w_out)]}
    order = ("g_pre", "w_in", "b_qkv", "ln_v_g", "ln_v_b", "w_spatial", "b_spatial", "attn_sinks", "w_out", "g_post")
    leaves = {**small, **big}
    return (loss[0, 0], grad_x[None], *[leaves[n][t] for t in range(4) for n in order])
```

```python
import jax
import jax.numpy as jnp
from jax import lax
from jax.experimental import pallas as pl
from jax.experimental.pallas import tpu as pltpu

F32 = jnp.float32
BF16 = jnp.bfloat16
MESH = pl.DeviceIdType.MESH

D = 2048
DG = 1024
T = 128
NG = 8
HD = 64
NQ = 16
D_IN = 5376
OFF_U, OFF_V, OFF_ZA, OFF_Q, OFF_K, OFF_VA, OFF_ZB = 0, 1024, 2048, 3072, 4096, 4224, 4352
D_QKV = 1280
EPS = 1e-6
ROPE_THETA = 10000.0
N_CHIPS = 4
N_DEV = 8
W_IN_ROWS = D_IN // N_DEV
W_OUT_ROWS = D // N_DEV

ADAM_LR, ADAM_B1, ADAM_B2, ADAM_EPS, ADAM_WD, ADAM_STEP = 0.001, 0.9, 0.999, 1e-08, 0.01, 10

VMEM_LIMIT = 56 * 1024 * 1024


def _tile(n, pref):
    return pref if n % pref == 0 else n


def _params(sem=None, vmem=VMEM_LIMIT):
    return pltpu.CompilerParams(dimension_semantics=sem, vmem_limit_bytes=vmem)


def _sigmoid(z):
    return 1.0 / (1.0 + jnp.exp(-z))


def _row_mean(v):
    return jnp.mean(v, axis=-1, keepdims=True)


def _dot(a, b, dims):
    return lax.dot_general(a, b, (dims, ((), ())), preferred_element_type=F32)


NN = ((1,), (0,))
NT = ((1,), (1,))
TN = ((0,), (0,))


PROJ_TN = 768
N_PROJ_TILES = D_IN // PROJ_TN
PROJ_SCHEDULE = ((0, 2, 4, 1, 3, 6, 5, 5), (2, 0, 6, 1, 4, 3, 5, 4), (4, 6, 0, 5, 2, 1, 3, 4), (6, 4, 2, 5, 3, 0, 1, 5))


LOCAL_CHUNKS = 6


def _place_locally(src_ref, dst_rows_ref, sems, n_chunks=LOCAL_CHUNKS):
    ch = src_ref.shape[0] // n_chunks
    return [pltpu.make_async_copy(src_ref.at[pl.ds(q * ch, ch), :], dst_rows_ref.at[pl.ds(q * ch, ch), :], sems.at[q])
            for q in range(n_chunks)]


def _prenorm_inproj(sched, x, g, bias, wt_part, pos, invf):
    S = x.shape[0]
    tp, tm = _tile(S, 512), _tile(S, 1024)
    n_pre, ns = S // tp, S // tm
    n_steps = n_pre + N_PROJ_TILES * ns
    pos_of = lambda i: jnp.maximum(i - n_pre, 0) // ns
    row_of = lambda i: jnp.maximum(i - n_pre, 0) % ns

    def body(sched_ref, x_ref, g_ref, b_ref, wpart_ref, pos_ref, invf_ref, h_ref, proj_ref, wt_ref, cos_ref, sin_ref,
             h_all, w_tile, stage, send_sems, recv_sems, w_sems, local_sems):
        i = pl.program_id(0)
        x_, y_, c = _mesh_pos()
        me, sibling = (x_, y_, c), (x_, y_, 1 - c)
        chips = [(1 - x_, y_), (x_, 1 - y_), (1 - x_, 1 - y_)]

        def block(pos):
            px, py, pc = pos
            return wt_ref.at[pl.ds(pl.multiple_of((4 * px + 2 * py + pc) * W_IN_ROWS, 16), W_IN_ROWS), :]

        def copy(k, blk, to):
            return pltpu.make_async_remote_copy(
                src_ref=wpart_ref if blk is me else block(blk), dst_ref=block(blk),
                send_sem=send_sems.at[k], recv_sem=recv_sems.at[k], device_id=to, device_id_type=MESH)

        stage_in = pltpu.make_async_copy(wpart_ref, stage, local_sems.at[0])
        stage_out = pltpu.make_async_copy(stage, block(me), local_sems.at[1])

        relay = (jnp.where(c == 0, x_, 1 - x_), jnp.where(c == 0, 1 - y_, y_))
        relayed = (jnp.where(c == 0, 1 - x_, x_), jnp.where(c == 0, y_, 1 - y_))

        def own_sends():
            return [copy(0, me, sibling), copy(2, me, (*chips[1], c)), copy(1, me, (*chips[0], c))]

        def passed_on():
            return [copy(4, (*chips[0], c), sibling), copy(5, (*chips[1], c), sibling), copy(3, (*relayed, c), (*relay, c))]

        def neighbours_arrive():
            copy(1, (*chips[0], c), me).wait_recv()
            copy(2, (*chips[1], c), me).wait_recv()
            for cp in passed_on():
                cp.start()
            copy(5, (*chips[1], 1 - c), me).wait_recv()

        def diagonal_arrives():
            copy(3, (*chips[2], c), me).wait_recv()
            copy(6, (*chips[2], c), sibling).start()
            copy(6, (*chips[2], 1 - c), me).wait_recv()

        def tile_load(p):
            slot = p % 2
            rows = wt_ref.at[pl.ds(pl.multiple_of(sched_ref[p] * PROJ_TN, 16), PROJ_TN), :]
            return pltpu.make_async_copy(rows, w_tile.at[slot], w_sems.at[slot])

        def prepare(p):
            p = jnp.asarray(p, jnp.int32)

            @pl.when(p == 0)
            def _():
                copy(0, sibling, me).wait_recv()
                stage_out.wait()

            pl.when(p == 1)(neighbours_arrive)
            pl.when(p == 2)(lambda: copy(4, (*chips[0], 1 - c), me).wait_recv())
            pl.when(p == sched_ref[N_PROJ_TILES])(diagonal_arrives)
            tile_load(p).start()

        @pl.when(i == 0)
        def _():
            stage_in.start()
            for cp in own_sends():
                cp.start()
            stage_in.wait()
            stage_out.start()

        @pl.when(i < n_pre)
        def _():
            xv = x_ref[...]
            r = lax.rsqrt(_row_mean(xv * xv) + EPS)
            hv = (xv * r * g_ref[...]).astype(BF16)
            h_ref[...] = hv
            h_all[pl.ds(pl.multiple_of(i * tp, tp), tp), :] = hv
            ang = pos_ref[...].astype(F32) * invf_ref[...]
            cos_ref[...] = jnp.cos(ang)
            sin_ref[...] = jnp.sin(ang)

        pl.when(i == n_pre - 1)(lambda: prepare(0))

        @pl.when(i >= n_pre)
        def _():
            p, s = pos_of(i), row_of(i)
            pl.when(s == 0)(lambda: tile_load(p).wait())
            pl.when((s == ns - 1) & (p < N_PROJ_TILES - 1))(lambda: prepare(p + 1))
            hv = h_all[pl.ds(pl.multiple_of(s * tm, tm), tm), :]
            proj_ref[...] = _dot(hv, w_tile[p % 2], NT) + b_ref[...]

        @pl.when(i == n_steps - 1)
        def _():
            for cp in own_sends() + passed_on() + [copy(6, (*chips[2], c), sibling)]:
                cp.wait_send()

    return pl.pallas_call(
        body, name="prenorm_inproj",
        grid_spec=pltpu.PrefetchScalarGridSpec(
            num_scalar_prefetch=1, grid=(n_steps,),
            in_specs=[pl.BlockSpec((tp, D), lambda i, sc: (jnp.minimum(i, n_pre - 1), 0)),
                      pl.BlockSpec((1, D), lambda i, sc: (0, 0)),
                      pl.BlockSpec((1, PROJ_TN), lambda i, sc: (0, sc[pos_of(i)])),
                      HBM_SPEC,
                      pl.BlockSpec((tp, 1), lambda i, sc: (jnp.minimum(i, n_pre - 1), 0)),
                      pl.BlockSpec((1, 128), lambda i, sc: (0, 0))],
            out_specs=[pl.BlockSpec((tp, D), lambda i, sc: (jnp.minimum(i, n_pre - 1), 0)),
                       pl.BlockSpec((tm, PROJ_TN), lambda i, sc: (row_of(i), sc[pos_of(i)])),
                       HBM_SPEC,
                       pl.BlockSpec((tp, 128), lambda i, sc: (jnp.minimum(i, n_pre - 1), 0)),
                       pl.BlockSpec((tp, 128), lambda i, sc: (jnp.minimum(i, n_pre - 1), 0))],
            scratch_shapes=[pltpu.VMEM((S, D), BF16), pltpu.VMEM((2, PROJ_TN, D), BF16),
                            pltpu.VMEM((W_IN_ROWS, D), BF16),
                            pltpu.SemaphoreType.DMA((7,)), pltpu.SemaphoreType.DMA((7,)),
                            pltpu.SemaphoreType.DMA((2,)), pltpu.SemaphoreType.DMA((2,))]),
        out_shape=[jax.ShapeDtypeStruct((S, D), BF16), jax.ShapeDtypeStruct((S, D_IN), F32),
                   jax.ShapeDtypeStruct((D_IN, D), BF16),
                   jax.ShapeDtypeStruct((S, 128), F32), jax.ShapeDtypeStruct((S, 128), F32)],
        compiler_params=_params(("arbitrary",)),
    )(sched, x, g, bias, wt_part, pos, invf)


def _rot_half(xs, first_half):
    return jnp.where(first_half, -pltpu.roll(xs, 96, 1), pltpu.roll(xs, 32, 1))


GH = NQ // 2


MASKED = -1e30


def _attn_consts():
    lane = lax.broadcasted_iota(jnp.int32, (T, 128), 1)
    row = lax.broadcasted_iota(jnp.int32, (GH * T, T), 0) & (T - 1)
    on_diag_or_below = row >= lax.broadcasted_iota(jnp.int32, (GH * T, T), 1)
    return (lane & (HD - 1)) < (HD // 2), lane < HD, on_diag_or_below


def _stack_heads(slab_fn, grp, lo64):
    blocks = []
    for jj in range(GH // 2):
        s = slab_fn(GH // 2 * grp + jj)
        blocks += [jnp.where(lo64, s, 0.0), jnp.where(lo64, 0.0, s)]
    return jnp.concatenate(blocks, axis=0)


def _unstack_heads(stacked, jj, lo64):
    return jnp.where(lo64, stacked[(2 * jj) * T:(2 * jj + 1) * T], stacked[(2 * jj + 1) * T:(2 * jj + 2) * T])


def _both_halves(kv, grp):
    lo = lax.broadcasted_iota(jnp.int32, kv.shape, 1) < HD
    swapped = pltpu.roll(kv, HD, 1)
    return jnp.where(lo, kv, swapped) if grp == 0 else jnp.where(lo, swapped, kv)


def _one_half(acc, grp):
    lo = lax.broadcasted_iota(jnp.int32, acc.shape, 1) < HD
    return jnp.where(lo if grp == 0 else jnp.logical_not(lo), acc + pltpu.roll(acc, HD, 1), 0.0)


def _layer_norm_stats(v):
    mu = _row_mean(v)
    xc = v - mu
    var = _row_mean(xc * xc)
    rs = lax.rsqrt(var + EPS)
    return xc * rs, rs


def _fold(both, own):
    return jnp.where(own, both[:, T:2 * T], both[:, 0:T])


def _unfold(p, own):
    return jnp.concatenate([jnp.where(own, 0.0, p), jnp.where(own, p, 0.0)], axis=1).astype(BF16)


def _band_softmax(q_scaled, k_both, own, has_prev, sink):
    s_both = _dot(q_scaled, k_both, NT)
    s_prev = s_both[:, 0:T]
    if has_prev is not None:
        s_prev = s_prev + jnp.where(has_prev, 0.0, MASKED)
    s = jnp.where(own, s_both[:, T:2 * T], s_prev)
    m = jnp.maximum(jnp.max(s, axis=-1, keepdims=True), sink)
    e = jnp.exp(s - m)
    es = jnp.exp(sink - m)
    inv = 1.0 / (jnp.sum(e, axis=-1, keepdims=True) + es)
    return e * inv, es * inv


BPS_FWD = 4
BPS_BWD = 4


def _mid_specs(nt, rev, bps):
    tile = (lambda i: nt - 1 - i) if rev else (lambda i: i)
    prev = lambda i: jnp.maximum(bps * tile(i) - 1, 0)
    rows = bps * T
    return tile, [
        pl.BlockSpec((rows, D_IN), lambda i: (tile(i), 0)),
        pl.BlockSpec((T, 2 * T), lambda i: (prev(i), OFF_K // (2 * T))),
    ], [
        pl.BlockSpec((rows, 128), lambda i: (tile(i), 0)),
        pl.BlockSpec((rows, 128), lambda i: (tile(i), 0)),
        pl.BlockSpec((T, 128), lambda i: (prev(i), 0)),
        pl.BlockSpec((T, 128), lambda i: (prev(i), 0)),
        pl.BlockSpec((1, DG), lambda i: (0, 0)),
        pl.BlockSpec((1, DG), lambda i: (0, 0)),
        pl.BlockSpec((NG, T, T), lambda i: (0, 0, 0)),
        pl.BlockSpec((NG, T, 1), lambda i: (0, 0, 0)),
        pl.BlockSpec((NQ * T, 1), lambda i: (0, 0)),
    ]


def _rope(xs, cosv, sinv, first_half):
    return xs * cosv + _rot_half(xs, first_half) * sinv


def _mid_fwd(proj, cos, sin, ln_g, ln_b, w_s, b_s, sinks, wo_part):
    S = proj.shape[0]
    BPS = BPS_FWD if S % (BPS_FWD * T) == 0 else 1
    nt = S // (BPS * T)
    tile, proj_specs, par_specs = _mid_specs(nt, False, BPS)

    def body(p_ref, kvp_ref, cos_ref, sin_ref, cosp_ref, sinp_ref, lng_ref, lnb_ref, ws_ref, bs_ref, sink_ref, wpart_ref,
             y_ref, wo_ref, wm_ref, send_sems, recv_sems, local_sems):
        i = pl.program_id(0)
        first_half, lo64, own = _attn_consts()
        gather = _RowGather(wpart_ref, wo_ref, W_OUT_ROWS, send_sems, recv_sems, local_sems)
        pl.when(i == 0)(gather.start)

        @pl.when(i == 0)
        def _():
            tril = lax.broadcasted_iota(jnp.int32, (T, T), 0) >= lax.broadcasted_iota(jnp.int32, (T, T), 1)
            for g in range(NG):
                wm_ref[g] = jnp.where(tril, ws_ref[g], 0.0).astype(BF16)

        def block(b, k_prev, v_prev, has_prev):
            rows = slice(b * T, (b + 1) * T)
            xhat, _ = _layer_norm_stats(p_ref[rows, OFF_V:OFF_V + DG])
            vn = xhat * lng_ref[...] + lnb_ref[...]
            for g in range(NG):
                sl = slice(128 * g, 128 * g + 128)
                mixed = _dot(wm_ref[g], vn[:, sl].astype(BF16), NN) + bs_ref[g]
                z = p_ref[rows, OFF_ZA + 128 * g:OFF_ZA + 128 * g + 128]
                u = p_ref[rows, OFF_U + 128 * g:OFF_U + 128 * g + 128]
                y_ref[rows, sl] = (u * mixed * (z * _sigmoid(z))).astype(BF16)

            cosv, sinv = cos_ref[rows, :], sin_ref[rows, :]
            k_cur = _rope(p_ref[rows, OFF_K:OFF_K + 128], cosv, sinv, first_half)
            v_cur = p_ref[rows, OFF_VA:OFF_VA + 128]

            def q_slab(j):
                return _rope(p_ref[rows, OFF_Q + 128 * j:OFF_Q + 128 * j + 128], cosv, sinv, first_half) * (HD ** -0.5)

            k_both = jnp.concatenate([k_prev, k_cur], axis=0)
            v_both = jnp.concatenate([v_prev, v_cur], axis=0)
            for grp in range(2):
                qs = _stack_heads(q_slab, grp, lo64).astype(BF16)
                kg, vg = _both_halves(k_both, grp).astype(BF16), _both_halves(v_both, grp).astype(BF16)
                p, _ = _band_softmax(qs, kg, own, has_prev, sink_ref[GH * T * grp:GH * T * (grp + 1), :])
                o_st = _dot(_unfold(p, own), vg, NN)
                for jj in range(GH // 2):
                    c0 = 128 * (GH // 2 * grp + jj)
                    zb = p_ref[rows, OFF_ZB + c0:OFF_ZB + c0 + 128]
                    o = _unstack_heads(o_st, jj, lo64)
                    y_ref[rows, DG + c0:DG + c0 + 128] = (o * (zb * _sigmoid(zb))).astype(BF16)
            return k_cur, v_cur

        k_prev = _rope(kvp_ref[:, 0:128], cosp_ref[...], sinp_ref[...], first_half)
        kv = block(0, k_prev, kvp_ref[:, 128:256], i > 0)
        for b in range(1, BPS):
            kv = block(b, *kv, None)
        pl.when(i == nt - 1)(gather.finish)

    return pl.pallas_call(
        body, name="mid_fwd", grid=(nt,),
        in_specs=proj_specs + par_specs + [HBM_SPEC],
        out_specs=[pl.BlockSpec((BPS * T, D), lambda i: (tile(i), 0)), HBM_SPEC],
        out_shape=[jax.ShapeDtypeStruct((S, D), BF16), jax.ShapeDtypeStruct((D, D), BF16)],
        scratch_shapes=[pltpu.VMEM((NG, T, T), BF16)] + GATHER_SEMS,
        compiler_params=_params(("arbitrary",)),
    )(proj, proj, cos, sin, cos, sin, ln_g, ln_b, w_s, b_s, sinks, wo_part)


def _mid_bwd(proj, dycat, cos, sin, ln_g, ln_b, w_s, b_s, sinks, po):
    S = proj.shape[0]
    BPS = BPS_BWD if S % (BPS_BWD * T) == 0 else 1
    nt = S // (BPS * T)
    tile, proj_specs, par_specs = _mid_specs(nt, True, BPS)
    const2 = lambda i: (0, 0)

    def body(p_ref, kvp_ref, dyc_ref, cos_ref, sin_ref, cosp_ref, sinp_ref, lng_ref, lnb_ref, ws_ref, bs_ref, sink_ref,
             po_ref, dp_ref, dlng_ref, dlnb_ref, dws_ref, dbs_ref, dsink_ref, dbqkv_ref, bo_ref,
             carry_ref, dvn_ref, wm_ref, wmt_ref, send_sems, recv_sems, local_sem):
        i = pl.program_id(0)
        first_half, lo64, own = _attn_consts()
        tril = lax.broadcasted_iota(jnp.int32, (T, T), 0) >= lax.broadcasted_iota(jnp.int32, (T, T), 1)
        exchange = _OwnerExchange(po_ref, bo_ref, send_sems, recv_sems, local_sem)
        pl.when(i == 0)(exchange.start)

        @pl.when(i == 0)
        def _():
            for g in range(NG):
                wm = jnp.where(tril, ws_ref[g], 0.0)
                wm_ref[g] = wm.astype(BF16)
                wmt_ref[g] = wm.T.astype(BF16)
            dlng_ref[...] = jnp.zeros_like(dlng_ref)
            dlnb_ref[...] = jnp.zeros_like(dlnb_ref)
            dws_ref[...] = jnp.zeros_like(dws_ref)
            dbs_ref[...] = jnp.zeros_like(dbs_ref)
            dsink_ref[...] = jnp.zeros_like(dsink_ref)
            dbqkv_ref[...] = jnp.zeros_like(dbqkv_ref)
            carry_ref[...] = jnp.zeros_like(carry_ref)

        def roped_k(b):
            rows = slice(b * T, (b + 1) * T)
            return _rope(p_ref[rows, OFF_K:OFF_K + 128], cos_ref[rows, :], sin_ref[rows, :], first_half)

        def block(b, k_prev, v_prev, has_prev, dk_next, dv_next):
            rows = slice(b * T, (b + 1) * T)
            xhat, rs = _layer_norm_stats(p_ref[rows, OFF_V:OFF_V + DG])
            lng = lng_ref[...]
            vn = xhat * lng + lnb_ref[...]
            for g in range(NG):
                sl = slice(128 * g, 128 * g + 128)
                vng = vn[:, sl].astype(BF16)
                mixed = _dot(wm_ref[g], vng, NN) + bs_ref[g]
                z = p_ref[rows, OFF_ZA + 128 * g:OFF_ZA + 128 * g + 128]
                u = p_ref[rows, OFF_U + 128 * g:OFF_U + 128 * g + 128]
                dy = dyc_ref[rows, sl]
                sg = _sigmoid(z)
                sa = z * sg
                dp_ref[rows, OFF_U + 128 * g:OFF_U + 128 * g + 128] = (dy * mixed * sa).astype(BF16)
                dp_ref[rows, OFF_ZA + 128 * g:OFF_ZA + 128 * g + 128] = (
                    dy * u * mixed * (sg * (1.0 + z * (1.0 - sg)))).astype(BF16)
                dm = dy * u * sa
                dmb = dm.astype(BF16)
                dvn_ref[rows, sl] = _dot(wmt_ref[g], dmb, NN)
                dws_ref[g] += jnp.where(tril, _dot(dmb, vng, NT), 0.0)
                dbs_ref[g] += jnp.sum(dm, axis=1, keepdims=True)
            dvn = dvn_ref[rows, :]
            dlng_ref[...] += jnp.sum(dvn * xhat, axis=0, keepdims=True)
            dlnb_ref[...] += jnp.sum(dvn, axis=0, keepdims=True)
            dxh = dvn * lng
            dv_g = rs * (dxh - _row_mean(dxh)
                         - xhat * _row_mean(dxh * xhat))
            dp_ref[rows, OFF_V:OFF_V + DG] = dv_g.astype(BF16)

            cosv, sinv = cos_ref[rows, :], sin_ref[rows, :]
            k_cur = roped_k(b)
            v_cur = p_ref[rows, OFF_VA:OFF_VA + 128]

            def q_slab(j):
                return _rope(p_ref[rows, OFF_Q + 128 * j:OFF_Q + 128 * j + 128], cosv, sinv, first_half) * (HD ** -0.5)

            def do_slab(j):
                zb = p_ref[rows, OFF_ZB + 128 * j:OFF_ZB + 128 * j + 128]
                return dyc_ref[rows, DG + 128 * j:DG + 128 * j + 128] * (zb * _sigmoid(zb))

            k_both = jnp.concatenate([k_prev, k_cur], axis=0)
            v_both = jnp.concatenate([v_prev, v_cur], axis=0)
            dk_both, dv_both = jnp.zeros((2 * T, 128), F32), jnp.zeros((2 * T, 128), F32)
            for grp in range(2):
                qs = _stack_heads(q_slab, grp, lo64).astype(BF16)
                d_o = _stack_heads(do_slab, grp, lo64)
                dob = d_o.astype(BF16)
                kg, vg = _both_halves(k_both, grp).astype(BF16), _both_halves(v_both, grp).astype(BF16)
                p, ps = _band_softmax(qs, kg, own, has_prev, sink_ref[GH * T * grp:GH * T * (grp + 1), :])
                p_both = _unfold(p, own)
                o_st = _dot(p_both, vg, NN)
                delta = jnp.sum(d_o * o_st, axis=-1, keepdims=True)
                ds_both = _unfold(p * (_fold(_dot(dob, vg, NT), own) - delta), own)
                dq_st = _dot(ds_both, kg, NN) * (HD ** -0.5)
                dk_both = dk_both + _one_half(_dot(ds_both, qs, TN), grp)
                dv_both = dv_both + _one_half(_dot(p_both, dob, TN), grp)
                dsink_rows = ps * delta
                for hh in range(GH):
                    h = GH * grp + hh
                    dsink_ref[h:h + 1, :] += jnp.broadcast_to(
                        -jnp.sum(dsink_rows[hh * T:(hh + 1) * T], axis=0, keepdims=True), (1, 128))
                for jj in range(GH // 2):
                    c0 = 128 * (GH // 2 * grp + jj)
                    zb = p_ref[rows, OFF_ZB + c0:OFF_ZB + c0 + 128]
                    sg = _sigmoid(zb)
                    o = _unstack_heads(o_st, jj, lo64)
                    dp_ref[rows, OFF_ZB + c0:OFF_ZB + c0 + 128] = (
                        dyc_ref[rows, DG + c0:DG + c0 + 128] * o * (sg * (1.0 + zb * (1.0 - sg)))).astype(BF16)
                    dq = _unstack_heads(dq_st, jj, lo64)
                    dq_pre = dq * cosv - _rot_half(dq, first_half) * sinv
                    dp_ref[rows, OFF_Q + c0:OFF_Q + c0 + 128] = dq_pre.astype(BF16)
                    dbqkv_ref[:, c0:c0 + 128] += jnp.sum(dq_pre, axis=0, keepdims=True)
            dk_prev, dv_prev = dk_both[0:T], dv_both[0:T]
            dk_cur, dv_cur = dk_both[T:2 * T] + dk_next, dv_both[T:2 * T] + dv_next
            dk_pre = dk_cur * cosv - _rot_half(dk_cur, first_half) * sinv
            dp_ref[rows, OFF_K:OFF_K + 128] = dk_pre.astype(BF16)
            dp_ref[rows, OFF_VA:OFF_VA + 128] = dv_cur.astype(BF16)
            dbqkv_ref[:, 1024:1152] += jnp.sum(dk_pre, axis=0, keepdims=True)
            dbqkv_ref[:, 1152:1280] += jnp.sum(dv_cur, axis=0, keepdims=True)
            return dk_prev, dv_prev

        grads = carry_ref[:, 0:128], carry_ref[:, 128:256]
        for b in range(BPS - 1, 0, -1):
            prows = slice((b - 1) * T, b * T)
            grads = block(b, roped_k(b - 1), p_ref[prows, OFF_VA:OFF_VA + 128], None, *grads)
        k_prev = _rope(kvp_ref[:, 0:128], cosp_ref[...], sinp_ref[...], first_half)
        grads = block(0, k_prev, kvp_ref[:, 128:256], i < nt - 1, *grads)
        carry_ref[:, 0:128], carry_ref[:, 128:256] = grads
        pl.when(i == nt - 1)(exchange.finish)

    return pl.pallas_call(
        body, name="mid_bwd", grid=(nt,),
        in_specs=proj_specs + [pl.BlockSpec((BPS * T, D), lambda i: (tile(i), 0))] + par_specs + [HBM_SPEC],
        out_specs=[pl.BlockSpec((BPS * T, D_IN), lambda i: (tile(i), 0)),
                   pl.BlockSpec((1, DG), const2), pl.BlockSpec((1, DG), const2),
                   pl.BlockSpec((NG, T, T), lambda i: (0, 0, 0)), pl.BlockSpec((NG, T, 1), lambda i: (0, 0, 0)),
                   pl.BlockSpec((NQ, 128), const2), pl.BlockSpec((1, D_QKV), const2), HBM_SPEC],
        out_shape=[jax.ShapeDtypeStruct((S, D_IN), BF16),
                   jax.ShapeDtypeStruct((1, DG), F32), jax.ShapeDtypeStruct((1, DG), F32),
                   jax.ShapeDtypeStruct((NG, T, T), F32), jax.ShapeDtypeStruct((NG, T, 1), F32),
                   jax.ShapeDtypeStruct((NQ, 128), F32), jax.ShapeDtypeStruct((1, D_QKV), F32),
                   jax.ShapeDtypeStruct(po.shape, BF16)],
        scratch_shapes=[pltpu.VMEM((T, 2 * T), F32), pltpu.VMEM((BPS * T, DG), F32),
                        pltpu.VMEM((NG, T, T), BF16), pltpu.VMEM((NG, T, T), BF16)] + OWNER_SEMS,
        compiler_params=_params(("arbitrary",)),
    )(proj, proj, dycat, cos, sin, cos, sin, ln_g, ln_b, w_s, b_s, sinks, po)


def _outproj_loss(ycat, wo, x, target, g_post):
    S = ycat.shape[0]
    tm = _tile(S, 512)
    nt = S // tm
    n_part = 2 if tm % 32 == 0 else 1
    tp = tm // n_part
    const2 = lambda i: (0, 0)

    def body(yc_ref, w_ref, x_ref, t_ref, g_ref, dy_ref, dout_ref, loss_ref, dg_ref, lacc_ref):
        i = pl.program_id(0)

        @pl.when(i == 0)
        def _():
            dg_ref[...] = jnp.zeros_like(dg_ref)
            lacc_ref[...] = jnp.zeros_like(lacc_ref)

        g = g_ref[...]
        ys = [_dot(yc_ref[q * tp:(q + 1) * tp, :], w_ref[...], NN) for q in range(n_part)]
        for q, y in enumerate(ys):
            rows = slice(q * tp, (q + 1) * tp)
            r = lax.rsqrt(_row_mean(y * y) + EPS)
            yh = y * r
            diff = x_ref[rows, :] + yh * g - t_ref[rows, :]
            lacc_ref[...] += jnp.sum(diff * diff, axis=0, keepdims=True)
            dout = diff * (1.0 / D)
            dout_ref[rows, :] = dout
            dg_ref[...] += jnp.sum(dout * yh, axis=0, keepdims=True)
            dyh = dout * g
            dy_ref[rows, :] = (r * (dyh - yh * _row_mean(dyh * yh))).astype(BF16)

        @pl.when(i == nt - 1)
        def _():
            loss_ref[...] = jnp.broadcast_to(jnp.sum(lacc_ref[...], axis=1, keepdims=True) * (0.5 / D), (1, 128))

    row = lambda i: (i, 0)
    return pl.pallas_call(
        body, name="outproj_loss", grid=(nt,),
        in_specs=[pl.BlockSpec((tm, D), row), pl.BlockSpec((D, D), const2, pipeline_mode=pl.Buffered(1)),
                  pl.BlockSpec((tm, D), row), pl.BlockSpec((tm, D), row), pl.BlockSpec((1, D), const2)],
        out_specs=[pl.BlockSpec((tm, D), row), pl.BlockSpec((tm, D), row), pl.BlockSpec((1, 128), const2),
                   pl.BlockSpec((1, D), const2)],
        out_shape=[jax.ShapeDtypeStruct((S, D), BF16), jax.ShapeDtypeStruct((S, D), F32),
                   jax.ShapeDtypeStruct((1, 128), F32), jax.ShapeDtypeStruct((1, D), F32)],
        scratch_shapes=[pltpu.VMEM((1, D), F32)],
        compiler_params=_params(("arbitrary",)),
    )(ycat, wo, x, target, g_post)


def _dycat(dy, wo, dwo):
    S = dy.shape[0]
    tm = _tile(S, 512)
    nt = S // tm

    def body(dy_ref, w_ref, dwo_ref, o_ref, r_ref, send_sems, recv_sems):
        i = pl.program_id(0)
        copies = _sibling_copies(dwo_ref, r_ref, W_OUT_ROWS, send_sems, recv_sems)

        @pl.when(i == 0)
        def _():
            for cp in copies:
                cp.start()

        o_ref[...] = _dot(dy_ref[...], w_ref[...], NT)

        @pl.when(i == nt - 1)
        def _():
            for cp in copies:
                cp.wait_recv()
            for cp in copies:
                cp.wait_send()

    return pl.pallas_call(
        body, name="dycat", grid=(nt,),
        in_specs=[pl.BlockSpec((tm, D), lambda i: (i, 0)),
                  pl.BlockSpec((D, D), lambda i: (0, 0), pipeline_mode=pl.Buffered(1)), HBM_SPEC],
        out_specs=[pl.BlockSpec((tm, D), lambda i: (i, 0)), HBM_SPEC],
        out_shape=[jax.ShapeDtypeStruct((S, D), F32), jax.ShapeDtypeStruct((N_CHIPS, W_OUT_ROWS, D), BF16)],
        scratch_shapes=[pltpu.SemaphoreType.DMA((N_CHIPS,)), pltpu.SemaphoreType.DMA((N_CHIPS,))],
        compiler_params=_params(("arbitrary",)),
    )(dy, wo, dwo)


def _matmul_tn(a, b, tm, name, mat_grads=None):
    K, M = a.shape
    N = b.shape[1]
    tk = _tile(K, 1024)
    ni, nk = M // tm, K // tk
    hosting = mat_grads is not None

    def body(a_ref, b_ref, *rest):
        if hosting:
            g_in, (o_ref, tot_ref, acc_ref, mbuf, send_sems, recv_sems) = rest[:len(MAT)], rest[len(MAT):]
        else:
            o_ref, acc_ref = rest
        i, k = pl.program_id(0), pl.program_id(1)

        if hosting:
            gather = _SlotGather(mbuf, send_sems, recv_sems)

            @pl.when((i == 0) & (k == 0))
            def _():
                r0 = 0
                for q, (_, rows) in enumerate(MAT):
                    mbuf[gather.my_id, r0:r0 + rows, :] = g_in[q][...]
                    r0 += rows
                gather.start()

        @pl.when(k == 0)
        def _():
            acc_ref[...] = jnp.zeros_like(acc_ref)

        acc_ref[...] += _dot(a_ref[...], b_ref[...], TN)

        @pl.when(k == nk - 1)
        def _():
            o_ref[...] = acc_ref[...].astype(BF16)

        if hosting:
            @pl.when((i == ni - 1) & (k == nk - 1))
            def _():
                gather.finish()
                tot_ref[...] = gather.total()

    whole = pl.BlockSpec(memory_space=pltpu.VMEM)
    in_specs = [pl.BlockSpec((tk, tm), lambda i, k: (k, i)), pl.BlockSpec((tk, N), lambda i, k: (k, 0))]
    out_specs = [pl.BlockSpec((tm, N), lambda i, k: (i, 0))]
    out_shape = [jax.ShapeDtypeStruct((M, N), BF16)]
    scratch = [pltpu.VMEM((tm, N), F32)]
    if hosting:
        in_specs += [whole] * len(MAT)
        out_specs.append(whole)
        out_shape.append(jax.ShapeDtypeStruct((MAT_ROWS, 128), F32))
        scratch += [pltpu.VMEM((N_DEV, MAT_ROWS, 128), F32), pltpu.SemaphoreType.DMA((7,)),
                    pltpu.SemaphoreType.DMA((7,))]
    res = pl.pallas_call(
        body, name=name, grid=(ni, nk), in_specs=in_specs, out_specs=out_specs, out_shape=out_shape,
        scratch_shapes=scratch,
        compiler_params=_params(("arbitrary", "arbitrary") if hosting else ("parallel", "arbitrary")),
    )(a, b, *(mat_grads or ()))
    return res if hosting else res[0]


def _dh_prenorm_bwd(dproj, wt, x, dout, g_pre, pt):
    S = x.shape[0]
    tm, tk = _tile(S, 1024), 768
    te = tm // 2
    ni, nk, ne = S // tm, D_IN // tk, tm // te
    per = nk + ne
    tile_of, phase_of = (lambda t: t // per), (lambda t: t % per)
    k_of = lambda t: jnp.minimum(phase_of(t), nk - 1)
    half_of = lambda t: tile_of(t) * ne + jnp.maximum(phase_of(t) - nk, 0)

    def body(dp_ref, w_ref, x_ref, dout_ref, g_ref, pt_ref, gx_ref, dg_ref, bt_ref, acc_ref,
             send_sems, recv_sems, local_sem):
        t = pl.program_id(0)
        ph = phase_of(t)
        exchange = _OwnerExchange(pt_ref, bt_ref, send_sems, recv_sems, local_sem)
        pl.when(t == 0)(exchange.start)

        @pl.when(t == 0)
        def _():
            dg_ref[...] = jnp.zeros_like(dg_ref)

        @pl.when(ph == 0)
        def _():
            acc_ref[...] = _dot(dp_ref[...], w_ref[...], NN)

        @pl.when((ph > 0) & (ph < nk))
        def _():
            acc_ref[...] += _dot(dp_ref[...], w_ref[...], NN)

        @pl.when(ph >= nk)
        def _():
            dh = acc_ref[pl.ds(pl.multiple_of((ph - nk) * te, te), te), :]
            xv = x_ref[...]
            r = lax.rsqrt(_row_mean(xv * xv) + EPS)
            xh = xv * r
            dg_ref[...] += jnp.sum(dh * xh, axis=0, keepdims=True)
            dxh = dh * g_ref[...]
            gx_ref[...] = dout_ref[...] + r * (dxh - xh * _row_mean(dxh * xh))

        pl.when(t == ni * per - 1)(exchange.finish)

    half = lambda t: (half_of(t), 0)
    return pl.pallas_call(
        body, name="dh_prenorm_bwd", grid=(ni * per,),
        in_specs=[pl.BlockSpec((tm, tk), lambda t: (tile_of(t), k_of(t))), pl.BlockSpec((tk, D), lambda t: (k_of(t), 0)),
                  pl.BlockSpec((te, D), half), pl.BlockSpec((te, D), half), pl.BlockSpec((1, D), lambda t: (0, 0)),
                  HBM_SPEC],
        out_specs=[pl.BlockSpec((te, D), half), pl.BlockSpec((1, D), lambda t: (0, 0)), HBM_SPEC],
        out_shape=[jax.ShapeDtypeStruct((S, D), F32), jax.ShapeDtypeStruct((1, D), F32),
                   jax.ShapeDtypeStruct(pt.shape, BF16)],
        scratch_shapes=[pltpu.VMEM((tm, D), F32)] + OWNER_SEMS,
        compiler_params=_params(("arbitrary",)),
    )(dproj, wt, x, dout, g_pre, pt)


def _presum(c_arr, own, recv, half_rows):
    n_cols = own.shape[-1]
    own4 = own.reshape(N_CHIPS, 2, half_rows, n_cols)

    def body(c_ref, own_ref, recv_ref, o_ref):
        o_ref[...] = (own_ref[...].astype(F32) + recv_ref[...].astype(F32)).astype(BF16)

    return pl.pallas_call(
        body, name="presum_%d" % half_rows,
        grid_spec=pltpu.PrefetchScalarGridSpec(
            num_scalar_prefetch=1, grid=(N_CHIPS,),
            in_specs=[pl.BlockSpec((None, None, half_rows, n_cols), lambda j, c: (j, c[0], 0, 0)),
                      pl.BlockSpec((None, half_rows, n_cols), lambda j, c: (j, 0, 0))],
            out_specs=pl.BlockSpec((None, half_rows, n_cols), lambda j, c: (j, 0, 0))),
        out_shape=jax.ShapeDtypeStruct((N_CHIPS, half_rows, n_cols), BF16),
        compiler_params=_params(("parallel",)),
    )(c_arr, own4, recv)


def _sibling_presum(c_arr, dw, half_rows, name):
    n_cols = dw.shape[-1]
    dw4 = dw.reshape(N_CHIPS, 2, half_rows, n_cols)

    def body(c_ref, own_ref, dw_ref, o_ref, r_ref, stage, send_sems, recv_sems, load_sem):
        j = pl.program_id(0)
        copies = _sibling_copies(dw_ref, r_ref, half_rows, send_sems, recv_sems)

        @pl.when(j == 0)
        def _():
            for cp in copies:
                cp.start()

        for q in range(N_CHIPS):
            pl.when(j == q)(copies[q].wait_recv)
        load = pltpu.make_async_copy(r_ref.at[j], stage, load_sem)
        load.start()
        load.wait()
        o_ref[...] = (own_ref[...].astype(F32) + stage[...].astype(F32)).astype(BF16)

        @pl.when(j == N_CHIPS - 1)
        def _():
            for cp in copies:
                cp.wait_send()

    out, _ = pl.pallas_call(
        body, name=name,
        grid_spec=pltpu.PrefetchScalarGridSpec(
            num_scalar_prefetch=1, grid=(N_CHIPS,),
            in_specs=[pl.BlockSpec((None, None, half_rows, n_cols), lambda j, c: (j, c[0], 0, 0)), HBM_SPEC],
            out_specs=[pl.BlockSpec((None, half_rows, n_cols), lambda j, c: (j, 0, 0)), HBM_SPEC],
            scratch_shapes=[pltpu.VMEM((half_rows, n_cols), BF16), pltpu.SemaphoreType.DMA((N_CHIPS,)),
                            pltpu.SemaphoreType.DMA((N_CHIPS,)), pltpu.SemaphoreType.DMA(())]),
        out_shape=[jax.ShapeDtypeStruct((N_CHIPS, half_rows, n_cols), BF16),
                   jax.ShapeDtypeStruct((N_CHIPS, half_rows, n_cols), BF16)],
        compiler_params=_params(("arbitrary",)),
    )(c_arr, dw4, dw)
    return out


def _sum_chips(c_arr, parts, name):
    _, rows, n_cols = parts.shape
    nt = 2
    tr = rows // nt

    def body(c_ref, p_ref, o_ref):
        o_ref[...] = ((p_ref[0].astype(F32) + p_ref[1].astype(F32)) + p_ref[2].astype(F32)) + p_ref[3].astype(F32)

    return pl.pallas_call(
        body, name=name,
        grid_spec=pltpu.PrefetchScalarGridSpec(
            num_scalar_prefetch=1, grid=(nt,),
            in_specs=[pl.BlockSpec((N_CHIPS, tr, n_cols), lambda i, c: (0, i, 0))],
            out_specs=pl.BlockSpec((tr, n_cols), lambda i, c: (c[0] * nt + i, 0))),
        out_shape=jax.ShapeDtypeStruct((2 * rows, n_cols), F32),
        compiler_params=_params(("parallel",)),
    )(c_arr, parts)


def _adamw_math(w, g, m, v):
    mn = ADAM_B1 * m + (1.0 - ADAM_B1) * g
    vn = ADAM_B2 * v + (1.0 - ADAM_B2) * (g * g)
    m_hat = mn / (1.0 - ADAM_B1 ** ADAM_STEP)
    v_hat = vn / (1.0 - ADAM_B2 ** ADAM_STEP)
    return -ADAM_LR * (m_hat / (jnp.sqrt(v_hat) + ADAM_EPS) + ADAM_WD * w), mn, vn


def _adamw(w, g, m, v, name):
    R, C = w.shape
    tr = next((t for t in (256, 192, 128) if R % t == 0), R)

    def body(w_ref, g_ref, m_ref, v_ref, go_ref, d_ref, mo_ref, vo_ref):
        gv = g_ref[...]
        go_ref[...] = gv
        d_ref[...], mo_ref[...], vo_ref[...] = _adamw_math(w_ref[...], gv, m_ref[...], v_ref[...])

    spec = pl.BlockSpec((tr, C), lambda i: (i, 0))
    shp = jax.ShapeDtypeStruct((R, C), F32)
    return pl.pallas_call(
        body, name=name, grid=(R // tr,), in_specs=[spec] * 4, out_specs=[spec] * 4, out_shape=[shp] * 4,
        compiler_params=_params(("parallel",)),
    )(w, g, m, v)


HBM_SPEC = pl.BlockSpec(memory_space=pltpu.HBM)
GATHER_LOCAL_CHUNKS = 4
GATHER_SEMS = [pltpu.SemaphoreType.DMA((7,)), pltpu.SemaphoreType.DMA((7,)),
               pltpu.SemaphoreType.DMA((GATHER_LOCAL_CHUNKS,))]
OWNER_SEMS = [pltpu.SemaphoreType.DMA((3,)), pltpu.SemaphoreType.DMA((3,)), pltpu.SemaphoreType.DMA(())]


def _mesh_pos():
    return lax.axis_index("x"), lax.axis_index("y"), lax.axis_index("c")


class _RowGather:
    def __init__(self, src_ref, full_ref, rows, send_sems, recv_sems, local_sems):
        self.src, self.full, self.rows = src_ref, full_ref, rows
        self.send, self.recv, self.local = send_sems, recv_sems, local_sems
        x, y, c = _mesh_pos()
        self.c, self.me, self.sibling = c, (x, y, c), (x, y, 1 - c)
        self.chips = [(1 - x, y), (x, 1 - y), (1 - x, 1 - y)]

    def _block(self, pos):
        px, py, pc = pos
        return self.full.at[pl.ds(pl.multiple_of((4 * px + 2 * py + pc) * self.rows, 16), self.rows), :]

    def _copy(self, k, blk, to):
        return pltpu.make_async_remote_copy(
            src_ref=self.src if blk is self.me else self._block(blk), dst_ref=self._block(blk),
            send_sem=self.send.at[k], recv_sem=self.recv.at[k], device_id=to, device_id_type=MESH)

    def _mine(self):
        return _place_locally(self.src, self._block(self.me), self.local, GATHER_LOCAL_CHUNKS)

    def _first(self):
        return [self._copy(0, self.me, self.sibling)] + [
            self._copy(1 + j, self.me, (*chip, self.c)) for j, chip in enumerate(self.chips)]

    def start(self):
        for cp in self._first() + self._mine():
            cp.start()

    def finish(self):
        passed = [self._copy(4 + j, (*chip, self.c), self.sibling) for j, chip in enumerate(self.chips)]
        for j, chip in enumerate(self.chips):
            self._copy(1 + j, (*chip, self.c), self.me).wait_recv()
            passed[j].start()
        self._copy(0, self.sibling, self.me).wait_recv()
        for j, chip in enumerate(self.chips):
            self._copy(4 + j, (*chip, 1 - self.c), self.me).wait_recv()
        for cp in self._first() + passed:
            cp.wait_send()
        for cp in self._mine():
            cp.wait()


class _OwnerExchange:
    def __init__(self, src_ref, dst_ref, send_sems, recv_sems, local_sem):
        self.src, self.dst, self.send, self.recv, self.local = src_ref, dst_ref, send_sems, recv_sems, local_sem
        x, y, c = _mesh_pos()
        self.c, self.my_chip = c, 2 * x + y
        self.peers = [(1 - x, y), (x, 1 - y), (1 - x, 1 - y)]

    def _copies(self):
        local = pltpu.make_async_copy(self.src.at[self.my_chip], self.dst.at[self.my_chip], self.local)
        remote = [pltpu.make_async_remote_copy(
            src_ref=self.src.at[2 * px + py], dst_ref=self.dst.at[self.my_chip],
            send_sem=self.send.at[k], recv_sem=self.recv.at[k], device_id=(px, py, self.c), device_id_type=MESH)
            for k, (px, py) in enumerate(self.peers)]
        return local, remote

    def start(self):
        local, remote = self._copies()
        local.start()
        for cp in remote:
            cp.start()

    def finish(self):
        local, remote = self._copies()
        for cp in remote:
            cp.wait_recv()
        for cp in remote:
            cp.wait_send()
        local.wait()


def _sibling_copies(dw_ref, r_ref, rows, send_sems, recv_sems):
    x, y, c = _mesh_pos()
    return [pltpu.make_async_remote_copy(
        src_ref=dw_ref.at[pl.ds(pl.multiple_of((2 * j + (1 - c)) * rows, 16), rows), :], dst_ref=r_ref.at[j],
        send_sem=send_sems.at[j], recv_sem=recv_sems.at[j], device_id=(x, y, 1 - c), device_id_type=MESH)
        for j in range(N_CHIPS)]


PAIR_CHUNKS = 4


def _pair_halves(gt, go):
    def body(gt_in, go_in, gt_ref, go_ref, send_sems, recv_sems):
        del gt_in, go_in
        x, y, c = _mesh_pos()
        copies = []
        for a, (ref, rows) in enumerate(((gt_ref, W_IN_ROWS), (go_ref, W_OUT_ROWS))):
            ch = rows // PAIR_CHUNKS
            for q in range(PAIR_CHUNKS):
                part = ref.at[pl.ds(pl.multiple_of(c * rows + q * ch, 8), ch), :]
                copies.append(pltpu.make_async_remote_copy(
                    src_ref=part, dst_ref=part, send_sem=send_sems.at[PAIR_CHUNKS * a + q],
                    recv_sem=recv_sems.at[PAIR_CHUNKS * a + q], device_id=(x, y, 1 - c), device_id_type=MESH))
        for cp in copies:
            cp.start()
        for cp in copies:
            cp.wait_recv()
        for cp in copies:
            cp.wait_send()

    return pl.pallas_call(
        body, name="pair_halves",
        in_specs=[HBM_SPEC, HBM_SPEC], out_specs=[HBM_SPEC, HBM_SPEC],
        out_shape=[jax.ShapeDtypeStruct(gt.shape, F32), jax.ShapeDtypeStruct(go.shape, F32)],
        input_output_aliases={0: 0, 1: 1},
        scratch_shapes=[pltpu.SemaphoreType.DMA((2 * PAIR_CHUNKS,)), pltpu.SemaphoreType.DMA((2 * PAIR_CHUNKS,))],
    )(gt, go)


VEC = (("g_pre", 2048), ("g_post", 2048), ("b_qkv", 1280), ("ln_v_g", 1024), ("ln_v_b", 1024), ("attn_sinks", 16))
VEC_ROWS = 8
LOSS_ROW = len(VEC)
MAT = (("w_spatial", NG * T), ("b_spatial", NG))
MAT_ROWS = sum(r for _, r in MAT)


class _SlotGather:
    def __init__(self, buf, send_sems, recv_sems):
        self.buf, self.send, self.recv = buf, send_sems, recv_sems
        x, y, c = _mesh_pos()
        self.c, self.me, self.sibling, self.my_id = c, (x, y, c), (x, y, 1 - c), 4 * x + 2 * y + c
        self.chips = [(1 - x, y), (x, 1 - y), (1 - x, 1 - y)]

    def _copy(self, k, blk, to):
        px, py, pc = blk
        slot = self.buf.at[4 * px + 2 * py + pc]
        return pltpu.make_async_remote_copy(
            src_ref=slot, dst_ref=slot, send_sem=self.send.at[k], recv_sem=self.recv.at[k],
            device_id=to, device_id_type=MESH)

    def _first(self):
        return [self._copy(0, self.me, self.sibling)] + [
            self._copy(1 + j, self.me, (*chip, self.c)) for j, chip in enumerate(self.chips)]

    def start(self):
        for cp in self._first():
            cp.start()

    def finish(self):
        passed = [self._copy(4 + j, (*chip, self.c), self.sibling) for j, chip in enumerate(self.chips)]
        for j, chip in enumerate(self.chips):
            self._copy(1 + j, (*chip, self.c), self.me).wait_recv()
            passed[j].start()
        self._copy(0, self.sibling, self.me).wait_recv()
        for j, chip in enumerate(self.chips):
            self._copy(4 + j, (*chip, 1 - self.c), self.me).wait_recv()
        for cp in self._first() + passed:
            cp.wait_send()

    def total(self):
        t = self.buf[0]
        for d in range(1, N_DEV):
            t = t + self.buf[d]
        return t


def _small_update(vec_grads, loss_part, mat_total, vec_state, mat_state):
    n_vec, n_mat = len(VEC), len(MAT)
    n_par = n_vec + n_mat
    n_in = n_vec + 2 + 3 * n_par

    def body(*refs):
        g_in, loss_in, tot_m = refs[:n_vec], refs[n_vec], refs[n_vec + 1]
        st_in = refs[n_vec + 2:n_in]
        outs, loss_out = refs[n_in:n_in + 4 * n_par], refs[n_in + 4 * n_par]
        vbuf, tot_v, send_sems, recv_sems = refs[n_in + 4 * n_par + 1:]
        gather = _SlotGather(vbuf, send_sems, recv_sems)
        vbuf[gather.my_id] = jnp.zeros((VEC_ROWS, D), F32)
        for r, (_, n) in enumerate(VEC):
            vbuf[gather.my_id, r:r + 1, 0:n] = g_in[r][...]
        vbuf[gather.my_id, LOSS_ROW:LOSS_ROW + 1, 0:128] = loss_in[...]
        gather.start()
        gather.finish()
        tot_v[...] = gather.total()
        loss_out[...] = tot_v[LOSS_ROW:LOSS_ROW + 1, 0:128]
        r0 = 0
        for q in range(n_par):
            if q < n_vec:
                g = tot_v[q:q + 1, 0:VEC[q][1]]
            else:
                rows = MAT[q - n_vec][1]
                g = tot_m[r0:r0 + rows, :]
                r0 += rows
            w, m, v = (st_in[3 * q + t][...] for t in range(3))
            outs[4 * q][...] = g
            outs[4 * q + 1][...], outs[4 * q + 2][...], outs[4 * q + 3][...] = _adamw_math(w, g, m, v)

    state = [a for wmv in list(vec_state) + list(mat_state) for a in wmv]
    vmem = pl.BlockSpec(memory_space=pltpu.VMEM)
    shapes = [g.shape for g in vec_grads] + [wmv[0].shape for wmv in mat_state]
    out_shape = [jax.ShapeDtypeStruct(s, F32) for s in shapes for _ in range(4)]
    out_shape.append(jax.ShapeDtypeStruct((1, 128), F32))
    res = pl.pallas_call(
        body, name="small_update",
        in_specs=[vmem] * n_in, out_specs=[vmem] * len(out_shape), out_shape=out_shape,
        scratch_shapes=[pltpu.VMEM((N_DEV, VEC_ROWS, D), F32), pltpu.VMEM((VEC_ROWS, D), F32),
                        pltpu.SemaphoreType.DMA((7,)), pltpu.SemaphoreType.DMA((7,))],
        compiler_params=pltpu.CompilerParams(vmem_limit_bytes=VMEM_LIMIT),
    )(*vec_grads, loss_part, mat_total, *state)
    return [res[4 * q:4 * q + 4] for q in range(n_par)], res[-1]


def kernel(x, positions, g_pre, w_in, b_qkv, ln_v_g, ln_v_b, w_spatial, b_spatial, attn_sinks, w_out, g_post, loss_target, m_g_pre, m_w_in, m_b_qkv, m_ln_v_g, m_ln_v_b, m_w_spatial, m_b_spatial, m_attn_sinks, m_w_out, m_g_post, v_g_pre, v_w_in, v_b_qkv, v_ln_v_g, v_ln_v_b, v_w_spatial, v_b_spatial, v_attn_sinks, v_w_out, v_g_post):
    S = x.shape[1]
    c = lax.axis_index("c")
    c_arr = jnp.reshape(c, (1,)).astype(jnp.int32)
    x2 = x[0]
    target = loss_target[0]
    pos = positions.reshape(S, 1)
    half = HD // 2
    inv_freq = ROPE_THETA ** (-jnp.arange(half, dtype=F32) * (2.0 / HD))
    invf = jnp.tile(inv_freq, 128 // half).reshape(1, 128)
    bias = jnp.concatenate([jnp.zeros((OFF_Q,), F32), b_qkv[0], jnp.zeros((D_IN - OFF_ZB,), F32)]).reshape(1, D_IN)
    b_s_col = b_spatial[0].reshape(NG, T, 1)
    sinks = jnp.repeat(attn_sinks[0], T).reshape(NQ * T, 1)

    chip = 2 * lax.axis_index("x") + lax.axis_index("y")
    wt_part = lax.dynamic_slice_in_dim(w_in[0].T.astype(BF16), c * W_IN_ROWS, W_IN_ROWS, axis=0)
    wo_part = lax.dynamic_slice_in_dim(w_out[0].astype(BF16), c * W_OUT_ROWS, W_OUT_ROWS, axis=0)
    sched = jnp.asarray(PROJ_SCHEDULE, jnp.int32)[chip]

    h, proj, wt, cos, sin = _prenorm_inproj(sched, x2, g_pre, bias, wt_part, pos, invf)
    ycat, wo = _mid_fwd(proj, cos, sin, ln_v_g, ln_v_b, w_spatial[0], b_s_col, sinks, wo_part)
    dy, dout, loss_part, dg_post = _outproj_loss(ycat, wo, x2, target, g_post)

    dwo = _matmul_tn(ycat, dy, 512, "dw_out")
    dycat, ro = _dycat(dy, wo, dwo)
    po = _presum(c_arr, dwo, ro, W_OUT_ROWS)
    dproj, dln_g, dln_b, dws, dbs, dsink, dbqkv, bo = _mid_bwd(
        proj, dycat, cos, sin, ln_v_g, ln_v_b, w_spatial[0], b_s_col, sinks, po)
    dwt, mat_total = _matmul_tn(dproj, h, 768, "dw_in_t",
                                mat_grads=[dws.reshape(NG * T, T), dbs.reshape(NG, T)])
    pt = _sibling_presum(c_arr, dwt, W_IN_ROWS, "sibling_presum_in")
    grad_x, dg_pre, bt = _dh_prenorm_bwd(dproj, wt, x2, dout, g_pre, pt)
    gt, go = _pair_halves(_sum_chips(c_arr, bt, "sum_chips_in"), _sum_chips(c_arr, bo, "sum_chips_out"))

    g_w_in, d_w_in, nm_w_in, nv_w_in = (a.T for a in _adamw(w_in[0].T, gt, m_w_in[0].T, v_w_in[0].T, "adamw_w_in"))
    g_w_out, d_w_out, nm_w_out, nv_w_out = _adamw(w_out[0], go, m_w_out[0], v_w_out[0], "adamw_w_out")

    state = {"g_pre": (g_pre, m_g_pre, v_g_pre), "g_post": (g_post, m_g_post, v_g_post),
             "b_qkv": (b_qkv, m_b_qkv, v_b_qkv), "ln_v_g": (ln_v_g, m_ln_v_g, v_ln_v_g),
             "ln_v_b": (ln_v_b, m_ln_v_b, v_ln_v_b), "attn_sinks": (attn_sinks, m_attn_sinks, v_attn_sinks),
             "w_spatial": tuple(a.reshape(NG * T, T) for a in (w_spatial, m_w_spatial, v_w_spatial)),
             "b_spatial": tuple(a.reshape(NG, T) for a in (b_spatial, m_b_spatial, v_b_spatial))}
    grads = {"g_pre": dg_pre, "g_post": dg_post, "b_qkv": dbqkv, "ln_v_g": dln_g, "ln_v_b": dln_b,
             "attn_sinks": dsink[:, 0].reshape(1, NQ)}
    results, loss = _small_update([grads[n] for n, _ in VEC], loss_part, mat_total,
                                  [state[n] for n, _ in VEC], [state[n] for n, _ in MAT])
    small = {n: [a.reshape(w.shape) for a in res]
             for (n, _), res, w in zip(VEC + MAT, results, [state[n][0] for n, _ in VEC + MAT])}
    small["w_spatial"] = [a.reshape(w_spatial.shape) for a in small["w_spatial"]]
    small["b_spatial"] = [a.reshape(b_spatial.shape) for a in small["b_spatial"]]
    big = {"w_in": [a[None] for a in (g_w_in, d_w_in, nm_w_in, nv_w_in)],
           "w_out": [a[None] for a in (g_w_out, d_w_out, nm_w_out, nv_w_out)]}
    order = ("g_pre", "w_in", "b_qkv", "ln_v_g", "ln_v_b", "w_spatial", "b_spatial", "attn_sinks", "w_out", "g_post")
    leaves = {**small, **big}
    return (loss[0, 0], grad_x[None], *[leaves[n][t] for t in range(4) for n in order])
```

```python
import jax
import jax.numpy as jnp
from jax import lax
from jax.experimental import pallas as pl
from jax.experimental.pallas import tpu as pltpu

F32 = jnp.float32
BF16 = jnp.bfloat16
MESH = pl.DeviceIdType.MESH

D = 2048
DG = 1024
T = 128
NG = 8
HD = 64
NQ = 16
D_IN = 5376
OFF_U, OFF_V, OFF_ZA, OFF_Q, OFF_K, OFF_VA, OFF_ZB = 0, 1024, 2048, 3072, 4096, 4224, 4352
D_QKV = 1280
EPS = 1e-6
ROPE_THETA = 10000.0
N_CHIPS = 4
N_DEV = 8
W_IN_ROWS = D_IN // N_DEV
W_OUT_ROWS = D // N_DEV

ADAM_LR, ADAM_B1, ADAM_B2, ADAM_EPS, ADAM_WD, ADAM_STEP = 0.001, 0.9, 0.999, 1e-08, 0.01, 10

VMEM_LIMIT = 56 * 1024 * 1024


def _tile(n, pref):
    return pref if n % pref == 0 else n


def _params(sem=None, vmem=VMEM_LIMIT):
    return pltpu.CompilerParams(dimension_semantics=sem, vmem_limit_bytes=vmem)


def _sigmoid(z):
    return 1.0 / (1.0 + jnp.exp(-z))


def _row_mean(v):
    return jnp.mean(v, axis=-1, keepdims=True)


def _dot(a, b, dims):
    return lax.dot_general(a, b, (dims, ((), ())), preferred_element_type=F32)


NN = ((1,), (0,))
NT = ((1,), (1,))
TN = ((0,), (0,))


PROJ_TN = 768
N_PROJ_TILES = D_IN // PROJ_TN
PROJ_SCHEDULE = ((0, 2, 4, 1, 3, 6, 5, 5), (2, 0, 6, 1, 4, 3, 5, 4), (4, 6, 0, 5, 2, 1, 3, 4), (6, 4, 2, 5, 3, 0, 1, 5))


LOCAL_CHUNKS = 6


def _place_locally(src_ref, dst_rows_ref, sems, n_chunks=LOCAL_CHUNKS):
    ch = src_ref.shape[0] // n_chunks
    return [pltpu.make_async_copy(src_ref.at[pl.ds(q * ch, ch), :], dst_rows_ref.at[pl.ds(q * ch, ch), :], sems.at[q])
            for q in range(n_chunks)]


def _prenorm_inproj(sched, x, g, bias, wt_part, pos, invf):
    S = x.shape[0]
    tp, tm = _tile(S, 512), _tile(S, 1024)
    n_pre, ns = S // tp, S // tm
    n_steps = n_pre + N_PROJ_TILES * ns
    pos_of = lambda i: jnp.maximum(i - n_pre, 0) // ns
    row_of = lambda i: jnp.maximum(i - n_pre, 0) % ns

    def body(sched_ref, x_ref, g_ref, b_ref, wpart_ref, pos_ref, invf_ref, h_ref, proj_ref, wt_ref, cos_ref, sin_ref,
             h_all, w_tile, stage, send_sems, recv_sems, w_sems, local_sems):
        i = pl.program_id(0)
        x_, y_, c = _mesh_pos()
        me, sibling = (x_, y_, c), (x_, y_, 1 - c)
        chips = [(1 - x_, y_), (x_, 1 - y_), (1 - x_, 1 - y_)]

        def block(pos):
            px, py, pc = pos
            return wt_ref.at[pl.ds(pl.multiple_of((4 * px + 2 * py + pc) * W_IN_ROWS, 16), W_IN_ROWS), :]

        def copy(k, blk, to):
            return pltpu.make_async_remote_copy(
                src_ref=wpart_ref if blk is me else block(blk), dst_ref=block(blk),
                send_sem=send_sems.at[k], recv_sem=recv_sems.at[k], device_id=to, device_id_type=MESH)

        stage_in = pltpu.make_async_copy(wpart_ref, stage, local_sems.at[0])
        stage_out = pltpu.make_async_copy(stage, block(me), local_sems.at[1])

        relay = (jnp.where(c == 0, x_, 1 - x_), jnp.where(c == 0, 1 - y_, y_))
        relayed = (jnp.where(c == 0, 1 - x_, x_), jnp.where(c == 0, y_, 1 - y_))

        def own_sends():
            return [copy(0, me, sibling), copy(2, me, (*chips[1], c)), copy(1, me, (*chips[0], c))]

        def passed_on():
            return [copy(4, (*chips[0], c), sibling), copy(5, (*chips[1], c), sibling), copy(3, (*relayed, c), (*relay, c))]

        def neighbours_arrive():
            copy(1, (*chips[0], c), me).wait_recv()
            copy(2, (*chips[1], c), me).wait_recv()
            for cp in passed_on():
                cp.start()
            copy(5, (*chips[1], 1 - c), me).wait_recv()

        def diagonal_arrives():
            copy(3, (*chips[2], c), me).wait_recv()
            copy(6, (*chips[2], c), sibling).start()
            copy(6, (*chips[2], 1 - c), me).wait_recv()

        def tile_load(p):
            slot = p % 2
            rows = wt_ref.at[pl.ds(pl.multiple_of(sched_ref[p] * PROJ_TN, 16), PROJ_TN), :]
            return pltpu.make_async_copy(rows, w_tile.at[slot], w_sems.at[slot])

        def prepare(p):
            p = jnp.asarray(p, jnp.int32)

            @pl.when(p == 0)
            def _():
                copy(0, sibling, me).wait_recv()
                stage_out.wait()

            pl.when(p == 1)(neighbours_arrive)
            pl.when(p == 2)(lambda: copy(4, (*chips[0], 1 - c), me).wait_recv())
            pl.when(p == sched_ref[N_PROJ_TILES])(diagonal_arrives)
            tile_load(p).start()

        @pl.when(i == 0)
        def _():
            stage_in.start()
            for cp in own_sends():
                cp.start()
            stage_in.wait()
            stage_out.start()

        @pl.when(i < n_pre)
        def _():
            xv = x_ref[...]
            r = lax.rsqrt(_row_mean(xv * xv) + EPS)
            hv = (xv * r * g_ref[...]).astype(BF16)
            h_ref[...] = hv
            h_all[pl.ds(pl.multiple_of(i * tp, tp), tp), :] = hv
            ang = pos_ref[...].astype(F32) * invf_ref[...]
            cos_ref[...] = jnp.cos(ang)
            sin_ref[...] = jnp.sin(ang)

        pl.when(i == n_pre - 1)(lambda: prepare(0))

        @pl.when(i >= n_pre)
        def _():
            p, s = pos_of(i), row_of(i)
            pl.when(s == 0)(lambda: tile_load(p).wait())
            pl.when((s == ns - 1) & (p < N_PROJ_TILES - 1))(lambda: prepare(p + 1))
            hv = h_all[pl.ds(pl.multiple_of(s * tm, tm), tm), :]
            proj_ref[...] = _dot(hv, w_tile[p % 2], NT) + b_ref[...]

        @pl.when(i == n_steps - 1)
        def _():
            for cp in own_sends() + passed_on() + [copy(6, (*chips[2], c), sibling)]:
                cp.wait_send()

    return pl.pallas_call(
        body, name="prenorm_inproj",
        grid_spec=pltpu.PrefetchScalarGridSpec(
            num_scalar_prefetch=1, grid=(n_steps,),
            in_specs=[pl.BlockSpec((tp, D), lambda i, sc: (jnp.minimum(i, n_pre - 1), 0)),
                      pl.BlockSpec((1, D), lambda i, sc: (0, 0)),
                      pl.BlockSpec((1, PROJ_TN), lambda i, sc: (0, sc[pos_of(i)])),
                      HBM_SPEC,
                      pl.BlockSpec((tp, 1), lambda i, sc: (jnp.minimum(i, n_pre - 1), 0)),
                      pl.BlockSpec((1, 128), lambda i, sc: (0, 0))],
            out_specs=[pl.BlockSpec((tp, D), lambda i, sc: (jnp.minimum(i, n_pre - 1), 0)),
                       pl.BlockSpec((tm, PROJ_TN), lambda i, sc: (row_of(i), sc[pos_of(i)])),
                       HBM_SPEC,
                       pl.BlockSpec((tp, 128), lambda i, sc: (jnp.minimum(i, n_pre - 1), 0)),
                       pl.BlockSpec((tp, 128), lambda i, sc: (jnp.minimum(i, n_pre - 1), 0))],
            scratch_shapes=[pltpu.VMEM((S, D), BF16), pltpu.VMEM((2, PROJ_TN, D), BF16),
                            pltpu.VMEM((W_IN_ROWS, D), BF16),
                            pltpu.SemaphoreType.DMA((7,)), pltpu.SemaphoreType.DMA((7,)),
                            pltpu.SemaphoreType.DMA((2,)), pltpu.SemaphoreType.DMA((2,))]),
        out_shape=[jax.ShapeDtypeStruct((S, D), BF16), jax.ShapeDtypeStruct((S, D_IN), F32),
                   jax.ShapeDtypeStruct((D_IN, D), BF16),
                   jax.ShapeDtypeStruct((S, 128), F32), jax.ShapeDtypeStruct((S, 128), F32)],
        compiler_params=_params(("arbitrary",)),
    )(sched, x, g, bias, wt_part, pos, invf)


def _rot_half(xs, first_half):
    return jnp.where(first_half, -pltpu.roll(xs, 96, 1), pltpu.roll(xs, 32, 1))


GH = NQ // 2


MASKED = -1e30


def _attn_consts():
    lane = lax.broadcasted_iota(jnp.int32, (T, 128), 1)
    row = lax.broadcasted_iota(jnp.int32, (GH * T, T), 0) & (T - 1)
    on_diag_or_below = row >= lax.broadcasted_iota(jnp.int32, (GH * T, T), 1)
    return (lane & (HD - 1)) < (HD // 2), lane < HD, on_diag_or_below


def _stack_heads(slab_fn, grp, lo64):
    blocks = []
    for jj in range(GH // 2):
        s = slab_fn(GH // 2 * grp + jj)
        blocks += [jnp.where(lo64, s, 0.0), jnp.where(lo64, 0.0, s)]
    return jnp.concatenate(blocks, axis=0)


def _unstack_heads(stacked, jj, lo64):
    return jnp.where(lo64, stacked[(2 * jj) * T:(2 * jj + 1) * T], stacked[(2 * jj + 1) * T:(2 * jj + 2) * T])


def _both_halves(kv, grp):
    lo = lax.broadcasted_iota(jnp.int32, kv.shape, 1) < HD
    swapped = pltpu.roll(kv, HD, 1)
    return jnp.where(lo, kv, swapped) if grp == 0 else jnp.where(lo, swapped, kv)


def _one_half(acc, grp):
    lo = lax.broadcasted_iota(jnp.int32, acc.shape, 1) < HD
    return jnp.where(lo if grp == 0 else jnp.logical_not(lo), acc + pltpu.roll(acc, HD, 1), 0.0)


def _layer_norm_stats(v):
    mu = _row_mean(v)
    xc = v - mu
    var = _row_mean(xc * xc)
    rs = lax.rsqrt(var + EPS)
    return xc * rs, rs


def _fold(both, own):
    return jnp.where(own, both[:, T:2 * T], both[:, 0:T])


def _unfold(p, own):
    return jnp.concatenate([jnp.where(own, 0.0, p), jnp.where(own, p, 0.0)], axis=1).astype(BF16)


def _band_softmax(q_scaled, k_both, own, has_prev, sink):
    s_both = _dot(q_scaled, k_both, NT)
    s_prev = s_both[:, 0:T]
    if has_prev is not None:
        s_prev = s_prev + jnp.where(has_prev, 0.0, MASKED)
    s = jnp.where(own, s_both[:, T:2 * T], s_prev)
    m = jnp.maximum(jnp.max(s, axis=-1, keepdims=True), sink)
    e = jnp.exp(s - m)
    es = jnp.exp(sink - m)
    inv = 1.0 / (jnp.sum(e, axis=-1, keepdims=True) + es)
    return e * inv, es * inv


BPS_FWD = 4
BPS_BWD = 4


def _mid_specs(nt, rev, bps):
    tile = (lambda i: nt - 1 - i) if rev else (lambda i: i)
    prev = lambda i: jnp.maximum(bps * tile(i) - 1, 0)
    rows = bps * T
    return tile, [
        pl.BlockSpec((rows, D_IN), lambda i: (tile(i), 0)),
        pl.BlockSpec((T, 2 * T), lambda i: (prev(i), OFF_K // (2 * T))),
    ], [
        pl.BlockSpec((rows, 128), lambda i: (tile(i), 0)),
        pl.BlockSpec((rows, 128), lambda i: (tile(i), 0)),
        pl.BlockSpec((T, 128), lambda i: (prev(i), 0)),
        pl.BlockSpec((T, 128), lambda i: (prev(i), 0)),
        pl.BlockSpec((1, DG), lambda i: (0, 0)),
        pl.BlockSpec((1, DG), lambda i: (0, 0)),
        pl.BlockSpec((NG, T, T), lambda i: (0, 0, 0)),
        pl.BlockSpec((NG, T, 1), lambda i: (0, 0, 0)),
        pl.BlockSpec((NQ * T, 1), lambda i: (0, 0)),
    ]


def _rope(xs, cosv, sinv, first_half):
    return xs * cosv + _rot_half(xs, first_half) * sinv


def _mid_fwd(proj, cos, sin, ln_g, ln_b, w_s, b_s, sinks, wo_part):
    S = proj.shape[0]
    BPS = BPS_FWD if S % (BPS_FWD * T) == 0 else 1
    nt = S // (BPS * T)
    tile, proj_specs, par_specs = _mid_specs(nt, False, BPS)

    def body(p_ref, kvp_ref, cos_ref, sin_ref, cosp_ref, sinp_ref, lng_ref, lnb_ref, ws_ref, bs_ref, sink_ref, wpart_ref,
             y_ref, wo_ref, wm_ref, send_sems, recv_sems, local_sems):
        i = pl.program_id(0)
        first_half, lo64, own = _attn_consts()
        gather = _RowGather(wpart_ref, wo_ref, W_OUT_ROWS, send_sems, recv_sems, local_sems)
        pl.when(i == 0)(gather.start)

        @pl.when(i == 0)
        def _():
            tril = lax.broadcasted_iota(jnp.int32, (T, T), 0) >= lax.broadcasted_iota(jnp.int32, (T, T), 1)
            for g in range(NG):
                wm_ref[g] = jnp.where(tril, ws_ref[g], 0.0).astype(BF16)

        def block(b, k_prev, v_prev, has_prev):
            rows = slice(b * T, (b + 1) * T)
            xhat, _ = _layer_norm_stats(p_ref[rows, OFF_V:OFF_V + DG])
            vn = xhat * lng_ref[...] + lnb_ref[...]
            for g in range(NG):
                sl = slice(128 * g, 128 * g + 128)
                mixed = _dot(wm_ref[g], vn[:, sl].astype(BF16), NN) + bs_ref[g]
                z = p_ref[rows, OFF_ZA + 128 * g:OFF_ZA + 128 * g + 128]
                u = p_ref[rows, OFF_U + 128 * g:OFF_U + 128 * g + 128]
                y_ref[rows, sl] = (u * mixed * (z * _sigmoid(z))).astype(BF16)

            cosv, sinv = cos_ref[rows, :], sin_ref[rows, :]
            k_cur = _rope(p_ref[rows, OFF_K:OFF_K + 128], cosv, sinv, first_half)
            v_cur = p_ref[rows, OFF_VA:OFF_VA + 128]

            def q_slab(j):
                return _rope(p_ref[rows, OFF_Q + 128 * j:OFF_Q + 128 * j + 128], cosv, sinv, first_half) * (HD ** -0.5)

            k_both = jnp.concatenate([k_prev, k_cur], axis=0)
            v_both = jnp.concatenate([v_prev, v_cur], axis=0)
            for grp in range(2):
                qs = _stack_heads(q_slab, grp, lo64).astype(BF16)
                kg, vg = _both_halves(k_both, grp).astype(BF16), _both_halves(v_both, grp).astype(BF16)
                p, _ = _band_softmax(qs, kg, own, has_prev, sink_ref[GH * T * grp:GH * T * (grp + 1), :])
                o_st = _dot(_unfold(p, own), vg, NN)
                for jj in range(GH // 2):
                    c0 = 128 * (GH // 2 * grp + jj)
                    zb = p_ref[rows, OFF_ZB + c0:OFF_ZB + c0 + 128]
                    o = _unstack_heads(o_st, jj, lo64)
                    y_ref[rows, DG + c0:DG + c0 + 128] = (o * (zb * _sigmoid(zb))).astype(BF16)
            return k_cur, v_cur

        k_prev = _rope(kvp_ref[:, 0:128], cosp_ref[...], sinp_ref[...], first_half)
        kv = block(0, k_prev, kvp_ref[:, 128:256], i > 0)
        for b in range(1, BPS):
            kv = block(b, *kv, None)
        pl.when(i == nt - 1)(gather.finish)

    return pl.pallas_call(
        body, name="mid_fwd", grid=(nt,),
        in_specs=proj_specs + par_specs + [HBM_SPEC],
        out_specs=[pl.BlockSpec((BPS * T, D), lambda i: (tile(i), 0)), HBM_SPEC],
        out_shape=[jax.ShapeDtypeStruct((S, D), BF16), jax.ShapeDtypeStruct((D, D), BF16)],
        scratch_shapes=[pltpu.VMEM((NG, T, T), BF16)] + GATHER_SEMS,
        compiler_params=_params(("arbitrary",)),
    )(proj, proj, cos, sin, cos, sin, ln_g, ln_b, w_s, b_s, sinks, wo_part)


def _mid_bwd(proj, dycat, cos, sin, ln_g, ln_b, w_s, b_s, sinks, po):
    S = proj.shape[0]
    BPS = BPS_BWD if S % (BPS_BWD * T) == 0 else 1
    nt = S // (BPS * T)
    tile, proj_specs, par_specs = _mid_specs(nt, True, BPS)
    const2 = lambda i: (0, 0)

    def body(p_ref, kvp_ref, dyc_ref, cos_ref, sin_ref, cosp_ref, sinp_ref, lng_ref, lnb_ref, ws_ref, bs_ref, sink_ref,
             po_ref, dp_ref, dlng_ref, dlnb_ref, dws_ref, dbs_ref, dsink_ref, dbqkv_ref, bo_ref,
             carry_ref, dvn_ref, wm_ref, wmt_ref, send_sems, recv_sems, local_sem):
        i = pl.program_id(0)
        first_half, lo64, own = _attn_consts()
        tril = lax.broadcasted_iota(jnp.int32, (T, T), 0) >= lax.broadcasted_iota(jnp.int32, (T, T), 1)
        exchange = _OwnerExchange(po_ref, bo_ref, send_sems, recv_sems, local_sem)
        pl.when(i == 0)(exchange.start)

        @pl.when(i == 0)
        def _():
            for g in range(NG):
                wm = jnp.where(tril, ws_ref[g], 0.0)
                wm_ref[g] = wm.astype(BF16)
                wmt_ref[g] = wm.T.astype(BF16)
            dlng_ref[...] = jnp.zeros_like(dlng_ref)
            dlnb_ref[...] = jnp.zeros_like(dlnb_ref)
            dws_ref[...] = jnp.zeros_like(dws_ref)
            dbs_ref[...] = jnp.zeros_like(dbs_ref)
            dsink_ref[...] = jnp.zeros_like(dsink_ref)
            dbqkv_ref[...] = jnp.zeros_like(dbqkv_ref)
            carry_ref[...] = jnp.zeros_like(carry_ref)

        def roped_k(b):
            rows = slice(b * T, (b + 1) * T)
            return _rope(p_ref[rows, OFF_K:OFF_K + 128], cos_ref[rows, :], sin_ref[rows, :], first_half)

        def block(b, k_prev, v_prev, has_prev, dk_next, dv_next):
            rows = slice(b * T, (b + 1) * T)
            xhat, rs = _layer_norm_stats(p_ref[rows, OFF_V:OFF_V + DG])
            lng = lng_ref[...]
            vn = xhat * lng + lnb_ref[...]
            for g in range(NG):
                sl = slice(128 * g, 128 * g + 128)
                vng = vn[:, sl].astype(BF16)
                mixed = _dot(wm_ref[g], vng, NN) + bs_ref[g]
                z = p_ref[rows, OFF_ZA + 128 * g:OFF_ZA + 128 * g + 128]
                u = p_ref[rows, OFF_U + 128 * g:OFF_U + 128 * g + 128]
                dy = dyc_ref[rows, sl]
                sg = _sigmoid(z)
                sa = z * sg
                dp_ref[rows, OFF_U + 128 * g:OFF_U + 128 * g + 128] = (dy * mixed * sa).astype(BF16)
                dp_ref[rows, OFF_ZA + 128 * g:OFF_ZA + 128 * g + 128] = (
                    dy * u * mixed * (sg * (1.0 + z * (1.0 - sg)))).astype(BF16)
                dm = dy * u * sa
                dmb = dm.astype(BF16)
                dvn_ref[rows, sl] = _dot(wmt_ref[g], dmb, NN)
                dws_ref[g] += jnp.where(tril, _dot(dmb, vng, NT), 0.0)
                dbs_ref[g] += jnp.sum(dm, axis=1, keepdims=True)
            dvn = dvn_ref[rows, :]
            dlng_ref[...] += jnp.sum(dvn * xhat, axis=0, keepdims=True)
            dlnb_ref[...] += jnp.sum(dvn, axis=0, keepdims=True)
            dxh = dvn * lng
            dv_g = rs * (dxh - _row_mean(dxh)
                         - xhat * _row_mean(dxh * xhat))
            dp_ref[rows, OFF_V:OFF_V + DG] = dv_g.astype(BF16)

            cosv, sinv = cos_ref[rows, :], sin_ref[rows, :]
            k_cur = roped_k(b)
            v_cur = p_ref[rows, OFF_VA:OFF_VA + 128]

            def q_slab(j):
                return _rope(p_ref[rows, OFF_Q + 128 * j:OFF_Q + 128 * j + 128], cosv, sinv, first_half) * (HD ** -0.5)

            def do_slab(j):
                zb = p_ref[rows, OFF_ZB + 128 * j:OFF_ZB + 128 * j + 128]
                return dyc_ref[rows, DG + 128 * j:DG + 128 * j + 128] * (zb * _sigmoid(zb))

            k_both = jnp.concatenate([k_prev, k_cur], axis=0)
            v_both = jnp.concatenate([v_prev, v_cur], axis=0)
            dk_both, dv_both = jnp.zeros((2 * T, 128), F32), jnp.zeros((2 * T, 128), F32)
            for grp in range(2):
                qs = _stack_heads(q_slab, grp, lo64).astype(BF16)
                d_o = _stack_heads(do_slab, grp, lo64)
                dob = d_o.astype(BF16)
                kg, vg = _both_halves(k_both, grp).astype(BF16), _both_halves(v_both, grp).astype(BF16)
                p, ps = _band_softmax(qs, kg, own, has_prev, sink_ref[GH * T * grp:GH * T * (grp + 1), :])
                p_both = _unfold(p, own)
                o_st = _dot(p_both, vg, NN)
                delta = jnp.sum(d_o * o_st, axis=-1, keepdims=True)
                ds_both = _unfold(p * (_fold(_dot(dob, vg, NT), own) - delta), own)
                dq_st = _dot(ds_both, kg, NN) * (HD ** -0.5)
                dk_both = dk_both + _one_half(_dot(ds_both, qs, TN), grp)
                dv_both = dv_both + _one_half(_dot(p_both, dob, TN), grp)
                dsink_rows = ps * delta
                for hh in range(GH):
                    h = GH * grp + hh
                    dsink_ref[h:h + 1, :] += jnp.broadcast_to(
                        -jnp.sum(dsink_rows[hh * T:(hh + 1) * T], axis=0, keepdims=True), (1, 128))
                for jj in range(GH // 2):
                    c0 = 128 * (GH // 2 * grp + jj)
                    zb = p_ref[rows, OFF_ZB + c0:OFF_ZB + c0 + 128]
                    sg = _sigmoid(zb)
                    o = _unstack_heads(o_st, jj, lo64)
                    dp_ref[rows, OFF_ZB + c0:OFF_ZB + c0 + 128] = (
                        dyc_ref[rows, DG + c0:DG + c0 + 128] * o * (sg * (1.0 + zb * (1.0 - sg)))).astype(BF16)
                    dq = _unstack_heads(dq_st, jj, lo64)
                    dq_pre = dq * cosv - _rot_half(dq, first_half) * sinv
                    dp_ref[rows, OFF_Q + c0:OFF_Q + c0 + 128] = dq_pre.astype(BF16)
                    dbqkv_ref[:, c0:c0 + 128] += jnp.sum(dq_pre, axis=0, keepdims=True)
            dk_prev, dv_prev = dk_both[0:T], dv_both[0:T]
            dk_cur, dv_cur = dk_both[T:2 * T] + dk_next, dv_both[T:2 * T] + dv_next
            dk_pre = dk_cur * cosv - _rot_half(dk_cur, first_half) * sinv
            dp_ref[rows, OFF_K:OFF_K + 128] = dk_pre.astype(BF16)
            dp_ref[rows, OFF_VA:OFF_VA + 128] = dv_cur.astype(BF16)
            dbqkv_ref[:, 1024:1152] += jnp.sum(dk_pre, axis=0, keepdims=True)
            dbqkv_ref[:, 1152:1280] += jnp.sum(dv_cur, axis=0, keepdims=True)
            return dk_prev, dv_prev

        grads = carry_ref[:, 0:128], carry_ref[:, 128:256]
        for b in range(BPS - 1, 0, -1):
            prows = slice((b - 1) * T, b * T)
            grads = block(b, roped_k(b - 1), p_ref[prows, OFF_VA:OFF_VA + 128], None, *grads)
        k_prev = _rope(kvp_ref[:, 0:128], cosp_ref[...], sinp_ref[...], first_half)
        grads = block(0, k_prev, kvp_ref[:, 128:256], i < nt - 1, *grads)
        carry_ref[:, 0:128], carry_ref[:, 128:256] = grads
        pl.when(i == nt - 1)(exchange.finish)

    return pl.pallas_call(
        body, name="mid_bwd", grid=(nt,),
        in_specs=proj_specs + [pl.BlockSpec((BPS * T, D), lambda i: (tile(i), 0))] + par_specs + [HBM_SPEC],
        out_specs=[pl.BlockSpec((BPS * T, D_IN), lambda i: (tile(i), 0)),
                   pl.BlockSpec((1, DG), const2), pl.BlockSpec((1, DG), const2),
                   pl.BlockSpec((NG, T, T), lambda i: (0, 0, 0)), pl.BlockSpec((NG, T, 1), lambda i: (0, 0, 0)),
                   pl.BlockSpec((NQ, 128), const2), pl.BlockSpec((1, D_QKV), const2), HBM_SPEC],
        out_shape=[jax.ShapeDtypeStruct((S, D_IN), BF16),
                   jax.ShapeDtypeStruct((1, DG), F32), jax.ShapeDtypeStruct((1, DG), F32),
                   jax.ShapeDtypeStruct((NG, T, T), F32), jax.ShapeDtypeStruct((NG, T, 1), F32),
                   jax.ShapeDtypeStruct((NQ, 128), F32), jax.ShapeDtypeStruct((1, D_QKV), F32),
                   jax.ShapeDtypeStruct(po.shape, BF16)],
        scratch_shapes=[pltpu.VMEM((T, 2 * T), F32), pltpu.VMEM((BPS * T, DG), F32),
                        pltpu.VMEM((NG, T, T), BF16), pltpu.VMEM((NG, T, T), BF16)] + OWNER_SEMS,
        compiler_params=_params(("arbitrary",)),
    )(proj, proj, dycat, cos, sin, cos, sin, ln_g, ln_b, w_s, b_s, sinks, po)


def _outproj_loss(ycat, wo, x, target, g_post):
    S = ycat.shape[0]
    tm = _tile(S, 512)
    nt = S // tm
    n_part = 2 if tm % 32 == 0 else 1
    tp = tm // n_part
    const2 = lambda i: (0, 0)

    def body(yc_ref, w_ref, x_ref, t_ref, g_ref, dy_ref, dout_ref, loss_ref, dg_ref, lacc_ref):
        i = pl.program_id(0)

        @pl.when(i == 0)
        def _():
            dg_ref[...] = jnp.zeros_like(dg_ref)
            lacc_ref[...] = jnp.zeros_like(lacc_ref)

        g = g_ref[...]
        ys = [_dot(yc_ref[q * tp:(q + 1) * tp, :], w_ref[...], NN) for q in range(n_part)]
        for q, y in enumerate(ys):
            rows = slice(q * tp, (q + 1) * tp)
            r = lax.rsqrt(_row_mean(y * y) + EPS)
            yh = y * r
            diff = x_ref[rows, :] + yh * g - t_ref[rows, :]
            lacc_ref[...] += jnp.sum(diff * diff, axis=0, keepdims=True)
            dout = diff * (1.0 / D)
            dout_ref[rows, :] = dout
            dg_ref[...] += jnp.sum(dout * yh, axis=0, keepdims=True)
            dyh = dout * g
            dy_ref[rows, :] = (r * (dyh - yh * _row_mean(dyh * yh))).astype(BF16)

        @pl.when(i == nt - 1)
        def _():
            loss_ref[...] = jnp.broadcast_to(jnp.sum(lacc_ref[...], axis=1, keepdims=True) * (0.5 / D), (1, 128))

    row = lambda i: (i, 0)
    return pl.pallas_call(
        body, name="outproj_loss", grid=(nt,),
        in_specs=[pl.BlockSpec((tm, D), row), pl.BlockSpec((D, D), const2, pipeline_mode=pl.Buffered(1)),
                  pl.BlockSpec((tm, D), row), pl.BlockSpec((tm, D), row), pl.BlockSpec((1, D), const2)],
        out_specs=[pl.BlockSpec((tm, D), row), pl.BlockSpec((tm, D), row), pl.BlockSpec((1, 128), const2),
                   pl.BlockSpec((1, D), const2)],
        out_shape=[jax.ShapeDtypeStruct((S, D), BF16), jax.ShapeDtypeStruct((S, D), F32),
                   jax.ShapeDtypeStruct((1, 128), F32), jax.ShapeDtypeStruct((1, D), F32)],
        scratch_shapes=[pltpu.VMEM((1, D), F32)],
        compiler_params=_params(("arbitrary",)),
    )(ycat, wo, x, target, g_post)


def _dycat(dy, wo, dwo):
    S = dy.shape[0]
    tm = _tile(S, 512)
    nt = S // tm

    def body(dy_ref, w_ref, dwo_ref, o_ref, r_ref, send_sems, recv_sems):
        i = pl.program_id(0)
        copies = _sibling_copies(dwo_ref, r_ref, W_OUT_ROWS, send_sems, recv_sems)

        @pl.when(i == 0)
        def _():
            for cp in copies:
                cp.start()

        o_ref[...] = _dot(dy_ref[...], w_ref[...], NT)

        @pl.when(i == nt - 1)
        def _():
            for cp in copies:
                cp.wait_recv()
            for cp in copies:
                cp.wait_send()

    return pl.pallas_call(
        body, name="dycat", grid=(nt,),
        in_specs=[pl.BlockSpec((tm, D), lambda i: (i, 0)),
                  pl.BlockSpec((D, D), lambda i: (0, 0), pipeline_mode=pl.Buffered(1)), HBM_SPEC],
        out_specs=[pl.BlockSpec((tm, D), lambda i: (i, 0)), HBM_SPEC],
        out_shape=[jax.ShapeDtypeStruct((S, D), F32), jax.ShapeDtypeStruct((N_CHIPS, W_OUT_ROWS, D), BF16)],
        scratch_shapes=[pltpu.SemaphoreType.DMA((N_CHIPS,)), pltpu.SemaphoreType.DMA((N_CHIPS,))],
        compiler_params=_params(("arbitrary",)),
    )(dy, wo, dwo)


def _matmul_tn(a, b, tm, name, mat_grads=None):
    K, M = a.shape
    N = b.shape[1]
    tk = _tile(K, 1024)
    ni, nk = M // tm, K // tk
    hosting = mat_grads is not None

    def body(a_ref, b_ref, *rest):
        if hosting:
            g_in, (o_ref, tot_ref, acc_ref, mbuf, send_sems, recv_sems) = rest[:len(MAT)], rest[len(MAT):]
        else:
            o_ref, acc_ref = rest
        i, k = pl.program_id(0), pl.program_id(1)

        if hosting:
            gather = _SlotGather(mbuf, send_sems, recv_sems)

            @pl.when((i == 0) & (k == 0))
            def _():
                r0 = 0
                for q, (_, rows) in enumerate(MAT):
                    mbuf[gather.my_id, r0:r0 + rows, :] = g_in[q][...]
                    r0 += rows
                gather.start()

        @pl.when(k == 0)
        def _():
            acc_ref[...] = jnp.zeros_like(acc_ref)

        acc_ref[...] += _dot(a_ref[...], b_ref[...], TN)

        @pl.when(k == nk - 1)
        def _():
            o_ref[...] = acc_ref[...].astype(BF16)

        if hosting:
            @pl.when((i == ni - 1) & (k == nk - 1))
            def _():
                gather.finish()
                tot_ref[...] = gather.total()

    whole = pl.BlockSpec(memory_space=pltpu.VMEM)
    in_specs = [pl.BlockSpec((tk, tm), lambda i, k: (k, i)), pl.BlockSpec((tk, N), lambda i, k: (k, 0))]
    out_specs = [pl.BlockSpec((tm, N), lambda i, k: (i, 0))]
    out_shape = [jax.ShapeDtypeStruct((M, N), BF16)]
    scratch = [pltpu.VMEM((tm, N), F32)]
    if hosting:
        in_specs += [whole] * len(MAT)
        out_specs.append(whole)
        out_shape.append(jax.ShapeDtypeStruct((MAT_ROWS, 128), F32))
        scratch += [pltpu.VMEM((N_DEV, MAT_ROWS, 128), F32), pltpu.SemaphoreType.DMA((7,)),
                    pltpu.SemaphoreType.DMA((7,))]
    res = pl.pallas_call(
        body, name=name, grid=(ni, nk), in_specs=in_specs, out_specs=out_specs, out_shape=out_shape,
        scratch_shapes=scratch,
        compiler_params=_params(("arbitrary", "arbitrary") if hosting else ("parallel", "arbitrary")),
    )(a, b, *(mat_grads or ()))
    return res if hosting else res[0]


def _dh_prenorm_bwd(dproj, wt, x, dout, g_pre, pt):
    S = x.shape[0]
    tm, tk = _tile(S, 1024), 768
    ne = 4
    te = tm // ne
    ni, nk = S // tm, D_IN // tk
    n_mm = ni * nk
    n_steps = n_mm + ne
    mm_of = lambda t: jnp.minimum(t, n_mm - 1)

    def quarter_of(t):
        j, q = t // nk - 1, t % nk
        nxt = jnp.where(q < ne, j * ne + q, (j + 1) * ne)
        return jnp.clip(nxt, 0, ni * ne - 1)

    def body(dp_ref, w_ref, x_ref, dout_ref, g_ref, pt_ref, gx_ref, dg_ref, bt_ref, acc_ref,
             send_sems, recv_sems, local_sem):
        t = pl.program_id(0)
        exchange = _OwnerExchange(pt_ref, bt_ref, send_sems, recv_sems, local_sem)
        pl.when(t == 0)(exchange.start)

        @pl.when(t == 0)
        def _():
            dg_ref[...] = jnp.zeros_like(dg_ref)

        has_mm = t < n_mm
        has_q = (t >= nk) & (t % nk < ne)

        def matmul():
            slot = (mm_of(t) // nk) % 2
            old = jnp.where(mm_of(t) % nk > 0, acc_ref[slot], 0.0)
            acc_ref[slot] = old + _dot(dp_ref[...], w_ref[...], NN)

        def quarter():
            slot = (t // nk - 1) % 2
            dh = acc_ref[slot, pl.ds(pl.multiple_of((t % nk) * te, te), te), :]
            xv = x_ref[...]
            r = lax.rsqrt(_row_mean(xv * xv) + EPS)
            xh = xv * r
            dg_ref[...] += jnp.sum(dh * xh, axis=0, keepdims=True)
            dxh = dh * g_ref[...]
            gx_ref[...] = dout_ref[...] + r * (dxh - xh * _row_mean(dxh * xh))

        @pl.when(has_mm & has_q)
        def _():
            matmul()
            quarter()

        pl.when(has_mm & jnp.logical_not(has_q))(matmul)
        pl.when(jnp.logical_not(has_mm) & has_q)(quarter)
        pl.when(t == n_steps - 1)(exchange.finish)

    quarter_block = lambda t: (quarter_of(t), 0)
    return pl.pallas_call(
        body, name="dh_prenorm_bwd", grid=(n_steps,),
        in_specs=[pl.BlockSpec((tm, tk), lambda t: (mm_of(t) // nk, mm_of(t) % nk)),
                  pl.BlockSpec((tk, D), lambda t: (mm_of(t) % nk, 0)),
                  pl.BlockSpec((te, D), quarter_block), pl.BlockSpec((te, D), quarter_block),
                  pl.BlockSpec((1, D), lambda t: (0, 0)), HBM_SPEC],
        out_specs=[pl.BlockSpec((te, D), quarter_block), pl.BlockSpec((1, D), lambda t: (0, 0)), HBM_SPEC],
        out_shape=[jax.ShapeDtypeStruct((S, D), F32), jax.ShapeDtypeStruct((1, D), F32),
                   jax.ShapeDtypeStruct(pt.shape, BF16)],
        scratch_shapes=[pltpu.VMEM((2, tm, D), F32)] + OWNER_SEMS,
        compiler_params=_params(("arbitrary",)),
    )(dproj, wt, x, dout, g_pre, pt)


def _presum(c_arr, own, recv, half_rows):
    n_cols = own.shape[-1]
    own4 = own.reshape(N_CHIPS, 2, half_rows, n_cols)

    def body(c_ref, own_ref, recv_ref, o_ref):
        o_ref[...] = (own_ref[...].astype(F32) + recv_ref[...].astype(F32)).astype(BF16)

    return pl.pallas_call(
        body, name="presum_%d" % half_rows,
        grid_spec=pltpu.PrefetchScalarGridSpec(
            num_scalar_prefetch=1, grid=(N_CHIPS,),
            in_specs=[pl.BlockSpec((None, None, half_rows, n_cols), lambda j, c: (j, c[0], 0, 0)),
                      pl.BlockSpec((None, half_rows, n_cols), lambda j, c: (j, 0, 0))],
            out_specs=pl.BlockSpec((None, half_rows, n_cols), lambda j, c: (j, 0, 0))),
        out_shape=jax.ShapeDtypeStruct((N_CHIPS, half_rows, n_cols), BF16),
        compiler_params=_params(("parallel",)),
    )(c_arr, own4, recv)


def _sibling_presum(c_arr, dw, half_rows, name):
    n_cols = dw.shape[-1]
    dw4 = dw.reshape(N_CHIPS, 2, half_rows, n_cols)

    def body(c_ref, own_ref, dw_ref, o_ref, r_ref, stage, send_sems, recv_sems, load_sem):
        j = pl.program_id(0)
        copies = _sibling_copies(dw_ref, r_ref, half_rows, send_sems, recv_sems)

        @pl.when(j == 0)
        def _():
            for cp in copies:
                cp.start()

        for q in range(N_CHIPS):
            pl.when(j == q)(copies[q].wait_recv)
        load = pltpu.make_async_copy(r_ref.at[j], stage, load_sem)
        load.start()
        load.wait()
        o_ref[...] = (own_ref[...].astype(F32) + stage[...].astype(F32)).astype(BF16)

        @pl.when(j == N_CHIPS - 1)
        def _():
            for cp in copies:
                cp.wait_send()

    out, _ = pl.pallas_call(
        body, name=name,
        grid_spec=pltpu.PrefetchScalarGridSpec(
            num_scalar_prefetch=1, grid=(N_CHIPS,),
            in_specs=[pl.BlockSpec((None, None, half_rows, n_cols), lambda j, c: (j, c[0], 0, 0)), HBM_SPEC],
            out_specs=[pl.BlockSpec((None, half_rows, n_cols), lambda j, c: (j, 0, 0)), HBM_SPEC],
            scratch_shapes=[pltpu.VMEM((half_rows, n_cols), BF16), pltpu.SemaphoreType.DMA((N_CHIPS,)),
                            pltpu.SemaphoreType.DMA((N_CHIPS,)), pltpu.SemaphoreType.DMA(())]),
        out_shape=[jax.ShapeDtypeStruct((N_CHIPS, half_rows, n_cols), BF16),
                   jax.ShapeDtypeStruct((N_CHIPS, half_rows, n_cols), BF16)],
        compiler_params=_params(("arbitrary",)),
    )(c_arr, dw4, dw)
    return out


def _sum_chips(c_arr, parts, name):
    _, rows, n_cols = parts.shape
    nt = 2
    tr = rows // nt

    def body(c_ref, p_ref, o_ref):
        o_ref[...] = ((p_ref[0].astype(F32) + p_ref[1].astype(F32)) + p_ref[2].astype(F32)) + p_ref[3].astype(F32)

    return pl.pallas_call(
        body, name=name,
        grid_spec=pltpu.PrefetchScalarGridSpec(
            num_scalar_prefetch=1, grid=(nt,),
            in_specs=[pl.BlockSpec((N_CHIPS, tr, n_cols), lambda i, c: (0, i, 0))],
            out_specs=pl.BlockSpec((tr, n_cols), lambda i, c: (c[0] * nt + i, 0))),
        out_shape=jax.ShapeDtypeStruct((2 * rows, n_cols), F32),
        compiler_params=_params(("parallel",)),
    )(c_arr, parts)


def _adamw_math(w, g, m, v):
    mn = ADAM_B1 * m + (1.0 - ADAM_B1) * g
    vn = ADAM_B2 * v + (1.0 - ADAM_B2) * (g * g)
    m_hat = mn / (1.0 - ADAM_B1 ** ADAM_STEP)
    v_hat = vn / (1.0 - ADAM_B2 ** ADAM_STEP)
    return -ADAM_LR * (m_hat / (jnp.sqrt(v_hat) + ADAM_EPS) + ADAM_WD * w), mn, vn


def _adamw(w, g, m, v, name):
    R, C = w.shape
    tr = next((t for t in (256, 192, 128) if R % t == 0), R)

    def body(w_ref, g_ref, m_ref, v_ref, go_ref, d_ref, mo_ref, vo_ref):
        gv = g_ref[...]
        go_ref[...] = gv
        d_ref[...], mo_ref[...], vo_ref[...] = _adamw_math(w_ref[...], gv, m_ref[...], v_ref[...])

    spec = pl.BlockSpec((tr, C), lambda i: (i, 0))
    shp = jax.ShapeDtypeStruct((R, C), F32)
    return pl.pallas_call(
        body, name=name, grid=(R // tr,), in_specs=[spec] * 4, out_specs=[spec] * 4, out_shape=[shp] * 4,
        compiler_params=_params(("parallel",)),
    )(w, g, m, v)


HBM_SPEC = pl.BlockSpec(memory_space=pltpu.HBM)
GATHER_LOCAL_CHUNKS = 4
GATHER_SEMS = [pltpu.SemaphoreType.DMA((7,)), pltpu.SemaphoreType.DMA((7,)),
               pltpu.SemaphoreType.DMA((GATHER_LOCAL_CHUNKS,))]
OWNER_SEMS = [pltpu.SemaphoreType.DMA((3,)), pltpu.SemaphoreType.DMA((3,)), pltpu.SemaphoreType.DMA(())]


def _mesh_pos():
    return lax.axis_index("x"), lax.axis_index("y"), lax.axis_index("c")


class _RowGather:
    def __init__(self, src_ref, full_ref, rows, send_sems, recv_sems, local_sems):
        self.src, self.full, self.rows = src_ref, full_ref, rows
        self.send, self.recv, self.local = send_sems, recv_sems, local_sems
        x, y, c = _mesh_pos()
        self.c, self.me, self.sibling = c, (x, y, c), (x, y, 1 - c)
        self.chips = [(1 - x, y), (x, 1 - y), (1 - x, 1 - y)]

    def _block(self, pos):
        px, py, pc = pos
        return self.full.at[pl.ds(pl.multiple_of((4 * px + 2 * py + pc) * self.rows, 16), self.rows), :]

    def _copy(self, k, blk, to):
        return pltpu.make_async_remote_copy(
            src_ref=self.src if blk is self.me else self._block(blk), dst_ref=self._block(blk),
            send_sem=self.send.at[k], recv_sem=self.recv.at[k], device_id=to, device_id_type=MESH)

    def _mine(self):
        return _place_locally(self.src, self._block(self.me), self.local, GATHER_LOCAL_CHUNKS)

    def _first(self):
        return [self._copy(0, self.me, self.sibling)] + [
            self._copy(1 + j, self.me, (*chip, self.c)) for j, chip in enumerate(self.chips)]

    def start(self):
        for cp in self._first() + self._mine():
            cp.start()

    def finish(self):
        passed = [self._copy(4 + j, (*chip, self.c), self.sibling) for j, chip in enumerate(self.chips)]
        for j, chip in enumerate(self.chips):
            self._copy(1 + j, (*chip, self.c), self.me).wait_recv()
            passed[j].start()
        self._copy(0, self.sibling, self.me).wait_recv()
        for j, chip in enumerate(self.chips):
            self._copy(4 + j, (*chip, 1 - self.c), self.me).wait_recv()
        for cp in self._first() + passed:
            cp.wait_send()
        for cp in self._mine():
            cp.wait()


class _OwnerExchange:
    def __init__(self, src_ref, dst_ref, send_sems, recv_sems, local_sem):
        self.src, self.dst, self.send, self.recv, self.local = src_ref, dst_ref, send_sems, recv_sems, local_sem
        x, y, c = _mesh_pos()
        self.c, self.my_chip = c, 2 * x + y
        self.peers = [(1 - x, y), (x, 1 - y), (1 - x, 1 - y)]

    def _copies(self):
        local = pltpu.make_async_copy(self.src.at[self.my_chip], self.dst.at[self.my_chip], self.local)
        remote = [pltpu.make_async_remote_copy(
            src_ref=self.src.at[2 * px + py], dst_ref=self.dst.at[self.my_chip],
            send_sem=self.send.at[k], recv_sem=self.recv.at[k], device_id=(px, py, self.c), device_id_type=MESH)
            for k, (px, py) in enumerate(self.peers)]
        return local, remote

    def start(self):
        local, remote = self._copies()
        local.start()
        for cp in remote:
            cp.start()

    def finish(self):
        local, remote = self._copies()
        for cp in remote:
            cp.wait_recv()
        for cp in remote:
            cp.wait_send()
        local.wait()


def _sibling_copies(dw_ref, r_ref, rows, send_sems, recv_sems):
    x, y, c = _mesh_pos()
    return [pltpu.make_async_remote_copy(
        src_ref=dw_ref.at[pl.ds(pl.multiple_of((2 * j + (1 - c)) * rows, 16), rows), :], dst_ref=r_ref.at[j],
        send_sem=send_sems.at[j], recv_sem=recv_sems.at[j], device_id=(x, y, 1 - c), device_id_type=MESH)
        for j in range(N_CHIPS)]


PAIR_CHUNKS = 4


def _pair_halves(gt, go):
    def body(gt_in, go_in, gt_ref, go_ref, send_sems, recv_sems):
        del gt_in, go_in
        x, y, c = _mesh_pos()
        copies = []
        for a, (ref, rows) in enumerate(((gt_ref, W_IN_ROWS), (go_ref, W_OUT_ROWS))):
            ch = rows // PAIR_CHUNKS
            for q in range(PAIR_CHUNKS):
                part = ref.at[pl.ds(pl.multiple_of(c * rows + q * ch, 8), ch), :]
                copies.append(pltpu.make_async_remote_copy(
                    src_ref=part, dst_ref=part, send_sem=send_sems.at[PAIR_CHUNKS * a + q],
                    recv_sem=recv_sems.at[PAIR_CHUNKS * a + q], device_id=(x, y, 1 - c), device_id_type=MESH))
        for cp in copies:
            cp.start()
        for cp in copies:
            cp.wait_recv()
        for cp in copies:
            cp.wait_send()

    return pl.pallas_call(
        body, name="pair_halves",
        in_specs=[HBM_SPEC, HBM_SPEC], out_specs=[HBM_SPEC, HBM_SPEC],
        out_shape=[jax.ShapeDtypeStruct(gt.shape, F32), jax.ShapeDtypeStruct(go.shape, F32)],
        input_output_aliases={0: 0, 1: 1},
        scratch_shapes=[pltpu.SemaphoreType.DMA((2 * PAIR_CHUNKS,)), pltpu.SemaphoreType.DMA((2 * PAIR_CHUNKS,))],
    )(gt, go)


VEC = (("g_pre", 2048), ("g_post", 2048), ("b_qkv", 1280), ("ln_v_g", 1024), ("ln_v_b", 1024), ("attn_sinks", 16))
VEC_ROWS = 8
LOSS_ROW = len(VEC)
MAT = (("w_spatial", NG * T), ("b_spatial", NG))
MAT_ROWS = sum(r for _, r in MAT)


class _SlotGather:
    def __init__(self, buf, send_sems, recv_sems):
        self.buf, self.send, self.recv = buf, send_sems, recv_sems
        x, y, c = _mesh_pos()
        self.c, self.me, self.sibling, self.my_id = c, (x, y, c), (x, y, 1 - c), 4 * x + 2 * y + c
        self.chips = [(1 - x, y), (x, 1 - y), (1 - x, 1 - y)]

    def _copy(self, k, blk, to):
        px, py, pc = blk
        slot = self.buf.at[4 * px + 2 * py + pc]
        return pltpu.make_async_remote_copy(
            src_ref=slot, dst_ref=slot, send_sem=self.send.at[k], recv_sem=self.recv.at[k],
            device_id=to, device_id_type=MESH)

    def _first(self):
        return [self._copy(0, self.me, self.sibling)] + [
            self._copy(1 + j, self.me, (*chip, self.c)) for j, chip in enumerate(self.chips)]

    def start(self):
        for cp in self._first():
            cp.start()

    def finish(self):
        passed = [self._copy(4 + j, (*chip, self.c), self.sibling) for j, chip in enumerate(self.chips)]
        for j, chip in enumerate(self.chips):
            self._copy(1 + j, (*chip, self.c), self.me).wait_recv()
            passed[j].start()
        self._copy(0, self.sibling, self.me).wait_recv()
        for j, chip in enumerate(self.chips):
            self._copy(4 + j, (*chip, 1 - self.c), self.me).wait_recv()
        for cp in self._first() + passed:
            cp.wait_send()

    def total(self):
        t = self.buf[0]
        for d in range(1, N_DEV):
            t = t + self.buf[d]
        return t


def _small_update(vec_grads, loss_part, mat_total, vec_state, mat_state):
    n_vec, n_mat = len(VEC), len(MAT)
    n_par = n_vec + n_mat
    n_in = n_vec + 2 + 3 * n_par

    def body(*refs):
        g_in, loss_in, tot_m = refs[:n_vec], refs[n_vec], refs[n_vec + 1]
        st_in = refs[n_vec + 2:n_in]
        outs, loss_out = refs[n_in:n_in + 4 * n_par], refs[n_in + 4 * n_par]
        vbuf, tot_v, send_sems, recv_sems = refs[n_in + 4 * n_par + 1:]
        gather = _SlotGather(vbuf, send_sems, recv_sems)
        vbuf[gather.my_id] = jnp.zeros((VEC_ROWS, D), F32)
        for r, (_, n) in enumerate(VEC):
            vbuf[gather.my_id, r:r + 1, 0:n] = g_in[r][...]
        vbuf[gather.my_id, LOSS_ROW:LOSS_ROW + 1, 0:128] = loss_in[...]
        gather.start()
        gather.finish()
        tot_v[...] = gather.total()
        loss_out[...] = tot_v[LOSS_ROW:LOSS_ROW + 1, 0:128]
        r0 = 0
        for q in range(n_par):
            if q < n_vec:
                g = tot_v[q:q + 1, 0:VEC[q][1]]
            else:
                rows = MAT[q - n_vec][1]
                g = tot_m[r0:r0 + rows, :]
                r0 += rows
            w, m, v = (st_in[3 * q + t][...] for t in range(3))
            outs[4 * q][...] = g
            outs[4 * q + 1][...], outs[4 * q + 2][...], outs[4 * q + 3][...] = _adamw_math(w, g, m, v)

    state = [a for wmv in list(vec_state) + list(mat_state) for a in wmv]
    vmem = pl.BlockSpec(memory_space=pltpu.VMEM)
    shapes = [g.shape for g in vec_grads] + [wmv[0].shape for wmv in mat_state]
    out_shape = [jax.ShapeDtypeStruct(s, F32) for s in shapes for _ in range(4)]
    out_shape.append(jax.ShapeDtypeStruct((1, 128), F32))
    res = pl.pallas_call(
        body, name="small_update",
        in_specs=[vmem] * n_in, out_specs=[vmem] * len(out_shape), out_shape=out_shape,
        scratch_shapes=[pltpu.VMEM((N_DEV, VEC_ROWS, D), F32), pltpu.VMEM((VEC_ROWS, D), F32),
                        pltpu.SemaphoreType.DMA((7,)), pltpu.SemaphoreType.DMA((7,))],
        compiler_params=pltpu.CompilerParams(vmem_limit_bytes=VMEM_LIMIT),
    )(*vec_grads, loss_part, mat_total, *state)
    return [res[4 * q:4 * q + 4] for q in range(n_par)], res[-1]


def kernel(x, positions, g_pre, w_in, b_qkv, ln_v_g, ln_v_b, w_spatial, b_spatial, attn_sinks, w_out, g_post, loss_target, m_g_pre, m_w_in, m_b_qkv, m_ln_v_g, m_ln_v_b, m_w_spatial, m_b_spatial, m_attn_sinks, m_w_out, m_g_post, v_g_pre, v_w_in, v_b_qkv, v_ln_v_g, v_ln_v_b, v_w_spatial, v_b_spatial, v_attn_sinks, v_w_out, v_g_post):
    S = x.shape[1]
    c = lax.axis_index("c")
    c_arr = jnp.reshape(c, (1,)).astype(jnp.int32)
    x2 = x[0]
    target = loss_target[0]
    pos = positions.reshape(S, 1)
    half = HD // 2
    inv_freq = ROPE_THETA ** (-jnp.arange(half, dtype=F32) * (2.0 / HD))
    invf = jnp.tile(inv_freq, 128 // half).reshape(1, 128)
    bias = jnp.concatenate([jnp.zeros((OFF_Q,), F32), b_qkv[0], jnp.zeros((D_IN - OFF_ZB,), F32)]).reshape(1, D_IN)
    b_s_col = b_spatial[0].reshape(NG, T, 1)
    sinks = jnp.repeat(attn_sinks[0], T).reshape(NQ * T, 1)

    chip = 2 * lax.axis_index("x") + lax.axis_index("y")
    wt_part = lax.dynamic_slice_in_dim(w_in[0].T.astype(BF16), c * W_IN_ROWS, W_IN_ROWS, axis=0)
    wo_part = lax.dynamic_slice_in_dim(w_out[0].astype(BF16), c * W_OUT_ROWS, W_OUT_ROWS, axis=0)
    sched = jnp.asarray(PROJ_SCHEDULE, jnp.int32)[chip]

    h, proj, wt, cos, sin = _prenorm_inproj(sched, x2, g_pre, bias, wt_part, pos, invf)
    ycat, wo = _mid_fwd(proj, cos, sin, ln_v_g, ln_v_b, w_spatial[0], b_s_col, sinks, wo_part)
    dy, dout, loss_part, dg_post = _outproj_loss(ycat, wo, x2, target, g_post)

    dwo = _matmul_tn(ycat, dy, 1024, "dw_out")
    dycat, ro = _dycat(dy, wo, dwo)
    po = _presum(c_arr, dwo, ro, W_OUT_ROWS)
    dproj, dln_g, dln_b, dws, dbs, dsink, dbqkv, bo = _mid_bwd(
        proj, dycat, cos, sin, ln_v_g, ln_v_b, w_spatial[0], b_s_col, sinks, po)
    dwt, mat_total = _matmul_tn(dproj, h, 768, "dw_in_t",
                                mat_grads=[dws.reshape(NG * T, T), dbs.reshape(NG, T)])
    pt = _sibling_presum(c_arr, dwt, W_IN_ROWS, "sibling_presum_in")
    grad_x, dg_pre, bt = _dh_prenorm_bwd(dproj, wt, x2, dout, g_pre, pt)
    gt, go = _pair_halves(_sum_chips(c_arr, bt, "sum_chips_in"), _sum_chips(c_arr, bo, "sum_chips_out"))

    g_w_in, d_w_in, nm_w_in, nv_w_in = (a.T for a in _adamw(w_in[0].T, gt, m_w_in[0].T, v_w_in[0].T, "adamw_w_in"))
    g_w_out, d_w_out, nm_w_out, nv_w_out = _adamw(w_out[0], go, m_w_out[0], v_w_out[0], "adamw_w_out")

    state = {"g_pre": (g_pre, m_g_pre, v_g_pre), "g_post": (g_post, m_g_post, v_g_post),
             "b_qkv": (b_qkv, m_b_qkv, v_b_qkv), "ln_v_g": (ln_v_g, m_ln_v_g, v_ln_v_g),
             "ln_v_b": (ln_v_b, m_ln_v_b, v_ln_v_b), "attn_sinks": (attn_sinks, m_attn_sinks, v_attn_sinks),
             "w_spatial": tuple(a.reshape(NG * T, T) for a in (w_spatial, m_w_spatial, v_w_spatial)),
             "b_spatial": tuple(a.reshape(NG, T) for a in (b_spatial, m_b_spatial, v_b_spatial))}
    grads = {"g_pre": dg_pre, "g_post": dg_post, "b_qkv": dbqkv, "ln_v_g": dln_g, "ln_v_b": dln_b,
             "attn_sinks": dsink[:, 0].reshape(1, NQ)}
    results, loss = _small_update([grads[n] for n, _ in VEC], loss_part, mat_total,
                                  [state[n] for n, _ in VEC], [state[n] for n, _ in MAT])
    small = {n: [a.reshape(w.shape) for a in res]
             for (n, _), res, w in zip(VEC + MAT, results, [state[n][0] for n, _ in VEC + MAT])}
    small["w_spatial"] = [a.reshape(w_spatial.shape) for a in small["w_spatial"]]
    small["b_spatial"] = [a.reshape(b_spatial.shape) for a in small["b_spatial"]]
    big = {"w_in": [a[None] for a in (g_w_in, d_w_in, nm_w_in, nv_w_in)],
           "w_out": [a[None] for a in (g_w_out, d_w_out, nm_w_out, nv_w_out)]}
    order = ("g_pre", "w_in", "b_qkv", "ln_v_g", "ln_v_b", "w_spatial", "b_spatial", "attn_sinks", "w_out", "g_post")
    leaves = {**small, **big}
    return (loss[0, 0], grad_x[None], *[leaves[n][t] for t in range(4) for n in order])
```

```python
import jax
import jax.numpy as jnp
from jax import lax
from jax.experimental import pallas as pl
from jax.experimental.pallas import tpu as pltpu

F32 = jnp.float32
BF16 = jnp.bfloat16
MESH = pl.DeviceIdType.MESH

D = 2048
DG = 1024
T = 128
NG = 8
HD = 64
NQ = 16
D_IN = 5376
OFF_U, OFF_V, OFF_ZA, OFF_Q, OFF_K, OFF_VA, OFF_ZB = 0, 1024, 2048, 3072, 4096, 4224, 4352
D_QKV = 1280
EPS = 1e-6
ROPE_THETA = 10000.0
N_CHIPS = 4
N_DEV = 8
W_IN_ROWS = D_IN // N_DEV
W_OUT_ROWS = D // N_DEV

ADAM_LR, ADAM_B1, ADAM_B2, ADAM_EPS, ADAM_WD, ADAM_STEP = 0.001, 0.9, 0.999, 1e-08, 0.01, 10

VMEM_LIMIT = 56 * 1024 * 1024


def _tile(n, pref):
    return pref if n % pref == 0 else n


def _params(sem=None, vmem=VMEM_LIMIT):
    return pltpu.CompilerParams(dimension_semantics=sem, vmem_limit_bytes=vmem)


def _sigmoid(z):
    return 1.0 / (1.0 + jnp.exp(-z))


def _row_mean(v):
    return jnp.mean(v, axis=-1, keepdims=True)


def _dot(a, b, dims):
    return lax.dot_general(a, b, (dims, ((), ())), preferred_element_type=F32)


NN = ((1,), (0,))
NT = ((1,), (1,))
TN = ((0,), (0,))


PROJ_TN = 768
N_PROJ_TILES = D_IN // PROJ_TN
PROJ_SCHEDULE = ((0, 2, 4, 1, 3, 6, 5, 5), (2, 0, 6, 1, 4, 3, 5, 4), (4, 6, 0, 5, 2, 1, 3, 4), (6, 4, 2, 5, 3, 0, 1, 5))


LOCAL_CHUNKS = 6


def _place_locally(src_ref, dst_rows_ref, sems, n_chunks=LOCAL_CHUNKS):
    ch = src_ref.shape[0] // n_chunks
    return [pltpu.make_async_copy(src_ref.at[pl.ds(q * ch, ch), :], dst_rows_ref.at[pl.ds(q * ch, ch), :], sems.at[q])
            for q in range(n_chunks)]


def _prenorm_inproj(sched, x, g, bias, wt_part, pos, invf):
    S = x.shape[0]
    tp, tm = _tile(S, 512), _tile(S, 1024)
    n_pre, ns = S // tp, S // tm
    n_steps = n_pre + N_PROJ_TILES * ns
    pos_of = lambda i: jnp.maximum(i - n_pre, 0) // ns
    row_of = lambda i: jnp.maximum(i - n_pre, 0) % ns

    def body(sched_ref, x_ref, g_ref, b_ref, wpart_ref, pos_ref, invf_ref, h_ref, proj_ref, wt_ref, cos_ref, sin_ref,
             h_all, w_tile, stage, send_sems, recv_sems, w_sems, local_sems):
        i = pl.program_id(0)
        x_, y_, c = _mesh_pos()
        me, sibling = (x_, y_, c), (x_, y_, 1 - c)
        chips = [(1 - x_, y_), (x_, 1 - y_), (1 - x_, 1 - y_)]

        def block(pos):
            px, py, pc = pos
            return wt_ref.at[pl.ds(pl.multiple_of((4 * px + 2 * py + pc) * W_IN_ROWS, 16), W_IN_ROWS), :]

        def copy(k, blk, to):
            return pltpu.make_async_remote_copy(
                src_ref=wpart_ref if blk is me else block(blk), dst_ref=block(blk),
                send_sem=send_sems.at[k], recv_sem=recv_sems.at[k], device_id=to, device_id_type=MESH)

        stage_in = pltpu.make_async_copy(wpart_ref, stage, local_sems.at[0])
        stage_out = pltpu.make_async_copy(stage, block(me), local_sems.at[1])

        relay = (jnp.where(c == 0, x_, 1 - x_), jnp.where(c == 0, 1 - y_, y_))
        relayed = (jnp.where(c == 0, 1 - x_, x_), jnp.where(c == 0, y_, 1 - y_))

        def own_sends():
            return [copy(0, me, sibling), copy(2, me, (*chips[1], c)), copy(1, me, (*chips[0], c))]

        def passed_on():
            return [copy(4, (*chips[0], c), sibling), copy(5, (*chips[1], c), sibling), copy(3, (*relayed, c), (*relay, c))]

        def neighbours_arrive():
            copy(1, (*chips[0], c), me).wait_recv()
            copy(2, (*chips[1], c), me).wait_recv()
            for cp in passed_on():
                cp.start()
            copy(5, (*chips[1], 1 - c), me).wait_recv()

        def diagonal_arrives():
            copy(3, (*chips[2], c), me).wait_recv()
            copy(6, (*chips[2], c), sibling).start()
            copy(6, (*chips[2], 1 - c), me).wait_recv()

        def tile_load(p):
            slot = p % 2
            rows = wt_ref.at[pl.ds(pl.multiple_of(sched_ref[p] * PROJ_TN, 16), PROJ_TN), :]
            return pltpu.make_async_copy(rows, w_tile.at[slot], w_sems.at[slot])

        def prepare(p):
            p = jnp.asarray(p, jnp.int32)

            @pl.when(p == 0)
            def _():
                copy(0, sibling, me).wait_recv()
                stage_out.wait()

            pl.when(p == 1)(neighbours_arrive)
            pl.when(p == 2)(lambda: copy(4, (*chips[0], 1 - c), me).wait_recv())
            pl.when(p == sched_ref[N_PROJ_TILES])(diagonal_arrives)
            tile_load(p).start()

        @pl.when(i == 0)
        def _():
            stage_in.start()
            for cp in own_sends():
                cp.start()
            stage_in.wait()
            stage_out.start()

        @pl.when(i < n_pre)
        def _():
            xv = x_ref[...]
            r = lax.rsqrt(_row_mean(xv * xv) + EPS)
            hv = (xv * r * g_ref[...]).astype(BF16)
            h_ref[...] = hv
            h_all[pl.ds(pl.multiple_of(i * tp, tp), tp), :] = hv
            ang = pos_ref[...].astype(F32) * invf_ref[...]
            cos_ref[...] = jnp.cos(ang)
            sin_ref[...] = jnp.sin(ang)

        pl.when(i == n_pre - 1)(lambda: prepare(0))

        @pl.when(i >= n_pre)
        def _():
            p, s = pos_of(i), row_of(i)
            pl.when(s == 0)(lambda: tile_load(p).wait())
            pl.when((s == ns - 1) & (p < N_PROJ_TILES - 1))(lambda: prepare(p + 1))
            hv = h_all[pl.ds(pl.multiple_of(s * tm, tm), tm), :]
            proj_ref[...] = _dot(hv, w_tile[p % 2], NT) + b_ref[...]

        @pl.when(i == n_steps - 1)
        def _():
            for cp in own_sends() + passed_on() + [copy(6, (*chips[2], c), sibling)]:
                cp.wait_send()

    return pl.pallas_call(
        body, name="prenorm_inproj",
        grid_spec=pltpu.PrefetchScalarGridSpec(
            num_scalar_prefetch=1, grid=(n_steps,),
            in_specs=[pl.BlockSpec((tp, D), lambda i, sc: (jnp.minimum(i, n_pre - 1), 0)),
                      pl.BlockSpec((1, D), lambda i, sc: (0, 0)),
                      pl.BlockSpec((1, PROJ_TN), lambda i, sc: (0, sc[pos_of(i)])),
                      HBM_SPEC,
                      pl.BlockSpec((tp, 1), lambda i, sc: (jnp.minimum(i, n_pre - 1), 0)),
                      pl.BlockSpec((1, 128), lambda i, sc: (0, 0))],
            out_specs=[pl.BlockSpec((tp, D), lambda i, sc: (jnp.minimum(i, n_pre - 1), 0)),
                       pl.BlockSpec((tm, PROJ_TN), lambda i, sc: (row_of(i), sc[pos_of(i)])),
                       HBM_SPEC,
                       pl.BlockSpec((tp, 128), lambda i, sc: (jnp.minimum(i, n_pre - 1), 0)),
                       pl.BlockSpec((tp, 128), lambda i, sc: (jnp.minimum(i, n_pre - 1), 0))],
            scratch_shapes=[pltpu.VMEM((S, D), BF16), pltpu.VMEM((2, PROJ_TN, D), BF16),
                            pltpu.VMEM((W_IN_ROWS, D), BF16),
                            pltpu.SemaphoreType.DMA((7,)), pltpu.SemaphoreType.DMA((7,)),
                            pltpu.SemaphoreType.DMA((2,)), pltpu.SemaphoreType.DMA((2,))]),
        out_shape=[jax.ShapeDtypeStruct((S, D), BF16), jax.ShapeDtypeStruct((S, D_IN), F32),
                   jax.ShapeDtypeStruct((D_IN, D), BF16),
                   jax.ShapeDtypeStruct((S, 128), F32), jax.ShapeDtypeStruct((S, 128), F32)],
        compiler_params=_params(("arbitrary",)),
    )(sched, x, g, bias, wt_part, pos, invf)


def _rot_half(xs, first_half):
    return jnp.where(first_half, -pltpu.roll(xs, 96, 1), pltpu.roll(xs, 32, 1))


GH = NQ // 2


MASKED = -1e30


def _attn_consts():
    lane = lax.broadcasted_iota(jnp.int32, (T, 128), 1)
    row = lax.broadcasted_iota(jnp.int32, (GH * T, T), 0) & (T - 1)
    on_diag_or_below = row >= lax.broadcasted_iota(jnp.int32, (GH * T, T), 1)
    return (lane & (HD - 1)) < (HD // 2), lane < HD, on_diag_or_below


def _stack_heads(slab_fn, grp, lo64):
    blocks = []
    for jj in range(GH // 2):
        s = slab_fn(GH // 2 * grp + jj)
        blocks += [jnp.where(lo64, s, 0.0), jnp.where(lo64, 0.0, s)]
    return jnp.concatenate(blocks, axis=0)


def _unstack_heads(stacked, jj, lo64):
    return jnp.where(lo64, stacked[(2 * jj) * T:(2 * jj + 1) * T], stacked[(2 * jj + 1) * T:(2 * jj + 2) * T])


def _both_halves(kv, grp):
    lo = lax.broadcasted_iota(jnp.int32, kv.shape, 1) < HD
    swapped = pltpu.roll(kv, HD, 1)
    return jnp.where(lo, kv, swapped) if grp == 0 else jnp.where(lo, swapped, kv)


def _one_half(acc, grp):
    lo = lax.broadcasted_iota(jnp.int32, acc.shape, 1) < HD
    return jnp.where(lo if grp == 0 else jnp.logical_not(lo), acc + pltpu.roll(acc, HD, 1), 0.0)


def _layer_norm_stats(v):
    mu = _row_mean(v)
    xc = v - mu
    var = _row_mean(xc * xc)
    rs = lax.rsqrt(var + EPS)
    return xc * rs, rs


def _fold(both, own):
    return jnp.where(own, both[:, T:2 * T], both[:, 0:T])


def _unfold(p, own):
    return jnp.concatenate([jnp.where(own, 0.0, p), jnp.where(own, p, 0.0)], axis=1).astype(BF16)


def _band_softmax(q_scaled, k_both, own, has_prev, sink):
    s_both = _dot(q_scaled, k_both, NT)
    s_prev = s_both[:, 0:T]
    if has_prev is not None:
        s_prev = s_prev + jnp.where(has_prev, 0.0, MASKED)
    s = jnp.where(own, s_both[:, T:2 * T], s_prev)
    m = jnp.maximum(jnp.max(s, axis=-1, keepdims=True), sink)
    e = jnp.exp(s - m)
    es = jnp.exp(sink - m)
    inv = 1.0 / (jnp.sum(e, axis=-1, keepdims=True) + es)
    return e * inv, es * inv


BPS_FWD = 4
BPS_BWD = 4


def _mid_specs(nt, rev, bps):
    tile = (lambda i: nt - 1 - i) if rev else (lambda i: i)
    prev = lambda i: jnp.maximum(bps * tile(i) - 1, 0)
    rows = bps * T
    return tile, [
        pl.BlockSpec((rows, D_IN), lambda i: (tile(i), 0)),
        pl.BlockSpec((T, 2 * T), lambda i: (prev(i), OFF_K // (2 * T))),
    ], [
        pl.BlockSpec((rows, 128), lambda i: (tile(i), 0)),
        pl.BlockSpec((rows, 128), lambda i: (tile(i), 0)),
        pl.BlockSpec((T, 128), lambda i: (prev(i), 0)),
        pl.BlockSpec((T, 128), lambda i: (prev(i), 0)),
        pl.BlockSpec((1, DG), lambda i: (0, 0)),
        pl.BlockSpec((1, DG), lambda i: (0, 0)),
        pl.BlockSpec((NG, T, T), lambda i: (0, 0, 0)),
        pl.BlockSpec((NG, T, 1), lambda i: (0, 0, 0)),
        pl.BlockSpec((NQ * T, 1), lambda i: (0, 0)),
    ]


def _rope(xs, cosv, sinv, first_half):
    return xs * cosv + _rot_half(xs, first_half) * sinv


def _mid_fwd(proj, cos, sin, ln_g, ln_b, w_s, b_s, sinks, wo_part):
    S = proj.shape[0]
    BPS = BPS_FWD if S % (BPS_FWD * T) == 0 else 1
    nt = S // (BPS * T)
    tile, proj_specs, par_specs = _mid_specs(nt, False, BPS)

    def body(p_ref, kvp_ref, cos_ref, sin_ref, cosp_ref, sinp_ref, lng_ref, lnb_ref, ws_ref, bs_ref, sink_ref, wpart_ref,
             y_ref, wo_ref, q_ref, wm_ref, send_sems, recv_sems, local_sems):
        i = pl.program_id(0)
        first_half, lo64, own = _attn_consts()
        gather = _RowGather(wpart_ref, wo_ref, W_OUT_ROWS, send_sems, recv_sems, local_sems)
        pl.when(i == 0)(gather.start)

        @pl.when(i == 0)
        def _():
            tril = lax.broadcasted_iota(jnp.int32, (T, T), 0) >= lax.broadcasted_iota(jnp.int32, (T, T), 1)
            for g in range(NG):
                wm_ref[g] = jnp.where(tril, ws_ref[g], 0.0).astype(BF16)

        def block(b, k_prev, v_prev, has_prev):
            rows = slice(b * T, (b + 1) * T)
            xhat, _ = _layer_norm_stats(p_ref[rows, OFF_V:OFF_V + DG])
            vn = xhat * lng_ref[...] + lnb_ref[...]
            for g in range(NG):
                sl = slice(128 * g, 128 * g + 128)
                mixed = _dot(wm_ref[g], vn[:, sl].astype(BF16), NN) + bs_ref[g]
                z = p_ref[rows, OFF_ZA + 128 * g:OFF_ZA + 128 * g + 128]
                u = p_ref[rows, OFF_U + 128 * g:OFF_U + 128 * g + 128]
                y_ref[rows, sl] = (u * mixed * (z * _sigmoid(z))).astype(BF16)

            cosv, sinv = cos_ref[rows, :], sin_ref[rows, :]
            k_cur = _rope(p_ref[rows, OFF_K:OFF_K + 128], cosv, sinv, first_half)
            v_cur = p_ref[rows, OFF_VA:OFF_VA + 128]

            def q_slab(j):
                q = (_rope(p_ref[rows, OFF_Q + 128 * j:OFF_Q + 128 * j + 128], cosv, sinv, first_half)
                     * (HD ** -0.5)).astype(BF16)
                q_ref[rows, 128 * j:128 * j + 128] = q
                return q

            k_both = jnp.concatenate([k_prev, k_cur], axis=0)
            v_both = jnp.concatenate([v_prev, v_cur], axis=0)
            for grp in range(2):
                qs = _stack_heads(q_slab, grp, lo64)
                kg, vg = _both_halves(k_both, grp).astype(BF16), _both_halves(v_both, grp).astype(BF16)
                p, _ = _band_softmax(qs, kg, own, has_prev, sink_ref[GH * T * grp:GH * T * (grp + 1), :])
                o_st = _dot(_unfold(p, own), vg, NN)
                for jj in range(GH // 2):
                    c0 = 128 * (GH // 2 * grp + jj)
                    zb = p_ref[rows, OFF_ZB + c0:OFF_ZB + c0 + 128]
                    o = _unstack_heads(o_st, jj, lo64)
                    y_ref[rows, DG + c0:DG + c0 + 128] = (o * (zb * _sigmoid(zb))).astype(BF16)
            return k_cur, v_cur

        k_prev = _rope(kvp_ref[:, 0:128], cosp_ref[...], sinp_ref[...], first_half)
        kv = block(0, k_prev, kvp_ref[:, 128:256], i > 0)
        for b in range(1, BPS):
            kv = block(b, *kv, None)
        pl.when(i == nt - 1)(gather.finish)

    return pl.pallas_call(
        body, name="mid_fwd", grid=(nt,),
        in_specs=proj_specs + par_specs + [HBM_SPEC],
        out_specs=[pl.BlockSpec((BPS * T, D), lambda i: (tile(i), 0)), HBM_SPEC,
                   pl.BlockSpec((BPS * T, NQ * HD), lambda i: (tile(i), 0))],
        out_shape=[jax.ShapeDtypeStruct((S, D), BF16), jax.ShapeDtypeStruct((D, D), BF16),
                   jax.ShapeDtypeStruct((S, NQ * HD), BF16)],
        scratch_shapes=[pltpu.VMEM((NG, T, T), BF16)] + GATHER_SEMS,
        compiler_params=_params(("arbitrary",)),
    )(proj, proj, cos, sin, cos, sin, ln_g, ln_b, w_s, b_s, sinks, wo_part)


def _mid_bwd(proj, dycat, q_roped, cos, sin, ln_g, ln_b, w_s, b_s, sinks, po):
    S = proj.shape[0]
    BPS = BPS_BWD if S % (BPS_BWD * T) == 0 else 1
    nt = S // (BPS * T)
    tile, proj_specs, par_specs = _mid_specs(nt, True, BPS)
    const2 = lambda i: (0, 0)

    def body(p_ref, kvp_ref, dyc_ref, q_ref, cos_ref, sin_ref, cosp_ref, sinp_ref, lng_ref, lnb_ref, ws_ref, bs_ref,
             sink_ref, po_ref, dp_ref, dlng_ref, dlnb_ref, dws_ref, dbs_ref, dsink_ref, dbqkv_ref, bo_ref,
             carry_ref, dvn_ref, wm_ref, wmt_ref, send_sems, recv_sems, local_sem):
        i = pl.program_id(0)
        first_half, lo64, own = _attn_consts()
        tril = lax.broadcasted_iota(jnp.int32, (T, T), 0) >= lax.broadcasted_iota(jnp.int32, (T, T), 1)
        exchange = _OwnerExchange(po_ref, bo_ref, send_sems, recv_sems, local_sem)
        pl.when(i == 0)(exchange.start)

        @pl.when(i == 0)
        def _():
            for g in range(NG):
                wm = jnp.where(tril, ws_ref[g], 0.0)
                wm_ref[g] = wm.astype(BF16)
                wmt_ref[g] = wm.T.astype(BF16)
            dlng_ref[...] = jnp.zeros_like(dlng_ref)
            dlnb_ref[...] = jnp.zeros_like(dlnb_ref)
            dws_ref[...] = jnp.zeros_like(dws_ref)
            dbs_ref[...] = jnp.zeros_like(dbs_ref)
            dsink_ref[...] = jnp.zeros_like(dsink_ref)
            dbqkv_ref[...] = jnp.zeros_like(dbqkv_ref)
            carry_ref[...] = jnp.zeros_like(carry_ref)

        def roped_k(b):
            rows = slice(b * T, (b + 1) * T)
            return _rope(p_ref[rows, OFF_K:OFF_K + 128], cos_ref[rows, :], sin_ref[rows, :], first_half)

        def block(b, k_prev, v_prev, has_prev, dk_next, dv_next):
            rows = slice(b * T, (b + 1) * T)
            xhat, rs = _layer_norm_stats(p_ref[rows, OFF_V:OFF_V + DG])
            lng = lng_ref[...]
            vn = xhat * lng + lnb_ref[...]
            for g in range(NG):
                sl = slice(128 * g, 128 * g + 128)
                vng = vn[:, sl].astype(BF16)
                mixed = _dot(wm_ref[g], vng, NN) + bs_ref[g]
                z = p_ref[rows, OFF_ZA + 128 * g:OFF_ZA + 128 * g + 128]
                u = p_ref[rows, OFF_U + 128 * g:OFF_U + 128 * g + 128]
                dy = dyc_ref[rows, sl]
                sg = _sigmoid(z)
                sa = z * sg
                dp_ref[rows, OFF_U + 128 * g:OFF_U + 128 * g + 128] = (dy * mixed * sa).astype(BF16)
                dp_ref[rows, OFF_ZA + 128 * g:OFF_ZA + 128 * g + 128] = (
                    dy * u * mixed * (sg * (1.0 + z * (1.0 - sg)))).astype(BF16)
                dm = dy * u * sa
                dmb = dm.astype(BF16)
                dvn_ref[rows, sl] = _dot(wmt_ref[g], dmb, NN)
                dws_ref[g] += jnp.where(tril, _dot(dmb, vng, NT), 0.0)
                dbs_ref[g] += jnp.sum(dm, axis=1, keepdims=True)
            dvn = dvn_ref[rows, :]
            dlng_ref[...] += jnp.sum(dvn * xhat, axis=0, keepdims=True)
            dlnb_ref[...] += jnp.sum(dvn, axis=0, keepdims=True)
            dxh = dvn * lng
            dv_g = rs * (dxh - _row_mean(dxh)
                         - xhat * _row_mean(dxh * xhat))
            dp_ref[rows, OFF_V:OFF_V + DG] = dv_g.astype(BF16)

            cosv, sinv = cos_ref[rows, :], sin_ref[rows, :]
            k_cur = roped_k(b)
            v_cur = p_ref[rows, OFF_VA:OFF_VA + 128]

            def q_slab(j):
                return q_ref[rows, 128 * j:128 * j + 128]

            def do_slab(j):
                zb = p_ref[rows, OFF_ZB + 128 * j:OFF_ZB + 128 * j + 128]
                return dyc_ref[rows, DG + 128 * j:DG + 128 * j + 128] * (zb * _sigmoid(zb))

            k_both = jnp.concatenate([k_prev, k_cur], axis=0)
            v_both = jnp.concatenate([v_prev, v_cur], axis=0)
            dk_both, dv_both = jnp.zeros((2 * T, 128), F32), jnp.zeros((2 * T, 128), F32)
            for grp in range(2):
                qs = _stack_heads(q_slab, grp, lo64)
                d_o = _stack_heads(do_slab, grp, lo64)
                dob = d_o.astype(BF16)
                kg, vg = _both_halves(k_both, grp).astype(BF16), _both_halves(v_both, grp).astype(BF16)
                p, ps = _band_softmax(qs, kg, own, has_prev, sink_ref[GH * T * grp:GH * T * (grp + 1), :])
                p_both = _unfold(p, own)
                o_st = _dot(p_both, vg, NN)
                delta = jnp.sum(d_o * o_st, axis=-1, keepdims=True)
                ds_both = _unfold(p * (_fold(_dot(dob, vg, NT), own) - delta), own)
                dq_st = _dot(ds_both, kg, NN) * (HD ** -0.5)
                dk_both = dk_both + _one_half(_dot(ds_both, qs, TN), grp)
                dv_both = dv_both + _one_half(_dot(p_both, dob, TN), grp)
                dsink_rows = ps * delta
                for hh in range(GH):
                    h = GH * grp + hh
                    dsink_ref[h:h + 1, :] += jnp.broadcast_to(
                        -jnp.sum(dsink_rows[hh * T:(hh + 1) * T], axis=0, keepdims=True), (1, 128))
                for jj in range(GH // 2):
                    c0 = 128 * (GH // 2 * grp + jj)
                    zb = p_ref[rows, OFF_ZB + c0:OFF_ZB + c0 + 128]
                    sg = _sigmoid(zb)
                    o = _unstack_heads(o_st, jj, lo64)
                    dp_ref[rows, OFF_ZB + c0:OFF_ZB + c0 + 128] = (
                        dyc_ref[rows, DG + c0:DG + c0 + 128] * o * (sg * (1.0 + zb * (1.0 - sg)))).astype(BF16)
                    dq = _unstack_heads(dq_st, jj, lo64)
                    dq_pre = dq * cosv - _rot_half(dq, first_half) * sinv
                    dp_ref[rows, OFF_Q + c0:OFF_Q + c0 + 128] = dq_pre.astype(BF16)
                    dbqkv_ref[:, c0:c0 + 128] += jnp.sum(dq_pre, axis=0, keepdims=True)
            dk_prev, dv_prev = dk_both[0:T], dv_both[0:T]
            dk_cur, dv_cur = dk_both[T:2 * T] + dk_next, dv_both[T:2 * T] + dv_next
            dk_pre = dk_cur * cosv - _rot_half(dk_cur, first_half) * sinv
            dp_ref[rows, OFF_K:OFF_K + 128] = dk_pre.astype(BF16)
            dp_ref[rows, OFF_VA:OFF_VA + 128] = dv_cur.astype(BF16)
            dbqkv_ref[:, 1024:1152] += jnp.sum(dk_pre, axis=0, keepdims=True)
            dbqkv_ref[:, 1152:1280] += jnp.sum(dv_cur, axis=0, keepdims=True)
            return dk_prev, dv_prev

        grads = carry_ref[:, 0:128], carry_ref[:, 128:256]
        for b in range(BPS - 1, 0, -1):
            prows = slice((b - 1) * T, b * T)
            grads = block(b, roped_k(b - 1), p_ref[prows, OFF_VA:OFF_VA + 128], None, *grads)
        k_prev = _rope(kvp_ref[:, 0:128], cosp_ref[...], sinp_ref[...], first_half)
        grads = block(0, k_prev, kvp_ref[:, 128:256], i < nt - 1, *grads)
        carry_ref[:, 0:128], carry_ref[:, 128:256] = grads
        pl.when(i == nt - 1)(exchange.finish)

    return pl.pallas_call(
        body, name="mid_bwd", grid=(nt,),
        in_specs=proj_specs + [pl.BlockSpec((BPS * T, D), lambda i: (tile(i), 0)),
                               pl.BlockSpec((BPS * T, NQ * HD), lambda i: (tile(i), 0))] + par_specs + [HBM_SPEC],
        out_specs=[pl.BlockSpec((BPS * T, D_IN), lambda i: (tile(i), 0)),
                   pl.BlockSpec((1, DG), const2), pl.BlockSpec((1, DG), const2),
                   pl.BlockSpec((NG, T, T), lambda i: (0, 0, 0)), pl.BlockSpec((NG, T, 1), lambda i: (0, 0, 0)),
                   pl.BlockSpec((NQ, 128), const2), pl.BlockSpec((1, D_QKV), const2), HBM_SPEC],
        out_shape=[jax.ShapeDtypeStruct((S, D_IN), BF16),
                   jax.ShapeDtypeStruct((1, DG), F32), jax.ShapeDtypeStruct((1, DG), F32),
                   jax.ShapeDtypeStruct((NG, T, T), F32), jax.ShapeDtypeStruct((NG, T, 1), F32),
                   jax.ShapeDtypeStruct((NQ, 128), F32), jax.ShapeDtypeStruct((1, D_QKV), F32),
                   jax.ShapeDtypeStruct(po.shape, BF16)],
        scratch_shapes=[pltpu.VMEM((T, 2 * T), F32), pltpu.VMEM((BPS * T, DG), F32),
                        pltpu.VMEM((NG, T, T), BF16), pltpu.VMEM((NG, T, T), BF16)] + OWNER_SEMS,
        compiler_params=_params(("arbitrary",)),
    )(proj, proj, dycat, q_roped, cos, sin, cos, sin, ln_g, ln_b, w_s, b_s, sinks, po)


def _outproj_loss(ycat, wo, x, target, g_post):
    S = ycat.shape[0]
    tm = _tile(S, 512)
    nt = S // tm
    n_part = 2 if tm % 32 == 0 else 1
    tp = tm // n_part
    const2 = lambda i: (0, 0)

    def body(yc_ref, w_ref, x_ref, t_ref, g_ref, dy_ref, dout_ref, loss_ref, dg_ref, lacc_ref):
        i = pl.program_id(0)

        @pl.when(i == 0)
        def _():
            dg_ref[...] = jnp.zeros_like(dg_ref)
            lacc_ref[...] = jnp.zeros_like(lacc_ref)

        g = g_ref[...]
        ys = [_dot(yc_ref[q * tp:(q + 1) * tp, :], w_ref[...], NN) for q in range(n_part)]
        for q, y in enumerate(ys):
            rows = slice(q * tp, (q + 1) * tp)
            r = lax.rsqrt(_row_mean(y * y) + EPS)
            yh = y * r
            diff = x_ref[rows, :] + yh * g - t_ref[rows, :]
            lacc_ref[...] += jnp.sum(diff * diff, axis=0, keepdims=True)
            dout = diff * (1.0 / D)
            dout_ref[rows, :] = dout
            dg_ref[...] += jnp.sum(dout * yh, axis=0, keepdims=True)
            dyh = dout * g
            dy_ref[rows, :] = (r * (dyh - yh * _row_mean(dyh * yh))).astype(BF16)

        @pl.when(i == nt - 1)
        def _():
            loss_ref[...] = jnp.broadcast_to(jnp.sum(lacc_ref[...], axis=1, keepdims=True) * (0.5 / D), (1, 128))

    row = lambda i: (i, 0)
    return pl.pallas_call(
        body, name="outproj_loss", grid=(nt,),
        in_specs=[pl.BlockSpec((tm, D), row), pl.BlockSpec((D, D), const2, pipeline_mode=pl.Buffered(1)),
                  pl.BlockSpec((tm, D), row), pl.BlockSpec((tm, D), row), pl.BlockSpec((1, D), const2)],
        out_specs=[pl.BlockSpec((tm, D), row), pl.BlockSpec((tm, D), row), pl.BlockSpec((1, 128), const2),
                   pl.BlockSpec((1, D), const2)],
        out_shape=[jax.ShapeDtypeStruct((S, D), BF16), jax.ShapeDtypeStruct((S, D), F32),
                   jax.ShapeDtypeStruct((1, 128), F32), jax.ShapeDtypeStruct((1, D), F32)],
        scratch_shapes=[pltpu.VMEM((1, D), F32)],
        compiler_params=_params(("arbitrary",)),
    )(ycat, wo, x, target, g_post)


def _dycat(dy, wo, dwo):
    S = dy.shape[0]
    tm = _tile(S, 512)
    nt = S // tm

    def body(dy_ref, w_ref, dwo_ref, o_ref, r_ref, send_sems, recv_sems):
        i = pl.program_id(0)
        copies = _sibling_copies(dwo_ref, r_ref, W_OUT_ROWS, send_sems, recv_sems)

        @pl.when(i == 0)
        def _():
            for cp in copies:
                cp.start()

        o_ref[...] = _dot(dy_ref[...], w_ref[...], NT)

        @pl.when(i == nt - 1)
        def _():
            for cp in copies:
                cp.wait_recv()
            for cp in copies:
                cp.wait_send()

    return pl.pallas_call(
        body, name="dycat", grid=(nt,),
        in_specs=[pl.BlockSpec((tm, D), lambda i: (i, 0)),
                  pl.BlockSpec((D, D), lambda i: (0, 0), pipeline_mode=pl.Buffered(1)), HBM_SPEC],
        out_specs=[pl.BlockSpec((tm, D), lambda i: (i, 0)), HBM_SPEC],
        out_shape=[jax.ShapeDtypeStruct((S, D), F32), jax.ShapeDtypeStruct((N_CHIPS, W_OUT_ROWS, D), BF16)],
        scratch_shapes=[pltpu.SemaphoreType.DMA((N_CHIPS,)), pltpu.SemaphoreType.DMA((N_CHIPS,))],
        compiler_params=_params(("arbitrary",)),
    )(dy, wo, dwo)


def _matmul_tn(a, b, tm, name, mat_grads=None):
    K, M = a.shape
    N = b.shape[1]
    tk = _tile(K, 1024)
    ni, nk = M // tm, K // tk
    hosting = mat_grads is not None

    def body(a_ref, b_ref, *rest):
        if hosting:
            g_in, (o_ref, tot_ref, acc_ref, mbuf, send_sems, recv_sems) = rest[:len(MAT)], rest[len(MAT):]
        else:
            o_ref, acc_ref = rest
        i, k = pl.program_id(0), pl.program_id(1)

        if hosting:
            gather = _SlotGather(mbuf, send_sems, recv_sems)

            @pl.when((i == 0) & (k == 0))
            def _():
                r0 = 0
                for q, (_, rows) in enumerate(MAT):
                    mbuf[gather.my_id, r0:r0 + rows, :] = g_in[q][...]
                    r0 += rows
                gather.start()

        @pl.when(k == 0)
        def _():
            acc_ref[...] = jnp.zeros_like(acc_ref)

        acc_ref[...] += _dot(a_ref[...], b_ref[...], TN)

        @pl.when(k == nk - 1)
        def _():
            o_ref[...] = acc_ref[...].astype(BF16)

        if hosting:
            @pl.when((i == ni - 1) & (k == nk - 1))
            def _():
                gather.finish()
                tot_ref[...] = gather.total()

    whole = pl.BlockSpec(memory_space=pltpu.VMEM)
    in_specs = [pl.BlockSpec((tk, tm), lambda i, k: (k, i)), pl.BlockSpec((tk, N), lambda i, k: (k, 0))]
    out_specs = [pl.BlockSpec((tm, N), lambda i, k: (i, 0))]
    out_shape = [jax.ShapeDtypeStruct((M, N), BF16)]
    scratch = [pltpu.VMEM((tm, N), F32)]
    if hosting:
        in_specs += [whole] * len(MAT)
        out_specs.append(whole)
        out_shape.append(jax.ShapeDtypeStruct((MAT_ROWS, 128), F32))
        scratch += [pltpu.VMEM((N_DEV, MAT_ROWS, 128), F32), pltpu.SemaphoreType.DMA((7,)),
                    pltpu.SemaphoreType.DMA((7,))]
    res = pl.pallas_call(
        body, name=name, grid=(ni, nk), in_specs=in_specs, out_specs=out_specs, out_shape=out_shape,
        scratch_shapes=scratch,
        compiler_params=_params(("arbitrary", "arbitrary") if hosting else ("parallel", "arbitrary")),
    )(a, b, *(mat_grads or ()))
    return res if hosting else res[0]


def _dh_prenorm_bwd(dproj, wt, x, dout, g_pre, pt):
    S = x.shape[0]
    tm, tk = _tile(S, 1024), 768
    ne = 4
    te = tm // ne
    ni, nk = S // tm, D_IN // tk
    n_mm = ni * nk
    n_steps = n_mm + ne
    mm_of = lambda t: jnp.minimum(t, n_mm - 1)

    def quarter_of(t):
        j, q = t // nk - 1, t % nk
        nxt = jnp.where(q < ne, j * ne + q, (j + 1) * ne)
        return jnp.clip(nxt, 0, ni * ne - 1)

    def body(dp_ref, w_ref, x_ref, dout_ref, g_ref, pt_ref, gx_ref, dg_ref, bt_ref, acc_ref,
             send_sems, recv_sems, local_sem):
        t = pl.program_id(0)
        exchange = _OwnerExchange(pt_ref, bt_ref, send_sems, recv_sems, local_sem)
        pl.when(t == 0)(exchange.start)

        @pl.when(t == 0)
        def _():
            dg_ref[...] = jnp.zeros_like(dg_ref)

        has_mm = t < n_mm
        has_q = (t >= nk) & (t % nk < ne)

        def matmul():
            slot = (mm_of(t) // nk) % 2
            old = jnp.where(mm_of(t) % nk > 0, acc_ref[slot], 0.0)
            acc_ref[slot] = old + _dot(dp_ref[...], w_ref[...], NN)

        def quarter():
            slot = (t // nk - 1) % 2
            dh = acc_ref[slot, pl.ds(pl.multiple_of((t % nk) * te, te), te), :]
            xv = x_ref[...]
            r = lax.rsqrt(_row_mean(xv * xv) + EPS)
            xh = xv * r
            dg_ref[...] += jnp.sum(dh * xh, axis=0, keepdims=True)
            dxh = dh * g_ref[...]
            gx_ref[...] = dout_ref[...] + r * (dxh - xh * _row_mean(dxh * xh))

        @pl.when(has_mm & has_q)
        def _():
            matmul()
            quarter()

        pl.when(has_mm & jnp.logical_not(has_q))(matmul)
        pl.when(jnp.logical_not(has_mm) & has_q)(quarter)
        pl.when(t == n_steps - 1)(exchange.finish)

    quarter_block = lambda t: (quarter_of(t), 0)
    return pl.pallas_call(
        body, name="dh_prenorm_bwd", grid=(n_steps,),
        in_specs=[pl.BlockSpec((tm, tk), lambda t: (mm_of(t) // nk, mm_of(t) % nk)),
                  pl.BlockSpec((tk, D), lambda t: (mm_of(t) % nk, 0)),
                  pl.BlockSpec((te, D), quarter_block), pl.BlockSpec((te, D), quarter_block),
                  pl.BlockSpec((1, D), lambda t: (0, 0)), HBM_SPEC],
        out_specs=[pl.BlockSpec((te, D), quarter_block), pl.BlockSpec((1, D), lambda t: (0, 0)), HBM_SPEC],
        out_shape=[jax.ShapeDtypeStruct((S, D), F32), jax.ShapeDtypeStruct((1, D), F32),
                   jax.ShapeDtypeStruct(pt.shape, BF16)],
        scratch_shapes=[pltpu.VMEM((2, tm, D), F32)] + OWNER_SEMS,
        compiler_params=_params(("arbitrary",)),
    )(dproj, wt, x, dout, g_pre, pt)


def _presum(c_arr, own, recv, half_rows):
    n_cols = own.shape[-1]
    own4 = own.reshape(N_CHIPS, 2, half_rows, n_cols)

    def body(c_ref, own_ref, recv_ref, o_ref):
        o_ref[...] = (own_ref[...].astype(F32) + recv_ref[...].astype(F32)).astype(BF16)

    return pl.pallas_call(
        body, name="presum_%d" % half_rows,
        grid_spec=pltpu.PrefetchScalarGridSpec(
            num_scalar_prefetch=1, grid=(N_CHIPS,),
            in_specs=[pl.BlockSpec((None, None, half_rows, n_cols), lambda j, c: (j, c[0], 0, 0)),
                      pl.BlockSpec((None, half_rows, n_cols), lambda j, c: (j, 0, 0))],
            out_specs=pl.BlockSpec((None, half_rows, n_cols), lambda j, c: (j, 0, 0))),
        out_shape=jax.ShapeDtypeStruct((N_CHIPS, half_rows, n_cols), BF16),
        compiler_params=_params(("parallel",)),
    )(c_arr, own4, recv)


def _sibling_presum(c_arr, dw, half_rows, name):
    n_cols = dw.shape[-1]
    dw4 = dw.reshape(N_CHIPS, 2, half_rows, n_cols)

    def body(c_ref, own_ref, dw_ref, o_ref, r_ref, stage, send_sems, recv_sems, load_sem):
        j = pl.program_id(0)
        copies = _sibling_copies(dw_ref, r_ref, half_rows, send_sems, recv_sems)

        @pl.when(j == 0)
        def _():
            for cp in copies:
                cp.start()

        for q in range(N_CHIPS):
            pl.when(j == q)(copies[q].wait_recv)
        load = pltpu.make_async_copy(r_ref.at[j], stage, load_sem)
        load.start()
        load.wait()
        o_ref[...] = (own_ref[...].astype(F32) + stage[...].astype(F32)).astype(BF16)

        @pl.when(j == N_CHIPS - 1)
        def _():
            for cp in copies:
                cp.wait_send()

    out, _ = pl.pallas_call(
        body, name=name,
        grid_spec=pltpu.PrefetchScalarGridSpec(
            num_scalar_prefetch=1, grid=(N_CHIPS,),
            in_specs=[pl.BlockSpec((None, None, half_rows, n_cols), lambda j, c: (j, c[0], 0, 0)), HBM_SPEC],
            out_specs=[pl.BlockSpec((None, half_rows, n_cols), lambda j, c: (j, 0, 0)), HBM_SPEC],
            scratch_shapes=[pltpu.VMEM((half_rows, n_cols), BF16), pltpu.SemaphoreType.DMA((N_CHIPS,)),
                            pltpu.SemaphoreType.DMA((N_CHIPS,)), pltpu.SemaphoreType.DMA(())]),
        out_shape=[jax.ShapeDtypeStruct((N_CHIPS, half_rows, n_cols), BF16),
                   jax.ShapeDtypeStruct((N_CHIPS, half_rows, n_cols), BF16)],
        compiler_params=_params(("arbitrary",)),
    )(c_arr, dw4, dw)
    return out


def _sum_chips(c_arr, parts, name):
    _, rows, n_cols = parts.shape
    nt = 2
    tr = rows // nt

    def body(c_ref, p_ref, o_ref):
        o_ref[...] = ((p_ref[0].astype(F32) + p_ref[1].astype(F32)) + p_ref[2].astype(F32)) + p_ref[3].astype(F32)

    return pl.pallas_call(
        body, name=name,
        grid_spec=pltpu.PrefetchScalarGridSpec(
            num_scalar_prefetch=1, grid=(nt,),
            in_specs=[pl.BlockSpec((N_CHIPS, tr, n_cols), lambda i, c: (0, i, 0))],
            out_specs=pl.BlockSpec((tr, n_cols), lambda i, c: (c[0] * nt + i, 0))),
        out_shape=jax.ShapeDtypeStruct((2 * rows, n_cols), F32),
        compiler_params=_params(("parallel",)),
    )(c_arr, parts)


def _adamw_math(w, g, m, v):
    mn = ADAM_B1 * m + (1.0 - ADAM_B1) * g
    vn = ADAM_B2 * v + (1.0 - ADAM_B2) * (g * g)
    m_hat = mn / (1.0 - ADAM_B1 ** ADAM_STEP)
    v_hat = vn / (1.0 - ADAM_B2 ** ADAM_STEP)
    return -ADAM_LR * (m_hat / (jnp.sqrt(v_hat) + ADAM_EPS) + ADAM_WD * w), mn, vn


def _adamw(w, g, m, v, name):
    R, C = w.shape
    tr = next((t for t in (256, 192, 128) if R % t == 0), R)

    def body(w_ref, g_ref, m_ref, v_ref, go_ref, d_ref, mo_ref, vo_ref):
        gv = g_ref[...]
        go_ref[...] = gv
        d_ref[...], mo_ref[...], vo_ref[...] = _adamw_math(w_ref[...], gv, m_ref[...], v_ref[...])

    spec = pl.BlockSpec((tr, C), lambda i: (i, 0))
    shp = jax.ShapeDtypeStruct((R, C), F32)
    return pl.pallas_call(
        body, name=name, grid=(R // tr,), in_specs=[spec] * 4, out_specs=[spec] * 4, out_shape=[shp] * 4,
        compiler_params=_params(("parallel",)),
    )(w, g, m, v)


HBM_SPEC = pl.BlockSpec(memory_space=pltpu.HBM)
GATHER_LOCAL_CHUNKS = 4
GATHER_SEMS = [pltpu.SemaphoreType.DMA((7,)), pltpu.SemaphoreType.DMA((7,)),
               pltpu.SemaphoreType.DMA((GATHER_LOCAL_CHUNKS,))]
OWNER_SEMS = [pltpu.SemaphoreType.DMA((3,)), pltpu.SemaphoreType.DMA((3,)), pltpu.SemaphoreType.DMA(())]


def _mesh_pos():
    return lax.axis_index("x"), lax.axis_index("y"), lax.axis_index("c")


class _RowGather:
    def __init__(self, src_ref, full_ref, rows, send_sems, recv_sems, local_sems):
        self.src, self.full, self.rows = src_ref, full_ref, rows
        self.send, self.recv, self.local = send_sems, recv_sems, local_sems
        x, y, c = _mesh_pos()
        self.c, self.me, self.sibling = c, (x, y, c), (x, y, 1 - c)
        self.chips = [(1 - x, y), (x, 1 - y), (1 - x, 1 - y)]

    def _block(self, pos):
        px, py, pc = pos
        return self.full.at[pl.ds(pl.multiple_of((4 * px + 2 * py + pc) * self.rows, 16), self.rows), :]

    def _copy(self, k, blk, to):
        return pltpu.make_async_remote_copy(
            src_ref=self.src if blk is self.me else self._block(blk), dst_ref=self._block(blk),
            send_sem=self.send.at[k], recv_sem=self.recv.at[k], device_id=to, device_id_type=MESH)

    def _mine(self):
        return _place_locally(self.src, self._block(self.me), self.local, GATHER_LOCAL_CHUNKS)

    def _first(self):
        return [self._copy(0, self.me, self.sibling)] + [
            self._copy(1 + j, self.me, (*chip, self.c)) for j, chip in enumerate(self.chips)]

    def start(self):
        for cp in self._first() + self._mine():
            cp.start()

    def finish(self):
        passed = [self._copy(4 + j, (*chip, self.c), self.sibling) for j, chip in enumerate(self.chips)]
        for j, chip in enumerate(self.chips):
            self._copy(1 + j, (*chip, self.c), self.me).wait_recv()
            passed[j].start()
        self._copy(0, self.sibling, self.me).wait_recv()
        for j, chip in enumerate(self.chips):
            self._copy(4 + j, (*chip, 1 - self.c), self.me).wait_recv()
        for cp in self._first() + passed:
            cp.wait_send()
        for cp in self._mine():
            cp.wait()


class _OwnerExchange:
    def __init__(self, src_ref, dst_ref, send_sems, recv_sems, local_sem):
        self.src, self.dst, self.send, self.recv, self.local = src_ref, dst_ref, send_sems, recv_sems, local_sem
        x, y, c = _mesh_pos()
        self.c, self.my_chip = c, 2 * x + y
        self.peers = [(1 - x, y), (x, 1 - y), (1 - x, 1 - y)]

    def _copies(self):
        local = pltpu.make_async_copy(self.src.at[self.my_chip], self.dst.at[self.my_chip], self.local)
        remote = [pltpu.make_async_remote_copy(
            src_ref=self.src.at[2 * px + py], dst_ref=self.dst.at[self.my_chip],
            send_sem=self.send.at[k], recv_sem=self.recv.at[k], device_id=(px, py, self.c), device_id_type=MESH)
            for k, (px, py) in enumerate(self.peers)]
        return local, remote

    def start(self):
        local, remote = self._copies()
        local.start()
        for cp in remote:
            cp.start()

    def finish(self):
        local, remote = self._copies()
        for cp in remote:
            cp.wait_recv()
        for cp in remote:
            cp.wait_send()
        local.wait()


def _sibling_copies(dw_ref, r_ref, rows, send_sems, recv_sems):
    x, y, c = _mesh_pos()
    return [pltpu.make_async_remote_copy(
        src_ref=dw_ref.at[pl.ds(pl.multiple_of((2 * j + (1 - c)) * rows, 16), rows), :], dst_ref=r_ref.at[j],
        send_sem=send_sems.at[j], recv_sem=recv_sems.at[j], device_id=(x, y, 1 - c), device_id_type=MESH)
        for j in range(N_CHIPS)]


PAIR_CHUNKS = 4


def _pair_halves(gt, go):
    def body(gt_in, go_in, gt_ref, go_ref, send_sems, recv_sems):
        del gt_in, go_in
        x, y, c = _mesh_pos()
        copies = []
        for a, (ref, rows) in enumerate(((gt_ref, W_IN_ROWS), (go_ref, W_OUT_ROWS))):
            ch = rows // PAIR_CHUNKS
            for q in range(PAIR_CHUNKS):
                part = ref.at[pl.ds(pl.multiple_of(c * rows + q * ch, 8), ch), :]
                copies.append(pltpu.make_async_remote_copy(
                    src_ref=part, dst_ref=part, send_sem=send_sems.at[PAIR_CHUNKS * a + q],
                    recv_sem=recv_sems.at[PAIR_CHUNKS * a + q], device_id=(x, y, 1 - c), device_id_type=MESH))
        for cp in copies:
            cp.start()
        for cp in copies:
            cp.wait_recv()
        for cp in copies:
            cp.wait_send()

    return pl.pallas_call(
        body, name="pair_halves",
        in_specs=[HBM_SPEC, HBM_SPEC], out_specs=[HBM_SPEC, HBM_SPEC],
        out_shape=[jax.ShapeDtypeStruct(gt.shape, F32), jax.ShapeDtypeStruct(go.shape, F32)],
        input_output_aliases={0: 0, 1: 1},
        scratch_shapes=[pltpu.SemaphoreType.DMA((2 * PAIR_CHUNKS,)), pltpu.SemaphoreType.DMA((2 * PAIR_CHUNKS,))],
    )(gt, go)


VEC = (("g_pre", 2048), ("g_post", 2048), ("b_qkv", 1280), ("ln_v_g", 1024), ("ln_v_b", 1024), ("attn_sinks", 16))
VEC_ROWS = 8
LOSS_ROW = len(VEC)
MAT = (("w_spatial", NG * T), ("b_spatial", NG))
MAT_ROWS = sum(r for _, r in MAT)


class _SlotGather:
    def __init__(self, buf, send_sems, recv_sems):
        self.buf, self.send, self.recv = buf, send_sems, recv_sems
        x, y, c = _mesh_pos()
        self.c, self.me, self.sibling, self.my_id = c, (x, y, c), (x, y, 1 - c), 4 * x + 2 * y + c
        self.chips = [(1 - x, y), (x, 1 - y), (1 - x, 1 - y)]

    def _copy(self, k, blk, to):
        px, py, pc = blk
        slot = self.buf.at[4 * px + 2 * py + pc]
        return pltpu.make_async_remote_copy(
            src_ref=slot, dst_ref=slot, send_sem=self.send.at[k], recv_sem=self.recv.at[k],
            device_id=to, device_id_type=MESH)

    def _first(self):
        return [self._copy(0, self.me, self.sibling)] + [
            self._copy(1 + j, self.me, (*chip, self.c)) for j, chip in enumerate(self.chips)]

    def start(self):
        for cp in self._first():
            cp.start()

    def finish(self):
        passed = [self._copy(4 + j, (*chip, self.c), self.sibling) for j, chip in enumerate(self.chips)]
        for j, chip in enumerate(self.chips):
            self._copy(1 + j, (*chip, self.c), self.me).wait_recv()
            passed[j].start()
        self._copy(0, self.sibling, self.me).wait_recv()
        for j, chip in enumerate(self.chips):
            self._copy(4 + j, (*chip, 1 - self.c), self.me).wait_recv()
        for cp in self._first() + passed:
            cp.wait_send()

    def total(self):
        t = self.buf[0]
        for d in range(1, N_DEV):
            t = t + self.buf[d]
        return t


def _small_update(vec_grads, loss_part, mat_total, vec_state, mat_state):
    n_vec, n_mat = len(VEC), len(MAT)
    n_par = n_vec + n_mat
    n_in = n_vec + 2 + 3 * n_par

    def body(*refs):
        g_in, loss_in, tot_m = refs[:n_vec], refs[n_vec], refs[n_vec + 1]
        st_in = refs[n_vec + 2:n_in]
        outs, loss_out = refs[n_in:n_in + 4 * n_par], refs[n_in + 4 * n_par]
        vbuf, tot_v, send_sems, recv_sems = refs[n_in + 4 * n_par + 1:]
        gather = _SlotGather(vbuf, send_sems, recv_sems)
        vbuf[gather.my_id] = jnp.zeros((VEC_ROWS, D), F32)
        for r, (_, n) in enumerate(VEC):
            vbuf[gather.my_id, r:r + 1, 0:n] = g_in[r][...]
        vbuf[gather.my_id, LOSS_ROW:LOSS_ROW + 1, 0:128] = loss_in[...]
        gather.start()
        gather.finish()
        tot_v[...] = gather.total()
        loss_out[...] = tot_v[LOSS_ROW:LOSS_ROW + 1, 0:128]
        r0 = 0
        for q in range(n_par):
            if q < n_vec:
                g = tot_v[q:q + 1, 0:VEC[q][1]]
            else:
                rows = MAT[q - n_vec][1]
                g = tot_m[r0:r0 + rows, :]
                r0 += rows
            w, m, v = (st_in[3 * q + t][...] for t in range(3))
            outs[4 * q][...] = g
            outs[4 * q + 1][...], outs[4 * q + 2][...], outs[4 * q + 3][...] = _adamw_math(w, g, m, v)

    state = [a for wmv in list(vec_state) + list(mat_state) for a in wmv]
    vmem = pl.BlockSpec(memory_space=pltpu.VMEM)
    shapes = [g.shape for g in vec_grads] + [wmv[0].shape for wmv in mat_state]
    out_shape = [jax.ShapeDtypeStruct(s, F32) for s in shapes for _ in range(4)]
    out_shape.append(jax.ShapeDtypeStruct((1, 128), F32))
    res = pl.pallas_call(
        body, name="small_update",
        in_specs=[vmem] * n_in, out_specs=[vmem] * len(out_shape), out_shape=out_shape,
        scratch_shapes=[pltpu.VMEM((N_DEV, VEC_ROWS, D), F32), pltpu.VMEM((VEC_ROWS, D), F32),
                        pltpu.SemaphoreType.DMA((7,)), pltpu.SemaphoreType.DMA((7,))],
        compiler_params=pltpu.CompilerParams(vmem_limit_bytes=VMEM_LIMIT),
    )(*vec_grads, loss_part, mat_total, *state)
    return [res[4 * q:4 * q + 4] for q in range(n_par)], res[-1]


def kernel(x, positions, g_pre, w_in, b_qkv, ln_v_g, ln_v_b, w_spatial, b_spatial, attn_sinks, w_out, g_post, loss_target, m_g_pre, m_w_in, m_b_qkv, m_ln_v_g, m_ln_v_b, m_w_spatial, m_b_spatial, m_attn_sinks, m_w_out, m_g_post, v_g_pre, v_w_in, v_b_qkv, v_ln_v_g, v_ln_v_b, v_w_spatial, v_b_spatial, v_attn_sinks, v_w_out, v_g_post):
    S = x.shape[1]
    c = lax.axis_index("c")
    c_arr = jnp.reshape(c, (1,)).astype(jnp.int32)
    x2 = x[0]
    target = loss_target[0]
    pos = positions.reshape(S, 1)
    half = HD // 2
    inv_freq = ROPE_THETA ** (-jnp.arange(half, dtype=F32) * (2.0 / HD))
    invf = jnp.tile(inv_freq, 128 // half).reshape(1, 128)
    bias = jnp.concatenate([jnp.zeros((OFF_Q,), F32), b_qkv[0], jnp.zeros((D_IN - OFF_ZB,), F32)]).reshape(1, D_IN)
    b_s_col = b_spatial[0].reshape(NG, T, 1)
    sinks = jnp.repeat(attn_sinks[0], T).reshape(NQ * T, 1)

    chip = 2 * lax.axis_index("x") + lax.axis_index("y")
    wt_part = lax.dynamic_slice_in_dim(w_in[0].T.astype(BF16), c * W_IN_ROWS, W_IN_ROWS, axis=0)
    wo_part = lax.dynamic_slice_in_dim(w_out[0].astype(BF16), c * W_OUT_ROWS, W_OUT_ROWS, axis=0)
    sched = jnp.asarray(PROJ_SCHEDULE, jnp.int32)[chip]

    h, proj, wt, cos, sin = _prenorm_inproj(sched, x2, g_pre, bias, wt_part, pos, invf)
    ycat, wo, q_roped = _mid_fwd(proj, cos, sin, ln_v_g, ln_v_b, w_spatial[0], b_s_col, sinks, wo_part)
    dy, dout, loss_part, dg_post = _outproj_loss(ycat, wo, x2, target, g_post)

    dwo = _matmul_tn(ycat, dy, 1024, "dw_out")
    dycat, ro = _dycat(dy, wo, dwo)
    po = _presum(c_arr, dwo, ro, W_OUT_ROWS)
    dproj, dln_g, dln_b, dws, dbs, dsink, dbqkv, bo = _mid_bwd(
        proj, dycat, q_roped, cos, sin, ln_v_g, ln_v_b, w_spatial[0], b_s_col, sinks, po)
    dwt, mat_total = _matmul_tn(dproj, h, 768, "dw_in_t",
                                mat_grads=[dws.reshape(NG * T, T), dbs.reshape(NG, T)])
    pt = _sibling_presum(c_arr, dwt, W_IN_ROWS, "sibling_presum_in")
    grad_x, dg_pre, bt = _dh_prenorm_bwd(dproj, wt, x2, dout, g_pre, pt)
    gt, go = _pair_halves(_sum_chips(c_arr, bt, "sum_chips_in"), _sum_chips(c_arr, bo, "sum_chips_out"))

    g_w_in, d_w_in, nm_w_in, nv_w_in = (a.T for a in _adamw(w_in[0].T, gt, m_w_in[0].T, v_w_in[0].T, "adamw_w_in"))
    g_w_out, d_w_out, nm_w_out, nv_w_out = _adamw(w_out[0], go, m_w_out[0], v_w_out[0], "adamw_w_out")

    state = {"g_pre": (g_pre, m_g_pre, v_g_pre), "g_post": (g_post, m_g_post, v_g_post),
             "b_qkv": (b_qkv, m_b_qkv, v_b_qkv), "ln_v_g": (ln_v_g, m_ln_v_g, v_ln_v_g),
             "ln_v_b": (ln_v_b, m_ln_v_b, v_ln_v_b), "attn_sinks": (attn_sinks, m_attn_sinks, v_attn_sinks),
             "w_spatial": tuple(a.reshape(NG * T, T) for a in (w_spatial, m_w_spatial, v_w_spatial)),
             "b_spatial": tuple(a.reshape(NG, T) for a in (b_spatial, m_b_spatial, v_b_spatial))}
    grads = {"g_pre": dg_pre, "g_post": dg_post, "b_qkv": dbqkv, "ln_v_g": dln_g, "ln_v_b": dln_b,
             "attn_sinks": dsink[:, 0].reshape(1, NQ)}
    results, loss = _small_update([grads[n] for n, _ in VEC], loss_part, mat_total,
                                  [state[n] for n, _ in VEC], [state[n] for n, _ in MAT])
    small = {n: [a.reshape(w.shape) for a in res]
             for (n, _), res, w in zip(VEC + MAT, results, [state[n][0] for n, _ in VEC + MAT])}
    small["w_spatial"] = [a.reshape(w_spatial.shape) for a in small["w_spatial"]]
    small["b_spatial"] = [a.reshape(b_spatial.shape) for a in small["b_spatial"]]
    big = {"w_in": [a[None] for a in (g_w_in, d_w_in, nm_w_in, nv_w_in)],
           "w_out": [a[None] for a in (g_w_out, d_w_out, nm_w_out, nv_w_out)]}
    order = ("g_pre", "w_in", "b_qkv", "ln_v_g", "ln_v_b", "w_spatial", "b_spatial", "attn_sinks", "w_out", "g_post")
    leaves = {**small, **big}
    return (loss[0, 0], grad_x[None], *[leaves[n][t] for t in range(4) for n in order])
```

```python
import jax
import jax.numpy as jnp
from jax import lax
from jax.experimental import pallas as pl
from jax.experimental.pallas import tpu as pltpu

F32 = jnp.float32
BF16 = jnp.bfloat16
MESH = pl.DeviceIdType.MESH

D = 2048
DG = 1024
T = 128
NG = 8
HD = 64
NQ = 16
D_IN = 5376
OFF_U, OFF_V, OFF_ZA, OFF_Q, OFF_K, OFF_VA, OFF_ZB = 0, 1024, 2048, 3072, 4096, 4224, 4352
D_QKV = 1280
EPS = 1e-6
ROPE_THETA = 10000.0
N_CHIPS = 4
N_DEV = 8
W_IN_ROWS = D_IN // N_DEV
W_OUT_ROWS = D // N_DEV

ADAM_LR, ADAM_B1, ADAM_B2, ADAM_EPS, ADAM_WD, ADAM_STEP = 0.001, 0.9, 0.999, 1e-08, 0.01, 10

VMEM_LIMIT = 56 * 1024 * 1024


def _tile(n, pref):
    return pref if n % pref == 0 else n


def _params(sem=None, vmem=VMEM_LIMIT):
    return pltpu.CompilerParams(dimension_semantics=sem, vmem_limit_bytes=vmem)


def _sigmoid(z):
    return 1.0 / (1.0 + jnp.exp(-z))


def _row_mean(v):
    return jnp.mean(v, axis=-1, keepdims=True)


def _dot(a, b, dims):
    return lax.dot_general(a, b, (dims, ((), ())), preferred_element_type=F32)


NN = ((1,), (0,))
NT = ((1,), (1,))
TN = ((0,), (0,))


PROJ_TN = 768
N_PROJ_TILES = D_IN // PROJ_TN
PROJ_SCHEDULE = ((0, 1, 2, 4, 3, 6, 5, 5), (2, 1, 0, 6, 4, 3, 5, 4), (4, 5, 6, 0, 2, 1, 3, 4), (6, 5, 4, 2, 3, 0, 1, 5))


def _place_locally(src_ref, dst_rows_ref, sems, n_chunks):
    ch = src_ref.shape[0] // n_chunks
    return [pltpu.make_async_copy(src_ref.at[pl.ds(q * ch, ch), :], dst_rows_ref.at[pl.ds(q * ch, ch), :],
                                  sems.at[q]) for q in range(n_chunks)]


def _prenorm_inproj(sched, x, g, bias, wt_part, pos, invf):
    S = x.shape[0]
    tp, tm = _tile(S, 512), _tile(S, 1024)
    n_pre, ns = S // tp, S // tm
    n_steps = n_pre + N_PROJ_TILES * ns
    pos_of = lambda i: jnp.maximum(i - n_pre, 0) // ns
    row_of = lambda i: jnp.maximum(i - n_pre, 0) % ns

    def body(sched_ref, x_ref, g_ref, b_ref, wpart_ref, pos_ref, invf_ref, h_ref, proj_ref, wt_ref, cos_ref, sin_ref,
             h_all, w_tile, stage, send_sems, recv_sems, w_sems, local_sems):
        i = pl.program_id(0)
        x_, y_, c = _mesh_pos()
        me, sibling = (x_, y_, c), (x_, y_, 1 - c)
        chips = [(1 - x_, y_), (x_, 1 - y_), (1 - x_, 1 - y_)]

        def block(pos):
            px, py, pc = pos
            return wt_ref.at[pl.ds(pl.multiple_of((4 * px + 2 * py + pc) * W_IN_ROWS, 16), W_IN_ROWS), :]

        def copy(k, blk, to):
            return pltpu.make_async_remote_copy(
                src_ref=wpart_ref if blk is me else block(blk), dst_ref=block(blk),
                send_sem=send_sems.at[k], recv_sem=recv_sems.at[k], device_id=to, device_id_type=MESH)

        stage_in = pltpu.make_async_copy(wpart_ref, stage, local_sems.at[0])
        stage_out = pltpu.make_async_copy(stage, block(me), local_sems.at[1])

        relay = (jnp.where(c == 0, x_, 1 - x_), jnp.where(c == 0, 1 - y_, y_))
        relayed = (jnp.where(c == 0, 1 - x_, x_), jnp.where(c == 0, y_, 1 - y_))

        early = c == y_
        y_send = copy(2, me, (*chips[1], c))

        def own_sends():
            return [copy(0, me, sibling), y_send, copy(1, me, (*chips[0], c))]

        def passed_on():
            return [copy(4, (*chips[0], c), sibling), copy(5, (*chips[1], c), sibling),
                    copy(3, (*relayed, c), (*relay, c))]

        def early_block_arrives():
            @pl.when(early)
            def _():
                copy(2, (*chips[1], c), me).wait_recv()
                copy(5, (*chips[1], c), sibling).start()
                y_send.start()

            pl.when(jnp.logical_not(early))(lambda: copy(5, (*chips[1], 1 - c), me).wait_recv())

        def neighbours_arrive():
            copy(1, (*chips[0], c), me).wait_recv()
            copy(4, (*chips[0], c), sibling).start()

            @pl.when(early)
            def _():
                copy(3, (*relayed, c), (*relay, c)).start()
                copy(5, (*chips[1], 1 - c), me).wait_recv()

            @pl.when(jnp.logical_not(early))
            def _():
                copy(2, (*chips[1], c), me).wait_recv()
                copy(5, (*chips[1], c), sibling).start()
                copy(3, (*relayed, c), (*relay, c)).start()

        def diagonal_arrives():
            copy(3, (*chips[2], c), me).wait_recv()
            copy(6, (*chips[2], c), sibling).start()
            copy(6, (*chips[2], 1 - c), me).wait_recv()

        def tile_load(p):
            slot = p % 2
            rows = wt_ref.at[pl.ds(pl.multiple_of(sched_ref[p] * PROJ_TN, 16), PROJ_TN), :]
            return pltpu.make_async_copy(rows, w_tile.at[slot], w_sems.at[slot])

        def prepare(p):
            p = jnp.asarray(p, jnp.int32)

            @pl.when(p == 0)
            def _():
                copy(0, sibling, me).wait_recv()
                stage_out.wait()

            pl.when(p == 1)(early_block_arrives)
            pl.when(p == 2)(neighbours_arrive)
            pl.when(p == 3)(lambda: copy(4, (*chips[0], 1 - c), me).wait_recv())
            pl.when(p == sched_ref[N_PROJ_TILES])(diagonal_arrives)
            tile_load(p).start()

        @pl.when(i == 0)
        def _():
            stage_in.start()
            copy(0, me, sibling).start()
            copy(1, me, (*chips[0], c)).start()
            pl.when(jnp.logical_not(early))(y_send.start)
            stage_in.wait()
            stage_out.start()

        @pl.when(i < n_pre)
        def _():
            xv = x_ref[...]
            r = lax.rsqrt(_row_mean(xv * xv) + EPS)
            hv = (xv * r * g_ref[...]).astype(BF16)
            h_ref[...] = hv
            h_all[pl.ds(pl.multiple_of(i * tp, tp), tp), :] = hv
            ang = pos_ref[...].astype(F32) * invf_ref[...]
            cos_ref[...] = jnp.cos(ang)
            sin_ref[...] = jnp.sin(ang)

        pl.when(i == n_pre - 1)(lambda: prepare(0))

        @pl.when(i >= n_pre)
        def _():
            p, s = pos_of(i), row_of(i)
            pl.when(s == 0)(lambda: tile_load(p).wait())
            pl.when((s == ns - 1) & (p < N_PROJ_TILES - 1))(lambda: prepare(p + 1))
            hv = h_all[pl.ds(pl.multiple_of(s * tm, tm), tm), :]
            proj_ref[...] = _dot(hv, w_tile[p % 2], NT) + b_ref[...]

        @pl.when(i == n_steps - 1)
        def _():
            for cp in own_sends() + passed_on() + [copy(6, (*chips[2], c), sibling)]:
                cp.wait_send()

    return pl.pallas_call(
        body, name="prenorm_inproj",
        grid_spec=pltpu.PrefetchScalarGridSpec(
            num_scalar_prefetch=1, grid=(n_steps,),
            in_specs=[pl.BlockSpec((tp, D), lambda i, sc: (jnp.minimum(i, n_pre - 1), 0)),
                      pl.BlockSpec((1, D), lambda i, sc: (0, 0)),
                      pl.BlockSpec((1, PROJ_TN), lambda i, sc: (0, sc[pos_of(i)])),
                      HBM_SPEC,
                      pl.BlockSpec((tp, 1), lambda i, sc: (jnp.minimum(i, n_pre - 1), 0)),
                      pl.BlockSpec((1, 128), lambda i, sc: (0, 0))],
            out_specs=[pl.BlockSpec((tp, D), lambda i, sc: (jnp.minimum(i, n_pre - 1), 0)),
                       pl.BlockSpec((tm, PROJ_TN), lambda i, sc: (row_of(i), sc[pos_of(i)])),
                       HBM_SPEC,
                       pl.BlockSpec((tp, 128), lambda i, sc: (jnp.minimum(i, n_pre - 1), 0)),
                       pl.BlockSpec((tp, 128), lambda i, sc: (jnp.minimum(i, n_pre - 1), 0))],
            scratch_shapes=[pltpu.VMEM((S, D), BF16), pltpu.VMEM((2, PROJ_TN, D), BF16),
                            pltpu.VMEM((W_IN_ROWS, D), BF16),
                            pltpu.SemaphoreType.DMA((7,)), pltpu.SemaphoreType.DMA((7,)),
                            pltpu.SemaphoreType.DMA((2,)), pltpu.SemaphoreType.DMA((2,))]),
        out_shape=[jax.ShapeDtypeStruct((S, D), BF16), jax.ShapeDtypeStruct((S, D_IN), F32),
                   jax.ShapeDtypeStruct((D_IN, D), BF16),
                   jax.ShapeDtypeStruct((S, 128), F32), jax.ShapeDtypeStruct((S, 128), F32)],
        compiler_params=_params(("arbitrary",)),
    )(sched, x, g, bias, wt_part, pos, invf)


def _rot_half(xs, first_half):
    return jnp.where(first_half, -pltpu.roll(xs, 96, 1), pltpu.roll(xs, 32, 1))


GH = NQ // 2


MASKED = -1e30


def _attn_consts():
    lane = lax.broadcasted_iota(jnp.int32, (T, 128), 1)
    row = lax.broadcasted_iota(jnp.int32, (GH * T, T), 0) & (T - 1)
    on_diag_or_below = row >= lax.broadcasted_iota(jnp.int32, (GH * T, T), 1)
    return (lane & (HD - 1)) < (HD // 2), lane < HD, on_diag_or_below


def _stack_heads(slab_fn, grp, lo64):
    blocks = []
    for jj in range(GH // 2):
        s = slab_fn(GH // 2 * grp + jj)
        blocks += [jnp.where(lo64, s, 0.0), jnp.where(lo64, 0.0, s)]
    return jnp.concatenate(blocks, axis=0)


def _unstack_heads(stacked, jj, lo64):
    return jnp.where(lo64, stacked[(2 * jj) * T:(2 * jj + 1) * T], stacked[(2 * jj + 1) * T:(2 * jj + 2) * T])


def _both_halves(kv, grp):
    lo = lax.broadcasted_iota(jnp.int32, kv.shape, 1) < HD
    swapped = pltpu.roll(kv, HD, 1)
    return jnp.where(lo, kv, swapped) if grp == 0 else jnp.where(lo, swapped, kv)


def _one_half(acc, grp):
    lo = lax.broadcasted_iota(jnp.int32, acc.shape, 1) < HD
    return jnp.where(lo if grp == 0 else jnp.logical_not(lo), acc + pltpu.roll(acc, HD, 1), 0.0)


def _layer_norm_stats(v):
    mu = _row_mean(v)
    xc = v - mu
    var = _row_mean(xc * xc)
    rs = lax.rsqrt(var + EPS)
    return xc * rs, rs


def _fold(both, own):
    return jnp.where(own, both[:, T:2 * T], both[:, 0:T])


def _unfold(p, own):
    return jnp.concatenate([jnp.where(own, 0.0, p), jnp.where(own, p, 0.0)], axis=1).astype(BF16)


def _band_softmax(q_scaled, k_both, own, has_prev, sink):
    s_both = _dot(q_scaled, k_both, NT)
    s_prev = s_both[:, 0:T]
    if has_prev is not None:
        s_prev = s_prev + jnp.where(has_prev, 0.0, MASKED)
    s = jnp.where(own, s_both[:, T:2 * T], s_prev)
    m = jnp.maximum(jnp.max(s, axis=-1, keepdims=True), sink)
    e = jnp.exp(s - m)
    es = jnp.exp(sink - m)
    inv = 1.0 / (jnp.sum(e, axis=-1, keepdims=True) + es)
    return e * inv, es * inv


BPS_FWD = 4
BPS_BWD = 4


def _mid_specs(nt, rev, bps):
    tile = (lambda i: nt - 1 - i) if rev else (lambda i: i)
    prev = lambda i: jnp.maximum(bps * tile(i) - 1, 0)
    rows = bps * T
    return tile, [
        pl.BlockSpec((rows, D_IN), lambda i: (tile(i), 0)),
        pl.BlockSpec((T, 2 * T), lambda i: (prev(i), OFF_K // (2 * T))),
    ], [
        pl.BlockSpec((rows, 128), lambda i: (tile(i), 0)),
        pl.BlockSpec((rows, 128), lambda i: (tile(i), 0)),
        pl.BlockSpec((T, 128), lambda i: (prev(i), 0)),
        pl.BlockSpec((T, 128), lambda i: (prev(i), 0)),
        pl.BlockSpec((1, DG), lambda i: (0, 0)),
        pl.BlockSpec((1, DG), lambda i: (0, 0)),
        pl.BlockSpec((NG, T, T), lambda i: (0, 0, 0)),
        pl.BlockSpec((NG, T, 1), lambda i: (0, 0, 0)),
        pl.BlockSpec((NQ * T, 1), lambda i: (0, 0)),
    ]


def _rope(xs, cosv, sinv, first_half):
    return xs * cosv + _rot_half(xs, first_half) * sinv


def _mid_fwd(proj, cos, sin, ln_g, ln_b, w_s, b_s, sinks, wo_part):
    S = proj.shape[0]
    BPS = BPS_FWD if S % (BPS_FWD * T) == 0 else 1
    nt = S // (BPS * T)
    tile, proj_specs, par_specs = _mid_specs(nt, False, BPS)

    def body(p_ref, kvp_ref, cos_ref, sin_ref, cosp_ref, sinp_ref, lng_ref, lnb_ref, ws_ref, bs_ref, sink_ref,
             wpart_ref, y_ref, wo_ref, q_ref, wm_ref, send_sems, recv_sems, local_sems):
        i = pl.program_id(0)
        first_half, lo64, own = _attn_consts()
        gather = _RowGather(wpart_ref, wo_ref, W_OUT_ROWS, send_sems, recv_sems, local_sems)
        pl.when(i == 0)(gather.start)

        @pl.when(i == 0)
        def _():
            tril = lax.broadcasted_iota(jnp.int32, (T, T), 0) >= lax.broadcasted_iota(jnp.int32, (T, T), 1)
            for g in range(NG):
                wm_ref[g] = jnp.where(tril, ws_ref[g], 0.0).astype(BF16)

        def block(b, k_prev, v_prev, has_prev):
            rows = slice(b * T, (b + 1) * T)
            xhat, _ = _layer_norm_stats(p_ref[rows, OFF_V:OFF_V + DG])
            vn = xhat * lng_ref[...] + lnb_ref[...]
            for g in range(NG):
                sl = slice(128 * g, 128 * g + 128)
                mixed = _dot(wm_ref[g], vn[:, sl].astype(BF16), NN) + bs_ref[g]
                z = p_ref[rows, OFF_ZA + 128 * g:OFF_ZA + 128 * g + 128]
                u = p_ref[rows, OFF_U + 128 * g:OFF_U + 128 * g + 128]
                y_ref[rows, sl] = (u * mixed * (z * _sigmoid(z))).astype(BF16)

            cosv, sinv = cos_ref[rows, :], sin_ref[rows, :]
            k_cur = _rope(p_ref[rows, OFF_K:OFF_K + 128], cosv, sinv, first_half)
            v_cur = p_ref[rows, OFF_VA:OFF_VA + 128]

            def q_slab(j):
                q = (_rope(p_ref[rows, OFF_Q + 128 * j:OFF_Q + 128 * j + 128], cosv, sinv, first_half)
                     * (HD ** -0.5)).astype(BF16)
                q_ref[rows, 128 * j:128 * j + 128] = q
                return q

            k_both = jnp.concatenate([k_prev, k_cur], axis=0)
            v_both = jnp.concatenate([v_prev, v_cur], axis=0)
            for grp in range(2):
                qs = _stack_heads(q_slab, grp, lo64)
                kg, vg = _both_halves(k_both, grp).astype(BF16), _both_halves(v_both, grp).astype(BF16)
                p, _ = _band_softmax(qs, kg, own, has_prev, sink_ref[GH * T * grp:GH * T * (grp + 1), :])
                o_st = _dot(_unfold(p, own), vg, NN)
                for jj in range(GH // 2):
                    c0 = 128 * (GH // 2 * grp + jj)
                    zb = p_ref[rows, OFF_ZB + c0:OFF_ZB + c0 + 128]
                    o = _unstack_heads(o_st, jj, lo64)
                    y_ref[rows, DG + c0:DG + c0 + 128] = (o * (zb * _sigmoid(zb))).astype(BF16)
            return k_cur, v_cur

        k_prev = _rope(kvp_ref[:, 0:128], cosp_ref[...], sinp_ref[...], first_half)
        kv = block(0, k_prev, kvp_ref[:, 128:256], i > 0)
        for b in range(1, BPS):
            kv = block(b, *kv, None)
        pl.when(i == nt - 1)(gather.finish)

    return pl.pallas_call(
        body, name="mid_fwd", grid=(nt,),
        in_specs=proj_specs + par_specs + [HBM_SPEC],
        out_specs=[pl.BlockSpec((BPS * T, D), lambda i: (tile(i), 0)), HBM_SPEC,
                   pl.BlockSpec((BPS * T, NQ * HD), lambda i: (tile(i), 0))],
        out_shape=[jax.ShapeDtypeStruct((S, D), BF16), jax.ShapeDtypeStruct((D, D), BF16),
                   jax.ShapeDtypeStruct((S, NQ * HD), BF16)],
        scratch_shapes=[pltpu.VMEM((NG, T, T), BF16)] + GATHER_SEMS,
        compiler_params=_params(("arbitrary",)),
    )(proj, proj, cos, sin, cos, sin, ln_g, ln_b, w_s, b_s, sinks, wo_part)


def _mid_bwd(proj, dycat, q_roped, cos, sin, ln_g, ln_b, w_s, b_s, sinks, po):
    S = proj.shape[0]
    BPS = BPS_BWD if S % (BPS_BWD * T) == 0 else 1
    nt = S // (BPS * T)
    tile, proj_specs, par_specs = _mid_specs(nt, True, BPS)
    const2 = lambda i: (0, 0)

    def body(p_ref, kvp_ref, dyc_ref, q_ref, cos_ref, sin_ref, cosp_ref, sinp_ref, lng_ref, lnb_ref, ws_ref, bs_ref,
             sink_ref, po_ref, dp_ref, dlng_ref, dlnb_ref, dws_ref, dbs_ref, dsink_ref, dbqkv_ref, bo_ref,
             carry_ref, dvn_ref, wm_ref, wmt_ref, send_sems, recv_sems, local_sem):
        i = pl.program_id(0)
        first_half, lo64, own = _attn_consts()
        tril = lax.broadcasted_iota(jnp.int32, (T, T), 0) >= lax.broadcasted_iota(jnp.int32, (T, T), 1)
        exchange = _OwnerExchange(po_ref, bo_ref, send_sems, recv_sems, local_sem)
        pl.when(i == 0)(exchange.start)

        @pl.when(i == 0)
        def _():
            for g in range(NG):
                wm = jnp.where(tril, ws_ref[g], 0.0)
                wm_ref[g] = wm.astype(BF16)
                wmt_ref[g] = wm.T.astype(BF16)
            dlng_ref[...] = jnp.zeros_like(dlng_ref)
            dlnb_ref[...] = jnp.zeros_like(dlnb_ref)
            dws_ref[...] = jnp.zeros_like(dws_ref)
            dbs_ref[...] = jnp.zeros_like(dbs_ref)
            dsink_ref[...] = jnp.zeros_like(dsink_ref)
            dbqkv_ref[...] = jnp.zeros_like(dbqkv_ref)
            carry_ref[...] = jnp.zeros_like(carry_ref)

        def roped_k(b):
            rows = slice(b * T, (b + 1) * T)
            return _rope(p_ref[rows, OFF_K:OFF_K + 128], cos_ref[rows, :], sin_ref[rows, :], first_half)

        def block(b, k_prev, v_prev, has_prev, dk_next, dv_next):
            rows = slice(b * T, (b + 1) * T)
            xhat, rs = _layer_norm_stats(p_ref[rows, OFF_V:OFF_V + DG])
            lng = lng_ref[...]
            vn = xhat * lng + lnb_ref[...]
            for g in range(NG):
                sl = slice(128 * g, 128 * g + 128)
                vng = vn[:, sl].astype(BF16)
                mixed = _dot(wm_ref[g], vng, NN) + bs_ref[g]
                z = p_ref[rows, OFF_ZA + 128 * g:OFF_ZA + 128 * g + 128]
                u = p_ref[rows, OFF_U + 128 * g:OFF_U + 128 * g + 128]
                dy = dyc_ref[rows, sl]
                sg = _sigmoid(z)
                sa = z * sg
                dp_ref[rows, OFF_U + 128 * g:OFF_U + 128 * g + 128] = (dy * mixed * sa).astype(BF16)
                dp_ref[rows, OFF_ZA + 128 * g:OFF_ZA + 128 * g + 128] = (
                    dy * u * mixed * (sg * (1.0 + z * (1.0 - sg)))).astype(BF16)
                dm = dy * u * sa
                dmb = dm.astype(BF16)
                dvn_ref[rows, sl] = _dot(wmt_ref[g], dmb, NN)
                dws_ref[g] += jnp.where(tril, _dot(dmb, vng, NT), 0.0)
                dbs_ref[g] += jnp.sum(dm, axis=1, keepdims=True)
            dvn = dvn_ref[rows, :]
            dlng_ref[...] += jnp.sum(dvn * xhat, axis=0, keepdims=True)
            dlnb_ref[...] += jnp.sum(dvn, axis=0, keepdims=True)
            dxh = dvn * lng
            dv_g = rs * (dxh - _row_mean(dxh)
                         - xhat * _row_mean(dxh * xhat))
            dp_ref[rows, OFF_V:OFF_V + DG] = dv_g.astype(BF16)

            cosv, sinv = cos_ref[rows, :], sin_ref[rows, :]
            k_cur = roped_k(b)
            v_cur = p_ref[rows, OFF_VA:OFF_VA + 128]

            def q_slab(j):
                return q_ref[rows, 128 * j:128 * j + 128]

            def do_slab(j):
                zb = p_ref[rows, OFF_ZB + 128 * j:OFF_ZB + 128 * j + 128]
                return dyc_ref[rows, DG + 128 * j:DG + 128 * j + 128] * (zb * _sigmoid(zb))

            k_both = jnp.concatenate([k_prev, k_cur], axis=0)
            v_both = jnp.concatenate([v_prev, v_cur], axis=0)
            dk_both, dv_both = jnp.zeros((2 * T, 128), F32), jnp.zeros((2 * T, 128), F32)
            for grp in range(2):
                qs = _stack_heads(q_slab, grp, lo64)
                d_o = _stack_heads(do_slab, grp, lo64)
                dob = d_o.astype(BF16)
                kg, vg = _both_halves(k_both, grp).astype(BF16), _both_halves(v_both, grp).astype(BF16)
                p, ps = _band_softmax(qs, kg, own, has_prev, sink_ref[GH * T * grp:GH * T * (grp + 1), :])
                p_both = _unfold(p, own)
                o_st = _dot(p_both, vg, NN)
                delta = jnp.sum(d_o * o_st, axis=-1, keepdims=True)
                ds_both = _unfold(p * (_fold(_dot(dob, vg, NT), own) - delta), own)
                dq_st = _dot(ds_both, kg, NN) * (HD ** -0.5)
                dk_both = dk_both + _one_half(_dot(ds_both, qs, TN), grp)
                dv_both = dv_both + _one_half(_dot(p_both, dob, TN), grp)
                dsink_rows = ps * delta
                for hh in range(GH):
                    h = GH * grp + hh
                    dsink_ref[h:h + 1, :] += jnp.broadcast_to(
                        -jnp.sum(dsink_rows[hh * T:(hh + 1) * T], axis=0, keepdims=True), (1, 128))
                for jj in range(GH // 2):
                    c0 = 128 * (GH // 2 * grp + jj)
                    zb = p_ref[rows, OFF_ZB + c0:OFF_ZB + c0 + 128]
                    sg = _sigmoid(zb)
                    o = _unstack_heads(o_st, jj, lo64)
                    dp_ref[rows, OFF_ZB + c0:OFF_ZB + c0 + 128] = (
                        dyc_ref[rows, DG + c0:DG + c0 + 128] * o * (sg * (1.0 + zb * (1.0 - sg)))).astype(BF16)
                    dq = _unstack_heads(dq_st, jj, lo64)
                    dq_pre = dq * cosv - _rot_half(dq, first_half) * sinv
                    dp_ref[rows, OFF_Q + c0:OFF_Q + c0 + 128] = dq_pre.astype(BF16)
                    dbqkv_ref[:, c0:c0 + 128] += jnp.sum(dq_pre, axis=0, keepdims=True)
            dk_prev, dv_prev = dk_both[0:T], dv_both[0:T]
            dk_cur, dv_cur = dk_both[T:2 * T] + dk_next, dv_both[T:2 * T] + dv_next
            dk_pre = dk_cur * cosv - _rot_half(dk_cur, first_half) * sinv
            dp_ref[rows, OFF_K:OFF_K + 128] = dk_pre.astype(BF16)
            dp_ref[rows, OFF_VA:OFF_VA + 128] = dv_cur.astype(BF16)
            dbqkv_ref[:, 1024:1152] += jnp.sum(dk_pre, axis=0, keepdims=True)
            dbqkv_ref[:, 1152:1280] += jnp.sum(dv_cur, axis=0, keepdims=True)
            return dk_prev, dv_prev

        grads = carry_ref[:, 0:128], carry_ref[:, 128:256]
        for b in range(BPS - 1, 0, -1):
            prows = slice((b - 1) * T, b * T)
            grads = block(b, roped_k(b - 1), p_ref[prows, OFF_VA:OFF_VA + 128], None, *grads)
        k_prev = _rope(kvp_ref[:, 0:128], cosp_ref[...], sinp_ref[...], first_half)
        grads = block(0, k_prev, kvp_ref[:, 128:256], i < nt - 1, *grads)
        carry_ref[:, 0:128], carry_ref[:, 128:256] = grads
        pl.when(i == nt - 1)(exchange.finish)

    return pl.pallas_call(
        body, name="mid_bwd", grid=(nt,),
        in_specs=proj_specs + [pl.BlockSpec((BPS * T, D), lambda i: (tile(i), 0)),
                               pl.BlockSpec((BPS * T, NQ * HD), lambda i: (tile(i), 0))] + par_specs + [HBM_SPEC],
        out_specs=[pl.BlockSpec((BPS * T, D_IN), lambda i: (tile(i), 0)),
                   pl.BlockSpec((1, DG), const2), pl.BlockSpec((1, DG), const2),
                   pl.BlockSpec((NG, T, T), lambda i: (0, 0, 0)), pl.BlockSpec((NG, T, 1), lambda i: (0, 0, 0)),
                   pl.BlockSpec((NQ, 128), const2), pl.BlockSpec((1, D_QKV), const2), HBM_SPEC],
        out_shape=[jax.ShapeDtypeStruct((S, D_IN), BF16),
                   jax.ShapeDtypeStruct((1, DG), F32), jax.ShapeDtypeStruct((1, DG), F32),
                   jax.ShapeDtypeStruct((NG, T, T), F32), jax.ShapeDtypeStruct((NG, T, 1), F32),
                   jax.ShapeDtypeStruct((NQ, 128), F32), jax.ShapeDtypeStruct((1, D_QKV), F32),
                   jax.ShapeDtypeStruct(po.shape, BF16)],
        scratch_shapes=[pltpu.VMEM((T, 2 * T), F32), pltpu.VMEM((BPS * T, DG), F32),
                        pltpu.VMEM((NG, T, T), BF16), pltpu.VMEM((NG, T, T), BF16)] + OWNER_SEMS,
        compiler_params=_params(("arbitrary",)),
    )(proj, proj, dycat, q_roped, cos, sin, cos, sin, ln_g, ln_b, w_s, b_s, sinks, po)


def _outproj_loss(ycat, wo, x, target, g_post):
    S = ycat.shape[0]
    tm = _tile(S, 512)
    nt = S // tm
    n_part = 2 if tm % 32 == 0 else 1
    tp = tm // n_part
    const2 = lambda i: (0, 0)

    def body(yc_ref, w_ref, x_ref, t_ref, g_ref, dy_ref, dout_ref, loss_ref, dg_ref, lacc_ref):
        i = pl.program_id(0)

        @pl.when(i == 0)
        def _():
            dg_ref[...] = jnp.zeros_like(dg_ref)
            lacc_ref[...] = jnp.zeros_like(lacc_ref)

        g = g_ref[...]
        ys = [_dot(yc_ref[q * tp:(q + 1) * tp, :], w_ref[...], NN) for q in range(n_part)]
        for q, y in enumerate(ys):
            rows = slice(q * tp, (q + 1) * tp)
            r = lax.rsqrt(_row_mean(y * y) + EPS)
            yh = y * r
            diff = x_ref[rows, :] + yh * g - t_ref[rows, :]
            lacc_ref[...] += jnp.sum(diff * diff, axis=0, keepdims=True)
            dout = diff * (1.0 / D)
            dout_ref[rows, :] = dout
            dg_ref[...] += jnp.sum(dout * yh, axis=0, keepdims=True)
            dyh = dout * g
            dy_ref[rows, :] = (r * (dyh - yh * _row_mean(dyh * yh))).astype(BF16)

        @pl.when(i == nt - 1)
        def _():
            loss_ref[...] = jnp.broadcast_to(jnp.sum(lacc_ref[...], axis=1, keepdims=True) * (0.5 / D), (1, 128))

    row = lambda i: (i, 0)
    return pl.pallas_call(
        body, name="outproj_loss", grid=(nt,),
        in_specs=[pl.BlockSpec((tm, D), row), pl.BlockSpec((D, D), const2, pipeline_mode=pl.Buffered(1)),
                  pl.BlockSpec((tm, D), row), pl.BlockSpec((tm, D), row), pl.BlockSpec((1, D), const2)],
        out_specs=[pl.BlockSpec((tm, D), row), pl.BlockSpec((tm, D), row), pl.BlockSpec((1, 128), const2),
                   pl.BlockSpec((1, D), const2)],
        out_shape=[jax.ShapeDtypeStruct((S, D), BF16), jax.ShapeDtypeStruct((S, D), F32),
                   jax.ShapeDtypeStruct((1, 128), F32), jax.ShapeDtypeStruct((1, D), F32)],
        scratch_shapes=[pltpu.VMEM((1, D), F32)],
        compiler_params=_params(("arbitrary",)),
    )(ycat, wo, x, target, g_post)


def _dycat(dy, wo, dwo):
    S = dy.shape[0]
    tm = _tile(S, 512)
    nt = S // tm

    def body(dy_ref, w_ref, dwo_ref, o_ref, r_ref, send_sems, recv_sems):
        i = pl.program_id(0)
        copies = _sibling_copies(dwo_ref, r_ref, W_OUT_ROWS, send_sems, recv_sems)

        @pl.when(i == 0)
        def _():
            for cp in copies:
                cp.start()

        o_ref[...] = _dot(dy_ref[...], w_ref[...], NT)

        @pl.when(i == nt - 1)
        def _():
            for cp in copies:
                cp.wait_recv()
            for cp in copies:
                cp.wait_send()

    return pl.pallas_call(
        body, name="dycat", grid=(nt,),
        in_specs=[pl.BlockSpec((tm, D), lambda i: (i, 0)),
                  pl.BlockSpec((D, D), lambda i: (0, 0), pipeline_mode=pl.Buffered(1)), HBM_SPEC],
        out_specs=[pl.BlockSpec((tm, D), lambda i: (i, 0)), HBM_SPEC],
        out_shape=[jax.ShapeDtypeStruct((S, D), F32), jax.ShapeDtypeStruct((N_CHIPS, W_OUT_ROWS, D), BF16)],
        scratch_shapes=[pltpu.SemaphoreType.DMA((N_CHIPS,)), pltpu.SemaphoreType.DMA((N_CHIPS,))],
        compiler_params=_params(("arbitrary",)),
    )(dy, wo, dwo)


def _matmul_tn(a, b, tm, name, mat_grads=None):
    K, M = a.shape
    N = b.shape[1]
    tk = _tile(K, 1024)
    ni, nk = M // tm, K // tk
    hosting = mat_grads is not None

    def body(a_ref, b_ref, *rest):
        if hosting:
            g_in, (o_ref, tot_ref, acc_ref, mbuf, send_sems, recv_sems) = rest[:len(MAT)], rest[len(MAT):]
        else:
            o_ref, acc_ref = rest
        i, k = pl.program_id(0), pl.program_id(1)

        if hosting:
            gather = _SlotGather(mbuf, send_sems, recv_sems)

            @pl.when((i == 0) & (k == 0))
            def _():
                r0 = 0
                for q, (_, rows) in enumerate(MAT):
                    mbuf[gather.my_id, r0:r0 + rows, :] = g_in[q][...]
                    r0 += rows
                gather.start()

        @pl.when(k == 0)
        def _():
            acc_ref[...] = jnp.zeros_like(acc_ref)

        acc_ref[...] += _dot(a_ref[...], b_ref[...], TN)

        @pl.when(k == nk - 1)
        def _():
            o_ref[...] = acc_ref[...].astype(BF16)

        if hosting:
            @pl.when((i == ni - 1) & (k == nk - 1))
            def _():
                gather.finish()
                tot_ref[...] = gather.total()

    whole = pl.BlockSpec(memory_space=pltpu.VMEM)
    in_specs = [pl.BlockSpec((tk, tm), lambda i, k: (k, i)), pl.BlockSpec((tk, N), lambda i, k: (k, 0))]
    out_specs = [pl.BlockSpec((tm, N), lambda i, k: (i, 0))]
    out_shape = [jax.ShapeDtypeStruct((M, N), BF16)]
    scratch = [pltpu.VMEM((tm, N), F32)]
    if hosting:
        in_specs += [whole] * len(MAT)
        out_specs.append(whole)
        out_shape.append(jax.ShapeDtypeStruct((MAT_ROWS, 128), F32))
        scratch += [pltpu.VMEM((N_DEV, MAT_ROWS, 128), F32), pltpu.SemaphoreType.DMA((7,)),
                    pltpu.SemaphoreType.DMA((7,))]
    res = pl.pallas_call(
        body, name=name, grid=(ni, nk), in_specs=in_specs, out_specs=out_specs, out_shape=out_shape,
        scratch_shapes=scratch,
        compiler_params=_params(("arbitrary", "arbitrary") if hosting else ("parallel", "arbitrary")),
    )(a, b, *(mat_grads or ()))
    return res if hosting else res[0]


def _dh_prenorm_bwd(dproj, wt, x, dout, g_pre, pt):
    S = x.shape[0]
    tm, tk = _tile(S, 1024), 768
    ne = 4
    te = tm // ne
    ni, nk = S // tm, D_IN // tk
    n_mm = ni * nk
    n_steps = n_mm + ne
    mm_of = lambda t: jnp.minimum(t, n_mm - 1)

    def quarter_of(t):
        j, q = t // nk - 1, t % nk
        nxt = jnp.where(q < ne, j * ne + q, (j + 1) * ne)
        return jnp.clip(nxt, 0, ni * ne - 1)

    def body(dp_ref, w_ref, x_ref, dout_ref, g_ref, pt_ref, gx_ref, dg_ref, bt_ref, acc_ref,
             send_sems, recv_sems, local_sem):
        t = pl.program_id(0)
        exchange = _OwnerExchange(pt_ref, bt_ref, send_sems, recv_sems, local_sem)
        pl.when(t == 0)(exchange.start)

        @pl.when(t == 0)
        def _():
            dg_ref[...] = jnp.zeros_like(dg_ref)

        has_mm = t < n_mm
        has_q = (t >= nk) & (t % nk < ne)

        def matmul():
            slot = (mm_of(t) // nk) % 2
            old = jnp.where(mm_of(t) % nk > 0, acc_ref[slot], 0.0)
            acc_ref[slot] = old + _dot(dp_ref[...], w_ref[...], NN)

        def quarter():
            slot = (t // nk - 1) % 2
            dh = acc_ref[slot, pl.ds(pl.multiple_of((t % nk) * te, te), te), :]
            xv = x_ref[...]
            r = lax.rsqrt(_row_mean(xv * xv) + EPS)
            xh = xv * r
            dg_ref[...] += jnp.sum(dh * xh, axis=0, keepdims=True)
            dxh = dh * g_ref[...]
            gx_ref[...] = dout_ref[...] + r * (dxh - xh * _row_mean(dxh * xh))

        @pl.when(has_mm & has_q)
        def _():
            matmul()
            quarter()

        pl.when(has_mm & jnp.logical_not(has_q))(matmul)
        pl.when(jnp.logical_not(has_mm) & has_q)(quarter)
        pl.when(t == n_steps - 1)(exchange.finish)

    quarter_block = lambda t: (quarter_of(t), 0)
    return pl.pallas_call(
        body, name="dh_prenorm_bwd", grid=(n_steps,),
        in_specs=[pl.BlockSpec((tm, tk), lambda t: (mm_of(t) // nk, mm_of(t) % nk)),
                  pl.BlockSpec((tk, D), lambda t: (mm_of(t) % nk, 0)),
                  pl.BlockSpec((te, D), quarter_block), pl.BlockSpec((te, D), quarter_block),
                  pl.BlockSpec((1, D), lambda t: (0, 0)), HBM_SPEC],
        out_specs=[pl.BlockSpec((te, D), quarter_block), pl.BlockSpec((1, D), lambda t: (0, 0)), HBM_SPEC],
        out_shape=[jax.ShapeDtypeStruct((S, D), F32), jax.ShapeDtypeStruct((1, D), F32),
                   jax.ShapeDtypeStruct(pt.shape, BF16)],
        scratch_shapes=[pltpu.VMEM((2, tm, D), F32)] + OWNER_SEMS,
        compiler_params=_params(("arbitrary",)),
    )(dproj, wt, x, dout, g_pre, pt)


def _presum(c_arr, own, recv, half_rows):
    n_cols = own.shape[-1]
    own4 = own.reshape(N_CHIPS, 2, half_rows, n_cols)

    def body(c_ref, own_ref, recv_ref, o_ref):
        o_ref[...] = (own_ref[...].astype(F32) + recv_ref[...].astype(F32)).astype(BF16)

    return pl.pallas_call(
        body, name="presum_%d" % half_rows,
        grid_spec=pltpu.PrefetchScalarGridSpec(
            num_scalar_prefetch=1, grid=(N_CHIPS,),
            in_specs=[pl.BlockSpec((None, None, half_rows, n_cols), lambda j, c: (j, c[0], 0, 0)),
                      pl.BlockSpec((None, half_rows, n_cols), lambda j, c: (j, 0, 0))],
            out_specs=pl.BlockSpec((None, half_rows, n_cols), lambda j, c: (j, 0, 0))),
        out_shape=jax.ShapeDtypeStruct((N_CHIPS, half_rows, n_cols), BF16),
        compiler_params=_params(("parallel",)),
    )(c_arr, own4, recv)


def _sibling_presum(c_arr, dw, half_rows, name):
    n_cols = dw.shape[-1]
    dw4 = dw.reshape(N_CHIPS, 2, half_rows, n_cols)

    def body(c_ref, own_ref, dw_ref, o_ref, r_ref, stage, send_sems, recv_sems, load_sem):
        j = pl.program_id(0)
        copies = _sibling_copies(dw_ref, r_ref, half_rows, send_sems, recv_sems)

        @pl.when(j == 0)
        def _():
            for cp in copies:
                cp.start()

        for q in range(N_CHIPS):
            pl.when(j == q)(copies[q].wait_recv)
        load = pltpu.make_async_copy(r_ref.at[j], stage, load_sem)
        load.start()
        load.wait()
        o_ref[...] = (own_ref[...].astype(F32) + stage[...].astype(F32)).astype(BF16)

        @pl.when(j == N_CHIPS - 1)
        def _():
            for cp in copies:
                cp.wait_send()

    out, _ = pl.pallas_call(
        body, name=name,
        grid_spec=pltpu.PrefetchScalarGridSpec(
            num_scalar_prefetch=1, grid=(N_CHIPS,),
            in_specs=[pl.BlockSpec((None, None, half_rows, n_cols), lambda j, c: (j, c[0], 0, 0)), HBM_SPEC],
            out_specs=[pl.BlockSpec((None, half_rows, n_cols), lambda j, c: (j, 0, 0)), HBM_SPEC],
            scratch_shapes=[pltpu.VMEM((half_rows, n_cols), BF16), pltpu.SemaphoreType.DMA((N_CHIPS,)),
                            pltpu.SemaphoreType.DMA((N_CHIPS,)), pltpu.SemaphoreType.DMA(())]),
        out_shape=[jax.ShapeDtypeStruct((N_CHIPS, half_rows, n_cols), BF16),
                   jax.ShapeDtypeStruct((N_CHIPS, half_rows, n_cols), BF16)],
        compiler_params=_params(("arbitrary",)),
    )(c_arr, dw4, dw)
    return out


def _sum_chips(c_arr, parts, name):
    _, rows, n_cols = parts.shape
    nt = 2
    tr = rows // nt

    def body(c_ref, p_ref, o_ref):
        o_ref[...] = ((p_ref[0].astype(F32) + p_ref[1].astype(F32)) + p_ref[2].astype(F32)) + p_ref[3].astype(F32)

    return pl.pallas_call(
        body, name=name,
        grid_spec=pltpu.PrefetchScalarGridSpec(
            num_scalar_prefetch=1, grid=(nt,),
            in_specs=[pl.BlockSpec((N_CHIPS, tr, n_cols), lambda i, c: (0, i, 0))],
            out_specs=pl.BlockSpec((tr, n_cols), lambda i, c: (c[0] * nt + i, 0))),
        out_shape=jax.ShapeDtypeStruct((2 * rows, n_cols), F32),
        compiler_params=_params(("parallel",)),
    )(c_arr, parts)


def _adamw_math(w, g, m, v):
    mn = ADAM_B1 * m + (1.0 - ADAM_B1) * g
    vn = ADAM_B2 * v + (1.0 - ADAM_B2) * (g * g)
    m_hat = mn / (1.0 - ADAM_B1 ** ADAM_STEP)
    v_hat = vn / (1.0 - ADAM_B2 ** ADAM_STEP)
    return -ADAM_LR * (m_hat / (jnp.sqrt(v_hat) + ADAM_EPS) + ADAM_WD * w), mn, vn


def _adamw(w, g, m, v, name):
    R, C = w.shape
    tr = next((t for t in (256, 192, 128) if R % t == 0), R)

    def body(w_ref, g_ref, m_ref, v_ref, go_ref, d_ref, mo_ref, vo_ref):
        gv = g_ref[...]
        go_ref[...] = gv
        d_ref[...], mo_ref[...], vo_ref[...] = _adamw_math(w_ref[...], gv, m_ref[...], v_ref[...])

    spec = pl.BlockSpec((tr, C), lambda i: (i, 0))
    shp = jax.ShapeDtypeStruct((R, C), F32)
    return pl.pallas_call(
        body, name=name, grid=(R // tr,), in_specs=[spec] * 4, out_specs=[spec] * 4, out_shape=[shp] * 4,
        compiler_params=_params(("parallel",)),
    )(w, g, m, v)


HBM_SPEC = pl.BlockSpec(memory_space=pltpu.HBM)
GATHER_LOCAL_CHUNKS = 4
GATHER_SEMS = [pltpu.SemaphoreType.DMA((7,)), pltpu.SemaphoreType.DMA((7,)),
               pltpu.SemaphoreType.DMA((GATHER_LOCAL_CHUNKS,))]
OWNER_SEMS = [pltpu.SemaphoreType.DMA((3,)), pltpu.SemaphoreType.DMA((3,)), pltpu.SemaphoreType.DMA(())]


def _mesh_pos():
    return lax.axis_index("x"), lax.axis_index("y"), lax.axis_index("c")


class _RowGather:
    def __init__(self, src_ref, full_ref, rows, send_sems, recv_sems, local_sems):
        self.src, self.full, self.rows = src_ref, full_ref, rows
        self.send, self.recv, self.local = send_sems, recv_sems, local_sems
        x, y, c = _mesh_pos()
        self.c, self.me, self.sibling = c, (x, y, c), (x, y, 1 - c)
        self.chips = [(1 - x, y), (x, 1 - y), (1 - x, 1 - y)]

    def _block(self, pos):
        px, py, pc = pos
        return self.full.at[pl.ds(pl.multiple_of((4 * px + 2 * py + pc) * self.rows, 16), self.rows), :]

    def _copy(self, k, blk, to):
        return pltpu.make_async_remote_copy(
            src_ref=self.src if blk is self.me else self._block(blk), dst_ref=self._block(blk),
            send_sem=self.send.at[k], recv_sem=self.recv.at[k], device_id=to, device_id_type=MESH)

    def _mine(self):
        return _place_locally(self.src, self._block(self.me), self.local, GATHER_LOCAL_CHUNKS)

    def _first(self):
        return [self._copy(0, self.me, self.sibling)] + [
            self._copy(1 + j, self.me, (*chip, self.c)) for j, chip in enumerate(self.chips)]

    def start(self):
        for cp in self._first() + self._mine():
            cp.start()

    def finish(self):
        passed = [self._copy(4 + j, (*chip, self.c), self.sibling) for j, chip in enumerate(self.chips)]
        for j, chip in enumerate(self.chips):
            self._copy(1 + j, (*chip, self.c), self.me).wait_recv()
            passed[j].start()
        self._copy(0, self.sibling, self.me).wait_recv()
        for j, chip in enumerate(self.chips):
            self._copy(4 + j, (*chip, 1 - self.c), self.me).wait_recv()
        for cp in self._first() + passed:
            cp.wait_send()
        for cp in self._mine():
            cp.wait()


class _OwnerExchange:
    def __init__(self, src_ref, dst_ref, send_sems, recv_sems, local_sem):
        self.src, self.dst, self.send, self.recv, self.local = src_ref, dst_ref, send_sems, recv_sems, local_sem
        x, y, c = _mesh_pos()
        self.c, self.my_chip = c, 2 * x + y
        self.peers = [(1 - x, y), (x, 1 - y), (1 - x, 1 - y)]

    def _copies(self):
        local = pltpu.make_async_copy(self.src.at[self.my_chip], self.dst.at[self.my_chip], self.local)
        remote = [pltpu.make_async_remote_copy(
            src_ref=self.src.at[2 * px + py], dst_ref=self.dst.at[self.my_chip],
            send_sem=self.send.at[k], recv_sem=self.recv.at[k], device_id=(px, py, self.c), device_id_type=MESH)
            for k, (px, py) in enumerate(self.peers)]
        return local, remote

    def start(self):
        local, remote = self._copies()
        local.start()
        for cp in remote:
            cp.start()

    def finish(self):
        local, remote = self._copies()
        for cp in remote:
            cp.wait_recv()
        for cp in remote:
            cp.wait_send()
        local.wait()


def _sibling_copies(dw_ref, r_ref, rows, send_sems, recv_sems):
    x, y, c = _mesh_pos()
    return [pltpu.make_async_remote_copy(
        src_ref=dw_ref.at[pl.ds(pl.multiple_of((2 * j + (1 - c)) * rows, 16), rows), :], dst_ref=r_ref.at[j],
        send_sem=send_sems.at[j], recv_sem=recv_sems.at[j], device_id=(x, y, 1 - c), device_id_type=MESH)
        for j in range(N_CHIPS)]


PAIR_CHUNKS = 4


def _pair_halves(gt, go):
    def body(gt_in, go_in, gt_ref, go_ref, send_sems, recv_sems):
        del gt_in, go_in
        x, y, c = _mesh_pos()
        copies = []
        for a, (ref, rows) in enumerate(((gt_ref, W_IN_ROWS), (go_ref, W_OUT_ROWS))):
            ch = rows // PAIR_CHUNKS
            for q in range(PAIR_CHUNKS):
                part = ref.at[pl.ds(pl.multiple_of(c * rows + q * ch, 8), ch), :]
                copies.append(pltpu.make_async_remote_copy(
                    src_ref=part, dst_ref=part, send_sem=send_sems.at[PAIR_CHUNKS * a + q],
                    recv_sem=recv_sems.at[PAIR_CHUNKS * a + q], device_id=(x, y, 1 - c), device_id_type=MESH))
        for cp in copies:
            cp.start()
        for cp in copies:
            cp.wait_recv()
        for cp in copies:
            cp.wait_send()

    return pl.pallas_call(
        body, name="pair_halves",
        in_specs=[HBM_SPEC, HBM_SPEC], out_specs=[HBM_SPEC, HBM_SPEC],
        out_shape=[jax.ShapeDtypeStruct(gt.shape, F32), jax.ShapeDtypeStruct(go.shape, F32)],
        input_output_aliases={0: 0, 1: 1},
        scratch_shapes=[pltpu.SemaphoreType.DMA((2 * PAIR_CHUNKS,)), pltpu.SemaphoreType.DMA((2 * PAIR_CHUNKS,))],
    )(gt, go)


VEC = (("g_pre", 2048), ("g_post", 2048), ("b_qkv", 1280), ("ln_v_g", 1024), ("ln_v_b", 1024), ("attn_sinks", 16))
VEC_ROWS = 8
LOSS_ROW = len(VEC)
MAT = (("w_spatial", NG * T), ("b_spatial", NG))
MAT_ROWS = sum(r for _, r in MAT)


class _SlotGather:
    def __init__(self, buf, send_sems, recv_sems):
        self.buf, self.send, self.recv = buf, send_sems, recv_sems
        x, y, c = _mesh_pos()
        self.c, self.me, self.sibling, self.my_id = c, (x, y, c), (x, y, 1 - c), 4 * x + 2 * y + c
        self.chips = [(1 - x, y), (x, 1 - y), (1 - x, 1 - y)]

    def _copy(self, k, blk, to):
        px, py, pc = blk
        slot = self.buf.at[4 * px + 2 * py + pc]
        return pltpu.make_async_remote_copy(
            src_ref=slot, dst_ref=slot, send_sem=self.send.at[k], recv_sem=self.recv.at[k],
            device_id=to, device_id_type=MESH)

    def _first(self):
        return [self._copy(0, self.me, self.sibling)] + [
            self._copy(1 + j, self.me, (*chip, self.c)) for j, chip in enumerate(self.chips)]

    def start(self):
        for cp in self._first():
            cp.start()

    def finish(self):
        passed = [self._copy(4 + j, (*chip, self.c), self.sibling) for j, chip in enumerate(self.chips)]
        for j, chip in enumerate(self.chips):
            self._copy(1 + j, (*chip, self.c), self.me).wait_recv()
            passed[j].start()
        self._copy(0, self.sibling, self.me).wait_recv()
        for j, chip in enumerate(self.chips):
            self._copy(4 + j, (*chip, 1 - self.c), self.me).wait_recv()
        for cp in self._first() + passed:
            cp.wait_send()

    def total(self):
        t = self.buf[0]
        for d in range(1, N_DEV):
            t = t + self.buf[d]
        return t


def _small_update(vec_grads, loss_part, mat_total, vec_state, mat_state):
    n_vec, n_mat = len(VEC), len(MAT)
    n_par = n_vec + n_mat
    n_in = n_vec + 2 + 3 * n_par

    def body(*refs):
        g_in, loss_in, tot_m = refs[:n_vec], refs[n_vec], refs[n_vec + 1]
        st_in = refs[n_vec + 2:n_in]
        outs, loss_out = refs[n_in:n_in + 4 * n_par], refs[n_in + 4 * n_par]
        vbuf, tot_v, send_sems, recv_sems = refs[n_in + 4 * n_par + 1:]
        gather = _SlotGather(vbuf, send_sems, recv_sems)
        vbuf[gather.my_id] = jnp.zeros((VEC_ROWS, D), F32)
        for r, (_, n) in enumerate(VEC):
            vbuf[gather.my_id, r:r + 1, 0:n] = g_in[r][...]
        vbuf[gather.my_id, LOSS_ROW:LOSS_ROW + 1, 0:128] = loss_in[...]
        gather.start()
        gather.finish()
        tot_v[...] = gather.total()
        loss_out[...] = tot_v[LOSS_ROW:LOSS_ROW + 1, 0:128]
        r0 = 0
        for q in range(n_par):
            if q < n_vec:
                g = tot_v[q:q + 1, 0:VEC[q][1]]
            else:
                rows = MAT[q - n_vec][1]
                g = tot_m[r0:r0 + rows, :]
                r0 += rows
            w, m, v = (st_in[3 * q + t][...] for t in range(3))
            outs[4 * q][...] = g
            outs[4 * q + 1][...], outs[4 * q + 2][...], outs[4 * q + 3][...] = _adamw_math(w, g, m, v)

    state = [a for wmv in list(vec_state) + list(mat_state) for a in wmv]
    vmem = pl.BlockSpec(memory_space=pltpu.VMEM)
    shapes = [g.shape for g in vec_grads] + [wmv[0].shape for wmv in mat_state]
    out_shape = [jax.ShapeDtypeStruct(s, F32) for s in shapes for _ in range(4)]
    out_shape.append(jax.ShapeDtypeStruct((1, 128), F32))
    res = pl.pallas_call(
        body, name="small_update",
        in_specs=[vmem] * n_in, out_specs=[vmem] * len(out_shape), out_shape=out_shape,
        scratch_shapes=[pltpu.VMEM((N_DEV, VEC_ROWS, D), F32), pltpu.VMEM((VEC_ROWS, D), F32),
                        pltpu.SemaphoreType.DMA((7,)), pltpu.SemaphoreType.DMA((7,))],
        compiler_params=pltpu.CompilerParams(vmem_limit_bytes=VMEM_LIMIT),
    )(*vec_grads, loss_part, mat_total, *state)
    return [res[4 * q:4 * q + 4] for q in range(n_par)], res[-1]


def kernel(x, positions, g_pre, w_in, b_qkv, ln_v_g, ln_v_b, w_spatial, b_spatial, attn_sinks, w_out, g_post, loss_target, m_g_pre, m_w_in, m_b_qkv, m_ln_v_g, m_ln_v_b, m_w_spatial, m_b_spatial, m_attn_sinks, m_w_out, m_g_post, v_g_pre, v_w_in, v_b_qkv, v_ln_v_g, v_ln_v_b, v_w_spatial, v_b_spatial, v_attn_sinks, v_w_out, v_g_post):
    S = x.shape[1]
    c = lax.axis_index("c")
    c_arr = jnp.reshape(c, (1,)).astype(jnp.int32)
    x2 = x[0]
    target = loss_target[0]
    pos = positions.reshape(S, 1)
    half = HD // 2
    inv_freq = ROPE_THETA ** (-jnp.arange(half, dtype=F32) * (2.0 / HD))
    invf = jnp.tile(inv_freq, 128 // half).reshape(1, 128)
    bias = jnp.concatenate([jnp.zeros((OFF_Q,), F32), b_qkv[0], jnp.zeros((D_IN - OFF_ZB,), F32)]).reshape(1, D_IN)
    b_s_col = b_spatial[0].reshape(NG, T, 1)
    sinks = jnp.repeat(attn_sinks[0], T).reshape(NQ * T, 1)

    chip = 2 * lax.axis_index("x") + lax.axis_index("y")
    wt_part = lax.dynamic_slice_in_dim(w_in[0].T.astype(BF16), c * W_IN_ROWS, W_IN_ROWS, axis=0)
    wo_part = lax.dynamic_slice_in_dim(w_out[0].astype(BF16), c * W_OUT_ROWS, W_OUT_ROWS, axis=0)
    sched = jnp.asarray(PROJ_SCHEDULE, jnp.int32)[chip]

    h, proj, wt, cos, sin = _prenorm_inproj(sched, x2, g_pre, bias, wt_part, pos, invf)
    ycat, wo, q_roped = _mid_fwd(proj, cos, sin, ln_v_g, ln_v_b, w_spatial[0], b_s_col, sinks, wo_part)
    dy, dout, loss_part, dg_post = _outproj_loss(ycat, wo, x2, target, g_post)

    dwo = _matmul_tn(ycat, dy, 1024, "dw_out")
    dycat, ro = _dycat(dy, wo, dwo)
    po = _presum(c_arr, dwo, ro, W_OUT_ROWS)
    dproj, dln_g, dln_b, dws, dbs, dsink, dbqkv, bo = _mid_bwd(
        proj, dycat, q_roped, cos, sin, ln_v_g, ln_v_b, w_spatial[0], b_s_col, sinks, po)
    dwt, mat_total = _matmul_tn(dproj, h, 768, "dw_in_t",
                                mat_grads=[dws.reshape(NG * T, T), dbs.reshape(NG, T)])
    pt = _sibling_presum(c_arr, dwt, W_IN_ROWS, "sibling_presum_in")
    grad_x, dg_pre, bt = _dh_prenorm_bwd(dproj, wt, x2, dout, g_pre, pt)
    gt, go = _pair_halves(_sum_chips(c_arr, bt, "sum_chips_in"), _sum_chips(c_arr, bo, "sum_chips_out"))

    g_w_in, d_w_in, nm_w_in, nv_w_in = (a.T for a in _adamw(w_in[0].T, gt, m_w_in[0].T, v_w_in[0].T, "adamw_w_in"))
    g_w_out, d_w_out, nm_w_out, nv_w_out = _adamw(w_out[0], go, m_w_out[0], v_w_out[0], "adamw_w_out")

    state = {"g_pre": (g_pre, m_g_pre, v_g_pre), "g_post": (g_post, m_g_post, v_g_post),
             "b_qkv": (b_qkv, m_b_qkv, v_b_qkv), "ln_v_g": (ln_v_g, m_ln_v_g, v_ln_v_g),
             "ln_v_b": (ln_v_b, m_ln_v_b, v_ln_v_b), "attn_sinks": (attn_sinks, m_attn_sinks, v_attn_sinks),
             "w_spatial": tuple(a.reshape(NG * T, T) for a in (w_spatial, m_w_spatial, v_w_spatial)),
             "b_spatial": tuple(a.reshape(NG, T) for a in (b_spatial, m_b_spatial, v_b_spatial))}
    grads = {"g_pre": dg_pre, "g_post": dg_post, "b_qkv": dbqkv, "ln_v_g": dln_g, "ln_v_b": dln_b,
             "attn_sinks": dsink[:, 0].reshape(1, NQ)}
    results, loss = _small_update([grads[n] for n, _ in VEC], loss_part, mat_total,
                                  [state[n] for n, _ in VEC], [state[n] for n, _ in MAT])
    small = {n: [a.reshape(w.shape) for a in res]
             for (n, _), res, w in zip(VEC + MAT, results, [state[n][0] for n, _ in VEC + MAT])}
    small["w_spatial"] = [a.reshape(w_spatial.shape) for a in small["w_spatial"]]
    small["b_spatial"] = [a.reshape(b_spatial.shape) for a in small["b_spatial"]]
    big = {"w_in": [a[None] for a in (g_w_in, d_w_in, nm_w_in, nv_w_in)],
           "w_out": [a[None] for a in (g_w_out, d_w_out, nm_w_out, nv_w_out)]}
    order = ("g_pre", "w_in", "b_qkv", "ln_v_g", "ln_v_b", "w_spatial", "b_spatial", "attn_sinks", "w_out", "g_post")
    leaves = {**small, **big}
    return (loss[0, 0], grad_x[None], *[leaves[n][t] for t in range(4) for n in order])
```

```python
import jax
import jax.numpy as jnp
from jax import lax
from jax.experimental import pallas as pl
from jax.experimental.pallas import tpu as pltpu

F32 = jnp.float32
BF16 = jnp.bfloat16
MESH = pl.DeviceIdType.MESH

D = 2048
DG = 1024
T = 128
NG = 8
HD = 64
NQ = 16
D_IN = 5376
OFF_U, OFF_V, OFF_ZA, OFF_Q, OFF_K, OFF_VA, OFF_ZB = 0, 1024, 2048, 3072, 4096, 4224, 4352
D_QKV = 1280
EPS = 1e-6
ROPE_THETA = 10000.0
N_CHIPS = 4
N_DEV = 8
W_IN_ROWS = D_IN // N_DEV
W_OUT_ROWS = D // N_DEV

ADAM_LR, ADAM_B1, ADAM_B2, ADAM_EPS, ADAM_WD, ADAM_STEP = 0.001, 0.9, 0.999, 1e-08, 0.01, 10

VMEM_LIMIT = 56 * 1024 * 1024


def _tile(n, pref):
    return pref if n % pref == 0 else n


def _params(sem=None, vmem=VMEM_LIMIT):
    return pltpu.CompilerParams(dimension_semantics=sem, vmem_limit_bytes=vmem)


def _sigmoid(z):
    return 1.0 / (1.0 + jnp.exp(-z))


def _row_mean(v):
    return jnp.mean(v, axis=-1, keepdims=True)


def _dot(a, b, dims):
    return lax.dot_general(a, b, (dims, ((), ())), preferred_element_type=F32)


NN = ((1,), (0,))
NT = ((1,), (1,))
TN = ((0,), (0,))


PROJ_TN = 768
N_PROJ_TILES = D_IN // PROJ_TN
PROJ_SCHEDULE = ((0, 1, 2, 4, 3, 6, 5, 5), (2, 1, 0, 6, 4, 3, 5, 4), (4, 5, 6, 0, 2, 1, 3, 4), (6, 5, 4, 2, 3, 0, 1, 5))


def _place_locally(src_ref, dst_rows_ref, sems, n_chunks):
    ch = src_ref.shape[0] // n_chunks
    return [pltpu.make_async_copy(src_ref.at[pl.ds(q * ch, ch), :], dst_rows_ref.at[pl.ds(q * ch, ch), :],
                                  sems.at[q]) for q in range(n_chunks)]


def _prenorm_inproj(sched, x, g, bias, wt_part, pos, invf):
    S = x.shape[0]
    tp, tm = _tile(S, 512), _tile(S, 1024)
    n_pre, ns = S // tp, S // tm
    n_steps = n_pre + N_PROJ_TILES * ns
    pos_of = lambda i: jnp.maximum(i - n_pre, 0) // ns
    row_of = lambda i: jnp.maximum(i - n_pre, 0) % ns

    def body(sched_ref, x_ref, g_ref, b_ref, wpart_ref, pos_ref, invf_ref, h_ref, proj_ref, wt_ref, cos_ref, sin_ref,
             h_all, w_tile, stage, send_sems, recv_sems, w_sems, local_sems):
        i = pl.program_id(0)
        x_, y_, c = _mesh_pos()
        me, sibling = (x_, y_, c), (x_, y_, 1 - c)
        chips = [(1 - x_, y_), (x_, 1 - y_), (1 - x_, 1 - y_)]

        def block(pos):
            px, py, pc = pos
            return wt_ref.at[pl.ds(pl.multiple_of((4 * px + 2 * py + pc) * W_IN_ROWS, 16), W_IN_ROWS), :]

        def copy(k, blk, to):
            return pltpu.make_async_remote_copy(
                src_ref=wpart_ref if blk is me else block(blk), dst_ref=block(blk),
                send_sem=send_sems.at[k], recv_sem=recv_sems.at[k], device_id=to, device_id_type=MESH)

        stage_in = pltpu.make_async_copy(wpart_ref, stage, local_sems.at[0])
        stage_out = pltpu.make_async_copy(stage, block(me), local_sems.at[1])

        relay = (jnp.where(c == 0, x_, 1 - x_), jnp.where(c == 0, 1 - y_, y_))
        relayed = (jnp.where(c == 0, 1 - x_, x_), jnp.where(c == 0, y_, 1 - y_))

        early = c == y_
        y_send = copy(2, me, (*chips[1], c))

        def own_sends():
            return [copy(0, me, sibling), y_send, copy(1, me, (*chips[0], c))]

        def passed_on():
            return [copy(4, (*chips[0], c), sibling), copy(5, (*chips[1], c), sibling),
                    copy(3, (*relayed, c), (*relay, c))]

        def early_block_arrives():
            @pl.when(early)
            def _():
                copy(2, (*chips[1], c), me).wait_recv()
                copy(5, (*chips[1], c), sibling).start()
                y_send.start()

            pl.when(jnp.logical_not(early))(lambda: copy(5, (*chips[1], 1 - c), me).wait_recv())

        def neighbours_arrive():
            copy(1, (*chips[0], c), me).wait_recv()
            copy(4, (*chips[0], c), sibling).start()

            @pl.when(early)
            def _():
                copy(3, (*relayed, c), (*relay, c)).start()
                copy(5, (*chips[1], 1 - c), me).wait_recv()

            @pl.when(jnp.logical_not(early))
            def _():
                copy(2, (*chips[1], c), me).wait_recv()
                copy(5, (*chips[1], c), sibling).start()
                copy(3, (*relayed, c), (*relay, c)).start()

        def diagonal_arrives():
            copy(3, (*chips[2], c), me).wait_recv()
            copy(6, (*chips[2], c), sibling).start()
            copy(6, (*chips[2], 1 - c), me).wait_recv()

        def tile_load(p):
            slot = p % 2
            rows = wt_ref.at[pl.ds(pl.multiple_of(sched_ref[p] * PROJ_TN, 16), PROJ_TN), :]
            return pltpu.make_async_copy(rows, w_tile.at[slot], w_sems.at[slot])

        def prepare(p):
            p = jnp.asarray(p, jnp.int32)

            @pl.when(p == 0)
            def _():
                copy(0, sibling, me).wait_recv()
                stage_out.wait()

            pl.when(p == 1)(early_block_arrives)
            pl.when(p == 2)(neighbours_arrive)
            pl.when(p == 3)(lambda: copy(4, (*chips[0], 1 - c), me).wait_recv())
            pl.when(p == sched_ref[N_PROJ_TILES])(diagonal_arrives)
            tile_load(p).start()

        @pl.when(i == 0)
        def _():
            stage_in.start()
            copy(0, me, sibling).start()
            copy(1, me, (*chips[0], c)).start()
            pl.when(jnp.logical_not(early))(y_send.start)
            stage_in.wait()
            stage_out.start()

        @pl.when(i < n_pre)
        def _():
            xv = x_ref[...]
            r = lax.rsqrt(_row_mean(xv * xv) + EPS)
            hv = (xv * r * g_ref[...]).astype(BF16)
            h_ref[...] = hv
            h_all[pl.ds(pl.multiple_of(i * tp, tp), tp), :] = hv
            ang = pos_ref[...].astype(F32) * invf_ref[...]
            cos_ref[...] = jnp.cos(ang)
            sin_ref[...] = jnp.sin(ang)

        pl.when(i == n_pre - 1)(lambda: prepare(0))

        @pl.when(i >= n_pre)
        def _():
            p, s = pos_of(i), row_of(i)
            pl.when(s == 0)(lambda: tile_load(p).wait())
            pl.when((s == ns - 1) & (p < N_PROJ_TILES - 1))(lambda: prepare(p + 1))
            hv = h_all[pl.ds(pl.multiple_of(s * tm, tm), tm), :]
            proj_ref[...] = _dot(hv, w_tile[p % 2], NT) + b_ref[...]

        @pl.when(i == n_steps - 1)
        def _():
            for cp in own_sends() + passed_on() + [copy(6, (*chips[2], c), sibling)]:
                cp.wait_send()

    return pl.pallas_call(
        body, name="prenorm_inproj",
        grid_spec=pltpu.PrefetchScalarGridSpec(
            num_scalar_prefetch=1, grid=(n_steps,),
            in_specs=[pl.BlockSpec((tp, D), lambda i, sc: (jnp.minimum(i, n_pre - 1), 0)),
                      pl.BlockSpec((1, D), lambda i, sc: (0, 0)),
                      pl.BlockSpec((1, PROJ_TN), lambda i, sc: (0, sc[pos_of(i)])),
                      HBM_SPEC,
                      pl.BlockSpec((tp, 1), lambda i, sc: (jnp.minimum(i, n_pre - 1), 0)),
                      pl.BlockSpec((1, 128), lambda i, sc: (0, 0))],
            out_specs=[pl.BlockSpec((tp, D), lambda i, sc: (jnp.minimum(i, n_pre - 1), 0)),
                       pl.BlockSpec((tm, PROJ_TN), lambda i, sc: (row_of(i), sc[pos_of(i)])),
                       HBM_SPEC,
                       pl.BlockSpec((tp, 128), lambda i, sc: (jnp.minimum(i, n_pre - 1), 0)),
                       pl.BlockSpec((tp, 128), lambda i, sc: (jnp.minimum(i, n_pre - 1), 0))],
            scratch_shapes=[pltpu.VMEM((S, D), BF16), pltpu.VMEM((2, PROJ_TN, D), BF16),
                            pltpu.VMEM((W_IN_ROWS, D), BF16),
                            pltpu.SemaphoreType.DMA((7,)), pltpu.SemaphoreType.DMA((7,)),
                            pltpu.SemaphoreType.DMA((2,)), pltpu.SemaphoreType.DMA((2,))]),
        out_shape=[jax.ShapeDtypeStruct((S, D), BF16), jax.ShapeDtypeStruct((S, D_IN), F32),
                   jax.ShapeDtypeStruct((D_IN, D), BF16),
                   jax.ShapeDtypeStruct((S, 128), F32), jax.ShapeDtypeStruct((S, 128), F32)],
        compiler_params=_params(("arbitrary",)),
    )(sched, x, g, bias, wt_part, pos, invf)


def _rot_half(xs, first_half):
    return jnp.where(first_half, -pltpu.roll(xs, 96, 1), pltpu.roll(xs, 32, 1))


GH = NQ // 2


MASKED = -1e30


def _attn_consts():
    lane = lax.broadcasted_iota(jnp.int32, (T, 128), 1)
    row = lax.broadcasted_iota(jnp.int32, (GH * T, T), 0) & (T - 1)
    on_diag_or_below = row >= lax.broadcasted_iota(jnp.int32, (GH * T, T), 1)
    return (lane & (HD - 1)) < (HD // 2), lane < HD, on_diag_or_below


def _stack_heads(slab_fn, grp, lo64):
    blocks = []
    for jj in range(GH // 2):
        s = slab_fn(GH // 2 * grp + jj)
        blocks += [jnp.where(lo64, s, 0.0), jnp.where(lo64, 0.0, s)]
    return jnp.concatenate(blocks, axis=0)


def _unstack_heads(stacked, jj, lo64):
    return jnp.where(lo64, stacked[(2 * jj) * T:(2 * jj + 1) * T], stacked[(2 * jj + 1) * T:(2 * jj + 2) * T])


def _both_halves(kv, grp):
    lo = lax.broadcasted_iota(jnp.int32, kv.shape, 1) < HD
    swapped = pltpu.roll(kv, HD, 1)
    return jnp.where(lo, kv, swapped) if grp == 0 else jnp.where(lo, swapped, kv)


def _one_half(acc, grp):
    lo = lax.broadcasted_iota(jnp.int32, acc.shape, 1) < HD
    return jnp.where(lo if grp == 0 else jnp.logical_not(lo), acc + pltpu.roll(acc, HD, 1), 0.0)


def _layer_norm_stats(v):
    mu = _row_mean(v)
    xc = v - mu
    var = _row_mean(xc * xc)
    rs = lax.rsqrt(var + EPS)
    return xc * rs, rs


def _fold(both, own):
    return jnp.where(own, both[:, T:2 * T], both[:, 0:T])


def _unfold(p, own):
    return jnp.concatenate([jnp.where(own, 0.0, p), jnp.where(own, p, 0.0)], axis=1).astype(BF16)


def _band_softmax(q_scaled, k_both, own, has_prev, sink):
    s_both = _dot(q_scaled, k_both, NT)
    s_prev = s_both[:, 0:T]
    if has_prev is not None:
        s_prev = s_prev + jnp.where(has_prev, 0.0, MASKED)
    s = jnp.where(own, s_both[:, T:2 * T], s_prev)
    m = jnp.maximum(jnp.max(s, axis=-1, keepdims=True), sink)
    e = jnp.exp(s - m)
    es = jnp.exp(sink - m)
    inv = 1.0 / (jnp.sum(e, axis=-1, keepdims=True) + es)
    return e * inv, es * inv


BPS_FWD = 4
BPS_BWD = 4


def _mid_specs(nt, rev, bps):
    tile = (lambda i: nt - 1 - i) if rev else (lambda i: i)
    prev = lambda i: jnp.maximum(bps * tile(i) - 1, 0)
    rows = bps * T
    return tile, [
        pl.BlockSpec((rows, D_IN), lambda i: (tile(i), 0)),
        pl.BlockSpec((T, 2 * T), lambda i: (prev(i), OFF_K // (2 * T))),
    ], [
        pl.BlockSpec((rows, 128), lambda i: (tile(i), 0)),
        pl.BlockSpec((rows, 128), lambda i: (tile(i), 0)),
        pl.BlockSpec((T, 128), lambda i: (prev(i), 0)),
        pl.BlockSpec((T, 128), lambda i: (prev(i), 0)),
        pl.BlockSpec((1, DG), lambda i: (0, 0)),
        pl.BlockSpec((1, DG), lambda i: (0, 0)),
        pl.BlockSpec((NG, T, T), lambda i: (0, 0, 0)),
        pl.BlockSpec((NG, T, 1), lambda i: (0, 0, 0)),
        pl.BlockSpec((NQ * T, 1), lambda i: (0, 0)),
    ]


def _rope(xs, cosv, sinv, first_half):
    return xs * cosv + _rot_half(xs, first_half) * sinv


def _mid_fwd(proj, cos, sin, ln_g, ln_b, w_s, b_s, sinks, wo_part):
    S = proj.shape[0]
    BPS = BPS_FWD if S % (BPS_FWD * T) == 0 else 1
    nt = S // (BPS * T)
    tile, proj_specs, par_specs = _mid_specs(nt, False, BPS)

    def body(p_ref, kvp_ref, cos_ref, sin_ref, cosp_ref, sinp_ref, lng_ref, lnb_ref, ws_ref, bs_ref, sink_ref,
             wpart_ref, y_ref, wo_ref, q_ref, wm_ref, send_sems, recv_sems, local_sems):
        i = pl.program_id(0)
        first_half, lo64, own = _attn_consts()
        gather = _RowGather(wpart_ref, wo_ref, W_OUT_ROWS, send_sems, recv_sems, local_sems)
        pl.when(i == 0)(gather.start)

        @pl.when(i == 0)
        def _():
            tril = lax.broadcasted_iota(jnp.int32, (T, T), 0) >= lax.broadcasted_iota(jnp.int32, (T, T), 1)
            for g in range(NG):
                wm_ref[g] = jnp.where(tril, ws_ref[g], 0.0).astype(BF16)

        def block(b, k_prev, v_prev, has_prev):
            rows = slice(b * T, (b + 1) * T)
            xhat, _ = _layer_norm_stats(p_ref[rows, OFF_V:OFF_V + DG])
            vn = xhat * lng_ref[...] + lnb_ref[...]
            for g in range(NG):
                sl = slice(128 * g, 128 * g + 128)
                mixed = _dot(wm_ref[g], vn[:, sl].astype(BF16), NN) + bs_ref[g]
                z = p_ref[rows, OFF_ZA + 128 * g:OFF_ZA + 128 * g + 128]
                u = p_ref[rows, OFF_U + 128 * g:OFF_U + 128 * g + 128]
                y_ref[rows, sl] = (u * mixed * (z * _sigmoid(z))).astype(BF16)

            cosv, sinv = cos_ref[rows, :], sin_ref[rows, :]
            k_cur = _rope(p_ref[rows, OFF_K:OFF_K + 128], cosv, sinv, first_half)
            v_cur = p_ref[rows, OFF_VA:OFF_VA + 128]

            def q_slab(j):
                q = (_rope(p_ref[rows, OFF_Q + 128 * j:OFF_Q + 128 * j + 128], cosv, sinv, first_half)
                     * (HD ** -0.5)).astype(BF16)
                q_ref[rows, 128 * j:128 * j + 128] = q
                return q

            k_both = jnp.concatenate([k_prev, k_cur], axis=0)
            v_both = jnp.concatenate([v_prev, v_cur], axis=0)
            for grp in range(2):
                qs = _stack_heads(q_slab, grp, lo64)
                kg, vg = _both_halves(k_both, grp).astype(BF16), _both_halves(v_both, grp).astype(BF16)
                p, _ = _band_softmax(qs, kg, own, has_prev, sink_ref[GH * T * grp:GH * T * (grp + 1), :])
                o_st = _dot(_unfold(p, own), vg, NN)
                for jj in range(GH // 2):
                    c0 = 128 * (GH // 2 * grp + jj)
                    zb = p_ref[rows, OFF_ZB + c0:OFF_ZB + c0 + 128]
                    o = _unstack_heads(o_st, jj, lo64)
                    y_ref[rows, DG + c0:DG + c0 + 128] = (o * (zb * _sigmoid(zb))).astype(BF16)
            return k_cur, v_cur

        k_prev = _rope(kvp_ref[:, 0:128], cosp_ref[...], sinp_ref[...], first_half)
        kv = block(0, k_prev, kvp_ref[:, 128:256], i > 0)
        for b in range(1, BPS):
            kv = block(b, *kv, None)
        pl.when(i == nt - 1)(gather.finish)

    return pl.pallas_call(
        body, name="mid_fwd", grid=(nt,),
        in_specs=proj_specs + par_specs + [HBM_SPEC],
        out_specs=[pl.BlockSpec((BPS * T, D), lambda i: (tile(i), 0)), HBM_SPEC,
                   pl.BlockSpec((BPS * T, NQ * HD), lambda i: (tile(i), 0))],
        out_shape=[jax.ShapeDtypeStruct((S, D), BF16), jax.ShapeDtypeStruct((D, D), BF16),
                   jax.ShapeDtypeStruct((S, NQ * HD), BF16)],
        scratch_shapes=[pltpu.VMEM((NG, T, T), BF16)] + GATHER_SEMS,
        compiler_params=_params(("arbitrary",)),
    )(proj, proj, cos, sin, cos, sin, ln_g, ln_b, w_s, b_s, sinks, wo_part)


def _mid_bwd(proj, dycat, q_roped, cos, sin, ln_g, ln_b, w_s, b_s, sinks, po):
    S = proj.shape[0]
    BPS = BPS_BWD if S % (BPS_BWD * T) == 0 else 1
    nt = S // (BPS * T)
    tile, proj_specs, par_specs = _mid_specs(nt, True, BPS)
    const2 = lambda i: (0, 0)

    def body(p_ref, kvp_ref, dyc_ref, q_ref, cos_ref, sin_ref, cosp_ref, sinp_ref, lng_ref, lnb_ref, ws_ref, bs_ref,
             sink_ref, po_ref, dp_ref, dlng_ref, dlnb_ref, dws_ref, dbs_ref, dsink_ref, dbqkv_ref, bo_ref,
             carry_ref, dvn_ref, wm_ref, wmt_ref, send_sems, recv_sems, local_sem):
        i = pl.program_id(0)
        first_half, lo64, own = _attn_consts()
        tril = lax.broadcasted_iota(jnp.int32, (T, T), 0) >= lax.broadcasted_iota(jnp.int32, (T, T), 1)
        exchange = _OwnerExchange(po_ref, bo_ref, send_sems, recv_sems, local_sem)
        pl.when(i == 0)(exchange.start)

        @pl.when(i == 0)
        def _():
            for g in range(NG):
                wm = jnp.where(tril, ws_ref[g], 0.0)
                wm_ref[g] = wm.astype(BF16)
                wmt_ref[g] = wm.T.astype(BF16)
            dlng_ref[...] = jnp.zeros_like(dlng_ref)
            dlnb_ref[...] = jnp.zeros_like(dlnb_ref)
            dws_ref[...] = jnp.zeros_like(dws_ref)
            dbs_ref[...] = jnp.zeros_like(dbs_ref)
            dsink_ref[...] = jnp.zeros_like(dsink_ref)
            dbqkv_ref[...] = jnp.zeros_like(dbqkv_ref)
            carry_ref[...] = jnp.zeros_like(carry_ref)

        def roped_k(b):
            rows = slice(b * T, (b + 1) * T)
            return _rope(p_ref[rows, OFF_K:OFF_K + 128], cos_ref[rows, :], sin_ref[rows, :], first_half)

        def block(b, k_prev, v_prev, has_prev, dk_next, dv_next):
            rows = slice(b * T, (b + 1) * T)
            xhat, rs = _layer_norm_stats(p_ref[rows, OFF_V:OFF_V + DG])
            lng = lng_ref[...]
            vn = xhat * lng + lnb_ref[...]
            for g in range(NG):
                sl = slice(128 * g, 128 * g + 128)
                vng = vn[:, sl].astype(BF16)
                mixed = _dot(wm_ref[g], vng, NN) + bs_ref[g]
                z = p_ref[rows, OFF_ZA + 128 * g:OFF_ZA + 128 * g + 128]
                u = p_ref[rows, OFF_U + 128 * g:OFF_U + 128 * g + 128]
                dy = dyc_ref[rows, sl]
                sg = _sigmoid(z)
                sa = z * sg
                dp_ref[rows, OFF_U + 128 * g:OFF_U + 128 * g + 128] = (dy * mixed * sa).astype(BF16)
                dp_ref[rows, OFF_ZA + 128 * g:OFF_ZA + 128 * g + 128] = (
                    dy * u * mixed * (sg * (1.0 + z * (1.0 - sg)))).astype(BF16)
                dm = dy * u * sa
                dmb = dm.astype(BF16)
                dvn_ref[rows, sl] = _dot(wmt_ref[g], dmb, NN)
                dws_ref[g] += jnp.where(tril, _dot(dmb, vng, NT), 0.0)
                dbs_ref[g] += jnp.sum(dm, axis=1, keepdims=True)
            dvn = dvn_ref[rows, :]
            dlng_ref[...] += jnp.sum(dvn * xhat, axis=0, keepdims=True)
            dlnb_ref[...] += jnp.sum(dvn, axis=0, keepdims=True)
            dxh = dvn * lng
            dv_g = rs * (dxh - _row_mean(dxh)
                         - xhat * _row_mean(dxh * xhat))
            dp_ref[rows, OFF_V:OFF_V + DG] = dv_g.astype(BF16)

            cosv, sinv = cos_ref[rows, :], sin_ref[rows, :]
            k_cur = roped_k(b)
            v_cur = p_ref[rows, OFF_VA:OFF_VA + 128]

            def q_slab(j):
                return q_ref[rows, 128 * j:128 * j + 128]

            def do_slab(j):
                zb = p_ref[rows, OFF_ZB + 128 * j:OFF_ZB + 128 * j + 128]
                return dyc_ref[rows, DG + 128 * j:DG + 128 * j + 128] * (zb * _sigmoid(zb))

            k_both = jnp.concatenate([k_prev, k_cur], axis=0)
            v_both = jnp.concatenate([v_prev, v_cur], axis=0)
            dk_both, dv_both = jnp.zeros((2 * T, 128), F32), jnp.zeros((2 * T, 128), F32)
            for grp in range(2):
                qs = _stack_heads(q_slab, grp, lo64)
                d_o = _stack_heads(do_slab, grp, lo64)
                dob = d_o.astype(BF16)
                kg, vg = _both_halves(k_both, grp).astype(BF16), _both_halves(v_both, grp).astype(BF16)
                p, ps = _band_softmax(qs, kg, own, has_prev, sink_ref[GH * T * grp:GH * T * (grp + 1), :])
                p_both = _unfold(p, own)
                o_st = _dot(p_both, vg, NN)
                delta = jnp.sum(d_o * o_st, axis=-1, keepdims=True)
                ds_both = _unfold(p * (_fold(_dot(dob, vg, NT), own) - delta), own)
                dq_st = _dot(ds_both, kg, NN) * (HD ** -0.5)
                dk_both = dk_both + _one_half(_dot(ds_both, qs, TN), grp)
                dv_both = dv_both + _one_half(_dot(p_both, dob, TN), grp)
                dsink_rows = ps * delta
                for hh in range(GH):
                    h = GH * grp + hh
                    dsink_ref[h:h + 1, :] += jnp.broadcast_to(
                        -jnp.sum(dsink_rows[hh * T:(hh + 1) * T], axis=0, keepdims=True), (1, 128))
                for jj in range(GH // 2):
                    c0 = 128 * (GH // 2 * grp + jj)
                    zb = p_ref[rows, OFF_ZB + c0:OFF_ZB + c0 + 128]
                    sg = _sigmoid(zb)
                    o = _unstack_heads(o_st, jj, lo64)
                    dp_ref[rows, OFF_ZB + c0:OFF_ZB + c0 + 128] = (
                        dyc_ref[rows, DG + c0:DG + c0 + 128] * o * (sg * (1.0 + zb * (1.0 - sg)))).astype(BF16)
                    dq = _unstack_heads(dq_st, jj, lo64)
                    dq_pre = dq * cosv - _rot_half(dq, first_half) * sinv
                    dp_ref[rows, OFF_Q + c0:OFF_Q + c0 + 128] = dq_pre.astype(BF16)
                    dbqkv_ref[:, c0:c0 + 128] += jnp.sum(dq_pre, axis=0, keepdims=True)
            dk_prev, dv_prev = dk_both[0:T], dv_both[0:T]
            dk_cur, dv_cur = dk_both[T:2 * T] + dk_next, dv_both[T:2 * T] + dv_next
            dk_pre = dk_cur * cosv - _rot_half(dk_cur, first_half) * sinv
            dp_ref[rows, OFF_K:OFF_K + 128] = dk_pre.astype(BF16)
            dp_ref[rows, OFF_VA:OFF_VA + 128] = dv_cur.astype(BF16)
            dbqkv_ref[:, 1024:1152] += jnp.sum(dk_pre, axis=0, keepdims=True)
            dbqkv_ref[:, 1152:1280] += jnp.sum(dv_cur, axis=0, keepdims=True)
            return dk_prev, dv_prev

        grads = carry_ref[:, 0:128], carry_ref[:, 128:256]
        for b in range(BPS - 1, 0, -1):
            prows = slice((b - 1) * T, b * T)
            grads = block(b, roped_k(b - 1), p_ref[prows, OFF_VA:OFF_VA + 128], None, *grads)
        k_prev = _rope(kvp_ref[:, 0:128], cosp_ref[...], sinp_ref[...], first_half)
        grads = block(0, k_prev, kvp_ref[:, 128:256], i < nt - 1, *grads)
        carry_ref[:, 0:128], carry_ref[:, 128:256] = grads
        pl.when(i == nt - 1)(exchange.finish)

    return pl.pallas_call(
        body, name="mid_bwd", grid=(nt,),
        in_specs=proj_specs + [pl.BlockSpec((BPS * T, D), lambda i: (tile(i), 0)),
                               pl.BlockSpec((BPS * T, NQ * HD), lambda i: (tile(i), 0))] + par_specs + [HBM_SPEC],
        out_specs=[pl.BlockSpec((BPS * T, D_IN), lambda i: (tile(i), 0)),
                   pl.BlockSpec((1, DG), const2), pl.BlockSpec((1, DG), const2),
                   pl.BlockSpec((NG, T, T), lambda i: (0, 0, 0)), pl.BlockSpec((NG, T, 1), lambda i: (0, 0, 0)),
                   pl.BlockSpec((NQ, 128), const2), pl.BlockSpec((1, D_QKV), const2), HBM_SPEC],
        out_shape=[jax.ShapeDtypeStruct((S, D_IN), BF16),
                   jax.ShapeDtypeStruct((1, DG), F32), jax.ShapeDtypeStruct((1, DG), F32),
                   jax.ShapeDtypeStruct((NG, T, T), F32), jax.ShapeDtypeStruct((NG, T, 1), F32),
                   jax.ShapeDtypeStruct((NQ, 128), F32), jax.ShapeDtypeStruct((1, D_QKV), F32),
                   jax.ShapeDtypeStruct(po.shape, BF16)],
        scratch_shapes=[pltpu.VMEM((T, 2 * T), F32), pltpu.VMEM((BPS * T, DG), F32),
                        pltpu.VMEM((NG, T, T), BF16), pltpu.VMEM((NG, T, T), BF16)] + OWNER_SEMS,
        compiler_params=_params(("arbitrary",)),
    )(proj, proj, dycat, q_roped, cos, sin, cos, sin, ln_g, ln_b, w_s, b_s, sinks, po)


def _outproj_loss(ycat, wo, x, target, g_post):
    S = ycat.shape[0]
    tm = _tile(S, 512)
    nt = S // tm
    n_part = 2 if tm % 32 == 0 else 1
    tp = tm // n_part
    const2 = lambda i: (0, 0)

    def body(yc_ref, w_ref, x_ref, t_ref, g_ref, dy_ref, dout_ref, loss_ref, dg_ref, lacc_ref):
        i = pl.program_id(0)

        @pl.when(i == 0)
        def _():
            dg_ref[...] = jnp.zeros_like(dg_ref)
            lacc_ref[...] = jnp.zeros_like(lacc_ref)

        g = g_ref[...]
        ys = [_dot(yc_ref[q * tp:(q + 1) * tp, :], w_ref[...], NN) for q in range(n_part)]
        for q, y in enumerate(ys):
            rows = slice(q * tp, (q + 1) * tp)
            r = lax.rsqrt(_row_mean(y * y) + EPS)
            yh = y * r
            diff = x_ref[rows, :] + yh * g - t_ref[rows, :]
            lacc_ref[...] += jnp.sum(diff * diff, axis=0, keepdims=True)
            dout = diff * (1.0 / D)
            dout_ref[rows, :] = dout
            dg_ref[...] += jnp.sum(dout * yh, axis=0, keepdims=True)
            dyh = dout * g
            dy_ref[rows, :] = (r * (dyh - yh * _row_mean(dyh * yh))).astype(BF16)

        @pl.when(i == nt - 1)
        def _():
            loss_ref[...] = jnp.broadcast_to(jnp.sum(lacc_ref[...], axis=1, keepdims=True) * (0.5 / D), (1, 128))

    row = lambda i: (i, 0)
    return pl.pallas_call(
        body, name="outproj_loss", grid=(nt,),
        in_specs=[pl.BlockSpec((tm, D), row), pl.BlockSpec((D, D), const2, pipeline_mode=pl.Buffered(1)),
                  pl.BlockSpec((tm, D), row), pl.BlockSpec((tm, D), row), pl.BlockSpec((1, D), const2)],
        out_specs=[pl.BlockSpec((tm, D), row), pl.BlockSpec((tm, D), row), pl.BlockSpec((1, 128), const2),
                   pl.BlockSpec((1, D), const2)],
        out_shape=[jax.ShapeDtypeStruct((S, D), BF16), jax.ShapeDtypeStruct((S, D), F32),
                   jax.ShapeDtypeStruct((1, 128), F32), jax.ShapeDtypeStruct((1, D), F32)],
        scratch_shapes=[pltpu.VMEM((1, D), F32)],
        compiler_params=_params(("arbitrary",)),
    )(ycat, wo, x, target, g_post)


def _dycat(dy, wo, dwo):
    S = dy.shape[0]
    tm = _tile(S, 1024)
    nt = S // tm

    def body(dy_ref, w_ref, dwo_ref, o_ref, r_ref, send_sems, recv_sems):
        i = pl.program_id(0)
        copies = _sibling_copies(dwo_ref, r_ref, W_OUT_ROWS, send_sems, recv_sems)

        @pl.when(i == 0)
        def _():
            for cp in copies:
                cp.start()

        o_ref[...] = _dot(dy_ref[...], w_ref[...], NT)

        @pl.when(i == nt - 1)
        def _():
            for cp in copies:
                cp.wait_recv()
            for cp in copies:
                cp.wait_send()

    return pl.pallas_call(
        body, name="dycat", grid=(nt,),
        in_specs=[pl.BlockSpec((tm, D), lambda i: (i, 0)),
                  pl.BlockSpec((D, D), lambda i: (0, 0), pipeline_mode=pl.Buffered(1)), HBM_SPEC],
        out_specs=[pl.BlockSpec((tm, D), lambda i: (i, 0)), HBM_SPEC],
        out_shape=[jax.ShapeDtypeStruct((S, D), F32), jax.ShapeDtypeStruct((N_CHIPS, W_OUT_ROWS, D), BF16)],
        scratch_shapes=[pltpu.SemaphoreType.DMA((N_CHIPS,)), pltpu.SemaphoreType.DMA((N_CHIPS,))],
        compiler_params=_params(("arbitrary",)),
    )(dy, wo, dwo)


def _matmul_tn(a, b, tm, name, mat_grads=None):
    K, M = a.shape
    N = b.shape[1]
    tk = _tile(K, 2048)
    ni, nk = M // tm, K // tk
    hosting = mat_grads is not None

    def body(a_ref, b_ref, *rest):
        if hosting:
            g_in, (o_ref, tot_ref, acc_ref, mbuf, send_sems, recv_sems) = rest[:len(MAT)], rest[len(MAT):]
        else:
            o_ref, acc_ref = rest
        i, k = pl.program_id(0), pl.program_id(1)

        if hosting:
            gather = _SlotGather(mbuf, send_sems, recv_sems)

            @pl.when((i == 0) & (k == 0))
            def _():
                r0 = 0
                for q, (_, rows) in enumerate(MAT):
                    mbuf[gather.my_id, r0:r0 + rows, :] = g_in[q][...]
                    r0 += rows
                gather.start()

        @pl.when(k == 0)
        def _():
            acc_ref[...] = jnp.zeros_like(acc_ref)

        acc_ref[...] += _dot(a_ref[...], b_ref[...], TN)

        @pl.when(k == nk - 1)
        def _():
            o_ref[...] = acc_ref[...].astype(BF16)

        if hosting:
            @pl.when((i == ni - 1) & (k == nk - 1))
            def _():
                gather.finish()
                tot_ref[...] = gather.total()

    whole = pl.BlockSpec(memory_space=pltpu.VMEM)
    in_specs = [pl.BlockSpec((tk, tm), lambda i, k: (k, i)), pl.BlockSpec((tk, N), lambda i, k: (k, 0))]
    out_specs = [pl.BlockSpec((tm, N), lambda i, k: (i, 0))]
    out_shape = [jax.ShapeDtypeStruct((M, N), BF16)]
    scratch = [pltpu.VMEM((tm, N), F32)]
    if hosting:
        in_specs += [whole] * len(MAT)
        out_specs.append(whole)
        out_shape.append(jax.ShapeDtypeStruct((MAT_ROWS, 128), F32))
        scratch += [pltpu.VMEM((N_DEV, MAT_ROWS, 128), F32), pltpu.SemaphoreType.DMA((7,)),
                    pltpu.SemaphoreType.DMA((7,))]
    res = pl.pallas_call(
        body, name=name, grid=(ni, nk), in_specs=in_specs, out_specs=out_specs, out_shape=out_shape,
        scratch_shapes=scratch,
        compiler_params=_params(("arbitrary", "arbitrary") if hosting else ("parallel", "arbitrary")),
    )(a, b, *(mat_grads or ()))
    return res if hosting else res[0]


def _dh_prenorm_bwd(dproj, wt, x, dout, g_pre, pt):
    S = x.shape[0]
    tm, tk = _tile(S, 1024), 768
    ne = 4
    te = tm // ne
    ni, nk = S // tm, D_IN // tk
    n_mm = ni * nk
    n_steps = n_mm + ne
    mm_of = lambda t: jnp.minimum(t, n_mm - 1)

    def quarter_of(t):
        j, q = t // nk - 1, t % nk
        nxt = jnp.where(q < ne, j * ne + q, (j + 1) * ne)
        return jnp.clip(nxt, 0, ni * ne - 1)

    def body(dp_ref, w_ref, x_ref, dout_ref, g_ref, pt_ref, gx_ref, dg_ref, bt_ref, acc_ref,
             send_sems, recv_sems, local_sem):
        t = pl.program_id(0)
        exchange = _OwnerExchange(pt_ref, bt_ref, send_sems, recv_sems, local_sem)
        pl.when(t == 0)(exchange.start)

        @pl.when(t == 0)
        def _():
            dg_ref[...] = jnp.zeros_like(dg_ref)

        has_mm = t < n_mm
        has_q = (t >= nk) & (t % nk < ne)

        def matmul():
            slot = (mm_of(t) // nk) % 2
            old = jnp.where(mm_of(t) % nk > 0, acc_ref[slot], 0.0)
            acc_ref[slot] = old + _dot(dp_ref[...], w_ref[...], NN)

        def quarter():
            slot = (t // nk - 1) % 2
            dh = acc_ref[slot, pl.ds(pl.multiple_of((t % nk) * te, te), te), :]
            xv = x_ref[...]
            r = lax.rsqrt(_row_mean(xv * xv) + EPS)
            xh = xv * r
            dg_ref[...] += jnp.sum(dh * xh, axis=0, keepdims=True)
            dxh = dh * g_ref[...]
            gx_ref[...] = dout_ref[...] + r * (dxh - xh * _row_mean(dxh * xh))

        @pl.when(has_mm & has_q)
        def _():
            matmul()
            quarter()

        pl.when(has_mm & jnp.logical_not(has_q))(matmul)
        pl.when(jnp.logical_not(has_mm) & has_q)(quarter)
        pl.when(t == n_steps - 1)(exchange.finish)

    quarter_block = lambda t: (quarter_of(t), 0)
    return pl.pallas_call(
        body, name="dh_prenorm_bwd", grid=(n_steps,),
        in_specs=[pl.BlockSpec((tm, tk), lambda t: (mm_of(t) // nk, mm_of(t) % nk)),
                  pl.BlockSpec((tk, D), lambda t: (mm_of(t) % nk, 0)),
                  pl.BlockSpec((te, D), quarter_block), pl.BlockSpec((te, D), quarter_block),
                  pl.BlockSpec((1, D), lambda t: (0, 0)), HBM_SPEC],
        out_specs=[pl.BlockSpec((te, D), quarter_block), pl.BlockSpec((1, D), lambda t: (0, 0)), HBM_SPEC],
        out_shape=[jax.ShapeDtypeStruct((S, D), F32), jax.ShapeDtypeStruct((1, D), F32),
                   jax.ShapeDtypeStruct(pt.shape, BF16)],
        scratch_shapes=[pltpu.VMEM((2, tm, D), F32)] + OWNER_SEMS,
        compiler_params=_params(("arbitrary",)),
    )(dproj, wt, x, dout, g_pre, pt)


def _presum(c_arr, own, recv, half_rows):
    n_cols = own.shape[-1]
    own4 = own.reshape(N_CHIPS, 2, half_rows, n_cols)

    def body(c_ref, own_ref, recv_ref, o_ref):
        o_ref[...] = (own_ref[...].astype(F32) + recv_ref[...].astype(F32)).astype(BF16)

    return pl.pallas_call(
        body, name="presum_%d" % half_rows,
        grid_spec=pltpu.PrefetchScalarGridSpec(
            num_scalar_prefetch=1, grid=(N_CHIPS,),
            in_specs=[pl.BlockSpec((None, None, half_rows, n_cols), lambda j, c: (j, c[0], 0, 0)),
                      pl.BlockSpec((None, half_rows, n_cols), lambda j, c: (j, 0, 0))],
            out_specs=pl.BlockSpec((None, half_rows, n_cols), lambda j, c: (j, 0, 0))),
        out_shape=jax.ShapeDtypeStruct((N_CHIPS, half_rows, n_cols), BF16),
        compiler_params=_params(("parallel",)),
    )(c_arr, own4, recv)


def _sibling_presum(c_arr, dw, half_rows, name):
    n_cols = dw.shape[-1]
    dw4 = dw.reshape(N_CHIPS, 2, half_rows, n_cols)

    def body(c_ref, own_ref, dw_ref, o_ref, r_ref, stage, send_sems, recv_sems, load_sem):
        j = pl.program_id(0)
        copies = _sibling_copies(dw_ref, r_ref, half_rows, send_sems, recv_sems)

        @pl.when(j == 0)
        def _():
            for cp in copies:
                cp.start()

        for q in range(N_CHIPS):
            pl.when(j == q)(copies[q].wait_recv)
        load = pltpu.make_async_copy(r_ref.at[j], stage, load_sem)
        load.start()
        load.wait()
        o_ref[...] = (own_ref[...].astype(F32) + stage[...].astype(F32)).astype(BF16)

        @pl.when(j == N_CHIPS - 1)
        def _():
            for cp in copies:
                cp.wait_send()

    out, _ = pl.pallas_call(
        body, name=name,
        grid_spec=pltpu.PrefetchScalarGridSpec(
            num_scalar_prefetch=1, grid=(N_CHIPS,),
            in_specs=[pl.BlockSpec((None, None, half_rows, n_cols), lambda j, c: (j, c[0], 0, 0)), HBM_SPEC],
            out_specs=[pl.BlockSpec((None, half_rows, n_cols), lambda j, c: (j, 0, 0)), HBM_SPEC],
            scratch_shapes=[pltpu.VMEM((half_rows, n_cols), BF16), pltpu.SemaphoreType.DMA((N_CHIPS,)),
                            pltpu.SemaphoreType.DMA((N_CHIPS,)), pltpu.SemaphoreType.DMA(())]),
        out_shape=[jax.ShapeDtypeStruct((N_CHIPS, half_rows, n_cols), BF16),
                   jax.ShapeDtypeStruct((N_CHIPS, half_rows, n_cols), BF16)],
        compiler_params=_params(("arbitrary",)),
    )(c_arr, dw4, dw)
    return out


def _sum_chips(c_arr, parts, name):
    _, rows, n_cols = parts.shape
    nt = 2
    tr = rows // nt

    def body(c_ref, p_ref, o_ref):
        o_ref[...] = ((p_ref[0].astype(F32) + p_ref[1].astype(F32)) + p_ref[2].astype(F32)) + p_ref[3].astype(F32)

    return pl.pallas_call(
        body, name=name,
        grid_spec=pltpu.PrefetchScalarGridSpec(
            num_scalar_prefetch=1, grid=(nt,),
            in_specs=[pl.BlockSpec((N_CHIPS, tr, n_cols), lambda i, c: (0, i, 0))],
            out_specs=pl.BlockSpec((tr, n_cols), lambda i, c: (c[0] * nt + i, 0))),
        out_shape=jax.ShapeDtypeStruct((2 * rows, n_cols), F32),
        compiler_params=_params(("parallel",)),
    )(c_arr, parts)


def _adamw_math(w, g, m, v):
    mn = ADAM_B1 * m + (1.0 - ADAM_B1) * g
    vn = ADAM_B2 * v + (1.0 - ADAM_B2) * (g * g)
    m_hat = mn / (1.0 - ADAM_B1 ** ADAM_STEP)
    v_hat = vn / (1.0 - ADAM_B2 ** ADAM_STEP)
    return -ADAM_LR * (m_hat / (jnp.sqrt(v_hat) + ADAM_EPS) + ADAM_WD * w), mn, vn


def _adamw(w, g, m, v, name):
    R, C = w.shape
    tr = next((t for t in (256, 192, 128) if R % t == 0), R)

    def body(w_ref, g_ref, m_ref, v_ref, go_ref, d_ref, mo_ref, vo_ref):
        gv = g_ref[...]
        go_ref[...] = gv
        d_ref[...], mo_ref[...], vo_ref[...] = _adamw_math(w_ref[...], gv, m_ref[...], v_ref[...])

    spec = pl.BlockSpec((tr, C), lambda i: (i, 0))
    shp = jax.ShapeDtypeStruct((R, C), F32)
    return pl.pallas_call(
        body, name=name, grid=(R // tr,), in_specs=[spec] * 4, out_specs=[spec] * 4, out_shape=[shp] * 4,
        compiler_params=_params(("parallel",)),
    )(w, g, m, v)


HBM_SPEC = pl.BlockSpec(memory_space=pltpu.HBM)
GATHER_LOCAL_CHUNKS = 4
GATHER_SEMS = [pltpu.SemaphoreType.DMA((7,)), pltpu.SemaphoreType.DMA((7,)),
               pltpu.SemaphoreType.DMA((GATHER_LOCAL_CHUNKS,))]
OWNER_SEMS = [pltpu.SemaphoreType.DMA((3,)), pltpu.SemaphoreType.DMA((3,)), pltpu.SemaphoreType.DMA(())]


def _mesh_pos():
    return lax.axis_index("x"), lax.axis_index("y"), lax.axis_index("c")


class _RowGather:
    def __init__(self, src_ref, full_ref, rows, send_sems, recv_sems, local_sems):
        self.src, self.full, self.rows = src_ref, full_ref, rows
        self.send, self.recv, self.local = send_sems, recv_sems, local_sems
        x, y, c = _mesh_pos()
        self.c, self.me, self.sibling = c, (x, y, c), (x, y, 1 - c)
        self.chips = [(1 - x, y), (x, 1 - y), (1 - x, 1 - y)]

    def _block(self, pos):
        px, py, pc = pos
        return self.full.at[pl.ds(pl.multiple_of((4 * px + 2 * py + pc) * self.rows, 16), self.rows), :]

    def _copy(self, k, blk, to):
        return pltpu.make_async_remote_copy(
            src_ref=self.src if blk is self.me else self._block(blk), dst_ref=self._block(blk),
            send_sem=self.send.at[k], recv_sem=self.recv.at[k], device_id=to, device_id_type=MESH)

    def _mine(self):
        return _place_locally(self.src, self._block(self.me), self.local, GATHER_LOCAL_CHUNKS)

    def _first(self):
        return [self._copy(0, self.me, self.sibling)] + [
            self._copy(1 + j, self.me, (*chip, self.c)) for j, chip in enumerate(self.chips)]

    def start(self):
        for cp in self._first() + self._mine():
            cp.start()

    def finish(self):
        passed = [self._copy(4 + j, (*chip, self.c), self.sibling) for j, chip in enumerate(self.chips)]
        for j, chip in enumerate(self.chips):
            self._copy(1 + j, (*chip, self.c), self.me).wait_recv()
            passed[j].start()
        self._copy(0, self.sibling, self.me).wait_recv()
        for j, chip in enumerate(self.chips):
            self._copy(4 + j, (*chip, 1 - self.c), self.me).wait_recv()
        for cp in self._first() + passed:
            cp.wait_send()
        for cp in self._mine():
            cp.wait()


class _OwnerExchange:
    def __init__(self, src_ref, dst_ref, send_sems, recv_sems, local_sem):
        self.src, self.dst, self.send, self.recv, self.local = src_ref, dst_ref, send_sems, recv_sems, local_sem
        x, y, c = _mesh_pos()
        self.c, self.my_chip = c, 2 * x + y
        self.peers = [(1 - x, y), (x, 1 - y), (1 - x, 1 - y)]

    def _copies(self):
        local = pltpu.make_async_copy(self.src.at[self.my_chip], self.dst.at[self.my_chip], self.local)
        remote = [pltpu.make_async_remote_copy(
            src_ref=self.src.at[2 * px + py], dst_ref=self.dst.at[self.my_chip],
            send_sem=self.send.at[k], recv_sem=self.recv.at[k], device_id=(px, py, self.c), device_id_type=MESH)
            for k, (px, py) in enumerate(self.peers)]
        return local, remote

    def start(self):
        local, remote = self._copies()
        local.start()
        for cp in remote:
            cp.start()

    def finish(self):
        local, remote = self._copies()
        for cp in remote:
            cp.wait_recv()
        for cp in remote:
            cp.wait_send()
        local.wait()


def _sibling_copies(dw_ref, r_ref, rows, send_sems, recv_sems):
    x, y, c = _mesh_pos()
    return [pltpu.make_async_remote_copy(
        src_ref=dw_ref.at[pl.ds(pl.multiple_of((2 * j + (1 - c)) * rows, 16), rows), :], dst_ref=r_ref.at[j],
        send_sem=send_sems.at[j], recv_sem=recv_sems.at[j], device_id=(x, y, 1 - c), device_id_type=MESH)
        for j in range(N_CHIPS)]


PAIR_CHUNKS = 4


def _pair_halves(gt, go):
    def body(gt_in, go_in, gt_ref, go_ref, send_sems, recv_sems):
        del gt_in, go_in
        x, y, c = _mesh_pos()
        copies = []
        for a, (ref, rows) in enumerate(((gt_ref, W_IN_ROWS), (go_ref, W_OUT_ROWS))):
            ch = rows // PAIR_CHUNKS
            for q in range(PAIR_CHUNKS):
                part = ref.at[pl.ds(pl.multiple_of(c * rows + q * ch, 8), ch), :]
                copies.append(pltpu.make_async_remote_copy(
                    src_ref=part, dst_ref=part, send_sem=send_sems.at[PAIR_CHUNKS * a + q],
                    recv_sem=recv_sems.at[PAIR_CHUNKS * a + q], device_id=(x, y, 1 - c), device_id_type=MESH))
        for cp in copies:
            cp.start()
        for cp in copies:
            cp.wait_recv()
        for cp in copies:
            cp.wait_send()

    return pl.pallas_call(
        body, name="pair_halves",
        in_specs=[HBM_SPEC, HBM_SPEC], out_specs=[HBM_SPEC, HBM_SPEC],
        out_shape=[jax.ShapeDtypeStruct(gt.shape, F32), jax.ShapeDtypeStruct(go.shape, F32)],
        input_output_aliases={0: 0, 1: 1},
        scratch_shapes=[pltpu.SemaphoreType.DMA((2 * PAIR_CHUNKS,)), pltpu.SemaphoreType.DMA((2 * PAIR_CHUNKS,))],
    )(gt, go)


VEC = (("g_pre", 2048), ("g_post", 2048), ("b_qkv", 1280), ("ln_v_g", 1024), ("ln_v_b", 1024), ("attn_sinks", 16))
VEC_ROWS = 8
LOSS_ROW = len(VEC)
MAT = (("w_spatial", NG * T), ("b_spatial", NG))
MAT_ROWS = sum(r for _, r in MAT)


class _SlotGather:
    def __init__(self, buf, send_sems, recv_sems):
        self.buf, self.send, self.recv = buf, send_sems, recv_sems
        x, y, c = _mesh_pos()
        self.c, self.me, self.sibling, self.my_id = c, (x, y, c), (x, y, 1 - c), 4 * x + 2 * y + c
        self.chips = [(1 - x, y), (x, 1 - y), (1 - x, 1 - y)]

    def _copy(self, k, blk, to):
        px, py, pc = blk
        slot = self.buf.at[4 * px + 2 * py + pc]
        return pltpu.make_async_remote_copy(
            src_ref=slot, dst_ref=slot, send_sem=self.send.at[k], recv_sem=self.recv.at[k],
            device_id=to, device_id_type=MESH)

    def _first(self):
        return [self._copy(0, self.me, self.sibling)] + [
            self._copy(1 + j, self.me, (*chip, self.c)) for j, chip in enumerate(self.chips)]

    def start(self):
        for cp in self._first():
            cp.start()

    def finish(self):
        passed = [self._copy(4 + j, (*chip, self.c), self.sibling) for j, chip in enumerate(self.chips)]
        for j, chip in enumerate(self.chips):
            self._copy(1 + j, (*chip, self.c), self.me).wait_recv()
            passed[j].start()
        self._copy(0, self.sibling, self.me).wait_recv()
        for j, chip in enumerate(self.chips):
            self._copy(4 + j, (*chip, 1 - self.c), self.me).wait_recv()
        for cp in self._first() + passed:
            cp.wait_send()

    def total(self):
        t = self.buf[0]
        for d in range(1, N_DEV):
            t = t + self.buf[d]
        return t


def _small_update(vec_grads, loss_part, mat_total, vec_state, mat_state):
    n_vec, n_mat = len(VEC), len(MAT)
    n_par = n_vec + n_mat
    n_in = n_vec + 2 + 3 * n_par

    def body(*refs):
        g_in, loss_in, tot_m = refs[:n_vec], refs[n_vec], refs[n_vec + 1]
        st_in = refs[n_vec + 2:n_in]
        outs, loss_out = refs[n_in:n_in + 4 * n_par], refs[n_in + 4 * n_par]
        vbuf, tot_v, send_sems, recv_sems = refs[n_in + 4 * n_par + 1:]
        gather = _SlotGather(vbuf, send_sems, recv_sems)
        vbuf[gather.my_id] = jnp.zeros((VEC_ROWS, D), F32)
        for r, (_, n) in enumerate(VEC):
            vbuf[gather.my_id, r:r + 1, 0:n] = g_in[r][...]
        vbuf[gather.my_id, LOSS_ROW:LOSS_ROW + 1, 0:128] = loss_in[...]
        gather.start()
        gather.finish()
        tot_v[...] = gather.total()
        loss_out[...] = tot_v[LOSS_ROW:LOSS_ROW + 1, 0:128]
        r0 = 0
        for q in range(n_par):
            if q < n_vec:
                g = tot_v[q:q + 1, 0:VEC[q][1]]
            else:
                rows = MAT[q - n_vec][1]
                g = tot_m[r0:r0 + rows, :]
                r0 += rows
            w, m, v = (st_in[3 * q + t][...] for t in range(3))
            outs[4 * q][...] = g
            outs[4 * q + 1][...], outs[4 * q + 2][...], outs[4 * q + 3][...] = _adamw_math(w, g, m, v)

    state = [a for wmv in list(vec_state) + list(mat_state) for a in wmv]
    vmem = pl.BlockSpec(memory_space=pltpu.VMEM)
    shapes = [g.shape for g in vec_grads] + [wmv[0].shape for wmv in mat_state]
    out_shape = [jax.ShapeDtypeStruct(s, F32) for s in shapes for _ in range(4)]
    out_shape.append(jax.ShapeDtypeStruct((1, 128), F32))
    res = pl.pallas_call(
        body, name="small_update",
        in_specs=[vmem] * n_in, out_specs=[vmem] * len(out_shape), out_shape=out_shape,
        scratch_shapes=[pltpu.VMEM((N_DEV, VEC_ROWS, D), F32), pltpu.VMEM((VEC_ROWS, D), F32),
                        pltpu.SemaphoreType.DMA((7,)), pltpu.SemaphoreType.DMA((7,))],
        compiler_params=pltpu.CompilerParams(vmem_limit_bytes=VMEM_LIMIT),
    )(*vec_grads, loss_part, mat_total, *state)
    return [res[4 * q:4 * q + 4] for q in range(n_par)], res[-1]


def kernel(x, positions, g_pre, w_in, b_qkv, ln_v_g, ln_v_b, w_spatial, b_spatial, attn_sinks, w_out, g_post, loss_target, m_g_pre, m_w_in, m_b_qkv, m_ln_v_g, m_ln_v_b, m_w_spatial, m_b_spatial, m_attn_sinks, m_w_out, m_g_post, v_g_pre, v_w_in, v_b_qkv, v_ln_v_g, v_ln_v_b, v_w_spatial, v_b_spatial, v_attn_sinks, v_w_out, v_g_post):
    S = x.shape[1]
    c = lax.axis_index("c")
    c_arr = jnp.reshape(c, (1,)).astype(jnp.int32)
    x2 = x[0]
    target = loss_target[0]
    pos = positions.reshape(S, 1)
    half = HD // 2
    inv_freq = ROPE_THETA ** (-jnp.arange(half, dtype=F32) * (2.0 / HD))
    invf = jnp.tile(inv_freq, 128 // half).reshape(1, 128)
    bias = jnp.concatenate([jnp.zeros((OFF_Q,), F32), b_qkv[0], jnp.zeros((D_IN - OFF_ZB,), F32)]).reshape(1, D_IN)
    b_s_col = b_spatial[0].reshape(NG, T, 1)
    sinks = jnp.repeat(attn_sinks[0], T).reshape(NQ * T, 1)

    chip = 2 * lax.axis_index("x") + lax.axis_index("y")
    wt_part = lax.dynamic_slice_in_dim(w_in[0].T.astype(BF16), c * W_IN_ROWS, W_IN_ROWS, axis=0)
    wo_part = lax.dynamic_slice_in_dim(w_out[0].astype(BF16), c * W_OUT_ROWS, W_OUT_ROWS, axis=0)
    sched = jnp.asarray(PROJ_SCHEDULE, jnp.int32)[chip]

    h, proj, wt, cos, sin = _prenorm_inproj(sched, x2, g_pre, bias, wt_part, pos, invf)
    ycat, wo, q_roped = _mid_fwd(proj, cos, sin, ln_v_g, ln_v_b, w_spatial[0], b_s_col, sinks, wo_part)
    dy, dout, loss_part, dg_post = _outproj_loss(ycat, wo, x2, target, g_post)

    dwo = _matmul_tn(ycat, dy, 1024, "dw_out")
    dycat, ro = _dycat(dy, wo, dwo)
    po = _presum(c_arr, dwo, ro, W_OUT_ROWS)
    dproj, dln_g, dln_b, dws, dbs, dsink, dbqkv, bo = _mid_bwd(
        proj, dycat, q_roped, cos, sin, ln_v_g, ln_v_b, w_spatial[0], b_s_col, sinks, po)
    dwt, mat_total = _matmul_tn(dproj, h, 768, "dw_in_t",
                                mat_grads=[dws.reshape(NG * T, T), dbs.reshape(NG, T)])
    pt = _sibling_presum(c_arr, dwt, W_IN_ROWS, "sibling_presum_in")
    grad_x, dg_pre, bt = _dh_prenorm_bwd(dproj, wt, x2, dout, g_pre, pt)
    gt, go = _pair_halves(_sum_chips(c_arr, bt, "sum_chips_in"), _sum_chips(c_arr, bo, "sum_chips_out"))

    g_w_in, d_w_in, nm_w_in, nv_w_in = (a.T for a in _adamw(w_in[0].T, gt, m_w_in[0].T, v_w_in[0].T, "adamw_w_in"))
    g_w_out, d_w_out, nm_w_out, nv_w_out = _adamw(w_out[0], go, m_w_out[0], v_w_out[0], "adamw_w_out")

    state = {"g_pre": (g_pre, m_g_pre, v_g_pre), "g_post": (g_post, m_g_post, v_g_post),
             "b_qkv": (b_qkv, m_b_qkv, v_b_qkv), "ln_v_g": (ln_v_g, m_ln_v_g, v_ln_v_g),
             "ln_v_b": (ln_v_b, m_ln_v_b, v_ln_v_b), "attn_sinks": (attn_sinks, m_attn_sinks, v_attn_sinks),
             "w_spatial": tuple(a.reshape(NG * T, T) for a in (w_spatial, m_w_spatial, v_w_spatial)),
             "b_spatial": tuple(a.reshape(NG, T) for a in (b_spatial, m_b_spatial, v_b_spatial))}
    grads = {"g_pre": dg_pre, "g_post": dg_post, "b_qkv": dbqkv, "ln_v_g": dln_g, "ln_v_b": dln_b,
             "attn_sinks": dsink[:, 0].reshape(1, NQ)}
    results, loss = _small_update([grads[n] for n, _ in VEC], loss_part, mat_total,
                                  [state[n] for n, _ in VEC], [state[n] for n, _ in MAT])
    small = {n: [a.reshape(w.shape) for a in res]
             for (n, _), res, w in zip(VEC + MAT, results, [state[n][0] for n, _ in VEC + MAT])}
    small["w_spatial"] = [a.reshape(w_spatial.shape) for a in small["w_spatial"]]
    small["b_spatial"] = [a.reshape(b_spatial.shape) for a in small["b_spatial"]]
    big = {"w_in": [a[None] for a in (g_w_in, d_w_in, nm_w_in, nv_w_in)],
           "w_out": [a[None] for a in (g_w_out, d_w_out, nm_w_out, nv_w_out)]}
    order = ("g_pre", "w_in", "b_qkv", "ln_v_g", "ln_v_b", "w_spatial", "b_spatial", "attn_sinks", "w_out", "g_post")
    leaves = {**small, **big}
    return (loss[0, 0], grad_x[None], *[leaves[n][t] for t in range(4) for n in order])
```

```python
import jax
import jax.numpy as jnp
from jax import lax
from jax.experimental import pallas as pl
from jax.experimental.pallas import tpu as pltpu

F32 = jnp.float32
BF16 = jnp.bfloat16
MESH = pl.DeviceIdType.MESH

D = 2048
DG = 1024
T = 128
NG = 8
HD = 64
NQ = 16
D_IN = 5376
OFF_U, OFF_V, OFF_ZA, OFF_Q, OFF_K, OFF_VA, OFF_ZB = 0, 1024, 2048, 3072, 4096, 4224, 4352
D_QKV = 1280
EPS = 1e-6
ROPE_THETA = 10000.0
N_CHIPS = 4
N_DEV = 8
W_IN_ROWS = D_IN // N_DEV
W_OUT_ROWS = D // N_DEV

ADAM_LR, ADAM_B1, ADAM_B2, ADAM_EPS, ADAM_WD, ADAM_STEP = 0.001, 0.9, 0.999, 1e-08, 0.01, 10

VMEM_LIMIT = 56 * 1024 * 1024


def _tile(n, pref):
    return pref if n % pref == 0 else n


def _params(sem=None, vmem=VMEM_LIMIT):
    return pltpu.CompilerParams(dimension_semantics=sem, vmem_limit_bytes=vmem)


def _sigmoid(z):
    return 1.0 / (1.0 + jnp.exp(-z))


def _row_mean(v):
    return jnp.mean(v, axis=-1, keepdims=True)


def _dot(a, b, dims):
    return lax.dot_general(a, b, (dims, ((), ())), preferred_element_type=F32)


NN = ((1,), (0,))
NT = ((1,), (1,))
TN = ((0,), (0,))


PROJ_TN = 768
N_PROJ_TILES = D_IN // PROJ_TN
PROJ_SCHEDULE = ((0, 1, 2, 4, 3, 6, 5, 5), (2, 1, 0, 6, 4, 3, 5, 4), (4, 5, 6, 0, 2, 1, 3, 4), (6, 5, 4, 2, 3, 0, 1, 5))


def _place_locally(src_ref, dst_rows_ref, sems, n_chunks):
    ch = src_ref.shape[0] // n_chunks
    return [pltpu.make_async_copy(src_ref.at[pl.ds(q * ch, ch), :], dst_rows_ref.at[pl.ds(q * ch, ch), :],
                                  sems.at[q]) for q in range(n_chunks)]


def _prenorm_inproj(sched, x, g, bias, wt_part, pos, invf):
    S = x.shape[0]
    tp, tm = _tile(S, 512), _tile(S, 1024)
    n_pre, ns = S // tp, S // tm
    n_steps = n_pre + N_PROJ_TILES * ns
    pos_of = lambda i: jnp.maximum(i - n_pre, 0) // ns
    row_of = lambda i: jnp.maximum(i - n_pre, 0) % ns

    def body(sched_ref, x_ref, g_ref, b_ref, wpart_ref, pos_ref, invf_ref, h_ref, proj_ref, wt_ref, cos_ref, sin_ref,
             h_all, w_tile, stage, send_sems, recv_sems, w_sems, local_sems):
        i = pl.program_id(0)
        x_, y_, c = _mesh_pos()
        me, sibling = (x_, y_, c), (x_, y_, 1 - c)
        chips = [(1 - x_, y_), (x_, 1 - y_), (1 - x_, 1 - y_)]

        def block(pos):
            px, py, pc = pos
            return wt_ref.at[pl.ds(pl.multiple_of((4 * px + 2 * py + pc) * W_IN_ROWS, 16), W_IN_ROWS), :]

        def copy(k, blk, to):
            return pltpu.make_async_remote_copy(
                src_ref=wpart_ref if blk is me else block(blk), dst_ref=block(blk),
                send_sem=send_sems.at[k], recv_sem=recv_sems.at[k], device_id=to, device_id_type=MESH)

        stage_in = pltpu.make_async_copy(wpart_ref, stage, local_sems.at[0])
        stage_out = pltpu.make_async_copy(stage, block(me), local_sems.at[1])

        relay = (jnp.where(c == 0, x_, 1 - x_), jnp.where(c == 0, 1 - y_, y_))
        relayed = (jnp.where(c == 0, 1 - x_, x_), jnp.where(c == 0, y_, 1 - y_))

        early = c == y_
        y_send = copy(2, me, (*chips[1], c))

        def own_sends():
            return [copy(0, me, sibling), y_send, copy(1, me, (*chips[0], c))]

        def passed_on():
            return [copy(4, (*chips[0], c), sibling), copy(5, (*chips[1], c), sibling),
                    copy(3, (*relayed, c), (*relay, c))]

        def early_block_arrives():
            @pl.when(early)
            def _():
                copy(2, (*chips[1], c), me).wait_recv()
                copy(5, (*chips[1], c), sibling).start()
                y_send.start()

            pl.when(jnp.logical_not(early))(lambda: copy(5, (*chips[1], 1 - c), me).wait_recv())

        def neighbours_arrive():
            copy(1, (*chips[0], c), me).wait_recv()
            copy(4, (*chips[0], c), sibling).start()

            @pl.when(early)
            def _():
                copy(3, (*relayed, c), (*relay, c)).start()
                copy(5, (*chips[1], 1 - c), me).wait_recv()

            @pl.when(jnp.logical_not(early))
            def _():
                copy(2, (*chips[1], c), me).wait_recv()
                copy(5, (*chips[1], c), sibling).start()
                copy(3, (*relayed, c), (*relay, c)).start()

        def diagonal_arrives():
            copy(3, (*chips[2], c), me).wait_recv()
            copy(6, (*chips[2], c), sibling).start()
            copy(6, (*chips[2], 1 - c), me).wait_recv()

        def tile_load(p):
            slot = p % 2
            rows = wt_ref.at[pl.ds(pl.multiple_of(sched_ref[p] * PROJ_TN, 16), PROJ_TN), :]
            return pltpu.make_async_copy(rows, w_tile.at[slot], w_sems.at[slot])

        def prepare(p):
            p = jnp.asarray(p, jnp.int32)

            @pl.when(p == 0)
            def _():
                copy(0, sibling, me).wait_recv()
                stage_out.wait()

            pl.when(p == 1)(early_block_arrives)
            pl.when(p == 2)(neighbours_arrive)
            pl.when(p == 3)(lambda: copy(4, (*chips[0], 1 - c), me).wait_recv())
            pl.when(p == sched_ref[N_PROJ_TILES])(diagonal_arrives)
            tile_load(p).start()

        @pl.when(i == 0)
        def _():
            stage_in.start()
            copy(0, me, sibling).start()
            copy(1, me, (*chips[0], c)).start()
            pl.when(jnp.logical_not(early))(y_send.start)
            stage_in.wait()
            stage_out.start()

        @pl.when(i < n_pre)
        def _():
            xv = x_ref[...]
            r = lax.rsqrt(_row_mean(xv * xv) + EPS)
            hv = (xv * r * g_ref[...]).astype(BF16)
            h_ref[...] = hv
            h_all[pl.ds(pl.multiple_of(i * tp, tp), tp), :] = hv
            ang = pos_ref[...].astype(F32) * invf_ref[...]
            cos_ref[...] = jnp.cos(ang)
            sin_ref[...] = jnp.sin(ang)

        pl.when(i == n_pre - 1)(lambda: prepare(0))

        @pl.when(i >= n_pre)
        def _():
            p, s = pos_of(i), row_of(i)
            pl.when(s == 0)(lambda: tile_load(p).wait())
            pl.when((s == ns - 1) & (p < N_PROJ_TILES - 1))(lambda: prepare(p + 1))
            hv = h_all[pl.ds(pl.multiple_of(s * tm, tm), tm), :]
            proj_ref[...] = _dot(hv, w_tile[p % 2], NT) + b_ref[...]

        @pl.when(i == n_steps - 1)
        def _():
            for cp in own_sends() + passed_on() + [copy(6, (*chips[2], c), sibling)]:
                cp.wait_send()

    return pl.pallas_call(
        body, name="prenorm_inproj",
        grid_spec=pltpu.PrefetchScalarGridSpec(
            num_scalar_prefetch=1, grid=(n_steps,),
            in_specs=[pl.BlockSpec((tp, D), lambda i, sc: (jnp.minimum(i, n_pre - 1), 0)),
                      pl.BlockSpec((1, D), lambda i, sc: (0, 0)),
                      pl.BlockSpec((1, PROJ_TN), lambda i, sc: (0, sc[pos_of(i)])),
                      HBM_SPEC,
                      pl.BlockSpec((tp, 1), lambda i, sc: (jnp.minimum(i, n_pre - 1), 0)),
                      pl.BlockSpec((1, 128), lambda i, sc: (0, 0))],
            out_specs=[pl.BlockSpec((tp, D), lambda i, sc: (jnp.minimum(i, n_pre - 1), 0)),
                       pl.BlockSpec((tm, PROJ_TN), lambda i, sc: (row_of(i), sc[pos_of(i)])),
                       HBM_SPEC,
                       pl.BlockSpec((tp, 128), lambda i, sc: (jnp.minimum(i, n_pre - 1), 0)),
                       pl.BlockSpec((tp, 128), lambda i, sc: (jnp.minimum(i, n_pre - 1), 0))],
            scratch_shapes=[pltpu.VMEM((S, D), BF16), pltpu.VMEM((2, PROJ_TN, D), BF16),
                            pltpu.VMEM((W_IN_ROWS, D), BF16),
                            pltpu.SemaphoreType.DMA((7,)), pltpu.SemaphoreType.DMA((7,)),
                            pltpu.SemaphoreType.DMA((2,)), pltpu.SemaphoreType.DMA((2,))]),
        out_shape=[jax.ShapeDtypeStruct((S, D), BF16), jax.ShapeDtypeStruct((S, D_IN), F32),
                   jax.ShapeDtypeStruct((D_IN, D), BF16),
                   jax.ShapeDtypeStruct((S, 128), F32), jax.ShapeDtypeStruct((S, 128), F32)],
        compiler_params=_params(("arbitrary",)),
    )(sched, x, g, bias, wt_part, pos, invf)


def _rot_half(xs, first_half):
    return jnp.where(first_half, -pltpu.roll(xs, 96, 1), pltpu.roll(xs, 32, 1))


GH = NQ // 2


MASKED = -1e30


def _attn_consts():
    lane = lax.broadcasted_iota(jnp.int32, (T, 128), 1)
    row = lax.broadcasted_iota(jnp.int32, (GH * T, T), 0) & (T - 1)
    on_diag_or_below = row >= lax.broadcasted_iota(jnp.int32, (GH * T, T), 1)
    return (lane & (HD - 1)) < (HD // 2), lane < HD, on_diag_or_below


def _stack_heads(slab_fn, grp, lo64):
    blocks = []
    for jj in range(GH // 2):
        s = slab_fn(GH // 2 * grp + jj)
        blocks += [jnp.where(lo64, s, 0.0), jnp.where(lo64, 0.0, s)]
    return jnp.concatenate(blocks, axis=0)


def _unstack_heads(stacked, jj, lo64):
    return jnp.where(lo64, stacked[(2 * jj) * T:(2 * jj + 1) * T], stacked[(2 * jj + 1) * T:(2 * jj + 2) * T])


def _both_halves(kv, grp):
    lo = lax.broadcasted_iota(jnp.int32, kv.shape, 1) < HD
    swapped = pltpu.roll(kv, HD, 1)
    return jnp.where(lo, kv, swapped) if grp == 0 else jnp.where(lo, swapped, kv)


def _one_half(acc, grp):
    lo = lax.broadcasted_iota(jnp.int32, acc.shape, 1) < HD
    return jnp.where(lo if grp == 0 else jnp.logical_not(lo), acc + pltpu.roll(acc, HD, 1), 0.0)


def _layer_norm_stats(v):
    mu = _row_mean(v)
    xc = v - mu
    var = _row_mean(xc * xc)
    rs = lax.rsqrt(var + EPS)
    return xc * rs, rs


def _fold(both, own):
    return jnp.where(own, both[:, T:2 * T], both[:, 0:T])


def _unfold(p, own):
    return jnp.concatenate([jnp.where(own, 0.0, p), jnp.where(own, p, 0.0)], axis=1).astype(BF16)


def _band_softmax(q_scaled, k_both, own, has_prev, sink):
    s_both = _dot(q_scaled, k_both, NT)
    s_prev = s_both[:, 0:T]
    if has_prev is not None:
        s_prev = s_prev + jnp.where(has_prev, 0.0, MASKED)
    s = jnp.where(own, s_both[:, T:2 * T], s_prev)
    m = jnp.maximum(jnp.max(s, axis=-1, keepdims=True), sink)
    e = jnp.exp(s - m)
    es = jnp.exp(sink - m)
    inv = 1.0 / (jnp.sum(e, axis=-1, keepdims=True) + es)
    return e * inv, es * inv


BPS_FWD = 4
BPS_BWD = 4


def _mid_specs(nt, rev, bps):
    tile = (lambda i: nt - 1 - i) if rev else (lambda i: i)
    prev = lambda i: jnp.maximum(bps * tile(i) - 1, 0)
    rows = bps * T
    return tile, [
        pl.BlockSpec((rows, D_IN), lambda i: (tile(i), 0)),
        pl.BlockSpec((T, 2 * T), lambda i: (prev(i), OFF_K // (2 * T))),
    ], [
        pl.BlockSpec((rows, 128), lambda i: (tile(i), 0)),
        pl.BlockSpec((rows, 128), lambda i: (tile(i), 0)),
        pl.BlockSpec((T, 128), lambda i: (prev(i), 0)),
        pl.BlockSpec((T, 128), lambda i: (prev(i), 0)),
        pl.BlockSpec((1, DG), lambda i: (0, 0)),
        pl.BlockSpec((1, DG), lambda i: (0, 0)),
        pl.BlockSpec((NG, T, T), lambda i: (0, 0, 0)),
        pl.BlockSpec((NG, T, 1), lambda i: (0, 0, 0)),
        pl.BlockSpec((NQ * T, 1), lambda i: (0, 0)),
    ]


def _rope(xs, cosv, sinv, first_half):
    return xs * cosv + _rot_half(xs, first_half) * sinv


def _mid_fwd(proj, cos, sin, ln_g, ln_b, w_s, b_s, sinks, wo_part):
    S = proj.shape[0]
    BPS = BPS_FWD if S % (BPS_FWD * T) == 0 else 1
    nt = S // (BPS * T)
    tile, proj_specs, par_specs = _mid_specs(nt, False, BPS)

    def body(p_ref, kvp_ref, cos_ref, sin_ref, cosp_ref, sinp_ref, lng_ref, lnb_ref, ws_ref, bs_ref, sink_ref,
             wpart_ref, y_ref, wo_ref, q_ref, wm_ref, send_sems, recv_sems, local_sems):
        i = pl.program_id(0)
        first_half, lo64, own = _attn_consts()
        gather = _RowGather(wpart_ref, wo_ref, W_OUT_ROWS, send_sems, recv_sems, local_sems)
        pl.when(i == 0)(gather.start)
        pl.when(i == nt // 2)(gather.pass_on)

        @pl.when(i == 0)
        def _():
            tril = lax.broadcasted_iota(jnp.int32, (T, T), 0) >= lax.broadcasted_iota(jnp.int32, (T, T), 1)
            for g in range(NG):
                wm_ref[g] = jnp.where(tril, ws_ref[g], 0.0).astype(BF16)

        def block(b, k_prev, v_prev, has_prev):
            rows = slice(b * T, (b + 1) * T)
            xhat, _ = _layer_norm_stats(p_ref[rows, OFF_V:OFF_V + DG])
            vn = xhat * lng_ref[...] + lnb_ref[...]
            for g in range(NG):
                sl = slice(128 * g, 128 * g + 128)
                mixed = _dot(wm_ref[g], vn[:, sl].astype(BF16), NN) + bs_ref[g]
                z = p_ref[rows, OFF_ZA + 128 * g:OFF_ZA + 128 * g + 128]
                u = p_ref[rows, OFF_U + 128 * g:OFF_U + 128 * g + 128]
                y_ref[rows, sl] = (u * mixed * (z * _sigmoid(z))).astype(BF16)

            cosv, sinv = cos_ref[rows, :], sin_ref[rows, :]
            k_cur = _rope(p_ref[rows, OFF_K:OFF_K + 128], cosv, sinv, first_half)
            v_cur = p_ref[rows, OFF_VA:OFF_VA + 128]

            def q_slab(j):
                q = (_rope(p_ref[rows, OFF_Q + 128 * j:OFF_Q + 128 * j + 128], cosv, sinv, first_half)
                     * (HD ** -0.5)).astype(BF16)
                q_ref[rows, 128 * j:128 * j + 128] = q
                return q

            k_both = jnp.concatenate([k_prev, k_cur], axis=0)
            v_both = jnp.concatenate([v_prev, v_cur], axis=0)
            for grp in range(2):
                qs = _stack_heads(q_slab, grp, lo64)
                kg, vg = _both_halves(k_both, grp).astype(BF16), _both_halves(v_both, grp).astype(BF16)
                p, _ = _band_softmax(qs, kg, own, has_prev, sink_ref[GH * T * grp:GH * T * (grp + 1), :])
                o_st = _dot(_unfold(p, own), vg, NN)
                for jj in range(GH // 2):
                    c0 = 128 * (GH // 2 * grp + jj)
                    zb = p_ref[rows, OFF_ZB + c0:OFF_ZB + c0 + 128]
                    o = _unstack_heads(o_st, jj, lo64)
                    y_ref[rows, DG + c0:DG + c0 + 128] = (o * (zb * _sigmoid(zb))).astype(BF16)
            return k_cur, v_cur

        k_prev = _rope(kvp_ref[:, 0:128], cosp_ref[...], sinp_ref[...], first_half)
        kv = block(0, k_prev, kvp_ref[:, 128:256], i > 0)
        for b in range(1, BPS):
            kv = block(b, *kv, None)
        pl.when(i == nt - 1)(gather.finish)

    return pl.pallas_call(
        body, name="mid_fwd", grid=(nt,),
        in_specs=proj_specs + par_specs + [HBM_SPEC],
        out_specs=[pl.BlockSpec((BPS * T, D), lambda i: (tile(i), 0)), HBM_SPEC,
                   pl.BlockSpec((BPS * T, NQ * HD), lambda i: (tile(i), 0))],
        out_shape=[jax.ShapeDtypeStruct((S, D), BF16), jax.ShapeDtypeStruct((D, D), BF16),
                   jax.ShapeDtypeStruct((S, NQ * HD), BF16)],
        scratch_shapes=[pltpu.VMEM((NG, T, T), BF16)] + GATHER_SEMS,
        compiler_params=_params(("arbitrary",)),
    )(proj, proj, cos, sin, cos, sin, ln_g, ln_b, w_s, b_s, sinks, wo_part)


def _mid_bwd(proj, dycat, q_roped, cos, sin, ln_g, ln_b, w_s, b_s, sinks, po):
    S = proj.shape[0]
    BPS = BPS_BWD if S % (BPS_BWD * T) == 0 else 1
    nt = S // (BPS * T)
    tile, proj_specs, par_specs = _mid_specs(nt, True, BPS)
    const2 = lambda i: (0, 0)

    def body(p_ref, kvp_ref, dyc_ref, q_ref, cos_ref, sin_ref, cosp_ref, sinp_ref, lng_ref, lnb_ref, ws_ref, bs_ref,
             sink_ref, po_ref, dp_ref, dlng_ref, dlnb_ref, dws_ref, dbs_ref, dsink_ref, dbqkv_ref, bo_ref,
             carry_ref, dvn_ref, wm_ref, wmt_ref, send_sems, recv_sems, local_sem):
        i = pl.program_id(0)
        first_half, lo64, own = _attn_consts()
        tril = lax.broadcasted_iota(jnp.int32, (T, T), 0) >= lax.broadcasted_iota(jnp.int32, (T, T), 1)
        exchange = _OwnerExchange(po_ref, bo_ref, send_sems, recv_sems, local_sem)
        pl.when(i == 0)(exchange.start)

        @pl.when(i == 0)
        def _():
            for g in range(NG):
                wm = jnp.where(tril, ws_ref[g], 0.0)
                wm_ref[g] = wm.astype(BF16)
                wmt_ref[g] = wm.T.astype(BF16)
            dlng_ref[...] = jnp.zeros_like(dlng_ref)
            dlnb_ref[...] = jnp.zeros_like(dlnb_ref)
            dws_ref[...] = jnp.zeros_like(dws_ref)
            dbs_ref[...] = jnp.zeros_like(dbs_ref)
            dsink_ref[...] = jnp.zeros_like(dsink_ref)
            dbqkv_ref[...] = jnp.zeros_like(dbqkv_ref)
            carry_ref[...] = jnp.zeros_like(carry_ref)

        def roped_k(b):
            rows = slice(b * T, (b + 1) * T)
            return _rope(p_ref[rows, OFF_K:OFF_K + 128], cos_ref[rows, :], sin_ref[rows, :], first_half)

        def block(b, k_prev, v_prev, has_prev, dk_next, dv_next):
            rows = slice(b * T, (b + 1) * T)
            xhat, rs = _layer_norm_stats(p_ref[rows, OFF_V:OFF_V + DG])
            lng = lng_ref[...]
            vn = xhat * lng + lnb_ref[...]
            for g in range(NG):
                sl = slice(128 * g, 128 * g + 128)
                vng = vn[:, sl].astype(BF16)
                mixed = _dot(wm_ref[g], vng, NN) + bs_ref[g]
                z = p_ref[rows, OFF_ZA + 128 * g:OFF_ZA + 128 * g + 128]
                u = p_ref[rows, OFF_U + 128 * g:OFF_U + 128 * g + 128]
                dy = dyc_ref[rows, sl]
                sg = _sigmoid(z)
                sa = z * sg
                dp_ref[rows, OFF_U + 128 * g:OFF_U + 128 * g + 128] = (dy * mixed * sa).astype(BF16)
                dp_ref[rows, OFF_ZA + 128 * g:OFF_ZA + 128 * g + 128] = (
                    dy * u * mixed * (sg * (1.0 + z * (1.0 - sg)))).astype(BF16)
                dm = dy * u * sa
                dmb = dm.astype(BF16)
                dvn_ref[rows, sl] = _dot(wmt_ref[g], dmb, NN)
                dws_ref[g] += jnp.where(tril, _dot(dmb, vng, NT), 0.0)
                dbs_ref[g] += jnp.sum(dm, axis=1, keepdims=True)
            dvn = dvn_ref[rows, :]
            dlng_ref[...] += jnp.sum(dvn * xhat, axis=0, keepdims=True)
            dlnb_ref[...] += jnp.sum(dvn, axis=0, keepdims=True)
            dxh = dvn * lng
            dv_g = rs * (dxh - _row_mean(dxh)
                         - xhat * _row_mean(dxh * xhat))
            dp_ref[rows, OFF_V:OFF_V + DG] = dv_g.astype(BF16)

            cosv, sinv = cos_ref[rows, :], sin_ref[rows, :]
            k_cur = roped_k(b)
            v_cur = p_ref[rows, OFF_VA:OFF_VA + 128]

            def q_slab(j):
                return q_ref[rows, 128 * j:128 * j + 128]

            def do_slab(j):
                zb = p_ref[rows, OFF_ZB + 128 * j:OFF_ZB + 128 * j + 128]
                return dyc_ref[rows, DG + 128 * j:DG + 128 * j + 128] * (zb * _sigmoid(zb))

            k_both = jnp.concatenate([k_prev, k_cur], axis=0)
            v_both = jnp.concatenate([v_prev, v_cur], axis=0)
            dk_both, dv_both = jnp.zeros((2 * T, 128), F32), jnp.zeros((2 * T, 128), F32)
            for grp in range(2):
                qs = _stack_heads(q_slab, grp, lo64)
                d_o = _stack_heads(do_slab, grp, lo64)
                dob = d_o.astype(BF16)
                kg, vg = _both_halves(k_both, grp).astype(BF16), _both_halves(v_both, grp).astype(BF16)
                p, ps = _band_softmax(qs, kg, own, has_prev, sink_ref[GH * T * grp:GH * T * (grp + 1), :])
                p_both = _unfold(p, own)
                o_st = _dot(p_both, vg, NN)
                delta = jnp.sum(d_o * o_st, axis=-1, keepdims=True)
                ds_both = _unfold(p * (_fold(_dot(dob, vg, NT), own) - delta), own)
                dq_st = _dot(ds_both, kg, NN) * (HD ** -0.5)
                dk_both = dk_both + _one_half(_dot(ds_both, qs, TN), grp)
                dv_both = dv_both + _one_half(_dot(p_both, dob, TN), grp)
                dsink_rows = ps * delta
                for hh in range(GH):
                    h = GH * grp + hh
                    dsink_ref[h:h + 1, :] += jnp.broadcast_to(
                        -jnp.sum(dsink_rows[hh * T:(hh + 1) * T], axis=0, keepdims=True), (1, 128))
                for jj in range(GH // 2):
                    c0 = 128 * (GH // 2 * grp + jj)
                    zb = p_ref[rows, OFF_ZB + c0:OFF_ZB + c0 + 128]
                    sg = _sigmoid(zb)
                    o = _unstack_heads(o_st, jj, lo64)
                    dp_ref[rows, OFF_ZB + c0:OFF_ZB + c0 + 128] = (
                        dyc_ref[rows, DG + c0:DG + c0 + 128] * o * (sg * (1.0 + zb * (1.0 - sg)))).astype(BF16)
                    dq = _unstack_heads(dq_st, jj, lo64)
                    dq_pre = dq * cosv - _rot_half(dq, first_half) * sinv
                    dp_ref[rows, OFF_Q + c0:OFF_Q + c0 + 128] = dq_pre.astype(BF16)
                    dbqkv_ref[:, c0:c0 + 128] += jnp.sum(dq_pre, axis=0, keepdims=True)
            dk_prev, dv_prev = dk_both[0:T], dv_both[0:T]
            dk_cur, dv_cur = dk_both[T:2 * T] + dk_next, dv_both[T:2 * T] + dv_next
            dk_pre = dk_cur * cosv - _rot_half(dk_cur, first_half) * sinv
            dp_ref[rows, OFF_K:OFF_K + 128] = dk_pre.astype(BF16)
            dp_ref[rows, OFF_VA:OFF_VA + 128] = dv_cur.astype(BF16)
            dbqkv_ref[:, 1024:1152] += jnp.sum(dk_pre, axis=0, keepdims=True)
            dbqkv_ref[:, 1152:1280] += jnp.sum(dv_cur, axis=0, keepdims=True)
            return dk_prev, dv_prev

        grads = carry_ref[:, 0:128], carry_ref[:, 128:256]
        for b in range(BPS - 1, 0, -1):
            prows = slice((b - 1) * T, b * T)
            grads = block(b, roped_k(b - 1), p_ref[prows, OFF_VA:OFF_VA + 128], None, *grads)
        k_prev = _rope(kvp_ref[:, 0:128], cosp_ref[...], sinp_ref[...], first_half)
        grads = block(0, k_prev, kvp_ref[:, 128:256], i < nt - 1, *grads)
        carry_ref[:, 0:128], carry_ref[:, 128:256] = grads
        pl.when(i == nt - 1)(exchange.finish)

    return pl.pallas_call(
        body, name="mid_bwd", grid=(nt,),
        in_specs=proj_specs + [pl.BlockSpec((BPS * T, D), lambda i: (tile(i), 0)),
                               pl.BlockSpec((BPS * T, NQ * HD), lambda i: (tile(i), 0))] + par_specs + [HBM_SPEC],
        out_specs=[pl.BlockSpec((BPS * T, D_IN), lambda i: (tile(i), 0)),
                   pl.BlockSpec((1, DG), const2), pl.BlockSpec((1, DG), const2),
                   pl.BlockSpec((NG, T, T), lambda i: (0, 0, 0)), pl.BlockSpec((NG, T, 1), lambda i: (0, 0, 0)),
                   pl.BlockSpec((NQ, 128), const2), pl.BlockSpec((1, D_QKV), const2), HBM_SPEC],
        out_shape=[jax.ShapeDtypeStruct((S, D_IN), BF16),
                   jax.ShapeDtypeStruct((1, DG), F32), jax.ShapeDtypeStruct((1, DG), F32),
                   jax.ShapeDtypeStruct((NG, T, T), F32), jax.ShapeDtypeStruct((NG, T, 1), F32),
                   jax.ShapeDtypeStruct((NQ, 128), F32), jax.ShapeDtypeStruct((1, D_QKV), F32),
                   jax.ShapeDtypeStruct(po.shape, BF16)],
        scratch_shapes=[pltpu.VMEM((T, 2 * T), F32), pltpu.VMEM((BPS * T, DG), F32),
                        pltpu.VMEM((NG, T, T), BF16), pltpu.VMEM((NG, T, T), BF16)] + OWNER_SEMS,
        compiler_params=_params(("arbitrary",)),
    )(proj, proj, dycat, q_roped, cos, sin, cos, sin, ln_g, ln_b, w_s, b_s, sinks, po)


def _outproj_loss(ycat, wo, x, target, g_post):
    S = ycat.shape[0]
    tm = _tile(S, 512)
    nt = S // tm
    n_part = 2 if tm % 32 == 0 else 1
    tp = tm // n_part
    const2 = lambda i: (0, 0)

    def body(yc_ref, w_ref, x_ref, t_ref, g_ref, dy_ref, dout_ref, loss_ref, dg_ref, lacc_ref):
        i = pl.program_id(0)

        @pl.when(i == 0)
        def _():
            dg_ref[...] = jnp.zeros_like(dg_ref)
            lacc_ref[...] = jnp.zeros_like(lacc_ref)

        g = g_ref[...]
        ys = [_dot(yc_ref[q * tp:(q + 1) * tp, :], w_ref[...], NN) for q in range(n_part)]
        for q, y in enumerate(ys):
            rows = slice(q * tp, (q + 1) * tp)
            r = lax.rsqrt(_row_mean(y * y) + EPS)
            yh = y * r
            diff = x_ref[rows, :] + yh * g - t_ref[rows, :]
            lacc_ref[...] += jnp.sum(diff * diff, axis=0, keepdims=True)
            dout = diff * (1.0 / D)
            dout_ref[rows, :] = dout
            dg_ref[...] += jnp.sum(dout * yh, axis=0, keepdims=True)
            dyh = dout * g
            dy_ref[rows, :] = (r * (dyh - yh * _row_mean(dyh * yh))).astype(BF16)

        @pl.when(i == nt - 1)
        def _():
            loss_ref[...] = jnp.broadcast_to(jnp.sum(lacc_ref[...], axis=1, keepdims=True) * (0.5 / D), (1, 128))

    row = lambda i: (i, 0)
    return pl.pallas_call(
        body, name="outproj_loss", grid=(nt,),
        in_specs=[pl.BlockSpec((tm, D), row), pl.BlockSpec((D, D), const2, pipeline_mode=pl.Buffered(1)),
                  pl.BlockSpec((tm, D), row), pl.BlockSpec((tm, D), row), pl.BlockSpec((1, D), const2)],
        out_specs=[pl.BlockSpec((tm, D), row), pl.BlockSpec((tm, D), row), pl.BlockSpec((1, 128), const2),
                   pl.BlockSpec((1, D), const2)],
        out_shape=[jax.ShapeDtypeStruct((S, D), BF16), jax.ShapeDtypeStruct((S, D), F32),
                   jax.ShapeDtypeStruct((1, 128), F32), jax.ShapeDtypeStruct((1, D), F32)],
        scratch_shapes=[pltpu.VMEM((1, D), F32)],
        compiler_params=_params(("arbitrary",)),
    )(ycat, wo, x, target, g_post)


def _dycat(dy, wo, dwo):
    S = dy.shape[0]
    tm = _tile(S, 1024)
    nt = S // tm

    def body(dy_ref, w_ref, dwo_ref, o_ref, r_ref, send_sems, recv_sems):
        i = pl.program_id(0)
        copies = _sibling_copies(dwo_ref, r_ref, W_OUT_ROWS, send_sems, recv_sems)

        @pl.when(i == 0)
        def _():
            for cp in copies:
                cp.start()

        o_ref[...] = _dot(dy_ref[...], w_ref[...], NT)

        @pl.when(i == nt - 1)
        def _():
            for cp in copies:
                cp.wait_recv()
            for cp in copies:
                cp.wait_send()

    return pl.pallas_call(
        body, name="dycat", grid=(nt,),
        in_specs=[pl.BlockSpec((tm, D), lambda i: (i, 0)),
                  pl.BlockSpec((D, D), lambda i: (0, 0), pipeline_mode=pl.Buffered(1)), HBM_SPEC],
        out_specs=[pl.BlockSpec((tm, D), lambda i: (i, 0)), HBM_SPEC],
        out_shape=[jax.ShapeDtypeStruct((S, D), F32), jax.ShapeDtypeStruct((N_CHIPS, W_OUT_ROWS, D), BF16)],
        scratch_shapes=[pltpu.SemaphoreType.DMA((N_CHIPS,)), pltpu.SemaphoreType.DMA((N_CHIPS,))],
        compiler_params=_params(("arbitrary",)),
    )(dy, wo, dwo)


def _matmul_tn(a, b, tm, name, mat_grads=None):
    K, M = a.shape
    N = b.shape[1]
    tk = _tile(K, 2048)
    ni, nk = M // tm, K // tk
    hosting = mat_grads is not None

    def body(a_ref, b_ref, *rest):
        if hosting:
            g_in, (o_ref, tot_ref, acc_ref, mbuf, send_sems, recv_sems) = rest[:len(MAT)], rest[len(MAT):]
        else:
            o_ref, acc_ref = rest
        i, k = pl.program_id(0), pl.program_id(1)

        if hosting:
            gather = _SlotGather(mbuf, send_sems, recv_sems)

            @pl.when((i == 0) & (k == 0))
            def _():
                r0 = 0
                for q, (_, rows) in enumerate(MAT):
                    mbuf[gather.my_id, r0:r0 + rows, :] = g_in[q][...]
                    r0 += rows
                gather.start()

            pl.when((i == ni // 2) & (k == 0))(gather.pass_on)

        @pl.when(k == 0)
        def _():
            acc_ref[...] = jnp.zeros_like(acc_ref)

        acc_ref[...] += _dot(a_ref[...], b_ref[...], TN)

        @pl.when(k == nk - 1)
        def _():
            o_ref[...] = acc_ref[...].astype(BF16)

        if hosting:
            @pl.when((i == ni - 1) & (k == nk - 1))
            def _():
                gather.finish()
                tot_ref[...] = gather.total()

    whole = pl.BlockSpec(memory_space=pltpu.VMEM)
    in_specs = [pl.BlockSpec((tk, tm), lambda i, k: (k, i)), pl.BlockSpec((tk, N), lambda i, k: (k, 0))]
    out_specs = [pl.BlockSpec((tm, N), lambda i, k: (i, 0))]
    out_shape = [jax.ShapeDtypeStruct((M, N), BF16)]
    scratch = [pltpu.VMEM((tm, N), F32)]
    if hosting:
        in_specs += [whole] * len(MAT)
        out_specs.append(whole)
        out_shape.append(jax.ShapeDtypeStruct((MAT_ROWS, 128), F32))
        scratch += [pltpu.VMEM((N_DEV, MAT_ROWS, 128), F32), pltpu.SemaphoreType.DMA((7,)),
                    pltpu.SemaphoreType.DMA((7,))]
    res = pl.pallas_call(
        body, name=name, grid=(ni, nk), in_specs=in_specs, out_specs=out_specs, out_shape=out_shape,
        scratch_shapes=scratch,
        compiler_params=_params(("arbitrary", "arbitrary") if hosting else ("parallel", "arbitrary")),
    )(a, b, *(mat_grads or ()))
    return res if hosting else res[0]


def _dh_prenorm_bwd(dproj, wt, x, dout, g_pre, pt):
    S = x.shape[0]
    tm, tk = _tile(S, 1024), 768
    ne = 4
    te = tm // ne
    ni, nk = S // tm, D_IN // tk
    n_mm = ni * nk
    n_steps = n_mm + ne
    mm_of = lambda t: jnp.minimum(t, n_mm - 1)

    def quarter_of(t):
        j, q = t // nk - 1, t % nk
        nxt = jnp.where(q < ne, j * ne + q, (j + 1) * ne)
        return jnp.clip(nxt, 0, ni * ne - 1)

    def body(dp_ref, w_ref, x_ref, dout_ref, g_ref, pt_ref, gx_ref, dg_ref, bt_ref, acc_ref,
             send_sems, recv_sems, local_sem):
        t = pl.program_id(0)
        exchange = _OwnerExchange(pt_ref, bt_ref, send_sems, recv_sems, local_sem)
        pl.when(t == 0)(exchange.start)

        @pl.when(t == 0)
        def _():
            dg_ref[...] = jnp.zeros_like(dg_ref)

        has_mm = t < n_mm
        has_q = (t >= nk) & (t % nk < ne)

        def matmul():
            slot = (mm_of(t) // nk) % 2
            old = jnp.where(mm_of(t) % nk > 0, acc_ref[slot], 0.0)
            acc_ref[slot] = old + _dot(dp_ref[...], w_ref[...], NN)

        def quarter():
            slot = (t // nk - 1) % 2
            dh = acc_ref[slot, pl.ds(pl.multiple_of((t % nk) * te, te), te), :]
            xv = x_ref[...]
            r = lax.rsqrt(_row_mean(xv * xv) + EPS)
            xh = xv * r
            dg_ref[...] += jnp.sum(dh * xh, axis=0, keepdims=True)
            dxh = dh * g_ref[...]
            gx_ref[...] = dout_ref[...] + r * (dxh - xh * _row_mean(dxh * xh))

        @pl.when(has_mm & has_q)
        def _():
            matmul()
            quarter()

        pl.when(has_mm & jnp.logical_not(has_q))(matmul)
        pl.when(jnp.logical_not(has_mm) & has_q)(quarter)
        pl.when(t == n_steps - 1)(exchange.finish)

    quarter_block = lambda t: (quarter_of(t), 0)
    return pl.pallas_call(
        body, name="dh_prenorm_bwd", grid=(n_steps,),
        in_specs=[pl.BlockSpec((tm, tk), lambda t: (mm_of(t) // nk, mm_of(t) % nk)),
                  pl.BlockSpec((tk, D), lambda t: (mm_of(t) % nk, 0)),
                  pl.BlockSpec((te, D), quarter_block), pl.BlockSpec((te, D), quarter_block),
                  pl.BlockSpec((1, D), lambda t: (0, 0)), HBM_SPEC],
        out_specs=[pl.BlockSpec((te, D), quarter_block), pl.BlockSpec((1, D), lambda t: (0, 0)), HBM_SPEC],
        out_shape=[jax.ShapeDtypeStruct((S, D), F32), jax.ShapeDtypeStruct((1, D), F32),
                   jax.ShapeDtypeStruct(pt.shape, BF16)],
        scratch_shapes=[pltpu.VMEM((2, tm, D), F32)] + OWNER_SEMS,
        compiler_params=_params(("arbitrary",)),
    )(dproj, wt, x, dout, g_pre, pt)


def _presum(c_arr, own, recv, half_rows):
    n_cols = own.shape[-1]
    own4 = own.reshape(N_CHIPS, 2, half_rows, n_cols)

    def body(c_ref, own_ref, recv_ref, o_ref):
        o_ref[...] = (own_ref[...].astype(F32) + recv_ref[...].astype(F32)).astype(BF16)

    return pl.pallas_call(
        body, name="presum_%d" % half_rows,
        grid_spec=pltpu.PrefetchScalarGridSpec(
            num_scalar_prefetch=1, grid=(N_CHIPS,),
            in_specs=[pl.BlockSpec((None, None, half_rows, n_cols), lambda j, c: (j, c[0], 0, 0)),
                      pl.BlockSpec((None, half_rows, n_cols), lambda j, c: (j, 0, 0))],
            out_specs=pl.BlockSpec((None, half_rows, n_cols), lambda j, c: (j, 0, 0))),
        out_shape=jax.ShapeDtypeStruct((N_CHIPS, half_rows, n_cols), BF16),
        compiler_params=_params(("parallel",)),
    )(c_arr, own4, recv)


def _sibling_presum(c_arr, dw, half_rows, name):
    n_cols = dw.shape[-1]
    dw4 = dw.reshape(N_CHIPS, 2, half_rows, n_cols)

    def body(c_ref, own_ref, dw_ref, o_ref, r_ref, stage, send_sems, recv_sems, load_sem):
        j = pl.program_id(0)
        copies = _sibling_copies(dw_ref, r_ref, half_rows, send_sems, recv_sems)

        @pl.when(j == 0)
        def _():
            for cp in copies:
                cp.start()

        for q in range(N_CHIPS):
            pl.when(j == q)(copies[q].wait_recv)
        load = pltpu.make_async_copy(r_ref.at[j], stage, load_sem)
        load.start()
        load.wait()
        o_ref[...] = (own_ref[...].astype(F32) + stage[...].astype(F32)).astype(BF16)

        @pl.when(j == N_CHIPS - 1)
        def _():
            for cp in copies:
                cp.wait_send()

    out, _ = pl.pallas_call(
        body, name=name,
        grid_spec=pltpu.PrefetchScalarGridSpec(
            num_scalar_prefetch=1, grid=(N_CHIPS,),
            in_specs=[pl.BlockSpec((None, None, half_rows, n_cols), lambda j, c: (j, c[0], 0, 0)), HBM_SPEC],
            out_specs=[pl.BlockSpec((None, half_rows, n_cols), lambda j, c: (j, 0, 0)), HBM_SPEC],
            scratch_shapes=[pltpu.VMEM((half_rows, n_cols), BF16), pltpu.SemaphoreType.DMA((N_CHIPS,)),
                            pltpu.SemaphoreType.DMA((N_CHIPS,)), pltpu.SemaphoreType.DMA(())]),
        out_shape=[jax.ShapeDtypeStruct((N_CHIPS, half_rows, n_cols), BF16),
                   jax.ShapeDtypeStruct((N_CHIPS, half_rows, n_cols), BF16)],
        compiler_params=_params(("arbitrary",)),
    )(c_arr, dw4, dw)
    return out


def _sum_chips(c_arr, parts, name):
    _, rows, n_cols = parts.shape
    nt = 2
    tr = rows // nt

    def body(c_ref, p_ref, o_ref):
        o_ref[...] = ((p_ref[0].astype(F32) + p_ref[1].astype(F32)) + p_ref[2].astype(F32)) + p_ref[3].astype(F32)

    return pl.pallas_call(
        body, name=name,
        grid_spec=pltpu.PrefetchScalarGridSpec(
            num_scalar_prefetch=1, grid=(nt,),
            in_specs=[pl.BlockSpec((N_CHIPS, tr, n_cols), lambda i, c: (0, i, 0))],
            out_specs=pl.BlockSpec((tr, n_cols), lambda i, c: (c[0] * nt + i, 0))),
        out_shape=jax.ShapeDtypeStruct((2 * rows, n_cols), F32),
        compiler_params=_params(("parallel",)),
    )(c_arr, parts)


def _adamw_math(w, g, m, v):
    mn = ADAM_B1 * m + (1.0 - ADAM_B1) * g
    vn = ADAM_B2 * v + (1.0 - ADAM_B2) * (g * g)
    m_hat = mn / (1.0 - ADAM_B1 ** ADAM_STEP)
    v_hat = vn / (1.0 - ADAM_B2 ** ADAM_STEP)
    return -ADAM_LR * (m_hat / (jnp.sqrt(v_hat) + ADAM_EPS) + ADAM_WD * w), mn, vn


def _adamw(w, g, m, v, name):
    R, C = w.shape
    tr = next((t for t in (256, 192, 128) if R % t == 0), R)

    def body(w_ref, g_ref, m_ref, v_ref, go_ref, d_ref, mo_ref, vo_ref):
        gv = g_ref[...]
        go_ref[...] = gv
        d_ref[...], mo_ref[...], vo_ref[...] = _adamw_math(w_ref[...], gv, m_ref[...], v_ref[...])

    spec = pl.BlockSpec((tr, C), lambda i: (i, 0))
    shp = jax.ShapeDtypeStruct((R, C), F32)
    return pl.pallas_call(
        body, name=name, grid=(R // tr,), in_specs=[spec] * 4, out_specs=[spec] * 4, out_shape=[shp] * 4,
        compiler_params=_params(("parallel",)),
    )(w, g, m, v)


HBM_SPEC = pl.BlockSpec(memory_space=pltpu.HBM)
GATHER_LOCAL_CHUNKS = 4
GATHER_SEMS = [pltpu.SemaphoreType.DMA((7,)), pltpu.SemaphoreType.DMA((7,)),
               pltpu.SemaphoreType.DMA((GATHER_LOCAL_CHUNKS,))]
OWNER_SEMS = [pltpu.SemaphoreType.DMA((3,)), pltpu.SemaphoreType.DMA((3,)), pltpu.SemaphoreType.DMA(())]


def _mesh_pos():
    return lax.axis_index("x"), lax.axis_index("y"), lax.axis_index("c")


class _RowGather:
    def __init__(self, src_ref, full_ref, rows, send_sems, recv_sems, local_sems):
        self.src, self.full, self.rows = src_ref, full_ref, rows
        self.send, self.recv, self.local = send_sems, recv_sems, local_sems
        x, y, c = _mesh_pos()
        self.c, self.me, self.sibling = c, (x, y, c), (x, y, 1 - c)
        self.chips = [(1 - x, y), (x, 1 - y), (1 - x, 1 - y)]

    def _block(self, pos):
        px, py, pc = pos
        return self.full.at[pl.ds(pl.multiple_of((4 * px + 2 * py + pc) * self.rows, 16), self.rows), :]

    def _copy(self, k, blk, to):
        return pltpu.make_async_remote_copy(
            src_ref=self.src if blk is self.me else self._block(blk), dst_ref=self._block(blk),
            send_sem=self.send.at[k], recv_sem=self.recv.at[k], device_id=to, device_id_type=MESH)

    def _mine(self):
        return _place_locally(self.src, self._block(self.me), self.local, GATHER_LOCAL_CHUNKS)

    def _first(self):
        return [self._copy(0, self.me, self.sibling)] + [
            self._copy(1 + j, self.me, (*chip, self.c)) for j, chip in enumerate(self.chips)]

    def start(self):
        for cp in self._first() + self._mine():
            cp.start()

    def _passed(self):
        return [self._copy(4 + j, (*chip, self.c), self.sibling) for j, chip in enumerate(self.chips)]

    def pass_on(self):
        for j, chip in enumerate(self.chips):
            self._copy(1 + j, (*chip, self.c), self.me).wait_recv()
            self._passed()[j].start()

    def finish(self):
        self._copy(0, self.sibling, self.me).wait_recv()
        for j, chip in enumerate(self.chips):
            self._copy(4 + j, (*chip, 1 - self.c), self.me).wait_recv()
        for cp in self._first() + self._passed():
            cp.wait_send()
        for cp in self._mine():
            cp.wait()


class _OwnerExchange:
    def __init__(self, src_ref, dst_ref, send_sems, recv_sems, local_sem):
        self.src, self.dst, self.send, self.recv, self.local = src_ref, dst_ref, send_sems, recv_sems, local_sem
        x, y, c = _mesh_pos()
        self.c, self.my_chip = c, 2 * x + y
        self.peers = [(1 - x, y), (x, 1 - y), (1 - x, 1 - y)]

    def _copies(self):
        local = pltpu.make_async_copy(self.src.at[self.my_chip], self.dst.at[self.my_chip], self.local)
        remote = [pltpu.make_async_remote_copy(
            src_ref=self.src.at[2 * px + py], dst_ref=self.dst.at[self.my_chip],
            send_sem=self.send.at[k], recv_sem=self.recv.at[k], device_id=(px, py, self.c), device_id_type=MESH)
            for k, (px, py) in enumerate(self.peers)]
        return local, remote

    def start(self):
        local, remote = self._copies()
        local.start()
        for cp in remote:
            cp.start()

    def finish(self):
        local, remote = self._copies()
        for cp in remote:
            cp.wait_recv()
        for cp in remote:
            cp.wait_send()
        local.wait()


def _sibling_copies(dw_ref, r_ref, rows, send_sems, recv_sems):
    x, y, c = _mesh_pos()
    return [pltpu.make_async_remote_copy(
        src_ref=dw_ref.at[pl.ds(pl.multiple_of((2 * j + (1 - c)) * rows, 16), rows), :], dst_ref=r_ref.at[j],
        send_sem=send_sems.at[j], recv_sem=recv_sems.at[j], device_id=(x, y, 1 - c), device_id_type=MESH)
        for j in range(N_CHIPS)]


PAIR_CHUNKS = 4


def _pair_halves(gt, go):
    def body(gt_in, go_in, gt_ref, go_ref, send_sems, recv_sems):
        del gt_in, go_in
        x, y, c = _mesh_pos()
        copies = []
        for a, (ref, rows) in enumerate(((gt_ref, W_IN_ROWS), (go_ref, W_OUT_ROWS))):
            ch = rows // PAIR_CHUNKS
            for q in range(PAIR_CHUNKS):
                part = ref.at[pl.ds(pl.multiple_of(c * rows + q * ch, 8), ch), :]
                copies.append(pltpu.make_async_remote_copy(
                    src_ref=part, dst_ref=part, send_sem=send_sems.at[PAIR_CHUNKS * a + q],
                    recv_sem=recv_sems.at[PAIR_CHUNKS * a + q], device_id=(x, y, 1 - c), device_id_type=MESH))
        for cp in copies:
            cp.start()
        for cp in copies:
            cp.wait_recv()
        for cp in copies:
            cp.wait_send()

    return pl.pallas_call(
        body, name="pair_halves",
        in_specs=[HBM_SPEC, HBM_SPEC], out_specs=[HBM_SPEC, HBM_SPEC],
        out_shape=[jax.ShapeDtypeStruct(gt.shape, F32), jax.ShapeDtypeStruct(go.shape, F32)],
        input_output_aliases={0: 0, 1: 1},
        scratch_shapes=[pltpu.SemaphoreType.DMA((2 * PAIR_CHUNKS,)), pltpu.SemaphoreType.DMA((2 * PAIR_CHUNKS,))],
    )(gt, go)


VEC = (("g_pre", 2048), ("g_post", 2048), ("b_qkv", 1280), ("ln_v_g", 1024), ("ln_v_b", 1024), ("attn_sinks", 16))
VEC_ROWS = 8
LOSS_ROW = len(VEC)
MAT = (("w_spatial", NG * T), ("b_spatial", NG))
MAT_ROWS = sum(r for _, r in MAT)


class _SlotGather:
    def __init__(self, buf, send_sems, recv_sems):
        self.buf, self.send, self.recv = buf, send_sems, recv_sems
        x, y, c = _mesh_pos()
        self.c, self.me, self.sibling, self.my_id = c, (x, y, c), (x, y, 1 - c), 4 * x + 2 * y + c
        self.chips = [(1 - x, y), (x, 1 - y), (1 - x, 1 - y)]

    def _copy(self, k, blk, to):
        px, py, pc = blk
        slot = self.buf.at[4 * px + 2 * py + pc]
        return pltpu.make_async_remote_copy(
            src_ref=slot, dst_ref=slot, send_sem=self.send.at[k], recv_sem=self.recv.at[k],
            device_id=to, device_id_type=MESH)

    def _first(self):
        return [self._copy(0, self.me, self.sibling)] + [
            self._copy(1 + j, self.me, (*chip, self.c)) for j, chip in enumerate(self.chips)]

    def start(self):
        for cp in self._first():
            cp.start()

    def _passed(self):
        return [self._copy(4 + j, (*chip, self.c), self.sibling) for j, chip in enumerate(self.chips)]

    def pass_on(self):
        for j, chip in enumerate(self.chips):
            self._copy(1 + j, (*chip, self.c), self.me).wait_recv()
            self._passed()[j].start()

    def finish(self):
        self._copy(0, self.sibling, self.me).wait_recv()
        for j, chip in enumerate(self.chips):
            self._copy(4 + j, (*chip, 1 - self.c), self.me).wait_recv()
        for cp in self._first() + self._passed():
            cp.wait_send()

    def total(self):
        t = self.buf[0]
        for d in range(1, N_DEV):
            t = t + self.buf[d]
        return t


def _small_update(vec_grads, loss_part, mat_total, vec_state, mat_state):
    n_vec, n_mat = len(VEC), len(MAT)
    n_par = n_vec + n_mat
    n_in = n_vec + 2 + 3 * n_par

    def body(*refs):
        g_in, loss_in, tot_m = refs[:n_vec], refs[n_vec], refs[n_vec + 1]
        st_in = refs[n_vec + 2:n_in]
        outs, loss_out = refs[n_in:n_in + 4 * n_par], refs[n_in + 4 * n_par]
        vbuf, tot_v, send_sems, recv_sems = refs[n_in + 4 * n_par + 1:]
        gather = _SlotGather(vbuf, send_sems, recv_sems)
        vbuf[gather.my_id] = jnp.zeros((VEC_ROWS, D), F32)
        for r, (_, n) in enumerate(VEC):
            vbuf[gather.my_id, r:r + 1, 0:n] = g_in[r][...]
        vbuf[gather.my_id, LOSS_ROW:LOSS_ROW + 1, 0:128] = loss_in[...]
        gather.start()
        gather.pass_on()
        gather.finish()
        tot_v[...] = gather.total()
        loss_out[...] = tot_v[LOSS_ROW:LOSS_ROW + 1, 0:128]
        r0 = 0
        for q in range(n_par):
            if q < n_vec:
                g = tot_v[q:q + 1, 0:VEC[q][1]]
            else:
                rows = MAT[q - n_vec][1]
                g = tot_m[r0:r0 + rows, :]
                r0 += rows
            w, m, v = (st_in[3 * q + t][...] for t in range(3))
            outs[4 * q][...] = g
            outs[4 * q + 1][...], outs[4 * q + 2][...], outs[4 * q + 3][...] = _adamw_math(w, g, m, v)

    state = [a for wmv in list(vec_state) + list(mat_state) for a in wmv]
    vmem = pl.BlockSpec(memory_space=pltpu.VMEM)
    shapes = [g.shape for g in vec_grads] + [wmv[0].shape for wmv in mat_state]
    out_shape = [jax.ShapeDtypeStruct(s, F32) for s in shapes for _ in range(4)]
    out_shape.append(jax.ShapeDtypeStruct((1, 128), F32))
    res = pl.pallas_call(
        body, name="small_update",
        in_specs=[vmem] * n_in, out_specs=[vmem] * len(out_shape), out_shape=out_shape,
        scratch_shapes=[pltpu.VMEM((N_DEV, VEC_ROWS, D), F32), pltpu.VMEM((VEC_ROWS, D), F32),
                        pltpu.SemaphoreType.DMA((7,)), pltpu.SemaphoreType.DMA((7,))],
        compiler_params=pltpu.CompilerParams(vmem_limit_bytes=VMEM_LIMIT),
    )(*vec_grads, loss_part, mat_total, *state)
    return [res[4 * q:4 * q + 4] for q in range(n_par)], res[-1]


def kernel(x, positions, g_pre, w_in, b_qkv, ln_v_g, ln_v_b, w_spatial, b_spatial, attn_sinks, w_out, g_post, loss_target, m_g_pre, m_w_in, m_b_qkv, m_ln_v_g, m_ln_v_b, m_w_spatial, m_b_spatial, m_attn_sinks, m_w_out, m_g_post, v_g_pre, v_w_in, v_b_qkv, v_ln_v_g, v_ln_v_b, v_w_spatial, v_b_spatial, v_attn_sinks, v_w_out, v_g_post):
    S = x.shape[1]
    c = lax.axis_index("c")
    c_arr = jnp.reshape(c, (1,)).astype(jnp.int32)
    x2 = x[0]
    target = loss_target[0]
    pos = positions.reshape(S, 1)
    half = HD // 2
    inv_freq = ROPE_THETA ** (-jnp.arange(half, dtype=F32) * (2.0 / HD))
    invf = jnp.tile(inv_freq, 128 // half).reshape(1, 128)
    bias = jnp.concatenate([jnp.zeros((OFF_Q,), F32), b_qkv[0], jnp.zeros((D_IN - OFF_ZB,), F32)]).reshape(1, D_IN)
    b_s_col = b_spatial[0].reshape(NG, T, 1)
    sinks = jnp.repeat(attn_sinks[0], T).reshape(NQ * T, 1)

    chip = 2 * lax.axis_index("x") + lax.axis_index("y")
    wt_part = lax.dynamic_slice_in_dim(w_in[0].T.astype(BF16), c * W_IN_ROWS, W_IN_ROWS, axis=0)
    wo_part = lax.dynamic_slice_in_dim(w_out[0].astype(BF16), c * W_OUT_ROWS, W_OUT_ROWS, axis=0)
    sched = jnp.asarray(PROJ_SCHEDULE, jnp.int32)[chip]

    h, proj, wt, cos, sin = _prenorm_inproj(sched, x2, g_pre, bias, wt_part, pos, invf)
    ycat, wo, q_roped = _mid_fwd(proj, cos, sin, ln_v_g, ln_v_b, w_spatial[0], b_s_col, sinks, wo_part)
    dy, dout, loss_part, dg_post = _outproj_loss(ycat, wo, x2, target, g_post)

    dwo = _matmul_tn(ycat, dy, 1024, "dw_out")
    dycat, ro = _dycat(dy, wo, dwo)
    po = _presum(c_arr, dwo, ro, W_OUT_ROWS)
    dproj, dln_g, dln_b, dws, dbs, dsink, dbqkv, bo = _mid_bwd(
        proj, dycat, q_roped, cos, sin, ln_v_g, ln_v_b, w_spatial[0], b_s_col, sinks, po)
    dwt, mat_total = _matmul_tn(dproj, h, 768, "dw_in_t",
                                mat_grads=[dws.reshape(NG * T, T), dbs.reshape(NG, T)])
    pt = _sibling_presum(c_arr, dwt, W_IN_ROWS, "sibling_presum_in")
    grad_x, dg_pre, bt = _dh_prenorm_bwd(dproj, wt, x2, dout, g_pre, pt)
    gt, go = _pair_halves(_sum_chips(c_arr, bt, "sum_chips_in"), _sum_chips(c_arr, bo, "sum_chips_out"))

    g_w_in, d_w_in, nm_w_in, nv_w_in = (a.T for a in _adamw(w_in[0].T, gt, m_w_in[0].T, v_w_in[0].T, "adamw_w_in"))
    g_w_out, d_w_out, nm_w_out, nv_w_out = _adamw(w_out[0], go, m_w_out[0], v_w_out[0], "adamw_w_out")

    state = {"g_pre": (g_pre, m_g_pre, v_g_pre), "g_post": (g_post, m_g_post, v_g_post),
             "b_qkv": (b_qkv, m_b_qkv, v_b_qkv), "ln_v_g": (ln_v_g, m_ln_v_g, v_ln_v_g),
             "ln_v_b": (ln_v_b, m_ln_v_b, v_ln_v_b), "attn_sinks": (attn_sinks, m_attn_sinks, v_attn_sinks),
             "w_spatial": tuple(a.reshape(NG * T, T) for a in (w_spatial, m_w_spatial, v_w_spatial)),
             "b_spatial": tuple(a.reshape(NG, T) for a in (b_spatial, m_b_spatial, v_b_spatial))}
    grads = {"g_pre": dg_pre, "g_post": dg_post, "b_qkv": dbqkv, "ln_v_g": dln_g, "ln_v_b": dln_b,
             "attn_sinks": dsink[:, 0].reshape(1, NQ)}
    results, loss = _small_update([grads[n] for n, _ in VEC], loss_part, mat_total,
                                  [state[n] for n, _ in VEC], [state[n] for n, _ in MAT])
    small = {n: [a.reshape(w.shape) for a in res]
             for (n, _), res, w in zip(VEC + MAT, results, [state[n][0] for n, _ in VEC + MAT])}
    small["w_spatial"] = [a.reshape(w_spatial.shape) for a in small["w_spatial"]]
    small["b_spatial"] = [a.reshape(b_spatial.shape) for a in small["b_spatial"]]
    big = {"w_in": [a[None] for a in (g_w_in, d_w_in, nm_w_in, nv_w_in)],
           "w_out": [a[None] for a in (g_w_out, d_w_out, nm_w_out, nv_w_out)]}
    order = ("g_pre", "w_in", "b_qkv", "ln_v_g", "ln_v_b", "w_spatial", "b_spatial", "attn_sinks", "w_out", "g_post")
    leaves = {**small, **big}
    return (loss[0, 0], grad_x[None], *[leaves[n][t] for t in range(4) for n in order])
```

```python
import jax
import jax.numpy as jnp
from jax import lax
from jax.experimental import pallas as pl
from jax.experimental.pallas import tpu as pltpu

F32 = jnp.float32
BF16 = jnp.bfloat16
MESH = pl.DeviceIdType.MESH

D = 2048
DG = 1024
T = 128
NG = 8
HD = 64
NQ = 16
D_IN = 5376
OFF_U, OFF_V, OFF_ZA, OFF_Q, OFF_K, OFF_VA, OFF_ZB = 0, 1024, 2048, 3072, 4096, 4224, 4352
D_QKV = 1280
EPS = 1e-6
ROPE_THETA = 10000.0
N_CHIPS = 4
N_DEV = 8
W_IN_ROWS = D_IN // N_DEV
W_OUT_ROWS = D // N_DEV

ADAM_LR, ADAM_B1, ADAM_B2, ADAM_EPS, ADAM_WD, ADAM_STEP = 0.001, 0.9, 0.999, 1e-08, 0.01, 10

VMEM_LIMIT = 56 * 1024 * 1024


def _tile(n, pref):
    return pref if n % pref == 0 else n


def _params(sem=None, vmem=VMEM_LIMIT):
    return pltpu.CompilerParams(dimension_semantics=sem, vmem_limit_bytes=vmem)


def _sigmoid(z):
    return 1.0 / (1.0 + jnp.exp(-z))


def _row_mean(v):
    return jnp.mean(v, axis=-1, keepdims=True)


def _dot(a, b, dims):
    return lax.dot_general(a, b, (dims, ((), ())), preferred_element_type=F32)


NN = ((1,), (0,))
NT = ((1,), (1,))
TN = ((0,), (0,))


PROJ_TN = 768
N_PROJ_TILES = D_IN // PROJ_TN
PROJ_SCHEDULE = ((0, 1, 2, 4, 3, 6, 5, 5), (2, 1, 0, 6, 4, 3, 5, 4), (4, 5, 6, 0, 2, 1, 3, 4), (6, 5, 4, 2, 3, 0, 1, 5))


def _place_locally(src_ref, dst_rows_ref, sems, n_chunks):
    ch = src_ref.shape[0] // n_chunks
    return [pltpu.make_async_copy(src_ref.at[pl.ds(q * ch, ch), :], dst_rows_ref.at[pl.ds(q * ch, ch), :],
                                  sems.at[q]) for q in range(n_chunks)]


def _prenorm_inproj(sched, x, g, bias, wt_part, pos, invf):
    S = x.shape[0]
    tp, tm = _tile(S, 512), _tile(S, 1024)
    n_pre, ns = S // tp, S // tm
    n_steps = n_pre + N_PROJ_TILES * ns
    pos_of = lambda i: jnp.maximum(i - n_pre, 0) // ns
    row_of = lambda i: jnp.maximum(i - n_pre, 0) % ns

    def body(sched_ref, x_ref, g_ref, b_ref, wpart_ref, pos_ref, invf_ref, h_ref, proj_ref, wt_ref, cos_ref, sin_ref,
             h_all, w_tile, stage, send_sems, recv_sems, w_sems, local_sems):
        i = pl.program_id(0)
        x_, y_, c = _mesh_pos()
        me, sibling = (x_, y_, c), (x_, y_, 1 - c)
        chips = [(1 - x_, y_), (x_, 1 - y_), (1 - x_, 1 - y_)]

        def block(pos):
            px, py, pc = pos
            return wt_ref.at[pl.ds(pl.multiple_of((4 * px + 2 * py + pc) * W_IN_ROWS, 16), W_IN_ROWS), :]

        def copy(k, blk, to):
            return pltpu.make_async_remote_copy(
                src_ref=wpart_ref if blk is me else block(blk), dst_ref=block(blk),
                send_sem=send_sems.at[k], recv_sem=recv_sems.at[k], device_id=to, device_id_type=MESH)

        stage_in = pltpu.make_async_copy(wpart_ref, stage, local_sems.at[0])
        stage_out = pltpu.make_async_copy(stage, block(me), local_sems.at[1])

        relay = (jnp.where(c == 0, x_, 1 - x_), jnp.where(c == 0, 1 - y_, y_))
        relayed = (jnp.where(c == 0, 1 - x_, x_), jnp.where(c == 0, y_, 1 - y_))

        early = c == y_
        y_send = copy(2, me, (*chips[1], c))

        def own_sends():
            return [copy(0, me, sibling), y_send, copy(1, me, (*chips[0], c))]

        def passed_on():
            return [copy(4, (*chips[0], c), sibling), copy(5, (*chips[1], c), sibling),
                    copy(3, (*relayed, c), (*relay, c))]

        def early_block_arrives():
            @pl.when(early)
            def _():
                copy(2, (*chips[1], c), me).wait_recv()
                copy(5, (*chips[1], c), sibling).start()
                y_send.start()

            pl.when(jnp.logical_not(early))(lambda: copy(5, (*chips[1], 1 - c), me).wait_recv())

        def neighbours_arrive():
            copy(1, (*chips[0], c), me).wait_recv()
            copy(4, (*chips[0], c), sibling).start()

            @pl.when(early)
            def _():
                copy(3, (*relayed, c), (*relay, c)).start()
                copy(5, (*chips[1], 1 - c), me).wait_recv()

            @pl.when(jnp.logical_not(early))
            def _():
                copy(2, (*chips[1], c), me).wait_recv()
                copy(5, (*chips[1], c), sibling).start()
                copy(3, (*relayed, c), (*relay, c)).start()

        def diagonal_arrives():
            copy(3, (*chips[2], c), me).wait_recv()
            copy(6, (*chips[2], c), sibling).start()
            copy(6, (*chips[2], 1 - c), me).wait_recv()

        def tile_load(p):
            slot = p % 2
            rows = wt_ref.at[pl.ds(pl.multiple_of(sched_ref[p] * PROJ_TN, 16), PROJ_TN), :]
            return pltpu.make_async_copy(rows, w_tile.at[slot], w_sems.at[slot])

        def prepare(p):
            p = jnp.asarray(p, jnp.int32)

            @pl.when(p == 0)
            def _():
                copy(0, sibling, me).wait_recv()
                stage_out.wait()

            pl.when(p == 1)(early_block_arrives)
            pl.when(p == 2)(neighbours_arrive)
            pl.when(p == 3)(lambda: copy(4, (*chips[0], 1 - c), me).wait_recv())
            pl.when(p == sched_ref[N_PROJ_TILES])(diagonal_arrives)
            tile_load(p).start()

        @pl.when(i == 0)
        def _():
            stage_in.start()
            copy(0, me, sibling).start()
            copy(1, me, (*chips[0], c)).start()
            pl.when(jnp.logical_not(early))(y_send.start)
            stage_in.wait()
            stage_out.start()

        @pl.when(i < n_pre)
        def _():
            xv = x_ref[...]
            r = lax.rsqrt(_row_mean(xv * xv) + EPS)
            hv = (xv * r * g_ref[...]).astype(BF16)
            h_ref[...] = hv
            h_all[pl.ds(pl.multiple_of(i * tp, tp), tp), :] = hv
            ang = pos_ref[...].astype(F32) * invf_ref[...]
            cos_ref[...] = jnp.cos(ang)
            sin_ref[...] = jnp.sin(ang)

        pl.when(i == n_pre - 1)(lambda: prepare(0))

        @pl.when(i >= n_pre)
        def _():
            p, s = pos_of(i), row_of(i)
            pl.when(s == 0)(lambda: tile_load(p).wait())
            pl.when((s == ns - 1) & (p < N_PROJ_TILES - 1))(lambda: prepare(p + 1))
            hv = h_all[pl.ds(pl.multiple_of(s * tm, tm), tm), :]
            proj_ref[...] = _dot(hv, w_tile[p % 2], NT) + b_ref[...]

        @pl.when(i == n_steps - 1)
        def _():
            for cp in own_sends() + passed_on() + [copy(6, (*chips[2], c), sibling)]:
                cp.wait_send()

    return pl.pallas_call(
        body, name="prenorm_inproj",
        grid_spec=pltpu.PrefetchScalarGridSpec(
            num_scalar_prefetch=1, grid=(n_steps,),
            in_specs=[pl.BlockSpec((tp, D), lambda i, sc: (jnp.minimum(i, n_pre - 1), 0)),
                      pl.BlockSpec((1, D), lambda i, sc: (0, 0)),
                      pl.BlockSpec((1, PROJ_TN), lambda i, sc: (0, sc[pos_of(i)])),
                      HBM_SPEC,
                      pl.BlockSpec((tp, 1), lambda i, sc: (jnp.minimum(i, n_pre - 1), 0)),
                      pl.BlockSpec((1, 128), lambda i, sc: (0, 0))],
            out_specs=[pl.BlockSpec((tp, D), lambda i, sc: (jnp.minimum(i, n_pre - 1), 0)),
                       pl.BlockSpec((tm, PROJ_TN), lambda i, sc: (row_of(i), sc[pos_of(i)])),
                       HBM_SPEC,
                       pl.BlockSpec((tp, 128), lambda i, sc: (jnp.minimum(i, n_pre - 1), 0)),
                       pl.BlockSpec((tp, 128), lambda i, sc: (jnp.minimum(i, n_pre - 1), 0))],
            scratch_shapes=[pltpu.VMEM((S, D), BF16), pltpu.VMEM((2, PROJ_TN, D), BF16),
                            pltpu.VMEM((W_IN_ROWS, D), BF16),
                            pltpu.SemaphoreType.DMA((7,)), pltpu.SemaphoreType.DMA((7,)),
                            pltpu.SemaphoreType.DMA((2,)), pltpu.SemaphoreType.DMA((2,))]),
        out_shape=[jax.ShapeDtypeStruct((S, D), BF16), jax.ShapeDtypeStruct((S, D_IN), F32),
                   jax.ShapeDtypeStruct((D_IN, D), BF16),
                   jax.ShapeDtypeStruct((S, 128), F32), jax.ShapeDtypeStruct((S, 128), F32)],
        compiler_params=_params(("arbitrary",)),
    )(sched, x, g, bias, wt_part, pos, invf)


def _rot_half(xs, first_half):
    return jnp.where(first_half, -pltpu.roll(xs, 96, 1), pltpu.roll(xs, 32, 1))


GH = NQ // 2


MASKED = -1e30


def _attn_consts():
    lane = lax.broadcasted_iota(jnp.int32, (T, 128), 1)
    row = lax.broadcasted_iota(jnp.int32, (GH * T, T), 0) & (T - 1)
    on_diag_or_below = row >= lax.broadcasted_iota(jnp.int32, (GH * T, T), 1)
    return (lane & (HD - 1)) < (HD // 2), lane < HD, on_diag_or_below


def _stack_heads(slab_fn, grp, lo64):
    blocks = []
    for jj in range(GH // 2):
        s = slab_fn(GH // 2 * grp + jj)
        blocks += [jnp.where(lo64, s, 0.0), jnp.where(lo64, 0.0, s)]
    return jnp.concatenate(blocks, axis=0)


def _unstack_heads(stacked, jj, lo64):
    return jnp.where(lo64, stacked[(2 * jj) * T:(2 * jj + 1) * T], stacked[(2 * jj + 1) * T:(2 * jj + 2) * T])


def _both_halves(kv, grp):
    lo = lax.broadcasted_iota(jnp.int32, kv.shape, 1) < HD
    swapped = pltpu.roll(kv, HD, 1)
    return jnp.where(lo, kv, swapped) if grp == 0 else jnp.where(lo, swapped, kv)


def _one_half(acc, grp):
    lo = lax.broadcasted_iota(jnp.int32, acc.shape, 1) < HD
    return jnp.where(lo if grp == 0 else jnp.logical_not(lo), acc + pltpu.roll(acc, HD, 1), 0.0)


def _layer_norm_stats(v):
    mu = _row_mean(v)
    xc = v - mu
    var = _row_mean(xc * xc)
    rs = lax.rsqrt(var + EPS)
    return xc * rs, rs


def _fold(both, own):
    return jnp.where(own, both[:, T:2 * T], both[:, 0:T])


def _unfold(p, own):
    return jnp.concatenate([jnp.where(own, 0.0, p), jnp.where(own, p, 0.0)], axis=1).astype(BF16)


def _band_softmax(q_scaled, k_both, own, has_prev, sink):
    s_both = _dot(q_scaled, k_both, NT)
    s_prev = s_both[:, 0:T]
    if has_prev is not None:
        s_prev = s_prev + jnp.where(has_prev, 0.0, MASKED)
    s = jnp.where(own, s_both[:, T:2 * T], s_prev)
    m = jnp.maximum(jnp.max(s, axis=-1, keepdims=True), sink)
    e = jnp.exp(s - m)
    es = jnp.exp(sink - m)
    inv = 1.0 / (jnp.sum(e, axis=-1, keepdims=True) + es)
    return e * inv, es * inv


BPS_FWD = 4
BPS_BWD = 4


def _mid_specs(nt, rev, bps):
    tile = (lambda i: nt - 1 - i) if rev else (lambda i: i)
    prev = lambda i: jnp.maximum(bps * tile(i) - 1, 0)
    rows = bps * T
    return tile, [
        pl.BlockSpec((rows, D_IN), lambda i: (tile(i), 0)),
        pl.BlockSpec((T, 2 * T), lambda i: (prev(i), OFF_K // (2 * T))),
    ], [
        pl.BlockSpec((rows, 128), lambda i: (tile(i), 0)),
        pl.BlockSpec((rows, 128), lambda i: (tile(i), 0)),
        pl.BlockSpec((T, 128), lambda i: (prev(i), 0)),
        pl.BlockSpec((T, 128), lambda i: (prev(i), 0)),
        pl.BlockSpec((1, DG), lambda i: (0, 0)),
        pl.BlockSpec((1, DG), lambda i: (0, 0)),
        pl.BlockSpec((NG, T, T), lambda i: (0, 0, 0)),
        pl.BlockSpec((NG, T, 1), lambda i: (0, 0, 0)),
        pl.BlockSpec((NQ * T, 1), lambda i: (0, 0)),
    ]


def _rope(xs, cosv, sinv, first_half):
    return xs * cosv + _rot_half(xs, first_half) * sinv


def _mid_fwd(proj, cos, sin, ln_g, ln_b, w_s, b_s, sinks, wo_part):
    S = proj.shape[0]
    BPS = BPS_FWD if S % (BPS_FWD * T) == 0 else 1
    nt = S // (BPS * T)
    tile, proj_specs, par_specs = _mid_specs(nt, False, BPS)

    def body(p_ref, kvp_ref, cos_ref, sin_ref, cosp_ref, sinp_ref, lng_ref, lnb_ref, ws_ref, bs_ref, sink_ref,
             wpart_ref, y_ref, wo_ref, q_ref, wm_ref, send_sems, recv_sems, local_sems):
        i = pl.program_id(0)
        first_half, lo64, own = _attn_consts()
        gather = _RowGather(wpart_ref, wo_ref, W_OUT_ROWS, send_sems, recv_sems, local_sems)
        pl.when(i == 0)(gather.start)
        pl.when(i == nt // 2)(gather.pass_on)

        @pl.when(i == 0)
        def _():
            tril = lax.broadcasted_iota(jnp.int32, (T, T), 0) >= lax.broadcasted_iota(jnp.int32, (T, T), 1)
            for g in range(NG):
                wm_ref[g] = jnp.where(tril, ws_ref[g], 0.0).astype(BF16)

        def block(b, k_prev, v_prev, has_prev):
            rows = slice(b * T, (b + 1) * T)
            xhat, _ = _layer_norm_stats(p_ref[rows, OFF_V:OFF_V + DG])
            vn = xhat * lng_ref[...] + lnb_ref[...]
            for g in range(NG):
                sl = slice(128 * g, 128 * g + 128)
                mixed = _dot(wm_ref[g], vn[:, sl].astype(BF16), NN) + bs_ref[g]
                z = p_ref[rows, OFF_ZA + 128 * g:OFF_ZA + 128 * g + 128]
                u = p_ref[rows, OFF_U + 128 * g:OFF_U + 128 * g + 128]
                y_ref[rows, sl] = (u * mixed * (z * _sigmoid(z))).astype(BF16)

            cosv, sinv = cos_ref[rows, :], sin_ref[rows, :]
            k_cur = _rope(p_ref[rows, OFF_K:OFF_K + 128], cosv, sinv, first_half)
            v_cur = p_ref[rows, OFF_VA:OFF_VA + 128]

            def q_slab(j):
                q = (_rope(p_ref[rows, OFF_Q + 128 * j:OFF_Q + 128 * j + 128], cosv, sinv, first_half)
                     * (HD ** -0.5)).astype(BF16)
                q_ref[rows, 128 * j:128 * j + 128] = q
                return q

            k_both = jnp.concatenate([k_prev, k_cur], axis=0)
            v_both = jnp.concatenate([v_prev, v_cur], axis=0)
            for grp in range(2):
                qs = _stack_heads(q_slab, grp, lo64)
                kg, vg = _both_halves(k_both, grp).astype(BF16), _both_halves(v_both, grp).astype(BF16)
                p, _ = _band_softmax(qs, kg, own, has_prev, sink_ref[GH * T * grp:GH * T * (grp + 1), :])
                o_st = _dot(_unfold(p, own), vg, NN)
                for jj in range(GH // 2):
                    c0 = 128 * (GH // 2 * grp + jj)
                    zb = p_ref[rows, OFF_ZB + c0:OFF_ZB + c0 + 128]
                    o = _unstack_heads(o_st, jj, lo64)
                    y_ref[rows, DG + c0:DG + c0 + 128] = (o * (zb * _sigmoid(zb))).astype(BF16)
            return k_cur, v_cur

        k_prev = _rope(kvp_ref[:, 0:128], cosp_ref[...], sinp_ref[...], first_half)
        kv = block(0, k_prev, kvp_ref[:, 128:256], i > 0)
        for b in range(1, BPS):
            kv = block(b, *kv, None)
        pl.when(i == nt - 1)(gather.finish)

    return pl.pallas_call(
        body, name="mid_fwd", grid=(nt,),
        in_specs=proj_specs + par_specs + [HBM_SPEC],
        out_specs=[pl.BlockSpec((BPS * T, D), lambda i: (tile(i), 0)), HBM_SPEC,
                   pl.BlockSpec((BPS * T, NQ * HD), lambda i: (tile(i), 0))],
        out_shape=[jax.ShapeDtypeStruct((S, D), BF16), jax.ShapeDtypeStruct((D, D), BF16),
                   jax.ShapeDtypeStruct((S, NQ * HD), BF16)],
        scratch_shapes=[pltpu.VMEM((NG, T, T), BF16)] + GATHER_SEMS,
        compiler_params=_params(("arbitrary",)),
    )(proj, proj, cos, sin, cos, sin, ln_g, ln_b, w_s, b_s, sinks, wo_part)


def _mid_bwd(proj, dycat, q_roped, cos, sin, ln_g, ln_b, w_s, b_s, sinks, po):
    S = proj.shape[0]
    BPS = BPS_BWD if S % (BPS_BWD * T) == 0 else 1
    nt = S // (BPS * T)
    tile, proj_specs, par_specs = _mid_specs(nt, True, BPS)
    const2 = lambda i: (0, 0)

    def body(p_ref, kvp_ref, dyc_ref, q_ref, cos_ref, sin_ref, cosp_ref, sinp_ref, lng_ref, lnb_ref, ws_ref, bs_ref,
             sink_ref, po_ref, dp_ref, dlng_ref, dlnb_ref, dws_ref, dbs_ref, dsink_ref, dbqkv_ref, bo_ref,
             carry_ref, dvn_ref, wm_ref, wmt_ref, send_sems, recv_sems, local_sem):
        i = pl.program_id(0)
        first_half, lo64, own = _attn_consts()
        tril = lax.broadcasted_iota(jnp.int32, (T, T), 0) >= lax.broadcasted_iota(jnp.int32, (T, T), 1)
        exchange = _OwnerExchange(po_ref, bo_ref, send_sems, recv_sems, local_sem)
        pl.when(i == 0)(exchange.start)

        @pl.when(i == 0)
        def _():
            for g in range(NG):
                wm = jnp.where(tril, ws_ref[g], 0.0)
                wm_ref[g] = wm.astype(BF16)
                wmt_ref[g] = wm.T.astype(BF16)
            dlng_ref[...] = jnp.zeros_like(dlng_ref)
            dlnb_ref[...] = jnp.zeros_like(dlnb_ref)
            dws_ref[...] = jnp.zeros_like(dws_ref)
            dbs_ref[...] = jnp.zeros_like(dbs_ref)
            dsink_ref[...] = jnp.zeros_like(dsink_ref)
            dbqkv_ref[...] = jnp.zeros_like(dbqkv_ref)
            carry_ref[...] = jnp.zeros_like(carry_ref)

        def roped_k(b):
            rows = slice(b * T, (b + 1) * T)
            return _rope(p_ref[rows, OFF_K:OFF_K + 128], cos_ref[rows, :], sin_ref[rows, :], first_half)

        def block(b, k_prev, v_prev, has_prev, dk_next, dv_next):
            rows = slice(b * T, (b + 1) * T)
            xhat, rs = _layer_norm_stats(p_ref[rows, OFF_V:OFF_V + DG])
            lng = lng_ref[...]
            vn = xhat * lng + lnb_ref[...]
            for g in range(NG):
                sl = slice(128 * g, 128 * g + 128)
                vng = vn[:, sl].astype(BF16)
                mixed = _dot(wm_ref[g], vng, NN) + bs_ref[g]
                z = p_ref[rows, OFF_ZA + 128 * g:OFF_ZA + 128 * g + 128]
                u = p_ref[rows, OFF_U + 128 * g:OFF_U + 128 * g + 128]
                dy = dyc_ref[rows, sl]
                sg = _sigmoid(z)
                sa = z * sg
                dp_ref[rows, OFF_U + 128 * g:OFF_U + 128 * g + 128] = (dy * mixed * sa).astype(BF16)
                dp_ref[rows, OFF_ZA + 128 * g:OFF_ZA + 128 * g + 128] = (
                    dy * u * mixed * (sg * (1.0 + z * (1.0 - sg)))).astype(BF16)
                dm = dy * u * sa
                dmb = dm.astype(BF16)
                dvn_ref[rows, sl] = _dot(wmt_ref[g], dmb, NN)
                dws_ref[g] += jnp.where(tril, _dot(dmb, vng, NT), 0.0)
                dbs_ref[g] += jnp.sum(dm, axis=1, keepdims=True)
            dvn = dvn_ref[rows, :]
            dlng_ref[...] += jnp.sum(dvn * xhat, axis=0, keepdims=True)
            dlnb_ref[...] += jnp.sum(dvn, axis=0, keepdims=True)
            dxh = dvn * lng
            dv_g = rs * (dxh - _row_mean(dxh)
                         - xhat * _row_mean(dxh * xhat))
            dp_ref[rows, OFF_V:OFF_V + DG] = dv_g.astype(BF16)

            cosv, sinv = cos_ref[rows, :], sin_ref[rows, :]
            k_cur = roped_k(b)
            v_cur = p_ref[rows, OFF_VA:OFF_VA + 128]

            def q_slab(j):
                return q_ref[rows, 128 * j:128 * j + 128]

            def do_slab(j):
                zb = p_ref[rows, OFF_ZB + 128 * j:OFF_ZB + 128 * j + 128]
                return dyc_ref[rows, DG + 128 * j:DG + 128 * j + 128] * (zb * _sigmoid(zb))

            k_both = jnp.concatenate([k_prev, k_cur], axis=0)
            v_both = jnp.concatenate([v_prev, v_cur], axis=0)
            dk_both, dv_both = jnp.zeros((2 * T, 128), F32), jnp.zeros((2 * T, 128), F32)
            for grp in range(2):
                qs = _stack_heads(q_slab, grp, lo64)
                d_o = _stack_heads(do_slab, grp, lo64)
                dob = d_o.astype(BF16)
                kg, vg = _both_halves(k_both, grp).astype(BF16), _both_halves(v_both, grp).astype(BF16)
                p, ps = _band_softmax(qs, kg, own, has_prev, sink_ref[GH * T * grp:GH * T * (grp + 1), :])
                p_both = _unfold(p, own)
                o_st = _dot(p_both, vg, NN)
                delta = jnp.sum(d_o * o_st, axis=-1, keepdims=True)
                ds_both = _unfold(p * (_fold(_dot(dob, vg, NT), own) - delta), own)
                dq_st = _dot(ds_both, kg, NN) * (HD ** -0.5)
                dk_both = dk_both + _one_half(_dot(ds_both, qs, TN), grp)
                dv_both = dv_both + _one_half(_dot(p_both, dob, TN), grp)
                dsink_rows = ps * delta
                for hh in range(GH):
                    h = GH * grp + hh
                    dsink_ref[h:h + 1, :] += jnp.broadcast_to(
                        -jnp.sum(dsink_rows[hh * T:(hh + 1) * T], axis=0, keepdims=True), (1, 128))
                for jj in range(GH // 2):
                    c0 = 128 * (GH // 2 * grp + jj)
                    zb = p_ref[rows, OFF_ZB + c0:OFF_ZB + c0 + 128]
                    sg = _sigmoid(zb)
                    o = _unstack_heads(o_st, jj, lo64)
                    dp_ref[rows, OFF_ZB + c0:OFF_ZB + c0 + 128] = (
                        dyc_ref[rows, DG + c0:DG + c0 + 128] * o * (sg * (1.0 + zb * (1.0 - sg)))).astype(BF16)
                    dq = _unstack_heads(dq_st, jj, lo64)
                    dq_pre = dq * cosv - _rot_half(dq, first_half) * sinv
                    dp_ref[rows, OFF_Q + c0:OFF_Q + c0 + 128] = dq_pre.astype(BF16)
                    dbqkv_ref[:, c0:c0 + 128] += jnp.sum(dq_pre, axis=0, keepdims=True)
            dk_prev, dv_prev = dk_both[0:T], dv_both[0:T]
            dk_cur, dv_cur = dk_both[T:2 * T] + dk_next, dv_both[T:2 * T] + dv_next
            dk_pre = dk_cur * cosv - _rot_half(dk_cur, first_half) * sinv
            dp_ref[rows, OFF_K:OFF_K + 128] = dk_pre.astype(BF16)
            dp_ref[rows, OFF_VA:OFF_VA + 128] = dv_cur.astype(BF16)
            dbqkv_ref[:, 1024:1152] += jnp.sum(dk_pre, axis=0, keepdims=True)
            dbqkv_ref[:, 1152:1280] += jnp.sum(dv_cur, axis=0, keepdims=True)
            return dk_prev, dv_prev

        grads = carry_ref[:, 0:128], carry_ref[:, 128:256]
        for b in range(BPS - 1, 0, -1):
            prows = slice((b - 1) * T, b * T)
            grads = block(b, roped_k(b - 1), p_ref[prows, OFF_VA:OFF_VA + 128], None, *grads)
        k_prev = _rope(kvp_ref[:, 0:128], cosp_ref[...], sinp_ref[...], first_half)
        grads = block(0, k_prev, kvp_ref[:, 128:256], i < nt - 1, *grads)
        carry_ref[:, 0:128], carry_ref[:, 128:256] = grads
        pl.when(i == nt - 1)(exchange.finish)

    return pl.pallas_call(
        body, name="mid_bwd", grid=(nt,),
        in_specs=proj_specs + [pl.BlockSpec((BPS * T, D), lambda i: (tile(i), 0)),
                               pl.BlockSpec((BPS * T, NQ * HD), lambda i: (tile(i), 0))] + par_specs + [HBM_SPEC],
        out_specs=[pl.BlockSpec((BPS * T, D_IN), lambda i: (tile(i), 0)),
                   pl.BlockSpec((1, DG), const2), pl.BlockSpec((1, DG), const2),
                   pl.BlockSpec((NG, T, T), lambda i: (0, 0, 0)), pl.BlockSpec((NG, T, 1), lambda i: (0, 0, 0)),
                   pl.BlockSpec((NQ, 128), const2), pl.BlockSpec((1, D_QKV), const2), HBM_SPEC],
        out_shape=[jax.ShapeDtypeStruct((S, D_IN), BF16),
                   jax.ShapeDtypeStruct((1, DG), F32), jax.ShapeDtypeStruct((1, DG), F32),
                   jax.ShapeDtypeStruct((NG, T, T), F32), jax.ShapeDtypeStruct((NG, T, 1), F32),
                   jax.ShapeDtypeStruct((NQ, 128), F32), jax.ShapeDtypeStruct((1, D_QKV), F32),
                   jax.ShapeDtypeStruct(po.shape, BF16)],
        scratch_shapes=[pltpu.VMEM((T, 2 * T), F32), pltpu.VMEM((BPS * T, DG), F32),
                        pltpu.VMEM((NG, T, T), BF16), pltpu.VMEM((NG, T, T), BF16)] + OWNER_SEMS,
        compiler_params=_params(("arbitrary",)),
    )(proj, proj, dycat, q_roped, cos, sin, cos, sin, ln_g, ln_b, w_s, b_s, sinks, po)


def _outproj_loss(ycat, wo, x, target, g_post):
    S = ycat.shape[0]
    tm = _tile(S, 512)
    nt = S // tm
    n_part = 2 if tm % 32 == 0 else 1
    tp = tm // n_part
    const2 = lambda i: (0, 0)

    def body(yc_ref, w_ref, x_ref, t_ref, g_ref, dy_ref, dout_ref, loss_ref, dg_ref, lacc_ref):
        i = pl.program_id(0)

        @pl.when(i == 0)
        def _():
            dg_ref[...] = jnp.zeros_like(dg_ref)
            lacc_ref[...] = jnp.zeros_like(lacc_ref)

        g = g_ref[...]
        ys = [_dot(yc_ref[q * tp:(q + 1) * tp, :], w_ref[...], NN) for q in range(n_part)]
        for q, y in enumerate(ys):
            rows = slice(q * tp, (q + 1) * tp)
            r = lax.rsqrt(_row_mean(y * y) + EPS)
            yh = y * r
            diff = x_ref[rows, :] + yh * g - t_ref[rows, :]
            lacc_ref[...] += jnp.sum(diff * diff, axis=0, keepdims=True)
            dout = diff * (1.0 / D)
            dout_ref[rows, :] = dout
            dg_ref[...] += jnp.sum(dout * yh, axis=0, keepdims=True)
            dyh = dout * g
            dy_ref[rows, :] = (r * (dyh - yh * _row_mean(dyh * yh))).astype(BF16)

        @pl.when(i == nt - 1)
        def _():
            loss_ref[...] = jnp.broadcast_to(jnp.sum(lacc_ref[...], axis=1, keepdims=True) * (0.5 / D), (1, 128))

    row = lambda i: (i, 0)
    return pl.pallas_call(
        body, name="outproj_loss", grid=(nt,),
        in_specs=[pl.BlockSpec((tm, D), row), pl.BlockSpec((D, D), const2, pipeline_mode=pl.Buffered(1)),
                  pl.BlockSpec((tm, D), row), pl.BlockSpec((tm, D), row), pl.BlockSpec((1, D), const2)],
        out_specs=[pl.BlockSpec((tm, D), row), pl.BlockSpec((tm, D), row), pl.BlockSpec((1, 128), const2),
                   pl.BlockSpec((1, D), const2)],
        out_shape=[jax.ShapeDtypeStruct((S, D), BF16), jax.ShapeDtypeStruct((S, D), F32),
                   jax.ShapeDtypeStruct((1, 128), F32), jax.ShapeDtypeStruct((1, D), F32)],
        scratch_shapes=[pltpu.VMEM((1, D), F32)],
        compiler_params=_params(("arbitrary",)),
    )(ycat, wo, x, target, g_post)


def _dycat(dy, wo, dwo):
    S = dy.shape[0]
    tm = _tile(S, 1024)
    nt = S // tm

    def body(dy_ref, w_ref, dwo_ref, o_ref, r_ref, send_sems, recv_sems):
        i = pl.program_id(0)
        copies = _sibling_copies(dwo_ref, r_ref, W_OUT_ROWS, send_sems, recv_sems)

        @pl.when(i == 0)
        def _():
            for cp in copies:
                cp.start()

        o_ref[...] = _dot(dy_ref[...], w_ref[...], NT)

        @pl.when(i == nt - 1)
        def _():
            for cp in copies:
                cp.wait_recv()
            for cp in copies:
                cp.wait_send()

    return pl.pallas_call(
        body, name="dycat", grid=(nt,),
        in_specs=[pl.BlockSpec((tm, D), lambda i: (i, 0)),
                  pl.BlockSpec((D, D), lambda i: (0, 0), pipeline_mode=pl.Buffered(1)), HBM_SPEC],
        out_specs=[pl.BlockSpec((tm, D), lambda i: (i, 0)), HBM_SPEC],
        out_shape=[jax.ShapeDtypeStruct((S, D), F32), jax.ShapeDtypeStruct((N_CHIPS, W_OUT_ROWS, D), BF16)],
        scratch_shapes=[pltpu.SemaphoreType.DMA((N_CHIPS,)), pltpu.SemaphoreType.DMA((N_CHIPS,))],
        compiler_params=_params(("arbitrary",)),
    )(dy, wo, dwo)


def _matmul_tn(a, b, tm, name, mat_grads=None):
    K, M = a.shape
    N = b.shape[1]
    tk = _tile(K, 2048)
    ni, nk = M // tm, K // tk
    hosting = mat_grads is not None

    def body(a_ref, b_ref, *rest):
        if hosting:
            g_in, (o_ref, tot_ref, acc_ref, mbuf, send_sems, recv_sems) = rest[:len(MAT)], rest[len(MAT):]
        else:
            o_ref, acc_ref = rest
        i, k = pl.program_id(0), pl.program_id(1)

        if hosting:
            gather = _SlotGather(mbuf, send_sems, recv_sems)

            @pl.when((i == 0) & (k == 0))
            def _():
                r0 = 0
                for q, (_, rows) in enumerate(MAT):
                    mbuf[gather.my_id, r0:r0 + rows, :] = g_in[q][...]
                    r0 += rows
                gather.start()

            pl.when((i == ni // 2) & (k == 0))(gather.pass_on)

        @pl.when(k == 0)
        def _():
            acc_ref[...] = jnp.zeros_like(acc_ref)

        acc_ref[...] += _dot(a_ref[...], b_ref[...], TN)

        @pl.when(k == nk - 1)
        def _():
            o_ref[...] = acc_ref[...].astype(BF16)

        if hosting:
            @pl.when((i == ni - 1) & (k == nk - 1))
            def _():
                gather.finish()
                tot_ref[...] = gather.total()

    whole = pl.BlockSpec(memory_space=pltpu.VMEM)
    in_specs = [pl.BlockSpec((tk, tm), lambda i, k: (k, i)), pl.BlockSpec((tk, N), lambda i, k: (k, 0))]
    out_specs = [pl.BlockSpec((tm, N), lambda i, k: (i, 0))]
    out_shape = [jax.ShapeDtypeStruct((M, N), BF16)]
    scratch = [pltpu.VMEM((tm, N), F32)]
    if hosting:
        in_specs += [whole] * len(MAT)
        out_specs.append(whole)
        out_shape.append(jax.ShapeDtypeStruct((MAT_ROWS, 128), F32))
        scratch += [pltpu.VMEM((N_DEV, MAT_ROWS, 128), F32), pltpu.SemaphoreType.DMA((7,)),
                    pltpu.SemaphoreType.DMA((7,))]
    res = pl.pallas_call(
        body, name=name, grid=(ni, nk), in_specs=in_specs, out_specs=out_specs, out_shape=out_shape,
        scratch_shapes=scratch,
        compiler_params=_params(("arbitrary", "arbitrary") if hosting else ("parallel", "arbitrary")),
    )(a, b, *(mat_grads or ()))
    return res if hosting else res[0]


def _dh_prenorm_bwd(dproj, wt, x, dout, g_pre, pt):
    S = x.shape[0]
    tm, tk = _tile(S, 1024), 768
    ne = 4
    te = tm // ne
    ni, nk = S // tm, D_IN // tk
    n_mm = ni * nk
    n_steps = n_mm + ne
    mm_of = lambda t: jnp.minimum(t, n_mm - 1)

    def quarter_of(t):
        j, q = t // nk - 1, t % nk
        nxt = jnp.where(q < ne, j * ne + q, (j + 1) * ne)
        return jnp.clip(nxt, 0, ni * ne - 1)

    def body(dp_ref, w_ref, x_ref, dout_ref, g_ref, pt_ref, gx_ref, dg_ref, bt_ref, acc_ref,
             send_sems, recv_sems, local_sem):
        t = pl.program_id(0)
        exchange = _OwnerExchange(pt_ref, bt_ref, send_sems, recv_sems, local_sem)
        pl.when(t == 0)(exchange.start)

        @pl.when(t == 0)
        def _():
            dg_ref[...] = jnp.zeros_like(dg_ref)

        has_mm = t < n_mm
        has_q = (t >= nk) & (t % nk < ne)

        def matmul():
            slot = (mm_of(t) // nk) % 2
            old = jnp.where(mm_of(t) % nk > 0, acc_ref[slot], 0.0)
            acc_ref[slot] = old + _dot(dp_ref[...], w_ref[...], NN)

        def quarter():
            slot = (t // nk - 1) % 2
            dh = acc_ref[slot, pl.ds(pl.multiple_of((t % nk) * te, te), te), :]
            xv = x_ref[...]
            r = lax.rsqrt(_row_mean(xv * xv) + EPS)
            xh = xv * r
            dg_ref[...] += jnp.sum(dh * xh, axis=0, keepdims=True)
            dxh = dh * g_ref[...]
            gx_ref[...] = dout_ref[...] + r * (dxh - xh * _row_mean(dxh * xh))

        @pl.when(has_mm & has_q)
        def _():
            matmul()
            quarter()

        pl.when(has_mm & jnp.logical_not(has_q))(matmul)
        pl.when(jnp.logical_not(has_mm) & has_q)(quarter)
        pl.when(t == n_steps - 1)(exchange.finish)

    quarter_block = lambda t: (quarter_of(t), 0)
    return pl.pallas_call(
        body, name="dh_prenorm_bwd", grid=(n_steps,),
        in_specs=[pl.BlockSpec((tm, tk), lambda t: (mm_of(t) // nk, mm_of(t) % nk)),
                  pl.BlockSpec((tk, D), lambda t: (mm_of(t) % nk, 0)),
                  pl.BlockSpec((te, D), quarter_block), pl.BlockSpec((te, D), quarter_block),
                  pl.BlockSpec((1, D), lambda t: (0, 0)), HBM_SPEC],
        out_specs=[pl.BlockSpec((te, D), quarter_block), pl.BlockSpec((1, D), lambda t: (0, 0)), HBM_SPEC],
        out_shape=[jax.ShapeDtypeStruct((S, D), F32), jax.ShapeDtypeStruct((1, D), F32),
                   jax.ShapeDtypeStruct(pt.shape, BF16)],
        scratch_shapes=[pltpu.VMEM((2, tm, D), F32)] + OWNER_SEMS,
        compiler_params=_params(("arbitrary",)),
    )(dproj, wt, x, dout, g_pre, pt)


def _presum(c_arr, own, recv, half_rows):
    n_cols = own.shape[-1]
    own4 = own.reshape(N_CHIPS, 2, half_rows, n_cols)

    def body(c_ref, own_ref, recv_ref, o_ref):
        o_ref[...] = (own_ref[...].astype(F32) + recv_ref[...].astype(F32)).astype(BF16)

    return pl.pallas_call(
        body, name="presum_%d" % half_rows,
        grid_spec=pltpu.PrefetchScalarGridSpec(
            num_scalar_prefetch=1, grid=(N_CHIPS,),
            in_specs=[pl.BlockSpec((None, None, half_rows, n_cols), lambda j, c: (j, c[0], 0, 0)),
                      pl.BlockSpec((None, half_rows, n_cols), lambda j, c: (j, 0, 0))],
            out_specs=pl.BlockSpec((None, half_rows, n_cols), lambda j, c: (j, 0, 0))),
        out_shape=jax.ShapeDtypeStruct((N_CHIPS, half_rows, n_cols), BF16),
        compiler_params=_params(("parallel",)),
    )(c_arr, own4, recv)


def _sibling_presum(c_arr, dw, half_rows, name):
    n_cols = dw.shape[-1]
    dw4 = dw.reshape(N_CHIPS, 2, half_rows, n_cols)

    def body(c_ref, own_ref, dw_ref, o_ref, r_ref, stage, send_sems, recv_sems, load_sem):
        j = pl.program_id(0)
        copies = _sibling_copies(dw_ref, r_ref, half_rows, send_sems, recv_sems)

        @pl.when(j == 0)
        def _():
            for cp in copies:
                cp.start()

        for q in range(N_CHIPS):
            pl.when(j == q)(copies[q].wait_recv)
        load = pltpu.make_async_copy(r_ref.at[j], stage, load_sem)
        load.start()
        load.wait()
        o_ref[...] = (own_ref[...].astype(F32) + stage[...].astype(F32)).astype(BF16)

        @pl.when(j == N_CHIPS - 1)
        def _():
            for cp in copies:
                cp.wait_send()

    out, _ = pl.pallas_call(
        body, name=name,
        grid_spec=pltpu.PrefetchScalarGridSpec(
            num_scalar_prefetch=1, grid=(N_CHIPS,),
            in_specs=[pl.BlockSpec((None, None, half_rows, n_cols), lambda j, c: (j, c[0], 0, 0)), HBM_SPEC],
            out_specs=[pl.BlockSpec((None, half_rows, n_cols), lambda j, c: (j, 0, 0)), HBM_SPEC],
            scratch_shapes=[pltpu.VMEM((half_rows, n_cols), BF16), pltpu.SemaphoreType.DMA((N_CHIPS,)),
                            pltpu.SemaphoreType.DMA((N_CHIPS,)), pltpu.SemaphoreType.DMA(())]),
        out_shape=[jax.ShapeDtypeStruct((N_CHIPS, half_rows, n_cols), BF16),
                   jax.ShapeDtypeStruct((N_CHIPS, half_rows, n_cols), BF16)],
        compiler_params=_params(("arbitrary",)),
    )(c_arr, dw4, dw)
    return out


def _sum_chips(c_arr, parts, name):
    _, rows, n_cols = parts.shape
    nt = 2
    tr = rows // nt

    def body(c_ref, p_ref, o_ref):
        o_ref[...] = ((p_ref[0].astype(F32) + p_ref[1].astype(F32)) + p_ref[2].astype(F32)) + p_ref[3].astype(F32)

    return pl.pallas_call(
        body, name=name,
        grid_spec=pltpu.PrefetchScalarGridSpec(
            num_scalar_prefetch=1, grid=(nt,),
            in_specs=[pl.BlockSpec((N_CHIPS, tr, n_cols), lambda i, c: (0, i, 0))],
            out_specs=pl.BlockSpec((tr, n_cols), lambda i, c: (c[0] * nt + i, 0))),
        out_shape=jax.ShapeDtypeStruct((2 * rows, n_cols), F32),
        compiler_params=_params(("parallel",)),
    )(c_arr, parts)


def _adamw_math(w, g, m, v):
    mn = ADAM_B1 * m + (1.0 - ADAM_B1) * g
    vn = ADAM_B2 * v + (1.0 - ADAM_B2) * (g * g)
    m_hat = mn / (1.0 - ADAM_B1 ** ADAM_STEP)
    v_hat = vn / (1.0 - ADAM_B2 ** ADAM_STEP)
    return -ADAM_LR * (m_hat / (jnp.sqrt(v_hat) + ADAM_EPS) + ADAM_WD * w), mn, vn


def _adamw(w, g, m, v, name):
    R, C = w.shape
    tr = next((t for t in (256, 192, 128) if R % t == 0), R)

    def body(w_ref, g_ref, m_ref, v_ref, go_ref, d_ref, mo_ref, vo_ref):
        gv = g_ref[...]
        go_ref[...] = gv
        d_ref[...], mo_ref[...], vo_ref[...] = _adamw_math(w_ref[...], gv, m_ref[...], v_ref[...])

    spec = pl.BlockSpec((tr, C), lambda i: (i, 0))
    shp = jax.ShapeDtypeStruct((R, C), F32)
    return pl.pallas_call(
        body, name=name, grid=(R // tr,), in_specs=[spec] * 4, out_specs=[spec] * 4, out_shape=[shp] * 4,
        compiler_params=_params(("parallel",)),
    )(w, g, m, v)


HBM_SPEC = pl.BlockSpec(memory_space=pltpu.HBM)
GATHER_LOCAL_CHUNKS = 4
GATHER_SEMS = [pltpu.SemaphoreType.DMA((7,)), pltpu.SemaphoreType.DMA((7,)),
               pltpu.SemaphoreType.DMA((GATHER_LOCAL_CHUNKS,))]
OWNER_SEMS = [pltpu.SemaphoreType.DMA((3,)), pltpu.SemaphoreType.DMA((3,)), pltpu.SemaphoreType.DMA(())]


def _mesh_pos():
    return lax.axis_index("x"), lax.axis_index("y"), lax.axis_index("c")


class _RowGather:
    def __init__(self, src_ref, full_ref, rows, send_sems, recv_sems, local_sems):
        self.src, self.full, self.rows = src_ref, full_ref, rows
        self.send, self.recv, self.local = send_sems, recv_sems, local_sems
        x, y, c = _mesh_pos()
        self.c, self.me, self.sibling = c, (x, y, c), (x, y, 1 - c)
        self.chips = [(1 - x, y), (x, 1 - y), (1 - x, 1 - y)]

    def _block(self, pos):
        px, py, pc = pos
        return self.full.at[pl.ds(pl.multiple_of((4 * px + 2 * py + pc) * self.rows, 16), self.rows), :]

    def _copy(self, k, blk, to):
        return pltpu.make_async_remote_copy(
            src_ref=self.src if blk is self.me else self._block(blk), dst_ref=self._block(blk),
            send_sem=self.send.at[k], recv_sem=self.recv.at[k], device_id=to, device_id_type=MESH)

    def _mine(self):
        return _place_locally(self.src, self._block(self.me), self.local, GATHER_LOCAL_CHUNKS)

    def _first(self):
        return [self._copy(0, self.me, self.sibling)] + [
            self._copy(1 + j, self.me, (*chip, self.c)) for j, chip in enumerate(self.chips)]

    def start(self):
        for cp in self._first() + self._mine():
            cp.start()

    def _passed(self):
        return [self._copy(4 + j, (*chip, self.c), self.sibling) for j, chip in enumerate(self.chips)]

    def pass_on(self):
        for j, chip in enumerate(self.chips):
            self._copy(1 + j, (*chip, self.c), self.me).wait_recv()
            self._passed()[j].start()

    def finish(self):
        self._copy(0, self.sibling, self.me).wait_recv()
        for j, chip in enumerate(self.chips):
            self._copy(4 + j, (*chip, 1 - self.c), self.me).wait_recv()
        for cp in self._first() + self._passed():
            cp.wait_send()
        for cp in self._mine():
            cp.wait()


class _OwnerExchange:
    def __init__(self, src_ref, dst_ref, send_sems, recv_sems, local_sem):
        self.src, self.dst, self.send, self.recv, self.local = src_ref, dst_ref, send_sems, recv_sems, local_sem
        x, y, c = _mesh_pos()
        self.c, self.my_chip = c, 2 * x + y
        self.peers = [(1 - x, y), (x, 1 - y), (1 - x, 1 - y)]

    def _copies(self):
        local = pltpu.make_async_copy(self.src.at[self.my_chip], self.dst.at[self.my_chip], self.local)
        remote = [pltpu.make_async_remote_copy(
            src_ref=self.src.at[2 * px + py], dst_ref=self.dst.at[self.my_chip],
            send_sem=self.send.at[k], recv_sem=self.recv.at[k], device_id=(px, py, self.c), device_id_type=MESH)
            for k, (px, py) in enumerate(self.peers)]
        return local, remote

    def start(self):
        local, remote = self._copies()
        local.start()
        for cp in remote:
            cp.start()

    def finish(self):
        local, remote = self._copies()
        for cp in remote:
            cp.wait_recv()
        for cp in remote:
            cp.wait_send()
        local.wait()


def _sibling_copies(dw_ref, r_ref, rows, send_sems, recv_sems):
    x, y, c = _mesh_pos()
    return [pltpu.make_async_remote_copy(
        src_ref=dw_ref.at[pl.ds(pl.multiple_of((2 * j + (1 - c)) * rows, 16), rows), :], dst_ref=r_ref.at[j],
        send_sem=send_sems.at[j], recv_sem=recv_sems.at[j], device_id=(x, y, 1 - c), device_id_type=MESH)
        for j in range(N_CHIPS)]


PAIR_CHUNKS = 4


def _pair_halves(gt, go):
    def body(gt_in, go_in, gt_ref, go_ref, send_sems, recv_sems):
        del gt_in, go_in
        x, y, c = _mesh_pos()
        copies = []
        for a, (ref, rows) in enumerate(((gt_ref, W_IN_ROWS), (go_ref, W_OUT_ROWS))):
            ch = rows // PAIR_CHUNKS
            for q in range(PAIR_CHUNKS):
                part = ref.at[pl.ds(pl.multiple_of(c * rows + q * ch, 8), ch), :]
                copies.append(pltpu.make_async_remote_copy(
                    src_ref=part, dst_ref=part, send_sem=send_sems.at[PAIR_CHUNKS * a + q],
                    recv_sem=recv_sems.at[PAIR_CHUNKS * a + q], device_id=(x, y, 1 - c), device_id_type=MESH))
        for cp in copies:
            cp.start()
        for cp in copies:
            cp.wait_recv()
        for cp in copies:
            cp.wait_send()

    return pl.pallas_call(
        body, name="pair_halves",
        in_specs=[HBM_SPEC, HBM_SPEC], out_specs=[HBM_SPEC, HBM_SPEC],
        out_shape=[jax.ShapeDtypeStruct(gt.shape, F32), jax.ShapeDtypeStruct(go.shape, F32)],
        input_output_aliases={0: 0, 1: 1},
        scratch_shapes=[pltpu.SemaphoreType.DMA((2 * PAIR_CHUNKS,)), pltpu.SemaphoreType.DMA((2 * PAIR_CHUNKS,))],
    )(gt, go)


VEC = (("g_pre", 2048), ("g_post", 2048), ("b_qkv", 1280), ("ln_v_g", 1024), ("ln_v_b", 1024), ("attn_sinks", 16))
VEC_ROWS = 8
LOSS_ROW = len(VEC)
MAT = (("w_spatial", NG * T), ("b_spatial", NG))
MAT_ROWS = sum(r for _, r in MAT)


class _SlotGather:
    def __init__(self, buf, send_sems, recv_sems):
        self.buf, self.send, self.recv = buf, send_sems, recv_sems
        x, y, c = _mesh_pos()
        self.c, self.me, self.sibling, self.my_id = c, (x, y, c), (x, y, 1 - c), 4 * x + 2 * y + c
        self.chips = [(1 - x, y), (x, 1 - y), (1 - x, 1 - y)]

    def _copy(self, k, blk, to):
        px, py, pc = blk
        slot = self.buf.at[4 * px + 2 * py + pc]
        return pltpu.make_async_remote_copy(
            src_ref=slot, dst_ref=slot, send_sem=self.send.at[k], recv_sem=self.recv.at[k],
            device_id=to, device_id_type=MESH)

    def _first(self):
        return [self._copy(0, self.me, self.sibling)] + [
            self._copy(1 + j, self.me, (*chip, self.c)) for j, chip in enumerate(self.chips)]

    def start(self):
        for cp in self._first():
            cp.start()

    def _passed(self):
        return [self._copy(4 + j, (*chip, self.c), self.sibling) for j, chip in enumerate(self.chips)]

    def pass_on(self):
        for j, chip in enumerate(self.chips):
            self._copy(1 + j, (*chip, self.c), self.me).wait_recv()
            self._passed()[j].start()

    def finish(self):
        self._copy(0, self.sibling, self.me).wait_recv()
        for j, chip in enumerate(self.chips):
            self._copy(4 + j, (*chip, 1 - self.c), self.me).wait_recv()
        for cp in self._first() + self._passed():
            cp.wait_send()

    def total(self):
        t = self.buf[0]
        for d in range(1, N_DEV):
            t = t + self.buf[d]
        return t


def _vec_allreduce(vec_grads, loss_part):
    n_vec = len(VEC)

    def body(*refs):
        g_in, loss_in, tot_v = refs[:n_vec], refs[n_vec], refs[n_vec + 1]
        vbuf, send_sems, recv_sems = refs[n_vec + 2:]
        gather = _SlotGather(vbuf, send_sems, recv_sems)
        vbuf[gather.my_id] = jnp.zeros((VEC_ROWS, D), F32)
        for r, (_, n) in enumerate(VEC):
            vbuf[gather.my_id, r:r + 1, 0:n] = g_in[r][...]
        vbuf[gather.my_id, LOSS_ROW:LOSS_ROW + 1, 0:128] = loss_in[...]
        gather.start()
        gather.pass_on()
        gather.finish()
        tot_v[...] = gather.total()

    vmem = pl.BlockSpec(memory_space=pltpu.VMEM)
    return pl.pallas_call(
        body, name="vec_allreduce", in_specs=[vmem] * (n_vec + 1), out_specs=vmem,
        out_shape=jax.ShapeDtypeStruct((VEC_ROWS, D), F32),
        scratch_shapes=[pltpu.VMEM((N_DEV, VEC_ROWS, D), F32), pltpu.SemaphoreType.DMA((7,)),
                        pltpu.SemaphoreType.DMA((7,))],
    )(*vec_grads, loss_part)


def _small_update(vec_total, mat_total, vec_state, mat_state):
    n_vec, n_mat = len(VEC), len(MAT)
    n_par = n_vec + n_mat
    n_in = 2 + 3 * n_par

    def body(*refs):
        tot_v, tot_m = refs[0], refs[1]
        st_in = refs[2:n_in]
        outs, loss_out = refs[n_in:n_in + 4 * n_par], refs[n_in + 4 * n_par]
        loss_out[...] = tot_v[LOSS_ROW:LOSS_ROW + 1, 0:128]
        r0 = 0
        for q in range(n_par):
            if q < n_vec:
                g = tot_v[q:q + 1, 0:VEC[q][1]]
            else:
                rows = MAT[q - n_vec][1]
                g = tot_m[r0:r0 + rows, :]
                r0 += rows
            w, m, v = (st_in[3 * q + t][...] for t in range(3))
            outs[4 * q][...] = g
            outs[4 * q + 1][...], outs[4 * q + 2][...], outs[4 * q + 3][...] = _adamw_math(w, g, m, v)

    state = [a for wmv in list(vec_state) + list(mat_state) for a in wmv]
    vmem = pl.BlockSpec(memory_space=pltpu.VMEM)
    out_shape = [jax.ShapeDtypeStruct(wmv[0].shape, F32) for wmv in list(vec_state) + list(mat_state) for _ in range(4)]
    out_shape.append(jax.ShapeDtypeStruct((1, 128), F32))
    res = pl.pallas_call(
        body, name="small_update",
        in_specs=[vmem] * n_in, out_specs=[vmem] * len(out_shape), out_shape=out_shape,
        compiler_params=pltpu.CompilerParams(vmem_limit_bytes=VMEM_LIMIT),
    )(vec_total, mat_total, *state)
    return [res[4 * q:4 * q + 4] for q in range(n_par)], res[-1]


def kernel(x, positions, g_pre, w_in, b_qkv, ln_v_g, ln_v_b, w_spatial, b_spatial, attn_sinks, w_out, g_post, loss_target, m_g_pre, m_w_in, m_b_qkv, m_ln_v_g, m_ln_v_b, m_w_spatial, m_b_spatial, m_attn_sinks, m_w_out, m_g_post, v_g_pre, v_w_in, v_b_qkv, v_ln_v_g, v_ln_v_b, v_w_spatial, v_b_spatial, v_attn_sinks, v_w_out, v_g_post):
    S = x.shape[1]
    c = lax.axis_index("c")
    c_arr = jnp.reshape(c, (1,)).astype(jnp.int32)
    x2 = x[0]
    target = loss_target[0]
    pos = positions.reshape(S, 1)
    half = HD // 2
    inv_freq = ROPE_THETA ** (-jnp.arange(half, dtype=F32) * (2.0 / HD))
    invf = jnp.tile(inv_freq, 128 // half).reshape(1, 128)
    bias = jnp.concatenate([jnp.zeros((OFF_Q,), F32), b_qkv[0], jnp.zeros((D_IN - OFF_ZB,), F32)]).reshape(1, D_IN)
    b_s_col = b_spatial[0].reshape(NG, T, 1)
    sinks = jnp.repeat(attn_sinks[0], T).reshape(NQ * T, 1)

    chip = 2 * lax.axis_index("x") + lax.axis_index("y")
    wt_part = lax.dynamic_slice_in_dim(w_in[0].T.astype(BF16), c * W_IN_ROWS, W_IN_ROWS, axis=0)
    wo_part = lax.dynamic_slice_in_dim(w_out[0].astype(BF16), c * W_OUT_ROWS, W_OUT_ROWS, axis=0)
    sched = jnp.asarray(PROJ_SCHEDULE, jnp.int32)[chip]

    h, proj, wt, cos, sin = _prenorm_inproj(sched, x2, g_pre, bias, wt_part, pos, invf)
    ycat, wo, q_roped = _mid_fwd(proj, cos, sin, ln_v_g, ln_v_b, w_spatial[0], b_s_col, sinks, wo_part)
    dy, dout, loss_part, dg_post = _outproj_loss(ycat, wo, x2, target, g_post)

    dwo = _matmul_tn(ycat, dy, 1024, "dw_out")
    dycat, ro = _dycat(dy, wo, dwo)
    po = _presum(c_arr, dwo, ro, W_OUT_ROWS)
    dproj, dln_g, dln_b, dws, dbs, dsink, dbqkv, bo = _mid_bwd(
        proj, dycat, q_roped, cos, sin, ln_v_g, ln_v_b, w_spatial[0], b_s_col, sinks, po)
    dwt, mat_total = _matmul_tn(dproj, h, 768, "dw_in_t",
                                mat_grads=[dws.reshape(NG * T, T), dbs.reshape(NG, T)])
    pt = _sibling_presum(c_arr, dwt, W_IN_ROWS, "sibling_presum_in")
    grad_x, dg_pre, bt = _dh_prenorm_bwd(dproj, wt, x2, dout, g_pre, pt)
    gt, go = _pair_halves(_sum_chips(c_arr, bt, "sum_chips_in"), _sum_chips(c_arr, bo, "sum_chips_out"))

    g_w_in, d_w_in, nm_w_in, nv_w_in = (a.T for a in _adamw(w_in[0].T, gt, m_w_in[0].T, v_w_in[0].T, "adamw_w_in"))
    g_w_out, d_w_out, nm_w_out, nv_w_out = _adamw(w_out[0], go, m_w_out[0], v_w_out[0], "adamw_w_out")

    state = {"g_pre": (g_pre, m_g_pre, v_g_pre), "g_post": (g_post, m_g_post, v_g_post),
             "b_qkv": (b_qkv, m_b_qkv, v_b_qkv), "ln_v_g": (ln_v_g, m_ln_v_g, v_ln_v_g),
             "ln_v_b": (ln_v_b, m_ln_v_b, v_ln_v_b), "attn_sinks": (attn_sinks, m_attn_sinks, v_attn_sinks),
             "w_spatial": tuple(a.reshape(NG * T, T) for a in (w_spatial, m_w_spatial, v_w_spatial)),
             "b_spatial": tuple(a.reshape(NG, T) for a in (b_spatial, m_b_spatial, v_b_spatial))}
    grads = {"g_pre": dg_pre, "g_post": dg_post, "b_qkv": dbqkv, "ln_v_g": dln_g, "ln_v_b": dln_b,
             "attn_sinks": dsink[:, 0].reshape(1, NQ)}
    vec_total = _vec_allreduce([grads[n] for n, _ in VEC], loss_part)
    results, loss = _small_update(vec_total, mat_total, [state[n] for n, _ in VEC], [state[n] for n, _ in MAT])
    small = {n: [a.reshape(w.shape) for a in res]
             for (n, _), res, w in zip(VEC + MAT, results, [state[n][0] for n, _ in VEC + MAT])}
    small["w_spatial"] = [a.reshape(w_spatial.shape) for a in small["w_spatial"]]
    small["b_spatial"] = [a.reshape(b_spatial.shape) for a in small["b_spatial"]]
    big = {"w_in": [a[None] for a in (g_w_in, d_w_in, nm_w_in, nv_w_in)],
           "w_out": [a[None] for a in (g_w_out, d_w_out, nm_w_out, nv_w_out)]}
    order = ("g_pre", "w_in", "b_qkv", "ln_v_g", "ln_v_b", "w_spatial", "b_spatial", "attn_sinks", "w_out", "g_post")
    leaves = {**small, **big}
    return (loss[0, 0], grad_x[None], *[leaves[n][t] for t in range(4) for n in order])
```

```python
import jax
import jax.numpy as jnp
from jax import lax
from jax.experimental import pallas as pl
from jax.experimental.pallas import tpu as pltpu

F32 = jnp.float32
BF16 = jnp.bfloat16
MESH = pl.DeviceIdType.MESH

D = 2048
DG = 1024
T = 128
NG = 8
HD = 64
NQ = 16
D_IN = 5376
OFF_U, OFF_V, OFF_ZA, OFF_Q, OFF_K, OFF_VA, OFF_ZB = 0, 1024, 2048, 3072, 4096, 4224, 4352
D_QKV = 1280
EPS = 1e-6
ROPE_THETA = 10000.0
N_CHIPS = 4
N_DEV = 8
W_IN_ROWS = D_IN // N_DEV
W_OUT_ROWS = D // N_DEV

ADAM_LR, ADAM_B1, ADAM_B2, ADAM_EPS, ADAM_WD, ADAM_STEP = 0.001, 0.9, 0.999, 1e-08, 0.01, 10

VMEM_LIMIT = 56 * 1024 * 1024


def _tile(n, pref):
    return pref if n % pref == 0 else n


def _params(sem=None, vmem=VMEM_LIMIT):
    return pltpu.CompilerParams(dimension_semantics=sem, vmem_limit_bytes=vmem)


def _sigmoid(z):
    return 1.0 / (1.0 + jnp.exp(-z))


def _row_mean(v):
    return jnp.mean(v, axis=-1, keepdims=True)


def _dot(a, b, dims):
    return lax.dot_general(a, b, (dims, ((), ())), preferred_element_type=F32)


NN = ((1,), (0,))
NT = ((1,), (1,))
TN = ((0,), (0,))


PROJ_TN = 768
N_PROJ_TILES = D_IN // PROJ_TN
PROJ_SCHEDULE = ((0, 1, 2, 4, 3, 6, 5, 5), (2, 1, 0, 6, 4, 3, 5, 4), (4, 5, 6, 0, 2, 1, 3, 4), (6, 5, 4, 2, 3, 0, 1, 5))


def _place_locally(src_ref, dst_rows_ref, sems, n_chunks):
    ch = src_ref.shape[0] // n_chunks
    return [pltpu.make_async_copy(src_ref.at[pl.ds(q * ch, ch), :], dst_rows_ref.at[pl.ds(q * ch, ch), :],
                                  sems.at[q]) for q in range(n_chunks)]


def _prenorm_inproj(sched, x, g, bias, wt_part, pos, invf):
    S = x.shape[0]
    tp, tm = _tile(S, 512), _tile(S, 1024)
    n_pre, ns = S // tp, S // tm
    n_steps = n_pre + N_PROJ_TILES * ns
    pos_of = lambda i: jnp.maximum(i - n_pre, 0) // ns
    row_of = lambda i: jnp.maximum(i - n_pre, 0) % ns

    def body(sched_ref, x_ref, g_ref, b_ref, wpart_ref, pos_ref, invf_ref, h_ref, proj_ref, wt_ref, cos_ref, sin_ref,
             h_all, w_tile, stage, send_sems, recv_sems, w_sems, local_sems):
        i = pl.program_id(0)
        x_, y_, c = _mesh_pos()
        me, sibling = (x_, y_, c), (x_, y_, 1 - c)
        chips = [(1 - x_, y_), (x_, 1 - y_), (1 - x_, 1 - y_)]

        def block(pos):
            px, py, pc = pos
            return wt_ref.at[pl.ds(pl.multiple_of((4 * px + 2 * py + pc) * W_IN_ROWS, 16), W_IN_ROWS), :]

        def copy(k, blk, to):
            return pltpu.make_async_remote_copy(
                src_ref=wpart_ref if blk is me else block(blk), dst_ref=block(blk),
                send_sem=send_sems.at[k], recv_sem=recv_sems.at[k], device_id=to, device_id_type=MESH)

        stage_in = pltpu.make_async_copy(wpart_ref, stage, local_sems.at[0])
        stage_out = pltpu.make_async_copy(stage, block(me), local_sems.at[1])

        relay = (jnp.where(c == 0, x_, 1 - x_), jnp.where(c == 0, 1 - y_, y_))
        relayed = (jnp.where(c == 0, 1 - x_, x_), jnp.where(c == 0, y_, 1 - y_))

        early = c == y_
        y_send = copy(2, me, (*chips[1], c))

        def own_sends():
            return [copy(0, me, sibling), y_send, copy(1, me, (*chips[0], c))]

        def passed_on():
            return [copy(4, (*chips[0], c), sibling), copy(5, (*chips[1], c), sibling),
                    copy(3, (*relayed, c), (*relay, c))]

        def early_block_arrives():
            @pl.when(early)
            def _():
                copy(2, (*chips[1], c), me).wait_recv()
                copy(5, (*chips[1], c), sibling).start()
                y_send.start()

            pl.when(jnp.logical_not(early))(lambda: copy(5, (*chips[1], 1 - c), me).wait_recv())

        def neighbours_arrive():
            copy(1, (*chips[0], c), me).wait_recv()
            copy(4, (*chips[0], c), sibling).start()

            @pl.when(early)
            def _():
                copy(3, (*relayed, c), (*relay, c)).start()
                copy(5, (*chips[1], 1 - c), me).wait_recv()

            @pl.when(jnp.logical_not(early))
            def _():
                copy(2, (*chips[1], c), me).wait_recv()
                copy(5, (*chips[1], c), sibling).start()
                copy(3, (*relayed, c), (*relay, c)).start()

        def diagonal_arrives():
            copy(3, (*chips[2], c), me).wait_recv()
            copy(6, (*chips[2], c), sibling).start()
            copy(6, (*chips[2], 1 - c), me).wait_recv()

        def tile_load(p):
            slot = p % 2
            rows = wt_ref.at[pl.ds(pl.multiple_of(sched_ref[p] * PROJ_TN, 16), PROJ_TN), :]
            return pltpu.make_async_copy(rows, w_tile.at[slot], w_sems.at[slot])

        def prepare(p):
            p = jnp.asarray(p, jnp.int32)

            @pl.when(p == 0)
            def _():
                copy(0, sibling, me).wait_recv()
                stage_out.wait()

            pl.when(p == 1)(early_block_arrives)
            pl.when(p == 2)(neighbours_arrive)
            pl.when(p == 3)(lambda: copy(4, (*chips[0], 1 - c), me).wait_recv())
            pl.when(p == sched_ref[N_PROJ_TILES])(diagonal_arrives)
            tile_load(p).start()

        @pl.when(i == 0)
        def _():
            stage_in.start()
            copy(0, me, sibling).start()
            copy(1, me, (*chips[0], c)).start()
            pl.when(jnp.logical_not(early))(y_send.start)
            stage_in.wait()
            stage_out.start()

        @pl.when(i < n_pre)
        def _():
            xv = x_ref[...]
            r = lax.rsqrt(_row_mean(xv * xv) + EPS)
            hv = (xv * r * g_ref[...]).astype(BF16)
            h_ref[...] = hv
            h_all[pl.ds(pl.multiple_of(i * tp, tp), tp), :] = hv
            ang = pos_ref[...].astype(F32) * invf_ref[...]
            cos_ref[...] = jnp.cos(ang)
            sin_ref[...] = jnp.sin(ang)

        pl.when(i == n_pre - 1)(lambda: prepare(0))

        @pl.when(i >= n_pre)
        def _():
            p, s = pos_of(i), row_of(i)
            pl.when(s == 0)(lambda: tile_load(p).wait())
            pl.when((s == ns - 1) & (p < N_PROJ_TILES - 1))(lambda: prepare(p + 1))
            hv = h_all[pl.ds(pl.multiple_of(s * tm, tm), tm), :]
            proj_ref[...] = _dot(hv, w_tile[p % 2], NT) + b_ref[...]

        @pl.when(i == n_steps - 1)
        def _():
            for cp in own_sends() + passed_on() + [copy(6, (*chips[2], c), sibling)]:
                cp.wait_send()

    return pl.pallas_call(
        body, name="prenorm_inproj",
        grid_spec=pltpu.PrefetchScalarGridSpec(
            num_scalar_prefetch=1, grid=(n_steps,),
            in_specs=[pl.BlockSpec((tp, D), lambda i, sc: (jnp.minimum(i, n_pre - 1), 0)),
                      pl.BlockSpec((1, D), lambda i, sc: (0, 0)),
                      pl.BlockSpec((1, PROJ_TN), lambda i, sc: (0, sc[pos_of(i)])),
                      HBM_SPEC,
                      pl.BlockSpec((tp, 1), lambda i, sc: (jnp.minimum(i, n_pre - 1), 0)),
                      pl.BlockSpec((1, 128), lambda i, sc: (0, 0))],
            out_specs=[pl.BlockSpec((tp, D), lambda i, sc: (jnp.minimum(i, n_pre - 1), 0)),
                       pl.BlockSpec((tm, PROJ_TN), lambda i, sc: (row_of(i), sc[pos_of(i)])),
                       HBM_SPEC,
                       pl.BlockSpec((tp, 128), lambda i, sc: (jnp.minimum(i, n_pre - 1), 0)),
                       pl.BlockSpec((tp, 128), lambda i, sc: (jnp.minimum(i, n_pre - 1), 0))],
            scratch_shapes=[pltpu.VMEM((S, D), BF16), pltpu.VMEM((2, PROJ_TN, D), BF16),
                            pltpu.VMEM((W_IN_ROWS, D), BF16),
                            pltpu.SemaphoreType.DMA((7,)), pltpu.SemaphoreType.DMA((7,)),
                            pltpu.SemaphoreType.DMA((2,)), pltpu.SemaphoreType.DMA((2,))]),
        out_shape=[jax.ShapeDtypeStruct((S, D), BF16), jax.ShapeDtypeStruct((S, D_IN), F32),
                   jax.ShapeDtypeStruct((D_IN, D), BF16),
                   jax.ShapeDtypeStruct((S, 128), F32), jax.ShapeDtypeStruct((S, 128), F32)],
        compiler_params=_params(("arbitrary",)),
    )(sched, x, g, bias, wt_part, pos, invf)


def _rot_half(xs, first_half):
    return jnp.where(first_half, -pltpu.roll(xs, 96, 1), pltpu.roll(xs, 32, 1))


GH = NQ // 2


MASKED = -1e30


def _attn_consts():
    lane = lax.broadcasted_iota(jnp.int32, (T, 128), 1)
    row = lax.broadcasted_iota(jnp.int32, (GH * T, T), 0) & (T - 1)
    on_diag_or_below = row >= lax.broadcasted_iota(jnp.int32, (GH * T, T), 1)
    return (lane & (HD - 1)) < (HD // 2), lane < HD, on_diag_or_below


def _stack_heads(slab_fn, grp, lo64):
    blocks = []
    for jj in range(GH // 2):
        s = slab_fn(GH // 2 * grp + jj)
        blocks += [jnp.where(lo64, s, 0.0), jnp.where(lo64, 0.0, s)]
    return jnp.concatenate(blocks, axis=0)


def _unstack_heads(stacked, jj, lo64):
    return jnp.where(lo64, stacked[(2 * jj) * T:(2 * jj + 1) * T], stacked[(2 * jj + 1) * T:(2 * jj + 2) * T])


def _both_halves(kv, grp):
    lo = lax.broadcasted_iota(jnp.int32, kv.shape, 1) < HD
    swapped = pltpu.roll(kv, HD, 1)
    return jnp.where(lo, kv, swapped) if grp == 0 else jnp.where(lo, swapped, kv)


def _one_half(acc, grp):
    lo = lax.broadcasted_iota(jnp.int32, acc.shape, 1) < HD
    return jnp.where(lo if grp == 0 else jnp.logical_not(lo), acc + pltpu.roll(acc, HD, 1), 0.0)


def _layer_norm_stats(v):
    mu = _row_mean(v)
    xc = v - mu
    var = _row_mean(xc * xc)
    rs = lax.rsqrt(var + EPS)
    return xc * rs, rs


def _fold(both, own):
    return jnp.where(own, both[:, T:2 * T], both[:, 0:T])


def _unfold(p, own):
    return jnp.concatenate([jnp.where(own, 0.0, p), jnp.where(own, p, 0.0)], axis=1).astype(BF16)


def _band_softmax(q_scaled, k_both, own, has_prev, sink):
    s_both = _dot(q_scaled, k_both, NT)
    s_prev = s_both[:, 0:T]
    if has_prev is not None:
        s_prev = s_prev + jnp.where(has_prev, 0.0, MASKED)
    s = jnp.where(own, s_both[:, T:2 * T], s_prev)
    m = jnp.maximum(jnp.max(s, axis=-1, keepdims=True), sink)
    e = jnp.exp(s - m)
    es = jnp.exp(sink - m)
    inv = 1.0 / (jnp.sum(e, axis=-1, keepdims=True) + es)
    return e * inv, es * inv


BPS_FWD = 4
BPS_BWD = 4


def _mid_specs(nt, rev, bps):
    tile = (lambda i: nt - 1 - i) if rev else (lambda i: i)
    prev = lambda i: jnp.maximum(bps * tile(i) - 1, 0)
    rows = bps * T
    return tile, [
        pl.BlockSpec((rows, D_IN), lambda i: (tile(i), 0)),
        pl.BlockSpec((T, 2 * T), lambda i: (prev(i), OFF_K // (2 * T))),
    ], [
        pl.BlockSpec((rows, 128), lambda i: (tile(i), 0)),
        pl.BlockSpec((rows, 128), lambda i: (tile(i), 0)),
        pl.BlockSpec((T, 128), lambda i: (prev(i), 0)),
        pl.BlockSpec((T, 128), lambda i: (prev(i), 0)),
        pl.BlockSpec((1, DG), lambda i: (0, 0)),
        pl.BlockSpec((1, DG), lambda i: (0, 0)),
        pl.BlockSpec((NG, T, T), lambda i: (0, 0, 0)),
        pl.BlockSpec((NG, T, 1), lambda i: (0, 0, 0)),
        pl.BlockSpec((NQ * T, 1), lambda i: (0, 0)),
    ]


def _rope(xs, cosv, sinv, first_half):
    return xs * cosv + _rot_half(xs, first_half) * sinv


def _mid_fwd(proj, cos, sin, ln_g, ln_b, w_s, b_s, sinks, wo_part):
    S = proj.shape[0]
    BPS = BPS_FWD if S % (BPS_FWD * T) == 0 else 1
    nt = S // (BPS * T)
    tile, proj_specs, par_specs = _mid_specs(nt, False, BPS)

    def body(p_ref, kvp_ref, cos_ref, sin_ref, cosp_ref, sinp_ref, lng_ref, lnb_ref, ws_ref, bs_ref, sink_ref,
             wpart_ref, y_ref, wo_ref, q_ref, wm_ref, send_sems, recv_sems, local_sems):
        i = pl.program_id(0)
        first_half, lo64, own = _attn_consts()
        gather = _RowGather(wpart_ref, wo_ref, W_OUT_ROWS, send_sems, recv_sems, local_sems)
        pl.when(i == 0)(gather.start)
        pl.when(i == nt // 2)(gather.pass_on)

        @pl.when(i == 0)
        def _():
            tril = lax.broadcasted_iota(jnp.int32, (T, T), 0) >= lax.broadcasted_iota(jnp.int32, (T, T), 1)
            for g in range(NG):
                wm_ref[g] = jnp.where(tril, ws_ref[g], 0.0).astype(BF16)

        def block(b, k_prev, v_prev, has_prev):
            rows = slice(b * T, (b + 1) * T)
            xhat, _ = _layer_norm_stats(p_ref[rows, OFF_V:OFF_V + DG])
            vn = xhat * lng_ref[...] + lnb_ref[...]
            for g in range(NG):
                sl = slice(128 * g, 128 * g + 128)
                mixed = _dot(wm_ref[g], vn[:, sl].astype(BF16), NN) + bs_ref[g]
                z = p_ref[rows, OFF_ZA + 128 * g:OFF_ZA + 128 * g + 128]
                u = p_ref[rows, OFF_U + 128 * g:OFF_U + 128 * g + 128]
                y_ref[rows, sl] = (u * mixed * (z * _sigmoid(z))).astype(BF16)

            cosv, sinv = cos_ref[rows, :], sin_ref[rows, :]
            k_cur = _rope(p_ref[rows, OFF_K:OFF_K + 128], cosv, sinv, first_half)
            v_cur = p_ref[rows, OFF_VA:OFF_VA + 128]

            def q_slab(j):
                q = (_rope(p_ref[rows, OFF_Q + 128 * j:OFF_Q + 128 * j + 128], cosv, sinv, first_half)
                     * (HD ** -0.5)).astype(BF16)
                q_ref[rows, 128 * j:128 * j + 128] = q
                return q

            k_both = jnp.concatenate([k_prev, k_cur], axis=0)
            v_both = jnp.concatenate([v_prev, v_cur], axis=0)
            for grp in range(2):
                qs = _stack_heads(q_slab, grp, lo64)
                kg, vg = _both_halves(k_both, grp).astype(BF16), _both_halves(v_both, grp).astype(BF16)
                p, _ = _band_softmax(qs, kg, own, has_prev, sink_ref[GH * T * grp:GH * T * (grp + 1), :])
                o_st = _dot(_unfold(p, own), vg, NN)
                for jj in range(GH // 2):
                    c0 = 128 * (GH // 2 * grp + jj)
                    zb = p_ref[rows, OFF_ZB + c0:OFF_ZB + c0 + 128]
                    o = _unstack_heads(o_st, jj, lo64)
                    y_ref[rows, DG + c0:DG + c0 + 128] = (o * (zb * _sigmoid(zb))).astype(BF16)
            return k_cur, v_cur

        k_prev = _rope(kvp_ref[:, 0:128], cosp_ref[...], sinp_ref[...], first_half)
        kv = block(0, k_prev, kvp_ref[:, 128:256], i > 0)
        for b in range(1, BPS):
            kv = block(b, *kv, None)
        pl.when(i == nt - 1)(gather.finish)

    return pl.pallas_call(
        body, name="mid_fwd", grid=(nt,),
        in_specs=proj_specs + par_specs + [HBM_SPEC],
        out_specs=[pl.BlockSpec((BPS * T, D), lambda i: (tile(i), 0)), HBM_SPEC,
                   pl.BlockSpec((BPS * T, NQ * HD), lambda i: (tile(i), 0))],
        out_shape=[jax.ShapeDtypeStruct((S, D), BF16), jax.ShapeDtypeStruct((D, D), BF16),
                   jax.ShapeDtypeStruct((S, NQ * HD), BF16)],
        scratch_shapes=[pltpu.VMEM((NG, T, T), BF16)] + GATHER_SEMS,
        compiler_params=_params(("arbitrary",)),
    )(proj, proj, cos, sin, cos, sin, ln_g, ln_b, w_s, b_s, sinks, wo_part)


def _mid_bwd(proj, dycat, q_roped, cos, sin, ln_g, ln_b, w_s, b_s, sinks, po):
    S = proj.shape[0]
    BPS = BPS_BWD if S % (BPS_BWD * T) == 0 else 1
    nt = S // (BPS * T)
    tile, proj_specs, par_specs = _mid_specs(nt, True, BPS)
    const2 = lambda i: (0, 0)

    def body(p_ref, kvp_ref, dyc_ref, q_ref, cos_ref, sin_ref, cosp_ref, sinp_ref, lng_ref, lnb_ref, ws_ref, bs_ref,
             sink_ref, po_ref, dp_ref, dlng_ref, dlnb_ref, dws_ref, dbs_ref, dsink_ref, dbqkv_ref, bo_ref,
             carry_ref, dvn_ref, wm_ref, wmt_ref, send_sems, recv_sems, local_sem):
        i = pl.program_id(0)
        first_half, lo64, own = _attn_consts()
        tril = lax.broadcasted_iota(jnp.int32, (T, T), 0) >= lax.broadcasted_iota(jnp.int32, (T, T), 1)
        exchange = _OwnerExchange(po_ref, bo_ref, send_sems, recv_sems, local_sem)
        pl.when(i == 0)(exchange.start)

        @pl.when(i == 0)
        def _():
            for g in range(NG):
                wm = jnp.where(tril, ws_ref[g], 0.0)
                wm_ref[g] = wm.astype(BF16)
                wmt_ref[g] = wm.T.astype(BF16)
            dlng_ref[...] = jnp.zeros_like(dlng_ref)
            dlnb_ref[...] = jnp.zeros_like(dlnb_ref)
            dws_ref[...] = jnp.zeros_like(dws_ref)
            dbs_ref[...] = jnp.zeros_like(dbs_ref)
            dsink_ref[...] = jnp.zeros_like(dsink_ref)
            dbqkv_ref[...] = jnp.zeros_like(dbqkv_ref)
            carry_ref[...] = jnp.zeros_like(carry_ref)

        def roped_k(b):
            rows = slice(b * T, (b + 1) * T)
            return _rope(p_ref[rows, OFF_K:OFF_K + 128], cos_ref[rows, :], sin_ref[rows, :], first_half)

        def block(b, k_prev, v_prev, has_prev, dk_next, dv_next):
            rows = slice(b * T, (b + 1) * T)
            xhat, rs = _layer_norm_stats(p_ref[rows, OFF_V:OFF_V + DG])
            lng = lng_ref[...]
            vn = xhat * lng + lnb_ref[...]
            for g in range(NG):
                sl = slice(128 * g, 128 * g + 128)
                vng = vn[:, sl].astype(BF16)
                mixed = _dot(wm_ref[g], vng, NN) + bs_ref[g]
                z = p_ref[rows, OFF_ZA + 128 * g:OFF_ZA + 128 * g + 128]
                u = p_ref[rows, OFF_U + 128 * g:OFF_U + 128 * g + 128]
                dy = dyc_ref[rows, sl]
                sg = _sigmoid(z)
                sa = z * sg
                dp_ref[rows, OFF_U + 128 * g:OFF_U + 128 * g + 128] = (dy * mixed * sa).astype(BF16)
                dp_ref[rows, OFF_ZA + 128 * g:OFF_ZA + 128 * g + 128] = (
                    dy * u * mixed * (sg * (1.0 + z * (1.0 - sg)))).astype(BF16)
                dm = dy * u * sa
                dmb = dm.astype(BF16)
                dvn_ref[rows, sl] = _dot(wmt_ref[g], dmb, NN)
                dws_ref[g] += jnp.where(tril, _dot(dmb, vng, NT), 0.0)
                dbs_ref[g] += jnp.sum(dm, axis=1, keepdims=True)
            dvn = dvn_ref[rows, :]
            dlng_ref[...] += jnp.sum(dvn * xhat, axis=0, keepdims=True)
            dlnb_ref[...] += jnp.sum(dvn, axis=0, keepdims=True)
            dxh = dvn * lng
            dv_g = rs * (dxh - _row_mean(dxh)
                         - xhat * _row_mean(dxh * xhat))
            dp_ref[rows, OFF_V:OFF_V + DG] = dv_g.astype(BF16)

            cosv, sinv = cos_ref[rows, :], sin_ref[rows, :]
            k_cur = roped_k(b)
            v_cur = p_ref[rows, OFF_VA:OFF_VA + 128]

            def q_slab(j):
                return q_ref[rows, 128 * j:128 * j + 128]

            def do_slab(j):
                zb = p_ref[rows, OFF_ZB + 128 * j:OFF_ZB + 128 * j + 128]
                return dyc_ref[rows, DG + 128 * j:DG + 128 * j + 128] * (zb * _sigmoid(zb))

            k_both = jnp.concatenate([k_prev, k_cur], axis=0)
            v_both = jnp.concatenate([v_prev, v_cur], axis=0)
            dk_both, dv_both = jnp.zeros((2 * T, 128), F32), jnp.zeros((2 * T, 128), F32)
            for grp in range(2):
                qs = _stack_heads(q_slab, grp, lo64)
                d_o = _stack_heads(do_slab, grp, lo64)
                dob = d_o.astype(BF16)
                kg, vg = _both_halves(k_both, grp).astype(BF16), _both_halves(v_both, grp).astype(BF16)
                p, ps = _band_softmax(qs, kg, own, has_prev, sink_ref[GH * T * grp:GH * T * (grp + 1), :])
                p_both = _unfold(p, own)
                o_st = _dot(p_both, vg, NN)
                delta = jnp.sum(d_o * o_st, axis=-1, keepdims=True)
                ds_both = _unfold(p * (_fold(_dot(dob, vg, NT), own) - delta), own)
                dq_st = _dot(ds_both, kg, NN) * (HD ** -0.5)
                dk_both = dk_both + _one_half(_dot(ds_both, qs, TN), grp)
                dv_both = dv_both + _one_half(_dot(p_both, dob, TN), grp)
                dsink_rows = ps * delta
                for hh in range(GH):
                    h = GH * grp + hh
                    dsink_ref[h:h + 1, :] += jnp.broadcast_to(
                        -jnp.sum(dsink_rows[hh * T:(hh + 1) * T], axis=0, keepdims=True), (1, 128))
                for jj in range(GH // 2):
                    c0 = 128 * (GH // 2 * grp + jj)
                    zb = p_ref[rows, OFF_ZB + c0:OFF_ZB + c0 + 128]
                    sg = _sigmoid(zb)
                    o = _unstack_heads(o_st, jj, lo64)
                    dp_ref[rows, OFF_ZB + c0:OFF_ZB + c0 + 128] = (
                        dyc_ref[rows, DG + c0:DG + c0 + 128] * o * (sg * (1.0 + zb * (1.0 - sg)))).astype(BF16)
                    dq = _unstack_heads(dq_st, jj, lo64)
                    dq_pre = dq * cosv - _rot_half(dq, first_half) * sinv
                    dp_ref[rows, OFF_Q + c0:OFF_Q + c0 + 128] = dq_pre.astype(BF16)
                    dbqkv_ref[:, c0:c0 + 128] += jnp.sum(dq_pre, axis=0, keepdims=True)
            dk_prev, dv_prev = dk_both[0:T], dv_both[0:T]
            dk_cur, dv_cur = dk_both[T:2 * T] + dk_next, dv_both[T:2 * T] + dv_next
            dk_pre = dk_cur * cosv - _rot_half(dk_cur, first_half) * sinv
            dp_ref[rows, OFF_K:OFF_K + 128] = dk_pre.astype(BF16)
            dp_ref[rows, OFF_VA:OFF_VA + 128] = dv_cur.astype(BF16)
            dbqkv_ref[:, 1024:1152] += jnp.sum(dk_pre, axis=0, keepdims=True)
            dbqkv_ref[:, 1152:1280] += jnp.sum(dv_cur, axis=0, keepdims=True)
            return dk_prev, dv_prev

        grads = carry_ref[:, 0:128], carry_ref[:, 128:256]
        for b in range(BPS - 1, 0, -1):
            prows = slice((b - 1) * T, b * T)
            grads = block(b, roped_k(b - 1), p_ref[prows, OFF_VA:OFF_VA + 128], None, *grads)
        k_prev = _rope(kvp_ref[:, 0:128], cosp_ref[...], sinp_ref[...], first_half)
        grads = block(0, k_prev, kvp_ref[:, 128:256], i < nt - 1, *grads)
        carry_ref[:, 0:128], carry_ref[:, 128:256] = grads
        pl.when(i == nt - 1)(exchange.finish)

    return pl.pallas_call(
        body, name="mid_bwd", grid=(nt,),
        in_specs=proj_specs + [pl.BlockSpec((BPS * T, D), lambda i: (tile(i), 0)),
                               pl.BlockSpec((BPS * T, NQ * HD), lambda i: (tile(i), 0))] + par_specs + [HBM_SPEC],
        out_specs=[pl.BlockSpec((BPS * T, D_IN), lambda i: (tile(i), 0)),
                   pl.BlockSpec((1, DG), const2), pl.BlockSpec((1, DG), const2),
                   pl.BlockSpec((NG, T, T), lambda i: (0, 0, 0)), pl.BlockSpec((NG, T, 1), lambda i: (0, 0, 0)),
                   pl.BlockSpec((NQ, 128), const2), pl.BlockSpec((1, D_QKV), const2), HBM_SPEC],
        out_shape=[jax.ShapeDtypeStruct((S, D_IN), BF16),
                   jax.ShapeDtypeStruct((1, DG), F32), jax.ShapeDtypeStruct((1, DG), F32),
                   jax.ShapeDtypeStruct((NG, T, T), F32), jax.ShapeDtypeStruct((NG, T, 1), F32),
                   jax.ShapeDtypeStruct((NQ, 128), F32), jax.ShapeDtypeStruct((1, D_QKV), F32),
                   jax.ShapeDtypeStruct(po.shape, BF16)],
        scratch_shapes=[pltpu.VMEM((T, 2 * T), F32), pltpu.VMEM((BPS * T, DG), F32),
                        pltpu.VMEM((NG, T, T), BF16), pltpu.VMEM((NG, T, T), BF16)] + OWNER_SEMS,
        compiler_params=_params(("arbitrary",)),
    )(proj, proj, dycat, q_roped, cos, sin, cos, sin, ln_g, ln_b, w_s, b_s, sinks, po)


def _outproj_loss(ycat, wo, x, target, g_post):
    S = ycat.shape[0]
    tm = _tile(S, 512)
    nt = S // tm
    n_part = 2 if tm % 32 == 0 else 1
    tp = tm // n_part
    const2 = lambda i: (0, 0)

    def body(yc_ref, w_ref, x_ref, t_ref, g_ref, dy_ref, dout_ref, loss_ref, dg_ref, lacc_ref):
        i = pl.program_id(0)

        @pl.when(i == 0)
        def _():
            dg_ref[...] = jnp.zeros_like(dg_ref)
            lacc_ref[...] = jnp.zeros_like(lacc_ref)

        g = g_ref[...]
        ys = [_dot(yc_ref[q * tp:(q + 1) * tp, :], w_ref[...], NN) for q in range(n_part)]
        for q, y in enumerate(ys):
            rows = slice(q * tp, (q + 1) * tp)
            r = lax.rsqrt(_row_mean(y * y) + EPS)
            yh = y * r
            diff = x_ref[rows, :] + yh * g - t_ref[rows, :]
            lacc_ref[...] += jnp.sum(diff * diff, axis=0, keepdims=True)
            dout = diff * (1.0 / D)
            dout_ref[rows, :] = dout
            dg_ref[...] += jnp.sum(dout * yh, axis=0, keepdims=True)
            dyh = dout * g
            dy_ref[rows, :] = (r * (dyh - yh * _row_mean(dyh * yh))).astype(BF16)

        @pl.when(i == nt - 1)
        def _():
            loss_ref[...] = jnp.broadcast_to(jnp.sum(lacc_ref[...], axis=1, keepdims=True) * (0.5 / D), (1, 128))

    row = lambda i: (i, 0)
    return pl.pallas_call(
        body, name="outproj_loss", grid=(nt,),
        in_specs=[pl.BlockSpec((tm, D), row), pl.BlockSpec((D, D), const2, pipeline_mode=pl.Buffered(1)),
                  pl.BlockSpec((tm, D), row), pl.BlockSpec((tm, D), row), pl.BlockSpec((1, D), const2)],
        out_specs=[pl.BlockSpec((tm, D), row), pl.BlockSpec((tm, D), row), pl.BlockSpec((1, 128), const2),
                   pl.BlockSpec((1, D), const2)],
        out_shape=[jax.ShapeDtypeStruct((S, D), BF16), jax.ShapeDtypeStruct((S, D), F32),
                   jax.ShapeDtypeStruct((1, 128), F32), jax.ShapeDtypeStruct((1, D), F32)],
        scratch_shapes=[pltpu.VMEM((1, D), F32)],
        compiler_params=_params(("arbitrary",)),
    )(ycat, wo, x, target, g_post)


def _dycat(dy, wo, dwo):
    S = dy.shape[0]
    tm = _tile(S, 1024)
    nt = S // tm

    def body(dy_ref, w_ref, dwo_ref, o_ref, r_ref, send_sems, recv_sems):
        i = pl.program_id(0)
        copies = _sibling_copies(dwo_ref, r_ref, W_OUT_ROWS, send_sems, recv_sems)

        @pl.when(i == 0)
        def _():
            for cp in copies:
                cp.start()

        o_ref[...] = _dot(dy_ref[...], w_ref[...], NT)

        @pl.when(i == nt - 1)
        def _():
            for cp in copies:
                cp.wait_recv()
            for cp in copies:
                cp.wait_send()

    return pl.pallas_call(
        body, name="dycat", grid=(nt,),
        in_specs=[pl.BlockSpec((tm, D), lambda i: (i, 0)),
                  pl.BlockSpec((D, D), lambda i: (0, 0), pipeline_mode=pl.Buffered(1)), HBM_SPEC],
        out_specs=[pl.BlockSpec((tm, D), lambda i: (i, 0)), HBM_SPEC],
        out_shape=[jax.ShapeDtypeStruct((S, D), F32), jax.ShapeDtypeStruct((N_CHIPS, W_OUT_ROWS, D), BF16)],
        scratch_shapes=[pltpu.SemaphoreType.DMA((N_CHIPS,)), pltpu.SemaphoreType.DMA((N_CHIPS,))],
        compiler_params=_params(("arbitrary",)),
    )(dy, wo, dwo)


def _matmul_tn(a, b, tm, name, mat_grads=None):
    K, M = a.shape
    N = b.shape[1]
    tk = _tile(K, 2048)
    ni, nk = M // tm, K // tk
    hosting = mat_grads is not None

    def body(a_ref, b_ref, *rest):
        if hosting:
            g_in, (o_ref, tot_ref, acc_ref, mbuf, send_sems, recv_sems) = rest[:len(MAT)], rest[len(MAT):]
        else:
            o_ref, acc_ref = rest
        i, k = pl.program_id(0), pl.program_id(1)

        if hosting:
            gather = _SlotGather(mbuf, send_sems, recv_sems)

            @pl.when((i == 0) & (k == 0))
            def _():
                r0 = 0
                for q, (_, rows) in enumerate(MAT):
                    mbuf[gather.my_id, r0:r0 + rows, :] = g_in[q][...]
                    r0 += rows
                gather.start()

            pl.when((i == ni // 2) & (k == 0))(gather.pass_on)

        @pl.when(k == 0)
        def _():
            acc_ref[...] = jnp.zeros_like(acc_ref)

        acc_ref[...] += _dot(a_ref[...], b_ref[...], TN)

        @pl.when(k == nk - 1)
        def _():
            o_ref[...] = acc_ref[...].astype(BF16)

        if hosting:
            @pl.when((i == ni - 1) & (k == nk - 1))
            def _():
                gather.finish()
                tot_ref[...] = gather.total()

    whole = pl.BlockSpec(memory_space=pltpu.VMEM)
    in_specs = [pl.BlockSpec((tk, tm), lambda i, k: (k, i)), pl.BlockSpec((tk, N), lambda i, k: (k, 0))]
    out_specs = [pl.BlockSpec((tm, N), lambda i, k: (i, 0))]
    out_shape = [jax.ShapeDtypeStruct((M, N), BF16)]
    scratch = [pltpu.VMEM((tm, N), F32)]
    if hosting:
        in_specs += [whole] * len(MAT)
        out_specs.append(whole)
        out_shape.append(jax.ShapeDtypeStruct((MAT_ROWS, 128), F32))
        scratch += [pltpu.VMEM((N_DEV, MAT_ROWS, 128), F32), pltpu.SemaphoreType.DMA((7,)),
                    pltpu.SemaphoreType.DMA((7,))]
    res = pl.pallas_call(
        body, name=name, grid=(ni, nk), in_specs=in_specs, out_specs=out_specs, out_shape=out_shape,
        scratch_shapes=scratch,
        compiler_params=_params(("arbitrary", "arbitrary") if hosting else ("parallel", "arbitrary")),
    )(a, b, *(mat_grads or ()))
    return res if hosting else res[0]


def _dh_prenorm_bwd(dproj, wt, x, dout, g_pre, pt):
    S = x.shape[0]
    tm, tk = _tile(S, 1024), 768
    ne = 4
    te = tm // ne
    ni, nk = S // tm, D_IN // tk
    n_mm = ni * nk
    n_steps = n_mm + ne
    mm_of = lambda t: jnp.minimum(t, n_mm - 1)

    def quarter_of(t):
        j, q = t // nk - 1, t % nk
        nxt = jnp.where(q < ne, j * ne + q, (j + 1) * ne)
        return jnp.clip(nxt, 0, ni * ne - 1)

    def body(dp_ref, w_ref, x_ref, dout_ref, g_ref, pt_ref, gx_ref, dg_ref, bt_ref, acc_ref,
             send_sems, recv_sems, local_sem):
        t = pl.program_id(0)
        exchange = _OwnerExchange(pt_ref, bt_ref, send_sems, recv_sems, local_sem)
        pl.when(t == 0)(exchange.start)

        @pl.when(t == 0)
        def _():
            dg_ref[...] = jnp.zeros_like(dg_ref)

        has_mm = t < n_mm
        has_q = (t >= nk) & (t % nk < ne)

        def matmul():
            slot = (mm_of(t) // nk) % 2
            old = jnp.where(mm_of(t) % nk > 0, acc_ref[slot], 0.0)
            acc_ref[slot] = old + _dot(dp_ref[...], w_ref[...], NN)

        def quarter():
            slot = (t // nk - 1) % 2
            dh = acc_ref[slot, pl.ds(pl.multiple_of((t % nk) * te, te), te), :]
            xv = x_ref[...]
            r = lax.rsqrt(_row_mean(xv * xv) + EPS)
            xh = xv * r
            dg_ref[...] += jnp.sum(dh * xh, axis=0, keepdims=True)
            dxh = dh * g_ref[...]
            gx_ref[...] = dout_ref[...] + r * (dxh - xh * _row_mean(dxh * xh))

        @pl.when(has_mm & has_q)
        def _():
            matmul()
            quarter()

        pl.when(has_mm & jnp.logical_not(has_q))(matmul)
        pl.when(jnp.logical_not(has_mm) & has_q)(quarter)
        pl.when(t == n_steps - 1)(exchange.finish)

    quarter_block = lambda t: (quarter_of(t), 0)
    return pl.pallas_call(
        body, name="dh_prenorm_bwd", grid=(n_steps,),
        in_specs=[pl.BlockSpec((tm, tk), lambda t: (mm_of(t) // nk, mm_of(t) % nk)),
                  pl.BlockSpec((tk, D), lambda t: (mm_of(t) % nk, 0)),
                  pl.BlockSpec((te, D), quarter_block), pl.BlockSpec((te, D), quarter_block),
                  pl.BlockSpec((1, D), lambda t: (0, 0)), HBM_SPEC],
        out_specs=[pl.BlockSpec((te, D), quarter_block), pl.BlockSpec((1, D), lambda t: (0, 0)), HBM_SPEC],
        out_shape=[jax.ShapeDtypeStruct((S, D), F32), jax.ShapeDtypeStruct((1, D), F32),
                   jax.ShapeDtypeStruct(pt.shape, BF16)],
        scratch_shapes=[pltpu.VMEM((2, tm, D), F32)] + OWNER_SEMS,
        compiler_params=_params(("arbitrary",)),
    )(dproj, wt, x, dout, g_pre, pt)


def _presum(c_arr, own, recv, half_rows):
    n_cols = own.shape[-1]
    own4 = own.reshape(N_CHIPS, 2, half_rows, n_cols)

    def body(c_ref, own_ref, recv_ref, o_ref):
        o_ref[...] = (own_ref[...].astype(F32) + recv_ref[...].astype(F32)).astype(BF16)

    return pl.pallas_call(
        body, name="presum_%d" % half_rows,
        grid_spec=pltpu.PrefetchScalarGridSpec(
            num_scalar_prefetch=1, grid=(N_CHIPS,),
            in_specs=[pl.BlockSpec((None, None, half_rows, n_cols), lambda j, c: (j, c[0], 0, 0)),
                      pl.BlockSpec((None, half_rows, n_cols), lambda j, c: (j, 0, 0))],
            out_specs=pl.BlockSpec((None, half_rows, n_cols), lambda j, c: (j, 0, 0))),
        out_shape=jax.ShapeDtypeStruct((N_CHIPS, half_rows, n_cols), BF16),
        compiler_params=_params(("parallel",)),
    )(c_arr, own4, recv)


def _sibling_presum(c_arr, dw, half_rows, name):
    n_cols = dw.shape[-1]
    dw4 = dw.reshape(N_CHIPS, 2, half_rows, n_cols)

    def body(c_ref, own_ref, dw_ref, o_ref, r_ref, stage, send_sems, recv_sems, load_sem):
        j = pl.program_id(0)
        copies = _sibling_copies(dw_ref, r_ref, half_rows, send_sems, recv_sems)

        @pl.when(j == 0)
        def _():
            for cp in copies:
                cp.start()

        for q in range(N_CHIPS):
            pl.when(j == q)(copies[q].wait_recv)
        load = pltpu.make_async_copy(r_ref.at[j], stage, load_sem)
        load.start()
        load.wait()
        o_ref[...] = (own_ref[...].astype(F32) + stage[...].astype(F32)).astype(BF16)

        @pl.when(j == N_CHIPS - 1)
        def _():
            for cp in copies:
                cp.wait_send()

    out, _ = pl.pallas_call(
        body, name=name,
        grid_spec=pltpu.PrefetchScalarGridSpec(
            num_scalar_prefetch=1, grid=(N_CHIPS,),
            in_specs=[pl.BlockSpec((None, None, half_rows, n_cols), lambda j, c: (j, c[0], 0, 0)), HBM_SPEC],
            out_specs=[pl.BlockSpec((None, half_rows, n_cols), lambda j, c: (j, 0, 0)), HBM_SPEC],
            scratch_shapes=[pltpu.VMEM((half_rows, n_cols), BF16), pltpu.SemaphoreType.DMA((N_CHIPS,)),
                            pltpu.SemaphoreType.DMA((N_CHIPS,)), pltpu.SemaphoreType.DMA(())]),
        out_shape=[jax.ShapeDtypeStruct((N_CHIPS, half_rows, n_cols), BF16),
                   jax.ShapeDtypeStruct((N_CHIPS, half_rows, n_cols), BF16)],
        compiler_params=_params(("arbitrary",)),
    )(c_arr, dw4, dw)
    return out


def _sum_chips(c_arr, parts, name):
    _, rows, n_cols = parts.shape
    nt = 2
    tr = rows // nt

    def body(c_ref, p_ref, o_ref):
        o_ref[...] = ((p_ref[0].astype(F32) + p_ref[1].astype(F32)) + p_ref[2].astype(F32)) + p_ref[3].astype(F32)

    return pl.pallas_call(
        body, name=name,
        grid_spec=pltpu.PrefetchScalarGridSpec(
            num_scalar_prefetch=1, grid=(nt,),
            in_specs=[pl.BlockSpec((N_CHIPS, tr, n_cols), lambda i, c: (0, i, 0))],
            out_specs=pl.BlockSpec((tr, n_cols), lambda i, c: (c[0] * nt + i, 0))),
        out_shape=jax.ShapeDtypeStruct((2 * rows, n_cols), F32),
        compiler_params=_params(("parallel",)),
    )(c_arr, parts)


def _adamw_math(w, g, m, v):
    mn = ADAM_B1 * m + (1.0 - ADAM_B1) * g
    vn = ADAM_B2 * v + (1.0 - ADAM_B2) * (g * g)
    m_hat = mn / (1.0 - ADAM_B1 ** ADAM_STEP)
    v_hat = vn / (1.0 - ADAM_B2 ** ADAM_STEP)
    return -ADAM_LR * (m_hat / (jnp.sqrt(v_hat) + ADAM_EPS) + ADAM_WD * w), mn, vn


def _adamw(w, g, m, v, name):
    R, C = w.shape
    tr = next((t for t in (256, 192, 128) if R % t == 0), R)

    def body(w_ref, g_ref, m_ref, v_ref, go_ref, d_ref, mo_ref, vo_ref):
        gv = g_ref[...]
        go_ref[...] = gv
        d_ref[...], mo_ref[...], vo_ref[...] = _adamw_math(w_ref[...], gv, m_ref[...], v_ref[...])

    spec = pl.BlockSpec((tr, C), lambda i: (i, 0))
    shp = jax.ShapeDtypeStruct((R, C), F32)
    return pl.pallas_call(
        body, name=name, grid=(R // tr,), in_specs=[spec] * 4, out_specs=[spec] * 4, out_shape=[shp] * 4,
        compiler_params=_params(("parallel",)),
    )(w, g, m, v)


HBM_SPEC = pl.BlockSpec(memory_space=pltpu.HBM)
GATHER_LOCAL_CHUNKS = 4
GATHER_SEMS = [pltpu.SemaphoreType.DMA((7,)), pltpu.SemaphoreType.DMA((7,)),
               pltpu.SemaphoreType.DMA((GATHER_LOCAL_CHUNKS,))]
OWNER_SEMS = [pltpu.SemaphoreType.DMA((3,)), pltpu.SemaphoreType.DMA((3,)), pltpu.SemaphoreType.DMA(())]


def _mesh_pos():
    return lax.axis_index("x"), lax.axis_index("y"), lax.axis_index("c")


class _RowGather:
    def __init__(self, src_ref, full_ref, rows, send_sems, recv_sems, local_sems):
        self.src, self.full, self.rows = src_ref, full_ref, rows
        self.send, self.recv, self.local = send_sems, recv_sems, local_sems
        x, y, c = _mesh_pos()
        self.c, self.me, self.sibling = c, (x, y, c), (x, y, 1 - c)
        self.chips = [(1 - x, y), (x, 1 - y), (1 - x, 1 - y)]

    def _block(self, pos):
        px, py, pc = pos
        return self.full.at[pl.ds(pl.multiple_of((4 * px + 2 * py + pc) * self.rows, 16), self.rows), :]

    def _copy(self, k, blk, to):
        return pltpu.make_async_remote_copy(
            src_ref=self.src if blk is self.me else self._block(blk), dst_ref=self._block(blk),
            send_sem=self.send.at[k], recv_sem=self.recv.at[k], device_id=to, device_id_type=MESH)

    def _mine(self):
        return _place_locally(self.src, self._block(self.me), self.local, GATHER_LOCAL_CHUNKS)

    def _first(self):
        return [self._copy(0, self.me, self.sibling)] + [
            self._copy(1 + j, self.me, (*chip, self.c)) for j, chip in enumerate(self.chips)]

    def start(self):
        for cp in self._first() + self._mine():
            cp.start()

    def _passed(self):
        return [self._copy(4 + j, (*chip, self.c), self.sibling) for j, chip in enumerate(self.chips)]

    def pass_on(self):
        for j, chip in enumerate(self.chips):
            self._copy(1 + j, (*chip, self.c), self.me).wait_recv()
            self._passed()[j].start()

    def finish(self):
        self._copy(0, self.sibling, self.me).wait_recv()
        for j, chip in enumerate(self.chips):
            self._copy(4 + j, (*chip, 1 - self.c), self.me).wait_recv()
        for cp in self._first() + self._passed():
            cp.wait_send()
        for cp in self._mine():
            cp.wait()


class _OwnerExchange:
    def __init__(self, src_ref, dst_ref, send_sems, recv_sems, local_sem):
        self.src, self.dst, self.send, self.recv, self.local = src_ref, dst_ref, send_sems, recv_sems, local_sem
        x, y, c = _mesh_pos()
        self.c, self.my_chip = c, 2 * x + y
        self.peers = [(1 - x, y), (x, 1 - y), (1 - x, 1 - y)]

    def _copies(self):
        local = pltpu.make_async_copy(self.src.at[self.my_chip], self.dst.at[self.my_chip], self.local)
        remote = [pltpu.make_async_remote_copy(
            src_ref=self.src.at[2 * px + py], dst_ref=self.dst.at[self.my_chip],
            send_sem=self.send.at[k], recv_sem=self.recv.at[k], device_id=(px, py, self.c), device_id_type=MESH)
            for k, (px, py) in enumerate(self.peers)]
        return local, remote

    def start(self):
        local, remote = self._copies()
        local.start()
        for cp in remote:
            cp.start()

    def finish(self):
        local, remote = self._copies()
        for cp in remote:
            cp.wait_recv()
        for cp in remote:
            cp.wait_send()
        local.wait()


def _sibling_copies(dw_ref, r_ref, rows, send_sems, recv_sems):
    x, y, c = _mesh_pos()
    return [pltpu.make_async_remote_copy(
        src_ref=dw_ref.at[pl.ds(pl.multiple_of((2 * j + (1 - c)) * rows, 16), rows), :], dst_ref=r_ref.at[j],
        send_sem=send_sems.at[j], recv_sem=recv_sems.at[j], device_id=(x, y, 1 - c), device_id_type=MESH)
        for j in range(N_CHIPS)]


PAIR_CHUNKS = 4


def _pair_halves(gt, go, vec_grads, loss_part):
    n_vec = len(VEC)

    def body(*refs):
        g_in, loss_in = refs[2:2 + n_vec], refs[2 + n_vec]
        gt_ref, go_ref, tot_v, vbuf, send_sems, recv_sems, vsend_sems, vrecv_sems = refs[3 + n_vec:]
        x, y, c = _mesh_pos()
        gather = _SlotGather(vbuf, vsend_sems, vrecv_sems)
        vbuf[gather.my_id] = jnp.zeros((VEC_ROWS, D), F32)
        for r, (_, n) in enumerate(VEC):
            vbuf[gather.my_id, r:r + 1, 0:n] = g_in[r][...]
        vbuf[gather.my_id, LOSS_ROW:LOSS_ROW + 1, 0:128] = loss_in[...]
        gather.start()
        copies = []
        for a, (ref, rows) in enumerate(((gt_ref, W_IN_ROWS), (go_ref, W_OUT_ROWS))):
            ch = rows // PAIR_CHUNKS
            for q in range(PAIR_CHUNKS):
                part = ref.at[pl.ds(pl.multiple_of(c * rows + q * ch, 8), ch), :]
                copies.append(pltpu.make_async_remote_copy(
                    src_ref=part, dst_ref=part, send_sem=send_sems.at[PAIR_CHUNKS * a + q],
                    recv_sem=recv_sems.at[PAIR_CHUNKS * a + q], device_id=(x, y, 1 - c), device_id_type=MESH))
        for cp in copies:
            cp.start()
        gather.pass_on()
        gather.finish()
        tot_v[...] = gather.total()
        for cp in copies:
            cp.wait_recv()
        for cp in copies:
            cp.wait_send()

    vmem = pl.BlockSpec(memory_space=pltpu.VMEM)
    return pl.pallas_call(
        body, name="pair_halves",
        in_specs=[HBM_SPEC, HBM_SPEC] + [vmem] * (n_vec + 1), out_specs=[HBM_SPEC, HBM_SPEC, vmem],
        out_shape=[jax.ShapeDtypeStruct(gt.shape, F32), jax.ShapeDtypeStruct(go.shape, F32),
                   jax.ShapeDtypeStruct((VEC_ROWS, D), F32)],
        input_output_aliases={0: 0, 1: 1},
        scratch_shapes=[pltpu.VMEM((N_DEV, VEC_ROWS, D), F32),
                        pltpu.SemaphoreType.DMA((2 * PAIR_CHUNKS,)), pltpu.SemaphoreType.DMA((2 * PAIR_CHUNKS,)),
                        pltpu.SemaphoreType.DMA((7,)), pltpu.SemaphoreType.DMA((7,))],
    )(gt, go, *vec_grads, loss_part)


VEC = (("g_pre", 2048), ("g_post", 2048), ("b_qkv", 1280), ("ln_v_g", 1024), ("ln_v_b", 1024), ("attn_sinks", 16))
VEC_ROWS = 8
LOSS_ROW = len(VEC)
MAT = (("w_spatial", NG * T), ("b_spatial", NG))
MAT_ROWS = sum(r for _, r in MAT)


class _SlotGather:
    def __init__(self, buf, send_sems, recv_sems):
        self.buf, self.send, self.recv = buf, send_sems, recv_sems
        x, y, c = _mesh_pos()
        self.c, self.me, self.sibling, self.my_id = c, (x, y, c), (x, y, 1 - c), 4 * x + 2 * y + c
        self.chips = [(1 - x, y), (x, 1 - y), (1 - x, 1 - y)]

    def _copy(self, k, blk, to):
        px, py, pc = blk
        slot = self.buf.at[4 * px + 2 * py + pc]
        return pltpu.make_async_remote_copy(
            src_ref=slot, dst_ref=slot, send_sem=self.send.at[k], recv_sem=self.recv.at[k],
            device_id=to, device_id_type=MESH)

    def _first(self):
        return [self._copy(0, self.me, self.sibling)] + [
            self._copy(1 + j, self.me, (*chip, self.c)) for j, chip in enumerate(self.chips)]

    def start(self):
        for cp in self._first():
            cp.start()

    def _passed(self):
        return [self._copy(4 + j, (*chip, self.c), self.sibling) for j, chip in enumerate(self.chips)]

    def pass_on(self):
        for j, chip in enumerate(self.chips):
            self._copy(1 + j, (*chip, self.c), self.me).wait_recv()
            self._passed()[j].start()

    def finish(self):
        self._copy(0, self.sibling, self.me).wait_recv()
        for j, chip in enumerate(self.chips):
            self._copy(4 + j, (*chip, 1 - self.c), self.me).wait_recv()
        for cp in self._first() + self._passed():
            cp.wait_send()

    def total(self):
        t = self.buf[0]
        for d in range(1, N_DEV):
            t = t + self.buf[d]
        return t


def _small_update(vec_total, mat_total, vec_state, mat_state):
    n_vec, n_mat = len(VEC), len(MAT)
    n_par = n_vec + n_mat
    n_in = 2 + 3 * n_par

    def body(*refs):
        tot_v, tot_m = refs[0], refs[1]
        st_in = refs[2:n_in]
        outs, loss_out = refs[n_in:n_in + 4 * n_par], refs[n_in + 4 * n_par]
        loss_out[...] = tot_v[LOSS_ROW:LOSS_ROW + 1, 0:128]
        r0 = 0
        for q in range(n_par):
            if q < n_vec:
                g = tot_v[q:q + 1, 0:VEC[q][1]]
            else:
                rows = MAT[q - n_vec][1]
                g = tot_m[r0:r0 + rows, :]
                r0 += rows
            w, m, v = (st_in[3 * q + t][...] for t in range(3))
            outs[4 * q][...] = g
            outs[4 * q + 1][...], outs[4 * q + 2][...], outs[4 * q + 3][...] = _adamw_math(w, g, m, v)

    state = [a for wmv in list(vec_state) + list(mat_state) for a in wmv]
    vmem = pl.BlockSpec(memory_space=pltpu.VMEM)
    out_shape = [jax.ShapeDtypeStruct(wmv[0].shape, F32) for wmv in list(vec_state) + list(mat_state) for _ in range(4)]
    out_shape.append(jax.ShapeDtypeStruct((1, 128), F32))
    res = pl.pallas_call(
        body, name="small_update",
        in_specs=[vmem] * n_in, out_specs=[vmem] * len(out_shape), out_shape=out_shape,
        compiler_params=pltpu.CompilerParams(vmem_limit_bytes=VMEM_LIMIT),
    )(vec_total, mat_total, *state)
    return [res[4 * q:4 * q + 4] for q in range(n_par)], res[-1]


def kernel(x, positions, g_pre, w_in, b_qkv, ln_v_g, ln_v_b, w_spatial, b_spatial, attn_sinks, w_out, g_post, loss_target, m_g_pre, m_w_in, m_b_qkv, m_ln_v_g, m_ln_v_b, m_w_spatial, m_b_spatial, m_attn_sinks, m_w_out, m_g_post, v_g_pre, v_w_in, v_b_qkv, v_ln_v_g, v_ln_v_b, v_w_spatial, v_b_spatial, v_attn_sinks, v_w_out, v_g_post):
    S = x.shape[1]
    c = lax.axis_index("c")
    c_arr = jnp.reshape(c, (1,)).astype(jnp.int32)
    x2 = x[0]
    target = loss_target[0]
    pos = positions.reshape(S, 1)
    half = HD // 2
    inv_freq = ROPE_THETA ** (-jnp.arange(half, dtype=F32) * (2.0 / HD))
    invf = jnp.tile(inv_freq, 128 // half).reshape(1, 128)
    bias = jnp.concatenate([jnp.zeros((OFF_Q,), F32), b_qkv[0], jnp.zeros((D_IN - OFF_ZB,), F32)]).reshape(1, D_IN)
    b_s_col = b_spatial[0].reshape(NG, T, 1)
    sinks = jnp.repeat(attn_sinks[0], T).reshape(NQ * T, 1)

    chip = 2 * lax.axis_index("x") + lax.axis_index("y")
    wt_part = lax.dynamic_slice_in_dim(w_in[0].T.astype(BF16), c * W_IN_ROWS, W_IN_ROWS, axis=0)
    wo_part = lax.dynamic_slice_in_dim(w_out[0].astype(BF16), c * W_OUT_ROWS, W_OUT_ROWS, axis=0)
    sched = jnp.asarray(PROJ_SCHEDULE, jnp.int32)[chip]

    h, proj, wt, cos, sin = _prenorm_inproj(sched, x2, g_pre, bias, wt_part, pos, invf)
    ycat, wo, q_roped = _mid_fwd(proj, cos, sin, ln_v_g, ln_v_b, w_spatial[0], b_s_col, sinks, wo_part)
    dy, dout, loss_part, dg_post = _outproj_loss(ycat, wo, x2, target, g_post)

    dwo = _matmul_tn(ycat, dy, 1024, "dw_out")
    dycat, ro = _dycat(dy, wo, dwo)
    po = _presum(c_arr, dwo, ro, W_OUT_ROWS)
    dproj, dln_g, dln_b, dws, dbs, dsink, dbqkv, bo = _mid_bwd(
        proj, dycat, q_roped, cos, sin, ln_v_g, ln_v_b, w_spatial[0], b_s_col, sinks, po)
    dwt, mat_total = _matmul_tn(dproj, h, 768, "dw_in_t",
                                mat_grads=[dws.reshape(NG * T, T), dbs.reshape(NG, T)])
    pt = _sibling_presum(c_arr, dwt, W_IN_ROWS, "sibling_presum_in")
    grad_x, dg_pre, bt = _dh_prenorm_bwd(dproj, wt, x2, dout, g_pre, pt)
    grads = {"g_pre": dg_pre, "g_post": dg_post, "b_qkv": dbqkv, "ln_v_g": dln_g, "ln_v_b": dln_b,
             "attn_sinks": dsink[:, 0].reshape(1, NQ)}
    gt, go, vec_total = _pair_halves(_sum_chips(c_arr, bt, "sum_chips_in"), _sum_chips(c_arr, bo, "sum_chips_out"),
                                     [grads[n] for n, _ in VEC], loss_part)

    g_w_in, d_w_in, nm_w_in, nv_w_in = (a.T for a in _adamw(w_in[0].T, gt, m_w_in[0].T, v_w_in[0].T, "adamw_w_in"))
    g_w_out, d_w_out, nm_w_out, nv_w_out = _adamw(w_out[0], go, m_w_out[0], v_w_out[0], "adamw_w_out")

    state = {"g_pre": (g_pre, m_g_pre, v_g_pre), "g_post": (g_post, m_g_post, v_g_post),
             "b_qkv": (b_qkv, m_b_qkv, v_b_qkv), "ln_v_g": (ln_v_g, m_ln_v_g, v_ln_v_g),
             "ln_v_b": (ln_v_b, m_ln_v_b, v_ln_v_b), "attn_sinks": (attn_sinks, m_attn_sinks, v_attn_sinks),
             "w_spatial": tuple(a.reshape(NG * T, T) for a in (w_spatial, m_w_spatial, v_w_spatial)),
             "b_spatial": tuple(a.reshape(NG, T) for a in (b_spatial, m_b_spatial, v_b_spatial))}
    results, loss = _small_update(vec_total, mat_total, [state[n] for n, _ in VEC], [state[n] for n, _ in MAT])
    small = {n: [a.reshape(w.shape) for a in res]
             for (n, _), res, w in zip(VEC + MAT, results, [state[n][0] for n, _ in VEC + MAT])}
    small["w_spatial"] = [a.reshape(w_spatial.shape) for a in small["w_spatial"]]
    small["b_spatial"] = [a.reshape(b_spatial.shape) for a in small["b_spatial"]]
    big = {"w_in": [a[None] for a in (g_w_in, d_w_in, nm_w_in, nv_w_in)],
           "w_out": [a[None] for a in (g_w_out, d_w_out, nm_w_out, nv_w_out)]}
    order = ("g_pre", "w_in", "b_qkv", "ln_v_g", "ln_v_b", "w_spatial", "b_spatial", "attn_sinks", "w_out", "g_post")
    leaves = {**small, **big}
    return (loss[0, 0], grad_x[None], *[leaves[n][t] for t in range(4) for n in order])
```

```python
import jax
import jax.numpy as jnp
from jax import lax
from jax.experimental import pallas as pl
from jax.experimental.pallas import tpu as pltpu

F32 = jnp.float32
BF16 = jnp.bfloat16
MESH = pl.DeviceIdType.MESH

D = 2048
DG = 1024
T = 128
NG = 8
HD = 64
NQ = 16
D_IN = 5376
OFF_U, OFF_V, OFF_ZA, OFF_Q, OFF_K, OFF_VA, OFF_ZB = 0, 1024, 2048, 3072, 4096, 4224, 4352
D_QKV = 1280
EPS = 1e-6
ROPE_THETA = 10000.0
N_CHIPS = 4
N_DEV = 8
W_IN_ROWS = D_IN // N_DEV
W_OUT_ROWS = D // N_DEV

ADAM_LR, ADAM_B1, ADAM_B2, ADAM_EPS, ADAM_WD, ADAM_STEP = 0.001, 0.9, 0.999, 1e-08, 0.01, 10

VMEM_LIMIT = 56 * 1024 * 1024


def _tile(n, pref):
    return pref if n % pref == 0 else n


def _params(sem=None, vmem=VMEM_LIMIT):
    return pltpu.CompilerParams(dimension_semantics=sem, vmem_limit_bytes=vmem)


def _sigmoid(z):
    return 1.0 / (1.0 + jnp.exp(-z))


def _row_mean(v):
    return jnp.mean(v, axis=-1, keepdims=True)


def _dot(a, b, dims):
    return lax.dot_general(a, b, (dims, ((), ())), preferred_element_type=F32)


NN = ((1,), (0,))
NT = ((1,), (1,))
TN = ((0,), (0,))


PROJ_TN = 768
N_PROJ_TILES = D_IN // PROJ_TN
PROJ_SCHEDULE = ((0, 1, 2, 4, 3, 6, 5, 5), (2, 1, 0, 6, 4, 3, 5, 4), (4, 5, 6, 0, 2, 1, 3, 4), (6, 5, 4, 2, 3, 0, 1, 5))


def _place_locally(src_ref, dst_rows_ref, sems, n_chunks):
    ch = src_ref.shape[0] // n_chunks
    return [pltpu.make_async_copy(src_ref.at[pl.ds(q * ch, ch), :], dst_rows_ref.at[pl.ds(q * ch, ch), :],
                                  sems.at[q]) for q in range(n_chunks)]


def _prenorm_inproj(sched, x, g, bias, wt_part, pos, invf):
    S = x.shape[0]
    tp, tm = _tile(S, 512), _tile(S, 1024)
    n_pre, ns = S // tp, S // tm
    n_steps = n_pre + N_PROJ_TILES * ns
    pos_of = lambda i: jnp.maximum(i - n_pre, 0) // ns
    row_of = lambda i: jnp.maximum(i - n_pre, 0) % ns

    def body(sched_ref, x_ref, g_ref, b_ref, wpart_ref, pos_ref, invf_ref, h_ref, proj_ref, wt_ref, cos_ref, sin_ref,
             h_all, w_tile, stage, send_sems, recv_sems, w_sems, local_sems):
        i = pl.program_id(0)
        x_, y_, c = _mesh_pos()
        me, sibling = (x_, y_, c), (x_, y_, 1 - c)
        chips = [(1 - x_, y_), (x_, 1 - y_), (1 - x_, 1 - y_)]

        def block(pos):
            px, py, pc = pos
            return wt_ref.at[pl.ds(pl.multiple_of((4 * px + 2 * py + pc) * W_IN_ROWS, 16), W_IN_ROWS), :]

        def copy(k, blk, to):
            return pltpu.make_async_remote_copy(
                src_ref=wpart_ref if blk is me else block(blk), dst_ref=block(blk),
                send_sem=send_sems.at[k], recv_sem=recv_sems.at[k], device_id=to, device_id_type=MESH)

        stage_in = pltpu.make_async_copy(wpart_ref, stage, local_sems.at[0])
        stage_out = pltpu.make_async_copy(stage, block(me), local_sems.at[1])

        relay = (jnp.where(c == 0, x_, 1 - x_), jnp.where(c == 0, 1 - y_, y_))
        relayed = (jnp.where(c == 0, 1 - x_, x_), jnp.where(c == 0, y_, 1 - y_))

        early = c == y_
        y_send = copy(2, me, (*chips[1], c))

        def own_sends():
            return [copy(0, me, sibling), y_send, copy(1, me, (*chips[0], c))]

        def passed_on():
            return [copy(4, (*chips[0], c), sibling), copy(5, (*chips[1], c), sibling),
                    copy(3, (*relayed, c), (*relay, c))]

        def early_block_arrives():
            @pl.when(early)
            def _():
                copy(2, (*chips[1], c), me).wait_recv()
                copy(5, (*chips[1], c), sibling).start()
                y_send.start()

            pl.when(jnp.logical_not(early))(lambda: copy(5, (*chips[1], 1 - c), me).wait_recv())

        def neighbours_arrive():
            copy(1, (*chips[0], c), me).wait_recv()
            copy(4, (*chips[0], c), sibling).start()

            @pl.when(early)
            def _():
                copy(3, (*relayed, c), (*relay, c)).start()
                copy(5, (*chips[1], 1 - c), me).wait_recv()

            @pl.when(jnp.logical_not(early))
            def _():
                copy(2, (*chips[1], c), me).wait_recv()
                copy(5, (*chips[1], c), sibling).start()
                copy(3, (*relayed, c), (*relay, c)).start()

        def diagonal_arrives():
            copy(3, (*chips[2], c), me).wait_recv()
            copy(6, (*chips[2], c), sibling).start()
            copy(6, (*chips[2], 1 - c), me).wait_recv()

        def tile_load(p):
            slot = p % 2
            rows = wt_ref.at[pl.ds(pl.multiple_of(sched_ref[p] * PROJ_TN, 16), PROJ_TN), :]
            return pltpu.make_async_copy(rows, w_tile.at[slot], w_sems.at[slot])

        def prepare(p):
            p = jnp.asarray(p, jnp.int32)

            @pl.when(p == 0)
            def _():
                copy(0, sibling, me).wait_recv()
                stage_out.wait()

            pl.when(p == 1)(early_block_arrives)
            pl.when(p == 2)(neighbours_arrive)
            pl.when(p == 3)(lambda: copy(4, (*chips[0], 1 - c), me).wait_recv())
            pl.when(p == sched_ref[N_PROJ_TILES])(diagonal_arrives)
            tile_load(p).start()

        @pl.when(i == 0)
        def _():
            stage_in.start()
            copy(0, me, sibling).start()
            copy(1, me, (*chips[0], c)).start()
            pl.when(jnp.logical_not(early))(y_send.start)
            stage_in.wait()
            stage_out.start()

        @pl.when(i < n_pre)
        def _():
            xv = x_ref[...]
            r = lax.rsqrt(_row_mean(xv * xv) + EPS)
            hv = (xv * r * g_ref[...]).astype(BF16)
            h_ref[...] = hv
            h_all[pl.ds(pl.multiple_of(i * tp, tp), tp), :] = hv
            ang = pos_ref[...].astype(F32) * invf_ref[...]
            cos_ref[...] = jnp.cos(ang)
            sin_ref[...] = jnp.sin(ang)

        pl.when(i == n_pre - 1)(lambda: prepare(0))

        @pl.when(i >= n_pre)
        def _():
            p, s = pos_of(i), row_of(i)
            pl.when(s == 0)(lambda: tile_load(p).wait())
            pl.when((s == ns - 1) & (p < N_PROJ_TILES - 1))(lambda: prepare(p + 1))
            hv = h_all[pl.ds(pl.multiple_of(s * tm, tm), tm), :]
            proj_ref[...] = _dot(hv, w_tile[p % 2], NT) + b_ref[...]

        @pl.when(i == n_steps - 1)
        def _():
            for cp in own_sends() + passed_on() + [copy(6, (*chips[2], c), sibling)]:
                cp.wait_send()

    return pl.pallas_call(
        body, name="prenorm_inproj",
        grid_spec=pltpu.PrefetchScalarGridSpec(
            num_scalar_prefetch=1, grid=(n_steps,),
            in_specs=[pl.BlockSpec((tp, D), lambda i, sc: (jnp.minimum(i, n_pre - 1), 0)),
                      pl.BlockSpec((1, D), lambda i, sc: (0, 0)),
                      pl.BlockSpec((1, PROJ_TN), lambda i, sc: (0, sc[pos_of(i)])),
                      HBM_SPEC,
                      pl.BlockSpec((tp, 1), lambda i, sc: (jnp.minimum(i, n_pre - 1), 0)),
                      pl.BlockSpec((1, 128), lambda i, sc: (0, 0))],
            out_specs=[pl.BlockSpec((tp, D), lambda i, sc: (jnp.minimum(i, n_pre - 1), 0)),
                       pl.BlockSpec((tm, PROJ_TN), lambda i, sc: (row_of(i), sc[pos_of(i)])),
                       HBM_SPEC,
                       pl.BlockSpec((tp, 128), lambda i, sc: (jnp.minimum(i, n_pre - 1), 0)),
                       pl.BlockSpec((tp, 128), lambda i, sc: (jnp.minimum(i, n_pre - 1), 0))],
            scratch_shapes=[pltpu.VMEM((S, D), BF16), pltpu.VMEM((2, PROJ_TN, D), BF16),
                            pltpu.VMEM((W_IN_ROWS, D), BF16),
                            pltpu.SemaphoreType.DMA((7,)), pltpu.SemaphoreType.DMA((7,)),
                            pltpu.SemaphoreType.DMA((2,)), pltpu.SemaphoreType.DMA((2,))]),
        out_shape=[jax.ShapeDtypeStruct((S, D), BF16), jax.ShapeDtypeStruct((S, D_IN), F32),
                   jax.ShapeDtypeStruct((D_IN, D), BF16),
                   jax.ShapeDtypeStruct((S, 128), F32), jax.ShapeDtypeStruct((S, 128), F32)],
        compiler_params=_params(("arbitrary",)),
    )(sched, x, g, bias, wt_part, pos, invf)


def _rot_half(xs, first_half):
    return jnp.where(first_half, -pltpu.roll(xs, 96, 1), pltpu.roll(xs, 32, 1))


GH = NQ // 2


MASKED = -1e30


def _attn_consts():
    lane = lax.broadcasted_iota(jnp.int32, (T, 128), 1)
    row = lax.broadcasted_iota(jnp.int32, (GH * T, T), 0) & (T - 1)
    on_diag_or_below = row >= lax.broadcasted_iota(jnp.int32, (GH * T, T), 1)
    return (lane & (HD - 1)) < (HD // 2), lane < HD, on_diag_or_below


def _stack_heads(slab_fn, grp, lo64):
    blocks = []
    for jj in range(GH // 2):
        s = slab_fn(GH // 2 * grp + jj)
        blocks += [jnp.where(lo64, s, 0.0), jnp.where(lo64, 0.0, s)]
    return jnp.concatenate(blocks, axis=0)


def _unstack_heads(stacked, jj, lo64):
    return jnp.where(lo64, stacked[(2 * jj) * T:(2 * jj + 1) * T], stacked[(2 * jj + 1) * T:(2 * jj + 2) * T])


def _both_halves(kv, grp):
    lo = lax.broadcasted_iota(jnp.int32, kv.shape, 1) < HD
    swapped = pltpu.roll(kv, HD, 1)
    return jnp.where(lo, kv, swapped) if grp == 0 else jnp.where(lo, swapped, kv)


def _one_half(acc, grp):
    lo = lax.broadcasted_iota(jnp.int32, acc.shape, 1) < HD
    return jnp.where(lo if grp == 0 else jnp.logical_not(lo), acc + pltpu.roll(acc, HD, 1), 0.0)


def _layer_norm_stats(v):
    mu = _row_mean(v)
    xc = v - mu
    var = _row_mean(xc * xc)
    rs = lax.rsqrt(var + EPS)
    return xc * rs, rs


def _fold(both, own):
    return jnp.where(own, both[:, T:2 * T], both[:, 0:T])


def _unfold(p, own):
    return jnp.concatenate([jnp.where(own, 0.0, p), jnp.where(own, p, 0.0)], axis=1).astype(BF16)


def _band_softmax(q_scaled, k_both, own, has_prev, sink):
    s_both = _dot(q_scaled, k_both, NT)
    s_prev = s_both[:, 0:T]
    if has_prev is not None:
        s_prev = s_prev + jnp.where(has_prev, 0.0, MASKED)
    s = jnp.where(own, s_both[:, T:2 * T], s_prev)
    m = jnp.maximum(jnp.max(s, axis=-1, keepdims=True), sink)
    e = jnp.exp(s - m)
    es = jnp.exp(sink - m)
    inv = 1.0 / (jnp.sum(e, axis=-1, keepdims=True) + es)
    return e * inv, es * inv


BPS_FWD = 4
BPS_BWD = 4


def _mid_specs(nt, rev, bps):
    tile = (lambda i: nt - 1 - i) if rev else (lambda i: i)
    prev = lambda i: jnp.maximum(bps * tile(i) - 1, 0)
    rows = bps * T
    return tile, [
        pl.BlockSpec((rows, D_IN), lambda i: (tile(i), 0)),
        pl.BlockSpec((T, 2 * T), lambda i: (prev(i), OFF_K // (2 * T))),
    ], [
        pl.BlockSpec((rows, 128), lambda i: (tile(i), 0)),
        pl.BlockSpec((rows, 128), lambda i: (tile(i), 0)),
        pl.BlockSpec((T, 128), lambda i: (prev(i), 0)),
        pl.BlockSpec((T, 128), lambda i: (prev(i), 0)),
        pl.BlockSpec((1, DG), lambda i: (0, 0)),
        pl.BlockSpec((1, DG), lambda i: (0, 0)),
        pl.BlockSpec((NG, T, T), lambda i: (0, 0, 0)),
        pl.BlockSpec((NG, T, 1), lambda i: (0, 0, 0)),
        pl.BlockSpec((NQ * T, 1), lambda i: (0, 0)),
    ]


def _rope(xs, cosv, sinv, first_half):
    return xs * cosv + _rot_half(xs, first_half) * sinv


def _mid_fwd(proj, cos, sin, ln_g, ln_b, w_s, b_s, sinks, wo_part):
    S = proj.shape[0]
    BPS = BPS_FWD if S % (BPS_FWD * T) == 0 else 1
    nt = S // (BPS * T)
    tile, proj_specs, par_specs = _mid_specs(nt, False, BPS)

    def body(p_ref, kvp_ref, cos_ref, sin_ref, cosp_ref, sinp_ref, lng_ref, lnb_ref, ws_ref, bs_ref, sink_ref,
             wpart_ref, y_ref, wo_ref, q_ref, wm_ref, send_sems, recv_sems, local_sems):
        i = pl.program_id(0)
        first_half, lo64, own = _attn_consts()
        gather = _RowGather(wpart_ref, wo_ref, W_OUT_ROWS, send_sems, recv_sems, local_sems)
        pl.when(i == 0)(gather.start)
        pl.when(i == nt // 2)(gather.pass_on)

        @pl.when(i == 0)
        def _():
            tril = lax.broadcasted_iota(jnp.int32, (T, T), 0) >= lax.broadcasted_iota(jnp.int32, (T, T), 1)
            for g in range(NG):
                wm_ref[g] = jnp.where(tril, ws_ref[g], 0.0).astype(BF16)

        def block(b, k_prev, v_prev, has_prev):
            rows = slice(b * T, (b + 1) * T)
            xhat, _ = _layer_norm_stats(p_ref[rows, OFF_V:OFF_V + DG])
            vn = xhat * lng_ref[...] + lnb_ref[...]
            for g in range(NG):
                sl = slice(128 * g, 128 * g + 128)
                mixed = _dot(wm_ref[g], vn[:, sl].astype(BF16), NN) + bs_ref[g]
                z = p_ref[rows, OFF_ZA + 128 * g:OFF_ZA + 128 * g + 128]
                u = p_ref[rows, OFF_U + 128 * g:OFF_U + 128 * g + 128]
                y_ref[rows, sl] = (u * mixed * (z * _sigmoid(z))).astype(BF16)

            cosv, sinv = cos_ref[rows, :], sin_ref[rows, :]
            k_cur = _rope(p_ref[rows, OFF_K:OFF_K + 128], cosv, sinv, first_half)
            v_cur = p_ref[rows, OFF_VA:OFF_VA + 128]

            def q_slab(j):
                q = (_rope(p_ref[rows, OFF_Q + 128 * j:OFF_Q + 128 * j + 128], cosv, sinv, first_half)
                     * (HD ** -0.5)).astype(BF16)
                q_ref[rows, 128 * j:128 * j + 128] = q
                return q

            k_both = jnp.concatenate([k_prev, k_cur], axis=0)
            v_both = jnp.concatenate([v_prev, v_cur], axis=0)
            for grp in range(2):
                qs = _stack_heads(q_slab, grp, lo64)
                kg, vg = _both_halves(k_both, grp).astype(BF16), _both_halves(v_both, grp).astype(BF16)
                p, _ = _band_softmax(qs, kg, own, has_prev, sink_ref[GH * T * grp:GH * T * (grp + 1), :])
                o_st = _dot(_unfold(p, own), vg, NN)
                for jj in range(GH // 2):
                    c0 = 128 * (GH // 2 * grp + jj)
                    zb = p_ref[rows, OFF_ZB + c0:OFF_ZB + c0 + 128]
                    o = _unstack_heads(o_st, jj, lo64)
                    y_ref[rows, DG + c0:DG + c0 + 128] = (o * (zb * _sigmoid(zb))).astype(BF16)
            return k_cur, v_cur

        k_prev = _rope(kvp_ref[:, 0:128], cosp_ref[...], sinp_ref[...], first_half)
        kv = block(0, k_prev, kvp_ref[:, 128:256], i > 0)
        for b in range(1, BPS):
            kv = block(b, *kv, None)
        pl.when(i == nt - 1)(gather.finish)

    return pl.pallas_call(
        body, name="mid_fwd", grid=(nt,),
        in_specs=proj_specs + par_specs + [HBM_SPEC],
        out_specs=[pl.BlockSpec((BPS * T, D), lambda i: (tile(i), 0)), HBM_SPEC,
                   pl.BlockSpec((BPS * T, NQ * HD), lambda i: (tile(i), 0))],
        out_shape=[jax.ShapeDtypeStruct((S, D), BF16), jax.ShapeDtypeStruct((D, D), BF16),
                   jax.ShapeDtypeStruct((S, NQ * HD), BF16)],
        scratch_shapes=[pltpu.VMEM((NG, T, T), BF16)] + GATHER_SEMS,
        compiler_params=_params(("arbitrary",)),
    )(proj, proj, cos, sin, cos, sin, ln_g, ln_b, w_s, b_s, sinks, wo_part)


def _mid_bwd(proj, dycat, q_roped, cos, sin, ln_g, ln_b, w_s, b_s, sinks, po):
    S = proj.shape[0]
    BPS = BPS_BWD if S % (BPS_BWD * T) == 0 else 1
    nt = S // (BPS * T)
    tile, proj_specs, par_specs = _mid_specs(nt, True, BPS)
    const2 = lambda i: (0, 0)

    def body(p_ref, kvp_ref, dyc_ref, q_ref, cos_ref, sin_ref, cosp_ref, sinp_ref, lng_ref, lnb_ref, ws_ref, bs_ref,
             sink_ref, po_ref, dp_ref, dlng_ref, dlnb_ref, dws_ref, dbs_ref, dsink_ref, dbqkv_ref, bo_ref,
             carry_ref, dvn_ref, wm_ref, wmt_ref, send_sems, recv_sems, local_sem):
        i = pl.program_id(0)
        first_half, lo64, own = _attn_consts()
        tril = lax.broadcasted_iota(jnp.int32, (T, T), 0) >= lax.broadcasted_iota(jnp.int32, (T, T), 1)
        exchange = _OwnerExchange(po_ref, bo_ref, send_sems, recv_sems, local_sem)
        pl.when(i == 0)(exchange.start)

        @pl.when(i == 0)
        def _():
            for g in range(NG):
                wm = jnp.where(tril, ws_ref[g], 0.0)
                wm_ref[g] = wm.astype(BF16)
                wmt_ref[g] = wm.T.astype(BF16)
            dlng_ref[...] = jnp.zeros_like(dlng_ref)
            dlnb_ref[...] = jnp.zeros_like(dlnb_ref)
            dws_ref[...] = jnp.zeros_like(dws_ref)
            dbs_ref[...] = jnp.zeros_like(dbs_ref)
            dsink_ref[...] = jnp.zeros_like(dsink_ref)
            dbqkv_ref[...] = jnp.zeros_like(dbqkv_ref)
            carry_ref[...] = jnp.zeros_like(carry_ref)

        def roped_k(b):
            rows = slice(b * T, (b + 1) * T)
            return _rope(p_ref[rows, OFF_K:OFF_K + 128], cos_ref[rows, :], sin_ref[rows, :], first_half)

        def block(b, k_prev, v_prev, has_prev, dk_next, dv_next):
            rows = slice(b * T, (b + 1) * T)
            xhat, rs = _layer_norm_stats(p_ref[rows, OFF_V:OFF_V + DG])
            lng = lng_ref[...]
            vn = xhat * lng + lnb_ref[...]
            for g in range(NG):
                sl = slice(128 * g, 128 * g + 128)
                vng = vn[:, sl].astype(BF16)
                mixed = _dot(wm_ref[g], vng, NN) + bs_ref[g]
                z = p_ref[rows, OFF_ZA + 128 * g:OFF_ZA + 128 * g + 128]
                u = p_ref[rows, OFF_U + 128 * g:OFF_U + 128 * g + 128]
                dy = dyc_ref[rows, sl]
                sg = _sigmoid(z)
                sa = z * sg
                dp_ref[rows, OFF_U + 128 * g:OFF_U + 128 * g + 128] = (dy * mixed * sa).astype(BF16)
                dp_ref[rows, OFF_ZA + 128 * g:OFF_ZA + 128 * g + 128] = (
                    dy * u * mixed * (sg * (1.0 + z * (1.0 - sg)))).astype(BF16)
                dm = dy * u * sa
                dmb = dm.astype(BF16)
                dvn_ref[rows, sl] = _dot(wmt_ref[g], dmb, NN)
                dws_ref[g] += jnp.where(tril, _dot(dmb, vng, NT), 0.0)
                dbs_ref[g] += jnp.sum(dm, axis=1, keepdims=True)
            dvn = dvn_ref[rows, :]
            dlng_ref[...] += jnp.sum(dvn * xhat, axis=0, keepdims=True)
            dlnb_ref[...] += jnp.sum(dvn, axis=0, keepdims=True)
            dxh = dvn * lng
            dv_g = rs * (dxh - _row_mean(dxh)
                         - xhat * _row_mean(dxh * xhat))
            dp_ref[rows, OFF_V:OFF_V + DG] = dv_g.astype(BF16)

            cosv, sinv = cos_ref[rows, :], sin_ref[rows, :]
            k_cur = roped_k(b)
            v_cur = p_ref[rows, OFF_VA:OFF_VA + 128]

            def q_slab(j):
                return q_ref[rows, 128 * j:128 * j + 128]

            def do_slab(j):
                zb = p_ref[rows, OFF_ZB + 128 * j:OFF_ZB + 128 * j + 128]
                return dyc_ref[rows, DG + 128 * j:DG + 128 * j + 128] * (zb * _sigmoid(zb))

            k_both = jnp.concatenate([k_prev, k_cur], axis=0)
            v_both = jnp.concatenate([v_prev, v_cur], axis=0)
            dk_both, dv_both = jnp.zeros((2 * T, 128), F32), jnp.zeros((2 * T, 128), F32)
            for grp in range(2):
                qs = _stack_heads(q_slab, grp, lo64)
                d_o = _stack_heads(do_slab, grp, lo64)
                dob = d_o.astype(BF16)
                kg, vg = _both_halves(k_both, grp).astype(BF16), _both_halves(v_both, grp).astype(BF16)
                p, ps = _band_softmax(qs, kg, own, has_prev, sink_ref[GH * T * grp:GH * T * (grp + 1), :])
                p_both = _unfold(p, own)
                o_st = _dot(p_both, vg, NN)
                delta = jnp.sum(d_o * o_st, axis=-1, keepdims=True)
                ds_both = _unfold(p * (_fold(_dot(dob, vg, NT), own) - delta), own)
                dq_st = _dot(ds_both, kg, NN) * (HD ** -0.5)
                dk_both = dk_both + _one_half(_dot(ds_both, qs, TN), grp)
                dv_both = dv_both + _one_half(_dot(p_both, dob, TN), grp)
                dsink_rows = ps * delta
                for hh in range(GH):
                    h = GH * grp + hh
                    dsink_ref[h:h + 1, :] += jnp.broadcast_to(
                        -jnp.sum(dsink_rows[hh * T:(hh + 1) * T], axis=0, keepdims=True), (1, 128))
                for jj in range(GH // 2):
                    c0 = 128 * (GH // 2 * grp + jj)
                    zb = p_ref[rows, OFF_ZB + c0:OFF_ZB + c0 + 128]
                    sg = _sigmoid(zb)
                    o = _unstack_heads(o_st, jj, lo64)
                    dp_ref[rows, OFF_ZB + c0:OFF_ZB + c0 + 128] = (
                        dyc_ref[rows, DG + c0:DG + c0 + 128] * o * (sg * (1.0 + zb * (1.0 - sg)))).astype(BF16)
                    dq = _unstack_heads(dq_st, jj, lo64)
                    dq_pre = dq * cosv - _rot_half(dq, first_half) * sinv
                    dp_ref[rows, OFF_Q + c0:OFF_Q + c0 + 128] = dq_pre.astype(BF16)
                    dbqkv_ref[:, c0:c0 + 128] += jnp.sum(dq_pre, axis=0, keepdims=True)
            dk_prev, dv_prev = dk_both[0:T], dv_both[0:T]
            dk_cur, dv_cur = dk_both[T:2 * T] + dk_next, dv_both[T:2 * T] + dv_next
            dk_pre = dk_cur * cosv - _rot_half(dk_cur, first_half) * sinv
            dp_ref[rows, OFF_K:OFF_K + 128] = dk_pre.astype(BF16)
            dp_ref[rows, OFF_VA:OFF_VA + 128] = dv_cur.astype(BF16)
            dbqkv_ref[:, 1024:1152] += jnp.sum(dk_pre, axis=0, keepdims=True)
            dbqkv_ref[:, 1152:1280] += jnp.sum(dv_cur, axis=0, keepdims=True)
            return dk_prev, dv_prev

        grads = carry_ref[:, 0:128], carry_ref[:, 128:256]
        for b in range(BPS - 1, 0, -1):
            prows = slice((b - 1) * T, b * T)
            grads = block(b, roped_k(b - 1), p_ref[prows, OFF_VA:OFF_VA + 128], None, *grads)
        k_prev = _rope(kvp_ref[:, 0:128], cosp_ref[...], sinp_ref[...], first_half)
        grads = block(0, k_prev, kvp_ref[:, 128:256], i < nt - 1, *grads)
        carry_ref[:, 0:128], carry_ref[:, 128:256] = grads
        pl.when(i == nt - 1)(exchange.finish)

    return pl.pallas_call(
        body, name="mid_bwd", grid=(nt,),
        in_specs=proj_specs + [pl.BlockSpec((BPS * T, D), lambda i: (tile(i), 0)),
                               pl.BlockSpec((BPS * T, NQ * HD), lambda i: (tile(i), 0))] + par_specs + [HBM_SPEC],
        out_specs=[pl.BlockSpec((BPS * T, D_IN), lambda i: (tile(i), 0)),
                   pl.BlockSpec((1, DG), const2), pl.BlockSpec((1, DG), const2),
                   pl.BlockSpec((NG, T, T), lambda i: (0, 0, 0)), pl.BlockSpec((NG, T, 1), lambda i: (0, 0, 0)),
                   pl.BlockSpec((NQ, 128), const2), pl.BlockSpec((1, D_QKV), const2), HBM_SPEC],
        out_shape=[jax.ShapeDtypeStruct((S, D_IN), BF16),
                   jax.ShapeDtypeStruct((1, DG), F32), jax.ShapeDtypeStruct((1, DG), F32),
                   jax.ShapeDtypeStruct((NG, T, T), F32), jax.ShapeDtypeStruct((NG, T, 1), F32),
                   jax.ShapeDtypeStruct((NQ, 128), F32), jax.ShapeDtypeStruct((1, D_QKV), F32),
                   jax.ShapeDtypeStruct(po.shape, BF16)],
        scratch_shapes=[pltpu.VMEM((T, 2 * T), F32), pltpu.VMEM((BPS * T, DG), F32),
                        pltpu.VMEM((NG, T, T), BF16), pltpu.VMEM((NG, T, T), BF16)] + OWNER_SEMS,
        compiler_params=_params(("arbitrary",)),
    )(proj, proj, dycat, q_roped, cos, sin, cos, sin, ln_g, ln_b, w_s, b_s, sinks, po)


def _outproj_loss(ycat, wo, x, target, g_post):
    S = ycat.shape[0]
    tm = _tile(S, 512)
    nt = S // tm
    n_part = 2 if tm % 32 == 0 else 1
    tp = tm // n_part
    const2 = lambda i: (0, 0)

    def body(yc_ref, w_ref, x_ref, t_ref, g_ref, dy_ref, dout_ref, loss_ref, dg_ref, lacc_ref):
        i = pl.program_id(0)

        @pl.when(i == 0)
        def _():
            dg_ref[...] = jnp.zeros_like(dg_ref)
            lacc_ref[...] = jnp.zeros_like(lacc_ref)

        g = g_ref[...]
        ys = [_dot(yc_ref[q * tp:(q + 1) * tp, :], w_ref[...], NN) for q in range(n_part)]
        for q, y in enumerate(ys):
            rows = slice(q * tp, (q + 1) * tp)
            r = lax.rsqrt(_row_mean(y * y) + EPS)
            yh = y * r
            diff = x_ref[rows, :] + yh * g - t_ref[rows, :]
            lacc_ref[...] += jnp.sum(diff * diff, axis=0, keepdims=True)
            dout = diff * (1.0 / D)
            dout_ref[rows, :] = dout
            dg_ref[...] += jnp.sum(dout * yh, axis=0, keepdims=True)
            dyh = dout * g
            dy_ref[rows, :] = (r * (dyh - yh * _row_mean(dyh * yh))).astype(BF16)

        @pl.when(i == nt - 1)
        def _():
            loss_ref[...] = jnp.broadcast_to(jnp.sum(lacc_ref[...], axis=1, keepdims=True) * (0.5 / D), (1, 128))

    row = lambda i: (i, 0)
    return pl.pallas_call(
        body, name="outproj_loss", grid=(nt,),
        in_specs=[pl.BlockSpec((tm, D), row), pl.BlockSpec((D, D), const2, pipeline_mode=pl.Buffered(1)),
                  pl.BlockSpec((tm, D), row), pl.BlockSpec((tm, D), row), pl.BlockSpec((1, D), const2)],
        out_specs=[pl.BlockSpec((tm, D), row), pl.BlockSpec((tm, D), row), pl.BlockSpec((1, 128), const2),
                   pl.BlockSpec((1, D), const2)],
        out_shape=[jax.ShapeDtypeStruct((S, D), BF16), jax.ShapeDtypeStruct((S, D), F32),
                   jax.ShapeDtypeStruct((1, 128), F32), jax.ShapeDtypeStruct((1, D), F32)],
        scratch_shapes=[pltpu.VMEM((1, D), F32)],
        compiler_params=_params(("arbitrary",)),
    )(ycat, wo, x, target, g_post)


def _dycat(dy, wo, dwo):
    S = dy.shape[0]
    tm = _tile(S, 1024)
    nt = S // tm

    def body(dy_ref, w_ref, dwo_ref, o_ref, r_ref, send_sems, recv_sems):
        i = pl.program_id(0)
        copies = _sibling_copies(dwo_ref, r_ref, W_OUT_ROWS, send_sems, recv_sems)

        @pl.when(i == 0)
        def _():
            for cp in copies:
                cp.start()

        o_ref[...] = _dot(dy_ref[...], w_ref[...], NT)

        @pl.when(i == nt - 1)
        def _():
            for cp in copies:
                cp.wait_recv()
            for cp in copies:
                cp.wait_send()

    return pl.pallas_call(
        body, name="dycat", grid=(nt,),
        in_specs=[pl.BlockSpec((tm, D), lambda i: (i, 0)),
                  pl.BlockSpec((D, D), lambda i: (0, 0), pipeline_mode=pl.Buffered(1)), HBM_SPEC],
        out_specs=[pl.BlockSpec((tm, D), lambda i: (i, 0)), HBM_SPEC],
        out_shape=[jax.ShapeDtypeStruct((S, D), F32), jax.ShapeDtypeStruct((N_CHIPS, W_OUT_ROWS, D), BF16)],
        scratch_shapes=[pltpu.SemaphoreType.DMA((N_CHIPS,)), pltpu.SemaphoreType.DMA((N_CHIPS,))],
        compiler_params=_params(("arbitrary",)),
    )(dy, wo, dwo)


def _block_segments(tm, n_tiles, half_rows):
    segs, sid = [], 0
    for t in range(n_tiles):
        segs.append([])
        for b in range(tm * t // half_rows, (tm * (t + 1) - 1) // half_rows + 1):
            segs[-1].append((sid, b, max(tm * t, half_rows * b), min(tm * (t + 1), half_rows * (b + 1))))
            sid += 1
    return segs, sid


def _matmul_tn(a, b, tm, name, mat_grads=None, half_rows=None):
    K, M = a.shape
    N = b.shape[1]
    tk = _tile(K, 2048)
    ni, nk = M // tm, K // tk
    hosting = mat_grads is not None

    def body(a_ref, b_ref, *rest):
        if hosting:
            g_in, (o_ref, tot_ref, r_ref, acc_ref, mbuf, send_sems, recv_sems, stage, seg_send, seg_recv) = (
                rest[:len(MAT)], rest[len(MAT):])
        else:
            o_ref, acc_ref = rest
        i, k = pl.program_id(0), pl.program_id(1)

        if hosting:
            gather = _SlotGather(mbuf, send_sems, recv_sems)

            @pl.when((i == 0) & (k == 0))
            def _():
                r0 = 0
                for q, (_, rows) in enumerate(MAT):
                    mbuf[gather.my_id, r0:r0 + rows, :] = g_in[q][...]
                    r0 += rows
                gather.start()

            pl.when((i == ni // 2) & (k == 0))(gather.pass_on)

            x_, y_, c = _mesh_pos()
            segs, _ = _block_segments(tm, ni, half_rows)

            def seg_copy(t, sid, blk, lo, hi):
                return pltpu.make_async_remote_copy(
                    src_ref=stage.at[t % 2, pl.ds(lo - tm * t, hi - lo), :],
                    dst_ref=r_ref.at[blk // 2, pl.ds(lo - half_rows * blk, hi - lo), :],
                    send_sem=seg_send.at[sid], recv_sem=seg_recv.at[sid],
                    device_id=(x_, y_, 1 - c), device_id_type=pl.DeviceIdType.MESH)

            def sent(t):
                for sid, blk, lo, hi in segs[t]:
                    pl.when(c == 1 - blk % 2)(seg_copy(t, sid, blk, lo, hi).wait_send)

        @pl.when(k == 0)
        def _():
            acc_ref[...] = jnp.zeros_like(acc_ref)

        acc_ref[...] += _dot(a_ref[...], b_ref[...], TN)

        @pl.when(k == nk - 1)
        def _():
            o_ref[...] = acc_ref[...].astype(BF16)

        if hosting:
            for t in range(ni):
                @pl.when((i == t) & (k == nk - 1))
                def _(t=t):
                    if t >= 2:
                        sent(t - 2)
                    stage[t % 2] = acc_ref[...].astype(BF16)
                    for sid, blk, lo, hi in segs[t]:
                        pl.when(c == 1 - blk % 2)(seg_copy(t, sid, blk, lo, hi).start)

            @pl.when((i == ni - 1) & (k == nk - 1))
            def _():
                gather.finish()
                tot_ref[...] = gather.total()
                sent(ni - 2)
                sent(ni - 1)
                for t in range(ni):
                    for sid, blk, lo, hi in segs[t]:
                        pl.when(c == blk % 2)(seg_copy(t, sid, blk, lo, hi).wait_recv)

    whole = pl.BlockSpec(memory_space=pltpu.VMEM)
    in_specs = [pl.BlockSpec((tk, tm), lambda i, k: (k, i)), pl.BlockSpec((tk, N), lambda i, k: (k, 0))]
    out_specs = [pl.BlockSpec((tm, N), lambda i, k: (i, 0))]
    out_shape = [jax.ShapeDtypeStruct((M, N), BF16)]
    scratch = [pltpu.VMEM((tm, N), F32)]
    if hosting:
        in_specs += [whole] * len(MAT)
        n_segs = _block_segments(tm, ni, half_rows)[1]
        out_specs += [whole, HBM_SPEC]
        out_shape += [jax.ShapeDtypeStruct((MAT_ROWS, 128), F32),
                      jax.ShapeDtypeStruct((M // (2 * half_rows), half_rows, N), BF16)]
        scratch += [pltpu.VMEM((N_DEV, MAT_ROWS, 128), F32), pltpu.SemaphoreType.DMA((7,)),
                    pltpu.SemaphoreType.DMA((7,)), pltpu.VMEM((2, tm, N), BF16),
                    pltpu.SemaphoreType.DMA((n_segs,)), pltpu.SemaphoreType.DMA((n_segs,))]
    res = pl.pallas_call(
        body, name=name, grid=(ni, nk), in_specs=in_specs, out_specs=out_specs, out_shape=out_shape,
        scratch_shapes=scratch,
        compiler_params=_params(("arbitrary", "arbitrary") if hosting else ("parallel", "arbitrary")),
    )(a, b, *(mat_grads or ()))
    return res if hosting else res[0]


def _dh_prenorm_bwd(dproj, wt, x, dout, g_pre, pt):
    S = x.shape[0]
    tm, tk = _tile(S, 1024), 768
    ne = 4
    te = tm // ne
    ni, nk = S // tm, D_IN // tk
    n_mm = ni * nk
    n_steps = n_mm + ne
    mm_of = lambda t: jnp.minimum(t, n_mm - 1)

    def quarter_of(t):
        j, q = t // nk - 1, t % nk
        nxt = jnp.where(q < ne, j * ne + q, (j + 1) * ne)
        return jnp.clip(nxt, 0, ni * ne - 1)

    def body(dp_ref, w_ref, x_ref, dout_ref, g_ref, pt_ref, gx_ref, dg_ref, bt_ref, acc_ref,
             send_sems, recv_sems, local_sem):
        t = pl.program_id(0)
        exchange = _OwnerExchange(pt_ref, bt_ref, send_sems, recv_sems, local_sem)
        pl.when(t == 0)(exchange.start)

        @pl.when(t == 0)
        def _():
            dg_ref[...] = jnp.zeros_like(dg_ref)

        has_mm = t < n_mm
        has_q = (t >= nk) & (t % nk < ne)

        def matmul():
            slot = (mm_of(t) // nk) % 2
            old = jnp.where(mm_of(t) % nk > 0, acc_ref[slot], 0.0)
            acc_ref[slot] = old + _dot(dp_ref[...], w_ref[...], NN)

        def quarter():
            slot = (t // nk - 1) % 2
            dh = acc_ref[slot, pl.ds(pl.multiple_of((t % nk) * te, te), te), :]
            xv = x_ref[...]
            r = lax.rsqrt(_row_mean(xv * xv) + EPS)
            xh = xv * r
            dg_ref[...] += jnp.sum(dh * xh, axis=0, keepdims=True)
            dxh = dh * g_ref[...]
            gx_ref[...] = dout_ref[...] + r * (dxh - xh * _row_mean(dxh * xh))

        @pl.when(has_mm & has_q)
        def _():
            matmul()
            quarter()

        pl.when(has_mm & jnp.logical_not(has_q))(matmul)
        pl.when(jnp.logical_not(has_mm) & has_q)(quarter)
        pl.when(t == n_steps - 1)(exchange.finish)

    quarter_block = lambda t: (quarter_of(t), 0)
    return pl.pallas_call(
        body, name="dh_prenorm_bwd", grid=(n_steps,),
        in_specs=[pl.BlockSpec((tm, tk), lambda t: (mm_of(t) // nk, mm_of(t) % nk)),
                  pl.BlockSpec((tk, D), lambda t: (mm_of(t) % nk, 0)),
                  pl.BlockSpec((te, D), quarter_block), pl.BlockSpec((te, D), quarter_block),
                  pl.BlockSpec((1, D), lambda t: (0, 0)), HBM_SPEC],
        out_specs=[pl.BlockSpec((te, D), quarter_block), pl.BlockSpec((1, D), lambda t: (0, 0)), HBM_SPEC],
        out_shape=[jax.ShapeDtypeStruct((S, D), F32), jax.ShapeDtypeStruct((1, D), F32),
                   jax.ShapeDtypeStruct(pt.shape, BF16)],
        scratch_shapes=[pltpu.VMEM((2, tm, D), F32)] + OWNER_SEMS,
        compiler_params=_params(("arbitrary",)),
    )(dproj, wt, x, dout, g_pre, pt)


def _presum(c_arr, own, recv, half_rows):
    n_cols = own.shape[-1]
    own4 = own.reshape(N_CHIPS, 2, half_rows, n_cols)

    def body(c_ref, own_ref, recv_ref, o_ref):
        o_ref[...] = (own_ref[...].astype(F32) + recv_ref[...].astype(F32)).astype(BF16)

    return pl.pallas_call(
        body, name="presum_%d" % half_rows,
        grid_spec=pltpu.PrefetchScalarGridSpec(
            num_scalar_prefetch=1, grid=(N_CHIPS,),
            in_specs=[pl.BlockSpec((None, None, half_rows, n_cols), lambda j, c: (j, c[0], 0, 0)),
                      pl.BlockSpec((None, half_rows, n_cols), lambda j, c: (j, 0, 0))],
            out_specs=pl.BlockSpec((None, half_rows, n_cols), lambda j, c: (j, 0, 0))),
        out_shape=jax.ShapeDtypeStruct((N_CHIPS, half_rows, n_cols), BF16),
        compiler_params=_params(("parallel",)),
    )(c_arr, own4, recv)


def _sibling_presum(c_arr, dw, half_rows, name):
    n_cols = dw.shape[-1]
    dw4 = dw.reshape(N_CHIPS, 2, half_rows, n_cols)

    def body(c_ref, own_ref, dw_ref, o_ref, r_ref, stage, send_sems, recv_sems, load_sem):
        j = pl.program_id(0)
        copies = _sibling_copies(dw_ref, r_ref, half_rows, send_sems, recv_sems)

        @pl.when(j == 0)
        def _():
            for cp in copies:
                cp.start()

        for q in range(N_CHIPS):
            pl.when(j == q)(copies[q].wait_recv)
        load = pltpu.make_async_copy(r_ref.at[j], stage, load_sem)
        load.start()
        load.wait()
        o_ref[...] = (own_ref[...].astype(F32) + stage[...].astype(F32)).astype(BF16)

        @pl.when(j == N_CHIPS - 1)
        def _():
            for cp in copies:
                cp.wait_send()

    out, _ = pl.pallas_call(
        body, name=name,
        grid_spec=pltpu.PrefetchScalarGridSpec(
            num_scalar_prefetch=1, grid=(N_CHIPS,),
            in_specs=[pl.BlockSpec((None, None, half_rows, n_cols), lambda j, c: (j, c[0], 0, 0)), HBM_SPEC],
            out_specs=[pl.BlockSpec((None, half_rows, n_cols), lambda j, c: (j, 0, 0)), HBM_SPEC],
            scratch_shapes=[pltpu.VMEM((half_rows, n_cols), BF16), pltpu.SemaphoreType.DMA((N_CHIPS,)),
                            pltpu.SemaphoreType.DMA((N_CHIPS,)), pltpu.SemaphoreType.DMA(())]),
        out_shape=[jax.ShapeDtypeStruct((N_CHIPS, half_rows, n_cols), BF16),
                   jax.ShapeDtypeStruct((N_CHIPS, half_rows, n_cols), BF16)],
        compiler_params=_params(("arbitrary",)),
    )(c_arr, dw4, dw)
    return out


def _sum_chips(c_arr, parts, name):
    _, rows, n_cols = parts.shape
    nt = 2
    tr = rows // nt

    def body(c_ref, p_ref, o_ref):
        o_ref[...] = ((p_ref[0].astype(F32) + p_ref[1].astype(F32)) + p_ref[2].astype(F32)) + p_ref[3].astype(F32)

    return pl.pallas_call(
        body, name=name,
        grid_spec=pltpu.PrefetchScalarGridSpec(
            num_scalar_prefetch=1, grid=(nt,),
            in_specs=[pl.BlockSpec((N_CHIPS, tr, n_cols), lambda i, c: (0, i, 0))],
            out_specs=pl.BlockSpec((tr, n_cols), lambda i, c: (c[0] * nt + i, 0))),
        out_shape=jax.ShapeDtypeStruct((2 * rows, n_cols), F32),
        compiler_params=_params(("parallel",)),
    )(c_arr, parts)


def _adamw_math(w, g, m, v):
    mn = ADAM_B1 * m + (1.0 - ADAM_B1) * g
    vn = ADAM_B2 * v + (1.0 - ADAM_B2) * (g * g)
    m_hat = mn / (1.0 - ADAM_B1 ** ADAM_STEP)
    v_hat = vn / (1.0 - ADAM_B2 ** ADAM_STEP)
    return -ADAM_LR * (m_hat / (jnp.sqrt(v_hat) + ADAM_EPS) + ADAM_WD * w), mn, vn


def _adamw(w, g, m, v, name):
    R, C = w.shape
    tr = next((t for t in (256, 192, 128) if R % t == 0), R)

    def body(w_ref, g_ref, m_ref, v_ref, go_ref, d_ref, mo_ref, vo_ref):
        gv = g_ref[...]
        go_ref[...] = gv
        d_ref[...], mo_ref[...], vo_ref[...] = _adamw_math(w_ref[...], gv, m_ref[...], v_ref[...])

    spec = pl.BlockSpec((tr, C), lambda i: (i, 0))
    shp = jax.ShapeDtypeStruct((R, C), F32)
    return pl.pallas_call(
        body, name=name, grid=(R // tr,), in_specs=[spec] * 4, out_specs=[spec] * 4, out_shape=[shp] * 4,
        compiler_params=_params(("parallel",)),
    )(w, g, m, v)


HBM_SPEC = pl.BlockSpec(memory_space=pltpu.HBM)
GATHER_LOCAL_CHUNKS = 4
GATHER_SEMS = [pltpu.SemaphoreType.DMA((7,)), pltpu.SemaphoreType.DMA((7,)),
               pltpu.SemaphoreType.DMA((GATHER_LOCAL_CHUNKS,))]
OWNER_SEMS = [pltpu.SemaphoreType.DMA((3,)), pltpu.SemaphoreType.DMA((3,)), pltpu.SemaphoreType.DMA(())]


def _mesh_pos():
    return lax.axis_index("x"), lax.axis_index("y"), lax.axis_index("c")


class _RowGather:
    def __init__(self, src_ref, full_ref, rows, send_sems, recv_sems, local_sems):
        self.src, self.full, self.rows = src_ref, full_ref, rows
        self.send, self.recv, self.local = send_sems, recv_sems, local_sems
        x, y, c = _mesh_pos()
        self.c, self.me, self.sibling = c, (x, y, c), (x, y, 1 - c)
        self.chips = [(1 - x, y), (x, 1 - y), (1 - x, 1 - y)]

    def _block(self, pos):
        px, py, pc = pos
        return self.full.at[pl.ds(pl.multiple_of((4 * px + 2 * py + pc) * self.rows, 16), self.rows), :]

    def _copy(self, k, blk, to):
        return pltpu.make_async_remote_copy(
            src_ref=self.src if blk is self.me else self._block(blk), dst_ref=self._block(blk),
            send_sem=self.send.at[k], recv_sem=self.recv.at[k], device_id=to, device_id_type=MESH)

    def _mine(self):
        return _place_locally(self.src, self._block(self.me), self.local, GATHER_LOCAL_CHUNKS)

    def _first(self):
        return [self._copy(0, self.me, self.sibling)] + [
            self._copy(1 + j, self.me, (*chip, self.c)) for j, chip in enumerate(self.chips)]

    def start(self):
        for cp in self._first() + self._mine():
            cp.start()

    def _passed(self):
        return [self._copy(4 + j, (*chip, self.c), self.sibling) for j, chip in enumerate(self.chips)]

    def pass_on(self):
        for j, chip in enumerate(self.chips):
            self._copy(1 + j, (*chip, self.c), self.me).wait_recv()
            self._passed()[j].start()

    def finish(self):
        self._copy(0, self.sibling, self.me).wait_recv()
        for j, chip in enumerate(self.chips):
            self._copy(4 + j, (*chip, 1 - self.c), self.me).wait_recv()
        for cp in self._first() + self._passed():
            cp.wait_send()
        for cp in self._mine():
            cp.wait()


class _OwnerExchange:
    def __init__(self, src_ref, dst_ref, send_sems, recv_sems, local_sem):
        self.src, self.dst, self.send, self.recv, self.local = src_ref, dst_ref, send_sems, recv_sems, local_sem
        x, y, c = _mesh_pos()
        self.c, self.my_chip = c, 2 * x + y
        self.peers = [(1 - x, y), (x, 1 - y), (1 - x, 1 - y)]

    def _copies(self):
        local = pltpu.make_async_copy(self.src.at[self.my_chip], self.dst.at[self.my_chip], self.local)
        remote = [pltpu.make_async_remote_copy(
            src_ref=self.src.at[2 * px + py], dst_ref=self.dst.at[self.my_chip],
            send_sem=self.send.at[k], recv_sem=self.recv.at[k], device_id=(px, py, self.c), device_id_type=MESH)
            for k, (px, py) in enumerate(self.peers)]
        return local, remote

    def start(self):
        local, remote = self._copies()
        local.start()
        for cp in remote:
            cp.start()

    def finish(self):
        local, remote = self._copies()
        for cp in remote:
            cp.wait_recv()
        for cp in remote:
            cp.wait_send()
        local.wait()


def _sibling_copies(dw_ref, r_ref, rows, send_sems, recv_sems):
    x, y, c = _mesh_pos()
    return [pltpu.make_async_remote_copy(
        src_ref=dw_ref.at[pl.ds(pl.multiple_of((2 * j + (1 - c)) * rows, 16), rows), :], dst_ref=r_ref.at[j],
        send_sem=send_sems.at[j], recv_sem=recv_sems.at[j], device_id=(x, y, 1 - c), device_id_type=MESH)
        for j in range(N_CHIPS)]


PAIR_CHUNKS = 4


def _pair_halves(gt, go, vec_grads, loss_part):
    n_vec = len(VEC)

    def body(*refs):
        g_in, loss_in = refs[2:2 + n_vec], refs[2 + n_vec]
        gt_ref, go_ref, tot_v, vbuf, send_sems, recv_sems, vsend_sems, vrecv_sems = refs[3 + n_vec:]
        x, y, c = _mesh_pos()
        gather = _SlotGather(vbuf, vsend_sems, vrecv_sems)
        vbuf[gather.my_id] = jnp.zeros((VEC_ROWS, D), F32)
        for r, (_, n) in enumerate(VEC):
            vbuf[gather.my_id, r:r + 1, 0:n] = g_in[r][...]
        vbuf[gather.my_id, LOSS_ROW:LOSS_ROW + 1, 0:128] = loss_in[...]
        gather.start()
        copies = []
        for a, (ref, rows) in enumerate(((gt_ref, W_IN_ROWS), (go_ref, W_OUT_ROWS))):
            ch = rows // PAIR_CHUNKS
            for q in range(PAIR_CHUNKS):
                part = ref.at[pl.ds(pl.multiple_of(c * rows + q * ch, 8), ch), :]
                copies.append(pltpu.make_async_remote_copy(
                    src_ref=part, dst_ref=part, send_sem=send_sems.at[PAIR_CHUNKS * a + q],
                    recv_sem=recv_sems.at[PAIR_CHUNKS * a + q], device_id=(x, y, 1 - c), device_id_type=MESH))
        for cp in copies:
            cp.start()
        gather.pass_on()
        gather.finish()
        tot_v[...] = gather.total()
        for cp in copies:
            cp.wait_recv()
        for cp in copies:
            cp.wait_send()

    vmem = pl.BlockSpec(memory_space=pltpu.VMEM)
    return pl.pallas_call(
        body, name="pair_halves",
        in_specs=[HBM_SPEC, HBM_SPEC] + [vmem] * (n_vec + 1), out_specs=[HBM_SPEC, HBM_SPEC, vmem],
        out_shape=[jax.ShapeDtypeStruct(gt.shape, F32), jax.ShapeDtypeStruct(go.shape, F32),
                   jax.ShapeDtypeStruct((VEC_ROWS, D), F32)],
        input_output_aliases={0: 0, 1: 1},
        scratch_shapes=[pltpu.VMEM((N_DEV, VEC_ROWS, D), F32),
                        pltpu.SemaphoreType.DMA((2 * PAIR_CHUNKS,)), pltpu.SemaphoreType.DMA((2 * PAIR_CHUNKS,)),
                        pltpu.SemaphoreType.DMA((7,)), pltpu.SemaphoreType.DMA((7,))],
    )(gt, go, *vec_grads, loss_part)


VEC = (("g_pre", 2048), ("g_post", 2048), ("b_qkv", 1280), ("ln_v_g", 1024), ("ln_v_b", 1024), ("attn_sinks", 16))
VEC_ROWS = 8
LOSS_ROW = len(VEC)
MAT = (("w_spatial", NG * T), ("b_spatial", NG))
MAT_ROWS = sum(r for _, r in MAT)


class _SlotGather:
    def __init__(self, buf, send_sems, recv_sems):
        self.buf, self.send, self.recv = buf, send_sems, recv_sems
        x, y, c = _mesh_pos()
        self.c, self.me, self.sibling, self.my_id = c, (x, y, c), (x, y, 1 - c), 4 * x + 2 * y + c
        self.chips = [(1 - x, y), (x, 1 - y), (1 - x, 1 - y)]

    def _copy(self, k, blk, to):
        px, py, pc = blk
        slot = self.buf.at[4 * px + 2 * py + pc]
        return pltpu.make_async_remote_copy(
            src_ref=slot, dst_ref=slot, send_sem=self.send.at[k], recv_sem=self.recv.at[k],
            device_id=to, device_id_type=MESH)

    def _first(self):
        return [self._copy(0, self.me, self.sibling)] + [
            self._copy(1 + j, self.me, (*chip, self.c)) for j, chip in enumerate(self.chips)]

    def start(self):
        for cp in self._first():
            cp.start()

    def _passed(self):
        return [self._copy(4 + j, (*chip, self.c), self.sibling) for j, chip in enumerate(self.chips)]

    def pass_on(self):
        for j, chip in enumerate(self.chips):
            self._copy(1 + j, (*chip, self.c), self.me).wait_recv()
            self._passed()[j].start()

    def finish(self):
        self._copy(0, self.sibling, self.me).wait_recv()
        for j, chip in enumerate(self.chips):
            self._copy(4 + j, (*chip, 1 - self.c), self.me).wait_recv()
        for cp in self._first() + self._passed():
            cp.wait_send()

    def total(self):
        t = self.buf[0]
        for d in range(1, N_DEV):
            t = t + self.buf[d]
        return t


def _small_update(vec_total, mat_total, vec_state, mat_state):
    n_vec, n_mat = len(VEC), len(MAT)
    n_par = n_vec + n_mat
    n_in = 2 + 3 * n_par

    def body(*refs):
        tot_v, tot_m = refs[0], refs[1]
        st_in = refs[2:n_in]
        outs, loss_out = refs[n_in:n_in + 4 * n_par], refs[n_in + 4 * n_par]
        loss_out[...] = tot_v[LOSS_ROW:LOSS_ROW + 1, 0:128]
        r0 = 0
        for q in range(n_par):
            if q < n_vec:
                g = tot_v[q:q + 1, 0:VEC[q][1]]
            else:
                rows = MAT[q - n_vec][1]
                g = tot_m[r0:r0 + rows, :]
                r0 += rows
            w, m, v = (st_in[3 * q + t][...] for t in range(3))
            outs[4 * q][...] = g
            outs[4 * q + 1][...], outs[4 * q + 2][...], outs[4 * q + 3][...] = _adamw_math(w, g, m, v)

    state = [a for wmv in list(vec_state) + list(mat_state) for a in wmv]
    vmem = pl.BlockSpec(memory_space=pltpu.VMEM)
    out_shape = [jax.ShapeDtypeStruct(wmv[0].shape, F32) for wmv in list(vec_state) + list(mat_state) for _ in range(4)]
    out_shape.append(jax.ShapeDtypeStruct((1, 128), F32))
    res = pl.pallas_call(
        body, name="small_update",
        in_specs=[vmem] * n_in, out_specs=[vmem] * len(out_shape), out_shape=out_shape,
        compiler_params=pltpu.CompilerParams(vmem_limit_bytes=VMEM_LIMIT),
    )(vec_total, mat_total, *state)
    return [res[4 * q:4 * q + 4] for q in range(n_par)], res[-1]


def kernel(x, positions, g_pre, w_in, b_qkv, ln_v_g, ln_v_b, w_spatial, b_spatial, attn_sinks, w_out, g_post, loss_target, m_g_pre, m_w_in, m_b_qkv, m_ln_v_g, m_ln_v_b, m_w_spatial, m_b_spatial, m_attn_sinks, m_w_out, m_g_post, v_g_pre, v_w_in, v_b_qkv, v_ln_v_g, v_ln_v_b, v_w_spatial, v_b_spatial, v_attn_sinks, v_w_out, v_g_post):
    S = x.shape[1]
    c = lax.axis_index("c")
    c_arr = jnp.reshape(c, (1,)).astype(jnp.int32)
    x2 = x[0]
    target = loss_target[0]
    pos = positions.reshape(S, 1)
    half = HD // 2
    inv_freq = ROPE_THETA ** (-jnp.arange(half, dtype=F32) * (2.0 / HD))
    invf = jnp.tile(inv_freq, 128 // half).reshape(1, 128)
    bias = jnp.concatenate([jnp.zeros((OFF_Q,), F32), b_qkv[0], jnp.zeros((D_IN - OFF_ZB,), F32)]).reshape(1, D_IN)
    b_s_col = b_spatial[0].reshape(NG, T, 1)
    sinks = jnp.repeat(attn_sinks[0], T).reshape(NQ * T, 1)

    chip = 2 * lax.axis_index("x") + lax.axis_index("y")
    wt_part = lax.dynamic_slice_in_dim(w_in[0].T.astype(BF16), c * W_IN_ROWS, W_IN_ROWS, axis=0)
    wo_part = lax.dynamic_slice_in_dim(w_out[0].astype(BF16), c * W_OUT_ROWS, W_OUT_ROWS, axis=0)
    sched = jnp.asarray(PROJ_SCHEDULE, jnp.int32)[chip]

    h, proj, wt, cos, sin = _prenorm_inproj(sched, x2, g_pre, bias, wt_part, pos, invf)
    ycat, wo, q_roped = _mid_fwd(proj, cos, sin, ln_v_g, ln_v_b, w_spatial[0], b_s_col, sinks, wo_part)
    dy, dout, loss_part, dg_post = _outproj_loss(ycat, wo, x2, target, g_post)

    dwo = _matmul_tn(ycat, dy, 1024, "dw_out")
    dycat, ro = _dycat(dy, wo, dwo)
    po = _presum(c_arr, dwo, ro, W_OUT_ROWS)
    dproj, dln_g, dln_b, dws, dbs, dsink, dbqkv, bo = _mid_bwd(
        proj, dycat, q_roped, cos, sin, ln_v_g, ln_v_b, w_spatial[0], b_s_col, sinks, po)
    dwt, mat_total, rt = _matmul_tn(dproj, h, 768, "dw_in_t",
                                    mat_grads=[dws.reshape(NG * T, T), dbs.reshape(NG, T)], half_rows=W_IN_ROWS)
    pt = _presum(c_arr, dwt, rt, W_IN_ROWS)
    grad_x, dg_pre, bt = _dh_prenorm_bwd(dproj, wt, x2, dout, g_pre, pt)
    grads = {"g_pre": dg_pre, "g_post": dg_post, "b_qkv": dbqkv, "ln_v_g": dln_g, "ln_v_b": dln_b,
             "attn_sinks": dsink[:, 0].reshape(1, NQ)}
    gt, go, vec_total = _pair_halves(_sum_chips(c_arr, bt, "sum_chips_in"), _sum_chips(c_arr, bo, "sum_chips_out"),
                                     [grads[n] for n, _ in VEC], loss_part)

    g_w_in, d_w_in, nm_w_in, nv_w_in = (a.T for a in _adamw(w_in[0].T, gt, m_w_in[0].T, v_w_in[0].T, "adamw_w_in"))
    g_w_out, d_w_out, nm_w_out, nv_w_out = _adamw(w_out[0], go, m_w_out[0], v_w_out[0], "adamw_w_out")

    state = {"g_pre": (g_pre, m_g_pre, v_g_pre), "g_post": (g_post, m_g_post, v_g_post),
             "b_qkv": (b_qkv, m_b_qkv, v_b_qkv), "ln_v_g": (ln_v_g, m_ln_v_g, v_ln_v_g),
             "ln_v_b": (ln_v_b, m_ln_v_b, v_ln_v_b), "attn_sinks": (attn_sinks, m_attn_sinks, v_attn_sinks),
             "w_spatial": tuple(a.reshape(NG * T, T) for a in (w_spatial, m_w_spatial, v_w_spatial)),
             "b_spatial": tuple(a.reshape(NG, T) for a in (b_spatial, m_b_spatial, v_b_spatial))}
    results, loss = _small_update(vec_total, mat_total, [state[n] for n, _ in VEC], [state[n] for n, _ in MAT])
    small = {n: [a.reshape(w.shape) for a in res]
             for (n, _), res, w in zip(VEC + MAT, results, [state[n][0] for n, _ in VEC + MAT])}
    small["w_spatial"] = [a.reshape(w_spatial.shape) for a in small["w_spatial"]]
    small["b_spatial"] = [a.reshape(b_spatial.shape) for a in small["b_spatial"]]
    big = {"w_in": [a[None] for a in (g_w_in, d_w_in, nm_w_in, nv_w_in)],
           "w_out": [a[None] for a in (g_w_out, d_w_out, nm_w_out, nv_w_out)]}
    order = ("g_pre", "w_in", "b_qkv", "ln_v_g", "ln_v_b", "w_spatial", "b_spatial", "attn_sinks", "w_out", "g_post")
    leaves = {**small, **big}
    return (loss[0, 0], grad_x[None], *[leaves[n][t] for t in range(4) for n in order])
```

```python
import jax
import jax.numpy as jnp
from jax import lax
from jax.experimental import pallas as pl
from jax.experimental.pallas import tpu as pltpu

F32 = jnp.float32
BF16 = jnp.bfloat16
MESH = pl.DeviceIdType.MESH

D = 2048
DG = 1024
T = 128
NG = 8
HD = 64
NQ = 16
D_IN = 5376
OFF_U, OFF_V, OFF_ZA, OFF_Q, OFF_K, OFF_VA, OFF_ZB = 0, 1024, 2048, 3072, 4096, 4224, 4352
D_QKV = 1280
EPS = 1e-6
ROPE_THETA = 10000.0
N_CHIPS = 4
N_DEV = 8
W_IN_ROWS = D_IN // N_DEV
W_OUT_ROWS = D // N_DEV

ADAM_LR, ADAM_B1, ADAM_B2, ADAM_EPS, ADAM_WD, ADAM_STEP = 0.001, 0.9, 0.999, 1e-08, 0.01, 10

VMEM_LIMIT = 56 * 1024 * 1024


def _tile(n, pref):
    return pref if n % pref == 0 else n


def _params(sem=None, vmem=VMEM_LIMIT):
    return pltpu.CompilerParams(dimension_semantics=sem, vmem_limit_bytes=vmem)


def _sigmoid(z):
    return 1.0 / (1.0 + jnp.exp(-z))


def _row_mean(v):
    return jnp.mean(v, axis=-1, keepdims=True)


def _dot(a, b, dims):
    return lax.dot_general(a, b, (dims, ((), ())), preferred_element_type=F32)


NN = ((1,), (0,))
NT = ((1,), (1,))
TN = ((0,), (0,))


PROJ_TN = 768
N_PROJ_TILES = D_IN // PROJ_TN
PROJ_SCHEDULE = ((0, 1, 2, 4, 3, 6, 5, 5), (2, 1, 0, 6, 4, 3, 5, 4), (4, 5, 6, 0, 2, 1, 3, 4), (6, 5, 4, 2, 3, 0, 1, 5))


def _place_locally(src_ref, dst_rows_ref, sems, n_chunks):
    ch = src_ref.shape[0] // n_chunks
    return [pltpu.make_async_copy(src_ref.at[pl.ds(q * ch, ch), :], dst_rows_ref.at[pl.ds(q * ch, ch), :],
                                  sems.at[q]) for q in range(n_chunks)]


def _prenorm_inproj(sched, x, g, bias, wt_part, pos, invf):
    S = x.shape[0]
    tp, tm = _tile(S, 512), _tile(S, 1024)
    n_pre, ns = S // tp, S // tm
    n_steps = n_pre + N_PROJ_TILES * ns
    pos_of = lambda i: jnp.maximum(i - n_pre, 0) // ns
    row_of = lambda i: jnp.maximum(i - n_pre, 0) % ns

    def body(sched_ref, x_ref, g_ref, b_ref, wpart_ref, pos_ref, invf_ref, h_ref, proj_ref, wt_ref, cos_ref, sin_ref,
             h_all, w_tile, stage, send_sems, recv_sems, w_sems, local_sems):
        i = pl.program_id(0)
        x_, y_, c = _mesh_pos()
        me, sibling = (x_, y_, c), (x_, y_, 1 - c)
        chips = [(1 - x_, y_), (x_, 1 - y_), (1 - x_, 1 - y_)]

        def block(pos):
            px, py, pc = pos
            return wt_ref.at[pl.ds(pl.multiple_of((4 * px + 2 * py + pc) * W_IN_ROWS, 16), W_IN_ROWS), :]

        def copy(k, blk, to):
            return pltpu.make_async_remote_copy(
                src_ref=wpart_ref if blk is me else block(blk), dst_ref=block(blk),
                send_sem=send_sems.at[k], recv_sem=recv_sems.at[k], device_id=to, device_id_type=MESH)

        stage_in = pltpu.make_async_copy(wpart_ref, stage, local_sems.at[0])
        stage_out = pltpu.make_async_copy(stage, block(me), local_sems.at[1])

        relay = (jnp.where(c == 0, x_, 1 - x_), jnp.where(c == 0, 1 - y_, y_))
        relayed = (jnp.where(c == 0, 1 - x_, x_), jnp.where(c == 0, y_, 1 - y_))

        early = c == y_
        y_send = copy(2, me, (*chips[1], c))

        def own_sends():
            return [copy(0, me, sibling), y_send, copy(1, me, (*chips[0], c))]

        def passed_on():
            return [copy(4, (*chips[0], c), sibling), copy(5, (*chips[1], c), sibling),
                    copy(3, (*relayed, c), (*relay, c))]

        def early_block_arrives():
            @pl.when(early)
            def _():
                copy(2, (*chips[1], c), me).wait_recv()
                copy(5, (*chips[1], c), sibling).start()
                y_send.start()

            pl.when(jnp.logical_not(early))(lambda: copy(5, (*chips[1], 1 - c), me).wait_recv())

        def neighbours_arrive():
            copy(1, (*chips[0], c), me).wait_recv()
            copy(4, (*chips[0], c), sibling).start()

            @pl.when(early)
            def _():
                copy(3, (*relayed, c), (*relay, c)).start()
                copy(5, (*chips[1], 1 - c), me).wait_recv()

            @pl.when(jnp.logical_not(early))
            def _():
                copy(2, (*chips[1], c), me).wait_recv()
                copy(5, (*chips[1], c), sibling).start()
                copy(3, (*relayed, c), (*relay, c)).start()

        def diagonal_arrives():
            copy(3, (*chips[2], c), me).wait_recv()
            copy(6, (*chips[2], c), sibling).start()
            copy(6, (*chips[2], 1 - c), me).wait_recv()

        def tile_load(p):
            slot = p % 2
            rows = wt_ref.at[pl.ds(pl.multiple_of(sched_ref[p] * PROJ_TN, 16), PROJ_TN), :]
            return pltpu.make_async_copy(rows, w_tile.at[slot], w_sems.at[slot])

        def prepare(p):
            p = jnp.asarray(p, jnp.int32)

            @pl.when(p == 0)
            def _():
                copy(0, sibling, me).wait_recv()
                stage_out.wait()

            pl.when(p == 1)(early_block_arrives)
            pl.when(p == 2)(neighbours_arrive)
            pl.when(p == 3)(lambda: copy(4, (*chips[0], 1 - c), me).wait_recv())
            pl.when(p == sched_ref[N_PROJ_TILES])(diagonal_arrives)
            tile_load(p).start()

        @pl.when(i == 0)
        def _():
            stage_in.start()
            copy(0, me, sibling).start()
            copy(1, me, (*chips[0], c)).start()
            pl.when(jnp.logical_not(early))(y_send.start)
            stage_in.wait()
            stage_out.start()

        @pl.when(i < n_pre)
        def _():
            xv = x_ref[...]
            r = lax.rsqrt(_row_mean(xv * xv) + EPS)
            hv = (xv * r * g_ref[...]).astype(BF16)
            h_ref[...] = hv
            h_all[pl.ds(pl.multiple_of(i * tp, tp), tp), :] = hv
            ang = pos_ref[...].astype(F32) * invf_ref[...]
            cos_ref[...] = jnp.cos(ang)
            sin_ref[...] = jnp.sin(ang)

        pl.when(i == n_pre - 1)(lambda: prepare(0))

        @pl.when(i >= n_pre)
        def _():
            p, s = pos_of(i), row_of(i)
            pl.when(s == 0)(lambda: tile_load(p).wait())
            pl.when((s == ns - 1) & (p < N_PROJ_TILES - 1))(lambda: prepare(p + 1))
            hv = h_all[pl.ds(pl.multiple_of(s * tm, tm), tm), :]
            proj_ref[...] = _dot(hv, w_tile[p % 2], NT) + b_ref[...]

        @pl.when(i == n_steps - 1)
        def _():
            for cp in own_sends() + passed_on() + [copy(6, (*chips[2], c), sibling)]:
                cp.wait_send()

    return pl.pallas_call(
        body, name="prenorm_inproj",
        grid_spec=pltpu.PrefetchScalarGridSpec(
            num_scalar_prefetch=1, grid=(n_steps,),
            in_specs=[pl.BlockSpec((tp, D), lambda i, sc: (jnp.minimum(i, n_pre - 1), 0)),
                      pl.BlockSpec((1, D), lambda i, sc: (0, 0)),
                      pl.BlockSpec((1, PROJ_TN), lambda i, sc: (0, sc[pos_of(i)])),
                      HBM_SPEC,
                      pl.BlockSpec((tp, 1), lambda i, sc: (jnp.minimum(i, n_pre - 1), 0)),
                      pl.BlockSpec((1, 128), lambda i, sc: (0, 0))],
            out_specs=[pl.BlockSpec((tp, D), lambda i, sc: (jnp.minimum(i, n_pre - 1), 0)),
                       pl.BlockSpec((tm, PROJ_TN), lambda i, sc: (row_of(i), sc[pos_of(i)])),
                       HBM_SPEC,
                       pl.BlockSpec((tp, 128), lambda i, sc: (jnp.minimum(i, n_pre - 1), 0)),
                       pl.BlockSpec((tp, 128), lambda i, sc: (jnp.minimum(i, n_pre - 1), 0))],
            scratch_shapes=[pltpu.VMEM((S, D), BF16), pltpu.VMEM((2, PROJ_TN, D), BF16),
                            pltpu.VMEM((W_IN_ROWS, D), BF16),
                            pltpu.SemaphoreType.DMA((7,)), pltpu.SemaphoreType.DMA((7,)),
                            pltpu.SemaphoreType.DMA((2,)), pltpu.SemaphoreType.DMA((2,))]),
        out_shape=[jax.ShapeDtypeStruct((S, D), BF16), jax.ShapeDtypeStruct((S, D_IN), F32),
                   jax.ShapeDtypeStruct((D_IN, D), BF16),
                   jax.ShapeDtypeStruct((S, 128), F32), jax.ShapeDtypeStruct((S, 128), F32)],
        compiler_params=_params(("arbitrary",)),
    )(sched, x, g, bias, wt_part, pos, invf)


def _rot_half(xs, first_half):
    return jnp.where(first_half, -pltpu.roll(xs, 96, 1), pltpu.roll(xs, 32, 1))


GH = NQ // 2


MASKED = -1e30


def _attn_consts():
    lane = lax.broadcasted_iota(jnp.int32, (T, 128), 1)
    row = lax.broadcasted_iota(jnp.int32, (GH * T, T), 0) & (T - 1)
    on_diag_or_below = row >= lax.broadcasted_iota(jnp.int32, (GH * T, T), 1)
    return (lane & (HD - 1)) < (HD // 2), lane < HD, on_diag_or_below


def _stack_heads(slab_fn, grp, lo64):
    blocks = []
    for jj in range(GH // 2):
        s = slab_fn(GH // 2 * grp + jj)
        blocks += [jnp.where(lo64, s, 0.0), jnp.where(lo64, 0.0, s)]
    return jnp.concatenate(blocks, axis=0)


def _unstack_heads(stacked, jj, lo64):
    return jnp.where(lo64, stacked[(2 * jj) * T:(2 * jj + 1) * T], stacked[(2 * jj + 1) * T:(2 * jj + 2) * T])


def _both_halves(kv, grp):
    lo = lax.broadcasted_iota(jnp.int32, kv.shape, 1) < HD
    swapped = pltpu.roll(kv, HD, 1)
    return jnp.where(lo, kv, swapped) if grp == 0 else jnp.where(lo, swapped, kv)


def _one_half(acc, grp):
    lo = lax.broadcasted_iota(jnp.int32, acc.shape, 1) < HD
    return jnp.where(lo if grp == 0 else jnp.logical_not(lo), acc + pltpu.roll(acc, HD, 1), 0.0)


def _layer_norm_stats(v):
    mu = _row_mean(v)
    xc = v - mu
    var = _row_mean(xc * xc)
    rs = lax.rsqrt(var + EPS)
    return xc * rs, rs


def _fold(both, own):
    return jnp.where(own, both[:, T:2 * T], both[:, 0:T])


def _unfold(p, own):
    return jnp.concatenate([jnp.where(own, 0.0, p), jnp.where(own, p, 0.0)], axis=1).astype(BF16)


def _band_softmax(q_scaled, k_both, own, has_prev, sink):
    s_both = _dot(q_scaled, k_both, NT)
    s_prev = s_both[:, 0:T]
    if has_prev is not None:
        s_prev = s_prev + jnp.where(has_prev, 0.0, MASKED)
    s = jnp.where(own, s_both[:, T:2 * T], s_prev)
    m = jnp.maximum(jnp.max(s, axis=-1, keepdims=True), sink)
    e = jnp.exp(s - m)
    es = jnp.exp(sink - m)
    inv = 1.0 / (jnp.sum(e, axis=-1, keepdims=True) + es)
    return e * inv, es * inv


BPS_FWD = 4
BPS_BWD = 4


def _mid_specs(nt, rev, bps):
    tile = (lambda i: nt - 1 - i) if rev else (lambda i: i)
    prev = lambda i: jnp.maximum(bps * tile(i) - 1, 0)
    rows = bps * T
    return tile, [
        pl.BlockSpec((rows, D_IN), lambda i: (tile(i), 0)),
        pl.BlockSpec((T, 2 * T), lambda i: (prev(i), OFF_K // (2 * T))),
    ], [
        pl.BlockSpec((rows, 128), lambda i: (tile(i), 0)),
        pl.BlockSpec((rows, 128), lambda i: (tile(i), 0)),
        pl.BlockSpec((T, 128), lambda i: (prev(i), 0)),
        pl.BlockSpec((T, 128), lambda i: (prev(i), 0)),
        pl.BlockSpec((1, DG), lambda i: (0, 0)),
        pl.BlockSpec((1, DG), lambda i: (0, 0)),
        pl.BlockSpec((NG, T, T), lambda i: (0, 0, 0)),
        pl.BlockSpec((NG, T, 1), lambda i: (0, 0, 0)),
        pl.BlockSpec((NQ * T, 1), lambda i: (0, 0)),
    ]


def _rope(xs, cosv, sinv, first_half):
    return xs * cosv + _rot_half(xs, first_half) * sinv


def _mid_fwd(proj, cos, sin, ln_g, ln_b, w_s, b_s, sinks, wo_part):
    S = proj.shape[0]
    BPS = BPS_FWD if S % (BPS_FWD * T) == 0 else 1
    nt = S // (BPS * T)
    tile, proj_specs, par_specs = _mid_specs(nt, False, BPS)

    def body(p_ref, kvp_ref, cos_ref, sin_ref, cosp_ref, sinp_ref, lng_ref, lnb_ref, ws_ref, bs_ref, sink_ref,
             wpart_ref, y_ref, wo_ref, q_ref, wm_ref, send_sems, recv_sems, local_sems):
        i = pl.program_id(0)
        first_half, lo64, own = _attn_consts()
        gather = _RowGather(wpart_ref, wo_ref, W_OUT_ROWS, send_sems, recv_sems, local_sems)
        pl.when(i == 0)(gather.start)
        pl.when(i == nt // 2)(gather.pass_on)

        @pl.when(i == 0)
        def _():
            tril = lax.broadcasted_iota(jnp.int32, (T, T), 0) >= lax.broadcasted_iota(jnp.int32, (T, T), 1)
            for g in range(NG):
                wm_ref[g] = jnp.where(tril, ws_ref[g], 0.0).astype(BF16)

        def block(b, k_prev, v_prev, has_prev):
            rows = slice(b * T, (b + 1) * T)
            xhat, _ = _layer_norm_stats(p_ref[rows, OFF_V:OFF_V + DG])
            vn = xhat * lng_ref[...] + lnb_ref[...]
            for g in range(NG):
                sl = slice(128 * g, 128 * g + 128)
                mixed = _dot(wm_ref[g], vn[:, sl].astype(BF16), NN) + bs_ref[g]
                z = p_ref[rows, OFF_ZA + 128 * g:OFF_ZA + 128 * g + 128]
                u = p_ref[rows, OFF_U + 128 * g:OFF_U + 128 * g + 128]
                y_ref[rows, sl] = (u * mixed * (z * _sigmoid(z))).astype(BF16)

            cosv, sinv = cos_ref[rows, :], sin_ref[rows, :]
            k_cur = _rope(p_ref[rows, OFF_K:OFF_K + 128], cosv, sinv, first_half)
            v_cur = p_ref[rows, OFF_VA:OFF_VA + 128]

            def q_slab(j):
                q = (_rope(p_ref[rows, OFF_Q + 128 * j:OFF_Q + 128 * j + 128], cosv, sinv, first_half)
                     * (HD ** -0.5)).astype(BF16)
                q_ref[rows, 128 * j:128 * j + 128] = q
                return q

            k_both = jnp.concatenate([k_prev, k_cur], axis=0)
            v_both = jnp.concatenate([v_prev, v_cur], axis=0)
            for grp in range(2):
                qs = _stack_heads(q_slab, grp, lo64)
                kg, vg = _both_halves(k_both, grp).astype(BF16), _both_halves(v_both, grp).astype(BF16)
                p, _ = _band_softmax(qs, kg, own, has_prev, sink_ref[GH * T * grp:GH * T * (grp + 1), :])
                o_st = _dot(_unfold(p, own), vg, NN)
                for jj in range(GH // 2):
                    c0 = 128 * (GH // 2 * grp + jj)
                    zb = p_ref[rows, OFF_ZB + c0:OFF_ZB + c0 + 128]
                    o = _unstack_heads(o_st, jj, lo64)
                    y_ref[rows, DG + c0:DG + c0 + 128] = (o * (zb * _sigmoid(zb))).astype(BF16)
            return k_cur, v_cur

        k_prev = _rope(kvp_ref[:, 0:128], cosp_ref[...], sinp_ref[...], first_half)
        kv = block(0, k_prev, kvp_ref[:, 128:256], i > 0)
        for b in range(1, BPS):
            kv = block(b, *kv, None)
        pl.when(i == nt - 1)(gather.finish)

    return pl.pallas_call(
        body, name="mid_fwd", grid=(nt,),
        in_specs=proj_specs + par_specs + [HBM_SPEC],
        out_specs=[pl.BlockSpec((BPS * T, D), lambda i: (tile(i), 0)), HBM_SPEC,
                   pl.BlockSpec((BPS * T, NQ * HD), lambda i: (tile(i), 0))],
        out_shape=[jax.ShapeDtypeStruct((S, D), BF16), jax.ShapeDtypeStruct((D, D), BF16),
                   jax.ShapeDtypeStruct((S, NQ * HD), BF16)],
        scratch_shapes=[pltpu.VMEM((NG, T, T), BF16)] + GATHER_SEMS,
        compiler_params=_params(("arbitrary",)),
    )(proj, proj, cos, sin, cos, sin, ln_g, ln_b, w_s, b_s, sinks, wo_part)


def _mid_bwd(proj, dycat, q_roped, cos, sin, ln_g, ln_b, w_s, b_s, sinks, po):
    S = proj.shape[0]
    BPS = BPS_BWD if S % (BPS_BWD * T) == 0 else 1
    nt = S // (BPS * T)
    tile, proj_specs, par_specs = _mid_specs(nt, True, BPS)
    const2 = lambda i: (0, 0)

    def body(p_ref, kvp_ref, dyc_ref, q_ref, cos_ref, sin_ref, cosp_ref, sinp_ref, lng_ref, lnb_ref, ws_ref, bs_ref,
             sink_ref, po_ref, dp_ref, dlng_ref, dlnb_ref, dws_ref, dbs_ref, dsink_ref, dbqkv_ref, bo_ref,
             carry_ref, dvn_ref, wm_ref, wmt_ref, send_sems, recv_sems, local_sem):
        i = pl.program_id(0)
        first_half, lo64, own = _attn_consts()
        tril = lax.broadcasted_iota(jnp.int32, (T, T), 0) >= lax.broadcasted_iota(jnp.int32, (T, T), 1)
        exchange = _OwnerExchange(po_ref, bo_ref, send_sems, recv_sems, local_sem)
        pl.when(i == 0)(exchange.start)

        @pl.when(i == 0)
        def _():
            for g in range(NG):
                wm = jnp.where(tril, ws_ref[g], 0.0)
                wm_ref[g] = wm.astype(BF16)
                wmt_ref[g] = wm.T.astype(BF16)
            dlng_ref[...] = jnp.zeros_like(dlng_ref)
            dlnb_ref[...] = jnp.zeros_like(dlnb_ref)
            dws_ref[...] = jnp.zeros_like(dws_ref)
            dbs_ref[...] = jnp.zeros_like(dbs_ref)
            dsink_ref[...] = jnp.zeros_like(dsink_ref)
            dbqkv_ref[...] = jnp.zeros_like(dbqkv_ref)
            carry_ref[...] = jnp.zeros_like(carry_ref)

        def roped_k(b):
            rows = slice(b * T, (b + 1) * T)
            return _rope(p_ref[rows, OFF_K:OFF_K + 128], cos_ref[rows, :], sin_ref[rows, :], first_half)

        def block(b, k_prev, v_prev, has_prev, dk_next, dv_next):
            rows = slice(b * T, (b + 1) * T)
            xhat, rs = _layer_norm_stats(p_ref[rows, OFF_V:OFF_V + DG])
            lng = lng_ref[...]
            vn = xhat * lng + lnb_ref[...]
            for g in range(NG):
                sl = slice(128 * g, 128 * g + 128)
                vng = vn[:, sl].astype(BF16)
                mixed = _dot(wm_ref[g], vng, NN) + bs_ref[g]
                z = p_ref[rows, OFF_ZA + 128 * g:OFF_ZA + 128 * g + 128]
                u = p_ref[rows, OFF_U + 128 * g:OFF_U + 128 * g + 128]
                dy = dyc_ref[rows, sl]
                sg = _sigmoid(z)
                sa = z * sg
                dp_ref[rows, OFF_U + 128 * g:OFF_U + 128 * g + 128] = (dy * mixed * sa).astype(BF16)
                dp_ref[rows, OFF_ZA + 128 * g:OFF_ZA + 128 * g + 128] = (
                    dy * u * mixed * (sg * (1.0 + z * (1.0 - sg)))).astype(BF16)
                dm = dy * u * sa
                dmb = dm.astype(BF16)
                dvn_ref[rows, sl] = _dot(wmt_ref[g], dmb, NN)
                dws_ref[g] += jnp.where(tril, _dot(dmb, vng, NT), 0.0)
                dbs_ref[g] += jnp.sum(dm, axis=1, keepdims=True)
            dvn = dvn_ref[rows, :]
            dlng_ref[...] += jnp.sum(dvn * xhat, axis=0, keepdims=True)
            dlnb_ref[...] += jnp.sum(dvn, axis=0, keepdims=True)
            dxh = dvn * lng
            dv_g = rs * (dxh - _row_mean(dxh)
                         - xhat * _row_mean(dxh * xhat))
            dp_ref[rows, OFF_V:OFF_V + DG] = dv_g.astype(BF16)

            cosv, sinv = cos_ref[rows, :], sin_ref[rows, :]
            k_cur = roped_k(b)
            v_cur = p_ref[rows, OFF_VA:OFF_VA + 128]

            def q_slab(j):
                return q_ref[rows, 128 * j:128 * j + 128]

            def do_slab(j):
                zb = p_ref[rows, OFF_ZB + 128 * j:OFF_ZB + 128 * j + 128]
                return dyc_ref[rows, DG + 128 * j:DG + 128 * j + 128] * (zb * _sigmoid(zb))

            k_both = jnp.concatenate([k_prev, k_cur], axis=0)
            v_both = jnp.concatenate([v_prev, v_cur], axis=0)
            dk_both, dv_both = jnp.zeros((2 * T, 128), F32), jnp.zeros((2 * T, 128), F32)
            for grp in range(2):
                qs = _stack_heads(q_slab, grp, lo64)
                d_o = _stack_heads(do_slab, grp, lo64)
                dob = d_o.astype(BF16)
                kg, vg = _both_halves(k_both, grp).astype(BF16), _both_halves(v_both, grp).astype(BF16)
                p, ps = _band_softmax(qs, kg, own, has_prev, sink_ref[GH * T * grp:GH * T * (grp + 1), :])
                p_both = _unfold(p, own)
                o_st = _dot(p_both, vg, NN)
                delta = jnp.sum(d_o * o_st, axis=-1, keepdims=True)
                ds_both = _unfold(p * (_fold(_dot(dob, vg, NT), own) - delta), own)
                dq_st = _dot(ds_both, kg, NN) * (HD ** -0.5)
                dk_both = dk_both + _one_half(_dot(ds_both, qs, TN), grp)
                dv_both = dv_both + _one_half(_dot(p_both, dob, TN), grp)
                dsink_rows = ps * delta
                for hh in range(GH):
                    h = GH * grp + hh
                    dsink_ref[h:h + 1, :] += jnp.broadcast_to(
                        -jnp.sum(dsink_rows[hh * T:(hh + 1) * T], axis=0, keepdims=True), (1, 128))
                for jj in range(GH // 2):
                    c0 = 128 * (GH // 2 * grp + jj)
                    zb = p_ref[rows, OFF_ZB + c0:OFF_ZB + c0 + 128]
                    sg = _sigmoid(zb)
                    o = _unstack_heads(o_st, jj, lo64)
                    dp_ref[rows, OFF_ZB + c0:OFF_ZB + c0 + 128] = (
                        dyc_ref[rows, DG + c0:DG + c0 + 128] * o * (sg * (1.0 + zb * (1.0 - sg)))).astype(BF16)
                    dq = _unstack_heads(dq_st, jj, lo64)
                    dq_pre = dq * cosv - _rot_half(dq, first_half) * sinv
                    dp_ref[rows, OFF_Q + c0:OFF_Q + c0 + 128] = dq_pre.astype(BF16)
                    dbqkv_ref[:, c0:c0 + 128] += jnp.sum(dq_pre, axis=0, keepdims=True)
            dk_prev, dv_prev = dk_both[0:T], dv_both[0:T]
            dk_cur, dv_cur = dk_both[T:2 * T] + dk_next, dv_both[T:2 * T] + dv_next
            dk_pre = dk_cur * cosv - _rot_half(dk_cur, first_half) * sinv
            dp_ref[rows, OFF_K:OFF_K + 128] = dk_pre.astype(BF16)
            dp_ref[rows, OFF_VA:OFF_VA + 128] = dv_cur.astype(BF16)
            dbqkv_ref[:, 1024:1152] += jnp.sum(dk_pre, axis=0, keepdims=True)
            dbqkv_ref[:, 1152:1280] += jnp.sum(dv_cur, axis=0, keepdims=True)
            return dk_prev, dv_prev

        grads = carry_ref[:, 0:128], carry_ref[:, 128:256]
        for b in range(BPS - 1, 0, -1):
            prows = slice((b - 1) * T, b * T)
            grads = block(b, roped_k(b - 1), p_ref[prows, OFF_VA:OFF_VA + 128], None, *grads)
        k_prev = _rope(kvp_ref[:, 0:128], cosp_ref[...], sinp_ref[...], first_half)
        grads = block(0, k_prev, kvp_ref[:, 128:256], i < nt - 1, *grads)
        carry_ref[:, 0:128], carry_ref[:, 128:256] = grads
        pl.when(i == nt - 1)(exchange.finish)

    return pl.pallas_call(
        body, name="mid_bwd", grid=(nt,),
        in_specs=proj_specs + [pl.BlockSpec((BPS * T, D), lambda i: (tile(i), 0)),
                               pl.BlockSpec((BPS * T, NQ * HD), lambda i: (tile(i), 0))] + par_specs + [HBM_SPEC],
        out_specs=[pl.BlockSpec((BPS * T, D_IN), lambda i: (tile(i), 0)),
                   pl.BlockSpec((1, DG), const2), pl.BlockSpec((1, DG), const2),
                   pl.BlockSpec((NG, T, T), lambda i: (0, 0, 0)), pl.BlockSpec((NG, T, 1), lambda i: (0, 0, 0)),
                   pl.BlockSpec((NQ, 128), const2), pl.BlockSpec((1, D_QKV), const2), HBM_SPEC],
        out_shape=[jax.ShapeDtypeStruct((S, D_IN), BF16),
                   jax.ShapeDtypeStruct((1, DG), F32), jax.ShapeDtypeStruct((1, DG), F32),
                   jax.ShapeDtypeStruct((NG, T, T), F32), jax.ShapeDtypeStruct((NG, T, 1), F32),
                   jax.ShapeDtypeStruct((NQ, 128), F32), jax.ShapeDtypeStruct((1, D_QKV), F32),
                   jax.ShapeDtypeStruct(po.shape, BF16)],
        scratch_shapes=[pltpu.VMEM((T, 2 * T), F32), pltpu.VMEM((BPS * T, DG), F32),
                        pltpu.VMEM((NG, T, T), BF16), pltpu.VMEM((NG, T, T), BF16)] + OWNER_SEMS,
        compiler_params=_params(("arbitrary",)),
    )(proj, proj, dycat, q_roped, cos, sin, cos, sin, ln_g, ln_b, w_s, b_s, sinks, po)


def _outproj_loss(ycat, wo, x, target, g_post):
    S = ycat.shape[0]
    tm = _tile(S, 512)
    nt = S // tm
    n_part = 2 if tm % 32 == 0 else 1
    tp = tm // n_part
    const2 = lambda i: (0, 0)

    def body(yc_ref, w_ref, x_ref, t_ref, g_ref, dy_ref, dout_ref, loss_ref, dg_ref, lacc_ref):
        i = pl.program_id(0)

        @pl.when(i == 0)
        def _():
            dg_ref[...] = jnp.zeros_like(dg_ref)
            lacc_ref[...] = jnp.zeros_like(lacc_ref)

        g = g_ref[...]
        ys = [_dot(yc_ref[q * tp:(q + 1) * tp, :], w_ref[...], NN) for q in range(n_part)]
        for q, y in enumerate(ys):
            rows = slice(q * tp, (q + 1) * tp)
            r = lax.rsqrt(_row_mean(y * y) + EPS)
            yh = y * r
            diff = x_ref[rows, :] + yh * g - t_ref[rows, :]
            lacc_ref[...] += jnp.sum(diff * diff, axis=0, keepdims=True)
            dout = diff * (1.0 / D)
            dout_ref[rows, :] = dout
            dg_ref[...] += jnp.sum(dout * yh, axis=0, keepdims=True)
            dyh = dout * g
            dy_ref[rows, :] = (r * (dyh - yh * _row_mean(dyh * yh))).astype(BF16)

        @pl.when(i == nt - 1)
        def _():
            loss_ref[...] = jnp.broadcast_to(jnp.sum(lacc_ref[...], axis=1, keepdims=True) * (0.5 / D), (1, 128))

    row = lambda i: (i, 0)
    return pl.pallas_call(
        body, name="outproj_loss", grid=(nt,),
        in_specs=[pl.BlockSpec((tm, D), row), pl.BlockSpec((D, D), const2, pipeline_mode=pl.Buffered(1)),
                  pl.BlockSpec((tm, D), row), pl.BlockSpec((tm, D), row), pl.BlockSpec((1, D), const2)],
        out_specs=[pl.BlockSpec((tm, D), row), pl.BlockSpec((tm, D), row), pl.BlockSpec((1, 128), const2),
                   pl.BlockSpec((1, D), const2)],
        out_shape=[jax.ShapeDtypeStruct((S, D), BF16), jax.ShapeDtypeStruct((S, D), F32),
                   jax.ShapeDtypeStruct((1, 128), F32), jax.ShapeDtypeStruct((1, D), F32)],
        scratch_shapes=[pltpu.VMEM((1, D), F32)],
        compiler_params=_params(("arbitrary",)),
    )(ycat, wo, x, target, g_post)


def _dycat(dy, wo, dwo):
    S = dy.shape[0]
    tm = _tile(S, 1024)
    nt = S // tm

    def body(dy_ref, w_ref, dwo_ref, o_ref, r_ref, send_sems, recv_sems):
        i = pl.program_id(0)
        copies = _sibling_copies(dwo_ref, r_ref, W_OUT_ROWS, send_sems, recv_sems)

        @pl.when(i == 0)
        def _():
            for cp in copies:
                cp.start()

        o_ref[...] = _dot(dy_ref[...], w_ref[...], NT)

        @pl.when(i == nt - 1)
        def _():
            for cp in copies:
                cp.wait_recv()
            for cp in copies:
                cp.wait_send()

    return pl.pallas_call(
        body, name="dycat", grid=(nt,),
        in_specs=[pl.BlockSpec((tm, D), lambda i: (i, 0)),
                  pl.BlockSpec((D, D), lambda i: (0, 0), pipeline_mode=pl.Buffered(1)), HBM_SPEC],
        out_specs=[pl.BlockSpec((tm, D), lambda i: (i, 0)), HBM_SPEC],
        out_shape=[jax.ShapeDtypeStruct((S, D), F32), jax.ShapeDtypeStruct((N_CHIPS, W_OUT_ROWS, D), BF16)],
        scratch_shapes=[pltpu.SemaphoreType.DMA((N_CHIPS,)), pltpu.SemaphoreType.DMA((N_CHIPS,))],
        compiler_params=_params(("arbitrary",)),
    )(dy, wo, dwo)


def _block_segments(tm, n_tiles, half_rows):
    segs, sid = [], 0
    for t in range(n_tiles):
        segs.append([])
        for b in range(tm * t // half_rows, (tm * (t + 1) - 1) // half_rows + 1):
            segs[-1].append((sid, b, max(tm * t, half_rows * b), min(tm * (t + 1), half_rows * (b + 1))))
            sid += 1
    return segs, sid


def _matmul_tn(a, b, tm, name, mat_grads=None, half_rows=None):
    K, M = a.shape
    N = b.shape[1]
    tk = _tile(K, 2048)
    ni, nk = M // tm, K // tk
    hosting = mat_grads is not None

    def body(a_ref, b_ref, *rest):
        if hosting:
            g_in, (o_ref, tot_ref, r_ref, acc_ref, mbuf, send_sems, recv_sems, stage, seg_send, seg_recv) = (
                rest[:len(MAT)], rest[len(MAT):])
        else:
            o_ref, acc_ref = rest
        i, k = pl.program_id(0), pl.program_id(1)

        if hosting:
            gather = _SlotGather(mbuf, send_sems, recv_sems)

            @pl.when((i == 0) & (k == 0))
            def _():
                r0 = 0
                for q, (_, rows) in enumerate(MAT):
                    mbuf[gather.my_id, r0:r0 + rows, :] = g_in[q][...]
                    r0 += rows
                gather.start()

            pl.when((i == ni // 2) & (k == 0))(gather.pass_on)

            x_, y_, c = _mesh_pos()
            segs, _ = _block_segments(tm, ni, half_rows)

            def seg_copy(p, sid, blk, lo, hi):
                return pltpu.make_async_remote_copy(
                    src_ref=stage.at[p % 2, pl.ds(lo - tm * order[p], hi - lo), :],
                    dst_ref=r_ref.at[blk // 2, pl.ds(lo - half_rows * blk, hi - lo), :],
                    send_sem=seg_send.at[sid], recv_sem=seg_recv.at[sid],
                    device_id=(x_, y_, 1 - c), device_id_type=pl.DeviceIdType.MESH)

            def sent(p):
                for sid, blk, lo, hi in segs[order[p]]:
                    pl.when(c == 1 - blk % 2)(seg_copy(p, sid, blk, lo, hi).wait_send)

            for p in range(2, ni):
                pl.when((i == p) & (k == nk - 1))(lambda p=p: sent(p - 2))

        @pl.when(k == 0)
        def _():
            acc_ref[...] = jnp.zeros_like(acc_ref)

        acc_ref[...] += _dot(a_ref[...], b_ref[...], TN)

        @pl.when(k == nk - 1)
        def _():
            tile = acc_ref[...].astype(BF16)
            o_ref[...] = tile
            if hosting:
                stage[i % 2] = tile

        if hosting:
            for p in range(ni):
                @pl.when((i == p) & (k == nk - 1))
                def _(p=p):
                    for sid, blk, lo, hi in segs[order[p]]:
                        pl.when(c == 1 - blk % 2)(seg_copy(p, sid, blk, lo, hi).start)

            @pl.when((i == ni - 1) & (k == nk - 1))
            def _():
                gather.finish()
                tot_ref[...] = gather.total()
                sent(ni - 2)
                sent(ni - 1)
                for p in range(ni):
                    for sid, blk, lo, hi in segs[order[p]]:
                        pl.when(c == blk % 2)(seg_copy(p, sid, blk, lo, hi).wait_recv)

    mid = ni // 2
    order = [t for t in range(ni) if t != mid] + [mid] if hosting else list(range(ni))

    def tile_of(i):
        return jnp.where(i == ni - 1, mid, jnp.where(i < mid, i, i + 1)) if hosting else i

    whole = pl.BlockSpec(memory_space=pltpu.VMEM)
    in_specs = [pl.BlockSpec((tk, tm), lambda i, k: (k, tile_of(i))), pl.BlockSpec((tk, N), lambda i, k: (k, 0))]
    out_specs = [pl.BlockSpec((tm, N), lambda i, k: (tile_of(i), 0))]
    out_shape = [jax.ShapeDtypeStruct((M, N), BF16)]
    scratch = [pltpu.VMEM((tm, N), F32)]
    if hosting:
        in_specs += [whole] * len(MAT)
        n_segs = _block_segments(tm, ni, half_rows)[1]
        out_specs += [whole, HBM_SPEC]
        out_shape += [jax.ShapeDtypeStruct((MAT_ROWS, 128), F32),
                      jax.ShapeDtypeStruct((M // (2 * half_rows), half_rows, N), BF16)]
        scratch += [pltpu.VMEM((N_DEV, MAT_ROWS, 128), F32), pltpu.SemaphoreType.DMA((7,)),
                    pltpu.SemaphoreType.DMA((7,)), pltpu.VMEM((2, tm, N), BF16),
                    pltpu.SemaphoreType.DMA((n_segs,)), pltpu.SemaphoreType.DMA((n_segs,))]
    res = pl.pallas_call(
        body, name=name, grid=(ni, nk), in_specs=in_specs, out_specs=out_specs, out_shape=out_shape,
        scratch_shapes=scratch,
        compiler_params=_params(("arbitrary", "arbitrary") if hosting else ("parallel", "arbitrary")),
    )(a, b, *(mat_grads or ()))
    return res if hosting else res[0]


def _dh_prenorm_bwd(dproj, wt, x, dout, g_pre, pt):
    S = x.shape[0]
    tm, tk = _tile(S, 1024), 768
    ne = 4
    te = tm // ne
    ni, nk = S // tm, D_IN // tk
    n_mm = ni * nk
    n_steps = n_mm + ne
    mm_of = lambda t: jnp.minimum(t, n_mm - 1)

    def quarter_of(t):
        j, q = t // nk - 1, t % nk
        nxt = jnp.where(q < ne, j * ne + q, (j + 1) * ne)
        return jnp.clip(nxt, 0, ni * ne - 1)

    def body(dp_ref, w_ref, x_ref, dout_ref, g_ref, pt_ref, gx_ref, dg_ref, bt_ref, acc_ref,
             send_sems, recv_sems, local_sem):
        t = pl.program_id(0)
        exchange = _OwnerExchange(pt_ref, bt_ref, send_sems, recv_sems, local_sem)
        pl.when(t == 0)(exchange.start)

        @pl.when(t == 0)
        def _():
            dg_ref[...] = jnp.zeros_like(dg_ref)

        has_mm = t < n_mm
        has_q = (t >= nk) & (t % nk < ne)

        def matmul():
            slot = (mm_of(t) // nk) % 2
            old = jnp.where(mm_of(t) % nk > 0, acc_ref[slot], 0.0)
            acc_ref[slot] = old + _dot(dp_ref[...], w_ref[...], NN)

        def quarter():
            slot = (t // nk - 1) % 2
            dh = acc_ref[slot, pl.ds(pl.multiple_of((t % nk) * te, te), te), :]
            xv = x_ref[...]
            r = lax.rsqrt(_row_mean(xv * xv) + EPS)
            xh = xv * r
            dg_ref[...] += jnp.sum(dh * xh, axis=0, keepdims=True)
            dxh = dh * g_ref[...]
            gx_ref[...] = dout_ref[...] + r * (dxh - xh * _row_mean(dxh * xh))

        @pl.when(has_mm & has_q)
        def _():
            matmul()
            quarter()

        pl.when(has_mm & jnp.logical_not(has_q))(matmul)
        pl.when(jnp.logical_not(has_mm) & has_q)(quarter)
        pl.when(t == n_steps - 1)(exchange.finish)

    quarter_block = lambda t: (quarter_of(t), 0)
    return pl.pallas_call(
        body, name="dh_prenorm_bwd", grid=(n_steps,),
        in_specs=[pl.BlockSpec((tm, tk), lambda t: (mm_of(t) // nk, mm_of(t) % nk)),
                  pl.BlockSpec((tk, D), lambda t: (mm_of(t) % nk, 0)),
                  pl.BlockSpec((te, D), quarter_block), pl.BlockSpec((te, D), quarter_block),
                  pl.BlockSpec((1, D), lambda t: (0, 0)), HBM_SPEC],
        out_specs=[pl.BlockSpec((te, D), quarter_block), pl.BlockSpec((1, D), lambda t: (0, 0)), HBM_SPEC],
        out_shape=[jax.ShapeDtypeStruct((S, D), F32), jax.ShapeDtypeStruct((1, D), F32),
                   jax.ShapeDtypeStruct(pt.shape, BF16)],
        scratch_shapes=[pltpu.VMEM((2, tm, D), F32)] + OWNER_SEMS,
        compiler_params=_params(("arbitrary",)),
    )(dproj, wt, x, dout, g_pre, pt)


def _presum(c_arr, own, recv, half_rows):
    n_cols = own.shape[-1]
    own4 = own.reshape(N_CHIPS, 2, half_rows, n_cols)

    def body(c_ref, own_ref, recv_ref, o_ref):
        o_ref[...] = (own_ref[...].astype(F32) + recv_ref[...].astype(F32)).astype(BF16)

    return pl.pallas_call(
        body, name="presum_%d" % half_rows,
        grid_spec=pltpu.PrefetchScalarGridSpec(
            num_scalar_prefetch=1, grid=(N_CHIPS,),
            in_specs=[pl.BlockSpec((None, None, half_rows, n_cols), lambda j, c: (j, c[0], 0, 0)),
                      pl.BlockSpec((None, half_rows, n_cols), lambda j, c: (j, 0, 0))],
            out_specs=pl.BlockSpec((None, half_rows, n_cols), lambda j, c: (j, 0, 0))),
        out_shape=jax.ShapeDtypeStruct((N_CHIPS, half_rows, n_cols), BF16),
        compiler_params=_params(("parallel",)),
    )(c_arr, own4, recv)


def _sibling_presum(c_arr, dw, half_rows, name):
    n_cols = dw.shape[-1]
    dw4 = dw.reshape(N_CHIPS, 2, half_rows, n_cols)

    def body(c_ref, own_ref, dw_ref, o_ref, r_ref, stage, send_sems, recv_sems, load_sem):
        j = pl.program_id(0)
        copies = _sibling_copies(dw_ref, r_ref, half_rows, send_sems, recv_sems)

        @pl.when(j == 0)
        def _():
            for cp in copies:
                cp.start()

        for q in range(N_CHIPS):
            pl.when(j == q)(copies[q].wait_recv)
        load = pltpu.make_async_copy(r_ref.at[j], stage, load_sem)
        load.start()
        load.wait()
        o_ref[...] = (own_ref[...].astype(F32) + stage[...].astype(F32)).astype(BF16)

        @pl.when(j == N_CHIPS - 1)
        def _():
            for cp in copies:
                cp.wait_send()

    out, _ = pl.pallas_call(
        body, name=name,
        grid_spec=pltpu.PrefetchScalarGridSpec(
            num_scalar_prefetch=1, grid=(N_CHIPS,),
            in_specs=[pl.BlockSpec((None, None, half_rows, n_cols), lambda j, c: (j, c[0], 0, 0)), HBM_SPEC],
            out_specs=[pl.BlockSpec((None, half_rows, n_cols), lambda j, c: (j, 0, 0)), HBM_SPEC],
            scratch_shapes=[pltpu.VMEM((half_rows, n_cols), BF16), pltpu.SemaphoreType.DMA((N_CHIPS,)),
                            pltpu.SemaphoreType.DMA((N_CHIPS,)), pltpu.SemaphoreType.DMA(())]),
        out_shape=[jax.ShapeDtypeStruct((N_CHIPS, half_rows, n_cols), BF16),
                   jax.ShapeDtypeStruct((N_CHIPS, half_rows, n_cols), BF16)],
        compiler_params=_params(("arbitrary",)),
    )(c_arr, dw4, dw)
    return out


def _sum_chips(c_arr, parts, name):
    _, rows, n_cols = parts.shape
    nt = 2
    tr = rows // nt

    def body(c_ref, p_ref, o_ref):
        o_ref[...] = ((p_ref[0].astype(F32) + p_ref[1].astype(F32)) + p_ref[2].astype(F32)) + p_ref[3].astype(F32)

    return pl.pallas_call(
        body, name=name,
        grid_spec=pltpu.PrefetchScalarGridSpec(
            num_scalar_prefetch=1, grid=(nt,),
            in_specs=[pl.BlockSpec((N_CHIPS, tr, n_cols), lambda i, c: (0, i, 0))],
            out_specs=pl.BlockSpec((tr, n_cols), lambda i, c: (c[0] * nt + i, 0))),
        out_shape=jax.ShapeDtypeStruct((2 * rows, n_cols), F32),
        compiler_params=_params(("parallel",)),
    )(c_arr, parts)


def _adamw_math(w, g, m, v):
    mn = ADAM_B1 * m + (1.0 - ADAM_B1) * g
    vn = ADAM_B2 * v + (1.0 - ADAM_B2) * (g * g)
    m_hat = mn / (1.0 - ADAM_B1 ** ADAM_STEP)
    v_hat = vn / (1.0 - ADAM_B2 ** ADAM_STEP)
    return -ADAM_LR * (m_hat / (jnp.sqrt(v_hat) + ADAM_EPS) + ADAM_WD * w), mn, vn


def _adamw(w, g, m, v, name):
    R, C = w.shape
    tr = next((t for t in (256, 192, 128) if R % t == 0), R)

    def body(w_ref, g_ref, m_ref, v_ref, go_ref, d_ref, mo_ref, vo_ref):
        gv = g_ref[...]
        go_ref[...] = gv
        d_ref[...], mo_ref[...], vo_ref[...] = _adamw_math(w_ref[...], gv, m_ref[...], v_ref[...])

    spec = pl.BlockSpec((tr, C), lambda i: (i, 0))
    shp = jax.ShapeDtypeStruct((R, C), F32)
    return pl.pallas_call(
        body, name=name, grid=(R // tr,), in_specs=[spec] * 4, out_specs=[spec] * 4, out_shape=[shp] * 4,
        compiler_params=_params(("parallel",)),
    )(w, g, m, v)


HBM_SPEC = pl.BlockSpec(memory_space=pltpu.HBM)
GATHER_LOCAL_CHUNKS = 4
GATHER_SEMS = [pltpu.SemaphoreType.DMA((7,)), pltpu.SemaphoreType.DMA((7,)),
               pltpu.SemaphoreType.DMA((GATHER_LOCAL_CHUNKS,))]
OWNER_SEMS = [pltpu.SemaphoreType.DMA((3,)), pltpu.SemaphoreType.DMA((3,)), pltpu.SemaphoreType.DMA(())]


def _mesh_pos():
    return lax.axis_index("x"), lax.axis_index("y"), lax.axis_index("c")


class _RowGather:
    def __init__(self, src_ref, full_ref, rows, send_sems, recv_sems, local_sems):
        self.src, self.full, self.rows = src_ref, full_ref, rows
        self.send, self.recv, self.local = send_sems, recv_sems, local_sems
        x, y, c = _mesh_pos()
        self.c, self.me, self.sibling = c, (x, y, c), (x, y, 1 - c)
        self.chips = [(1 - x, y), (x, 1 - y), (1 - x, 1 - y)]

    def _block(self, pos):
        px, py, pc = pos
        return self.full.at[pl.ds(pl.multiple_of((4 * px + 2 * py + pc) * self.rows, 16), self.rows), :]

    def _copy(self, k, blk, to):
        return pltpu.make_async_remote_copy(
            src_ref=self.src if blk is self.me else self._block(blk), dst_ref=self._block(blk),
            send_sem=self.send.at[k], recv_sem=self.recv.at[k], device_id=to, device_id_type=MESH)

    def _mine(self):
        return _place_locally(self.src, self._block(self.me), self.local, GATHER_LOCAL_CHUNKS)

    def _first(self):
        return [self._copy(0, self.me, self.sibling)] + [
            self._copy(1 + j, self.me, (*chip, self.c)) for j, chip in enumerate(self.chips)]

    def start(self):
        for cp in self._first() + self._mine():
            cp.start()

    def _passed(self):
        return [self._copy(4 + j, (*chip, self.c), self.sibling) for j, chip in enumerate(self.chips)]

    def pass_on(self):
        for j, chip in enumerate(self.chips):
            self._copy(1 + j, (*chip, self.c), self.me).wait_recv()
            self._passed()[j].start()

    def finish(self):
        self._copy(0, self.sibling, self.me).wait_recv()
        for j, chip in enumerate(self.chips):
            self._copy(4 + j, (*chip, 1 - self.c), self.me).wait_recv()
        for cp in self._first() + self._passed():
            cp.wait_send()
        for cp in self._mine():
            cp.wait()


class _OwnerExchange:
    def __init__(self, src_ref, dst_ref, send_sems, recv_sems, local_sem):
        self.src, self.dst, self.send, self.recv, self.local = src_ref, dst_ref, send_sems, recv_sems, local_sem
        x, y, c = _mesh_pos()
        self.c, self.my_chip = c, 2 * x + y
        self.peers = [(1 - x, y), (x, 1 - y), (1 - x, 1 - y)]

    def _copies(self):
        local = pltpu.make_async_copy(self.src.at[self.my_chip], self.dst.at[self.my_chip], self.local)
        remote = [pltpu.make_async_remote_copy(
            src_ref=self.src.at[2 * px + py], dst_ref=self.dst.at[self.my_chip],
            send_sem=self.send.at[k], recv_sem=self.recv.at[k], device_id=(px, py, self.c), device_id_type=MESH)
            for k, (px, py) in enumerate(self.peers)]
        return local, remote

    def start(self):
        local, remote = self._copies()
        local.start()
        for cp in remote:
            cp.start()

    def finish(self):
        local, remote = self._copies()
        for cp in remote:
            cp.wait_recv()
        for cp in remote:
            cp.wait_send()
        local.wait()


def _sibling_copies(dw_ref, r_ref, rows, send_sems, recv_sems):
    x, y, c = _mesh_pos()
    return [pltpu.make_async_remote_copy(
        src_ref=dw_ref.at[pl.ds(pl.multiple_of((2 * j + (1 - c)) * rows, 16), rows), :], dst_ref=r_ref.at[j],
        send_sem=send_sems.at[j], recv_sem=recv_sems.at[j], device_id=(x, y, 1 - c), device_id_type=MESH)
        for j in range(N_CHIPS)]


PAIR_CHUNKS = 4


def _pair_halves(gt, go, vec_grads, loss_part):
    n_vec = len(VEC)

    def body(*refs):
        g_in, loss_in = refs[2:2 + n_vec], refs[2 + n_vec]
        gt_ref, go_ref, tot_v, vbuf, send_sems, recv_sems, vsend_sems, vrecv_sems = refs[3 + n_vec:]
        x, y, c = _mesh_pos()
        gather = _SlotGather(vbuf, vsend_sems, vrecv_sems)
        vbuf[gather.my_id] = jnp.zeros((VEC_ROWS, D), F32)
        for r, (_, n) in enumerate(VEC):
            vbuf[gather.my_id, r:r + 1, 0:n] = g_in[r][...]
        vbuf[gather.my_id, LOSS_ROW:LOSS_ROW + 1, 0:128] = loss_in[...]
        gather.start()
        copies = []
        for a, (ref, rows) in enumerate(((gt_ref, W_IN_ROWS), (go_ref, W_OUT_ROWS))):
            ch = rows // PAIR_CHUNKS
            for q in range(PAIR_CHUNKS):
                part = ref.at[pl.ds(pl.multiple_of(c * rows + q * ch, 8), ch), :]
                copies.append(pltpu.make_async_remote_copy(
                    src_ref=part, dst_ref=part, send_sem=send_sems.at[PAIR_CHUNKS * a + q],
                    recv_sem=recv_sems.at[PAIR_CHUNKS * a + q], device_id=(x, y, 1 - c), device_id_type=MESH))
        for cp in copies:
            cp.start()
        gather.pass_on()
        gather.finish()
        tot_v[...] = gather.total()
        for cp in copies:
            cp.wait_recv()
        for cp in copies:
            cp.wait_send()

    vmem = pl.BlockSpec(memory_space=pltpu.VMEM)
    return pl.pallas_call(
        body, name="pair_halves",
        in_specs=[HBM_SPEC, HBM_SPEC] + [vmem] * (n_vec + 1), out_specs=[HBM_SPEC, HBM_SPEC, vmem],
        out_shape=[jax.ShapeDtypeStruct(gt.shape, F32), jax.ShapeDtypeStruct(go.shape, F32),
                   jax.ShapeDtypeStruct((VEC_ROWS, D), F32)],
        input_output_aliases={0: 0, 1: 1},
        scratch_shapes=[pltpu.VMEM((N_DEV, VEC_ROWS, D), F32),
                        pltpu.SemaphoreType.DMA((2 * PAIR_CHUNKS,)), pltpu.SemaphoreType.DMA((2 * PAIR_CHUNKS,)),
                        pltpu.SemaphoreType.DMA((7,)), pltpu.SemaphoreType.DMA((7,))],
    )(gt, go, *vec_grads, loss_part)


VEC = (("g_pre", 2048), ("g_post", 2048), ("b_qkv", 1280), ("ln_v_g", 1024), ("ln_v_b", 1024), ("attn_sinks", 16))
VEC_ROWS = 8
LOSS_ROW = len(VEC)
MAT = (("w_spatial", NG * T), ("b_spatial", NG))
MAT_ROWS = sum(r for _, r in MAT)


class _SlotGather:
    def __init__(self, buf, send_sems, recv_sems):
        self.buf, self.send, self.recv = buf, send_sems, recv_sems
        x, y, c = _mesh_pos()
        self.c, self.me, self.sibling, self.my_id = c, (x, y, c), (x, y, 1 - c), 4 * x + 2 * y + c
        self.chips = [(1 - x, y), (x, 1 - y), (1 - x, 1 - y)]

    def _copy(self, k, blk, to):
        px, py, pc = blk
        slot = self.buf.at[4 * px + 2 * py + pc]
        return pltpu.make_async_remote_copy(
            src_ref=slot, dst_ref=slot, send_sem=self.send.at[k], recv_sem=self.recv.at[k],
            device_id=to, device_id_type=MESH)

    def _first(self):
        return [self._copy(0, self.me, self.sibling)] + [
            self._copy(1 + j, self.me, (*chip, self.c)) for j, chip in enumerate(self.chips)]

    def start(self):
        for cp in self._first():
            cp.start()

    def _passed(self):
        return [self._copy(4 + j, (*chip, self.c), self.sibling) for j, chip in enumerate(self.chips)]

    def pass_on(self):
        for j, chip in enumerate(self.chips):
            self._copy(1 + j, (*chip, self.c), self.me).wait_recv()
            self._passed()[j].start()

    def finish(self):
        self._copy(0, self.sibling, self.me).wait_recv()
        for j, chip in enumerate(self.chips):
            self._copy(4 + j, (*chip, 1 - self.c), self.me).wait_recv()
        for cp in self._first() + self._passed():
            cp.wait_send()

    def total(self):
        t = self.buf[0]
        for d in range(1, N_DEV):
            t = t + self.buf[d]
        return t


def _small_update(vec_total, mat_total, vec_state, mat_state):
    n_vec, n_mat = len(VEC), len(MAT)
    n_par = n_vec + n_mat
    n_in = 2 + 3 * n_par

    def body(*refs):
        tot_v, tot_m = refs[0], refs[1]
        st_in = refs[2:n_in]
        outs, loss_out = refs[n_in:n_in + 4 * n_par], refs[n_in + 4 * n_par]
        loss_out[...] = tot_v[LOSS_ROW:LOSS_ROW + 1, 0:128]
        r0 = 0
        for q in range(n_par):
            if q < n_vec:
                g = tot_v[q:q + 1, 0:VEC[q][1]]
            else:
                rows = MAT[q - n_vec][1]
                g = tot_m[r0:r0 + rows, :]
                r0 += rows
            w, m, v = (st_in[3 * q + t][...] for t in range(3))
            outs[4 * q][...] = g
            outs[4 * q + 1][...], outs[4 * q + 2][...], outs[4 * q + 3][...] = _adamw_math(w, g, m, v)

    state = [a for wmv in list(vec_state) + list(mat_state) for a in wmv]
    vmem = pl.BlockSpec(memory_space=pltpu.VMEM)
    out_shape = [jax.ShapeDtypeStruct(wmv[0].shape, F32) for wmv in list(vec_state) + list(mat_state) for _ in range(4)]
    out_shape.append(jax.ShapeDtypeStruct((1, 128), F32))
    res = pl.pallas_call(
        body, name="small_update",
        in_specs=[vmem] * n_in, out_specs=[vmem] * len(out_shape), out_shape=out_shape,
        compiler_params=pltpu.CompilerParams(vmem_limit_bytes=VMEM_LIMIT),
    )(vec_total, mat_total, *state)
    return [res[4 * q:4 * q + 4] for q in range(n_par)], res[-1]


def kernel(x, positions, g_pre, w_in, b_qkv, ln_v_g, ln_v_b, w_spatial, b_spatial, attn_sinks, w_out, g_post, loss_target, m_g_pre, m_w_in, m_b_qkv, m_ln_v_g, m_ln_v_b, m_w_spatial, m_b_spatial, m_attn_sinks, m_w_out, m_g_post, v_g_pre, v_w_in, v_b_qkv, v_ln_v_g, v_ln_v_b, v_w_spatial, v_b_spatial, v_attn_sinks, v_w_out, v_g_post):
    S = x.shape[1]
    c = lax.axis_index("c")
    c_arr = jnp.reshape(c, (1,)).astype(jnp.int32)
    x2 = x[0]
    target = loss_target[0]
    pos = positions.reshape(S, 1)
    half = HD // 2
    inv_freq = ROPE_THETA ** (-jnp.arange(half, dtype=F32) * (2.0 / HD))
    invf = jnp.tile(inv_freq, 128 // half).reshape(1, 128)
    bias = jnp.concatenate([jnp.zeros((OFF_Q,), F32), b_qkv[0], jnp.zeros((D_IN - OFF_ZB,), F32)]).reshape(1, D_IN)
    b_s_col = b_spatial[0].reshape(NG, T, 1)
    sinks = jnp.repeat(attn_sinks[0], T).reshape(NQ * T, 1)

    chip = 2 * lax.axis_index("x") + lax.axis_index("y")
    wt_part = lax.dynamic_slice_in_dim(w_in[0].T.astype(BF16), c * W_IN_ROWS, W_IN_ROWS, axis=0)
    wo_part = lax.dynamic_slice_in_dim(w_out[0].astype(BF16), c * W_OUT_ROWS, W_OUT_ROWS, axis=0)
    sched = jnp.asarray(PROJ_SCHEDULE, jnp.int32)[chip]

    h, proj, wt, cos, sin = _prenorm_inproj(sched, x2, g_pre, bias, wt_part, pos, invf)
    ycat, wo, q_roped = _mid_fwd(proj, cos, sin, ln_v_g, ln_v_b, w_spatial[0], b_s_col, sinks, wo_part)
    dy, dout, loss_part, dg_post = _outproj_loss(ycat, wo, x2, target, g_post)

    dwo = _matmul_tn(ycat, dy, 1024, "dw_out")
    dycat, ro = _dycat(dy, wo, dwo)
    po = _presum(c_arr, dwo, ro, W_OUT_ROWS)
    dproj, dln_g, dln_b, dws, dbs, dsink, dbqkv, bo = _mid_bwd(
        proj, dycat, q_roped, cos, sin, ln_v_g, ln_v_b, w_spatial[0], b_s_col, sinks, po)
    dwt, mat_total, rt = _matmul_tn(dproj, h, 768, "dw_in_t",
                                    mat_grads=[dws.reshape(NG * T, T), dbs.reshape(NG, T)], half_rows=W_IN_ROWS)
    pt = _presum(c_arr, dwt, rt, W_IN_ROWS)
    grad_x, dg_pre, bt = _dh_prenorm_bwd(dproj, wt, x2, dout, g_pre, pt)
    grads = {"g_pre": dg_pre, "g_post": dg_post, "b_qkv": dbqkv, "ln_v_g": dln_g, "ln_v_b": dln_b,
             "attn_sinks": dsink[:, 0].reshape(1, NQ)}
    gt, go, vec_total = _pair_halves(_sum_chips(c_arr, bt, "sum_chips_in"), _sum_chips(c_arr, bo, "sum_chips_out"),
                                     [grads[n] for n, _ in VEC], loss_part)

    g_w_in, d_w_in, nm_w_in, nv_w_in = (a.T for a in _adamw(w_in[0].T, gt, m_w_in[0].T, v_w_in[0].T, "adamw_w_in"))
    g_w_out, d_w_out, nm_w_out, nv_w_out = _adamw(w_out[0], go, m_w_out[0], v_w_out[0], "adamw_w_out")

    state = {"g_pre": (g_pre, m_g_pre, v_g_pre), "g_post": (g_post, m_g_post, v_g_post),
             "b_qkv": (b_qkv, m_b_qkv, v_b_qkv), "ln_v_g": (ln_v_g, m_ln_v_g, v_ln_v_g),
             "ln_v_b": (ln_v_b, m_ln_v_b, v_ln_v_b), "attn_sinks": (attn_sinks, m_attn_sinks, v_attn_sinks),
             "w_spatial": tuple(a.reshape(NG * T, T) for a in (w_spatial, m_w_spatial, v_w_spatial)),
             "b_spatial": tuple(a.reshape(NG, T) for a in (b_spatial, m_b_spatial, v_b_spatial))}
    results, loss = _small_update(vec_total, mat_total, [state[n] for n, _ in VEC], [state[n] for n, _ in MAT])
    small = {n: [a.reshape(w.shape) for a in res]
             for (n, _), res, w in zip(VEC + MAT, results, [state[n][0] for n, _ in VEC + MAT])}
    small["w_spatial"] = [a.reshape(w_spatial.shape) for a in small["w_spatial"]]
    small["b_spatial"] = [a.reshape(b_spatial.shape) for a in small["b_spatial"]]
    big = {"w_in": [a[None] for a in (g_w_in, d_w_in, nm_w_in, nv_w_in)],
           "w_out": [a[None] for a in (g_w_out, d_w_out, nm_w_out, nv_w_out)]}
    order = ("g_pre", "w_in", "b_qkv", "ln_v_g", "ln_v_b", "w_spatial", "b_spatial", "attn_sinks", "w_out", "g_post")
    leaves = {**small, **big}
    return (loss[0, 0], grad_x[None], *[leaves[n][t] for t in range(4) for n in order])
```

```python
import jax
import jax.numpy as jnp
from jax import lax
from jax.experimental import pallas as pl
from jax.experimental.pallas import tpu as pltpu

F32 = jnp.float32
BF16 = jnp.bfloat16
MESH = pl.DeviceIdType.MESH

D = 2048
DG = 1024
T = 128
NG = 8
HD = 64
NQ = 16
D_IN = 5376
OFF_U, OFF_V, OFF_ZA, OFF_Q, OFF_K, OFF_VA, OFF_ZB = 0, 1024, 2048, 3072, 4096, 4224, 4352
D_QKV = 1280
EPS = 1e-6
ROPE_THETA = 10000.0
N_CHIPS = 4
N_DEV = 8
W_IN_ROWS = D_IN // N_DEV
W_OUT_ROWS = D // N_DEV

ADAM_LR, ADAM_B1, ADAM_B2, ADAM_EPS, ADAM_WD, ADAM_STEP = 0.001, 0.9, 0.999, 1e-08, 0.01, 10

VMEM_LIMIT = 56 * 1024 * 1024


def _tile(n, pref):
    return pref if n % pref == 0 else n


def _params(sem=None, vmem=VMEM_LIMIT):
    return pltpu.CompilerParams(dimension_semantics=sem, vmem_limit_bytes=vmem)


def _sigmoid(z):
    return 1.0 / (1.0 + jnp.exp(-z))


def _row_mean(v):
    return jnp.mean(v, axis=-1, keepdims=True)


def _dot(a, b, dims):
    return lax.dot_general(a, b, (dims, ((), ())), preferred_element_type=F32)


NN = ((1,), (0,))
NT = ((1,), (1,))
TN = ((0,), (0,))


PROJ_TN = 768
N_PROJ_TILES = D_IN // PROJ_TN
PROJ_SCHEDULE = ((0, 1, 2, 4, 3, 6, 5, 5), (2, 1, 0, 6, 4, 3, 5, 4), (4, 5, 6, 0, 2, 1, 3, 4), (6, 5, 4, 2, 3, 0, 1, 5))


def _place_locally(src_ref, dst_rows_ref, sems, n_chunks):
    ch = src_ref.shape[0] // n_chunks
    return [pltpu.make_async_copy(src_ref.at[pl.ds(q * ch, ch), :], dst_rows_ref.at[pl.ds(q * ch, ch), :],
                                  sems.at[q]) for q in range(n_chunks)]


def _prenorm_inproj(sched, x, g, bias, wt_part, pos, invf):
    S = x.shape[0]
    tp, tm = _tile(S, 512), _tile(S, 1024)
    n_pre, ns = S // tp, S // tm
    n_steps = n_pre + N_PROJ_TILES * ns
    pos_of = lambda i: jnp.maximum(i - n_pre, 0) // ns
    row_of = lambda i: jnp.maximum(i - n_pre, 0) % ns

    def body(sched_ref, x_ref, g_ref, b_ref, wpart_ref, pos_ref, invf_ref, h_ref, proj_ref, wt_ref, cos_ref, sin_ref,
             h_all, w_tile, stage, send_sems, recv_sems, w_sems, local_sems):
        i = pl.program_id(0)
        x_, y_, c = _mesh_pos()
        me, sibling = (x_, y_, c), (x_, y_, 1 - c)
        chips = [(1 - x_, y_), (x_, 1 - y_), (1 - x_, 1 - y_)]

        def block(pos):
            px, py, pc = pos
            return wt_ref.at[pl.ds(pl.multiple_of((4 * px + 2 * py + pc) * W_IN_ROWS, 16), W_IN_ROWS), :]

        def copy(k, blk, to):
            return pltpu.make_async_remote_copy(
                src_ref=wpart_ref if blk is me else block(blk), dst_ref=block(blk),
                send_sem=send_sems.at[k], recv_sem=recv_sems.at[k], device_id=to, device_id_type=MESH)

        stage_in = pltpu.make_async_copy(wpart_ref, stage, local_sems.at[0])
        stage_out = pltpu.make_async_copy(stage, block(me), local_sems.at[1])

        relay = (jnp.where(c == 0, x_, 1 - x_), jnp.where(c == 0, 1 - y_, y_))
        relayed = (jnp.where(c == 0, 1 - x_, x_), jnp.where(c == 0, y_, 1 - y_))

        early = c == y_
        y_send = copy(2, me, (*chips[1], c))

        def own_sends():
            return [copy(0, me, sibling), y_send, copy(1, me, (*chips[0], c))]

        def passed_on():
            return [copy(4, (*chips[0], c), sibling), copy(5, (*chips[1], c), sibling),
                    copy(3, (*relayed, c), (*relay, c))]

        def early_block_arrives():
            @pl.when(early)
            def _():
                copy(2, (*chips[1], c), me).wait_recv()
                copy(5, (*chips[1], c), sibling).start()
                y_send.start()

            pl.when(jnp.logical_not(early))(lambda: copy(5, (*chips[1], 1 - c), me).wait_recv())

        def neighbours_arrive():
            copy(1, (*chips[0], c), me).wait_recv()
            copy(4, (*chips[0], c), sibling).start()

            @pl.when(early)
            def _():
                copy(3, (*relayed, c), (*relay, c)).start()
                copy(5, (*chips[1], 1 - c), me).wait_recv()

            @pl.when(jnp.logical_not(early))
            def _():
                copy(2, (*chips[1], c), me).wait_recv()
                copy(5, (*chips[1], c), sibling).start()
                copy(3, (*relayed, c), (*relay, c)).start()

        def diagonal_arrives():
            copy(3, (*chips[2], c), me).wait_recv()
            copy(6, (*chips[2], c), sibling).start()
            copy(6, (*chips[2], 1 - c), me).wait_recv()

        def tile_load(p):
            slot = p % 2
            rows = wt_ref.at[pl.ds(pl.multiple_of(sched_ref[p] * PROJ_TN, 16), PROJ_TN), :]
            return pltpu.make_async_copy(rows, w_tile.at[slot], w_sems.at[slot])

        def prepare(p):
            p = jnp.asarray(p, jnp.int32)

            @pl.when(p == 0)
            def _():
                copy(0, sibling, me).wait_recv()
                stage_out.wait()

            pl.when(p == 1)(early_block_arrives)
            pl.when(p == 2)(neighbours_arrive)
            pl.when(p == 3)(lambda: copy(4, (*chips[0], 1 - c), me).wait_recv())
            pl.when(p == sched_ref[N_PROJ_TILES])(diagonal_arrives)
            tile_load(p).start()

        @pl.when(i == 0)
        def _():
            stage_in.start()
            copy(0, me, sibling).start()
            copy(1, me, (*chips[0], c)).start()
            pl.when(jnp.logical_not(early))(y_send.start)
            stage_in.wait()
            stage_out.start()

        @pl.when(i < n_pre)
        def _():
            xv = x_ref[...]
            r = lax.rsqrt(_row_mean(xv * xv) + EPS)
            hv = (xv * r * g_ref[...]).astype(BF16)
            h_ref[...] = hv
            h_all[pl.ds(pl.multiple_of(i * tp, tp), tp), :] = hv
            ang = pos_ref[...].astype(F32) * invf_ref[...]
            cos_ref[...] = jnp.cos(ang)
            sin_ref[...] = jnp.sin(ang)

        pl.when(i == n_pre - 1)(lambda: prepare(0))

        @pl.when(i >= n_pre)
        def _():
            p, s = pos_of(i), row_of(i)
            pl.when(s == 0)(lambda: tile_load(p).wait())
            pl.when((s == ns - 1) & (p < N_PROJ_TILES - 1))(lambda: prepare(p + 1))
            hv = h_all[pl.ds(pl.multiple_of(s * tm, tm), tm), :]
            proj_ref[...] = _dot(hv, w_tile[p % 2], NT) + b_ref[...]

        @pl.when(i == n_steps - 1)
        def _():
            for cp in own_sends() + passed_on() + [copy(6, (*chips[2], c), sibling)]:
                cp.wait_send()

    return pl.pallas_call(
        body, name="prenorm_inproj",
        grid_spec=pltpu.PrefetchScalarGridSpec(
            num_scalar_prefetch=1, grid=(n_steps,),
            in_specs=[pl.BlockSpec((tp, D), lambda i, sc: (jnp.minimum(i, n_pre - 1), 0)),
                      pl.BlockSpec((1, D), lambda i, sc: (0, 0)),
                      pl.BlockSpec((1, PROJ_TN), lambda i, sc: (0, sc[pos_of(i)])),
                      HBM_SPEC,
                      pl.BlockSpec((tp, 1), lambda i, sc: (jnp.minimum(i, n_pre - 1), 0)),
                      pl.BlockSpec((1, 128), lambda i, sc: (0, 0))],
            out_specs=[pl.BlockSpec((tp, D), lambda i, sc: (jnp.minimum(i, n_pre - 1), 0)),
                       pl.BlockSpec((tm, PROJ_TN), lambda i, sc: (row_of(i), sc[pos_of(i)])),
                       HBM_SPEC,
                       pl.BlockSpec((tp, 128), lambda i, sc: (jnp.minimum(i, n_pre - 1), 0)),
                       pl.BlockSpec((tp, 128), lambda i, sc: (jnp.minimum(i, n_pre - 1), 0))],
            scratch_shapes=[pltpu.VMEM((S, D), BF16), pltpu.VMEM((2, PROJ_TN, D), BF16),
                            pltpu.VMEM((W_IN_ROWS, D), BF16),
                            pltpu.SemaphoreType.DMA((7,)), pltpu.SemaphoreType.DMA((7,)),
                            pltpu.SemaphoreType.DMA((2,)), pltpu.SemaphoreType.DMA((2,))]),
        out_shape=[jax.ShapeDtypeStruct((S, D), BF16), jax.ShapeDtypeStruct((S, D_IN), F32),
                   jax.ShapeDtypeStruct((D_IN, D), BF16),
                   jax.ShapeDtypeStruct((S, 128), F32), jax.ShapeDtypeStruct((S, 128), F32)],
        compiler_params=_params(("arbitrary",)),
    )(sched, x, g, bias, wt_part, pos, invf)


def _rot_half(xs, first_half):
    return jnp.where(first_half, -pltpu.roll(xs, 96, 1), pltpu.roll(xs, 32, 1))


GH = NQ // 2


MASKED = -1e30


def _attn_consts():
    lane = lax.broadcasted_iota(jnp.int32, (T, 128), 1)
    row = lax.broadcasted_iota(jnp.int32, (GH * T, T), 0) & (T - 1)
    on_diag_or_below = row >= lax.broadcasted_iota(jnp.int32, (GH * T, T), 1)
    return (lane & (HD - 1)) < (HD // 2), lane < HD, on_diag_or_below


def _stack_heads(slab_fn, grp, lo64):
    blocks = []
    for jj in range(GH // 2):
        s = slab_fn(GH // 2 * grp + jj)
        blocks += [jnp.where(lo64, s, 0.0), jnp.where(lo64, 0.0, s)]
    return jnp.concatenate(blocks, axis=0)


def _unstack_heads(stacked, jj, lo64):
    return jnp.where(lo64, stacked[(2 * jj) * T:(2 * jj + 1) * T], stacked[(2 * jj + 1) * T:(2 * jj + 2) * T])


def _both_halves(kv, grp):
    lo = lax.broadcasted_iota(jnp.int32, kv.shape, 1) < HD
    swapped = pltpu.roll(kv, HD, 1)
    return jnp.where(lo, kv, swapped) if grp == 0 else jnp.where(lo, swapped, kv)


def _one_half(acc, grp):
    lo = lax.broadcasted_iota(jnp.int32, acc.shape, 1) < HD
    return jnp.where(lo if grp == 0 else jnp.logical_not(lo), acc + pltpu.roll(acc, HD, 1), 0.0)


def _layer_norm_stats(v):
    mu = _row_mean(v)
    xc = v - mu
    var = _row_mean(xc * xc)
    rs = lax.rsqrt(var + EPS)
    return xc * rs, rs


def _fold(both, own):
    return jnp.where(own, both[:, T:2 * T], both[:, 0:T])


def _unfold(p, own):
    return jnp.concatenate([jnp.where(own, 0.0, p), jnp.where(own, p, 0.0)], axis=1).astype(BF16)


def _band_softmax(q_scaled, k_both, own, has_prev, sink):
    s_both = _dot(q_scaled, k_both, NT)
    s_prev = s_both[:, 0:T]
    if has_prev is not None:
        s_prev = s_prev + jnp.where(has_prev, 0.0, MASKED)
    s = jnp.where(own, s_both[:, T:2 * T], s_prev)
    m = jnp.maximum(jnp.max(s, axis=-1, keepdims=True), sink)
    e = jnp.exp(s - m)
    es = jnp.exp(sink - m)
    inv = 1.0 / (jnp.sum(e, axis=-1, keepdims=True) + es)
    return e * inv, es * inv


BPS_FWD = 4
BPS_BWD = 4


def _mid_specs(nt, rev, bps):
    tile = (lambda i: nt - 1 - i) if rev else (lambda i: i)
    prev = lambda i: jnp.maximum(bps * tile(i) - 1, 0)
    rows = bps * T
    return tile, [
        pl.BlockSpec((rows, D_IN), lambda i: (tile(i), 0)),
        pl.BlockSpec((T, 2 * T), lambda i: (prev(i), OFF_K // (2 * T))),
    ], [
        pl.BlockSpec((rows, 128), lambda i: (tile(i), 0)),
        pl.BlockSpec((rows, 128), lambda i: (tile(i), 0)),
        pl.BlockSpec((T, 128), lambda i: (prev(i), 0)),
        pl.BlockSpec((T, 128), lambda i: (prev(i), 0)),
        pl.BlockSpec((1, DG), lambda i: (0, 0)),
        pl.BlockSpec((1, DG), lambda i: (0, 0)),
        pl.BlockSpec((NG, T, T), lambda i: (0, 0, 0)),
        pl.BlockSpec((NG, T, 1), lambda i: (0, 0, 0)),
        pl.BlockSpec((NQ * T, 1), lambda i: (0, 0)),
    ]


def _rope(xs, cosv, sinv, first_half):
    return xs * cosv + _rot_half(xs, first_half) * sinv


def _mid_fwd(proj, cos, sin, ln_g, ln_b, w_s, b_s, sinks, wo_part):
    S = proj.shape[0]
    BPS = BPS_FWD if S % (BPS_FWD * T) == 0 else 1
    nt = S // (BPS * T)
    tile, proj_specs, par_specs = _mid_specs(nt, False, BPS)

    def body(p_ref, kvp_ref, cos_ref, sin_ref, cosp_ref, sinp_ref, lng_ref, lnb_ref, ws_ref, bs_ref, sink_ref,
             wpart_ref, y_ref, wo_ref, q_ref, wm_ref, send_sems, recv_sems, local_sems):
        i = pl.program_id(0)
        first_half, lo64, own = _attn_consts()
        gather = _RowGather(wpart_ref, wo_ref, W_OUT_ROWS, send_sems, recv_sems, local_sems)
        pl.when(i == 0)(gather.start)
        pl.when(i == nt // 2)(gather.pass_on)

        @pl.when(i == 0)
        def _():
            tril = lax.broadcasted_iota(jnp.int32, (T, T), 0) >= lax.broadcasted_iota(jnp.int32, (T, T), 1)
            for g in range(NG):
                wm_ref[g] = jnp.where(tril, ws_ref[g], 0.0).astype(BF16)

        def block(b, k_prev, v_prev, has_prev):
            rows = slice(b * T, (b + 1) * T)
            xhat, _ = _layer_norm_stats(p_ref[rows, OFF_V:OFF_V + DG])
            vn = xhat * lng_ref[...] + lnb_ref[...]
            for g in range(NG):
                sl = slice(128 * g, 128 * g + 128)
                mixed = _dot(wm_ref[g], vn[:, sl].astype(BF16), NN) + bs_ref[g]
                z = p_ref[rows, OFF_ZA + 128 * g:OFF_ZA + 128 * g + 128]
                u = p_ref[rows, OFF_U + 128 * g:OFF_U + 128 * g + 128]
                y_ref[rows, sl] = (u * mixed * (z * _sigmoid(z))).astype(BF16)

            cosv, sinv = cos_ref[rows, :], sin_ref[rows, :]
            k_cur = _rope(p_ref[rows, OFF_K:OFF_K + 128], cosv, sinv, first_half)
            v_cur = p_ref[rows, OFF_VA:OFF_VA + 128]

            def q_slab(j):
                q = (_rope(p_ref[rows, OFF_Q + 128 * j:OFF_Q + 128 * j + 128], cosv, sinv, first_half)
                     * (HD ** -0.5)).astype(BF16)
                q_ref[rows, 128 * j:128 * j + 128] = q
                return q

            k_both = jnp.concatenate([k_prev, k_cur], axis=0)
            v_both = jnp.concatenate([v_prev, v_cur], axis=0)
            for grp in range(2):
                qs = _stack_heads(q_slab, grp, lo64)
                kg, vg = _both_halves(k_both, grp).astype(BF16), _both_halves(v_both, grp).astype(BF16)
                p, _ = _band_softmax(qs, kg, own, has_prev, sink_ref[GH * T * grp:GH * T * (grp + 1), :])
                o_st = _dot(_unfold(p, own), vg, NN)
                for jj in range(GH // 2):
                    c0 = 128 * (GH // 2 * grp + jj)
                    zb = p_ref[rows, OFF_ZB + c0:OFF_ZB + c0 + 128]
                    o = _unstack_heads(o_st, jj, lo64)
                    y_ref[rows, DG + c0:DG + c0 + 128] = (o * (zb * _sigmoid(zb))).astype(BF16)
            return k_cur, v_cur

        k_prev = _rope(kvp_ref[:, 0:128], cosp_ref[...], sinp_ref[...], first_half)
        kv = block(0, k_prev, kvp_ref[:, 128:256], i > 0)
        for b in range(1, BPS):
            kv = block(b, *kv, None)
        pl.when(i == nt - 1)(gather.finish)

    return pl.pallas_call(
        body, name="mid_fwd", grid=(nt,),
        in_specs=proj_specs + par_specs + [HBM_SPEC],
        out_specs=[pl.BlockSpec((BPS * T, D), lambda i: (tile(i), 0)), HBM_SPEC,
                   pl.BlockSpec((BPS * T, NQ * HD), lambda i: (tile(i), 0))],
        out_shape=[jax.ShapeDtypeStruct((S, D), BF16), jax.ShapeDtypeStruct((D, D), BF16),
                   jax.ShapeDtypeStruct((S, NQ * HD), BF16)],
        scratch_shapes=[pltpu.VMEM((NG, T, T), BF16)] + GATHER_SEMS,
        compiler_params=_params(("arbitrary",)),
    )(proj, proj, cos, sin, cos, sin, ln_g, ln_b, w_s, b_s, sinks, wo_part)


def _mid_bwd(proj, dycat, q_roped, cos, sin, ln_g, ln_b, w_s, b_s, sinks, po):
    S = proj.shape[0]
    BPS = BPS_BWD if S % (BPS_BWD * T) == 0 else 1
    nt = S // (BPS * T)
    tile, proj_specs, par_specs = _mid_specs(nt, True, BPS)
    const2 = lambda i: (0, 0)

    def body(p_ref, kvp_ref, dyc_ref, q_ref, cos_ref, sin_ref, cosp_ref, sinp_ref, lng_ref, lnb_ref, ws_ref, bs_ref,
             sink_ref, po_ref, dp_ref, dlng_ref, dlnb_ref, dws_ref, dbs_ref, dsink_ref, dbqkv_ref, bo_ref,
             carry_ref, dvn_ref, wm_ref, wmt_ref, send_sems, recv_sems, local_sem):
        i = pl.program_id(0)
        first_half, lo64, own = _attn_consts()
        tril = lax.broadcasted_iota(jnp.int32, (T, T), 0) >= lax.broadcasted_iota(jnp.int32, (T, T), 1)
        exchange = _OwnerExchange(po_ref, bo_ref, send_sems, recv_sems, local_sem)
        pl.when(i == 0)(exchange.start)

        @pl.when(i == 0)
        def _():
            for g in range(NG):
                wm = jnp.where(tril, ws_ref[g], 0.0)
                wm_ref[g] = wm.astype(BF16)
                wmt_ref[g] = wm.T.astype(BF16)
            dlng_ref[...] = jnp.zeros_like(dlng_ref)
            dlnb_ref[...] = jnp.zeros_like(dlnb_ref)
            dws_ref[...] = jnp.zeros_like(dws_ref)
            dbs_ref[...] = jnp.zeros_like(dbs_ref)
            dsink_ref[...] = jnp.zeros_like(dsink_ref)
            dbqkv_ref[...] = jnp.zeros_like(dbqkv_ref)
            carry_ref[...] = jnp.zeros_like(carry_ref)

        def roped_k(b):
            rows = slice(b * T, (b + 1) * T)
            return _rope(p_ref[rows, OFF_K:OFF_K + 128], cos_ref[rows, :], sin_ref[rows, :], first_half)

        def block(b, k_prev, v_prev, has_prev, dk_next, dv_next):
            rows = slice(b * T, (b + 1) * T)
            xhat, rs = _layer_norm_stats(p_ref[rows, OFF_V:OFF_V + DG])
            lng = lng_ref[...]
            vn = xhat * lng + lnb_ref[...]
            for g in range(NG):
                sl = slice(128 * g, 128 * g + 128)
                vng = vn[:, sl].astype(BF16)
                mixed = _dot(wm_ref[g], vng, NN) + bs_ref[g]
                z = p_ref[rows, OFF_ZA + 128 * g:OFF_ZA + 128 * g + 128]
                u = p_ref[rows, OFF_U + 128 * g:OFF_U + 128 * g + 128]
                dy = dyc_ref[rows, sl]
                sg = _sigmoid(z)
                sa = z * sg
                dp_ref[rows, OFF_U + 128 * g:OFF_U + 128 * g + 128] = (dy * mixed * sa).astype(BF16)
                dp_ref[rows, OFF_ZA + 128 * g:OFF_ZA + 128 * g + 128] = (
                    dy * u * mixed * (sg * (1.0 + z * (1.0 - sg)))).astype(BF16)
                dm = dy * u * sa
                dmb = dm.astype(BF16)
                dvn_ref[rows, sl] = _dot(wmt_ref[g], dmb, NN)
                dws_ref[g] += jnp.where(tril, _dot(dmb, vng, NT), 0.0)
                dbs_ref[g] += jnp.sum(dm, axis=1, keepdims=True)
            dvn = dvn_ref[rows, :]
            dlng_ref[...] += jnp.sum(dvn * xhat, axis=0, keepdims=True)
            dlnb_ref[...] += jnp.sum(dvn, axis=0, keepdims=True)
            dxh = dvn * lng
            dv_g = rs * (dxh - _row_mean(dxh)
                         - xhat * _row_mean(dxh * xhat))
            dp_ref[rows, OFF_V:OFF_V + DG] = dv_g.astype(BF16)

            cosv, sinv = cos_ref[rows, :], sin_ref[rows, :]
            k_cur = roped_k(b)
            v_cur = p_ref[rows, OFF_VA:OFF_VA + 128]

            def q_slab(j):
                return q_ref[rows, 128 * j:128 * j + 128]

            def do_slab(j):
                zb = p_ref[rows, OFF_ZB + 128 * j:OFF_ZB + 128 * j + 128]
                return dyc_ref[rows, DG + 128 * j:DG + 128 * j + 128] * (zb * _sigmoid(zb))

            k_both = jnp.concatenate([k_prev, k_cur], axis=0)
            v_both = jnp.concatenate([v_prev, v_cur], axis=0)
            dk_both, dv_both = jnp.zeros((2 * T, 128), F32), jnp.zeros((2 * T, 128), F32)
            for grp in range(2):
                qs = _stack_heads(q_slab, grp, lo64)
                d_o = _stack_heads(do_slab, grp, lo64)
                dob = d_o.astype(BF16)
                kg, vg = _both_halves(k_both, grp).astype(BF16), _both_halves(v_both, grp).astype(BF16)
                p, ps = _band_softmax(qs, kg, own, has_prev, sink_ref[GH * T * grp:GH * T * (grp + 1), :])
                p_both = _unfold(p, own)
                o_st = _dot(p_both, vg, NN)
                delta = jnp.sum(d_o * o_st, axis=-1, keepdims=True)
                ds_both = _unfold(p * (_fold(_dot(dob, vg, NT), own) - delta), own)
                dq_st = _dot(ds_both, kg, NN) * (HD ** -0.5)
                dk_both = dk_both + _one_half(_dot(ds_both, qs, TN), grp)
                dv_both = dv_both + _one_half(_dot(p_both, dob, TN), grp)
                dsink_rows = ps * delta
                for hh in range(GH):
                    h = GH * grp + hh
                    dsink_ref[h:h + 1, :] += jnp.broadcast_to(
                        -jnp.sum(dsink_rows[hh * T:(hh + 1) * T], axis=0, keepdims=True), (1, 128))
                for jj in range(GH // 2):
                    c0 = 128 * (GH // 2 * grp + jj)
                    zb = p_ref[rows, OFF_ZB + c0:OFF_ZB + c0 + 128]
                    sg = _sigmoid(zb)
                    o = _unstack_heads(o_st, jj, lo64)
                    dp_ref[rows, OFF_ZB + c0:OFF_ZB + c0 + 128] = (
                        dyc_ref[rows, DG + c0:DG + c0 + 128] * o * (sg * (1.0 + zb * (1.0 - sg)))).astype(BF16)
                    dq = _unstack_heads(dq_st, jj, lo64)
                    dq_pre = dq * cosv - _rot_half(dq, first_half) * sinv
                    dp_ref[rows, OFF_Q + c0:OFF_Q + c0 + 128] = dq_pre.astype(BF16)
                    dbqkv_ref[:, c0:c0 + 128] += jnp.sum(dq_pre, axis=0, keepdims=True)
            dk_prev, dv_prev = dk_both[0:T], dv_both[0:T]
            dk_cur, dv_cur = dk_both[T:2 * T] + dk_next, dv_both[T:2 * T] + dv_next
            dk_pre = dk_cur * cosv - _rot_half(dk_cur, first_half) * sinv
            dp_ref[rows, OFF_K:OFF_K + 128] = dk_pre.astype(BF16)
            dp_ref[rows, OFF_VA:OFF_VA + 128] = dv_cur.astype(BF16)
            dbqkv_ref[:, 1024:1152] += jnp.sum(dk_pre, axis=0, keepdims=True)
            dbqkv_ref[:, 1152:1280] += jnp.sum(dv_cur, axis=0, keepdims=True)
            return dk_prev, dv_prev

        grads = carry_ref[:, 0:128], carry_ref[:, 128:256]
        for b in range(BPS - 1, 0, -1):
            prows = slice((b - 1) * T, b * T)
            grads = block(b, roped_k(b - 1), p_ref[prows, OFF_VA:OFF_VA + 128], None, *grads)
        k_prev = _rope(kvp_ref[:, 0:128], cosp_ref[...], sinp_ref[...], first_half)
        grads = block(0, k_prev, kvp_ref[:, 128:256], i < nt - 1, *grads)
        carry_ref[:, 0:128], carry_ref[:, 128:256] = grads
        pl.when(i == nt - 1)(exchange.finish)

    return pl.pallas_call(
        body, name="mid_bwd", grid=(nt,),
        in_specs=proj_specs + [pl.BlockSpec((BPS * T, D), lambda i: (tile(i), 0)),
                               pl.BlockSpec((BPS * T, NQ * HD), lambda i: (tile(i), 0))] + par_specs + [HBM_SPEC],
        out_specs=[pl.BlockSpec((BPS * T, D_IN), lambda i: (tile(i), 0)),
                   pl.BlockSpec((1, DG), const2), pl.BlockSpec((1, DG), const2),
                   pl.BlockSpec((NG, T, T), lambda i: (0, 0, 0)), pl.BlockSpec((NG, T, 1), lambda i: (0, 0, 0)),
                   pl.BlockSpec((NQ, 128), const2), pl.BlockSpec((1, D_QKV), const2), HBM_SPEC],
        out_shape=[jax.ShapeDtypeStruct((S, D_IN), BF16),
                   jax.ShapeDtypeStruct((1, DG), F32), jax.ShapeDtypeStruct((1, DG), F32),
                   jax.ShapeDtypeStruct((NG, T, T), F32), jax.ShapeDtypeStruct((NG, T, 1), F32),
                   jax.ShapeDtypeStruct((NQ, 128), F32), jax.ShapeDtypeStruct((1, D_QKV), F32),
                   jax.ShapeDtypeStruct(po.shape, BF16)],
        scratch_shapes=[pltpu.VMEM((T, 2 * T), F32), pltpu.VMEM((BPS * T, DG), F32),
                        pltpu.VMEM((NG, T, T), BF16), pltpu.VMEM((NG, T, T), BF16)] + OWNER_SEMS,
        compiler_params=_params(("arbitrary",)),
    )(proj, proj, dycat, q_roped, cos, sin, cos, sin, ln_g, ln_b, w_s, b_s, sinks, po)


def _outproj_loss(ycat, wo, x, target, g_post):
    S = ycat.shape[0]
    tm = _tile(S, 512)
    nt = S // tm
    n_part = 2 if tm % 32 == 0 else 1
    tp = tm // n_part
    const2 = lambda i: (0, 0)

    def body(yc_ref, w_ref, x_ref, t_ref, g_ref, dy_ref, dout_ref, loss_ref, dg_ref, lacc_ref):
        i = pl.program_id(0)

        @pl.when(i == 0)
        def _():
            dg_ref[...] = jnp.zeros_like(dg_ref)
            lacc_ref[...] = jnp.zeros_like(lacc_ref)

        g = g_ref[...]
        ys = [_dot(yc_ref[q * tp:(q + 1) * tp, :], w_ref[...], NN) for q in range(n_part)]
        for q, y in enumerate(ys):
            rows = slice(q * tp, (q + 1) * tp)
            r = lax.rsqrt(_row_mean(y * y) + EPS)
            yh = y * r
            diff = x_ref[rows, :] + yh * g - t_ref[rows, :]
            lacc_ref[...] += jnp.sum(diff * diff, axis=0, keepdims=True)
            dout = diff * (1.0 / D)
            dout_ref[rows, :] = dout
            dg_ref[...] += jnp.sum(dout * yh, axis=0, keepdims=True)
            dyh = dout * g
            dy_ref[rows, :] = (r * (dyh - yh * _row_mean(dyh * yh))).astype(BF16)

        @pl.when(i == nt - 1)
        def _():
            loss_ref[...] = jnp.broadcast_to(jnp.sum(lacc_ref[...], axis=1, keepdims=True) * (0.5 / D), (1, 128))

    row = lambda i: (i, 0)
    return pl.pallas_call(
        body, name="outproj_loss", grid=(nt,),
        in_specs=[pl.BlockSpec((tm, D), row), pl.BlockSpec((D, D), const2, pipeline_mode=pl.Buffered(1)),
                  pl.BlockSpec((tm, D), row), pl.BlockSpec((tm, D), row), pl.BlockSpec((1, D), const2)],
        out_specs=[pl.BlockSpec((tm, D), row), pl.BlockSpec((tm, D), row), pl.BlockSpec((1, 128), const2),
                   pl.BlockSpec((1, D), const2)],
        out_shape=[jax.ShapeDtypeStruct((S, D), BF16), jax.ShapeDtypeStruct((S, D), F32),
                   jax.ShapeDtypeStruct((1, 128), F32), jax.ShapeDtypeStruct((1, D), F32)],
        scratch_shapes=[pltpu.VMEM((1, D), F32)],
        compiler_params=_params(("arbitrary",)),
    )(ycat, wo, x, target, g_post)


def _dycat(dy, wo, dwo):
    S = dy.shape[0]
    tm = _tile(S, 1024)
    nt = S // tm

    def body(dy_ref, w_ref, dwo_ref, o_ref, r_ref, send_sems, recv_sems):
        i = pl.program_id(0)
        copies = _sibling_copies(dwo_ref, r_ref, W_OUT_ROWS, send_sems, recv_sems)

        @pl.when(i == 0)
        def _():
            for cp in copies:
                cp.start()

        o_ref[...] = _dot(dy_ref[...], w_ref[...], NT)

        @pl.when(i == nt - 1)
        def _():
            for cp in copies:
                cp.wait_recv()
            for cp in copies:
                cp.wait_send()

    return pl.pallas_call(
        body, name="dycat", grid=(nt,),
        in_specs=[pl.BlockSpec((tm, D), lambda i: (i, 0)),
                  pl.BlockSpec((D, D), lambda i: (0, 0), pipeline_mode=pl.Buffered(1)), HBM_SPEC],
        out_specs=[pl.BlockSpec((tm, D), lambda i: (i, 0)), HBM_SPEC],
        out_shape=[jax.ShapeDtypeStruct((S, D), F32), jax.ShapeDtypeStruct((N_CHIPS, W_OUT_ROWS, D), BF16)],
        scratch_shapes=[pltpu.SemaphoreType.DMA((N_CHIPS,)), pltpu.SemaphoreType.DMA((N_CHIPS,))],
        compiler_params=_params(("arbitrary",)),
    )(dy, wo, dwo)


def _block_segments(tm, n_tiles, half_rows):
    segs, sid = [], 0
    for t in range(n_tiles):
        segs.append([])
        for b in range(tm * t // half_rows, (tm * (t + 1) - 1) // half_rows + 1):
            segs[-1].append((sid, b, max(tm * t, half_rows * b), min(tm * (t + 1), half_rows * (b + 1))))
            sid += 1
    return segs, sid


def _matmul_tn(a, b, tm, name, mat_grads=None, half_rows=None):
    K, M = a.shape
    N = b.shape[1]
    tk = _tile(K, 2048)
    ni, nk = M // tm, K // tk
    hosting = mat_grads is not None

    def body(a_ref, b_ref, *rest):
        if hosting:
            g_in, (o_ref, tot_ref, r_ref, acc_ref, mbuf, send_sems, recv_sems, stage, seg_send, seg_recv) = (
                rest[:len(MAT)], rest[len(MAT):])
        else:
            o_ref, acc_ref = rest
        i, k = pl.program_id(0), pl.program_id(1)

        if hosting:
            gather = _SlotGather(mbuf, send_sems, recv_sems)

            @pl.when((i == 0) & (k == 0))
            def _():
                r0 = 0
                for q, (_, rows) in enumerate(MAT):
                    mbuf[gather.my_id, r0:r0 + rows, :] = g_in[q][...]
                    r0 += rows
                gather.start()

            pl.when((i == ni // 2) & (k == 0))(gather.pass_on)

            x_, y_, c = _mesh_pos()
            segs, _ = _block_segments(tm, ni, half_rows)

            def seg_copy(p, sid, blk, lo, hi):
                return pltpu.make_async_remote_copy(
                    src_ref=stage.at[p % 2, pl.ds(lo - tm * order[p], hi - lo), :],
                    dst_ref=r_ref.at[blk // 2, pl.ds(lo - half_rows * blk, hi - lo), :],
                    send_sem=seg_send.at[sid], recv_sem=seg_recv.at[sid],
                    device_id=(x_, y_, 1 - c), device_id_type=pl.DeviceIdType.MESH)

            def sent(p):
                for sid, blk, lo, hi in segs[order[p]]:
                    pl.when(c == 1 - blk % 2)(seg_copy(p, sid, blk, lo, hi).wait_send)

            for p in range(2, ni):
                pl.when((i == p) & (k == nk - 1))(lambda p=p: sent(p - 2))

        @pl.when(k == 0)
        def _():
            acc_ref[...] = jnp.zeros_like(acc_ref)

        acc_ref[...] += _dot(a_ref[...], b_ref[...], TN)

        @pl.when(k == nk - 1)
        def _():
            tile = acc_ref[...].astype(BF16)
            o_ref[...] = tile
            if hosting:
                stage[i % 2] = tile

        if hosting:
            for p in range(ni):
                @pl.when((i == p) & (k == nk - 1))
                def _(p=p):
                    for sid, blk, lo, hi in segs[order[p]]:
                        pl.when(c == 1 - blk % 2)(seg_copy(p, sid, blk, lo, hi).start)

            @pl.when((i == ni - 1) & (k == nk - 1))
            def _():
                gather.finish()
                tot_ref[...] = gather.total()
                sent(ni - 2)
                sent(ni - 1)
                for p in range(ni):
                    for sid, blk, lo, hi in segs[order[p]]:
                        pl.when(c == blk % 2)(seg_copy(p, sid, blk, lo, hi).wait_recv)

    mid = ni // 2
    order = [t for t in range(ni) if t != mid] + [mid] if hosting else list(range(ni))

    def tile_of(i):
        return jnp.where(i == ni - 1, mid, jnp.where(i < mid, i, i + 1)) if hosting else i

    whole = pl.BlockSpec(memory_space=pltpu.VMEM)
    in_specs = [pl.BlockSpec((tk, tm), lambda i, k: (k, tile_of(i))), pl.BlockSpec((tk, N), lambda i, k: (k, 0))]
    out_specs = [pl.BlockSpec((tm, N), lambda i, k: (tile_of(i), 0))]
    out_shape = [jax.ShapeDtypeStruct((M, N), BF16)]
    scratch = [pltpu.VMEM((tm, N), F32)]
    if hosting:
        in_specs += [whole] * len(MAT)
        n_segs = _block_segments(tm, ni, half_rows)[1]
        out_specs += [whole, HBM_SPEC]
        out_shape += [jax.ShapeDtypeStruct((MAT_ROWS, 128), F32),
                      jax.ShapeDtypeStruct((M // (2 * half_rows), half_rows, N), BF16)]
        scratch += [pltpu.VMEM((N_DEV, MAT_ROWS, 128), F32), pltpu.SemaphoreType.DMA((7,)),
                    pltpu.SemaphoreType.DMA((7,)), pltpu.VMEM((2, tm, N), BF16),
                    pltpu.SemaphoreType.DMA((n_segs,)), pltpu.SemaphoreType.DMA((n_segs,))]
    res = pl.pallas_call(
        body, name=name, grid=(ni, nk), in_specs=in_specs, out_specs=out_specs, out_shape=out_shape,
        scratch_shapes=scratch,
        compiler_params=_params(("arbitrary", "arbitrary") if hosting else ("parallel", "arbitrary")),
    )(a, b, *(mat_grads or ()))
    return res if hosting else res[0]


def _dh_prenorm_bwd(dproj, wt, x, dout, g_pre, pt):
    S = x.shape[0]
    tm, tk = _tile(S, 1024), 768
    ne = 4
    te = tm // ne
    ni, nk = S // tm, D_IN // tk
    n_mm = ni * nk
    n_steps = n_mm + ne
    mm_of = lambda t: jnp.minimum(t, n_mm - 1)

    def quarter_of(t):
        j, q = t // nk - 1, t % nk
        nxt = jnp.where(q < ne, j * ne + q, (j + 1) * ne)
        return jnp.clip(nxt, 0, ni * ne - 1)

    def body(dp_ref, w_ref, x_ref, dout_ref, g_ref, pt_ref, gx_ref, dg_ref, bt_ref, acc_ref,
             send_sems, recv_sems, local_sem):
        t = pl.program_id(0)
        exchange = _OwnerExchange(pt_ref, bt_ref, send_sems, recv_sems, local_sem)
        pl.when(t == 0)(exchange.start)

        @pl.when(t == 0)
        def _():
            dg_ref[...] = jnp.zeros_like(dg_ref)

        has_mm = t < n_mm
        has_q = (t >= nk) & (t % nk < ne)

        def matmul():
            slot = (mm_of(t) // nk) % 2
            old = jnp.where(mm_of(t) % nk > 0, acc_ref[slot], 0.0)
            acc_ref[slot] = old + _dot(dp_ref[...], w_ref[...], NN)

        def quarter():
            slot = (t // nk - 1) % 2
            dh = acc_ref[slot, pl.ds(pl.multiple_of((t % nk) * te, te), te), :]
            xv = x_ref[...]
            r = lax.rsqrt(_row_mean(xv * xv) + EPS)
            xh = xv * r
            dg_ref[...] += jnp.sum(dh * xh, axis=0, keepdims=True)
            dxh = dh * g_ref[...]
            gx_ref[...] = dout_ref[...] + r * (dxh - xh * _row_mean(dxh * xh))

        @pl.when(has_mm & has_q)
        def _():
            matmul()
            quarter()

        pl.when(has_mm & jnp.logical_not(has_q))(matmul)
        pl.when(jnp.logical_not(has_mm) & has_q)(quarter)
        pl.when(t == n_steps - 1)(exchange.finish)

    quarter_block = lambda t: (quarter_of(t), 0)
    return pl.pallas_call(
        body, name="dh_prenorm_bwd", grid=(n_steps,),
        in_specs=[pl.BlockSpec((tm, tk), lambda t: (mm_of(t) // nk, mm_of(t) % nk)),
                  pl.BlockSpec((tk, D), lambda t: (mm_of(t) % nk, 0)),
                  pl.BlockSpec((te, D), quarter_block), pl.BlockSpec((te, D), quarter_block),
                  pl.BlockSpec((1, D), lambda t: (0, 0)), HBM_SPEC],
        out_specs=[pl.BlockSpec((te, D), quarter_block), pl.BlockSpec((1, D), lambda t: (0, 0)), HBM_SPEC],
        out_shape=[jax.ShapeDtypeStruct((S, D), F32), jax.ShapeDtypeStruct((1, D), F32),
                   jax.ShapeDtypeStruct(pt.shape, BF16)],
        scratch_shapes=[pltpu.VMEM((2, tm, D), F32)] + OWNER_SEMS,
        compiler_params=_params(("arbitrary",)),
    )(dproj, wt, x, dout, g_pre, pt)


def _presum(c_arr, own, recv, half_rows):
    n_cols = own.shape[-1]
    own4 = own.reshape(N_CHIPS, 2, half_rows, n_cols)

    def body(c_ref, own_ref, recv_ref, o_ref):
        o_ref[...] = (own_ref[...].astype(F32) + recv_ref[...].astype(F32)).astype(BF16)

    return pl.pallas_call(
        body, name="presum_%d" % half_rows,
        grid_spec=pltpu.PrefetchScalarGridSpec(
            num_scalar_prefetch=1, grid=(N_CHIPS,),
            in_specs=[pl.BlockSpec((None, None, half_rows, n_cols), lambda j, c: (j, c[0], 0, 0)),
                      pl.BlockSpec((None, half_rows, n_cols), lambda j, c: (j, 0, 0))],
            out_specs=pl.BlockSpec((None, half_rows, n_cols), lambda j, c: (j, 0, 0))),
        out_shape=jax.ShapeDtypeStruct((N_CHIPS, half_rows, n_cols), BF16),
        compiler_params=_params(("parallel",)),
    )(c_arr, own4, recv)


def _sibling_presum(c_arr, dw, half_rows, name):
    n_cols = dw.shape[-1]
    dw4 = dw.reshape(N_CHIPS, 2, half_rows, n_cols)

    def body(c_ref, own_ref, dw_ref, o_ref, r_ref, stage, send_sems, recv_sems, load_sem):
        j = pl.program_id(0)
        copies = _sibling_copies(dw_ref, r_ref, half_rows, send_sems, recv_sems)

        @pl.when(j == 0)
        def _():
            for cp in copies:
                cp.start()

        for q in range(N_CHIPS):
            pl.when(j == q)(copies[q].wait_recv)
        load = pltpu.make_async_copy(r_ref.at[j], stage, load_sem)
        load.start()
        load.wait()
        o_ref[...] = (own_ref[...].astype(F32) + stage[...].astype(F32)).astype(BF16)

        @pl.when(j == N_CHIPS - 1)
        def _():
            for cp in copies:
                cp.wait_send()

    out, _ = pl.pallas_call(
        body, name=name,
        grid_spec=pltpu.PrefetchScalarGridSpec(
            num_scalar_prefetch=1, grid=(N_CHIPS,),
            in_specs=[pl.BlockSpec((None, None, half_rows, n_cols), lambda j, c: (j, c[0], 0, 0)), HBM_SPEC],
            out_specs=[pl.BlockSpec((None, half_rows, n_cols), lambda j, c: (j, 0, 0)), HBM_SPEC],
            scratch_shapes=[pltpu.VMEM((half_rows, n_cols), BF16), pltpu.SemaphoreType.DMA((N_CHIPS,)),
                            pltpu.SemaphoreType.DMA((N_CHIPS,)), pltpu.SemaphoreType.DMA(())]),
        out_shape=[jax.ShapeDtypeStruct((N_CHIPS, half_rows, n_cols), BF16),
                   jax.ShapeDtypeStruct((N_CHIPS, half_rows, n_cols), BF16)],
        compiler_params=_params(("arbitrary",)),
    )(c_arr, dw4, dw)
    return out


def _sum_chips(c_arr, parts_a, parts_b, name):
    nt = 2

    def body(c_ref, a_ref, b_ref, oa_ref, ob_ref):
        for p_ref, o_ref in ((a_ref, oa_ref), (b_ref, ob_ref)):
            o_ref[...] = (((p_ref[0].astype(F32) + p_ref[1].astype(F32)) + p_ref[2].astype(F32))
                          + p_ref[3].astype(F32))

    return pl.pallas_call(
        body, name=name,
        grid_spec=pltpu.PrefetchScalarGridSpec(
            num_scalar_prefetch=1, grid=(nt,),
            in_specs=[pl.BlockSpec((N_CHIPS, p.shape[1] // nt, p.shape[2]), lambda i, c: (0, i, 0))
                      for p in (parts_a, parts_b)],
            out_specs=[pl.BlockSpec((p.shape[1] // nt, p.shape[2]), lambda i, c: (c[0] * nt + i, 0))
                       for p in (parts_a, parts_b)]),
        out_shape=[jax.ShapeDtypeStruct((2 * p.shape[1], p.shape[2]), F32) for p in (parts_a, parts_b)],
        compiler_params=_params(("parallel",)),
    )(c_arr, parts_a, parts_b)


def _adamw_math(w, g, m, v):
    mn = ADAM_B1 * m + (1.0 - ADAM_B1) * g
    vn = ADAM_B2 * v + (1.0 - ADAM_B2) * (g * g)
    m_hat = mn / (1.0 - ADAM_B1 ** ADAM_STEP)
    v_hat = vn / (1.0 - ADAM_B2 ** ADAM_STEP)
    return -ADAM_LR * (m_hat / (jnp.sqrt(v_hat) + ADAM_EPS) + ADAM_WD * w), mn, vn


def _adamw(w, g, m, v, name):
    R, C = w.shape
    tr = next((t for t in (256, 192, 128) if R % t == 0), R)

    def body(w_ref, g_ref, m_ref, v_ref, go_ref, d_ref, mo_ref, vo_ref):
        gv = g_ref[...]
        go_ref[...] = gv
        d_ref[...], mo_ref[...], vo_ref[...] = _adamw_math(w_ref[...], gv, m_ref[...], v_ref[...])

    spec = pl.BlockSpec((tr, C), lambda i: (i, 0))
    shp = jax.ShapeDtypeStruct((R, C), F32)
    return pl.pallas_call(
        body, name=name, grid=(R // tr,), in_specs=[spec] * 4, out_specs=[spec] * 4, out_shape=[shp] * 4,
        compiler_params=_params(("parallel",)),
    )(w, g, m, v)


HBM_SPEC = pl.BlockSpec(memory_space=pltpu.HBM)
GATHER_LOCAL_CHUNKS = 4
GATHER_SEMS = [pltpu.SemaphoreType.DMA((7,)), pltpu.SemaphoreType.DMA((7,)),
               pltpu.SemaphoreType.DMA((GATHER_LOCAL_CHUNKS,))]
OWNER_SEMS = [pltpu.SemaphoreType.DMA((3,)), pltpu.SemaphoreType.DMA((3,)), pltpu.SemaphoreType.DMA(())]


def _mesh_pos():
    return lax.axis_index("x"), lax.axis_index("y"), lax.axis_index("c")


class _RowGather:
    def __init__(self, src_ref, full_ref, rows, send_sems, recv_sems, local_sems):
        self.src, self.full, self.rows = src_ref, full_ref, rows
        self.send, self.recv, self.local = send_sems, recv_sems, local_sems
        x, y, c = _mesh_pos()
        self.c, self.me, self.sibling = c, (x, y, c), (x, y, 1 - c)
        self.chips = [(1 - x, y), (x, 1 - y), (1 - x, 1 - y)]

    def _block(self, pos):
        px, py, pc = pos
        return self.full.at[pl.ds(pl.multiple_of((4 * px + 2 * py + pc) * self.rows, 16), self.rows), :]

    def _copy(self, k, blk, to):
        return pltpu.make_async_remote_copy(
            src_ref=self.src if blk is self.me else self._block(blk), dst_ref=self._block(blk),
            send_sem=self.send.at[k], recv_sem=self.recv.at[k], device_id=to, device_id_type=MESH)

    def _mine(self):
        return _place_locally(self.src, self._block(self.me), self.local, GATHER_LOCAL_CHUNKS)

    def _first(self):
        return [self._copy(0, self.me, self.sibling)] + [
            self._copy(1 + j, self.me, (*chip, self.c)) for j, chip in enumerate(self.chips)]

    def start(self):
        for cp in self._first() + self._mine():
            cp.start()

    def _passed(self):
        return [self._copy(4 + j, (*chip, self.c), self.sibling) for j, chip in enumerate(self.chips)]

    def pass_on(self):
        for j, chip in enumerate(self.chips):
            self._copy(1 + j, (*chip, self.c), self.me).wait_recv()
            self._passed()[j].start()

    def finish(self):
        self._copy(0, self.sibling, self.me).wait_recv()
        for j, chip in enumerate(self.chips):
            self._copy(4 + j, (*chip, 1 - self.c), self.me).wait_recv()
        for cp in self._first() + self._passed():
            cp.wait_send()
        for cp in self._mine():
            cp.wait()


class _OwnerExchange:
    def __init__(self, src_ref, dst_ref, send_sems, recv_sems, local_sem):
        self.src, self.dst, self.send, self.recv, self.local = src_ref, dst_ref, send_sems, recv_sems, local_sem
        x, y, c = _mesh_pos()
        self.c, self.my_chip = c, 2 * x + y
        self.peers = [(1 - x, y), (x, 1 - y), (1 - x, 1 - y)]

    def _copies(self):
        local = pltpu.make_async_copy(self.src.at[self.my_chip], self.dst.at[self.my_chip], self.local)
        remote = [pltpu.make_async_remote_copy(
            src_ref=self.src.at[2 * px + py], dst_ref=self.dst.at[self.my_chip],
            send_sem=self.send.at[k], recv_sem=self.recv.at[k], device_id=(px, py, self.c), device_id_type=MESH)
            for k, (px, py) in enumerate(self.peers)]
        return local, remote

    def start(self):
        local, remote = self._copies()
        local.start()
        for cp in remote:
            cp.start()

    def finish(self):
        local, remote = self._copies()
        for cp in remote:
            cp.wait_recv()
        for cp in remote:
            cp.wait_send()
        local.wait()


def _sibling_copies(dw_ref, r_ref, rows, send_sems, recv_sems):
    x, y, c = _mesh_pos()
    return [pltpu.make_async_remote_copy(
        src_ref=dw_ref.at[pl.ds(pl.multiple_of((2 * j + (1 - c)) * rows, 16), rows), :], dst_ref=r_ref.at[j],
        send_sem=send_sems.at[j], recv_sem=recv_sems.at[j], device_id=(x, y, 1 - c), device_id_type=MESH)
        for j in range(N_CHIPS)]


PAIR_CHUNKS = 4


def _pair_halves(gt, go, vec_grads, loss_part):
    n_vec = len(VEC)

    def body(*refs):
        g_in, loss_in = refs[2:2 + n_vec], refs[2 + n_vec]
        gt_ref, go_ref, tot_v, vbuf, send_sems, recv_sems, vsend_sems, vrecv_sems = refs[3 + n_vec:]
        x, y, c = _mesh_pos()
        gather = _SlotGather(vbuf, vsend_sems, vrecv_sems)
        vbuf[gather.my_id] = jnp.zeros((VEC_ROWS, D), F32)
        for r, (_, n) in enumerate(VEC):
            vbuf[gather.my_id, r:r + 1, 0:n] = g_in[r][...]
        vbuf[gather.my_id, LOSS_ROW:LOSS_ROW + 1, 0:128] = loss_in[...]
        gather.start()
        copies = []
        for a, (ref, rows) in enumerate(((gt_ref, W_IN_ROWS), (go_ref, W_OUT_ROWS))):
            ch = rows // PAIR_CHUNKS
            for q in range(PAIR_CHUNKS):
                part = ref.at[pl.ds(pl.multiple_of(c * rows + q * ch, 8), ch), :]
                copies.append(pltpu.make_async_remote_copy(
                    src_ref=part, dst_ref=part, send_sem=send_sems.at[PAIR_CHUNKS * a + q],
                    recv_sem=recv_sems.at[PAIR_CHUNKS * a + q], device_id=(x, y, 1 - c), device_id_type=MESH))
        for cp in copies:
            cp.start()
        gather.pass_on()
        gather.finish()
        tot_v[...] = gather.total()
        for cp in copies:
            cp.wait_recv()
        for cp in copies:
            cp.wait_send()

    vmem = pl.BlockSpec(memory_space=pltpu.VMEM)
    return pl.pallas_call(
        body, name="pair_halves",
        in_specs=[HBM_SPEC, HBM_SPEC] + [vmem] * (n_vec + 1), out_specs=[HBM_SPEC, HBM_SPEC, vmem],
        out_shape=[jax.ShapeDtypeStruct(gt.shape, F32), jax.ShapeDtypeStruct(go.shape, F32),
                   jax.ShapeDtypeStruct((VEC_ROWS, D), F32)],
        input_output_aliases={0: 0, 1: 1},
        scratch_shapes=[pltpu.VMEM((N_DEV, VEC_ROWS, D), F32),
                        pltpu.SemaphoreType.DMA((2 * PAIR_CHUNKS,)), pltpu.SemaphoreType.DMA((2 * PAIR_CHUNKS,)),
                        pltpu.SemaphoreType.DMA((7,)), pltpu.SemaphoreType.DMA((7,))],
    )(gt, go, *vec_grads, loss_part)


VEC = (("g_pre", 2048), ("g_post", 2048), ("b_qkv", 1280), ("ln_v_g", 1024), ("ln_v_b", 1024), ("attn_sinks", 16))
VEC_ROWS = 8
LOSS_ROW = len(VEC)
MAT = (("w_spatial", NG * T), ("b_spatial", NG))
MAT_ROWS = sum(r for _, r in MAT)


class _SlotGather:
    def __init__(self, buf, send_sems, recv_sems):
        self.buf, self.send, self.recv = buf, send_sems, recv_sems
        x, y, c = _mesh_pos()
        self.c, self.me, self.sibling, self.my_id = c, (x, y, c), (x, y, 1 - c), 4 * x + 2 * y + c
        self.chips = [(1 - x, y), (x, 1 - y), (1 - x, 1 - y)]

    def _copy(self, k, blk, to):
        px, py, pc = blk
        slot = self.buf.at[4 * px + 2 * py + pc]
        return pltpu.make_async_remote_copy(
            src_ref=slot, dst_ref=slot, send_sem=self.send.at[k], recv_sem=self.recv.at[k],
            device_id=to, device_id_type=MESH)

    def _first(self):
        return [self._copy(0, self.me, self.sibling)] + [
            self._copy(1 + j, self.me, (*chip, self.c)) for j, chip in enumerate(self.chips)]

    def start(self):
        for cp in self._first():
            cp.start()

    def _passed(self):
        return [self._copy(4 + j, (*chip, self.c), self.sibling) for j, chip in enumerate(self.chips)]

    def pass_on(self):
        for j, chip in enumerate(self.chips):
            self._copy(1 + j, (*chip, self.c), self.me).wait_recv()
            self._passed()[j].start()

    def finish(self):
        self._copy(0, self.sibling, self.me).wait_recv()
        for j, chip in enumerate(self.chips):
            self._copy(4 + j, (*chip, 1 - self.c), self.me).wait_recv()
        for cp in self._first() + self._passed():
            cp.wait_send()

    def total(self):
        t = self.buf[0]
        for d in range(1, N_DEV):
            t = t + self.buf[d]
        return t


def _small_update(vec_total, mat_total, vec_state, mat_state):
    n_vec, n_mat = len(VEC), len(MAT)
    n_par = n_vec + n_mat
    n_in = 2 + 3 * n_par

    def body(*refs):
        tot_v, tot_m = refs[0], refs[1]
        st_in = refs[2:n_in]
        outs, loss_out = refs[n_in:n_in + 4 * n_par], refs[n_in + 4 * n_par]
        loss_out[...] = tot_v[LOSS_ROW:LOSS_ROW + 1, 0:128]
        r0 = 0
        for q in range(n_par):
            if q < n_vec:
                g = tot_v[q:q + 1, 0:VEC[q][1]]
            else:
                rows = MAT[q - n_vec][1]
                g = tot_m[r0:r0 + rows, :]
                r0 += rows
            w, m, v = (st_in[3 * q + t][...] for t in range(3))
            outs[4 * q][...] = g
            outs[4 * q + 1][...], outs[4 * q + 2][...], outs[4 * q + 3][...] = _adamw_math(w, g, m, v)

    state = [a for wmv in list(vec_state) + list(mat_state) for a in wmv]
    vmem = pl.BlockSpec(memory_space=pltpu.VMEM)
    out_shape = [jax.ShapeDtypeStruct(wmv[0].shape, F32) for wmv in list(vec_state) + list(mat_state) for _ in range(4)]
    out_shape.append(jax.ShapeDtypeStruct((1, 128), F32))
    res = pl.pallas_call(
        body, name="small_update",
        in_specs=[vmem] * n_in, out_specs=[vmem] * len(out_shape), out_shape=out_shape,
        compiler_params=pltpu.CompilerParams(vmem_limit_bytes=VMEM_LIMIT),
    )(vec_total, mat_total, *state)
    return [res[4 * q:4 * q + 4] for q in range(n_par)], res[-1]


def kernel(x, positions, g_pre, w_in, b_qkv, ln_v_g, ln_v_b, w_spatial, b_spatial, attn_sinks, w_out, g_post, loss_target, m_g_pre, m_w_in, m_b_qkv, m_ln_v_g, m_ln_v_b, m_w_spatial, m_b_spatial, m_attn_sinks, m_w_out, m_g_post, v_g_pre, v_w_in, v_b_qkv, v_ln_v_g, v_ln_v_b, v_w_spatial, v_b_spatial, v_attn_sinks, v_w_out, v_g_post):
    S = x.shape[1]
    c = lax.axis_index("c")
    c_arr = jnp.reshape(c, (1,)).astype(jnp.int32)
    x2 = x[0]
    target = loss_target[0]
    pos = positions.reshape(S, 1)
    half = HD // 2
    inv_freq = ROPE_THETA ** (-jnp.arange(half, dtype=F32) * (2.0 / HD))
    invf = jnp.tile(inv_freq, 128 // half).reshape(1, 128)
    bias = jnp.concatenate([jnp.zeros((OFF_Q,), F32), b_qkv[0], jnp.zeros((D_IN - OFF_ZB,), F32)]).reshape(1, D_IN)
    b_s_col = b_spatial[0].reshape(NG, T, 1)
    sinks = jnp.repeat(attn_sinks[0], T).reshape(NQ * T, 1)

    chip = 2 * lax.axis_index("x") + lax.axis_index("y")
    wt_part = lax.dynamic_slice_in_dim(w_in[0].T.astype(BF16), c * W_IN_ROWS, W_IN_ROWS, axis=0)
    wo_part = lax.dynamic_slice_in_dim(w_out[0].astype(BF16), c * W_OUT_ROWS, W_OUT_ROWS, axis=0)
    sched = jnp.asarray(PROJ_SCHEDULE, jnp.int32)[chip]

    h, proj, wt, cos, sin = _prenorm_inproj(sched, x2, g_pre, bias, wt_part, pos, invf)
    ycat, wo, q_roped = _mid_fwd(proj, cos, sin, ln_v_g, ln_v_b, w_spatial[0], b_s_col, sinks, wo_part)
    dy, dout, loss_part, dg_post = _outproj_loss(ycat, wo, x2, target, g_post)

    dwo = _matmul_tn(ycat, dy, 1024, "dw_out")
    dycat, ro = _dycat(dy, wo, dwo)
    po = _presum(c_arr, dwo, ro, W_OUT_ROWS)
    dproj, dln_g, dln_b, dws, dbs, dsink, dbqkv, bo = _mid_bwd(
        proj, dycat, q_roped, cos, sin, ln_v_g, ln_v_b, w_spatial[0], b_s_col, sinks, po)
    dwt, mat_total, rt = _matmul_tn(dproj, h, 768, "dw_in_t",
                                    mat_grads=[dws.reshape(NG * T, T), dbs.reshape(NG, T)], half_rows=W_IN_ROWS)
    pt = _presum(c_arr, dwt, rt, W_IN_ROWS)
    grad_x, dg_pre, bt = _dh_prenorm_bwd(dproj, wt, x2, dout, g_pre, pt)
    grads = {"g_pre": dg_pre, "g_post": dg_post, "b_qkv": dbqkv, "ln_v_g": dln_g, "ln_v_b": dln_b,
             "attn_sinks": dsink[:, 0].reshape(1, NQ)}
    st, so = _sum_chips(c_arr, bt, bo, "sum_chips")
    gt, go, vec_total = _pair_halves(st, so, [grads[n] for n, _ in VEC], loss_part)

    g_w_in, d_w_in, nm_w_in, nv_w_in = (a.T for a in _adamw(w_in[0].T, gt, m_w_in[0].T, v_w_in[0].T, "adamw_w_in"))
    g_w_out, d_w_out, nm_w_out, nv_w_out = _adamw(w_out[0], go, m_w_out[0], v_w_out[0], "adamw_w_out")

    state = {"g_pre": (g_pre, m_g_pre, v_g_pre), "g_post": (g_post, m_g_post, v_g_post),
             "b_qkv": (b_qkv, m_b_qkv, v_b_qkv), "ln_v_g": (ln_v_g, m_ln_v_g, v_ln_v_g),
             "ln_v_b": (ln_v_b, m_ln_v_b, v_ln_v_b), "attn_sinks": (attn_sinks, m_attn_sinks, v_attn_sinks),
             "w_spatial": tuple(a.reshape(NG * T, T) for a in (w_spatial, m_w_spatial, v_w_spatial)),
             "b_spatial": tuple(a.reshape(NG, T) for a in (b_spatial, m_b_spatial, v_b_spatial))}
    results, loss = _small_update(vec_total, mat_total, [state[n] for n, _ in VEC], [state[n] for n, _ in MAT])
    small = {n: [a.reshape(w.shape) for a in res]
             for (n, _), res, w in zip(VEC + MAT, results, [state[n][0] for n, _ in VEC + MAT])}
    small["w_spatial"] = [a.reshape(w_spatial.shape) for a in small["w_spatial"]]
    small["b_spatial"] = [a.reshape(b_spatial.shape) for a in small["b_spatial"]]
    big = {"w_in": [a[None] for a in (g_w_in, d_w_in, nm_w_in, nv_w_in)],
           "w_out": [a[None] for a in (g_w_out, d_w_out, nm_w_out, nv_w_out)]}
    order = ("g_pre", "w_in", "b_qkv", "ln_v_g", "ln_v_b", "w_spatial", "b_spatial", "attn_sinks", "w_out", "g_post")
    leaves = {**small, **big}
    return (loss[0, 0], grad_x[None], *[leaves[n][t] for t in range(4) for n in order])
```

```python
import jax
import jax.numpy as jnp
from jax import lax
from jax.experimental import pallas as pl
from jax.experimental.pallas import tpu as pltpu

F32 = jnp.float32
BF16 = jnp.bfloat16
MESH = pl.DeviceIdType.MESH

D = 2048
DG = 1024
T = 128
NG = 8
HD = 64
NQ = 16
D_IN = 5376
OFF_U, OFF_V, OFF_ZA, OFF_Q, OFF_K, OFF_VA, OFF_ZB = 0, 1024, 2048, 3072, 4096, 4224, 4352
D_QKV = 1280
EPS = 1e-6
ROPE_THETA = 10000.0
N_CHIPS = 4
N_DEV = 8
W_IN_ROWS = D_IN // N_DEV
W_OUT_ROWS = D // N_DEV

ADAM_LR, ADAM_B1, ADAM_B2, ADAM_EPS, ADAM_WD, ADAM_STEP = 0.001, 0.9, 0.999, 1e-08, 0.01, 10

VMEM_LIMIT = 56 * 1024 * 1024


def _tile(n, pref):
    return pref if n % pref == 0 else n


def _params(sem=None, vmem=VMEM_LIMIT):
    return pltpu.CompilerParams(dimension_semantics=sem, vmem_limit_bytes=vmem)


def _sigmoid(z):
    return 1.0 / (1.0 + jnp.exp(-z))


def _row_mean(v):
    return jnp.mean(v, axis=-1, keepdims=True)


def _dot(a, b, dims):
    return lax.dot_general(a, b, (dims, ((), ())), preferred_element_type=F32)


NN = ((1,), (0,))
NT = ((1,), (1,))
TN = ((0,), (0,))


PROJ_TN = 768
N_PROJ_TILES = D_IN // PROJ_TN
PROJ_SCHEDULE = ((0, 1, 2, 4, 3, 6, 5, 5), (2, 1, 0, 6, 4, 3, 5, 4), (4, 5, 6, 0, 2, 1, 3, 4), (6, 5, 4, 2, 3, 0, 1, 5))


def _place_locally(src_ref, dst_rows_ref, sems, n_chunks):
    ch = src_ref.shape[0] // n_chunks
    return [pltpu.make_async_copy(src_ref.at[pl.ds(q * ch, ch), :], dst_rows_ref.at[pl.ds(q * ch, ch), :],
                                  sems.at[q]) for q in range(n_chunks)]


def _prenorm_inproj(sched, x, g, bias, wt_part, pos, invf):
    S = x.shape[0]
    tp, tm = _tile(S, 512), _tile(S, 1024)
    n_pre, ns = S // tp, S // tm
    n_steps = n_pre + N_PROJ_TILES * ns
    pos_of = lambda i: jnp.maximum(i - n_pre, 0) // ns
    row_of = lambda i: jnp.maximum(i - n_pre, 0) % ns

    def body(sched_ref, x_ref, g_ref, b_ref, wpart_ref, pos_ref, invf_ref, h_ref, proj_ref, wt_ref, cos_ref, sin_ref,
             h_all, w_tile, stage, send_sems, recv_sems, w_sems, local_sems):
        i = pl.program_id(0)
        x_, y_, c = _mesh_pos()
        me, sibling = (x_, y_, c), (x_, y_, 1 - c)
        chips = [(1 - x_, y_), (x_, 1 - y_), (1 - x_, 1 - y_)]

        def block(pos):
            px, py, pc = pos
            return wt_ref.at[pl.ds(pl.multiple_of((4 * px + 2 * py + pc) * W_IN_ROWS, 16), W_IN_ROWS), :]

        def copy(k, blk, to):
            return pltpu.make_async_remote_copy(
                src_ref=wpart_ref if blk is me else block(blk), dst_ref=block(blk),
                send_sem=send_sems.at[k], recv_sem=recv_sems.at[k], device_id=to, device_id_type=MESH)

        stage_in = pltpu.make_async_copy(wpart_ref, stage, local_sems.at[0])
        stage_out = pltpu.make_async_copy(stage, block(me), local_sems.at[1])

        relay = (jnp.where(c == 0, x_, 1 - x_), jnp.where(c == 0, 1 - y_, y_))
        relayed = (jnp.where(c == 0, 1 - x_, x_), jnp.where(c == 0, y_, 1 - y_))

        early = c == y_
        y_send = copy(2, me, (*chips[1], c))

        def own_sends():
            return [copy(0, me, sibling), y_send, copy(1, me, (*chips[0], c))]

        def passed_on():
            return [copy(4, (*chips[0], c), sibling), copy(5, (*chips[1], c), sibling),
                    copy(3, (*relayed, c), (*relay, c))]

        def early_block_arrives():
            @pl.when(early)
            def _():
                copy(2, (*chips[1], c), me).wait_recv()
                copy(5, (*chips[1], c), sibling).start()
                y_send.start()

            pl.when(jnp.logical_not(early))(lambda: copy(5, (*chips[1], 1 - c), me).wait_recv())

        def neighbours_arrive():
            copy(1, (*chips[0], c), me).wait_recv()
            copy(4, (*chips[0], c), sibling).start()

            @pl.when(early)
            def _():
                copy(3, (*relayed, c), (*relay, c)).start()
                copy(5, (*chips[1], 1 - c), me).wait_recv()

            @pl.when(jnp.logical_not(early))
            def _():
                copy(2, (*chips[1], c), me).wait_recv()
                copy(5, (*chips[1], c), sibling).start()
                copy(3, (*relayed, c), (*relay, c)).start()

        def diagonal_arrives():
            copy(3, (*chips[2], c), me).wait_recv()
            copy(6, (*chips[2], c), sibling).start()
            copy(6, (*chips[2], 1 - c), me).wait_recv()

        def tile_load(p):
            slot = p % 2
            rows = wt_ref.at[pl.ds(pl.multiple_of(sched_ref[p] * PROJ_TN, 16), PROJ_TN), :]
            return pltpu.make_async_copy(rows, w_tile.at[slot], w_sems.at[slot])

        def prepare(p):
            p = jnp.asarray(p, jnp.int32)

            @pl.when(p == 0)
            def _():
                copy(0, sibling, me).wait_recv()
                stage_out.wait()

            pl.when(p == 1)(early_block_arrives)
            pl.when(p == 2)(neighbours_arrive)
            pl.when(p == 3)(lambda: copy(4, (*chips[0], 1 - c), me).wait_recv())
            pl.when(p == sched_ref[N_PROJ_TILES])(diagonal_arrives)
            tile_load(p).start()

        @pl.when(i == 0)
        def _():
            stage_in.start()
            copy(0, me, sibling).start()
            copy(1, me, (*chips[0], c)).start()
            pl.when(jnp.logical_not(early))(y_send.start)
            stage_in.wait()
            stage_out.start()

        @pl.when(i < n_pre)
        def _():
            xv = x_ref[...]
            r = lax.rsqrt(_row_mean(xv * xv) + EPS)
            hv = (xv * r * g_ref[...]).astype(BF16)
            h_ref[...] = hv
            h_all[pl.ds(pl.multiple_of(i * tp, tp), tp), :] = hv
            ang = pos_ref[...].astype(F32) * invf_ref[...]
            cos_ref[...] = jnp.cos(ang)
            sin_ref[...] = jnp.sin(ang)

        pl.when(i == n_pre - 1)(lambda: prepare(0))

        @pl.when(i >= n_pre)
        def _():
            p, s = pos_of(i), row_of(i)
            pl.when(s == 0)(lambda: tile_load(p).wait())
            pl.when((s == ns - 1) & (p < N_PROJ_TILES - 1))(lambda: prepare(p + 1))
            hv = h_all[pl.ds(pl.multiple_of(s * tm, tm), tm), :]
            proj_ref[...] = _dot(hv, w_tile[p % 2], NT) + b_ref[...]

        @pl.when(i == n_steps - 1)
        def _():
            for cp in own_sends() + passed_on() + [copy(6, (*chips[2], c), sibling)]:
                cp.wait_send()

    return pl.pallas_call(
        body, name="prenorm_inproj",
        grid_spec=pltpu.PrefetchScalarGridSpec(
            num_scalar_prefetch=1, grid=(n_steps,),
            in_specs=[pl.BlockSpec((tp, D), lambda i, sc: (jnp.minimum(i, n_pre - 1), 0)),
                      pl.BlockSpec((1, D), lambda i, sc: (0, 0)),
                      pl.BlockSpec((1, PROJ_TN), lambda i, sc: (0, sc[pos_of(i)])),
                      HBM_SPEC,
                      pl.BlockSpec((tp, 1), lambda i, sc: (jnp.minimum(i, n_pre - 1), 0)),
                      pl.BlockSpec((1, 128), lambda i, sc: (0, 0))],
            out_specs=[pl.BlockSpec((tp, D), lambda i, sc: (jnp.minimum(i, n_pre - 1), 0)),
                       pl.BlockSpec((tm, PROJ_TN), lambda i, sc: (row_of(i), sc[pos_of(i)])),
                       HBM_SPEC,
                       pl.BlockSpec((tp, 128), lambda i, sc: (jnp.minimum(i, n_pre - 1), 0)),
                       pl.BlockSpec((tp, 128), lambda i, sc: (jnp.minimum(i, n_pre - 1), 0))],
            scratch_shapes=[pltpu.VMEM((S, D), BF16), pltpu.VMEM((2, PROJ_TN, D), BF16),
                            pltpu.VMEM((W_IN_ROWS, D), BF16),
                            pltpu.SemaphoreType.DMA((7,)), pltpu.SemaphoreType.DMA((7,)),
                            pltpu.SemaphoreType.DMA((2,)), pltpu.SemaphoreType.DMA((2,))]),
        out_shape=[jax.ShapeDtypeStruct((S, D), BF16), jax.ShapeDtypeStruct((S, D_IN), F32),
                   jax.ShapeDtypeStruct((D_IN, D), BF16),
                   jax.ShapeDtypeStruct((S, 128), F32), jax.ShapeDtypeStruct((S, 128), F32)],
        compiler_params=_params(("arbitrary",)),
    )(sched, x, g, bias, wt_part, pos, invf)


def _rot_half(xs, first_half):
    return jnp.where(first_half, -pltpu.roll(xs, 96, 1), pltpu.roll(xs, 32, 1))


GH = NQ // 2


MASKED = -1e30


def _attn_consts():
    lane = lax.broadcasted_iota(jnp.int32, (T, 128), 1)
    row = lax.broadcasted_iota(jnp.int32, (GH * T, T), 0) & (T - 1)
    on_diag_or_below = row >= lax.broadcasted_iota(jnp.int32, (GH * T, T), 1)
    return (lane & (HD - 1)) < (HD // 2), lane < HD, on_diag_or_below


def _stack_heads(slab_fn, grp, lo64):
    blocks = []
    for jj in range(GH // 2):
        s = slab_fn(GH // 2 * grp + jj)
        blocks += [jnp.where(lo64, s, 0.0), jnp.where(lo64, 0.0, s)]
    return jnp.concatenate(blocks, axis=0)


def _unstack_heads(stacked, jj, lo64):
    return jnp.where(lo64, stacked[(2 * jj) * T:(2 * jj + 1) * T], stacked[(2 * jj + 1) * T:(2 * jj + 2) * T])


def _both_halves(kv, grp):
    lo = lax.broadcasted_iota(jnp.int32, kv.shape, 1) < HD
    swapped = pltpu.roll(kv, HD, 1)
    return jnp.where(lo, kv, swapped) if grp == 0 else jnp.where(lo, swapped, kv)


def _one_half(acc, grp):
    lo = lax.broadcasted_iota(jnp.int32, acc.shape, 1) < HD
    return jnp.where(lo if grp == 0 else jnp.logical_not(lo), acc + pltpu.roll(acc, HD, 1), 0.0)


def _layer_norm_stats(v):
    mu = _row_mean(v)
    xc = v - mu
    var = _row_mean(xc * xc)
    rs = lax.rsqrt(var + EPS)
    return xc * rs, rs


def _fold(both, own):
    return jnp.where(own, both[:, T:2 * T], both[:, 0:T])


def _unfold(p, own):
    return jnp.concatenate([jnp.where(own, 0.0, p), jnp.where(own, p, 0.0)], axis=1).astype(BF16)


def _band_softmax(q_scaled, k_both, own, has_prev, sink):
    s_both = _dot(q_scaled, k_both, NT)
    s_prev = s_both[:, 0:T]
    if has_prev is not None:
        s_prev = s_prev + jnp.where(has_prev, 0.0, MASKED)
    s = jnp.where(own, s_both[:, T:2 * T], s_prev)
    m = jnp.maximum(jnp.max(s, axis=-1, keepdims=True), sink)
    e = jnp.exp(s - m)
    es = jnp.exp(sink - m)
    inv = 1.0 / (jnp.sum(e, axis=-1, keepdims=True) + es)
    return e * inv, es * inv


BPS_FWD = 4
BPS_BWD = 4


def _mid_specs(nt, rev, bps):
    tile = (lambda i: nt - 1 - i) if rev else (lambda i: i)
    prev = lambda i: jnp.maximum(bps * tile(i) - 1, 0)
    rows = bps * T
    return tile, [
        pl.BlockSpec((rows, D_IN), lambda i: (tile(i), 0)),
        pl.BlockSpec((T, 2 * T), lambda i: (prev(i), OFF_K // (2 * T))),
    ], [
        pl.BlockSpec((rows, 128), lambda i: (tile(i), 0)),
        pl.BlockSpec((rows, 128), lambda i: (tile(i), 0)),
        pl.BlockSpec((T, 128), lambda i: (prev(i), 0)),
        pl.BlockSpec((T, 128), lambda i: (prev(i), 0)),
        pl.BlockSpec((1, DG), lambda i: (0, 0)),
        pl.BlockSpec((1, DG), lambda i: (0, 0)),
        pl.BlockSpec((NG, T, T), lambda i: (0, 0, 0)),
        pl.BlockSpec((NG, T, 1), lambda i: (0, 0, 0)),
        pl.BlockSpec((NQ * T, 1), lambda i: (0, 0)),
    ]


def _rope(xs, cosv, sinv, first_half):
    return xs * cosv + _rot_half(xs, first_half) * sinv


def _mid_fwd(proj, cos, sin, ln_g, ln_b, w_s, b_s, sinks, wo_part):
    S = proj.shape[0]
    BPS = BPS_FWD if S % (BPS_FWD * T) == 0 else 1
    nt = S // (BPS * T)
    tile, proj_specs, par_specs = _mid_specs(nt, False, BPS)

    def body(p_ref, kvp_ref, cos_ref, sin_ref, cosp_ref, sinp_ref, lng_ref, lnb_ref, ws_ref, bs_ref, sink_ref,
             wpart_ref, y_ref, wo_ref, q_ref, wm_ref, send_sems, recv_sems, local_sems):
        i = pl.program_id(0)
        first_half, lo64, own = _attn_consts()
        gather = _RowGather(wpart_ref, wo_ref, W_OUT_ROWS, send_sems, recv_sems, local_sems)
        pl.when(i == 0)(gather.start)
        pl.when(i == nt // 2)(gather.pass_on)

        @pl.when(i == 0)
        def _():
            tril = lax.broadcasted_iota(jnp.int32, (T, T), 0) >= lax.broadcasted_iota(jnp.int32, (T, T), 1)
            for g in range(NG):
                wm_ref[g] = jnp.where(tril, ws_ref[g], 0.0).astype(BF16)

        def block(b, k_prev, v_prev, has_prev):
            rows = slice(b * T, (b + 1) * T)
            xhat, _ = _layer_norm_stats(p_ref[rows, OFF_V:OFF_V + DG])
            vn = xhat * lng_ref[...] + lnb_ref[...]
            for g in range(NG):
                sl = slice(128 * g, 128 * g + 128)
                mixed = _dot(wm_ref[g], vn[:, sl].astype(BF16), NN) + bs_ref[g]
                z = p_ref[rows, OFF_ZA + 128 * g:OFF_ZA + 128 * g + 128]
                u = p_ref[rows, OFF_U + 128 * g:OFF_U + 128 * g + 128]
                y_ref[rows, sl] = (u * mixed * (z * _sigmoid(z))).astype(BF16)

            cosv, sinv = cos_ref[rows, :], sin_ref[rows, :]
            k_cur = _rope(p_ref[rows, OFF_K:OFF_K + 128], cosv, sinv, first_half)
            v_cur = p_ref[rows, OFF_VA:OFF_VA + 128]

            def q_slab(j):
                q = (_rope(p_ref[rows, OFF_Q + 128 * j:OFF_Q + 128 * j + 128], cosv, sinv, first_half)
                     * (HD ** -0.5)).astype(BF16)
                q_ref[rows, 128 * j:128 * j + 128] = q
                return q

            k_both = jnp.concatenate([k_prev, k_cur], axis=0)
            v_both = jnp.concatenate([v_prev, v_cur], axis=0)
            for grp in range(2):
                qs = _stack_heads(q_slab, grp, lo64)
                kg, vg = _both_halves(k_both, grp).astype(BF16), _both_halves(v_both, grp).astype(BF16)
                p, _ = _band_softmax(qs, kg, own, has_prev, sink_ref[GH * T * grp:GH * T * (grp + 1), :])
                o_st = _dot(_unfold(p, own), vg, NN)
                for jj in range(GH // 2):
                    c0 = 128 * (GH // 2 * grp + jj)
                    zb = p_ref[rows, OFF_ZB + c0:OFF_ZB + c0 + 128]
                    o = _unstack_heads(o_st, jj, lo64)
                    y_ref[rows, DG + c0:DG + c0 + 128] = (o * (zb * _sigmoid(zb))).astype(BF16)
            return k_cur, v_cur

        k_prev = _rope(kvp_ref[:, 0:128], cosp_ref[...], sinp_ref[...], first_half)
        kv = block(0, k_prev, kvp_ref[:, 128:256], i > 0)
        for b in range(1, BPS):
            kv = block(b, *kv, None)
        pl.when(i == nt - 1)(gather.finish)

    return pl.pallas_call(
        body, name="mid_fwd", grid=(nt,),
        in_specs=proj_specs + par_specs + [HBM_SPEC],
        out_specs=[pl.BlockSpec((BPS * T, D), lambda i: (tile(i), 0)), HBM_SPEC,
                   pl.BlockSpec((BPS * T, NQ * HD), lambda i: (tile(i), 0))],
        out_shape=[jax.ShapeDtypeStruct((S, D), BF16), jax.ShapeDtypeStruct((D, D), BF16),
                   jax.ShapeDtypeStruct((S, NQ * HD), BF16)],
        scratch_shapes=[pltpu.VMEM((NG, T, T), BF16)] + GATHER_SEMS,
        compiler_params=_params(("arbitrary",)),
    )(proj, proj, cos, sin, cos, sin, ln_g, ln_b, w_s, b_s, sinks, wo_part)


def _mid_bwd(proj, dycat, q_roped, cos, sin, ln_g, ln_b, w_s, b_s, sinks, po):
    S = proj.shape[0]
    BPS = BPS_BWD if S % (BPS_BWD * T) == 0 else 1
    nt = S // (BPS * T)
    tile, proj_specs, par_specs = _mid_specs(nt, True, BPS)
    const2 = lambda i: (0, 0)

    def body(p_ref, kvp_ref, dyc_ref, q_ref, cos_ref, sin_ref, cosp_ref, sinp_ref, lng_ref, lnb_ref, ws_ref, bs_ref,
             sink_ref, po_ref, dp_ref, dlng_ref, dlnb_ref, dws_ref, dbs_ref, dsink_ref, dbqkv_ref, bo_ref,
             carry_ref, dvn_ref, wm_ref, wmt_ref, send_sems, recv_sems, local_sem):
        i = pl.program_id(0)
        first_half, lo64, own = _attn_consts()
        tril = lax.broadcasted_iota(jnp.int32, (T, T), 0) >= lax.broadcasted_iota(jnp.int32, (T, T), 1)
        exchange = _OwnerExchange(po_ref, bo_ref, send_sems, recv_sems, local_sem)
        pl.when(i == 0)(exchange.start)

        @pl.when(i == 0)
        def _():
            for g in range(NG):
                wm = jnp.where(tril, ws_ref[g], 0.0)
                wm_ref[g] = wm.astype(BF16)
                wmt_ref[g] = wm.T.astype(BF16)
            dlng_ref[...] = jnp.zeros_like(dlng_ref)
            dlnb_ref[...] = jnp.zeros_like(dlnb_ref)
            dws_ref[...] = jnp.zeros_like(dws_ref)
            dbs_ref[...] = jnp.zeros_like(dbs_ref)
            dsink_ref[...] = jnp.zeros_like(dsink_ref)
            dbqkv_ref[...] = jnp.zeros_like(dbqkv_ref)
            carry_ref[...] = jnp.zeros_like(carry_ref)

        def roped_k(b):
            rows = slice(b * T, (b + 1) * T)
            return _rope(p_ref[rows, OFF_K:OFF_K + 128], cos_ref[rows, :], sin_ref[rows, :], first_half)

        def block(b, k_prev, v_prev, has_prev, dk_next, dv_next):
            rows = slice(b * T, (b + 1) * T)
            xhat, rs = _layer_norm_stats(p_ref[rows, OFF_V:OFF_V + DG])
            lng = lng_ref[...]
            vn = xhat * lng + lnb_ref[...]
            for g in range(NG):
                sl = slice(128 * g, 128 * g + 128)
                vng = vn[:, sl].astype(BF16)
                mixed = _dot(wm_ref[g], vng, NN) + bs_ref[g]
                z = p_ref[rows, OFF_ZA + 128 * g:OFF_ZA + 128 * g + 128]
                u = p_ref[rows, OFF_U + 128 * g:OFF_U + 128 * g + 128]
                dy = dyc_ref[rows, sl]
                sg = _sigmoid(z)
                sa = z * sg
                dp_ref[rows, OFF_U + 128 * g:OFF_U + 128 * g + 128] = (dy * mixed * sa).astype(BF16)
                dp_ref[rows, OFF_ZA + 128 * g:OFF_ZA + 128 * g + 128] = (
                    dy * u * mixed * (sg * (1.0 + z * (1.0 - sg)))).astype(BF16)
                dm = dy * u * sa
                dmb = dm.astype(BF16)
                dvn_ref[rows, sl] = _dot(wmt_ref[g], dmb, NN)
                dws_ref[g] += jnp.where(tril, _dot(dmb, vng, NT), 0.0)
                dbs_ref[g] += jnp.sum(dm, axis=1, keepdims=True)
            dvn = dvn_ref[rows, :]
            dlng_ref[...] += jnp.sum(dvn * xhat, axis=0, keepdims=True)
            dlnb_ref[...] += jnp.sum(dvn, axis=0, keepdims=True)
            dxh = dvn * lng
            dv_g = rs * (dxh - _row_mean(dxh)
                         - xhat * _row_mean(dxh * xhat))
            dp_ref[rows, OFF_V:OFF_V + DG] = dv_g.astype(BF16)

            cosv, sinv = cos_ref[rows, :], sin_ref[rows, :]
            k_cur = roped_k(b)
            v_cur = p_ref[rows, OFF_VA:OFF_VA + 128]

            def q_slab(j):
                return q_ref[rows, 128 * j:128 * j + 128]

            def do_slab(j):
                zb = p_ref[rows, OFF_ZB + 128 * j:OFF_ZB + 128 * j + 128]
                return dyc_ref[rows, DG + 128 * j:DG + 128 * j + 128] * (zb * _sigmoid(zb))

            k_both = jnp.concatenate([k_prev, k_cur], axis=0)
            v_both = jnp.concatenate([v_prev, v_cur], axis=0)
            dk_both, dv_both = jnp.zeros((2 * T, 128), F32), jnp.zeros((2 * T, 128), F32)
            for grp in range(2):
                qs = _stack_heads(q_slab, grp, lo64)
                d_o = _stack_heads(do_slab, grp, lo64)
                dob = d_o.astype(BF16)
                kg, vg = _both_halves(k_both, grp).astype(BF16), _both_halves(v_both, grp).astype(BF16)
                p, ps = _band_softmax(qs, kg, own, has_prev, sink_ref[GH * T * grp:GH * T * (grp + 1), :])
                p_both = _unfold(p, own)
                o_st = _dot(p_both, vg, NN)
                delta = jnp.sum(d_o * o_st, axis=-1, keepdims=True)
                ds_both = _unfold(p * (_fold(_dot(dob, vg, NT), own) - delta), own)
                dq_st = _dot(ds_both, kg, NN) * (HD ** -0.5)
                dk_both = dk_both + _one_half(_dot(ds_both, qs, TN), grp)
                dv_both = dv_both + _one_half(_dot(p_both, dob, TN), grp)
                dsink_rows = ps * delta
                for hh in range(GH):
                    h = GH * grp + hh
                    dsink_ref[h:h + 1, :] += jnp.broadcast_to(
                        -jnp.sum(dsink_rows[hh * T:(hh + 1) * T], axis=0, keepdims=True), (1, 128))
                for jj in range(GH // 2):
                    c0 = 128 * (GH // 2 * grp + jj)
                    zb = p_ref[rows, OFF_ZB + c0:OFF_ZB + c0 + 128]
                    sg = _sigmoid(zb)
                    o = _unstack_heads(o_st, jj, lo64)
                    dp_ref[rows, OFF_ZB + c0:OFF_ZB + c0 + 128] = (
                        dyc_ref[rows, DG + c0:DG + c0 + 128] * o * (sg * (1.0 + zb * (1.0 - sg)))).astype(BF16)
                    dq = _unstack_heads(dq_st, jj, lo64)
                    dq_pre = dq * cosv - _rot_half(dq, first_half) * sinv
                    dp_ref[rows, OFF_Q + c0:OFF_Q + c0 + 128] = dq_pre.astype(BF16)
                    dbqkv_ref[:, c0:c0 + 128] += jnp.sum(dq_pre, axis=0, keepdims=True)
            dk_prev, dv_prev = dk_both[0:T], dv_both[0:T]
            dk_cur, dv_cur = dk_both[T:2 * T] + dk_next, dv_both[T:2 * T] + dv_next
            dk_pre = dk_cur * cosv - _rot_half(dk_cur, first_half) * sinv
            dp_ref[rows, OFF_K:OFF_K + 128] = dk_pre.astype(BF16)
            dp_ref[rows, OFF_VA:OFF_VA + 128] = dv_cur.astype(BF16)
            dbqkv_ref[:, 1024:1152] += jnp.sum(dk_pre, axis=0, keepdims=True)
            dbqkv_ref[:, 1152:1280] += jnp.sum(dv_cur, axis=0, keepdims=True)
            return dk_prev, dv_prev

        grads = carry_ref[:, 0:128], carry_ref[:, 128:256]
        for b in range(BPS - 1, 0, -1):
            prows = slice((b - 1) * T, b * T)
            grads = block(b, roped_k(b - 1), p_ref[prows, OFF_VA:OFF_VA + 128], None, *grads)
        k_prev = _rope(kvp_ref[:, 0:128], cosp_ref[...], sinp_ref[...], first_half)
        grads = block(0, k_prev, kvp_ref[:, 128:256], i < nt - 1, *grads)
        carry_ref[:, 0:128], carry_ref[:, 128:256] = grads
        pl.when(i == nt - 1)(exchange.finish)

    return pl.pallas_call(
        body, name="mid_bwd", grid=(nt,),
        in_specs=proj_specs + [pl.BlockSpec((BPS * T, D), lambda i: (tile(i), 0)),
                               pl.BlockSpec((BPS * T, NQ * HD), lambda i: (tile(i), 0))] + par_specs + [HBM_SPEC],
        out_specs=[pl.BlockSpec((BPS * T, D_IN), lambda i: (tile(i), 0)),
                   pl.BlockSpec((1, DG), const2), pl.BlockSpec((1, DG), const2),
                   pl.BlockSpec((NG, T, T), lambda i: (0, 0, 0)), pl.BlockSpec((NG, T, 1), lambda i: (0, 0, 0)),
                   pl.BlockSpec((NQ, 128), const2), pl.BlockSpec((1, D_QKV), const2), HBM_SPEC],
        out_shape=[jax.ShapeDtypeStruct((S, D_IN), BF16),
                   jax.ShapeDtypeStruct((1, DG), F32), jax.ShapeDtypeStruct((1, DG), F32),
                   jax.ShapeDtypeStruct((NG, T, T), F32), jax.ShapeDtypeStruct((NG, T, 1), F32),
                   jax.ShapeDtypeStruct((NQ, 128), F32), jax.ShapeDtypeStruct((1, D_QKV), F32),
                   jax.ShapeDtypeStruct(po.shape, BF16)],
        scratch_shapes=[pltpu.VMEM((T, 2 * T), F32), pltpu.VMEM((BPS * T, DG), F32),
                        pltpu.VMEM((NG, T, T), BF16), pltpu.VMEM((NG, T, T), BF16)] + OWNER_SEMS,
        compiler_params=_params(("arbitrary",)),
    )(proj, proj, dycat, q_roped, cos, sin, cos, sin, ln_g, ln_b, w_s, b_s, sinks, po)


def _outproj_loss(ycat, wo, x, target, g_post):
    S = ycat.shape[0]
    tm = _tile(S, 512)
    nt = S // tm
    n_part = 2 if tm % 32 == 0 else 1
    tp = tm // n_part
    const2 = lambda i: (0, 0)

    def body(yc_ref, w_ref, x_ref, t_ref, g_ref, dy_ref, dout_ref, loss_ref, dg_ref, lacc_ref):
        i = pl.program_id(0)

        @pl.when(i == 0)
        def _():
            dg_ref[...] = jnp.zeros_like(dg_ref)
            lacc_ref[...] = jnp.zeros_like(lacc_ref)

        g = g_ref[...]
        ys = [_dot(yc_ref[q * tp:(q + 1) * tp, :], w_ref[...], NN) for q in range(n_part)]
        for q, y in enumerate(ys):
            rows = slice(q * tp, (q + 1) * tp)
            r = lax.rsqrt(_row_mean(y * y) + EPS)
            yh = y * r
            diff = x_ref[rows, :] + yh * g - t_ref[rows, :]
            lacc_ref[...] += jnp.sum(diff * diff, axis=0, keepdims=True)
            dout = diff * (1.0 / D)
            dout_ref[rows, :] = dout
            dg_ref[...] += jnp.sum(dout * yh, axis=0, keepdims=True)
            dyh = dout * g
            dy_ref[rows, :] = (r * (dyh - yh * _row_mean(dyh * yh))).astype(BF16)

        @pl.when(i == nt - 1)
        def _():
            loss_ref[...] = jnp.broadcast_to(jnp.sum(lacc_ref[...], axis=1, keepdims=True) * (0.5 / D), (1, 128))

    row = lambda i: (i, 0)
    return pl.pallas_call(
        body, name="outproj_loss", grid=(nt,),
        in_specs=[pl.BlockSpec((tm, D), row), pl.BlockSpec((D, D), const2, pipeline_mode=pl.Buffered(1)),
                  pl.BlockSpec((tm, D), row), pl.BlockSpec((tm, D), row), pl.BlockSpec((1, D), const2)],
        out_specs=[pl.BlockSpec((tm, D), row), pl.BlockSpec((tm, D), row), pl.BlockSpec((1, 128), const2),
                   pl.BlockSpec((1, D), const2)],
        out_shape=[jax.ShapeDtypeStruct((S, D), BF16), jax.ShapeDtypeStruct((S, D), F32),
                   jax.ShapeDtypeStruct((1, 128), F32), jax.ShapeDtypeStruct((1, D), F32)],
        scratch_shapes=[pltpu.VMEM((1, D), F32)],
        compiler_params=_params(("arbitrary",)),
    )(ycat, wo, x, target, g_post)


def _dycat(dy, wo, dwo):
    S = dy.shape[0]
    tm = _tile(S, 1024)
    nt = S // tm

    def body(dy_ref, w_ref, dwo_ref, o_ref, r_ref, send_sems, recv_sems):
        i = pl.program_id(0)
        copies = _sibling_copies(dwo_ref, r_ref, W_OUT_ROWS, send_sems, recv_sems)

        @pl.when(i == 0)
        def _():
            for cp in copies:
                cp.start()

        o_ref[...] = _dot(dy_ref[...], w_ref[...], NT)

        @pl.when(i == nt - 1)
        def _():
            for cp in copies:
                cp.wait_recv()
            for cp in copies:
                cp.wait_send()

    return pl.pallas_call(
        body, name="dycat", grid=(nt,),
        in_specs=[pl.BlockSpec((tm, D), lambda i: (i, 0)),
                  pl.BlockSpec((D, D), lambda i: (0, 0), pipeline_mode=pl.Buffered(1)), HBM_SPEC],
        out_specs=[pl.BlockSpec((tm, D), lambda i: (i, 0)), HBM_SPEC],
        out_shape=[jax.ShapeDtypeStruct((S, D), F32), jax.ShapeDtypeStruct((N_CHIPS, W_OUT_ROWS, D), BF16)],
        scratch_shapes=[pltpu.SemaphoreType.DMA((N_CHIPS,)), pltpu.SemaphoreType.DMA((N_CHIPS,))],
        compiler_params=_params(("arbitrary",)),
    )(dy, wo, dwo)


def _block_segments(tm, n_tiles, half_rows):
    segs, sid = [], 0
    for t in range(n_tiles):
        segs.append([])
        for b in range(tm * t // half_rows, (tm * (t + 1) - 1) // half_rows + 1):
            segs[-1].append((sid, b, max(tm * t, half_rows * b), min(tm * (t + 1), half_rows * (b + 1))))
            sid += 1
    return segs, sid


def _matmul_tn(a, b, tm, name, mat_grads=None, half_rows=None):
    K, M = a.shape
    N = b.shape[1]
    tk = _tile(K, 2048)
    ni, nk = M // tm, K // tk
    hosting = mat_grads is not None

    def body(a_ref, b_ref, *rest):
        if hosting:
            g_in, (o_ref, tot_ref, r_ref, acc_ref, mbuf, send_sems, recv_sems, stage, seg_send, seg_recv) = (
                rest[:len(MAT)], rest[len(MAT):])
        else:
            o_ref, acc_ref = rest
        i, k = pl.program_id(0), pl.program_id(1)

        if hosting:
            gather = _SlotGather(mbuf, send_sems, recv_sems)

            @pl.when((i == 0) & (k == 0))
            def _():
                r0 = 0
                for q, (_, rows) in enumerate(MAT):
                    mbuf[gather.my_id, r0:r0 + rows, :] = g_in[q][...]
                    r0 += rows
                gather.start()

            pl.when((i == ni // 2) & (k == 0))(gather.pass_on)

            x_, y_, c = _mesh_pos()
            segs, _ = _block_segments(tm, ni, half_rows)

            def seg_copy(p, sid, blk, lo, hi):
                return pltpu.make_async_remote_copy(
                    src_ref=stage.at[p % 2, pl.ds(lo - tm * order[p], hi - lo), :],
                    dst_ref=r_ref.at[blk // 2, pl.ds(lo - half_rows * blk, hi - lo), :],
                    send_sem=seg_send.at[sid], recv_sem=seg_recv.at[sid],
                    device_id=(x_, y_, 1 - c), device_id_type=pl.DeviceIdType.MESH)

            def sent(p):
                for sid, blk, lo, hi in segs[order[p]]:
                    pl.when(c == 1 - blk % 2)(seg_copy(p, sid, blk, lo, hi).wait_send)

            for p in range(2, ni):
                pl.when((i == p) & (k == nk - 1))(lambda p=p: sent(p - 2))

        @pl.when(k == 0)
        def _():
            acc_ref[...] = jnp.zeros_like(acc_ref)

        acc_ref[...] += _dot(a_ref[...], b_ref[...], TN)

        @pl.when(k == nk - 1)
        def _():
            tile = acc_ref[...].astype(BF16)
            o_ref[...] = tile
            if hosting:
                stage[i % 2] = tile

        if hosting:
            for p in range(ni):
                @pl.when((i == p) & (k == nk - 1))
                def _(p=p):
                    for sid, blk, lo, hi in segs[order[p]]:
                        pl.when(c == 1 - blk % 2)(seg_copy(p, sid, blk, lo, hi).start)

            @pl.when((i == ni - 1) & (k == nk - 1))
            def _():
                gather.finish()
                tot_ref[...] = gather.total()
                sent(ni - 2)
                sent(ni - 1)
                for p in range(ni):
                    for sid, blk, lo, hi in segs[order[p]]:
                        pl.when(c == blk % 2)(seg_copy(p, sid, blk, lo, hi).wait_recv)

    mid = ni // 2
    order = [t for t in range(ni) if t != mid] + [mid] if hosting else list(range(ni))

    def tile_of(i):
        return jnp.where(i == ni - 1, mid, jnp.where(i < mid, i, i + 1)) if hosting else i

    whole = pl.BlockSpec(memory_space=pltpu.VMEM)
    in_specs = [pl.BlockSpec((tk, tm), lambda i, k: (k, tile_of(i))), pl.BlockSpec((tk, N), lambda i, k: (k, 0))]
    out_specs = [pl.BlockSpec((tm, N), lambda i, k: (tile_of(i), 0))]
    out_shape = [jax.ShapeDtypeStruct((M, N), BF16)]
    scratch = [pltpu.VMEM((tm, N), F32)]
    if hosting:
        in_specs += [whole] * len(MAT)
        n_segs = _block_segments(tm, ni, half_rows)[1]
        out_specs += [whole, HBM_SPEC]
        out_shape += [jax.ShapeDtypeStruct((MAT_ROWS, 128), F32),
                      jax.ShapeDtypeStruct((M // (2 * half_rows), half_rows, N), BF16)]
        scratch += [pltpu.VMEM((N_DEV, MAT_ROWS, 128), F32), pltpu.SemaphoreType.DMA((7,)),
                    pltpu.SemaphoreType.DMA((7,)), pltpu.VMEM((2, tm, N), BF16),
                    pltpu.SemaphoreType.DMA((n_segs,)), pltpu.SemaphoreType.DMA((n_segs,))]
    res = pl.pallas_call(
        body, name=name, grid=(ni, nk), in_specs=in_specs, out_specs=out_specs, out_shape=out_shape,
        scratch_shapes=scratch,
        compiler_params=_params(("arbitrary", "arbitrary") if hosting else ("parallel", "arbitrary")),
    )(a, b, *(mat_grads or ()))
    return res if hosting else res[0]


def _dh_prenorm_bwd(dproj, wt, x, dout, g_pre, pt):
    S = x.shape[0]
    tm, tk = _tile(S, 1024), 768
    ne = 4
    te = tm // ne
    ni, nk = S // tm, D_IN // tk
    n_mm = ni * nk
    n_steps = n_mm + ne
    mm_of = lambda t: jnp.minimum(t, n_mm - 1)

    def quarter_of(t):
        j, q = t // nk - 1, t % nk
        nxt = jnp.where(q < ne, j * ne + q, (j + 1) * ne)
        return jnp.clip(nxt, 0, ni * ne - 1)

    def body(dp_ref, w_ref, x_ref, dout_ref, g_ref, pt_ref, gx_ref, dg_ref, bt_ref, acc_ref,
             send_sems, recv_sems, local_sem):
        t = pl.program_id(0)
        exchange = _OwnerExchange(pt_ref, bt_ref, send_sems, recv_sems, local_sem)
        pl.when(t == 0)(exchange.start)

        @pl.when(t == 0)
        def _():
            dg_ref[...] = jnp.zeros_like(dg_ref)

        has_mm = t < n_mm
        has_q = (t >= nk) & (t % nk < ne)

        def matmul():
            slot = (mm_of(t) // nk) % 2
            old = jnp.where(mm_of(t) % nk > 0, acc_ref[slot], 0.0)
            acc_ref[slot] = old + _dot(dp_ref[...], w_ref[...], NN)

        def quarter():
            slot = (t // nk - 1) % 2
            dh = acc_ref[slot, pl.ds(pl.multiple_of((t % nk) * te, te), te), :]
            xv = x_ref[...]
            r = lax.rsqrt(_row_mean(xv * xv) + EPS)
            xh = xv * r
            dg_ref[...] += jnp.sum(dh * xh, axis=0, keepdims=True)
            dxh = dh * g_ref[...]
            gx_ref[...] = dout_ref[...] + r * (dxh - xh * _row_mean(dxh * xh))

        @pl.when(has_mm & has_q)
        def _():
            matmul()
            quarter()

        pl.when(has_mm & jnp.logical_not(has_q))(matmul)
        pl.when(jnp.logical_not(has_mm) & has_q)(quarter)
        pl.when(t == n_steps - 1)(exchange.finish)

    quarter_block = lambda t: (quarter_of(t), 0)
    return pl.pallas_call(
        body, name="dh_prenorm_bwd", grid=(n_steps,),
        in_specs=[pl.BlockSpec((tm, tk), lambda t: (mm_of(t) // nk, mm_of(t) % nk)),
                  pl.BlockSpec((tk, D), lambda t: (mm_of(t) % nk, 0)),
                  pl.BlockSpec((te, D), quarter_block), pl.BlockSpec((te, D), quarter_block),
                  pl.BlockSpec((1, D), lambda t: (0, 0)), HBM_SPEC],
        out_specs=[pl.BlockSpec((te, D), quarter_block), pl.BlockSpec((1, D), lambda t: (0, 0)), HBM_SPEC],
        out_shape=[jax.ShapeDtypeStruct((S, D), F32), jax.ShapeDtypeStruct((1, D), F32),
                   jax.ShapeDtypeStruct(pt.shape, BF16)],
        scratch_shapes=[pltpu.VMEM((2, tm, D), F32)] + OWNER_SEMS,
        input_output_aliases={3: 0},
        compiler_params=_params(("arbitrary",)),
    )(dproj, wt, x, dout, g_pre, pt)


def _presum(c_arr, own, recv, half_rows):
    n_cols = own.shape[-1]
    own4 = own.reshape(N_CHIPS, 2, half_rows, n_cols)

    def body(c_ref, own_ref, recv_ref, o_ref):
        o_ref[...] = (own_ref[...].astype(F32) + recv_ref[...].astype(F32)).astype(BF16)

    return pl.pallas_call(
        body, name="presum_%d" % half_rows,
        grid_spec=pltpu.PrefetchScalarGridSpec(
            num_scalar_prefetch=1, grid=(N_CHIPS,),
            in_specs=[pl.BlockSpec((None, None, half_rows, n_cols), lambda j, c: (j, c[0], 0, 0)),
                      pl.BlockSpec((None, half_rows, n_cols), lambda j, c: (j, 0, 0))],
            out_specs=pl.BlockSpec((None, half_rows, n_cols), lambda j, c: (j, 0, 0))),
        out_shape=jax.ShapeDtypeStruct((N_CHIPS, half_rows, n_cols), BF16),
        compiler_params=_params(("parallel",)),
    )(c_arr, own4, recv)


def _sibling_presum(c_arr, dw, half_rows, name):
    n_cols = dw.shape[-1]
    dw4 = dw.reshape(N_CHIPS, 2, half_rows, n_cols)

    def body(c_ref, own_ref, dw_ref, o_ref, r_ref, stage, send_sems, recv_sems, load_sem):
        j = pl.program_id(0)
        copies = _sibling_copies(dw_ref, r_ref, half_rows, send_sems, recv_sems)

        @pl.when(j == 0)
        def _():
            for cp in copies:
                cp.start()

        for q in range(N_CHIPS):
            pl.when(j == q)(copies[q].wait_recv)
        load = pltpu.make_async_copy(r_ref.at[j], stage, load_sem)
        load.start()
        load.wait()
        o_ref[...] = (own_ref[...].astype(F32) + stage[...].astype(F32)).astype(BF16)

        @pl.when(j == N_CHIPS - 1)
        def _():
            for cp in copies:
                cp.wait_send()

    out, _ = pl.pallas_call(
        body, name=name,
        grid_spec=pltpu.PrefetchScalarGridSpec(
            num_scalar_prefetch=1, grid=(N_CHIPS,),
            in_specs=[pl.BlockSpec((None, None, half_rows, n_cols), lambda j, c: (j, c[0], 0, 0)), HBM_SPEC],
            out_specs=[pl.BlockSpec((None, half_rows, n_cols), lambda j, c: (j, 0, 0)), HBM_SPEC],
            scratch_shapes=[pltpu.VMEM((half_rows, n_cols), BF16), pltpu.SemaphoreType.DMA((N_CHIPS,)),
                            pltpu.SemaphoreType.DMA((N_CHIPS,)), pltpu.SemaphoreType.DMA(())]),
        out_shape=[jax.ShapeDtypeStruct((N_CHIPS, half_rows, n_cols), BF16),
                   jax.ShapeDtypeStruct((N_CHIPS, half_rows, n_cols), BF16)],
        compiler_params=_params(("arbitrary",)),
    )(c_arr, dw4, dw)
    return out


def _sum_chips(c_arr, parts, name):
    _, rows, n_cols = parts.shape
    nt = 2
    tr = rows // nt

    def body(c_ref, p_ref, o_ref):
        o_ref[...] = ((p_ref[0].astype(F32) + p_ref[1].astype(F32)) + p_ref[2].astype(F32)) + p_ref[3].astype(F32)

    return pl.pallas_call(
        body, name=name,
        grid_spec=pltpu.PrefetchScalarGridSpec(
            num_scalar_prefetch=1, grid=(nt,),
            in_specs=[pl.BlockSpec((N_CHIPS, tr, n_cols), lambda i, c: (0, i, 0))],
            out_specs=pl.BlockSpec((tr, n_cols), lambda i, c: (c[0] * nt + i, 0))),
        out_shape=jax.ShapeDtypeStruct((2 * rows, n_cols), F32),
        compiler_params=_params(("parallel",)),
    )(c_arr, parts)


def _adamw_math(w, g, m, v):
    mn = ADAM_B1 * m + (1.0 - ADAM_B1) * g
    vn = ADAM_B2 * v + (1.0 - ADAM_B2) * (g * g)
    m_hat = mn / (1.0 - ADAM_B1 ** ADAM_STEP)
    v_hat = vn / (1.0 - ADAM_B2 ** ADAM_STEP)
    return -ADAM_LR * (m_hat / (jnp.sqrt(v_hat) + ADAM_EPS) + ADAM_WD * w), mn, vn


def _adamw(w, g, m, v, name):
    R, C = w.shape
    tr = next((t for t in (256, 192, 128) if R % t == 0), R)

    def body(w_ref, g_ref, m_ref, v_ref, go_ref, d_ref, mo_ref, vo_ref):
        gv = g_ref[...]
        go_ref[...] = gv
        d_ref[...], mo_ref[...], vo_ref[...] = _adamw_math(w_ref[...], gv, m_ref[...], v_ref[...])

    spec = pl.BlockSpec((tr, C), lambda i: (i, 0))
    shp = jax.ShapeDtypeStruct((R, C), F32)
    return pl.pallas_call(
        body, name=name, grid=(R // tr,), in_specs=[spec] * 4, out_specs=[spec] * 4, out_shape=[shp] * 4,
        compiler_params=_params(("parallel",)),
    )(w, g, m, v)


HBM_SPEC = pl.BlockSpec(memory_space=pltpu.HBM)
GATHER_LOCAL_CHUNKS = 4
GATHER_SEMS = [pltpu.SemaphoreType.DMA((7,)), pltpu.SemaphoreType.DMA((7,)),
               pltpu.SemaphoreType.DMA((GATHER_LOCAL_CHUNKS,))]
OWNER_SEMS = [pltpu.SemaphoreType.DMA((3,)), pltpu.SemaphoreType.DMA((3,)), pltpu.SemaphoreType.DMA(())]


def _mesh_pos():
    return lax.axis_index("x"), lax.axis_index("y"), lax.axis_index("c")


class _RowGather:
    def __init__(self, src_ref, full_ref, rows, send_sems, recv_sems, local_sems):
        self.src, self.full, self.rows = src_ref, full_ref, rows
        self.send, self.recv, self.local = send_sems, recv_sems, local_sems
        x, y, c = _mesh_pos()
        self.c, self.me, self.sibling = c, (x, y, c), (x, y, 1 - c)
        self.chips = [(1 - x, y), (x, 1 - y), (1 - x, 1 - y)]

    def _block(self, pos):
        px, py, pc = pos
        return self.full.at[pl.ds(pl.multiple_of((4 * px + 2 * py + pc) * self.rows, 16), self.rows), :]

    def _copy(self, k, blk, to):
        return pltpu.make_async_remote_copy(
            src_ref=self.src if blk is self.me else self._block(blk), dst_ref=self._block(blk),
            send_sem=self.send.at[k], recv_sem=self.recv.at[k], device_id=to, device_id_type=MESH)

    def _mine(self):
        return _place_locally(self.src, self._block(self.me), self.local, GATHER_LOCAL_CHUNKS)

    def _first(self):
        return [self._copy(0, self.me, self.sibling)] + [
            self._copy(1 + j, self.me, (*chip, self.c)) for j, chip in enumerate(self.chips)]

    def start(self):
        for cp in self._first() + self._mine():
            cp.start()

    def _passed(self):
        return [self._copy(4 + j, (*chip, self.c), self.sibling) for j, chip in enumerate(self.chips)]

    def pass_on(self):
        for j, chip in enumerate(self.chips):
            self._copy(1 + j, (*chip, self.c), self.me).wait_recv()
            self._passed()[j].start()

    def finish(self):
        self._copy(0, self.sibling, self.me).wait_recv()
        for j, chip in enumerate(self.chips):
            self._copy(4 + j, (*chip, 1 - self.c), self.me).wait_recv()
        for cp in self._first() + self._passed():
            cp.wait_send()
        for cp in self._mine():
            cp.wait()


class _OwnerExchange:
    def __init__(self, src_ref, dst_ref, send_sems, recv_sems, local_sem):
        self.src, self.dst, self.send, self.recv, self.local = src_ref, dst_ref, send_sems, recv_sems, local_sem
        x, y, c = _mesh_pos()
        self.c, self.my_chip = c, 2 * x + y
        self.peers = [(1 - x, y), (x, 1 - y), (1 - x, 1 - y)]

    def _copies(self):
        local = pltpu.make_async_copy(self.src.at[self.my_chip], self.dst.at[self.my_chip], self.local)
        remote = [pltpu.make_async_remote_copy(
            src_ref=self.src.at[2 * px + py], dst_ref=self.dst.at[self.my_chip],
            send_sem=self.send.at[k], recv_sem=self.recv.at[k], device_id=(px, py, self.c), device_id_type=MESH)
            for k, (px, py) in enumerate(self.peers)]
        return local, remote

    def start(self):
        local, remote = self._copies()
        local.start()
        for cp in remote:
            cp.start()

    def finish(self):
        local, remote = self._copies()
        for cp in remote:
            cp.wait_recv()
        for cp in remote:
            cp.wait_send()
        local.wait()


def _sibling_copies(dw_ref, r_ref, rows, send_sems, recv_sems):
    x, y, c = _mesh_pos()
    return [pltpu.make_async_remote_copy(
        src_ref=dw_ref.at[pl.ds(pl.multiple_of((2 * j + (1 - c)) * rows, 16), rows), :], dst_ref=r_ref.at[j],
        send_sem=send_sems.at[j], recv_sem=recv_sems.at[j], device_id=(x, y, 1 - c), device_id_type=MESH)
        for j in range(N_CHIPS)]


PAIR_CHUNKS = 4


def _pair_halves(gt, go, vec_grads, loss_part):
    n_vec = len(VEC)

    def body(*refs):
        g_in, loss_in = refs[2:2 + n_vec], refs[2 + n_vec]
        gt_ref, go_ref, tot_v, vbuf, send_sems, recv_sems, vsend_sems, vrecv_sems = refs[3 + n_vec:]
        x, y, c = _mesh_pos()
        gather = _SlotGather(vbuf, vsend_sems, vrecv_sems)
        vbuf[gather.my_id] = jnp.zeros((VEC_ROWS, D), F32)
        for r, (_, n) in enumerate(VEC):
            vbuf[gather.my_id, r:r + 1, 0:n] = g_in[r][...]
        vbuf[gather.my_id, LOSS_ROW:LOSS_ROW + 1, 0:128] = loss_in[...]
        gather.start()
        copies = []
        for a, (ref, rows) in enumerate(((gt_ref, W_IN_ROWS), (go_ref, W_OUT_ROWS))):
            ch = rows // PAIR_CHUNKS
            for q in range(PAIR_CHUNKS):
                part = ref.at[pl.ds(pl.multiple_of(c * rows + q * ch, 8), ch), :]
                copies.append(pltpu.make_async_remote_copy(
                    src_ref=part, dst_ref=part, send_sem=send_sems.at[PAIR_CHUNKS * a + q],
                    recv_sem=recv_sems.at[PAIR_CHUNKS * a + q], device_id=(x, y, 1 - c), device_id_type=MESH))
        for cp in copies:
            cp.start()
        gather.pass_on()
        gather.finish()
        tot_v[...] = gather.total()
        for cp in copies:
            cp.wait_recv()
        for cp in copies:
            cp.wait_send()

    vmem = pl.BlockSpec(memory_space=pltpu.VMEM)
    return pl.pallas_call(
        body, name="pair_halves",
        in_specs=[HBM_SPEC, HBM_SPEC] + [vmem] * (n_vec + 1), out_specs=[HBM_SPEC, HBM_SPEC, vmem],
        out_shape=[jax.ShapeDtypeStruct(gt.shape, F32), jax.ShapeDtypeStruct(go.shape, F32),
                   jax.ShapeDtypeStruct((VEC_ROWS, D), F32)],
        input_output_aliases={0: 0, 1: 1},
        scratch_shapes=[pltpu.VMEM((N_DEV, VEC_ROWS, D), F32),
                        pltpu.SemaphoreType.DMA((2 * PAIR_CHUNKS,)), pltpu.SemaphoreType.DMA((2 * PAIR_CHUNKS,)),
                        pltpu.SemaphoreType.DMA((7,)), pltpu.SemaphoreType.DMA((7,))],
    )(gt, go, *vec_grads, loss_part)


VEC = (("g_pre", 2048), ("g_post", 2048), ("b_qkv", 1280), ("ln_v_g", 1024), ("ln_v_b", 1024), ("attn_sinks", 16))
VEC_ROWS = 8
LOSS_ROW = len(VEC)
MAT = (("w_spatial", NG * T), ("b_spatial", NG))
MAT_ROWS = sum(r for _, r in MAT)


class _SlotGather:
    def __init__(self, buf, send_sems, recv_sems):
        self.buf, self.send, self.recv = buf, send_sems, recv_sems
        x, y, c = _mesh_pos()
        self.c, self.me, self.sibling, self.my_id = c, (x, y, c), (x, y, 1 - c), 4 * x + 2 * y + c
        self.chips = [(1 - x, y), (x, 1 - y), (1 - x, 1 - y)]

    def _copy(self, k, blk, to):
        px, py, pc = blk
        slot = self.buf.at[4 * px + 2 * py + pc]
        return pltpu.make_async_remote_copy(
            src_ref=slot, dst_ref=slot, send_sem=self.send.at[k], recv_sem=self.recv.at[k],
            device_id=to, device_id_type=MESH)

    def _first(self):
        return [self._copy(0, self.me, self.sibling)] + [
            self._copy(1 + j, self.me, (*chip, self.c)) for j, chip in enumerate(self.chips)]

    def start(self):
        for cp in self._first():
            cp.start()

    def _passed(self):
        return [self._copy(4 + j, (*chip, self.c), self.sibling) for j, chip in enumerate(self.chips)]

    def pass_on(self):
        for j, chip in enumerate(self.chips):
            self._copy(1 + j, (*chip, self.c), self.me).wait_recv()
            self._passed()[j].start()

    def finish(self):
        self._copy(0, self.sibling, self.me).wait_recv()
        for j, chip in enumerate(self.chips):
            self._copy(4 + j, (*chip, 1 - self.c), self.me).wait_recv()
        for cp in self._first() + self._passed():
            cp.wait_send()

    def total(self):
        t = self.buf[0]
        for d in range(1, N_DEV):
            t = t + self.buf[d]
        return t


def _small_update(vec_total, mat_total, vec_state, mat_state):
    n_vec, n_mat = len(VEC), len(MAT)
    n_par = n_vec + n_mat
    n_in = 2 + 3 * n_par

    def body(*refs):
        tot_v, tot_m = refs[0], refs[1]
        st_in = refs[2:n_in]
        outs, loss_out = refs[n_in:n_in + 4 * n_par], refs[n_in + 4 * n_par]
        loss_out[...] = tot_v[LOSS_ROW:LOSS_ROW + 1, 0:128]
        r0 = 0
        for q in range(n_par):
            if q < n_vec:
                g = tot_v[q:q + 1, 0:VEC[q][1]]
            else:
                rows = MAT[q - n_vec][1]
                g = tot_m[r0:r0 + rows, :]
                r0 += rows
            w, m, v = (st_in[3 * q + t][...] for t in range(3))
            outs[4 * q][...] = g
            outs[4 * q + 1][...], outs[4 * q + 2][...], outs[4 * q + 3][...] = _adamw_math(w, g, m, v)

    state = [a for wmv in list(vec_state) + list(mat_state) for a in wmv]
    vmem = pl.BlockSpec(memory_space=pltpu.VMEM)
    out_shape = [jax.ShapeDtypeStruct(wmv[0].shape, F32) for wmv in list(vec_state) + list(mat_state) for _ in range(4)]
    out_shape.append(jax.ShapeDtypeStruct((1, 128), F32))
    res = pl.pallas_call(
        body, name="small_update",
        in_specs=[vmem] * n_in, out_specs=[vmem] * len(out_shape), out_shape=out_shape,
        compiler_params=pltpu.CompilerParams(vmem_limit_bytes=VMEM_LIMIT),
    )(vec_total, mat_total, *state)
    return [res[4 * q:4 * q + 4] for q in range(n_par)], res[-1]


def kernel(x, positions, g_pre, w_in, b_qkv, ln_v_g, ln_v_b, w_spatial, b_spatial, attn_sinks, w_out, g_post, loss_target, m_g_pre, m_w_in, m_b_qkv, m_ln_v_g, m_ln_v_b, m_w_spatial, m_b_spatial, m_attn_sinks, m_w_out, m_g_post, v_g_pre, v_w_in, v_b_qkv, v_ln_v_g, v_ln_v_b, v_w_spatial, v_b_spatial, v_attn_sinks, v_w_out, v_g_post):
    S = x.shape[1]
    c = lax.axis_index("c")
    c_arr = jnp.reshape(c, (1,)).astype(jnp.int32)
    x2 = x[0]
    target = loss_target[0]
    pos = positions.reshape(S, 1)
    half = HD // 2
    inv_freq = ROPE_THETA ** (-jnp.arange(half, dtype=F32) * (2.0 / HD))
    invf = jnp.tile(inv_freq, 128 // half).reshape(1, 128)
    bias = jnp.concatenate([jnp.zeros((OFF_Q,), F32), b_qkv[0], jnp.zeros((D_IN - OFF_ZB,), F32)]).reshape(1, D_IN)
    b_s_col = b_spatial[0].reshape(NG, T, 1)
    sinks = jnp.repeat(attn_sinks[0], T).reshape(NQ * T, 1)

    chip = 2 * lax.axis_index("x") + lax.axis_index("y")
    wt_part = lax.dynamic_slice_in_dim(w_in[0].T.astype(BF16), c * W_IN_ROWS, W_IN_ROWS, axis=0)
    wo_part = lax.dynamic_slice_in_dim(w_out[0].astype(BF16), c * W_OUT_ROWS, W_OUT_ROWS, axis=0)
    sched = jnp.asarray(PROJ_SCHEDULE, jnp.int32)[chip]

    h, proj, wt, cos, sin = _prenorm_inproj(sched, x2, g_pre, bias, wt_part, pos, invf)
    ycat, wo, q_roped = _mid_fwd(proj, cos, sin, ln_v_g, ln_v_b, w_spatial[0], b_s_col, sinks, wo_part)
    dy, dout, loss_part, dg_post = _outproj_loss(ycat, wo, x2, target, g_post)

    dwo = _matmul_tn(ycat, dy, 1024, "dw_out")
    dycat, ro = _dycat(dy, wo, dwo)
    po = _presum(c_arr, dwo, ro, W_OUT_ROWS)
    dproj, dln_g, dln_b, dws, dbs, dsink, dbqkv, bo = _mid_bwd(
        proj, dycat, q_roped, cos, sin, ln_v_g, ln_v_b, w_spatial[0], b_s_col, sinks, po)
    dwt, mat_total, rt = _matmul_tn(dproj, h, 768, "dw_in_t",
                                    mat_grads=[dws.reshape(NG * T, T), dbs.reshape(NG, T)], half_rows=W_IN_ROWS)
    pt = _presum(c_arr, dwt, rt, W_IN_ROWS)
    grad_x, dg_pre, bt = _dh_prenorm_bwd(dproj, wt, x2, dout, g_pre, pt)
    grads = {"g_pre": dg_pre, "g_post": dg_post, "b_qkv": dbqkv, "ln_v_g": dln_g, "ln_v_b": dln_b,
             "attn_sinks": dsink[:, 0].reshape(1, NQ)}
    gt, go, vec_total = _pair_halves(_sum_chips(c_arr, bt, "sum_chips_in"), _sum_chips(c_arr, bo, "sum_chips_out"),
                                     [grads[n] for n, _ in VEC], loss_part)

    g_w_in, d_w_in, nm_w_in, nv_w_in = (a.T for a in _adamw(w_in[0].T, gt, m_w_in[0].T, v_w_in[0].T, "adamw_w_in"))
    g_w_out, d_w_out, nm_w_out, nv_w_out = _adamw(w_out[0], go, m_w_out[0], v_w_out[0], "adamw_w_out")

    state = {"g_pre": (g_pre, m_g_pre, v_g_pre), "g_post": (g_post, m_g_post, v_g_post),
             "b_qkv": (b_qkv, m_b_qkv, v_b_qkv), "ln_v_g": (ln_v_g, m_ln_v_g, v_ln_v_g),
             "ln_v_b": (ln_v_b, m_ln_v_b, v_ln_v_b), "attn_sinks": (attn_sinks, m_attn_sinks, v_attn_sinks),
             "w_spatial": tuple(a.reshape(NG * T, T) for a in (w_spatial, m_w_spatial, v_w_spatial)),
             "b_spatial": tuple(a.reshape(NG, T) for a in (b_spatial, m_b_spatial, v_b_spatial))}
    results, loss = _small_update(vec_total, mat_total, [state[n] for n, _ in VEC], [state[n] for n, _ in MAT])
    small = {n: [a.reshape(w.shape) for a in res]
             for (n, _), res, w in zip(VEC + MAT, results, [state[n][0] for n, _ in VEC + MAT])}
    small["w_spatial"] = [a.reshape(w_spatial.shape) for a in small["w_spatial"]]
    small["b_spatial"] = [a.reshape(b_spatial.shape) for a in small["b_spatial"]]
    big = {"w_in": [a[None] for a in (g_w_in, d_w_in, nm_w_in, nv_w_in)],
           "w_out": [a[None] for a in (g_w_out, d_w_out, nm_w_out, nv_w_out)]}
    order = ("g_pre", "w_in", "b_qkv", "ln_v_g", "ln_v_b", "w_spatial", "b_spatial", "attn_sinks", "w_out", "g_post")
    leaves = {**small, **big}
    return (loss[0, 0], grad_x[None], *[leaves[n][t] for t in range(4) for n in order])
```
